```python
import jax, jax.numpy as jnp
from jax import lax
import numpy as np

D_MODEL = 1024
BATCH = 8
SEQ = 4096
DEPTH = 2

GRID_W = 64
CTX_LEN = 256
EPS = 1e-6
ATTN_QBLOCK = 128
ROPE_BASE = 10000.0

MLA_HEADS = 8
MLA_Q_RANK = 384
MLA_KV_RANK = 256
MLA_NOPE = 64
MLA_ROPE = 32
MLA_V = 64
ROPE_PAIRS = MLA_ROPE // 4
MLSTM_HEADS = 4
MLSTM_DH = 128
MLSTM_CONV = 3
MLSTM_CHUNK = 64
GLA_HEADS = 4
GLA_DK = 64
GLA_DV = 128
GLA_GATE_RANK = 16
GLA_TAU = 16.0
GLA_CHUNK = 64
NA_HEADS = 8
NA_DH = 64
NA_KH = 8
NA_KW = 16

BRANCH_A = MLA_HEADS * MLA_V
BRANCH_B = MLSTM_HEADS * MLSTM_DH
BRANCH_C = GLA_HEADS * GLA_DV
BRANCH_D = NA_HEADS * NA_DH
EVEN_SPLITS = (MLA_Q_RANK, MLA_KV_RANK, MLA_ROPE, BRANCH_B, BRANCH_B, BRANCH_B, BRANCH_B, 4 * MLSTM_HEADS, BRANCH_A + BRANCH_B)
ODD_SPLITS = (GLA_HEADS * GLA_DK, GLA_HEADS * GLA_DK, BRANCH_C, 2 * GLA_GATE_RANK, BRANCH_D, BRANCH_D, BRANCH_D, BRANCH_C + BRANCH_D)
EVEN_IN = sum(EVEN_SPLITS)
ODD_IN = sum(ODD_SPLITS)

kernel_name = 'hybrid_mla_mlstm_gla_natten_dit'


def split_cols(a, sizes):
    return jnp.split(a, np.cumsum(sizes)[:-1].tolist(), axis=-1)


def rmsnorm(x, g):
    xf = x.astype(jnp.float32)
    xf = xf * lax.rsqrt(jnp.mean(xf * xf, axis=-1, keepdims=True) + EPS)
    return xf.astype(x.dtype) * g


def head_rmsnorm(x, g, n_heads):
    B, T, W = x.shape
    return rmsnorm(x.reshape(B, T, n_heads, W // n_heads), g.reshape(n_heads, -1)).reshape(B, T, W)


def to_heads(a, n_heads):
    B, T, W = a.shape
    return a.reshape(B, T, n_heads, W // n_heads).transpose(0, 2, 1, 3)


def merge_heads(a):
    B, H, T, d = a.shape
    return a.transpose(0, 2, 1, 3).reshape(B, T, H * d)


def centred_dwconv(x, w, b):
    K = w.shape[0]
    T = x.shape[1]
    pad = K // 2
    xp = jnp.pad(x, ((0, 0), (pad, K - 1 - pad), (0, 0)))
    y = xp[:, 0:T] * w[0]
    for j in range(1, K):
        y = y + xp[:, j:j + T] * w[j]
    return y + b


def rope2d_tables(S):
    t = jnp.arange(S)
    inv = 1.0 / (ROPE_BASE ** (jnp.arange(ROPE_PAIRS, dtype=jnp.float32) / ROPE_PAIRS))
    ang = jnp.concatenate([(t // GRID_W)[:, None] * inv, (t % GRID_W)[:, None] * inv], axis=-1)
    return jnp.cos(ang), jnp.sin(ang)


def apply_rope2d(x, cos, sin):
    cos = cos.astype(x.dtype)
    sin = sin.astype(x.dtype)
    parts = []
    for a in range(2):
        y = x[..., a * 2 * ROPE_PAIRS:(a + 1) * 2 * ROPE_PAIRS]
        y1, y2 = y[..., :ROPE_PAIRS], y[..., ROPE_PAIRS:]
        ca = cos[:, a * ROPE_PAIRS:(a + 1) * ROPE_PAIRS]
        sa = sin[:, a * ROPE_PAIRS:(a + 1) * ROPE_PAIRS]
        parts += [y1 * ca - y2 * sa, y2 * ca + y1 * sa]
    return jnp.concatenate(parts, axis=-1)


def to_chunks(a, L):
    B, H, T = a.shape[:3]
    return jnp.moveaxis(a.reshape(B, H, T // L, L, *a.shape[3:]), 2, 0)


def from_chunks(a):
    nc, B, H, L = a.shape[:4]
    return jnp.moveaxis(a, 0, 2).reshape(B, H, nc * L, *a.shape[4:])


def block_attention(q, k, v, scale):
    B, H, S, dq = q.shape
    nb = S // ATTN_QBLOCK
    qb = jnp.moveaxis(q.reshape(B, H, nb, ATTN_QBLOCK, dq), 2, 0)

    def one_block(qi):
        s = jnp.einsum('bhqd,bhkd->bhqk', qi, k).astype(jnp.float32) * scale
        p = jax.nn.softmax(s, axis=-1).astype(v.dtype)
        return jnp.einsum('bhqk,bhkd->bhqd', p, v)

    o = lax.map(one_block, qb)
    return jnp.moveaxis(o, 0, 2).reshape(B, H, S, v.shape[-1])


def mlstm_chunked(q, k, v, ig, lf, state):
    L = MLSTM_CHUNK
    causal = jnp.tril(jnp.ones((L, L), dtype=bool))

    def step(carry, xs):
        C, n, m = carry
        qc, kc, vc, ic, fc = xs
        b = jnp.cumsum(fc, axis=-1)
        logw = jnp.where(causal, b[..., :, None] - b[..., None, :] + ic[..., None, :], -jnp.inf)
        inter = b + m[..., None]
        m_t = jnp.maximum(inter, jnp.max(logw, axis=-1))
        w_intra = jnp.exp(logw - m_t[..., None])
        w_inter = jnp.exp(inter - m_t)
        qk = jnp.einsum('bhtd,bhsd->bhts', qc, kc) * w_intra
        num = jnp.einsum('bhts,bhsv->bhtv', qk, vc) + w_inter[..., None] * jnp.einsum('bhtd,bhdv->bhtv', qc, C)
        den = jnp.sum(qk, axis=-1) + w_inter * jnp.einsum('bhtd,bhd->bht', qc, n)
        h = num / jnp.maximum(jnp.abs(den), jnp.exp(-m_t))[..., None]
        b_last = b[..., -1]
        logk = b_last[..., None] - b + ic
        m_new = jnp.maximum(b_last + m, jnp.max(logk, axis=-1))
        carry_decay = jnp.exp(b_last + m - m_new)
        kw = jnp.exp(logk - m_new[..., None])[..., None] * kc
        C = carry_decay[..., None, None] * C + jnp.einsum('bhsd,bhsv->bhdv', kw, vc)
        n = carry_decay[..., None] * n + jnp.sum(kw, axis=2)
        return (C, n, m_new), h

    xs = tuple(to_chunks(a.astype(jnp.float32), L) for a in (q, k, v, ig, lf))
    state, h = lax.scan(step, state, xs)
    return from_chunks(h), state


def gla_chunked(q, k, v, lg, S0):
    L = GLA_CHUNK
    causal = jnp.tril(jnp.ones((L, L), dtype=bool))

    def step(S, xs):
        qc, kc, vc, gc = xs
        bcum = jnp.cumsum(gc, axis=2)
        rel = bcum[:, :, :, None, :] - bcum[:, :, None, :, :]
        decay = jnp.exp(jnp.where(causal[:, :, None], rel, -jnp.inf))
        att = jnp.einsum('bhtd,bhsd,bhtsd->bhts', qc, kc, decay)
        o = jnp.einsum('bhts,bhsv->bhtv', att, vc) + jnp.einsum('bhtd,bhdv->bhtv', qc * jnp.exp(bcum), S)
        b_last = bcum[:, :, -1:]
        S_new = jnp.exp(b_last[:, :, 0])[..., None] * S + jnp.einsum('bhsd,bhsv->bhdv', kc * jnp.exp(b_last - bcum), vc)
        return S_new, o

    xs = tuple(to_chunks(a.astype(jnp.float32), L) for a in (q, k, v, lg))
    S, o = lax.scan(step, S0, xs)
    return from_chunks(o), S


def bidirectional_scan(chunk_fn, init_state, shared_c, gates_c, shared_l, gates_l):
    outs_c, outs_l = [], []
    for d in range(2):
        if d == 0:
            f = lambda a: a
        else:
            f = lambda a: jnp.flip(a, axis=2)
        hc, st = chunk_fn(*[f(a) for a in shared_c], *[f(g[d]) for g in gates_c], init_state)
        hl, _ = chunk_fn(*[f(a) for a in shared_l], *[f(g[d]) for g in gates_l], st)
        outs_c.append(f(hc))
        outs_l.append(f(hl))
    return outs_c[0] + outs_c[1], outs_l[0] + outs_l[1]


def neighbourhood_attention(q, k, v, k_ctx, v_ctx, rpb, rows):
    B, H, S, d = q.shape
    kh = min(NA_KH, rows)
    qg = q.reshape(B, H, rows, GRID_W, d)
    kg = k.reshape(B, H, rows, GRID_W, d)
    vg = v.reshape(B, H, rows, GRID_W, d)
    r_idx = jnp.arange(rows)
    c_idx = jnp.arange(GRID_W)
    win_rows = jnp.clip(r_idx - kh // 2, 0, rows - kh)[:, None] + jnp.arange(kh)
    win_cols = jnp.clip(c_idx - NA_KW // 2, 0, GRID_W - NA_KW)[:, None] + jnp.arange(NA_KW)
    dcol = win_cols - c_idx[:, None] + (NA_KW - 1)
    scale = d ** -0.5
    n_loc = kh * NA_KW

    def one_row(r):
        rws = win_rows[r]
        drow = rws - r + (NA_KH - 1)
        bias = rpb[:, drow[None, :, None], dcol[:, None, :]].astype(jnp.float32)
        qr = qg[:, :, r]
        kw = kg[:, :, rws][:, :, :, win_cols]
        vw = vg[:, :, rws][:, :, :, win_cols]
        s_loc = jnp.einsum('bhcd,bhicjd->bhcij', qr, kw).astype(jnp.float32) * scale + bias
        s_ctx = jnp.einsum('bhcd,bhnd->bhcn', qr, k_ctx).astype(jnp.float32) * scale
        logits = jnp.concatenate([s_loc.reshape(B, H, GRID_W, n_loc), s_ctx], axis=-1)
        p = jax.nn.softmax(logits, axis=-1).astype(v.dtype)
        p_loc = p[..., :n_loc].reshape(B, H, GRID_W, kh, NA_KW)
        return jnp.einsum('bhcij,bhicjd->bhcd', p_loc, vw) + jnp.einsum('bhcn,bhnd->bhcd', p[..., n_loc:], v_ctx)

    o = lax.map(one_row, r_idx)
    return jnp.moveaxis(o, 0, 2).reshape(B, H, S, d)


def mla_mlstm_mixer(u_lat, u_ctx, cos, sin, w_in, q_norm, w_uq, kv_norm, w_ukv, conv_w, conv_b, b_i, b_f, h_norm, w_out, need_ctx):
    def project(u, rotate):
        B, T, _ = u.shape
        cq, ckv, k_rope, mq, mk, mv, mo, gates, z = split_cols(u @ w_in, EVEN_SPLITS)
        q = to_heads(rmsnorm(cq, q_norm) @ w_uq, MLA_HEADS)
        kv = to_heads(rmsnorm(ckv, kv_norm) @ w_ukv, MLA_HEADS)
        q_nope, q_rope = q[..., :MLA_NOPE], q[..., MLA_NOPE:]
        k_nope, v = kv[..., :MLA_NOPE], kv[..., MLA_NOPE:]
        k_rope = k_rope[:, None]
        if rotate:
            q_rope = apply_rope2d(q_rope, cos, sin)
            k_rope = apply_rope2d(k_rope, cos, sin)
        q = jnp.concatenate([q_nope, q_rope], axis=-1)
        k = jnp.concatenate([k_nope, jnp.broadcast_to(k_rope, (B, MLA_HEADS, T, MLA_ROPE))], axis=-1)
        qk = jax.nn.silu(centred_dwconv(jnp.concatenate([mq, mk], axis=-1), conv_w, conv_b))
        mq = to_heads(qk[..., :BRANCH_B], MLSTM_HEADS) * MLSTM_DH ** -0.5
        mk = to_heads(qk[..., BRANCH_B:], MLSTM_HEADS)
        mv = to_heads(mv, MLSTM_HEADS)
        g = gates.reshape(B, T, 2, 2, MLSTM_HEADS).astype(jnp.float32)
        ig = (g[:, :, :, 0] + b_i).transpose(2, 0, 3, 1)
        lf = jax.nn.log_sigmoid(g[:, :, :, 1] + b_f).transpose(2, 0, 3, 1)
        return (q, k, v), (mq, mk, mv), (ig, lf), mo, z

    (qa_l, ka_l, va_l), seq_l, gates_l, mo_l, z_l = project(u_lat, True)
    (qa_c, ka_c, va_c), seq_c, gates_c, mo_c, z_c = project(u_ctx, False)
    scale = (MLA_NOPE + MLA_ROPE) ** -0.5
    a_lat = block_attention(qa_l, jnp.concatenate([ka_c, ka_l], axis=2), jnp.concatenate([va_c, va_l], axis=2), scale)
    B = u_lat.shape[0]
    init = (jnp.zeros((B, MLSTM_HEADS, MLSTM_DH, MLSTM_DH), jnp.float32),
            jnp.zeros((B, MLSTM_HEADS, MLSTM_DH), jnp.float32),
            jnp.zeros((B, MLSTM_HEADS), jnp.float32))
    h_c, h_l = bidirectional_scan(mlstm_chunked, init, seq_c, gates_c, seq_l, gates_l)

    def combine(a, h, mo, z, dtype):
        hm = head_rmsnorm(jax.nn.sigmoid(mo.astype(jnp.float32)) * merge_heads(h), h_norm, MLSTM_HEADS).astype(dtype)
        return (jnp.concatenate([merge_heads(a), hm], axis=-1) * jax.nn.silu(z)) @ w_out

    y_lat = combine(a_lat, h_l, mo_l, z_l, u_lat.dtype)
    y_ctx = None
    if need_ctx:
        y_ctx = combine(block_attention(qa_c, ka_c, va_c, scale), h_c, mo_c, z_c, u_ctx.dtype)
    return y_lat, y_ctx


def gla_na_mixer(u_lat, u_ctx, rows, w_in, w_gate, b_gate, gla_norm, rpb, w_out, need_ctx):
    def project(u):
        B, T, _ = u.shape
        gq, gk, gv, ga, nq, nk, nv, z = split_cols(u @ w_in, ODD_SPLITS)
        ga = ga.reshape(B, T, 2, GLA_GATE_RANK)
        lg = jax.nn.log_sigmoid(jnp.einsum('btdr,drk->dbtk', ga, w_gate).astype(jnp.float32) + b_gate[:, None, None]) / GLA_TAU
        lg = lg.reshape(2, B, T, GLA_HEADS, GLA_DK).transpose(0, 1, 3, 2, 4)
        seq = (to_heads(gq, GLA_HEADS) * GLA_DK ** -0.5, to_heads(gk, GLA_HEADS), to_heads(gv, GLA_HEADS))
        na = (to_heads(nq, NA_HEADS), to_heads(nk, NA_HEADS), to_heads(nv, NA_HEADS))
        return seq, (lg,), na, z

    seq_l, gates_l, (nq_l, nk_l, nv_l), z_l = project(u_lat)
    seq_c, gates_c, (nq_c, nk_c, nv_c), z_c = project(u_ctx)
    B = u_lat.shape[0]
    init = jnp.zeros((B, GLA_HEADS, GLA_DK, GLA_DV), jnp.float32)
    o_c, o_l = bidirectional_scan(gla_chunked, init, seq_c, gates_c, seq_l, gates_l)
    na_l = neighbourhood_attention(nq_l, nk_l, nv_l, nk_c, nv_c, rpb, rows)

    def combine(o, na, z, dtype):
        g = head_rmsnorm(merge_heads(o), gla_norm, GLA_HEADS).astype(dtype)
        return (jnp.concatenate([g, merge_heads(na)], axis=-1) * jax.nn.silu(z)) @ w_out

    y_lat = combine(o_l, na_l, z_l, u_lat.dtype)
    y_ctx = None
    if need_ctx:
        y_ctx = combine(o_c, block_attention(nq_c, nk_c, nv_c, NA_DH ** -0.5), z_c, u_ctx.dtype)
    return y_lat, y_ctx


def setup_inputs(seed: int = 0) -> dict:
    key = jax.random.key(seed)
    keys = iter(jax.random.split(key, 32))

    def rnd(shape, s):
        return jax.random.normal(next(keys), shape, jnp.float32) * s

    def gain(n):
        return 1.0 + rnd((n,), 0.05)

    D = D_MODEL
    return {
        'x': rnd((BATCH, SEQ, D), 1.0),
        'c': rnd((BATCH, D), 1.0),
        'ctx': rnd((BATCH, CTX_LEN, D), 1.0),
        'c_ctx': rnd((D,), 1.0),
        'l0_norm': gain(D),
        'l0_w_mod': rnd((D, 3 * D), 0.5 * D ** -0.5),
        'l0_b_mod': rnd((3 * D,), 0.02),
        'l0_w_in': rnd((D, EVEN_IN), D ** -0.5),
        'l0_mla_q_norm': gain(MLA_Q_RANK),
        'l0_mla_w_uq': rnd((MLA_Q_RANK, MLA_HEADS * (MLA_NOPE + MLA_ROPE)), MLA_Q_RANK ** -0.5),
        'l0_mla_kv_norm': gain(MLA_KV_RANK),
        'l0_mla_w_ukv': rnd((MLA_KV_RANK, MLA_HEADS * (MLA_NOPE + MLA_V)), MLA_KV_RANK ** -0.5),
        'l0_mlstm_conv_w': rnd((MLSTM_CONV, 2 * BRANCH_B), MLSTM_CONV ** -0.5),
        'l0_mlstm_conv_b': rnd((2 * BRANCH_B,), 0.02),
        'l0_mlstm_b_i': rnd((2, MLSTM_HEADS), 0.1),
        'l0_mlstm_b_f': jnp.linspace(3.0, 6.0, MLSTM_HEADS)[None] + rnd((2, MLSTM_HEADS), 0.1),
        'l0_mlstm_norm': gain(BRANCH_B),
        'l0_w_out': rnd((BRANCH_A + BRANCH_B, D), (BRANCH_A + BRANCH_B) ** -0.5),
        'l1_norm': gain(D),
        'l1_w_mod': rnd((D, 3 * D), 0.5 * D ** -0.5),
        'l1_b_mod': rnd((3 * D,), 0.02),
        'l1_w_in': rnd((D, ODD_IN), D ** -0.5),
        'l1_gla_w_gate': rnd((2, GLA_GATE_RANK, GLA_HEADS * GLA_DK), GLA_GATE_RANK ** -0.5),
        'l1_gla_b_gate': rnd((2, GLA_HEADS * GLA_DK), 0.5),
        'l1_gla_norm': gain(BRANCH_C),
        'l1_na_rpb': rnd((NA_HEADS, 2 * NA_KH - 1, 2 * NA_KW - 1), 0.1),
        'l1_w_out': rnd((BRANCH_C + BRANCH_D, D), (BRANCH_C + BRANCH_D) ** -0.5),
        'final_norm': gain(D),
    }


def reference(x, c, ctx, c_ctx,
              l0_norm, l0_w_mod, l0_b_mod, l0_w_in, l0_mla_q_norm, l0_mla_w_uq, l0_mla_kv_norm, l0_mla_w_ukv,
              l0_mlstm_conv_w, l0_mlstm_conv_b, l0_mlstm_b_i, l0_mlstm_b_f, l0_mlstm_norm, l0_w_out,
              l1_norm, l1_w_mod, l1_b_mod, l1_w_in, l1_gla_w_gate, l1_gla_b_gate, l1_gla_norm, l1_na_rpb, l1_w_out,
              final_norm):
    S = x.shape[1]
    rows = S // GRID_W
    cos, sin = rope2d_tables(S)
    norms = (l0_norm, l1_norm)
    mods = ((l0_w_mod, l0_b_mod), (l1_w_mod, l1_b_mod))
    mixer_params = (
        (l0_w_in, l0_mla_q_norm, l0_mla_w_uq, l0_mla_kv_norm, l0_mla_w_ukv, l0_mlstm_conv_w, l0_mlstm_conv_b,
         l0_mlstm_b_i, l0_mlstm_b_f, l0_mlstm_norm, l0_w_out),
        (l1_w_in, l1_gla_w_gate, l1_gla_b_gate, l1_gla_norm, l1_na_rpb, l1_w_out),
    )
    h_lat, h_ctx = x, ctx
    for layer in range(DEPTH):
        need_ctx = layer < DEPTH - 1
        w_mod, b_mod = mods[layer]
        shift_l, scale_l, gate_l = jnp.split(jax.nn.silu(c) @ w_mod + b_mod, 3, axis=-1)
        shift_c, scale_c, gate_c = jnp.split(jax.nn.silu(c_ctx) @ w_mod + b_mod, 3, axis=-1)
        u_lat = rmsnorm(h_lat, norms[layer]) * (1 + scale_l[:, None]) + shift_l[:, None]
        u_ctx = rmsnorm(h_ctx, norms[layer]) * (1 + scale_c) + shift_c
        if layer % 2 == 0:
            y_lat, y_ctx = mla_mlstm_mixer(u_lat, u_ctx, cos, sin, *mixer_params[layer], need_ctx=need_ctx)
        else:
            y_lat, y_ctx = gla_na_mixer(u_lat, u_ctx, rows, *mixer_params[layer], need_ctx=need_ctx)
        h_lat = h_lat + gate_l[:, None] * y_lat
        if need_ctx:
            h_ctx = h_ctx + gate_c * y_ctx
    return rmsnorm(h_lat, final_norm)
```

```python
import functools

import jax
import jax.numpy as jnp
import numpy as np
from jax import lax
from jax.experimental import pallas as pl
from jax.experimental.pallas import tpu as pltpu

F32 = jnp.float32
BF16 = jnp.bfloat16

D_MODEL = 1024
GRID_W = 64
EPS = 1e-6
ROPE_BASE = 10000.0

MLA_HEADS = 8
MLA_Q_RANK = 384
MLA_KV_RANK = 256
MLA_NOPE = 64
MLA_ROPE = 32
MLA_V = 64
ROPE_PAIRS = MLA_ROPE // 4
MLSTM_HEADS = 4
MLSTM_DH = 128
MLSTM_CONV = 3
GLA_HEADS = 4
GLA_DK = 64
GLA_DV = 128
GLA_GATE_RANK = 16
GLA_TAU = 16.0
NA_HEADS = 8
NA_DH = 64
NA_KH = 8
NA_KW = 16

CHUNK = 64
TM = 256
LANES = 128
SUBLANES = 8
HALO = SUBLANES
NA_QROWS = 4
NA_KROWS = NA_QROWS + NA_KH - 1
GLA_LEAF = 16
VMEM_LIMIT = 56 * 1024 * 1024
NEG_BIG = -1e30


def _dot(a, b):
    return jnp.dot(a, b, preferred_element_type=F32)


def _dot_nt(a, b):
    return lax.dot_general(a, b, (((1,), (1,)), ((), ())), preferred_element_type=F32)


def _dot_tn(a, b):
    return lax.dot_general(a, b, (((0,), (0,)), ((), ())), preferred_element_type=F32)


def _rms(x):
    return x * lax.rsqrt(jnp.mean(x * x, axis=-1, keepdims=True) + EPS)


def _sigmoid(x):
    return 1.0 / (1.0 + jnp.exp(-x))


def _silu(x):
    return x * _sigmoid(x)


def _log_sigmoid(x):
    return jnp.minimum(x, 0.0) - jnp.log(1.0 + jnp.exp(-jnp.abs(x)))


def _params(*sem):
    return pltpu.CompilerParams(dimension_semantics=sem, vmem_limit_bytes=VMEM_LIMIT)


def _full(shape):
    nd = len(shape)
    return pl.BlockSpec(shape, lambda *_: (0,) * nd)


def _mod_kernel(c_ref, w_ref, b_ref, o_ref):
    o_ref[...] = _dot(_silu(c_ref[...]).astype(BF16), w_ref[...]) + b_ref[...]


def _modulation(cc, w_mod, b_mod):
    rows, d = cc.shape
    n = w_mod.shape[1]
    return pl.pallas_call(
        _mod_kernel,
        out_shape=jax.ShapeDtypeStruct((rows, n), F32),
        grid=(n // d,),
        in_specs=[_full((rows, d)), pl.BlockSpec((d, d), lambda j: (0, j)), pl.BlockSpec((1, d), lambda j: (0, j))],
        out_specs=pl.BlockSpec((rows, d), lambda j: (0, j)),
        compiler_params=_params("parallel"),
        name="modulation",
    )(cc, w_mod.astype(BF16), b_mod[None])


def _mod_vectors(c, c_ctx, w_mod, b_mod):
    bsz = c.shape[0]
    rows = -(-(bsz + 1) // SUBLANES) * SUBLANES
    cc = jnp.zeros((rows, D_MODEL), F32).at[:bsz].set(c).at[bsz].set(c_ctx)
    mod = _modulation(cc, w_mod, b_mod)
    out = []
    for part in jnp.split(mod, 3, axis=-1):
        ctx_v = jnp.broadcast_to(part[bsz][None], (bsz, D_MODEL))
        out.append(jnp.stack([ctx_v, part[:bsz]], axis=1)[:, :, None, :])
    return out


def _modnorm(x, g, scale, shift):
    return _rms(x) * g * (1.0 + scale) + shift


def _seg_cumsum(x, reverse):
    width = x.shape[1]
    pos = lax.broadcasted_iota(jnp.int32, x.shape, 1) % CHUNK
    k = 1
    while k < CHUNK:
        if reverse:
            x = x + jnp.where(pos < CHUNK - k, pltpu.roll(x, width - k, axis=1), 0.0)
        else:
            x = x + jnp.where(pos >= k, pltpu.roll(x, k, axis=1), 0.0)
        k *= 2
    return x


def _proj0_kernel(h_ref, hp_ref, hn_ref, sc_ref, sh_ref, g_ref, cos_ref, sin_ref,
                  wcq_ref, wckv_ref, wkr_ref, wkrs_ref, wmqk_ref, wmv_ref, wmo_ref, wgt_ref, wz_ref,
                  qn_ref, wuq_ref, wuqs_ref, kvn_ref, wk_ref, wv_ref, cw_ref, cb_ref, gb_ref,
                  q_out, k_out, v_out, mq_out, mk_out, mv_out, gt_out, mo_out, z_out,
                  pbuf, *, nct, nt):
    t = pl.program_id(1)
    x = jnp.concatenate([hp_ref[...], h_ref[...], hn_ref[...]], axis=0)
    ub_ext = _modnorm(x, g_ref[...], sc_ref[...], sh_ref[...]).astype(BF16)
    ub = ub_ext[HALO:HALO + TM]

    cos = cos_ref[...]
    sin = sin_ref[...]
    cos_all = jnp.concatenate([cos] * MLA_HEADS, axis=1)
    sin_all = jnp.concatenate([sin] * MLA_HEADS, axis=1)

    nq = (_rms(_dot(ub, wcq_ref[...])) * qn_ref[...]).astype(BF16)
    q = _dot(nq, wuq_ref[...]) * cos_all + _dot(nq, wuqs_ref[...]) * sin_all
    q_out[...] = (q * (MLA_NOPE + MLA_ROPE) ** -0.5).astype(BF16)

    nkv = (_rms(_dot(ub, wckv_ref[...])) * kvn_ref[...]).astype(BF16)
    rot = _dot(ub, wkr_ref[...]) * cos + _dot(ub, wkrs_ref[...]) * sin
    k_out[...] = (_dot(nkv, wk_ref[...]) + jnp.concatenate([rot] * MLA_HEADS, axis=1)).astype(BF16)
    v_out[...] = _dot(nkv, wv_ref[...]).astype(BF16)

    pqk = _dot(ub_ext, wmqk_ref[...])
    prev_ok = jnp.logical_and(t != 0, t != nct)
    next_ok = jnp.logical_and(t != nct - 1, t != nt - 1)
    row = lax.broadcasted_iota(jnp.int32, pqk.shape, 0)
    keep = jnp.logical_and(jnp.logical_or(row >= HALO, prev_ok), jnp.logical_or(row < HALO + TM, next_ok))
    pbuf[...] = jnp.where(keep, pqk, 0.0)
    cw = cw_ref[...]
    y = (pbuf[HALO - 1:HALO - 1 + TM, :] * cw[0:1] + pbuf[HALO:HALO + TM, :] * cw[1:2]
         + pbuf[HALO + 1:HALO + 1 + TM, :] * cw[2:3] + cb_ref[...])
    y = _silu(y)
    nb = MLSTM_HEADS * MLSTM_DH
    mq_out[...] = (y[:, :nb] * MLSTM_DH ** -0.5).astype(BF16)
    mk_out[...] = y[:, nb:].astype(BF16)
    mv_out[...] = _dot(ub, wmv_ref[...]).astype(BF16)
    mo_out[...] = _sigmoid(_dot(ub, wmo_ref[...]))
    z_out[...] = _silu(_dot(ub, wz_ref[...]))

    gt = _dot_nt(wgt_ref[...], ub) + gb_ref[...]
    grow = lax.broadcasted_iota(jnp.int32, gt.shape, 0)
    is_f = (grow // MLSTM_HEADS) % 2 == 1
    gt = jnp.where(is_f, _log_sigmoid(gt), gt)
    csum = jnp.concatenate([_seg_cumsum(gt[:SUBLANES], False), _seg_cumsum(gt[SUBLANES:], True)], axis=0)
    gt = jnp.where(is_f, csum, gt)
    for ci in range(TM // CHUNK):
        gt_out[ci] = gt[:, ci * CHUNK:(ci + 1) * CHUNK]


def _rope_tables(n_ctx, seq):
    t = jnp.arange(seq)
    inv = 1.0 / (ROPE_BASE ** (jnp.arange(ROPE_PAIRS, dtype=F32) / ROPE_PAIRS))
    ang = jnp.concatenate([(t // GRID_W)[:, None] * inv, (t % GRID_W)[:, None] * inv], axis=-1)
    cos, sin = jnp.cos(ang), jnp.sin(ang)
    j = np.arange(MLA_ROPE)
    src = (j // (2 * ROPE_PAIRS)) * ROPE_PAIRS + (j % ROPE_PAIRS)
    sign = np.where((j % (2 * ROPE_PAIRS)) < ROPE_PAIRS, -1.0, 1.0).astype(np.float32)
    cos_full = cos[:, src]
    sin_full = sin[:, src] * sign
    cos_t = jnp.ones((n_ctx + seq, LANES), F32).at[n_ctx:, MLA_NOPE:MLA_NOPE + MLA_ROPE].set(cos_full)
    sin_t = jnp.zeros((n_ctx + seq, LANES), F32).at[n_ctx:, MLA_NOPE:MLA_NOPE + MLA_ROPE].set(sin_full)
    return cos_t, sin_t


def _rope_partner():
    j = np.arange(MLA_ROPE)
    return np.where((j % (2 * ROPE_PAIRS)) < ROPE_PAIRS, j + ROPE_PAIRS, j - ROPE_PAIRS)


def _proj0_weights(w_in, q_norm, w_uq, kv_norm, w_ukv, conv_w, conv_b, b_i, b_f):
    o = np.cumsum([0, MLA_Q_RANK, MLA_KV_RANK, MLA_ROPE] + [MLSTM_HEADS * MLSTM_DH] * 4 + [4 * MLSTM_HEADS, D_MODEL])
    part = _rope_partner()
    w_kr = w_in[:, o[2]:o[3]]
    pad_kr = jnp.zeros((D_MODEL, LANES), F32).at[:, MLA_NOPE:MLA_NOPE + MLA_ROPE]
    dq = MLA_NOPE + MLA_ROPE
    wuq = jnp.zeros((MLA_Q_RANK, MLA_HEADS, LANES), F32).at[:, :, :dq].set(w_uq.reshape(MLA_Q_RANK, MLA_HEADS, dq))
    wuq_sw = jnp.zeros((MLA_Q_RANK, MLA_HEADS, LANES), F32).at[:, :, MLA_NOPE:dq].set(
        w_uq.reshape(MLA_Q_RANK, MLA_HEADS, dq)[:, :, MLA_NOPE + part])
    wkv = w_ukv.reshape(MLA_KV_RANK, MLA_HEADS, MLA_NOPE + MLA_V)
    wk = jnp.zeros((MLA_KV_RANK, MLA_HEADS, LANES), F32).at[:, :, :MLA_NOPE].set(wkv[:, :, :MLA_NOPE])
    wv = wkv[:, :, MLA_NOPE:].reshape(MLA_KV_RANK, MLA_HEADS * MLA_V)
    gbias = jnp.stack([b_i, b_f], axis=1).reshape(4 * MLSTM_HEADS, 1)
    bf = lambda a: a.astype(BF16)
    return dict(
        wcq=bf(w_in[:, o[0]:o[1]]), wckv=bf(w_in[:, o[1]:o[2]]),
        wkr=bf(pad_kr.set(w_kr)), wkrs=bf(pad_kr.set(w_kr[:, part])),
        wmqk=bf(w_in[:, o[3]:o[5]]), wmv=bf(w_in[:, o[5]:o[6]]), wmo=bf(w_in[:, o[6]:o[7]]),
        wgt=bf(w_in[:, o[7]:o[8]].T), wz=bf(w_in[:, o[8]:o[9]]),
        qn=q_norm[None], wuq=bf(wuq.reshape(MLA_Q_RANK, -1)), wuqs=bf(wuq_sw.reshape(MLA_Q_RANK, -1)),
        kvn=kv_norm[None], wk=bf(wk.reshape(MLA_KV_RANK, -1)), wv=bf(wv),
        cw=jnp.zeros((SUBLANES, conv_w.shape[1]), F32).at[:MLSTM_CONV].set(conv_w), cb=conv_b[None], gb=gbias)


def _tile_specs(nct, nt):
    tile = lambda w: pl.BlockSpec((None, TM, w), lambda b, t: (b, t, 0))
    per = TM // HALO
    prev = pl.BlockSpec((None, HALO, D_MODEL), lambda b, t: (b, jnp.maximum(t * per - 1, 0), 0))
    nxt = pl.BlockSpec((None, HALO, D_MODEL), lambda b, t: (b, jnp.minimum((t + 1) * per, nt * per - 1), 0))
    mod = pl.BlockSpec((None, None, 1, D_MODEL), lambda b, t: (b, (t >= nct).astype(jnp.int32), 0, 0))
    return tile, prev, nxt, mod


def _proj0(h, scale, shift, norm_g, cos_t, sin_t, w, nct):
    bsz, ttot, _ = h.shape
    nt = ttot // TM
    nb = MLSTM_HEADS * MLSTM_DH
    tile, prev, nxt, mod = _tile_specs(nct, nt)
    tab = pl.BlockSpec((TM, LANES), lambda b, t: (t, 0))
    wnames = ["wcq", "wckv", "wkr", "wkrs", "wmqk", "wmv", "wmo", "wgt", "wz",
              "qn", "wuq", "wuqs", "kvn", "wk", "wv", "cw", "cb", "gb"]
    ws = [w[n] for n in wnames]
    tok = lambda width, dt: jax.ShapeDtypeStruct((bsz, ttot, width), dt)
    out_shape = (tok(MLA_HEADS * LANES, BF16), tok(MLA_HEADS * LANES, BF16), tok(MLA_HEADS * MLA_V, BF16),
                 tok(nb, BF16), tok(nb, BF16), tok(nb, BF16),
                 jax.ShapeDtypeStruct((bsz, ttot // CHUNK, 4 * MLSTM_HEADS, CHUNK), F32),
                 tok(nb, F32), tok(D_MODEL, F32))
    out_specs = (tile(MLA_HEADS * LANES), tile(MLA_HEADS * LANES), tile(MLA_HEADS * MLA_V),
                 tile(nb), tile(nb), tile(nb),
                 pl.BlockSpec((None, TM // CHUNK, 4 * MLSTM_HEADS, CHUNK), lambda b, t: (b, t, 0, 0)),
                 tile(nb), tile(D_MODEL))
    return pl.pallas_call(
        functools.partial(_proj0_kernel, nct=nct, nt=nt),
        out_shape=out_shape,
        grid=(bsz, nt),
        in_specs=[tile(D_MODEL), prev, nxt, mod, mod, _full((1, D_MODEL)), tab, tab] + [_full(a.shape) for a in ws],
        out_specs=out_specs,
        scratch_shapes=[pltpu.VMEM((TM + 2 * HALO, 2 * nb), F32)],
        compiler_params=_params("parallel", "parallel"),
        name="proj0",
    )(h, h, h, scale, shift, norm_g[None], cos_t, sin_t, *ws)


def _softmax_pv(s, v):
    m = jnp.max(s, axis=-1, keepdims=True)
    p = jnp.exp(s - m)
    return _dot(p.astype(BF16), v) / jnp.sum(p, axis=-1, keepdims=True)


def _mla_kernel(q_ref, k_ref, v_ref, o_ref, *, nct):
    t = pl.program_id(2)
    lane = lax.broadcasted_iota(jnp.int32, (TM, LANES), 1)

    def attend(nkeys):
        outs = []
        for j in range(2):
            qj = q_ref[:, j * LANES:(j + 1) * LANES]
            kj = k_ref[0:nkeys, j * LANES:(j + 1) * LANES]
            outs.append(_softmax_pv(_dot_nt(qj, kj), v_ref[0:nkeys, :]))
        o_ref[...] = jnp.where(lane < MLA_V, outs[0], outs[1])

    @pl.when(t < nct)
    def _():
        attend(nct * TM)

    @pl.when(t >= nct)
    def _():
        attend(k_ref.shape[0])


def _mla_attention(q, k, v, nct):
    bsz, ttot, _ = q.shape
    nt = ttot // TM
    return pl.pallas_call(
        functools.partial(_mla_kernel, nct=nct),
        out_shape=jax.ShapeDtypeStruct((bsz, ttot, MLA_HEADS * MLA_V), F32),
        grid=(bsz, MLA_HEADS // 2, nt),
        in_specs=[pl.BlockSpec((None, TM, 2 * LANES), lambda b, p, t: (b, t, p)),
                  pl.BlockSpec((None, ttot, 2 * LANES), lambda b, p, t: (b, 0, p)),
                  pl.BlockSpec((None, ttot, LANES), lambda b, p, t: (b, 0, p))],
        out_specs=pl.BlockSpec((None, TM, LANES), lambda b, p, t: (b, t, p)),
        compiler_params=_params("parallel", "parallel", "parallel"),
        name="mla_attention",
    )(q, k, v)


def _scan_block(j, nct, nt):
    return jnp.where(j < nct, nct - 1 - j, nt - 1 - (j - nct))


def _mlstm_kernel(qf_ref, kf_ref, vf_ref, gf_ref, qr_ref, kr_ref, vr_ref, gr_ref,
                  hf_ref, hr_ref, c_scr, m_scr):
    @pl.when(pl.program_id(1) == 0)
    def _():
        c_scr[...] = jnp.zeros_like(c_scr)
        m_scr[...] = jnp.zeros_like(m_scr)

    L = CHUNK
    nchunk = TM // L
    row = lax.broadcasted_iota(jnp.int32, (L, L), 0)
    col = lax.broadcasted_iota(jnp.int32, (L, L), 1)
    eye = row == col
    ones = jnp.ones((L, MLSTM_DH), BF16)

    def to_col(v_row):
        return jnp.sum(jnp.where(eye, jnp.broadcast_to(v_row, (L, L)), 0.0), axis=1, keepdims=True)

    def chunk_step(ci, carry):
        for d, (q_ref, k_ref, v_ref, g_ref, o_ref) in enumerate(
                ((qf_ref, kf_ref, vf_ref, gf_ref, hf_ref), (qr_ref, kr_ref, vr_ref, gr_ref, hr_ref))):
            cd = ci if d == 0 else nchunk - 1 - ci
            base = pl.multiple_of(cd * L, L)
            g = g_ref[cd]
            causal = (col <= row) if d == 0 else (col >= row)
            for hd in range(MLSTM_HEADS):
                ch = d * MLSTM_HEADS + hd
                r0 = 2 * MLSTM_HEADS * d + hd
                i_row = g[r0:r0 + 1]
                b_row = g[r0 + MLSTM_HEADS:r0 + MLSTM_HEADS + 1]
                b_last = b_row[:, L - 1:L] if d == 0 else b_row[:, 0:1]
                b_col = to_col(b_row)
                i_col = to_col(i_row)
                m = m_scr[ch, 0:1, 0:1]
                logw = jnp.where(causal, b_col - (b_row - i_row), -jnp.inf)
                inter = b_col + m
                m_t = jnp.maximum(inter, jnp.max(logw, axis=1, keepdims=True))
                w_intra = jnp.exp(logw - m_t)
                w_inter = jnp.exp(inter - m_t)
                hs = slice(hd * MLSTM_DH, (hd + 1) * MLSTM_DH)
                q = q_ref[pl.ds(base, L), hs]
                k = k_ref[pl.ds(base, L), hs]
                v_ext = jnp.concatenate([v_ref[pl.ds(base, L), hs], ones], axis=1)
                state = c_scr[ch]
                p = (_dot_nt(q, k) * w_intra).astype(BF16)
                o = _dot(p, v_ext) + w_inter * _dot(q, state.astype(BF16))
                den = jnp.maximum(jnp.abs(o[:, MLSTM_DH:]), jnp.exp(-m_t))
                o_ref[pl.ds(base, L), hs] = o[:, :MLSTM_DH] / den
                logk = b_last - b_col + i_col
                m_new = jnp.maximum(b_last + m, jnp.max(logk, axis=0, keepdims=True))
                kw = (jnp.exp(logk - m_new) * k.astype(F32)).astype(BF16)
                c_scr[ch] = jnp.exp(b_last + m - m_new) * state + _dot_tn(kw, v_ext)
                m_scr[ch] = jnp.broadcast_to(m_new, m_scr.shape[1:])
        return carry

    lax.fori_loop(0, nchunk, chunk_step, 0)


def _mlstm(mq, mk, mv, gt, nct):
    bsz, ttot, nb = mq.shape
    nt = ttot // TM
    per = TM // CHUNK
    fwd = pl.BlockSpec((None, TM, nb), lambda b, j: (b, j, 0))
    rev = pl.BlockSpec((None, TM, nb), lambda b, j: (b, _scan_block(j, nct, nt), 0))
    gshape = (None, per, 4 * MLSTM_HEADS, CHUNK)
    gfwd = pl.BlockSpec(gshape, lambda b, j: (b, j, 0, 0))
    grev = pl.BlockSpec(gshape, lambda b, j: (b, _scan_block(j, nct, nt), 0, 0))
    return pl.pallas_call(
        _mlstm_kernel,
        out_shape=(jax.ShapeDtypeStruct((bsz, ttot, nb), F32),) * 2,
        grid=(bsz, nt),
        in_specs=[fwd, fwd, fwd, gfwd, rev, rev, rev, grev],
        out_specs=(fwd, rev),
        scratch_shapes=[pltpu.VMEM((2 * MLSTM_HEADS, MLSTM_DH, 2 * MLSTM_DH), F32),
                        pltpu.VMEM((2 * MLSTM_HEADS, SUBLANES, LANES), F32)],
        compiler_params=_params("parallel", "arbitrary"),
        name="mlstm_scan",
    )(mq, mk, mv, gt, mq, mk, mv, gt)


def _head_rms(x, n_heads):
    w = x.shape[1] // n_heads
    return jnp.concatenate([_rms(x[:, i * w:(i + 1) * w]) for i in range(n_heads)], axis=1)


def _out0_kernel(h_ref, gate_ref, a_ref, hf_ref, hr_ref, mo_ref, z_ref, hn_ref, w_ref, o_ref):
    hm = _head_rms(mo_ref[...] * (hf_ref[...] + hr_ref[...]), MLSTM_HEADS) * hn_ref[...]
    cat = (jnp.concatenate([a_ref[...], hm], axis=1) * z_ref[...]).astype(BF16)
    o_ref[...] = h_ref[...] + gate_ref[...] * _dot(cat, w_ref[...])


def _out0(h, gate, a, hf, hr, mo, z, h_norm, w_out, nct):
    bsz, ttot, _ = h.shape
    nt = ttot // TM
    tile, _, _, mod = _tile_specs(nct, nt)
    nb = MLSTM_HEADS * MLSTM_DH
    return pl.pallas_call(
        _out0_kernel,
        out_shape=jax.ShapeDtypeStruct(h.shape, F32),
        grid=(bsz, nt),
        in_specs=[tile(D_MODEL), mod, tile(MLA_HEADS * MLA_V), tile(nb), tile(nb), tile(nb), tile(D_MODEL),
                  _full((1, nb)), _full(w_out.shape)],
        out_specs=tile(D_MODEL),
        compiler_params=_params("parallel", "parallel"),
        name="out0",
    )(h, gate, a, hf, hr, mo, z, h_norm[None], w_out.astype(BF16))


def _proj1_kernel(h_ref, sc_ref, sh_ref, g_ref, wq_ref, wk_ref, wv_ref, wga_ref, wnq_ref, wnk_ref, wnv_ref, wz_ref,
                  wgate_ref, bgate_ref,
                  gq_out, gk_out, gv_out, bc_out, nq_out, nk_out, nv_out, z_out):
    ub = _modnorm(h_ref[...], g_ref[...], sc_ref[...], sh_ref[...]).astype(BF16)
    gq_out[...] = (_dot(ub, wq_ref[...]) * GLA_DK ** -0.5).astype(BF16)
    gk_out[...] = _dot(ub, wk_ref[...]).astype(BF16)
    gv_out[...] = _dot(ub, wv_ref[...]).astype(BF16)
    nq_out[...] = (_dot(ub, wnq_ref[...]) * NA_DH ** -0.5).astype(BF16)
    nk_out[...] = _dot(ub, wnk_ref[...]).astype(BF16)
    nv_out[...] = _dot(ub, wnv_ref[...]).astype(BF16)
    z_out[...] = _silu(_dot(ub, wz_ref[...]))

    ga = _dot(ub, wga_ref[...]).astype(BF16)
    lg = _log_sigmoid(_dot(ga, wgate_ref[...]) + bgate_ref[...]) * (1.0 / GLA_TAU)
    hi = lg.astype(BF16)
    r1 = lg - hi.astype(F32)
    mid = r1.astype(BF16)
    lo = (r1 - mid.astype(F32)).astype(BF16)
    row = lax.broadcasted_iota(jnp.int32, (TM, TM), 0)
    col = lax.broadcasted_iota(jnp.int32, (TM, TM), 1)
    same = (row // CHUNK) == (col // CHUNK)
    nk = GLA_HEADS * GLA_DK
    for d in range(2):
        tri = jnp.where(jnp.logical_and(same, (col <= row) if d == 0 else (col >= row)), 1.0, 0.0).astype(BF16)
        cols = slice(d * nk, (d + 1) * nk)
        bc_out[:, cols] = _dot(tri, hi[:, cols]) + _dot(tri, mid[:, cols]) + _dot(tri, lo[:, cols])


def _proj1_weights(w_in, w_gate, b_gate):
    nk = GLA_HEADS * GLA_DK
    nc = GLA_HEADS * GLA_DV
    nd = NA_HEADS * NA_DH
    o = np.cumsum([0, nk, nk, nc, 2 * GLA_GATE_RANK, nd, nd, nd, nc + nd])
    r = GLA_GATE_RANK
    wgate = jnp.zeros((2 * r, 2 * nk), F32).at[:r, :nk].set(w_gate[0]).at[r:, nk:].set(w_gate[1])
    bf = lambda a: a.astype(BF16)
    names = ["wq", "wk", "wv", "wga", "wnq", "wnk", "wnv", "wz"]
    w = {n: bf(w_in[:, o[i]:o[i + 1]]) for i, n in enumerate(names)}
    w["wgate"] = bf(wgate)
    w["bgate"] = b_gate.reshape(1, 2 * nk)
    return w


def _proj1(h, scale, shift, norm_g, w, nct):
    bsz, ttot, _ = h.shape
    nt = ttot // TM
    tile, _, _, mod = _tile_specs(nct, nt)
    nk = GLA_HEADS * GLA_DK
    nc = GLA_HEADS * GLA_DV
    nd = NA_HEADS * NA_DH
    wnames = ["wq", "wk", "wv", "wga", "wnq", "wnk", "wnv", "wz", "wgate", "bgate"]
    ws = [w[n] for n in wnames]
    tok = lambda width, dt: jax.ShapeDtypeStruct((bsz, ttot, width), dt)
    return pl.pallas_call(
        _proj1_kernel,
        out_shape=(tok(nk, BF16), tok(nk, BF16), tok(nc, BF16), tok(2 * nk, F32),
                   tok(nd, BF16), tok(nd, BF16), tok(nd, BF16), tok(nc + nd, F32)),
        grid=(bsz, nt),
        in_specs=[tile(D_MODEL), mod, mod, _full((1, D_MODEL))] + [_full(a.shape) for a in ws],
        out_specs=(tile(nk), tile(nk), tile(nc), tile(2 * nk), tile(nd), tile(nd), tile(nd), tile(nc + nd)),
        compiler_params=_params("parallel", "parallel"),
        name="proj1",
    )(h, scale, shift, norm_g[None], *ws)


def _gla_kernel(qf_ref, kf_ref, vf_ref, bf_ref, qr_ref, kr_ref, vr_ref, br_ref, of_ref, or_ref, s_scr):
    @pl.when(pl.program_id(1) == 0)
    def _():
        s_scr[...] = jnp.zeros_like(s_scr)

    L = CHUNK
    nchunk = TM // L
    npair = GLA_HEADS // 2
    row = lax.broadcasted_iota(jnp.int32, (L, L), 0)
    col = lax.broadcasted_iota(jnp.int32, (L, L), 1)
    lane = lax.broadcasted_iota(jnp.int32, (L, LANES), 1)
    rowv = lax.broadcasted_iota(jnp.int32, (L, LANES), 0)
    head_lane = [lane < GLA_DK, lane >= GLA_DK]
    srow = lax.broadcasted_iota(jnp.int32, (LANES, 2 * GLA_DV), 0)
    scol = lax.broadcasted_iota(jnp.int32, (LANES, 2 * GLA_DV), 1)
    own_block = (srow < GLA_DK) == (scol < GLA_DV)
    e_row = lax.broadcasted_iota(jnp.int32, (LANES, LANES), 0)
    e_col = lax.broadcasted_iota(jnp.int32, (LANES, LANES), 1)
    leaf_row = lax.broadcasted_iota(jnp.int32, (GLA_LEAF, LANES), 0)

    def chunk_step(ci, carry):
        for d, (q_ref, k_ref, v_ref, b_ref, o_ref) in enumerate(
                ((qf_ref, kf_ref, vf_ref, bf_ref, of_ref), (qr_ref, kr_ref, vr_ref, br_ref, or_ref))):
            cd = ci if d == 0 else nchunk - 1 - ci
            base = pl.multiple_of(cd * L, L)
            tau_r = row if d == 0 else L - 1 - row
            tau_c = col if d == 0 else L - 1 - col
            tau_v = rowv if d == 0 else L - 1 - rowv
            for p in range(npair):
                ls = slice(p * LANES, (p + 1) * LANES)
                q = q_ref[pl.ds(base, L), ls].astype(F32)
                k = k_ref[pl.ds(base, L), ls].astype(F32)
                bc = b_ref[pl.ds(base, L), ls]
                v = v_ref[pl.ds(base, L), p * 2 * GLA_DV:(p + 1) * 2 * GLA_DV]
                b_last = bc[L - 1:L] if d == 0 else bc[0:1]
                state = s_scr[d, p]

                o_pair = _dot((q * jnp.exp(bc)).astype(BF16), state.astype(BF16))

                att = [jnp.zeros((L, L), F32), jnp.zeros((L, L), F32)]
                bs = L // 2
                while bs >= GLA_LEAF:
                    later = (tau_v // bs) % 2 == 1
                    nblk = L // (2 * bs)
                    ref_b = None
                    for blk in range(nblk):
                        tau_ref = blk * 2 * bs + bs - 1
                        idx = tau_ref if d == 0 else L - 1 - tau_ref
                        r = jnp.broadcast_to(bc[idx:idx + 1], (L, LANES))
                        ref_b = r if ref_b is None else jnp.where(tau_v // (2 * bs) == blk, r, ref_b)
                    qs = jnp.where(later, q * jnp.exp(jnp.where(later, bc - ref_b, 0.0)), 0.0)
                    ks = jnp.where(later, 0.0, k * jnp.exp(jnp.where(later, 0.0, ref_b - bc))).astype(BF16)
                    pair_mask = (tau_r // (2 * bs)) == (tau_c // (2 * bs))
                    for hh in range(2):
                        a = _dot_nt(jnp.where(head_lane[hh], qs, 0.0).astype(BF16), ks)
                        att[hh] = att[hh] + jnp.where(pair_mask, a, 0.0)
                    bs //= 2
                o_heads = [o_pair[:, hh * GLA_DV:(hh + 1) * GLA_DV]
                           + _dot(att[hh].astype(BF16), v[:, hh * GLA_DV:(hh + 1) * GLA_DV]) for hh in range(2)]

                vf = v.astype(F32)
                leaf_out = [[], []]
                for blk in range(L // GLA_LEAF):
                    rs = slice(blk * GLA_LEAF, (blk + 1) * GLA_LEAF)
                    qb, bb = q[rs], bc[rs]
                    acc = [jnp.zeros((GLA_LEAF, GLA_DV), F32), jnp.zeros((GLA_LEAF, GLA_DV), F32)]
                    for s in range(blk * GLA_LEAF, (blk + 1) * GLA_LEAF):
                        ok = (leaf_row + blk * GLA_LEAF >= s) if d == 0 else (leaf_row + blk * GLA_LEAF <= s)
                        w = jnp.where(ok, qb * jnp.exp(jnp.where(ok, bb - bc[s:s + 1], 0.0)) * k[s:s + 1], 0.0)
                        for hh in range(2):
                            a = jnp.sum(w[:, hh * GLA_DK:(hh + 1) * GLA_DK], axis=1, keepdims=True)
                            acc[hh] = acc[hh] + a * vf[s:s + 1, hh * GLA_DV:(hh + 1) * GLA_DV]
                    for hh in range(2):
                        leaf_out[hh].append(acc[hh])
                for hh in range(2):
                    o_h = o_heads[hh] + jnp.concatenate(leaf_out[hh], axis=0)
                    hcol = (2 * p + hh) * GLA_DV
                    o_ref[pl.ds(base, L), hcol:hcol + GLA_DV] = o_h

                ke = (k * jnp.exp(b_last - bc)).astype(BF16)
                decay_col = jnp.sum(jnp.where(e_row == e_col, jnp.broadcast_to(jnp.exp(b_last), (LANES, LANES)), 0.0),
                                    axis=1, keepdims=True)
                s_scr[d, p] = decay_col * state + jnp.where(own_block, _dot_tn(ke, v), 0.0)
        return carry

    lax.fori_loop(0, nchunk, chunk_step, 0)


def _gla(gq, gk, gv, bc, nct):
    bsz, ttot, nk = gq.shape
    nc = gv.shape[2]
    nt = ttot // TM
    fwd = lambda w: pl.BlockSpec((None, TM, w), lambda b, j: (b, j, 0))
    rev = lambda w: pl.BlockSpec((None, TM, w), lambda b, j: (b, _scan_block(j, nct, nt), 0))
    bfwd = pl.BlockSpec((None, TM, nk), lambda b, j: (b, j, 0))
    brev = pl.BlockSpec((None, TM, nk), lambda b, j: (b, _scan_block(j, nct, nt), 1))
    return pl.pallas_call(
        _gla_kernel,
        out_shape=(jax.ShapeDtypeStruct((bsz, ttot, nc), F32),) * 2,
        grid=(bsz, nt),
        in_specs=[fwd(nk), fwd(nk), fwd(nc), bfwd, rev(nk), rev(nk), rev(nc), brev],
        out_specs=(fwd(nc), rev(nc)),
        scratch_shapes=[pltpu.VMEM((2, GLA_HEADS // 2, LANES, 2 * GLA_DV), F32)],
        compiler_params=_params("parallel", "arbitrary"),
        name="gla_scan",
    )(gq, gk, gv, bc, gq, gk, gv, bc)


def _na_bias_tables(rpb, rows):
    kh = NA_KH
    tabs = []
    for r0 in (0, NA_QROWS, rows - NA_QROWS):
        kb = int(np.clip(r0 - kh // 2, 0, rows - NA_KROWS))
        qr = r0 + np.arange(NA_QROWS)
        ws = np.clip(qr - kh // 2, 0, rows - kh)
        krow = kb + np.arange(NA_KROWS)
        row_ok = (krow[None, :] >= ws[:, None]) & (krow[None, :] < ws[:, None] + kh)
        drow = np.clip(krow[None, :] - qr[:, None] + (NA_KH - 1), 0, 2 * NA_KH - 2)
        c = np.arange(GRID_W)
        cs = np.clip(c - NA_KW // 2, 0, GRID_W - NA_KW)
        col_ok = (c[None, :] >= cs[:, None]) & (c[None, :] < cs[:, None] + NA_KW)
        dcol = np.clip(c[None, :] - c[:, None] + (NA_KW - 1), 0, 2 * NA_KW - 2)
        bias = rpb[:, drow[:, None, :, None], dcol[None, :, None, :]]
        ok = row_ok[:, None, :, None] & col_ok[None, :, None, :]
        tabs.append(jnp.where(ok[None], bias.astype(F32), NEG_BIG).reshape(
            NA_HEADS, NA_QROWS * GRID_W, NA_KROWS * GRID_W))
    return jnp.stack(tabs)


def _na_kernel(q_ref, k_ref, v_ref, bias_ref, o_ref, *, n_ctx, rows):
    rb = pl.program_id(2)
    nq = NA_QROWS * GRID_W
    nk = NA_KROWS * GRID_W
    kb = jnp.clip(rb * NA_QROWS - NA_KH // 2, 0, rows - NA_KROWS)
    start = pl.multiple_of(n_ctx + kb * GRID_W, GRID_W)
    q = q_ref[...]
    k_loc = k_ref[pl.ds(start, nk), :]
    v_loc = v_ref[pl.ds(start, nk), :]
    k_ctx = k_ref[0:n_ctx, :]
    v_ctx = v_ref[0:n_ctx, :]
    lane = lax.broadcasted_iota(jnp.int32, (nq, LANES), 1)
    outs = []
    for j in range(2):
        mine = (lane < NA_DH) if j == 0 else (lane >= NA_DH)
        qj = jnp.where(mine, q, jnp.zeros_like(q))
        s_loc = _dot_nt(qj, k_loc) + bias_ref[j]
        s_ctx = _dot_nt(qj, k_ctx)
        m = jnp.maximum(jnp.max(s_loc, axis=-1, keepdims=True), jnp.max(s_ctx, axis=-1, keepdims=True))
        p_loc = jnp.exp(s_loc - m)
        p_ctx = jnp.exp(s_ctx - m)
        denom = jnp.sum(p_loc, axis=-1, keepdims=True) + jnp.sum(p_ctx, axis=-1, keepdims=True)
        outs.append((_dot(p_loc.astype(BF16), v_loc) + _dot(p_ctx.astype(BF16), v_ctx)) / denom)
    o_ref[...] = jnp.where(lane < NA_DH, outs[0], outs[1])


def _na_attention(nq, nk, nv, bias, n_ctx, rows):
    bsz, ttot, _ = nq.shape
    nrb = rows // NA_QROWS
    qrows = NA_QROWS * GRID_W
    krows = NA_KROWS * GRID_W
    qoff = n_ctx // qrows

    def variant(rb):
        return jnp.where(rb == 0, 0, jnp.where(rb == nrb - 1, 2, 1))

    return pl.pallas_call(
        functools.partial(_na_kernel, n_ctx=n_ctx, rows=rows),
        out_shape=jax.ShapeDtypeStruct((bsz, rows * GRID_W, NA_HEADS * NA_DH), F32),
        grid=(bsz, NA_HEADS // 2, nrb),
        in_specs=[pl.BlockSpec((None, qrows, LANES), lambda b, p, rb: (b, qoff + rb, p)),
                  pl.BlockSpec((None, ttot, LANES), lambda b, p, rb: (b, 0, p)),
                  pl.BlockSpec((None, ttot, LANES), lambda b, p, rb: (b, 0, p)),
                  pl.BlockSpec((None, 2, qrows, krows), lambda b, p, rb: (variant(rb), p, 0, 0))],
        out_specs=pl.BlockSpec((None, qrows, LANES), lambda b, p, rb: (b, rb, p)),
        compiler_params=_params("parallel", "parallel", "parallel"),
        name="neighbourhood_attention",
    )(nq, nk, nv, bias)


def _out1_kernel(h_ref, gate_ref, of_ref, or_ref, na_ref, z_ref, gn_ref, w_ref, fn_ref, o_ref):
    g = _head_rms(of_ref[...] + or_ref[...], GLA_HEADS) * gn_ref[...]
    cat = (jnp.concatenate([g, na_ref[...]], axis=1) * z_ref[...]).astype(BF16)
    h = h_ref[...] + gate_ref[...] * _dot(cat, w_ref[...])
    o_ref[...] = _rms(h) * fn_ref[...]


def _out1(h, gate, of, orv, na, z, gla_norm, w_out, final_norm, nct):
    bsz, ttot, _ = h.shape
    seq = na.shape[1]
    nc = GLA_HEADS * GLA_DV
    nd = NA_HEADS * NA_DH
    lat = lambda w: pl.BlockSpec((None, TM, w), lambda b, t: (b, t + nct, 0))
    return pl.pallas_call(
        _out1_kernel,
        out_shape=jax.ShapeDtypeStruct((bsz, seq, D_MODEL), F32),
        grid=(bsz, seq // TM),
        in_specs=[lat(D_MODEL), pl.BlockSpec((None, None, 1, D_MODEL), lambda b, t: (b, 1, 0, 0)),
                  lat(nc), lat(nc), pl.BlockSpec((None, TM, nd), lambda b, t: (b, t, 0)), lat(nc + nd),
                  _full((1, nc)), _full(w_out.shape), _full((1, D_MODEL))],
        out_specs=pl.BlockSpec((None, TM, D_MODEL), lambda b, t: (b, t, 0)),
        compiler_params=_params("parallel", "parallel"),
        name="out1",
    )(h, gate, of, orv, na, z, gla_norm[None], w_out.astype(BF16), final_norm[None])


def kernel(x, c, ctx, c_ctx, l0_norm, l0_w_mod, l0_b_mod, l0_w_in, l0_mla_q_norm, l0_mla_w_uq, l0_mla_kv_norm, l0_mla_w_ukv, l0_mlstm_conv_w, l0_mlstm_conv_b, l0_mlstm_b_i, l0_mlstm_b_f, l0_mlstm_norm, l0_w_out, l1_norm, l1_w_mod, l1_b_mod, l1_w_in, l1_gla_w_gate, l1_gla_b_gate, l1_gla_norm, l1_na_rpb, l1_w_out, final_norm):
    bsz, seq, d = x.shape
    n_ctx = ctx.shape[1]
    rows = seq // GRID_W
    assert d == D_MODEL and seq % TM == 0 and n_ctx % TM == 0
    assert rows % NA_QROWS == 0 and rows >= NA_KROWS and rows // NA_QROWS >= 3
    nct = n_ctx // TM

    h = jnp.concatenate([ctx, x], axis=1)

    shift, scale, gate = _mod_vectors(c, c_ctx, l0_w_mod, l0_b_mod)
    cos_t, sin_t = _rope_tables(n_ctx, seq)
    w0 = _proj0_weights(l0_w_in, l0_mla_q_norm, l0_mla_w_uq, l0_mla_kv_norm, l0_mla_w_ukv,
                        l0_mlstm_conv_w, l0_mlstm_conv_b, l0_mlstm_b_i, l0_mlstm_b_f)
    q, k, v, mq, mk, mv, gt, mo, z = _proj0(h, scale, shift, l0_norm, cos_t, sin_t, w0, nct)
    a = _mla_attention(q, k, v, nct)
    hf, hr = _mlstm(mq, mk, mv, gt, nct)
    h = _out0(h, gate, a, hf, hr, mo, z, l0_mlstm_norm, l0_w_out, nct)

    shift, scale, gate = _mod_vectors(c, c_ctx, l1_w_mod, l1_b_mod)
    w1 = _proj1_weights(l1_w_in, l1_gla_w_gate, l1_gla_b_gate)
    gq, gk, gv, bc, nq, nk, nv, z = _proj1(h, scale, shift, l1_norm, w1, nct)
    of, orv = _gla(gq, gk, gv, bc, nct)
    na = _na_attention(nq, nk, nv, _na_bias_tables(l1_na_rpb, rows), n_ctx, rows)
    return _out1(h, gate, of, orv, na, z, l1_gla_norm, l1_w_out, final_norm, nct)
```

```python
import functools

import jax
import jax.numpy as jnp
import numpy as np
from jax import lax
from jax.experimental import pallas as pl
from jax.experimental.pallas import tpu as pltpu

F32 = jnp.float32
BF16 = jnp.bfloat16

D_MODEL = 1024
GRID_W = 64
EPS = 1e-6
ROPE_BASE = 10000.0

MLA_HEADS = 8
MLA_Q_RANK = 384
MLA_KV_RANK = 256
MLA_NOPE = 64
MLA_ROPE = 32
MLA_V = 64
ROPE_PAIRS = MLA_ROPE // 4
MLSTM_HEADS = 4
MLSTM_DH = 128
MLSTM_CONV = 3
GLA_HEADS = 4
GLA_DK = 64
GLA_DV = 128
GLA_GATE_RANK = 16
GLA_TAU = 16.0
NA_HEADS = 8
NA_DH = 64
NA_KH = 8
NA_KW = 16

CHUNK = 64
TM = 256
LANES = 128
SUBLANES = 8
HALO = SUBLANES
NA_QROWS = 4
NA_KROWS = NA_QROWS + NA_KH - 1
GLA_LEAF = 16
VMEM_LIMIT = 56 * 1024 * 1024
NEG_BIG = -1e30


def _dot(a, b):
    return jnp.dot(a, b, preferred_element_type=F32)


def _dot_nt(a, b):
    return lax.dot_general(a, b, (((1,), (1,)), ((), ())), preferred_element_type=F32)


def _dot_tn(a, b):
    return lax.dot_general(a, b, (((0,), (0,)), ((), ())), preferred_element_type=F32)


def _rms(x):
    return x * lax.rsqrt(jnp.mean(x * x, axis=-1, keepdims=True) + EPS)


def _sigmoid(x):
    return 1.0 / (1.0 + jnp.exp(-x))


def _silu(x):
    return x * _sigmoid(x)


def _log_sigmoid(x):
    return jnp.minimum(x, 0.0) - jnp.log(1.0 + jnp.exp(-jnp.abs(x)))


def _params(*sem):
    return pltpu.CompilerParams(dimension_semantics=sem, vmem_limit_bytes=VMEM_LIMIT)


def _full(shape):
    nd = len(shape)
    return pl.BlockSpec(shape, lambda *_: (0,) * nd)


def _mod_kernel(c_ref, w_ref, b_ref, o_ref):
    o_ref[...] = _dot(_silu(c_ref[...]).astype(BF16), w_ref[...]) + b_ref[...]


def _modulation(cc, w_mod, b_mod):
    rows, d = cc.shape
    n = w_mod.shape[1]
    return pl.pallas_call(
        _mod_kernel,
        out_shape=jax.ShapeDtypeStruct((rows, n), F32),
        grid=(n // d,),
        in_specs=[_full((rows, d)), pl.BlockSpec((d, d), lambda j: (0, j)), pl.BlockSpec((1, d), lambda j: (0, j))],
        out_specs=pl.BlockSpec((rows, d), lambda j: (0, j)),
        compiler_params=_params("parallel"),
        name="modulation",
    )(cc, w_mod.astype(BF16), b_mod[None])


def _mod_vectors(c, c_ctx, w_mod, b_mod):
    bsz = c.shape[0]
    rows = -(-(bsz + 1) // SUBLANES) * SUBLANES
    cc = jnp.zeros((rows, D_MODEL), F32).at[:bsz].set(c).at[bsz].set(c_ctx)
    mod = _modulation(cc, w_mod, b_mod)
    out = []
    for part in jnp.split(mod, 3, axis=-1):
        ctx_v = jnp.broadcast_to(part[bsz][None], (bsz, D_MODEL))
        out.append(jnp.stack([ctx_v, part[:bsz]], axis=1)[:, :, None, :])
    return out


def _modnorm(x, g, scale, shift):
    return _rms(x) * g * (1.0 + scale) + shift


def _seg_cumsum(x, reverse):
    width = x.shape[1]
    pos = lax.broadcasted_iota(jnp.int32, x.shape, 1) % CHUNK
    k = 1
    while k < CHUNK:
        if reverse:
            x = x + jnp.where(pos < CHUNK - k, pltpu.roll(x, width - k, axis=1), 0.0)
        else:
            x = x + jnp.where(pos >= k, pltpu.roll(x, k, axis=1), 0.0)
        k *= 2
    return x


def _proj0_kernel(h_ref, hp_ref, hn_ref, sc_ref, sh_ref, g_ref, cos_ref, sin_ref,
                  wcq_ref, wckv_ref, wkr_ref, wkrs_ref, wmqk_ref, wmv_ref, wmo_ref, wgt_ref, wz_ref,
                  qn_ref, wuq_ref, wuqs_ref, kvn_ref, wk_ref, wv_ref, cw_ref, cb_ref, gb_ref,
                  q_out, k_out, v_out, mq_out, mk_out, mv_out, gt_out, mo_out, z_out,
                  pbuf, *, nct, nt):
    t = pl.program_id(1)
    x = jnp.concatenate([hp_ref[...], h_ref[...], hn_ref[...]], axis=0)
    ub_ext = _modnorm(x, g_ref[...], sc_ref[...], sh_ref[...]).astype(BF16)
    ub = ub_ext[HALO:HALO + TM]

    cos = cos_ref[...]
    sin = sin_ref[...]
    cos_all = jnp.concatenate([cos] * MLA_HEADS, axis=1)
    sin_all = jnp.concatenate([sin] * MLA_HEADS, axis=1)

    nq = (_rms(_dot(ub, wcq_ref[...])) * qn_ref[...]).astype(BF16)
    q = _dot(nq, wuq_ref[...]) * cos_all + _dot(nq, wuqs_ref[...]) * sin_all
    q_out[...] = (q * (MLA_NOPE + MLA_ROPE) ** -0.5).astype(BF16)

    nkv = (_rms(_dot(ub, wckv_ref[...])) * kvn_ref[...]).astype(BF16)
    rot = _dot(ub, wkr_ref[...]) * cos + _dot(ub, wkrs_ref[...]) * sin
    k_out[...] = (_dot(nkv, wk_ref[...]) + jnp.concatenate([rot] * MLA_HEADS, axis=1)).astype(BF16)
    v_out[...] = _dot(nkv, wv_ref[...]).astype(BF16)

    pqk = _dot(ub_ext, wmqk_ref[...])
    prev_ok = jnp.logical_and(t != 0, t != nct)
    next_ok = jnp.logical_and(t != nct - 1, t != nt - 1)
    row = lax.broadcasted_iota(jnp.int32, pqk.shape, 0)
    keep = jnp.logical_and(jnp.logical_or(row >= HALO, prev_ok), jnp.logical_or(row < HALO + TM, next_ok))
    pbuf[...] = jnp.where(keep, pqk, 0.0)
    cw = cw_ref[...]
    y = (pbuf[HALO - 1:HALO - 1 + TM, :] * cw[0:1] + pbuf[HALO:HALO + TM, :] * cw[1:2]
         + pbuf[HALO + 1:HALO + 1 + TM, :] * cw[2:3] + cb_ref[...])
    y = _silu(y)
    nb = MLSTM_HEADS * MLSTM_DH
    mq_out[...] = (y[:, :nb] * MLSTM_DH ** -0.5).astype(BF16)
    mk_out[...] = y[:, nb:].astype(BF16)
    mv_out[...] = _dot(ub, wmv_ref[...]).astype(BF16)
    mo_out[...] = _sigmoid(_dot(ub, wmo_ref[...]))
    z_out[...] = _silu(_dot(ub, wz_ref[...]))

    gt = _dot_nt(wgt_ref[...], ub) + gb_ref[...]
    grow = lax.broadcasted_iota(jnp.int32, gt.shape, 0)
    is_f = (grow // MLSTM_HEADS) % 2 == 1
    gt = jnp.where(is_f, _log_sigmoid(gt), gt)
    csum = jnp.concatenate([_seg_cumsum(gt[:SUBLANES], False), _seg_cumsum(gt[SUBLANES:], True)], axis=0)
    gt = jnp.where(is_f, csum, gt)
    for ci in range(TM // CHUNK):
        gt_out[ci] = gt[:, ci * CHUNK:(ci + 1) * CHUNK]


def _rope_tables(n_ctx, seq):
    t = jnp.arange(seq)
    inv = 1.0 / (ROPE_BASE ** (jnp.arange(ROPE_PAIRS, dtype=F32) / ROPE_PAIRS))
    ang = jnp.concatenate([(t // GRID_W)[:, None] * inv, (t % GRID_W)[:, None] * inv], axis=-1)
    cos, sin = jnp.cos(ang), jnp.sin(ang)
    j = np.arange(MLA_ROPE)
    src = (j // (2 * ROPE_PAIRS)) * ROPE_PAIRS + (j % ROPE_PAIRS)
    sign = np.where((j % (2 * ROPE_PAIRS)) < ROPE_PAIRS, -1.0, 1.0).astype(np.float32)
    cos_full = cos[:, src]
    sin_full = sin[:, src] * sign
    cos_t = jnp.ones((n_ctx + seq, LANES), F32).at[n_ctx:, MLA_NOPE:MLA_NOPE + MLA_ROPE].set(cos_full)
    sin_t = jnp.zeros((n_ctx + seq, LANES), F32).at[n_ctx:, MLA_NOPE:MLA_NOPE + MLA_ROPE].set(sin_full)
    return cos_t, sin_t


def _rope_partner():
    j = np.arange(MLA_ROPE)
    return np.where((j % (2 * ROPE_PAIRS)) < ROPE_PAIRS, j + ROPE_PAIRS, j - ROPE_PAIRS)


def _proj0_weights(w_in, q_norm, w_uq, kv_norm, w_ukv, conv_w, conv_b, b_i, b_f):
    o = np.cumsum([0, MLA_Q_RANK, MLA_KV_RANK, MLA_ROPE] + [MLSTM_HEADS * MLSTM_DH] * 4 + [4 * MLSTM_HEADS, D_MODEL])
    part = _rope_partner()
    w_kr = w_in[:, o[2]:o[3]]
    pad_kr = jnp.zeros((D_MODEL, LANES), F32).at[:, MLA_NOPE:MLA_NOPE + MLA_ROPE]
    dq = MLA_NOPE + MLA_ROPE
    wuq = jnp.zeros((MLA_Q_RANK, MLA_HEADS, LANES), F32).at[:, :, :dq].set(w_uq.reshape(MLA_Q_RANK, MLA_HEADS, dq))
    wuq_sw = jnp.zeros((MLA_Q_RANK, MLA_HEADS, LANES), F32).at[:, :, MLA_NOPE:dq].set(
        w_uq.reshape(MLA_Q_RANK, MLA_HEADS, dq)[:, :, MLA_NOPE + part])
    wkv = w_ukv.reshape(MLA_KV_RANK, MLA_HEADS, MLA_NOPE + MLA_V)
    wk = jnp.zeros((MLA_KV_RANK, MLA_HEADS, LANES), F32).at[:, :, :MLA_NOPE].set(wkv[:, :, :MLA_NOPE])
    wv = wkv[:, :, MLA_NOPE:].reshape(MLA_KV_RANK, MLA_HEADS * MLA_V)
    gbias = jnp.stack([b_i, b_f], axis=1).reshape(4 * MLSTM_HEADS, 1)
    bf = lambda a: a.astype(BF16)
    return dict(
        wcq=bf(w_in[:, o[0]:o[1]]), wckv=bf(w_in[:, o[1]:o[2]]),
        wkr=bf(pad_kr.set(w_kr)), wkrs=bf(pad_kr.set(w_kr[:, part])),
        wmqk=bf(w_in[:, o[3]:o[5]]), wmv=bf(w_in[:, o[5]:o[6]]), wmo=bf(w_in[:, o[6]:o[7]]),
        wgt=bf(w_in[:, o[7]:o[8]].T), wz=bf(w_in[:, o[8]:o[9]]),
        qn=q_norm[None], wuq=bf(wuq.reshape(MLA_Q_RANK, -1)), wuqs=bf(wuq_sw.reshape(MLA_Q_RANK, -1)),
        kvn=kv_norm[None], wk=bf(wk.reshape(MLA_KV_RANK, -1)), wv=bf(wv),
        cw=jnp.zeros((SUBLANES, conv_w.shape[1]), F32).at[:MLSTM_CONV].set(conv_w), cb=conv_b[None], gb=gbias)


def _tile_specs(nct, nt):
    tile = lambda w: pl.BlockSpec((None, TM, w), lambda b, t: (b, t, 0))
    per = TM // HALO
    prev = pl.BlockSpec((None, HALO, D_MODEL), lambda b, t: (b, jnp.maximum(t * per - 1, 0), 0))
    nxt = pl.BlockSpec((None, HALO, D_MODEL), lambda b, t: (b, jnp.minimum((t + 1) * per, nt * per - 1), 0))
    mod = pl.BlockSpec((None, None, 1, D_MODEL), lambda b, t: (b, (t >= nct).astype(jnp.int32), 0, 0))
    return tile, prev, nxt, mod


def _proj0(h, scale, shift, norm_g, cos_t, sin_t, w, nct):
    bsz, ttot, _ = h.shape
    nt = ttot // TM
    nb = MLSTM_HEADS * MLSTM_DH
    tile, prev, nxt, mod = _tile_specs(nct, nt)
    tab = pl.BlockSpec((TM, LANES), lambda b, t: (t, 0))
    wnames = ["wcq", "wckv", "wkr", "wkrs", "wmqk", "wmv", "wmo", "wgt", "wz",
              "qn", "wuq", "wuqs", "kvn", "wk", "wv", "cw", "cb", "gb"]
    ws = [w[n] for n in wnames]
    tok = lambda width, dt: jax.ShapeDtypeStruct((bsz, ttot, width), dt)
    out_shape = (tok(MLA_HEADS * LANES, BF16), tok(MLA_HEADS * LANES, BF16), tok(MLA_HEADS * MLA_V, BF16),
                 tok(nb, BF16), tok(nb, BF16), tok(nb, BF16),
                 jax.ShapeDtypeStruct((bsz, ttot // CHUNK, 4 * MLSTM_HEADS, CHUNK), F32),
                 tok(nb, F32), tok(D_MODEL, F32))
    out_specs = (tile(MLA_HEADS * LANES), tile(MLA_HEADS * LANES), tile(MLA_HEADS * MLA_V),
                 tile(nb), tile(nb), tile(nb),
                 pl.BlockSpec((None, TM // CHUNK, 4 * MLSTM_HEADS, CHUNK), lambda b, t: (b, t, 0, 0)),
                 tile(nb), tile(D_MODEL))
    return pl.pallas_call(
        functools.partial(_proj0_kernel, nct=nct, nt=nt),
        out_shape=out_shape,
        grid=(bsz, nt),
        in_specs=[tile(D_MODEL), prev, nxt, mod, mod, _full((1, D_MODEL)), tab, tab] + [_full(a.shape) for a in ws],
        out_specs=out_specs,
        scratch_shapes=[pltpu.VMEM((TM + 2 * HALO, 2 * nb), F32)],
        compiler_params=_params("parallel", "parallel"),
        name="proj0",
    )(h, h, h, scale, shift, norm_g[None], cos_t, sin_t, *ws)


def _softmax_pv(s, v):
    m = jnp.max(s, axis=-1, keepdims=True)
    p = jnp.exp(s - m)
    return _dot(p.astype(BF16), v) / jnp.sum(p, axis=-1, keepdims=True)


def _mla_kernel(q_ref, k_ref, v_ref, o_ref, *, nct):
    t = pl.program_id(2)
    lane = lax.broadcasted_iota(jnp.int32, (TM, LANES), 1)

    def attend(nkeys):
        outs = []
        for j in range(2):
            qj = q_ref[:, j * LANES:(j + 1) * LANES]
            kj = k_ref[0:nkeys, j * LANES:(j + 1) * LANES]
            outs.append(_softmax_pv(_dot_nt(qj, kj), v_ref[0:nkeys, :]))
        o_ref[...] = jnp.where(lane < MLA_V, outs[0], outs[1])

    @pl.when(t < nct)
    def _():
        attend(nct * TM)

    @pl.when(t >= nct)
    def _():
        attend(k_ref.shape[0])


def _mla_attention(q, k, v, nct):
    bsz, ttot, _ = q.shape
    nt = ttot // TM
    return pl.pallas_call(
        functools.partial(_mla_kernel, nct=nct),
        out_shape=jax.ShapeDtypeStruct((bsz, ttot, MLA_HEADS * MLA_V), F32),
        grid=(bsz, MLA_HEADS // 2, nt),
        in_specs=[pl.BlockSpec((None, TM, 2 * LANES), lambda b, p, t: (b, t, p)),
                  pl.BlockSpec((None, ttot, 2 * LANES), lambda b, p, t: (b, 0, p)),
                  pl.BlockSpec((None, ttot, LANES), lambda b, p, t: (b, 0, p))],
        out_specs=pl.BlockSpec((None, TM, LANES), lambda b, p, t: (b, t, p)),
        compiler_params=_params("parallel", "parallel", "parallel"),
        name="mla_attention",
    )(q, k, v)


def _scan_block(j, nct, nt):
    return jnp.where(j < nct, nct - 1 - j, nt - 1 - (j - nct))


def _mlstm_kernel(qf_ref, kf_ref, vf_ref, gf_ref, qr_ref, kr_ref, vr_ref, gr_ref,
                  hf_ref, hr_ref, c_scr, m_scr):
    @pl.when(pl.program_id(1) == 0)
    def _():
        c_scr[...] = jnp.zeros_like(c_scr)
        m_scr[...] = jnp.zeros_like(m_scr)

    L = CHUNK
    nchunk = TM // L
    row = lax.broadcasted_iota(jnp.int32, (L, L), 0)
    col = lax.broadcasted_iota(jnp.int32, (L, L), 1)
    eye = row == col
    ones = jnp.ones((L, MLSTM_DH), BF16)

    def to_col(v_row):
        return jnp.sum(jnp.where(eye, jnp.broadcast_to(v_row, (L, L)), 0.0), axis=1, keepdims=True)

    def chunk_step(ci, carry):
        for d, (q_ref, k_ref, v_ref, g_ref, o_ref) in enumerate(
                ((qf_ref, kf_ref, vf_ref, gf_ref, hf_ref), (qr_ref, kr_ref, vr_ref, gr_ref, hr_ref))):
            cd = ci if d == 0 else nchunk - 1 - ci
            base = pl.multiple_of(cd * L, L)
            g = g_ref[cd]
            causal = (col <= row) if d == 0 else (col >= row)
            for hd in range(MLSTM_HEADS):
                ch = d * MLSTM_HEADS + hd
                r0 = 2 * MLSTM_HEADS * d + hd
                i_row = g[r0:r0 + 1]
                b_row = g[r0 + MLSTM_HEADS:r0 + MLSTM_HEADS + 1]
                b_last = b_row[:, L - 1:L] if d == 0 else b_row[:, 0:1]
                b_col = to_col(b_row)
                i_col = to_col(i_row)
                m = m_scr[ch, 0:1, 0:1]
                logw = jnp.where(causal, b_col - (b_row - i_row), -jnp.inf)
                inter = b_col + m
                m_t = jnp.maximum(inter, jnp.max(logw, axis=1, keepdims=True))
                w_intra = jnp.exp(logw - m_t)
                w_inter = jnp.exp(inter - m_t)
                hs = slice(hd * MLSTM_DH, (hd + 1) * MLSTM_DH)
                q = q_ref[pl.ds(base, L), hs]
                k = k_ref[pl.ds(base, L), hs]
                v_ext = jnp.concatenate([v_ref[pl.ds(base, L), hs], ones], axis=1)
                state = c_scr[ch]
                p = (_dot_nt(q, k) * w_intra).astype(BF16)
                o = _dot(p, v_ext) + w_inter * _dot(q, state.astype(BF16))
                den = jnp.maximum(jnp.abs(o[:, MLSTM_DH:]), jnp.exp(-m_t))
                o_ref[pl.ds(base, L), hs] = o[:, :MLSTM_DH] / den
                logk = b_last - b_col + i_col
                m_new = jnp.maximum(b_last + m, jnp.max(logk, axis=0, keepdims=True))
                kw = (jnp.exp(logk - m_new) * k.astype(F32)).astype(BF16)
                c_scr[ch] = jnp.exp(b_last + m - m_new) * state + _dot_tn(kw, v_ext)
                m_scr[ch] = jnp.broadcast_to(m_new, m_scr.shape[1:])
        return carry

    lax.fori_loop(0, nchunk, chunk_step, 0)


def _mlstm(mq, mk, mv, gt, nct):
    bsz, ttot, nb = mq.shape
    nt = ttot // TM
    per = TM // CHUNK
    fwd = pl.BlockSpec((None, TM, nb), lambda b, j: (b, j, 0))
    rev = pl.BlockSpec((None, TM, nb), lambda b, j: (b, _scan_block(j, nct, nt), 0))
    gshape = (None, per, 4 * MLSTM_HEADS, CHUNK)
    gfwd = pl.BlockSpec(gshape, lambda b, j: (b, j, 0, 0))
    grev = pl.BlockSpec(gshape, lambda b, j: (b, _scan_block(j, nct, nt), 0, 0))
    return pl.pallas_call(
        _mlstm_kernel,
        out_shape=(jax.ShapeDtypeStruct((bsz, ttot, nb), F32),) * 2,
        grid=(bsz, nt),
        in_specs=[fwd, fwd, fwd, gfwd, rev, rev, rev, grev],
        out_specs=(fwd, rev),
        scratch_shapes=[pltpu.VMEM((2 * MLSTM_HEADS, MLSTM_DH, 2 * MLSTM_DH), F32),
                        pltpu.VMEM((2 * MLSTM_HEADS, SUBLANES, LANES), F32)],
        compiler_params=_params("parallel", "arbitrary"),
        name="mlstm_scan",
    )(mq, mk, mv, gt, mq, mk, mv, gt)


def _head_rms(x, n_heads):
    w = x.shape[1] // n_heads
    return jnp.concatenate([_rms(x[:, i * w:(i + 1) * w]) for i in range(n_heads)], axis=1)


def _out0_kernel(h_ref, gate_ref, a_ref, hf_ref, hr_ref, mo_ref, z_ref, hn_ref, w_ref, o_ref):
    hm = _head_rms(mo_ref[...] * (hf_ref[...] + hr_ref[...]), MLSTM_HEADS) * hn_ref[...]
    cat = (jnp.concatenate([a_ref[...], hm], axis=1) * z_ref[...]).astype(BF16)
    o_ref[...] = h_ref[...] + gate_ref[...] * _dot(cat, w_ref[...])


def _out0(h, gate, a, hf, hr, mo, z, h_norm, w_out, nct):
    bsz, ttot, _ = h.shape
    nt = ttot // TM
    tile, _, _, mod = _tile_specs(nct, nt)
    nb = MLSTM_HEADS * MLSTM_DH
    return pl.pallas_call(
        _out0_kernel,
        out_shape=jax.ShapeDtypeStruct(h.shape, F32),
        grid=(bsz, nt),
        in_specs=[tile(D_MODEL), mod, tile(MLA_HEADS * MLA_V), tile(nb), tile(nb), tile(nb), tile(D_MODEL),
                  _full((1, nb)), _full(w_out.shape)],
        out_specs=tile(D_MODEL),
        compiler_params=_params("parallel", "parallel"),
        name="out0",
    )(h, gate, a, hf, hr, mo, z, h_norm[None], w_out.astype(BF16))


def _proj1_kernel(h_ref, sc_ref, sh_ref, g_ref, wq_ref, wk_ref, wv_ref, wga_ref, wnq_ref, wnk_ref, wnv_ref, wz_ref,
                  wgate_ref, bgate_ref,
                  gq_out, gk_out, gv_out, bc_out, nq_out, nk_out, nv_out, z_out):
    ub = _modnorm(h_ref[...], g_ref[...], sc_ref[...], sh_ref[...]).astype(BF16)
    gq_out[...] = (_dot(ub, wq_ref[...]) * GLA_DK ** -0.5).astype(BF16)
    gk_out[...] = _dot(ub, wk_ref[...]).astype(BF16)
    gv_out[...] = _dot(ub, wv_ref[...]).astype(BF16)
    nq_out[...] = (_dot(ub, wnq_ref[...]) * NA_DH ** -0.5).astype(BF16)
    nk_out[...] = _dot(ub, wnk_ref[...]).astype(BF16)
    nv_out[...] = _dot(ub, wnv_ref[...]).astype(BF16)
    z_out[...] = _silu(_dot(ub, wz_ref[...]))

    ga = _dot(ub, wga_ref[...]).astype(BF16)
    lg = _log_sigmoid(_dot(ga, wgate_ref[...]) + bgate_ref[...]) * (1.0 / GLA_TAU)
    hi = lg.astype(BF16)
    r1 = lg - hi.astype(F32)
    mid = r1.astype(BF16)
    lo = (r1 - mid.astype(F32)).astype(BF16)
    row = lax.broadcasted_iota(jnp.int32, (TM, TM), 0)
    col = lax.broadcasted_iota(jnp.int32, (TM, TM), 1)
    same = (row // CHUNK) == (col // CHUNK)
    nk = GLA_HEADS * GLA_DK
    for d in range(2):
        tri = jnp.where(jnp.logical_and(same, (col <= row) if d == 0 else (col >= row)), 1.0, 0.0).astype(BF16)
        cols = slice(d * nk, (d + 1) * nk)
        bc_out[:, cols] = _dot(tri, hi[:, cols]) + _dot(tri, mid[:, cols]) + _dot(tri, lo[:, cols])


def _proj1_weights(w_in, w_gate, b_gate):
    nk = GLA_HEADS * GLA_DK
    nc = GLA_HEADS * GLA_DV
    nd = NA_HEADS * NA_DH
    o = np.cumsum([0, nk, nk, nc, 2 * GLA_GATE_RANK, nd, nd, nd, nc + nd])
    r = GLA_GATE_RANK
    wgate = jnp.zeros((2 * r, 2 * nk), F32).at[:r, :nk].set(w_gate[0]).at[r:, nk:].set(w_gate[1])
    bf = lambda a: a.astype(BF16)
    names = ["wq", "wk", "wv", "wga", "wnq", "wnk", "wnv", "wz"]
    w = {n: bf(w_in[:, o[i]:o[i + 1]]) for i, n in enumerate(names)}
    w["wgate"] = bf(wgate)
    w["bgate"] = b_gate.reshape(1, 2 * nk)
    return w


def _proj1(h, scale, shift, norm_g, w, nct):
    bsz, ttot, _ = h.shape
    nt = ttot // TM
    tile, _, _, mod = _tile_specs(nct, nt)
    nk = GLA_HEADS * GLA_DK
    nc = GLA_HEADS * GLA_DV
    nd = NA_HEADS * NA_DH
    wnames = ["wq", "wk", "wv", "wga", "wnq", "wnk", "wnv", "wz", "wgate", "bgate"]
    ws = [w[n] for n in wnames]
    tok = lambda width, dt: jax.ShapeDtypeStruct((bsz, ttot, width), dt)
    return pl.pallas_call(
        _proj1_kernel,
        out_shape=(tok(nk, BF16), tok(nk, BF16), tok(nc, BF16), tok(2 * nk, F32),
                   tok(nd, BF16), tok(nd, BF16), tok(nd, BF16), tok(nc + nd, F32)),
        grid=(bsz, nt),
        in_specs=[tile(D_MODEL), mod, mod, _full((1, D_MODEL))] + [_full(a.shape) for a in ws],
        out_specs=(tile(nk), tile(nk), tile(nc), tile(2 * nk), tile(nd), tile(nd), tile(nd), tile(nc + nd)),
        compiler_params=_params("parallel", "parallel"),
        name="proj1",
    )(h, scale, shift, norm_g[None], *ws)


def _gla_kernel(qf_ref, kf_ref, vf_ref, bf_ref, qr_ref, kr_ref, vr_ref, br_ref, of_ref, or_ref, s_scr):
    @pl.when(pl.program_id(1) == 0)
    def _():
        s_scr[...] = jnp.zeros_like(s_scr)

    L = CHUNK
    nchunk = TM // L
    npair = GLA_HEADS // 2
    row = lax.broadcasted_iota(jnp.int32, (L, L), 0)
    col = lax.broadcasted_iota(jnp.int32, (L, L), 1)
    lane = lax.broadcasted_iota(jnp.int32, (L, LANES), 1)
    rowv = lax.broadcasted_iota(jnp.int32, (L, LANES), 0)
    head_lane = [lane < GLA_DK, lane >= GLA_DK]
    srow = lax.broadcasted_iota(jnp.int32, (LANES, 2 * GLA_DV), 0)
    scol = lax.broadcasted_iota(jnp.int32, (LANES, 2 * GLA_DV), 1)
    own_block = (srow < GLA_DK) == (scol < GLA_DV)
    e_row = lax.broadcasted_iota(jnp.int32, (LANES, LANES), 0)
    e_col = lax.broadcasted_iota(jnp.int32, (LANES, LANES), 1)
    leaf_idx = lax.broadcasted_iota(jnp.int32, (GLA_LEAF * GLA_LEAF, LANES), 0)
    leaf_s, leaf_t = leaf_idx // GLA_LEAF, leaf_idx % GLA_LEAF
    head_ones = jnp.where(own_block, 1.0, 0.0).astype(BF16)

    def chunk_step(ci, carry):
        for d, (q_ref, k_ref, v_ref, b_ref, o_ref) in enumerate(
                ((qf_ref, kf_ref, vf_ref, bf_ref, of_ref), (qr_ref, kr_ref, vr_ref, br_ref, or_ref))):
            cd = ci if d == 0 else nchunk - 1 - ci
            base = pl.multiple_of(cd * L, L)
            tau_r = row if d == 0 else L - 1 - row
            tau_c = col if d == 0 else L - 1 - col
            tau_v = rowv if d == 0 else L - 1 - rowv
            for p in range(npair):
                ls = slice(p * LANES, (p + 1) * LANES)
                q = q_ref[pl.ds(base, L), ls].astype(F32)
                k = k_ref[pl.ds(base, L), ls].astype(F32)
                bc = b_ref[pl.ds(base, L), ls]
                v = v_ref[pl.ds(base, L), p * 2 * GLA_DV:(p + 1) * 2 * GLA_DV]
                b_last = bc[L - 1:L] if d == 0 else bc[0:1]
                state = s_scr[d, p]

                o_pair = _dot((q * jnp.exp(bc)).astype(BF16), state.astype(BF16))

                att = [jnp.zeros((L, L), F32), jnp.zeros((L, L), F32)]
                bs = L // 2
                while bs >= GLA_LEAF:
                    later = (tau_v // bs) % 2 == 1
                    nblk = L // (2 * bs)
                    ref_b = None
                    for blk in range(nblk):
                        tau_ref = blk * 2 * bs + bs - 1
                        idx = tau_ref if d == 0 else L - 1 - tau_ref
                        r = jnp.broadcast_to(bc[idx:idx + 1], (L, LANES))
                        ref_b = r if ref_b is None else jnp.where(tau_v // (2 * bs) == blk, r, ref_b)
                    qs = jnp.where(later, q * jnp.exp(jnp.where(later, bc - ref_b, 0.0)), 0.0)
                    ks = jnp.where(later, 0.0, k * jnp.exp(jnp.where(later, 0.0, ref_b - bc))).astype(BF16)
                    pair_mask = (tau_r // (2 * bs)) == (tau_c // (2 * bs))
                    for hh in range(2):
                        a = _dot_nt(jnp.where(head_lane[hh], qs, 0.0).astype(BF16), ks)
                        att[hh] = att[hh] + jnp.where(pair_mask, a, 0.0)
                    bs //= 2
                o_off = jnp.concatenate(
                    [_dot(att[hh].astype(BF16), v[:, hh * GLA_DV:(hh + 1) * GLA_DV]) for hh in range(2)], axis=1)

                vf = v.astype(F32)
                leaf_out = []
                for blk in range(L // GLA_LEAF):
                    rs = slice(blk * GLA_LEAF, (blk + 1) * GLA_LEAF)
                    qb, bb, kb, vb = q[rs], bc[rs], k[rs], vf[rs]
                    rep = lambda a: jnp.concatenate(
                        [jnp.broadcast_to(a[s:s + 1], (GLA_LEAF, a.shape[1])) for s in range(GLA_LEAF)], axis=0)
                    til = lambda a: jnp.concatenate([a] * GLA_LEAF, axis=0)
                    ok = (leaf_t >= leaf_s) if d == 0 else (leaf_t <= leaf_s)
                    w = jnp.where(ok, til(qb) * jnp.exp(jnp.where(ok, til(bb) - rep(bb), 0.0)) * rep(kb), 0.0)
                    a = _dot(w.astype(BF16), head_ones) * rep(vb)
                    acc = a[0:GLA_LEAF]
                    for s in range(1, GLA_LEAF):
                        acc = acc + a[s * GLA_LEAF:(s + 1) * GLA_LEAF]
                    leaf_out.append(acc)
                o_ref[pl.ds(base, L), p * 2 * GLA_DV:(p + 1) * 2 * GLA_DV] = (
                    o_pair + o_off + jnp.concatenate(leaf_out, axis=0))

                ke = (k * jnp.exp(b_last - bc)).astype(BF16)
                decay_col = jnp.sum(jnp.where(e_row == e_col, jnp.broadcast_to(jnp.exp(b_last), (LANES, LANES)), 0.0),
                                    axis=1, keepdims=True)
                s_scr[d, p] = decay_col * state + jnp.where(own_block, _dot_tn(ke, v), 0.0)
        return carry

    lax.fori_loop(0, nchunk, chunk_step, 0)


def _gla(gq, gk, gv, bc, nct):
    bsz, ttot, nk = gq.shape
    nc = gv.shape[2]
    nt = ttot // TM
    fwd = lambda w: pl.BlockSpec((None, TM, w), lambda b, j: (b, j, 0))
    rev = lambda w: pl.BlockSpec((None, TM, w), lambda b, j: (b, _scan_block(j, nct, nt), 0))
    bfwd = pl.BlockSpec((None, TM, nk), lambda b, j: (b, j, 0))
    brev = pl.BlockSpec((None, TM, nk), lambda b, j: (b, _scan_block(j, nct, nt), 1))
    return pl.pallas_call(
        _gla_kernel,
        out_shape=(jax.ShapeDtypeStruct((bsz, ttot, nc), F32),) * 2,
        grid=(bsz, nt),
        in_specs=[fwd(nk), fwd(nk), fwd(nc), bfwd, rev(nk), rev(nk), rev(nc), brev],
        out_specs=(fwd(nc), rev(nc)),
        scratch_shapes=[pltpu.VMEM((2, GLA_HEADS // 2, LANES, 2 * GLA_DV), F32)],
        compiler_params=_params("parallel", "arbitrary"),
        name="gla_scan",
    )(gq, gk, gv, bc, gq, gk, gv, bc)


def _na_bias_tables(rpb, rows):
    kh = NA_KH
    c = np.arange(GRID_W)
    cs = np.clip(c - NA_KW // 2, 0, GRID_W - NA_KW)
    col_ok = (c[None, :] >= cs[:, None]) & (c[None, :] < cs[:, None] + NA_KW)
    pad = GRID_W - NA_KW
    rpb_pad = jnp.pad(rpb.astype(F32), ((0, 0), (0, 0), (pad, pad)))
    t1 = jnp.stack([rpb_pad[:, :, GRID_W - 1 - cq:2 * GRID_W - 1 - cq] for cq in range(GRID_W)], axis=2)
    t1 = jnp.where(col_ok, t1, NEG_BIG)
    tabs = []
    for r0 in (0, NA_QROWS, rows - NA_QROWS):
        kb = int(np.clip(r0 - kh // 2, 0, rows - NA_KROWS))
        qr = r0 + np.arange(NA_QROWS)
        ws = np.clip(qr - kh // 2, 0, rows - kh)
        krow = kb + np.arange(NA_KROWS)
        row_ok = (krow[None, :] >= ws[:, None]) & (krow[None, :] < ws[:, None] + kh)
        drow = np.clip(krow[None, :] - qr[:, None] + (NA_KH - 1), 0, 2 * NA_KH - 2)
        tv = jnp.take(t1, jnp.asarray(drow.reshape(-1)), axis=1)
        tv = jnp.where(row_ok.reshape(1, -1, 1, 1), tv, NEG_BIG)
        tv = tv.reshape(NA_HEADS, NA_QROWS, NA_KROWS, GRID_W, GRID_W).transpose(0, 1, 3, 2, 4)
        tabs.append(tv.reshape(NA_HEADS, NA_QROWS * GRID_W, NA_KROWS * GRID_W))
    return jnp.stack(tabs)


def _na_kernel(q_ref, k_ref, v_ref, bias_ref, o_ref, *, n_ctx, rows):
    rb = pl.program_id(2)
    nq = NA_QROWS * GRID_W
    nk = NA_KROWS * GRID_W
    kb = jnp.clip(rb * NA_QROWS - NA_KH // 2, 0, rows - NA_KROWS)
    start = pl.multiple_of(n_ctx + kb * GRID_W, GRID_W)
    q = q_ref[...]
    k_loc = k_ref[pl.ds(start, nk), :]
    v_loc = v_ref[pl.ds(start, nk), :]
    k_ctx = k_ref[0:n_ctx, :]
    v_ctx = v_ref[0:n_ctx, :]
    lane = lax.broadcasted_iota(jnp.int32, (nq, LANES), 1)
    outs = []
    for j in range(2):
        mine = (lane < NA_DH) if j == 0 else (lane >= NA_DH)
        qj = jnp.where(mine, q, jnp.zeros_like(q))
        s_loc = _dot_nt(qj, k_loc) + bias_ref[j]
        s_ctx = _dot_nt(qj, k_ctx)
        m = jnp.maximum(jnp.max(s_loc, axis=-1, keepdims=True), jnp.max(s_ctx, axis=-1, keepdims=True))
        p_loc = jnp.exp(s_loc - m)
        p_ctx = jnp.exp(s_ctx - m)
        denom = jnp.sum(p_loc, axis=-1, keepdims=True) + jnp.sum(p_ctx, axis=-1, keepdims=True)
        outs.append((_dot(p_loc.astype(BF16), v_loc) + _dot(p_ctx.astype(BF16), v_ctx)) / denom)
    o_ref[...] = jnp.where(lane < NA_DH, outs[0], outs[1])


def _na_attention(nq, nk, nv, bias, n_ctx, rows):
    bsz, ttot, _ = nq.shape
    nrb = rows // NA_QROWS
    qrows = NA_QROWS * GRID_W
    krows = NA_KROWS * GRID_W
    qoff = n_ctx // qrows

    def variant(rb):
        return jnp.where(rb == 0, 0, jnp.where(rb == nrb - 1, 2, 1))

    return pl.pallas_call(
        functools.partial(_na_kernel, n_ctx=n_ctx, rows=rows),
        out_shape=jax.ShapeDtypeStruct((bsz, rows * GRID_W, NA_HEADS * NA_DH), F32),
        grid=(bsz, NA_HEADS // 2, nrb),
        in_specs=[pl.BlockSpec((None, qrows, LANES), lambda b, p, rb: (b, qoff + rb, p)),
                  pl.BlockSpec((None, ttot, LANES), lambda b, p, rb: (b, 0, p)),
                  pl.BlockSpec((None, ttot, LANES), lambda b, p, rb: (b, 0, p)),
                  pl.BlockSpec((None, 2, qrows, krows), lambda b, p, rb: (variant(rb), p, 0, 0))],
        out_specs=pl.BlockSpec((None, qrows, LANES), lambda b, p, rb: (b, rb, p)),
        compiler_params=_params("parallel", "parallel", "parallel"),
        name="neighbourhood_attention",
    )(nq, nk, nv, bias)


def _out1_kernel(h_ref, gate_ref, of_ref, or_ref, na_ref, z_ref, gn_ref, w_ref, fn_ref, o_ref):
    g = _head_rms(of_ref[...] + or_ref[...], GLA_HEADS) * gn_ref[...]
    cat = (jnp.concatenate([g, na_ref[...]], axis=1) * z_ref[...]).astype(BF16)
    h = h_ref[...] + gate_ref[...] * _dot(cat, w_ref[...])
    o_ref[...] = _rms(h) * fn_ref[...]


def _out1(h, gate, of, orv, na, z, gla_norm, w_out, final_norm, nct):
    bsz, ttot, _ = h.shape
    seq = na.shape[1]
    nc = GLA_HEADS * GLA_DV
    nd = NA_HEADS * NA_DH
    lat = lambda w: pl.BlockSpec((None, TM, w), lambda b, t: (b, t + nct, 0))
    return pl.pallas_call(
        _out1_kernel,
        out_shape=jax.ShapeDtypeStruct((bsz, seq, D_MODEL), F32),
        grid=(bsz, seq // TM),
        in_specs=[lat(D_MODEL), pl.BlockSpec((None, None, 1, D_MODEL), lambda b, t: (b, 1, 0, 0)),
                  lat(nc), lat(nc), pl.BlockSpec((None, TM, nd), lambda b, t: (b, t, 0)), lat(nc + nd),
                  _full((1, nc)), _full(w_out.shape), _full((1, D_MODEL))],
        out_specs=pl.BlockSpec((None, TM, D_MODEL), lambda b, t: (b, t, 0)),
        compiler_params=_params("parallel", "parallel"),
        name="out1",
    )(h, gate, of, orv, na, z, gla_norm[None], w_out.astype(BF16), final_norm[None])


def kernel(x, c, ctx, c_ctx, l0_norm, l0_w_mod, l0_b_mod, l0_w_in, l0_mla_q_norm, l0_mla_w_uq, l0_mla_kv_norm, l0_mla_w_ukv, l0_mlstm_conv_w, l0_mlstm_conv_b, l0_mlstm_b_i, l0_mlstm_b_f, l0_mlstm_norm, l0_w_out, l1_norm, l1_w_mod, l1_b_mod, l1_w_in, l1_gla_w_gate, l1_gla_b_gate, l1_gla_norm, l1_na_rpb, l1_w_out, final_norm):
    bsz, seq, d = x.shape
    n_ctx = ctx.shape[1]
    rows = seq // GRID_W
    assert d == D_MODEL and seq % TM == 0 and n_ctx % TM == 0
    assert rows % NA_QROWS == 0 and rows >= NA_KROWS and rows // NA_QROWS >= 3
    nct = n_ctx // TM

    h = jnp.concatenate([ctx, x], axis=1)

    shift, scale, gate = _mod_vectors(c, c_ctx, l0_w_mod, l0_b_mod)
    cos_t, sin_t = _rope_tables(n_ctx, seq)
    w0 = _proj0_weights(l0_w_in, l0_mla_q_norm, l0_mla_w_uq, l0_mla_kv_norm, l0_mla_w_ukv,
                        l0_mlstm_conv_w, l0_mlstm_conv_b, l0_mlstm_b_i, l0_mlstm_b_f)
    q, k, v, mq, mk, mv, gt, mo, z = _proj0(h, scale, shift, l0_norm, cos_t, sin_t, w0, nct)
    a = _mla_attention(q, k, v, nct)
    hf, hr = _mlstm(mq, mk, mv, gt, nct)
    h = _out0(h, gate, a, hf, hr, mo, z, l0_mlstm_norm, l0_w_out, nct)

    shift, scale, gate = _mod_vectors(c, c_ctx, l1_w_mod, l1_b_mod)
    w1 = _proj1_weights(l1_w_in, l1_gla_w_gate, l1_gla_b_gate)
    gq, gk, gv, bc, nq, nk, nv, z = _proj1(h, scale, shift, l1_norm, w1, nct)
    of, orv = _gla(gq, gk, gv, bc, nct)
    na = _na_attention(nq, nk, nv, _na_bias_tables(l1_na_rpb, rows), n_ctx, rows)
    return _out1(h, gate, of, orv, na, z, l1_gla_norm, l1_w_out, final_norm, nct)
```

```python
import functools

import jax
import jax.numpy as jnp
import numpy as np
from jax import lax
from jax.experimental import pallas as pl
from jax.experimental.pallas import tpu as pltpu

F32 = jnp.float32
BF16 = jnp.bfloat16

D_MODEL = 1024
GRID_W = 64
EPS = 1e-6
ROPE_BASE = 10000.0

MLA_HEADS = 8
MLA_Q_RANK = 384
MLA_KV_RANK = 256
MLA_NOPE = 64
MLA_ROPE = 32
MLA_V = 64
ROPE_PAIRS = MLA_ROPE // 4
MLSTM_HEADS = 4
MLSTM_DH = 128
MLSTM_CONV = 3
GLA_HEADS = 4
GLA_DK = 64
GLA_DV = 128
GLA_GATE_RANK = 16
GLA_TAU = 16.0
NA_HEADS = 8
NA_DH = 64
NA_KH = 8
NA_KW = 16

CHUNK = 64
TM = 256
LANES = 128
SUBLANES = 8
HALO = SUBLANES
NA_QROWS = 4
NA_KROWS = NA_QROWS + NA_KH - 1
GLA_LEAF = 16
MLA_HPS = 4
VMEM_LIMIT = 56 * 1024 * 1024
NEG_BIG = -1e30
LOG2E = 1.4426950408889634


def _dot(a, b):
    return jnp.dot(a, b, preferred_element_type=F32)


def _dot_nt(a, b):
    return lax.dot_general(a, b, (((1,), (1,)), ((), ())), preferred_element_type=F32)


def _dot_tn(a, b):
    return lax.dot_general(a, b, (((0,), (0,)), ((), ())), preferred_element_type=F32)


def _rms(x):
    return x * lax.rsqrt(jnp.mean(x * x, axis=-1, keepdims=True) + EPS)


def _sigmoid(x):
    return 1.0 / (1.0 + jnp.exp(-x))


def _silu(x):
    return x * _sigmoid(x)


def _log_sigmoid(x):
    return jnp.minimum(x, 0.0) - jnp.log(1.0 + jnp.exp(-jnp.abs(x)))


def _params(*sem):
    return pltpu.CompilerParams(dimension_semantics=sem, vmem_limit_bytes=VMEM_LIMIT)


def _full(shape):
    nd = len(shape)
    return pl.BlockSpec(shape, lambda *_: (0,) * nd)


def _mod_kernel(c_ref, w_ref, b_ref, o_ref):
    o_ref[...] = _dot(_silu(c_ref[...]).astype(BF16), w_ref[...]) + b_ref[...]


def _modulation(cc, w_mod, b_mod):
    rows, d = cc.shape
    n = w_mod.shape[1]
    return pl.pallas_call(
        _mod_kernel,
        out_shape=jax.ShapeDtypeStruct((rows, n), F32),
        grid=(n // d,),
        in_specs=[_full((rows, d)), pl.BlockSpec((d, d), lambda j: (0, j)), pl.BlockSpec((1, d), lambda j: (0, j))],
        out_specs=pl.BlockSpec((rows, d), lambda j: (0, j)),
        compiler_params=_params("parallel"),
        name="modulation",
    )(cc, w_mod.astype(BF16), b_mod[None])


def _mod_vectors(c, c_ctx, w_mod, b_mod):
    bsz = c.shape[0]
    rows = -(-(bsz + 1) // SUBLANES) * SUBLANES
    cc = jnp.zeros((rows, D_MODEL), F32).at[:bsz].set(c).at[bsz].set(c_ctx)
    mod = _modulation(cc, w_mod, b_mod)
    out = []
    for part in jnp.split(mod, 3, axis=-1):
        ctx_v = jnp.broadcast_to(part[bsz][None], (bsz, D_MODEL))
        out.append(jnp.stack([ctx_v, part[:bsz]], axis=1)[:, :, None, :])
    return out


def _modnorm(x, g, scale, shift):
    return _rms(x) * g * (1.0 + scale) + shift


def _seg_scan(x, reverse, op, fill):
    width = x.shape[1]
    pos = lax.broadcasted_iota(jnp.int32, x.shape, 1) % CHUNK
    k = 1
    while k < CHUNK:
        if reverse:
            x = op(x, jnp.where(pos < CHUNK - k, pltpu.roll(x, width - k, axis=1), fill))
        else:
            x = op(x, jnp.where(pos >= k, pltpu.roll(x, k, axis=1), fill))
        k *= 2
    return x


def _split3(x):
    hi = x.astype(BF16)
    r1 = x - hi.astype(F32)
    mid = r1.astype(BF16)
    return hi, mid, (r1 - mid.astype(F32)).astype(BF16)


def _proj0_kernel(h_ref, hp_ref, hn_ref, sc_ref, sh_ref, g_ref, cos_ref, sin_ref,
                  wcq_ref, wckv_ref, wkr_ref, wkrs_ref, wmqk_ref, wmv_ref, wmo_ref, wgt_ref, wz_ref,
                  qn_ref, wuq_ref, wuqs_ref, kvn_ref, wk_ref, wv_ref, cw_ref, cb_ref, gb_ref,
                  q_out, k_out, v_out, mq_out, mk_out, mv_out, gr_out, gc_out, mo_out, z_out,
                  pbuf, *, nct, nt):
    t = pl.program_id(1)
    x = jnp.concatenate([hp_ref[...], h_ref[...], hn_ref[...]], axis=0)
    ub_ext = _modnorm(x, g_ref[...], sc_ref[...], sh_ref[...]).astype(BF16)
    ub = ub_ext[HALO:HALO + TM]

    gt = _dot_nt(wgt_ref[...], ub) + gb_ref[...]
    half = MLSTM_HEADS
    per_dir = []
    for d in range(2):
        gd = gt[d * SUBLANES:(d + 1) * SUBLANES]
        bcum = _seg_scan(_log_sigmoid(gd), d == 1, jnp.add, 0.0)
        b_top = pltpu.roll(bcum, half, axis=0)
        c8 = gd - b_top
        pm8 = _seg_scan(c8, d == 1, jnp.maximum, -jnp.inf)
        per_dir.append((c8, pm8, b_top))

    cos = cos_ref[...]
    sin = sin_ref[...]
    cos_all = jnp.concatenate([cos] * MLA_HEADS, axis=1)
    sin_all = jnp.concatenate([sin] * MLA_HEADS, axis=1)

    nq = (_rms(_dot(ub, wcq_ref[...])) * qn_ref[...]).astype(BF16)
    q = _dot(nq, wuq_ref[...]) * cos_all + _dot(nq, wuqs_ref[...]) * sin_all
    q_out[...] = (q * ((MLA_NOPE + MLA_ROPE) ** -0.5 * LOG2E)).astype(BF16)

    nkv = (_rms(_dot(ub, wckv_ref[...])) * kvn_ref[...]).astype(BF16)
    rot = _dot(ub, wkr_ref[...]) * cos + _dot(ub, wkrs_ref[...]) * sin
    k_out[...] = (_dot(nkv, wk_ref[...]) + jnp.concatenate([rot] * MLA_HEADS, axis=1)).astype(BF16)
    vlane = lax.broadcasted_iota(jnp.int32, (TM, MLA_HEADS * LANES), 1)
    is_value = ((vlane % LANES) < MLA_V) == ((vlane // LANES) % 2 == 0)
    v_out[...] = jnp.where(is_value, _dot(nkv, wv_ref[...]), 1.0).astype(BF16)

    pqk = _dot(ub_ext, wmqk_ref[...])
    prev_ok = jnp.logical_and(t != 0, t != nct)
    next_ok = jnp.logical_and(t != nct - 1, t != nt - 1)
    row = lax.broadcasted_iota(jnp.int32, pqk.shape, 0)
    keep = jnp.logical_and(jnp.logical_or(row >= HALO, prev_ok), jnp.logical_or(row < HALO + TM, next_ok))
    pbuf[...] = jnp.where(keep, pqk, 0.0)
    cw = cw_ref[...]
    y = (pbuf[HALO - 1:HALO - 1 + TM, :] * cw[0:1] + pbuf[HALO:HALO + TM, :] * cw[1:2]
         + pbuf[HALO + 1:HALO + 1 + TM, :] * cw[2:3] + cb_ref[...])
    y = _silu(y)
    nb = MLSTM_HEADS * MLSTM_DH
    mq_out[...] = (y[:, :nb] * MLSTM_DH ** -0.5).astype(BF16)
    mk_out[...] = y[:, nb:].astype(BF16)
    mv_out[...] = _dot(ub, wmv_ref[...]).astype(BF16)
    mo_out[...] = _sigmoid(_dot(ub, wmo_ref[...])).astype(BF16)
    z_out[...] = _silu(_dot(ub, wz_ref[...])).astype(BF16)

    r8 =lax.broadcasted_iota(jnp.int32, (SUBLANES, TM), 0)
    both = lambda i: jnp.where(r8 < half, per_dir[0][i], pltpu.roll(per_dir[1][i], half, axis=0))
    c_rows, pm_rows, b_rows = both(0), both(1), both(2)
    for ci in range(TM // CHUNK):
        gr_out[ci] = c_rows[:, ci * CHUNK:(ci + 1) * CHUNK]
    stack = jnp.concatenate([c_rows, pm_rows, b_rows, jnp.zeros((LANES - 3 * SUBLANES, TM), F32)], axis=0)
    er = lax.broadcasted_iota(jnp.int32, (TM, TM), 0)
    ec = lax.broadcasted_iota(jnp.int32, (TM, TM), 1)
    eye = jnp.where(er == ec, 1.0, 0.0).astype(BF16)
    gc_out[...] = sum(_dot_nt(eye, part) for part in _split3(stack))


def _rope_tables(n_ctx, seq):
    t = jnp.arange(seq)
    inv = 1.0 / (ROPE_BASE ** (jnp.arange(ROPE_PAIRS, dtype=F32) / ROPE_PAIRS))
    ang = jnp.concatenate([(t // GRID_W)[:, None] * inv, (t % GRID_W)[:, None] * inv], axis=-1)
    cos, sin = jnp.cos(ang), jnp.sin(ang)
    j = np.arange(MLA_ROPE)
    src = (j // (2 * ROPE_PAIRS)) * ROPE_PAIRS + (j % ROPE_PAIRS)
    sign = np.where((j % (2 * ROPE_PAIRS)) < ROPE_PAIRS, -1.0, 1.0).astype(np.float32)
    cos_full = cos[:, src]
    sin_full = sin[:, src] * sign
    cos_t = jnp.ones((n_ctx + seq, LANES), F32).at[n_ctx:, MLA_NOPE:MLA_NOPE + MLA_ROPE].set(cos_full)
    sin_t = jnp.zeros((n_ctx + seq, LANES), F32).at[n_ctx:, MLA_NOPE:MLA_NOPE + MLA_ROPE].set(sin_full)
    return cos_t, sin_t


def _rope_partner():
    j = np.arange(MLA_ROPE)
    return np.where((j % (2 * ROPE_PAIRS)) < ROPE_PAIRS, j + ROPE_PAIRS, j - ROPE_PAIRS)


def _proj0_weights(w_in, q_norm, w_uq, kv_norm, w_ukv, conv_w, conv_b, b_i, b_f):
    o = np.cumsum([0, MLA_Q_RANK, MLA_KV_RANK, MLA_ROPE] + [MLSTM_HEADS * MLSTM_DH] * 4 + [4 * MLSTM_HEADS, D_MODEL])
    part = _rope_partner()
    w_kr = w_in[:, o[2]:o[3]]
    pad_kr = jnp.zeros((D_MODEL, LANES), F32).at[:, MLA_NOPE:MLA_NOPE + MLA_ROPE]
    dq = MLA_NOPE + MLA_ROPE
    wuq = jnp.zeros((MLA_Q_RANK, MLA_HEADS, LANES), F32).at[:, :, :dq].set(w_uq.reshape(MLA_Q_RANK, MLA_HEADS, dq))
    wuq_sw = jnp.zeros((MLA_Q_RANK, MLA_HEADS, LANES), F32).at[:, :, MLA_NOPE:dq].set(
        w_uq.reshape(MLA_Q_RANK, MLA_HEADS, dq)[:, :, MLA_NOPE + part])
    wkv = w_ukv.reshape(MLA_KV_RANK, MLA_HEADS, MLA_NOPE + MLA_V)
    wk = jnp.zeros((MLA_KV_RANK, MLA_HEADS, LANES), F32).at[:, :, :MLA_NOPE].set(wkv[:, :, :MLA_NOPE])
    wv = jnp.zeros((MLA_KV_RANK, MLA_HEADS // 2, 2, LANES), F32)
    wv = wv.at[:, :, 0, :MLA_V].set(wkv[:, 0::2, MLA_NOPE:]).at[:, :, 1, MLA_V:].set(wkv[:, 1::2, MLA_NOPE:])
    wv = wv.reshape(MLA_KV_RANK, MLA_HEADS * LANES)
    gbias = jnp.stack([b_i, b_f], axis=1).reshape(4 * MLSTM_HEADS, 1)
    bf = lambda a: a.astype(BF16)
    return dict(
        wcq=bf(w_in[:, o[0]:o[1]]), wckv=bf(w_in[:, o[1]:o[2]]),
        wkr=bf(pad_kr.set(w_kr)), wkrs=bf(pad_kr.set(w_kr[:, part])),
        wmqk=bf(w_in[:, o[3]:o[5]]), wmv=bf(w_in[:, o[5]:o[6]]), wmo=bf(w_in[:, o[6]:o[7]]),
        wgt=bf(w_in[:, o[7]:o[8]].T), wz=bf(w_in[:, o[8]:o[9]]),
        qn=q_norm[None], wuq=bf(wuq.reshape(MLA_Q_RANK, -1)), wuqs=bf(wuq_sw.reshape(MLA_Q_RANK, -1)),
        kvn=kv_norm[None], wk=bf(wk.reshape(MLA_KV_RANK, -1)), wv=bf(wv),
        cw=jnp.zeros((SUBLANES, conv_w.shape[1]), F32).at[:MLSTM_CONV].set(conv_w), cb=conv_b[None], gb=gbias)


def _tile_specs(nct, nt):
    tile = lambda w: pl.BlockSpec((None, TM, w), lambda b, t: (b, t, 0))
    per = TM // HALO
    prev = pl.BlockSpec((None, HALO, D_MODEL), lambda b, t: (b, jnp.maximum(t * per - 1, 0), 0))
    nxt = pl.BlockSpec((None, HALO, D_MODEL), lambda b, t: (b, jnp.minimum((t + 1) * per, nt * per - 1), 0))
    mod = pl.BlockSpec((None, None, 1, D_MODEL), lambda b, t: (b, (t >= nct).astype(jnp.int32), 0, 0))
    return tile, prev, nxt, mod


def _proj0(h, scale, shift, norm_g, cos_t, sin_t, w, nct):
    bsz, ttot, _ = h.shape
    nt = ttot // TM
    nb = MLSTM_HEADS * MLSTM_DH
    tile, prev, nxt, mod = _tile_specs(nct, nt)
    tab = pl.BlockSpec((TM, LANES), lambda b, t: (t, 0))
    wnames = ["wcq", "wckv", "wkr", "wkrs", "wmqk", "wmv", "wmo", "wgt", "wz",
              "qn", "wuq", "wuqs", "kvn", "wk", "wv", "cw", "cb", "gb"]
    ws = [w[n] for n in wnames]
    tok = lambda width, dt: jax.ShapeDtypeStruct((bsz, ttot, width), dt)
    out_shape = (tok(MLA_HEADS * LANES, BF16), tok(MLA_HEADS * LANES, BF16), tok(MLA_HEADS * LANES, BF16),
                 tok(nb, BF16), tok(nb, BF16), tok(nb, BF16),
                 jax.ShapeDtypeStruct((bsz, ttot // CHUNK, 2 * MLSTM_HEADS, CHUNK), F32), tok(LANES, F32),
                 tok(nb, BF16), tok(D_MODEL, BF16))
    out_specs = (tile(MLA_HEADS * LANES), tile(MLA_HEADS * LANES), tile(MLA_HEADS * LANES),
                 tile(nb), tile(nb), tile(nb),
                 pl.BlockSpec((None, TM // CHUNK, 2 * MLSTM_HEADS, CHUNK), lambda b, t: (b, t, 0, 0)), tile(LANES),
                 tile(nb), tile(D_MODEL))
    return pl.pallas_call(
        functools.partial(_proj0_kernel, nct=nct, nt=nt),
        out_shape=out_shape,
        grid=(bsz, nt),
        in_specs=[tile(D_MODEL), prev, nxt, mod, mod, _full((1, D_MODEL)), tab, tab] + [_full(a.shape) for a in ws],
        out_specs=out_specs,
        scratch_shapes=[pltpu.VMEM((TM + 2 * HALO, 2 * nb), F32)],
        compiler_params=_params("parallel", "parallel"),
        name="proj0",
    )(h, h, h, scale, shift, norm_g[None], cos_t, sin_t, *ws)


def _mla_kernel(q_ref, k_ref, v_ref, o_ref, *, nct):
    t = pl.program_id(2)
    lane = lax.broadcasted_iota(jnp.int32, (TM, LANES), 1)

    def attend(nkeys):
        outs = []
        for j in range(MLA_HPS):
            hs = slice(j * LANES, (j + 1) * LANES)
            s = _dot_nt(q_ref[:, hs], k_ref[0:nkeys, hs])
            p = jnp.exp2(s - jnp.max(s, axis=-1, keepdims=True))
            o = _dot(p.astype(BF16), v_ref[0:nkeys, hs])
            outs.append(o / pltpu.roll(o, MLA_V, axis=1))
        for j in range(MLA_HPS // 2):
            o_ref[:, j * LANES:(j + 1) * LANES] = jnp.where(lane < MLA_V, outs[2 * j], outs[2 * j + 1]).astype(BF16)

    @pl.when(t < nct)
    def _():
        attend(nct * TM)

    @pl.when(t >= nct)
    def _():
        attend(k_ref.shape[0])


def _mla_attention(q, k, v, nct):
    bsz, ttot, _ = q.shape
    nt = ttot // TM
    hw = MLA_HPS * LANES
    return pl.pallas_call(
        functools.partial(_mla_kernel, nct=nct),
        out_shape=jax.ShapeDtypeStruct((bsz, ttot, MLA_HEADS * MLA_V), BF16),
        grid=(bsz, MLA_HEADS // MLA_HPS, nt),
        in_specs=[pl.BlockSpec((None, TM, hw), lambda b, p, t: (b, t, p)),
                  pl.BlockSpec((None, ttot, hw), lambda b, p, t: (b, 0, p)),
                  pl.BlockSpec((None, ttot, hw), lambda b, p, t: (b, 0, p))],
        out_specs=pl.BlockSpec((None, TM, hw // 2), lambda b, p, t: (b, t, p)),
        compiler_params=_params("parallel", "parallel", "parallel"),
        name="mla_attention",
    )(q, k, v)


def _scan_block(j, nct, nt):
    return jnp.where(j < nct, nct - 1 - j, nt - 1 - (j - nct))


def _mlstm_kernel(qf_ref, kf_ref, vf_ref, grf_ref, gcf_ref, qr_ref, kr_ref, vr_ref, grr_ref, gcr_ref,
                  hf_ref, hr_ref, c_scr, m_scr):
    @pl.when(pl.program_id(1) == 0)
    def _():
        c_scr[...] = jnp.zeros_like(c_scr)
        m_scr[...] = jnp.zeros_like(m_scr)

    L = CHUNK
    nchunk = TM // L
    nh = MLSTM_HEADS
    row = lax.broadcasted_iota(jnp.int32, (L, L), 0)
    col = lax.broadcasted_iota(jnp.int32, (L, L), 1)
    ones = jnp.ones((L, MLSTM_DH), BF16)
    wide = lambda a: jnp.broadcast_to(a, (L, MLSTM_DH))

    def chunk_step(ci, carry):
        chains = []
        for d, (q_ref, k_ref, v_ref, gr_ref, gc_ref, o_ref) in enumerate(
                ((qf_ref, kf_ref, vf_ref, grf_ref, gcf_ref, hf_ref), (qr_ref, kr_ref, vr_ref, grr_ref, gcr_ref, hr_ref))):
            cd = ci if d == 0 else nchunk - 1 - ci
            base = pl.multiple_of(cd * L, L)
            g_rows = gr_ref[cd]
            g_cols = gc_ref[pl.ds(base, L), :]
            last = L - 1 if d == 0 else 0
            for hd in range(nh):
                ch = d * nh + hd
                hs = slice(hd * MLSTM_DH, (hd + 1) * MLSTM_DH)
                chains.append(dict(
                    ch=ch, hs=hs, base=base, o_ref=o_ref, causal=(col <= row) if d == 0 else (col >= row),
                    c_row=g_rows[ch:ch + 1], c_col=g_cols[:, ch:ch + 1],
                    pm_col=g_cols[:, 2 * nh + ch:2 * nh + ch + 1], b_col=g_cols[:, 4 * nh + ch:4 * nh + ch + 1],
                    pm_last=g_cols[last:last + 1, 2 * nh + ch:2 * nh + ch + 1],
                    b_last=g_cols[last:last + 1, 4 * nh + ch:4 * nh + ch + 1],
                    q=q_ref[pl.ds(base, L), hs], k=k_ref[pl.ds(base, L), hs],
                    v_ext=jnp.concatenate([v_ref[pl.ds(base, L), hs], ones], axis=1),
                    state=c_scr[ch], m=m_scr[ch, 0:1, 0:1]))
        for c in chains:
            c["s"] = _dot_nt(c["q"], c["k"])
            c["qc"] = _dot(c["q"], c["state"].astype(BF16))
        for c in chains:
            g_t = jnp.maximum(c["m"], c["pm_col"])
            g_w = wide(g_t)
            c["w_intra"] = jnp.exp(jnp.where(c["causal"], c["c_row"] - g_w[:, :L], -jnp.inf))
            c["w_inter"] = jnp.exp(c["m"] - g_w)
            c["floor"] = jnp.exp(-wide(c["b_col"] + g_t))
            c["g_last"] = jnp.maximum(c["m"], c["pm_last"])
        for c in chains:
            c["kw"] = (jnp.exp(wide(c["c_col"] - c["g_last"])) * c["k"].astype(F32)).astype(BF16)
        for c in chains:
            p = (c["s"] * c["w_intra"]).astype(BF16)
            c["o"] = _dot(p, c["v_ext"]) + jnp.concatenate([c["w_inter"]] * 2, axis=1) * c["qc"]
            c["dc"] = _dot_tn(c["kw"], c["v_ext"])
        for c in chains:
            o = c["o"]
            h = o[:, :MLSTM_DH] / jnp.maximum(jnp.abs(o[:, MLSTM_DH:]), c["floor"])
            c["o_ref"][pl.ds(c["base"], L), c["hs"]] = h.astype(BF16)
            c_scr[c["ch"]] = jnp.exp(c["m"] - c["g_last"]) * c["state"] + c["dc"]
            m_scr[c["ch"]] = jnp.broadcast_to(c["b_last"] + c["g_last"], m_scr.shape[1:])
        return carry

    lax.fori_loop(0, nchunk, chunk_step, 0)


def _mlstm(mq, mk, mv, g_rows, g_cols, nct):
    bsz, ttot, nb = mq.shape
    nt = ttot // TM
    per = TM // CHUNK
    fwd = lambda w: pl.BlockSpec((None, TM, w), lambda b, j: (b, j, 0))
    rev = lambda w: pl.BlockSpec((None, TM, w), lambda b, j: (b, _scan_block(j, nct, nt), 0))
    gshape = (None, per, 2 * MLSTM_HEADS, CHUNK)
    gfwd = pl.BlockSpec(gshape, lambda b, j: (b, j, 0, 0))
    grev = pl.BlockSpec(gshape, lambda b, j: (b, _scan_block(j, nct, nt), 0, 0))
    return pl.pallas_call(
        _mlstm_kernel,
        out_shape=(jax.ShapeDtypeStruct((bsz, ttot, nb), BF16),) * 2,
        grid=(bsz, nt),
        in_specs=[fwd(nb), fwd(nb), fwd(nb), gfwd, fwd(LANES), rev(nb), rev(nb), rev(nb), grev, rev(LANES)],
        out_specs=(fwd(nb), rev(nb)),
        scratch_shapes=[pltpu.VMEM((2 * MLSTM_HEADS, MLSTM_DH, 2 * MLSTM_DH), F32),
                        pltpu.VMEM((2 * MLSTM_HEADS, SUBLANES, LANES), F32)],
        compiler_params=_params("parallel", "arbitrary"),
        name="mlstm_scan",
    )(mq, mk, mv, g_rows, g_cols, mq, mk, mv, g_rows, g_cols)


def _head_rms(x, n_heads):
    w = x.shape[1] // n_heads
    return jnp.concatenate([_rms(x[:, i * w:(i + 1) * w]) for i in range(n_heads)], axis=1)


def _out0_kernel(h_ref, gate_ref, a_ref, hf_ref, hr_ref, mo_ref, z_ref, hn_ref, w_ref, o_ref):
    f32 = lambda r: r[...].astype(F32)
    hm = _head_rms(f32(mo_ref) * (f32(hf_ref) + f32(hr_ref)), MLSTM_HEADS) * hn_ref[...]
    cat = (jnp.concatenate([f32(a_ref), hm], axis=1) * f32(z_ref)).astype(BF16)
    o_ref[...] = h_ref[...] + gate_ref[...] * _dot(cat, w_ref[...])


def _out0(h, gate, a, hf, hr, mo, z, h_norm, w_out, nct):
    bsz, ttot, _ = h.shape
    nt = ttot // TM
    tile, _, _, mod = _tile_specs(nct, nt)
    nb = MLSTM_HEADS * MLSTM_DH
    return pl.pallas_call(
        _out0_kernel,
        out_shape=jax.ShapeDtypeStruct(h.shape, F32),
        grid=(bsz, nt),
        in_specs=[tile(D_MODEL), mod, tile(MLA_HEADS * MLA_V), tile(nb), tile(nb), tile(nb), tile(D_MODEL),
                  _full((1, nb)), _full(w_out.shape)],
        out_specs=tile(D_MODEL),
        compiler_params=_params("parallel", "parallel"),
        name="out0",
    )(h, gate, a, hf, hr, mo, z, h_norm[None], w_out.astype(BF16))


def _proj1_kernel(h_ref, sc_ref, sh_ref, g_ref, wq_ref, wk_ref, wv_ref, wga_ref, wnq_ref, wnk_ref, wnv_ref, wz_ref,
                  wgate_ref, bgate_ref,
                  gq_out, gk_out, gv_out, bc_out, nq_out, nk_out, nv_out, z_out):
    ub = _modnorm(h_ref[...], g_ref[...], sc_ref[...], sh_ref[...]).astype(BF16)
    gq_out[...] = (_dot(ub, wq_ref[...]) * GLA_DK ** -0.5).astype(BF16)
    gk_out[...] = _dot(ub, wk_ref[...]).astype(BF16)
    gv_out[...] = _dot(ub, wv_ref[...]).astype(BF16)
    nq_out[...] = (_dot(ub, wnq_ref[...]) * NA_DH ** -0.5).astype(BF16)
    nk_out[...] = _dot(ub, wnk_ref[...]).astype(BF16)
    nv_out[...] = _dot(ub, wnv_ref[...]).astype(BF16)
    z_out[...] = _silu(_dot(ub, wz_ref[...])).astype(BF16)

    ga = _dot(ub, wga_ref[...]).astype(BF16)
    lg = _log_sigmoid(_dot(ga, wgate_ref[...]) + bgate_ref[...]) * (1.0 / GLA_TAU)
    hi, mid, lo = _split3(lg)
    row = lax.broadcasted_iota(jnp.int32, (TM, TM), 0)
    col = lax.broadcasted_iota(jnp.int32, (TM, TM), 1)
    same = (row // CHUNK) == (col // CHUNK)
    nk = GLA_HEADS * GLA_DK
    for d in range(2):
        tri = jnp.where(jnp.logical_and(same, (col <= row) if d == 0 else (col >= row)), 1.0, 0.0).astype(BF16)
        cols = slice(d * nk, (d + 1) * nk)
        bc_out[:, cols] = _dot(tri, hi[:, cols]) + _dot(tri, mid[:, cols]) + _dot(tri, lo[:, cols])


def _proj1_weights(w_in, w_gate, b_gate):
    nk = GLA_HEADS * GLA_DK
    nc = GLA_HEADS * GLA_DV
    nd = NA_HEADS * NA_DH
    o = np.cumsum([0, nk, nk, nc, 2 * GLA_GATE_RANK, nd, nd, nd, nc + nd])
    r = GLA_GATE_RANK
    wgate = jnp.zeros((2 * r, 2 * nk), F32).at[:r, :nk].set(w_gate[0]).at[r:, nk:].set(w_gate[1])
    bf = lambda a: a.astype(BF16)
    names = ["wq", "wk", "wv", "wga", "wnq", "wnk", "wnv", "wz"]
    w = {n: bf(w_in[:, o[i]:o[i + 1]]) for i, n in enumerate(names)}
    w["wgate"] = bf(wgate)
    w["bgate"] = b_gate.reshape(1, 2 * nk)
    return w


def _proj1(h, scale, shift, norm_g, w, nct):
    bsz, ttot, _ = h.shape
    nt = ttot // TM
    tile, _, _, mod = _tile_specs(nct, nt)
    nk = GLA_HEADS * GLA_DK
    nc = GLA_HEADS * GLA_DV
    nd = NA_HEADS * NA_DH
    wnames = ["wq", "wk", "wv", "wga", "wnq", "wnk", "wnv", "wz", "wgate", "bgate"]
    ws = [w[n] for n in wnames]
    tok = lambda width, dt: jax.ShapeDtypeStruct((bsz, ttot, width), dt)
    return pl.pallas_call(
        _proj1_kernel,
        out_shape=(tok(nk, BF16), tok(nk, BF16), tok(nc, BF16), tok(2 * nk, F32),
                   tok(nd, BF16), tok(nd, BF16), tok(nd, BF16), tok(nc + nd, BF16)),
        grid=(bsz, nt),
        in_specs=[tile(D_MODEL), mod, mod, _full((1, D_MODEL))] + [_full(a.shape) for a in ws],
        out_specs=(tile(nk), tile(nk), tile(nc), tile(2 * nk), tile(nd), tile(nd), tile(nd), tile(nc + nd)),
        compiler_params=_params("parallel", "parallel"),
        name="proj1",
    )(h, scale, shift, norm_g[None], *ws)


def _gla_kernel(qf_ref, kf_ref, vf_ref, bf_ref, qr_ref, kr_ref, vr_ref, br_ref, of_ref, or_ref, s_scr):
    @pl.when(pl.program_id(1) == 0)
    def _():
        s_scr[...] = jnp.zeros_like(s_scr)

    L = CHUNK
    nchunk = TM // L
    npair = GLA_HEADS // 2
    row = lax.broadcasted_iota(jnp.int32, (L, L), 0)
    col = lax.broadcasted_iota(jnp.int32, (L, L), 1)
    lane = lax.broadcasted_iota(jnp.int32, (L, LANES), 1)
    rowv = lax.broadcasted_iota(jnp.int32, (L, LANES), 0)
    head_lane = [lane < GLA_DK, lane >= GLA_DK]
    srow = lax.broadcasted_iota(jnp.int32, (LANES, 2 * GLA_DV), 0)
    scol = lax.broadcasted_iota(jnp.int32, (LANES, 2 * GLA_DV), 1)
    own_block = (srow < GLA_DK) == (scol < GLA_DV)
    e_row = lax.broadcasted_iota(jnp.int32, (LANES, LANES), 0)
    e_col = lax.broadcasted_iota(jnp.int32, (LANES, LANES), 1)
    leaf_idx = lax.broadcasted_iota(jnp.int32, (GLA_LEAF * GLA_LEAF, LANES), 0)
    leaf_s, leaf_t = leaf_idx // GLA_LEAF, leaf_idx % GLA_LEAF
    head_ones = jnp.where(own_block, 1.0, 0.0).astype(BF16)

    def chunk_step(ci, carry):
        for d, (q_ref, k_ref, v_ref, b_ref, o_ref) in enumerate(
                ((qf_ref, kf_ref, vf_ref, bf_ref, of_ref), (qr_ref, kr_ref, vr_ref, br_ref, or_ref))):
            cd = ci if d == 0 else nchunk - 1 - ci
            base = pl.multiple_of(cd * L, L)
            tau_r = row if d == 0 else L - 1 - row
            tau_c = col if d == 0 else L - 1 - col
            tau_v = rowv if d == 0 else L - 1 - rowv
            for p in range(npair):
                ls = slice(p * LANES, (p + 1) * LANES)
                q = q_ref[pl.ds(base, L), ls].astype(F32)
                k = k_ref[pl.ds(base, L), ls].astype(F32)
                bc = b_ref[pl.ds(base, L), ls]
                v = v_ref[pl.ds(base, L), p * 2 * GLA_DV:(p + 1) * 2 * GLA_DV]
                b_last = bc[L - 1:L] if d == 0 else bc[0:1]
                state = s_scr[d, p]

                o_pair = _dot((q * jnp.exp(bc)).astype(BF16), state.astype(BF16))

                att = [jnp.zeros((L, L), F32), jnp.zeros((L, L), F32)]
                bs = L // 2
                while bs >= GLA_LEAF:
                    later = (tau_v // bs) % 2 == 1
                    nblk = L // (2 * bs)
                    ref_b = None
                    for blk in range(nblk):
                        tau_ref = blk * 2 * bs + bs - 1
                        idx = tau_ref if d == 0 else L - 1 - tau_ref
                        r = jnp.broadcast_to(bc[idx:idx + 1], (L, LANES))
                        ref_b = r if ref_b is None else jnp.where(tau_v // (2 * bs) == blk, r, ref_b)
                    qs = jnp.where(later, q * jnp.exp(jnp.where(later, bc - ref_b, 0.0)), 0.0)
                    ks = jnp.where(later, 0.0, k * jnp.exp(jnp.where(later, 0.0, ref_b - bc))).astype(BF16)
                    pair_mask = (tau_r // (2 * bs)) == (tau_c // (2 * bs))
                    for hh in range(2):
                        a = _dot_nt(jnp.where(head_lane[hh], qs, 0.0).astype(BF16), ks)
                        att[hh] = att[hh] + jnp.where(pair_mask, a, 0.0)
                    bs //= 2
                o_off = jnp.concatenate(
                    [_dot(att[hh].astype(BF16), v[:, hh * GLA_DV:(hh + 1) * GLA_DV]) for hh in range(2)], axis=1)

                vf = v.astype(F32)
                leaf_out = []
                for blk in range(L // GLA_LEAF):
                    rs = slice(blk * GLA_LEAF, (blk + 1) * GLA_LEAF)
                    qb, bb, kb, vb = q[rs], bc[rs], k[rs], vf[rs]
                    rep = lambda a: jnp.concatenate(
                        [jnp.broadcast_to(a[s:s + 1], (GLA_LEAF, a.shape[1])) for s in range(GLA_LEAF)], axis=0)
                    til = lambda a: jnp.concatenate([a] * GLA_LEAF, axis=0)
                    ok = (leaf_t >= leaf_s) if d == 0 else (leaf_t <= leaf_s)
                    w = jnp.where(ok, til(qb) * jnp.exp(jnp.where(ok, til(bb) - rep(bb), 0.0)) * rep(kb), 0.0)
                    a = _dot(w.astype(BF16), head_ones) * rep(vb)
                    acc = a[0:GLA_LEAF]
                    for s in range(1, GLA_LEAF):
                        acc = acc + a[s * GLA_LEAF:(s + 1) * GLA_LEAF]
                    leaf_out.append(acc)
                o_ref[pl.ds(base, L), p * 2 * GLA_DV:(p + 1) * 2 * GLA_DV] = (
                    o_pair + o_off + jnp.concatenate(leaf_out, axis=0)).astype(BF16)

                ke = (k * jnp.exp(b_last - bc)).astype(BF16)
                decay_col = jnp.sum(jnp.where(e_row == e_col, jnp.broadcast_to(jnp.exp(b_last), (LANES, LANES)), 0.0),
                                    axis=1, keepdims=True)
                s_scr[d, p] = decay_col * state + jnp.where(own_block, _dot_tn(ke, v), 0.0)
        return carry

    lax.fori_loop(0, nchunk, chunk_step, 0)


def _gla(gq, gk, gv, bc, nct):
    bsz, ttot, nk = gq.shape
    nc = gv.shape[2]
    nt = ttot // TM
    fwd = lambda w: pl.BlockSpec((None, TM, w), lambda b, j: (b, j, 0))
    rev = lambda w: pl.BlockSpec((None, TM, w), lambda b, j: (b, _scan_block(j, nct, nt), 0))
    bfwd = pl.BlockSpec((None, TM, nk), lambda b, j: (b, j, 0))
    brev = pl.BlockSpec((None, TM, nk), lambda b, j: (b, _scan_block(j, nct, nt), 1))
    return pl.pallas_call(
        _gla_kernel,
        out_shape=(jax.ShapeDtypeStruct((bsz, ttot, nc), BF16),) * 2,
        grid=(bsz, nt),
        in_specs=[fwd(nk), fwd(nk), fwd(nc), bfwd, rev(nk), rev(nk), rev(nc), brev],
        out_specs=(fwd(nc), rev(nc)),
        scratch_shapes=[pltpu.VMEM((2, GLA_HEADS // 2, LANES, 2 * GLA_DV), F32)],
        compiler_params=_params("parallel", "arbitrary"),
        name="gla_scan",
    )(gq, gk, gv, bc, gq, gk, gv, bc)


def _na_bias_tables(rpb, rows):
    kh = NA_KH
    c = np.arange(GRID_W)
    cs = np.clip(c - NA_KW // 2, 0, GRID_W - NA_KW)
    col_ok = (c[None, :] >= cs[:, None]) & (c[None, :] < cs[:, None] + NA_KW)
    pad = GRID_W - NA_KW
    rpb_pad = jnp.pad(rpb.astype(F32), ((0, 0), (0, 0), (pad, pad)))
    t1 = jnp.stack([rpb_pad[:, :, GRID_W - 1 - cq:2 * GRID_W - 1 - cq] for cq in range(GRID_W)], axis=2)
    t1 = jnp.where(col_ok, t1, NEG_BIG)
    tabs = []
    for r0 in (0, NA_QROWS, rows - NA_QROWS):
        kb = int(np.clip(r0 - kh // 2, 0, rows - NA_KROWS))
        qr = r0 + np.arange(NA_QROWS)
        ws = np.clip(qr - kh // 2, 0, rows - kh)
        krow = kb + np.arange(NA_KROWS)
        row_ok = (krow[None, :] >= ws[:, None]) & (krow[None, :] < ws[:, None] + kh)
        drow = np.clip(krow[None, :] - qr[:, None] + (NA_KH - 1), 0, 2 * NA_KH - 2)
        tv = jnp.take(t1, jnp.asarray(drow.reshape(-1)), axis=1)
        tv = jnp.where(row_ok.reshape(1, -1, 1, 1), tv, NEG_BIG)
        tv = tv.reshape(NA_HEADS, NA_QROWS, NA_KROWS, GRID_W, GRID_W).transpose(0, 1, 3, 2, 4)
        tabs.append(tv.reshape(NA_HEADS, NA_QROWS * GRID_W, NA_KROWS * GRID_W))
    return jnp.stack(tabs)


def _na_kernel(q_ref, k_ref, v_ref, bias_ref, o_ref, *, n_ctx, rows):
    rb = pl.program_id(2)
    nq = NA_QROWS * GRID_W
    nk = NA_KROWS * GRID_W
    kb = jnp.clip(rb * NA_QROWS - NA_KH // 2, 0, rows - NA_KROWS)
    start = pl.multiple_of(n_ctx + kb * GRID_W, GRID_W)
    q = q_ref[...]
    k_loc = k_ref[pl.ds(start, nk), :]
    v_loc = v_ref[pl.ds(start, nk), :]
    k_ctx = k_ref[0:n_ctx, :]
    v_ctx = v_ref[0:n_ctx, :]
    lane = lax.broadcasted_iota(jnp.int32, (nq, LANES), 1)
    outs = []
    for j in range(2):
        mine = (lane < NA_DH) if j == 0 else (lane >= NA_DH)
        qj = jnp.where(mine, q, jnp.zeros_like(q))
        s_loc = _dot_nt(qj, k_loc) + bias_ref[j]
        s_ctx = _dot_nt(qj, k_ctx)
        m = jnp.maximum(jnp.max(s_loc, axis=-1, keepdims=True), jnp.max(s_ctx, axis=-1, keepdims=True))
        p_loc = jnp.exp(s_loc - m)
        p_ctx = jnp.exp(s_ctx - m)
        denom = jnp.sum(p_loc, axis=-1, keepdims=True) + jnp.sum(p_ctx, axis=-1, keepdims=True)
        outs.append((_dot(p_loc.astype(BF16), v_loc) + _dot(p_ctx.astype(BF16), v_ctx)) / denom)
    o_ref[...] = jnp.where(lane < NA_DH, outs[0], outs[1]).astype(BF16)


def _na_attention(nq, nk, nv, bias, n_ctx, rows):
    bsz, ttot, _ = nq.shape
    nrb = rows // NA_QROWS
    qrows = NA_QROWS * GRID_W
    krows = NA_KROWS * GRID_W
    qoff = n_ctx // qrows

    def variant(rb):
        return jnp.where(rb == 0, 0, jnp.where(rb == nrb - 1, 2, 1))

    return pl.pallas_call(
        functools.partial(_na_kernel, n_ctx=n_ctx, rows=rows),
        out_shape=jax.ShapeDtypeStruct((bsz, rows * GRID_W, NA_HEADS * NA_DH), BF16),
        grid=(bsz, NA_HEADS // 2, nrb),
        in_specs=[pl.BlockSpec((None, qrows, LANES), lambda b, p, rb: (b, qoff + rb, p)),
                  pl.BlockSpec((None, ttot, LANES), lambda b, p, rb: (b, 0, p)),
                  pl.BlockSpec((None, ttot, LANES), lambda b, p, rb: (b, 0, p)),
                  pl.BlockSpec((None, 2, qrows, krows), lambda b, p, rb: (variant(rb), p, 0, 0))],
        out_specs=pl.BlockSpec((None, qrows, LANES), lambda b, p, rb: (b, rb, p)),
        compiler_params=_params("parallel", "parallel", "parallel"),
        name="neighbourhood_attention",
    )(nq, nk, nv, bias)


def _out1_kernel(h_ref, gate_ref, of_ref, or_ref, na_ref, z_ref, gn_ref, w_ref, fn_ref, o_ref):
    f32 = lambda r: r[...].astype(F32)
    g = _head_rms(f32(of_ref) + f32(or_ref), GLA_HEADS) * gn_ref[...]
    cat = (jnp.concatenate([g, f32(na_ref)], axis=1) * f32(z_ref)).astype(BF16)
    h = h_ref[...] + gate_ref[...] * _dot(cat, w_ref[...])
    o_ref[...] = _rms(h) * fn_ref[...]


def _out1(h, gate, of, orv, na, z, gla_norm, w_out, final_norm, nct):
    bsz, ttot, _ = h.shape
    seq = na.shape[1]
    nc = GLA_HEADS * GLA_DV
    nd = NA_HEADS * NA_DH
    lat = lambda w: pl.BlockSpec((None, TM, w), lambda b, t: (b, t + nct, 0))
    return pl.pallas_call(
        _out1_kernel,
        out_shape=jax.ShapeDtypeStruct((bsz, seq, D_MODEL), F32),
        grid=(bsz, seq // TM),
        in_specs=[lat(D_MODEL), pl.BlockSpec((None, None, 1, D_MODEL), lambda b, t: (b, 1, 0, 0)),
                  lat(nc), lat(nc), pl.BlockSpec((None, TM, nd), lambda b, t: (b, t, 0)), lat(nc + nd),
                  _full((1, nc)), _full(w_out.shape), _full((1, D_MODEL))],
        out_specs=pl.BlockSpec((None, TM, D_MODEL), lambda b, t: (b, t, 0)),
        compiler_params=_params("parallel", "parallel"),
        name="out1",
    )(h, gate, of, orv, na, z, gla_norm[None], w_out.astype(BF16), final_norm[None])


def kernel(x, c, ctx, c_ctx, l0_norm, l0_w_mod, l0_b_mod, l0_w_in, l0_mla_q_norm, l0_mla_w_uq, l0_mla_kv_norm, l0_mla_w_ukv, l0_mlstm_conv_w, l0_mlstm_conv_b, l0_mlstm_b_i, l0_mlstm_b_f, l0_mlstm_norm, l0_w_out, l1_norm, l1_w_mod, l1_b_mod, l1_w_in, l1_gla_w_gate, l1_gla_b_gate, l1_gla_norm, l1_na_rpb, l1_w_out, final_norm):
    bsz, seq, d = x.shape
    n_ctx = ctx.shape[1]
    rows = seq // GRID_W
    assert d == D_MODEL and seq % TM == 0 and n_ctx % TM == 0
    assert rows % NA_QROWS == 0 and rows >= NA_KROWS and rows // NA_QROWS >= 3
    nct = n_ctx // TM

    h = jnp.concatenate([ctx, x], axis=1)

    shift, scale, gate = _mod_vectors(c, c_ctx, l0_w_mod, l0_b_mod)
    cos_t, sin_t = _rope_tables(n_ctx, seq)
    w0 = _proj0_weights(l0_w_in, l0_mla_q_norm, l0_mla_w_uq, l0_mla_kv_norm, l0_mla_w_ukv,
                        l0_mlstm_conv_w, l0_mlstm_conv_b, l0_mlstm_b_i, l0_mlstm_b_f)
    q, k, v, mq, mk, mv, g_rows, g_cols, mo, z = _proj0(h, scale, shift, l0_norm, cos_t, sin_t, w0, nct)
    a = _mla_attention(q, k, v, nct)
    hf, hr = _mlstm(mq, mk, mv, g_rows, g_cols, nct)
    h = _out0(h, gate, a, hf, hr, mo, z, l0_mlstm_norm, l0_w_out, nct)

    shift, scale, gate = _mod_vectors(c, c_ctx, l1_w_mod, l1_b_mod)
    w1 = _proj1_weights(l1_w_in, l1_gla_w_gate, l1_gla_b_gate)
    gq, gk, gv, bc, nq, nk, nv, z = _proj1(h, scale, shift, l1_norm, w1, nct)
    of, orv = _gla(gq, gk, gv, bc, nct)
    na = _na_attention(nq, nk, nv, _na_bias_tables(l1_na_rpb, rows), n_ctx, rows)
    return _out1(h, gate, of, orv, na, z, l1_gla_norm, l1_w_out, final_norm, nct)
```

```python
import functools

import jax
import jax.numpy as jnp
import numpy as np
from jax import lax
from jax.experimental import pallas as pl
from jax.experimental.pallas import tpu as pltpu

F32 = jnp.float32
BF16 = jnp.bfloat16

D_MODEL = 1024
GRID_W = 64
EPS = 1e-6
ROPE_BASE = 10000.0

MLA_HEADS = 8
MLA_Q_RANK = 384
MLA_KV_RANK = 256
MLA_NOPE = 64
MLA_ROPE = 32
MLA_V = 64
ROPE_PAIRS = MLA_ROPE // 4
MLSTM_HEADS = 4
MLSTM_DH = 128
MLSTM_CONV = 3
GLA_HEADS = 4
GLA_DK = 64
GLA_DV = 128
GLA_GATE_RANK = 16
GLA_TAU = 16.0
NA_HEADS = 8
NA_DH = 64
NA_KH = 8
NA_KW = 16

CHUNK = 64
TM = 256
LANES = 128
SUBLANES = 8
HALO = SUBLANES
NA_QROWS = 4
NA_KROWS = NA_QROWS + NA_KH - 1
GLA_LEAF = 16
MLA_HPS = 4
NA_HPS = 4
VMEM_LIMIT = 56 * 1024 * 1024
NEG_BIG = -1e30
LOG2E = 1.4426950408889634


def _dot(a, b):
    return jnp.dot(a, b, preferred_element_type=F32)


def _dot_nt(a, b):
    return lax.dot_general(a, b, (((1,), (1,)), ((), ())), preferred_element_type=F32)


def _dot_tn(a, b):
    return lax.dot_general(a, b, (((0,), (0,)), ((), ())), preferred_element_type=F32)


def _rms(x):
    return x * lax.rsqrt(jnp.mean(x * x, axis=-1, keepdims=True) + EPS)


def _sigmoid(x):
    return 1.0 / (1.0 + jnp.exp(-x))


def _silu(x):
    return x * _sigmoid(x)


def _log_sigmoid(x):
    return jnp.minimum(x, 0.0) - jnp.log(1.0 + jnp.exp(-jnp.abs(x)))


def _params(*sem):
    return pltpu.CompilerParams(dimension_semantics=sem, vmem_limit_bytes=VMEM_LIMIT)


def _full(shape):
    nd = len(shape)
    return pl.BlockSpec(shape, lambda *_: (0,) * nd)


def _mod_kernel(c_ref, w_ref, b_ref, o_ref):
    o_ref[...] = _dot(_silu(c_ref[...]).astype(BF16), w_ref[...]) + b_ref[...]


def _modulation(cc, w_mod, b_mod):
    rows, d = cc.shape
    n = w_mod.shape[1]
    return pl.pallas_call(
        _mod_kernel,
        out_shape=jax.ShapeDtypeStruct((rows, n), F32),
        grid=(n // d,),
        in_specs=[_full((rows, d)), pl.BlockSpec((d, d), lambda j: (0, j)), pl.BlockSpec((1, d), lambda j: (0, j))],
        out_specs=pl.BlockSpec((rows, d), lambda j: (0, j)),
        compiler_params=_params("parallel"),
        name="modulation",
    )(cc, w_mod.astype(BF16), b_mod[None])


def _mod_vectors(c, c_ctx, w_mod, b_mod):
    bsz = c.shape[0]
    rows = -(-(bsz + 1) // SUBLANES) * SUBLANES
    cc = jnp.zeros((rows, D_MODEL), F32).at[:bsz].set(c).at[bsz].set(c_ctx)
    mod = _modulation(cc, w_mod, b_mod)
    out = []
    for part in jnp.split(mod, 3, axis=-1):
        ctx_v = jnp.broadcast_to(part[bsz][None], (bsz, D_MODEL))
        out.append(jnp.stack([ctx_v, part[:bsz]], axis=1)[:, :, None, :])
    return out


def _modnorm(x, g, scale, shift):
    return _rms(x) * g * (1.0 + scale) + shift


def _seg_scan(x, reverse, op, fill):
    width = x.shape[1]
    pos = lax.broadcasted_iota(jnp.int32, x.shape, 1) % CHUNK
    k = 1
    while k < CHUNK:
        if reverse:
            x = op(x, jnp.where(pos < CHUNK - k, pltpu.roll(x, width - k, axis=1), fill))
        else:
            x = op(x, jnp.where(pos >= k, pltpu.roll(x, k, axis=1), fill))
        k *= 2
    return x


def _split3(x):
    hi = x.astype(BF16)
    r1 = x - hi.astype(F32)
    mid = r1.astype(BF16)
    return hi, mid, (r1 - mid.astype(F32)).astype(BF16)


def _proj0_kernel(h_ref, hp_ref, hn_ref, sc_ref, sh_ref, g_ref, cos_ref, sin_ref,
                  wcq_ref, wckv_ref, wkr_ref, wkrs_ref, wmqk_ref, wmv_ref, wmo_ref, wgt_ref, wz_ref,
                  qn_ref, wuq_ref, wuqs_ref, kvn_ref, wk_ref, wv_ref, cw_ref, cb_ref, gb_ref,
                  q_out, k_out, v_out, mq_out, mk_out, mv_out, gr_out, gc_out, mo_out, z_out,
                  pbuf, *, nct, nt):
    t = pl.program_id(1)
    x = jnp.concatenate([hp_ref[...], h_ref[...], hn_ref[...]], axis=0)
    ub_ext = _modnorm(x, g_ref[...], sc_ref[...], sh_ref[...]).astype(BF16)
    ub = ub_ext[HALO:HALO + TM]

    gt = _dot_nt(wgt_ref[...], ub) + gb_ref[...]
    half = MLSTM_HEADS
    per_dir = []
    for d in range(2):
        gd = gt[d * SUBLANES:(d + 1) * SUBLANES]
        bcum = _seg_scan(_log_sigmoid(gd), d == 1, jnp.add, 0.0)
        b_top = pltpu.roll(bcum, half, axis=0)
        c8 = gd - b_top
        pm8 = _seg_scan(c8, d == 1, jnp.maximum, -jnp.inf)
        per_dir.append((c8, pm8, b_top))

    cos = cos_ref[...]
    sin = sin_ref[...]
    cos_all = jnp.concatenate([cos] * MLA_HEADS, axis=1)
    sin_all = jnp.concatenate([sin] * MLA_HEADS, axis=1)

    nq = (_rms(_dot(ub, wcq_ref[...])) * qn_ref[...]).astype(BF16)
    q = _dot(nq, wuq_ref[...]) * cos_all + _dot(nq, wuqs_ref[...]) * sin_all
    q_out[...] = (q * ((MLA_NOPE + MLA_ROPE) ** -0.5 * LOG2E)).astype(BF16)

    nkv = (_rms(_dot(ub, wckv_ref[...])) * kvn_ref[...]).astype(BF16)
    rot = _dot(ub, wkr_ref[...]) * cos + _dot(ub, wkrs_ref[...]) * sin
    k_out[...] = (_dot(nkv, wk_ref[...]) + jnp.concatenate([rot] * MLA_HEADS, axis=1)).astype(BF16)
    vlane = lax.broadcasted_iota(jnp.int32, (TM, MLA_HEADS * LANES), 1)
    is_value = ((vlane % LANES) < MLA_V) == ((vlane // LANES) % 2 == 0)
    v_out[...] = jnp.where(is_value, _dot(nkv, wv_ref[...]), 1.0).astype(BF16)

    pqk = _dot(ub_ext, wmqk_ref[...])
    prev_ok = jnp.logical_and(t != 0, t != nct)
    next_ok = jnp.logical_and(t != nct - 1, t != nt - 1)
    row = lax.broadcasted_iota(jnp.int32, pqk.shape, 0)
    keep = jnp.logical_and(jnp.logical_or(row >= HALO, prev_ok), jnp.logical_or(row < HALO + TM, next_ok))
    pbuf[...] = jnp.where(keep, pqk, 0.0)
    cw = cw_ref[...]
    y = (pbuf[HALO - 1:HALO - 1 + TM, :] * cw[0:1] + pbuf[HALO:HALO + TM, :] * cw[1:2]
         + pbuf[HALO + 1:HALO + 1 + TM, :] * cw[2:3] + cb_ref[...])
    y = _silu(y)
    nb = MLSTM_HEADS * MLSTM_DH
    mq_out[...] = (y[:, :nb] * MLSTM_DH ** -0.5).astype(BF16)
    mk_out[...] = y[:, nb:].astype(BF16)
    mv_out[...] = _dot(ub, wmv_ref[...]).astype(BF16)
    mo_out[...] = _sigmoid(_dot(ub, wmo_ref[...])).astype(BF16)
    z_out[...] = _silu(_dot(ub, wz_ref[...])).astype(BF16)

    r8 =lax.broadcasted_iota(jnp.int32, (SUBLANES, TM), 0)
    both = lambda i: jnp.where(r8 < half, per_dir[0][i], pltpu.roll(per_dir[1][i], half, axis=0)) * LOG2E
    c_rows, pm_rows, b_rows = both(0), both(1), both(2)
    for ci in range(TM // CHUNK):
        gr_out[ci] = c_rows[:, ci * CHUNK:(ci + 1) * CHUNK]
    stack = jnp.concatenate([c_rows, pm_rows, b_rows, jnp.zeros((LANES - 3 * SUBLANES, TM), F32)], axis=0)
    er = lax.broadcasted_iota(jnp.int32, (TM, TM), 0)
    ec = lax.broadcasted_iota(jnp.int32, (TM, TM), 1)
    eye = jnp.where(er == ec, 1.0, 0.0).astype(BF16)
    gc_out[...] = sum(_dot_nt(eye, part) for part in _split3(stack))


def _rope_tables(n_ctx, seq):
    t = jnp.arange(seq)
    inv = 1.0 / (ROPE_BASE ** (jnp.arange(ROPE_PAIRS, dtype=F32) / ROPE_PAIRS))
    ang = jnp.concatenate([(t // GRID_W)[:, None] * inv, (t % GRID_W)[:, None] * inv], axis=-1)
    cos, sin = jnp.cos(ang), jnp.sin(ang)
    j = np.arange(MLA_ROPE)
    src = (j // (2 * ROPE_PAIRS)) * ROPE_PAIRS + (j % ROPE_PAIRS)
    sign = np.where((j % (2 * ROPE_PAIRS)) < ROPE_PAIRS, -1.0, 1.0).astype(np.float32)
    cos_full = cos[:, src]
    sin_full = sin[:, src] * sign
    cos_t = jnp.ones((n_ctx + seq, LANES), F32).at[n_ctx:, MLA_NOPE:MLA_NOPE + MLA_ROPE].set(cos_full)
    sin_t = jnp.zeros((n_ctx + seq, LANES), F32).at[n_ctx:, MLA_NOPE:MLA_NOPE + MLA_ROPE].set(sin_full)
    return cos_t, sin_t


def _rope_partner():
    j = np.arange(MLA_ROPE)
    return np.where((j % (2 * ROPE_PAIRS)) < ROPE_PAIRS, j + ROPE_PAIRS, j - ROPE_PAIRS)


def _proj0_weights(w_in, q_norm, w_uq, kv_norm, w_ukv, conv_w, conv_b, b_i, b_f):
    o = np.cumsum([0, MLA_Q_RANK, MLA_KV_RANK, MLA_ROPE] + [MLSTM_HEADS * MLSTM_DH] * 4 + [4 * MLSTM_HEADS, D_MODEL])
    part = _rope_partner()
    w_kr = w_in[:, o[2]:o[3]]
    pad_kr = jnp.zeros((D_MODEL, LANES), F32).at[:, MLA_NOPE:MLA_NOPE + MLA_ROPE]
    dq = MLA_NOPE + MLA_ROPE
    wuq = jnp.zeros((MLA_Q_RANK, MLA_HEADS, LANES), F32).at[:, :, :dq].set(w_uq.reshape(MLA_Q_RANK, MLA_HEADS, dq))
    wuq_sw = jnp.zeros((MLA_Q_RANK, MLA_HEADS, LANES), F32).at[:, :, MLA_NOPE:dq].set(
        w_uq.reshape(MLA_Q_RANK, MLA_HEADS, dq)[:, :, MLA_NOPE + part])
    wkv = w_ukv.reshape(MLA_KV_RANK, MLA_HEADS, MLA_NOPE + MLA_V)
    wk = jnp.zeros((MLA_KV_RANK, MLA_HEADS, LANES), F32).at[:, :, :MLA_NOPE].set(wkv[:, :, :MLA_NOPE])
    zv = jnp.zeros((MLA_KV_RANK, MLA_HEADS // 2, MLA_V), F32)
    wv = jnp.concatenate([wkv[:, 0::2, MLA_NOPE:], zv, zv, wkv[:, 1::2, MLA_NOPE:]], axis=-1)
    wv = wv.reshape(MLA_KV_RANK, MLA_HEADS * LANES)
    gbias = jnp.stack([b_i, b_f], axis=1).reshape(4 * MLSTM_HEADS, 1)
    bf = lambda a: a.astype(BF16)
    return dict(
        wcq=bf(w_in[:, o[0]:o[1]]), wckv=bf(w_in[:, o[1]:o[2]]),
        wkr=bf(pad_kr.set(w_kr)), wkrs=bf(pad_kr.set(w_kr[:, part])),
        wmqk=bf(w_in[:, o[3]:o[5]]), wmv=bf(w_in[:, o[5]:o[6]]), wmo=bf(w_in[:, o[6]:o[7]]),
        wgt=bf(w_in[:, o[7]:o[8]].T), wz=bf(w_in[:, o[8]:o[9]]),
        qn=q_norm[None], wuq=bf(wuq.reshape(MLA_Q_RANK, -1)), wuqs=bf(wuq_sw.reshape(MLA_Q_RANK, -1)),
        kvn=kv_norm[None], wk=bf(wk.reshape(MLA_KV_RANK, -1)), wv=bf(wv),
        cw=jnp.zeros((SUBLANES, conv_w.shape[1]), F32).at[:MLSTM_CONV].set(conv_w), cb=conv_b[None], gb=gbias)


def _tile_specs(nct, nt):
    tile = lambda w: pl.BlockSpec((None, TM, w), lambda b, t: (b, t, 0))
    per = TM // HALO
    prev = pl.BlockSpec((None, HALO, D_MODEL), lambda b, t: (b, jnp.maximum(t * per - 1, 0), 0))
    nxt = pl.BlockSpec((None, HALO, D_MODEL), lambda b, t: (b, jnp.minimum((t + 1) * per, nt * per - 1), 0))
    mod = pl.BlockSpec((None, None, 1, D_MODEL), lambda b, t: (b, (t >= nct).astype(jnp.int32), 0, 0))
    return tile, prev, nxt, mod


def _proj0(h, scale, shift, norm_g, cos_t, sin_t, w, nct):
    bsz, ttot, _ = h.shape
    nt = ttot // TM
    nb = MLSTM_HEADS * MLSTM_DH
    tile, prev, nxt, mod = _tile_specs(nct, nt)
    tab = pl.BlockSpec((TM, LANES), lambda b, t: (t, 0))
    wnames = ["wcq", "wckv", "wkr", "wkrs", "wmqk", "wmv", "wmo", "wgt", "wz",
              "qn", "wuq", "wuqs", "kvn", "wk", "wv", "cw", "cb", "gb"]
    ws = [w[n] for n in wnames]
    tok = lambda width, dt: jax.ShapeDtypeStruct((bsz, ttot, width), dt)
    out_shape = (tok(MLA_HEADS * LANES, BF16), tok(MLA_HEADS * LANES, BF16), tok(MLA_HEADS * LANES, BF16),
                 tok(nb, BF16), tok(nb, BF16), tok(nb, BF16),
                 jax.ShapeDtypeStruct((bsz, ttot // CHUNK, 2 * MLSTM_HEADS, CHUNK), F32), tok(LANES, F32),
                 tok(nb, BF16), tok(D_MODEL, BF16))
    out_specs = (tile(MLA_HEADS * LANES), tile(MLA_HEADS * LANES), tile(MLA_HEADS * LANES),
                 tile(nb), tile(nb), tile(nb),
                 pl.BlockSpec((None, TM // CHUNK, 2 * MLSTM_HEADS, CHUNK), lambda b, t: (b, t, 0, 0)), tile(LANES),
                 tile(nb), tile(D_MODEL))
    return pl.pallas_call(
        functools.partial(_proj0_kernel, nct=nct, nt=nt),
        out_shape=out_shape,
        grid=(bsz, nt),
        in_specs=[tile(D_MODEL), prev, nxt, mod, mod, _full((1, D_MODEL)), tab, tab] + [_full(a.shape) for a in ws],
        out_specs=out_specs,
        scratch_shapes=[pltpu.VMEM((TM + 2 * HALO, 2 * nb), F32)],
        compiler_params=_params("parallel", "parallel"),
        name="proj0",
    )(h, h, h, scale, shift, norm_g[None], cos_t, sin_t, *ws)


def _mla_kernel(q_ref, k_ref, v_ref, o_ref, *, nct):
    t = pl.program_id(2)
    lane = lax.broadcasted_iota(jnp.int32, (TM, LANES), 1)

    def attend(nkeys):
        outs = []
        for j in range(MLA_HPS):
            hs = slice(j * LANES, (j + 1) * LANES)
            s = _dot_nt(q_ref[:, hs], k_ref[0:nkeys, hs])
            p = jnp.exp2(s - jnp.max(s, axis=-1, keepdims=True))
            o = _dot(p.astype(BF16), v_ref[0:nkeys, hs])
            outs.append(o / pltpu.roll(o, MLA_V, axis=1))
        for j in range(MLA_HPS // 2):
            o_ref[:, j * LANES:(j + 1) * LANES] = jnp.where(lane < MLA_V, outs[2 * j], outs[2 * j + 1]).astype(BF16)

    @pl.when(t < nct)
    def _():
        attend(nct * TM)

    @pl.when(t >= nct)
    def _():
        attend(k_ref.shape[0])


def _mla_attention(q, k, v, nct):
    bsz, ttot, _ = q.shape
    nt = ttot // TM
    hw = MLA_HPS * LANES
    return pl.pallas_call(
        functools.partial(_mla_kernel, nct=nct),
        out_shape=jax.ShapeDtypeStruct((bsz, ttot, MLA_HEADS * MLA_V), BF16),
        grid=(bsz, MLA_HEADS // MLA_HPS, nt),
        in_specs=[pl.BlockSpec((None, TM, hw), lambda b, p, t: (b, t, p)),
                  pl.BlockSpec((None, ttot, hw), lambda b, p, t: (b, 0, p)),
                  pl.BlockSpec((None, ttot, hw), lambda b, p, t: (b, 0, p))],
        out_specs=pl.BlockSpec((None, TM, hw // 2), lambda b, p, t: (b, t, p)),
        compiler_params=_params("parallel", "parallel", "parallel"),
        name="mla_attention",
    )(q, k, v)


def _scan_block(j, nct, nt):
    return jnp.where(j < nct, nct - 1 - j, nt - 1 - (j - nct))


def _mlstm_kernel(qf_ref, kf_ref, vf_ref, grf_ref, gcf_ref, qr_ref, kr_ref, vr_ref, grr_ref, gcr_ref,
                  hf_ref, hr_ref, c_scr, m_scr):
    @pl.when(pl.program_id(1) == 0)
    def _():
        c_scr[...] = jnp.zeros_like(c_scr)
        m_scr[...] = jnp.zeros_like(m_scr)

    L = CHUNK
    nchunk = TM // L
    nh = MLSTM_HEADS
    nchain = 2 * nh
    row = lax.broadcasted_iota(jnp.int32, (L, L), 0)
    col = lax.broadcasted_iota(jnp.int32, (L, L), 1)
    ones = jnp.ones((L, MLSTM_DH), BF16)
    wide = lambda a: jnp.broadcast_to(a, (L, MLSTM_DH))
    twice = lambda a: jnp.concatenate([a, a], axis=1)
    refs = ((qf_ref, kf_ref, vf_ref, grf_ref, gcf_ref, hf_ref), (qr_ref, kr_ref, vr_ref, grr_ref, gcr_ref, hr_ref))

    items = {}
    for ci in range(nchunk):
        for d, (q_ref, k_ref, v_ref, gr_ref, gc_ref, o_ref) in enumerate(refs):
            cd = ci if d == 0 else nchunk - 1 - ci
            rs = slice(cd * L, (cd + 1) * L)
            g_rows = gr_ref[cd]
            g_cols = gc_ref[rs, :]
            last = L - 1 if d == 0 else 0
            for hd in range(nh):
                ch = d * nh + hd
                hs = slice(hd * MLSTM_DH, (hd + 1) * MLSTM_DH)
                items[ci, ch] = dict(
                    rs=rs, hs=hs, o_ref=o_ref, causal=(col <= row) if d == 0 else (col >= row),
                    c_row=g_rows[ch:ch + 1], c_col=g_cols[:, ch:ch + 1],
                    pm_col=g_cols[:, nchain + ch:nchain + ch + 1], b_col=g_cols[:, 2 * nchain + ch:2 * nchain + ch + 1],
                    pm_last=g_cols[last:last + 1, nchain + ch:nchain + ch + 1],
                    b_last=g_cols[last:last + 1, 2 * nchain + ch:2 * nchain + ch + 1],
                    q=q_ref[rs, hs], k=k_ref[rs, hs],
                    v_ext=jnp.concatenate([v_ref[rs, hs], ones], axis=1))
    order = [items[ci, ch] for ci in range(nchunk) for ch in range(nchain)]
    for c in order:
        c["s"] = _dot_nt(c["q"], c["k"])
    for c in order:
        c["pm_w"] = wide(c["pm_col"])
        c["b_w"] = wide(c["b_col"])
        c["w"] = jnp.exp2(jnp.where(c["causal"], c["c_row"] - c["pm_w"][:, :L], -jnp.inf))
        c["kwf"] = jnp.exp2(wide(c["c_col"] - c["pm_last"]))
    for c in order:
        c["p"] = (c["s"] * c["w"]).astype(BF16)
        c["kw"] = (c["kwf"] * c["k"].astype(F32)).astype(BF16)
    for c in order:
        c["o1"] = _dot(c["p"], c["v_ext"])
        c["dc"] = _dot_tn(c["kw"], c["v_ext"])

    state = [c_scr[ch] for ch in range(nchain)]
    m = [m_scr[ch, 0:1, 0:1] for ch in range(nchain)]
    for ci in range(nchunk):
        cs = [items[ci, ch] for ch in range(nchain)]
        for ch, c in enumerate(cs):
            c["qc"] = _dot(c["q"], state[ch].astype(BF16))
        for ch, c in enumerate(cs):
            g_w = jnp.maximum(m[ch], c["pm_w"])
            c["e_intra"] = jnp.exp2(c["pm_w"] - g_w)
            c["e_inter"] = jnp.exp2(m[ch] - g_w)
            c["floor"] = jnp.exp2(-(c["b_w"] + g_w))
            g_last = jnp.maximum(m[ch], c["pm_last"])
            c["keep"] = jnp.exp2(m[ch] - g_last)
            c["gain"] = jnp.exp2(c["pm_last"] - g_last)
            c["m_new"] = c["b_last"] + g_last
        for ch, c in enumerate(cs):
            o = twice(c["e_intra"]) * c["o1"] + twice(c["e_inter"]) * c["qc"]
            h = o[:, :MLSTM_DH] / jnp.maximum(jnp.abs(o[:, MLSTM_DH:]), c["floor"])
            c["o_ref"][c["rs"], c["hs"]] = h.astype(BF16)
            state[ch] = c["keep"] * state[ch] + c["gain"] * c["dc"]
            m[ch] = c["m_new"]
    for ch in range(nchain):
        c_scr[ch] = state[ch]
        m_scr[ch] = jnp.broadcast_to(m[ch], m_scr.shape[1:])


def _mlstm(mq, mk, mv, g_rows, g_cols, nct):
    bsz, ttot, nb = mq.shape
    nt = ttot // TM
    per = TM // CHUNK
    fwd = lambda w: pl.BlockSpec((None, TM, w), lambda b, j: (b, j, 0))
    rev = lambda w: pl.BlockSpec((None, TM, w), lambda b, j: (b, _scan_block(j, nct, nt), 0))
    gshape = (None, per, 2 * MLSTM_HEADS, CHUNK)
    gfwd = pl.BlockSpec(gshape, lambda b, j: (b, j, 0, 0))
    grev = pl.BlockSpec(gshape, lambda b, j: (b, _scan_block(j, nct, nt), 0, 0))
    return pl.pallas_call(
        _mlstm_kernel,
        out_shape=(jax.ShapeDtypeStruct((bsz, ttot, nb), BF16),) * 2,
        grid=(bsz, nt),
        in_specs=[fwd(nb), fwd(nb), fwd(nb), gfwd, fwd(LANES), rev(nb), rev(nb), rev(nb), grev, rev(LANES)],
        out_specs=(fwd(nb), rev(nb)),
        scratch_shapes=[pltpu.VMEM((2 * MLSTM_HEADS, MLSTM_DH, 2 * MLSTM_DH), F32),
                        pltpu.VMEM((2 * MLSTM_HEADS, SUBLANES, LANES), F32)],
        compiler_params=_params("parallel", "arbitrary"),
        name="mlstm_scan",
    )(mq, mk, mv, g_rows, g_cols, mq, mk, mv, g_rows, g_cols)


def _head_rms(x, n_heads):
    w = x.shape[1] // n_heads
    return jnp.concatenate([_rms(x[:, i * w:(i + 1) * w]) for i in range(n_heads)], axis=1)


def _out0_kernel(h_ref, gate_ref, a_ref, hf_ref, hr_ref, mo_ref, z_ref, hn_ref, w_ref, o_ref):
    f32 = lambda r: r[...].astype(F32)
    hm = _head_rms(f32(mo_ref) * (f32(hf_ref) + f32(hr_ref)), MLSTM_HEADS) * hn_ref[...]
    cat = (jnp.concatenate([f32(a_ref), hm], axis=1) * f32(z_ref)).astype(BF16)
    o_ref[...] = h_ref[...] + gate_ref[...] * _dot(cat, w_ref[...])


def _out0(h, gate, a, hf, hr, mo, z, h_norm, w_out, nct):
    bsz, ttot, _ = h.shape
    nt = ttot // TM
    tile, _, _, mod = _tile_specs(nct, nt)
    nb = MLSTM_HEADS * MLSTM_DH
    return pl.pallas_call(
        _out0_kernel,
        out_shape=jax.ShapeDtypeStruct(h.shape, F32),
        grid=(bsz, nt),
        in_specs=[tile(D_MODEL), mod, tile(MLA_HEADS * MLA_V), tile(nb), tile(nb), tile(nb), tile(D_MODEL),
                  _full((1, nb)), _full(w_out.shape)],
        out_specs=tile(D_MODEL),
        compiler_params=_params("parallel", "parallel"),
        name="out0",
    )(h, gate, a, hf, hr, mo, z, h_norm[None], w_out.astype(BF16))


def _proj1_kernel(h_ref, sc_ref, sh_ref, g_ref, wq_ref, wk_ref, wv_ref, wga_ref, wnq_ref, wnk_ref, wnv_ref, wz_ref,
                  wgate_ref, bgate_ref,
                  gq_out, gk_out, gv_out, bc_out, nq_out, nk_out, nv_out, z_out):
    ub = _modnorm(h_ref[...], g_ref[...], sc_ref[...], sh_ref[...]).astype(BF16)
    gq_out[...] = (_dot(ub, wq_ref[...]) * GLA_DK ** -0.5).astype(BF16)
    gk_out[...] = _dot(ub, wk_ref[...]).astype(BF16)
    gv_out[...] = _dot(ub, wv_ref[...]).astype(BF16)
    nq_out[...] = (_dot(ub, wnq_ref[...]) * (NA_DH ** -0.5 * LOG2E)).astype(BF16)
    nk_out[...] = _dot(ub, wnk_ref[...]).astype(BF16)
    nv_out[...] = _dot(ub, wnv_ref[...]).astype(BF16)
    z_out[...] = _silu(_dot(ub, wz_ref[...])).astype(BF16)

    ga = _dot(ub, wga_ref[...]).astype(BF16)
    lg = _log_sigmoid(_dot(ga, wgate_ref[...]) + bgate_ref[...]) * (1.0 / GLA_TAU)
    hi, mid, lo = _split3(lg)
    row = lax.broadcasted_iota(jnp.int32, (TM, TM), 0)
    col = lax.broadcasted_iota(jnp.int32, (TM, TM), 1)
    same = (row // CHUNK) == (col // CHUNK)
    nk = GLA_HEADS * GLA_DK
    for d in range(2):
        tri = jnp.where(jnp.logical_and(same, (col <= row) if d == 0 else (col >= row)), 1.0, 0.0).astype(BF16)
        cols = slice(d * nk, (d + 1) * nk)
        bc_out[:, cols] = (_dot(tri, hi[:, cols]) + _dot(tri, mid[:, cols]) + _dot(tri, lo[:, cols])) * LOG2E


def _proj1_weights(w_in, w_gate, b_gate):
    nk = GLA_HEADS * GLA_DK
    nc = GLA_HEADS * GLA_DV
    nd = NA_HEADS * NA_DH
    o = np.cumsum([0, nk, nk, nc, 2 * GLA_GATE_RANK, nd, nd, nd, nc + nd])
    r = GLA_GATE_RANK
    wgate = jnp.zeros((2 * r, 2 * nk), F32).at[:r, :nk].set(w_gate[0]).at[r:, nk:].set(w_gate[1])
    bf = lambda a: a.astype(BF16)
    names = ["wq", "wk", "wv", "wga", "wnq", "wnk", "wnv", "wz"]
    w = {n: bf(w_in[:, o[i]:o[i + 1]]) for i, n in enumerate(names)}
    w["wgate"] = bf(wgate)
    w["bgate"] = b_gate.reshape(1, 2 * nk)
    return w


def _proj1(h, scale, shift, norm_g, w, nct):
    bsz, ttot, _ = h.shape
    nt = ttot // TM
    tile, _, _, mod = _tile_specs(nct, nt)
    nk = GLA_HEADS * GLA_DK
    nc = GLA_HEADS * GLA_DV
    nd = NA_HEADS * NA_DH
    wnames = ["wq", "wk", "wv", "wga", "wnq", "wnk", "wnv", "wz", "wgate", "bgate"]
    ws = [w[n] for n in wnames]
    tok = lambda width, dt: jax.ShapeDtypeStruct((bsz, ttot, width), dt)
    return pl.pallas_call(
        _proj1_kernel,
        out_shape=(tok(nk, BF16), tok(nk, BF16), tok(nc, BF16), tok(2 * nk, F32),
                   tok(nd, BF16), tok(nd, BF16), tok(nd, BF16), tok(nc + nd, BF16)),
        grid=(bsz, nt),
        in_specs=[tile(D_MODEL), mod, mod, _full((1, D_MODEL))] + [_full(a.shape) for a in ws],
        out_specs=(tile(nk), tile(nk), tile(nc), tile(2 * nk), tile(nd), tile(nd), tile(nd), tile(nc + nd)),
        compiler_params=_params("parallel", "parallel"),
        name="proj1",
    )(h, scale, shift, norm_g[None], *ws)


def _gla_kernel(qf_ref, kf_ref, vf_ref, bf_ref, qr_ref, kr_ref, vr_ref, br_ref, of_ref, or_ref, s_scr):
    @pl.when(pl.program_id(1) == 0)
    def _():
        s_scr[...] = jnp.zeros_like(s_scr)

    L = CHUNK
    nchunk = TM // L
    npair = GLA_HEADS // 2
    lane = lax.broadcasted_iota(jnp.int32, (L, LANES), 1)
    rowv = lax.broadcasted_iota(jnp.int32, (L, LANES), 0)
    srow = lax.broadcasted_iota(jnp.int32, (LANES, 2 * GLA_DV), 0)
    scol = lax.broadcasted_iota(jnp.int32, (LANES, 2 * GLA_DV), 1)
    own_block = (srow < GLA_DK) == (scol < GLA_DV)
    vcol = lax.broadcasted_iota(jnp.int32, (L, 2 * GLA_DV), 1)
    e_row = lax.broadcasted_iota(jnp.int32, (LANES, LANES), 0)
    e_col = lax.broadcasted_iota(jnp.int32, (LANES, LANES), 1)
    head_ones = jnp.where((e_row < GLA_DK) == (e_col < GLA_DK), 1.0, 0.0).astype(BF16)
    leaf_idx = lax.broadcasted_iota(jnp.int32, (GLA_LEAF * GLA_LEAF, LANES), 0)
    leaf_lane = lax.broadcasted_iota(jnp.int32, (GLA_LEAF * GLA_LEAF, LANES), 1) % L
    leaf_s, leaf_t = leaf_idx // GLA_LEAF, leaf_idx % GLA_LEAF
    rep = lambda a: jnp.concatenate(
        [jnp.broadcast_to(a[s:s + 1], (GLA_LEAF, a.shape[1])) for s in range(GLA_LEAF)], axis=0)
    til = lambda a: jnp.concatenate([a] * GLA_LEAF, axis=0)

    def chunk_step(ci, carry):
        for d, (q_ref, k_ref, v_ref, b_ref, o_ref) in enumerate(
                ((qf_ref, kf_ref, vf_ref, bf_ref, of_ref), (qr_ref, kr_ref, vr_ref, br_ref, or_ref))):
            cd = ci if d == 0 else nchunk - 1 - ci
            base = pl.multiple_of(cd * L, L)
            tau_v = rowv if d == 0 else L - 1 - rowv
            tau_s = (lane % L) if d == 0 else L - 1 - (lane % L)
            for p in range(npair):
                ls = slice(p * LANES, (p + 1) * LANES)
                q = q_ref[pl.ds(base, L), ls].astype(F32)
                k = k_ref[pl.ds(base, L), ls].astype(F32)
                bc = b_ref[pl.ds(base, L), ls]
                v = v_ref[pl.ds(base, L), p * 2 * GLA_DV:(p + 1) * 2 * GLA_DV]
                b_last = bc[L - 1:L] if d == 0 else bc[0:1]
                state = s_scr[d, p]

                o_pair = _dot((q * jnp.exp2(bc)).astype(BF16), state.astype(BF16))

                att = jnp.zeros((L, LANES), F32)
                bs = L // 2
                while bs >= GLA_LEAF:
                    later = (tau_v // bs) % 2 == 1
                    nblk = L // (2 * bs)
                    ref_b = None
                    for blk in range(nblk):
                        tau_ref = blk * 2 * bs + bs - 1
                        idx = tau_ref if d == 0 else L - 1 - tau_ref
                        r = jnp.broadcast_to(bc[idx:idx + 1], (L, LANES))
                        ref_b = r if ref_b is None else jnp.where(tau_v // (2 * bs) == blk, r, ref_b)
                    qs = jnp.where(later, q * jnp.exp2(jnp.where(later, bc - ref_b, 0.0)), 0.0).astype(BF16)
                    ks = jnp.where(later, 0.0, k * jnp.exp2(jnp.where(later, 0.0, ref_b - bc)))
                    ks2 = jnp.concatenate([jnp.where(lane < GLA_DK, ks, 0.0), jnp.where(lane < GLA_DK, 0.0, ks)],
                                          axis=0).astype(BF16)
                    same_parent = (tau_v // (2 * bs)) == (tau_s // (2 * bs))
                    att = att + jnp.where(same_parent, _dot_nt(qs, ks2), 0.0)
                    bs //= 2

                leaf = []
                for blk in range(L // GLA_LEAF):
                    rs = slice(blk * GLA_LEAF, (blk + 1) * GLA_LEAF)
                    qb, bb, kb = q[rs], bc[rs], k[rs]
                    ok = (leaf_t >= leaf_s) if d == 0 else (leaf_t <= leaf_s)
                    w = jnp.where(ok, til(qb) * jnp.exp2(til(bb) - rep(bb)) * rep(kb), 0.0)
                    a = jnp.where(leaf_lane == leaf_s + blk * GLA_LEAF, _dot(w.astype(BF16), head_ones), 0.0)
                    acc = a[0:GLA_LEAF]
                    for s in range(1, GLA_LEAF):
                        acc = acc + a[s * GLA_LEAF:(s + 1) * GLA_LEAF]
                    leaf.append(acc)
                att = (att + jnp.concatenate(leaf, axis=0)).astype(BF16)
                v_blk = jnp.concatenate([jnp.where(vcol < GLA_DV, v, jnp.zeros_like(v)),
                                         jnp.where(vcol < GLA_DV, jnp.zeros_like(v), v)], axis=0)
                o_ref[pl.ds(base, L), p * 2 * GLA_DV:(p + 1) * 2 * GLA_DV] = (o_pair + _dot(att, v_blk)).astype(BF16)

                ke = (k * jnp.exp2(b_last - bc)).astype(BF16)
                decay_col = jnp.sum(jnp.where(e_row == e_col, jnp.broadcast_to(jnp.exp2(b_last), (LANES, LANES)), 0.0),
                                    axis=1, keepdims=True)
                s_scr[d, p] = decay_col * state + jnp.where(own_block, _dot_tn(ke, v), 0.0)
        return carry

    lax.fori_loop(0, nchunk, chunk_step, 0)


def _gla(gq, gk, gv, bc, nct):
    bsz, ttot, nk = gq.shape
    nc = gv.shape[2]
    nt = ttot // TM
    fwd = lambda w: pl.BlockSpec((None, TM, w), lambda b, j: (b, j, 0))
    rev = lambda w: pl.BlockSpec((None, TM, w), lambda b, j: (b, _scan_block(j, nct, nt), 0))
    bfwd = pl.BlockSpec((None, TM, nk), lambda b, j: (b, j, 0))
    brev = pl.BlockSpec((None, TM, nk), lambda b, j: (b, _scan_block(j, nct, nt), 1))
    return pl.pallas_call(
        _gla_kernel,
        out_shape=(jax.ShapeDtypeStruct((bsz, ttot, nc), BF16),) * 2,
        grid=(bsz, nt),
        in_specs=[fwd(nk), fwd(nk), fwd(nc), bfwd, rev(nk), rev(nk), rev(nc), brev],
        out_specs=(fwd(nc), rev(nc)),
        scratch_shapes=[pltpu.VMEM((2, GLA_HEADS // 2, LANES, 2 * GLA_DV), F32)],
        compiler_params=_params("parallel", "arbitrary"),
        name="gla_scan",
    )(gq, gk, gv, bc, gq, gk, gv, bc)


def _na_bias_tables(rpb, rows):
    kh = NA_KH
    c = np.arange(GRID_W)
    cs = np.clip(c - NA_KW // 2, 0, GRID_W - NA_KW)
    col_ok = (c[None, :] >= cs[:, None]) & (c[None, :] < cs[:, None] + NA_KW)
    pad = GRID_W - NA_KW
    rpb_pad = jnp.pad(rpb.astype(F32) * LOG2E, ((0, 0), (0, 0), (pad, pad)))
    t1 = jnp.stack([rpb_pad[:, :, GRID_W - 1 - cq:2 * GRID_W - 1 - cq] for cq in range(GRID_W)], axis=2)
    t1 = jnp.where(col_ok, t1, NEG_BIG)
    tabs = []
    for r0 in (0, NA_QROWS, rows - NA_QROWS):
        kb = int(np.clip(r0 - kh // 2, 0, rows - NA_KROWS))
        qr = r0 + np.arange(NA_QROWS)
        ws = np.clip(qr - kh // 2, 0, rows - kh)
        krow = kb + np.arange(NA_KROWS)
        row_ok = (krow[None, :] >= ws[:, None]) & (krow[None, :] < ws[:, None] + kh)
        drow = np.clip(krow[None, :] - qr[:, None] + (NA_KH - 1), 0, 2 * NA_KH - 2)
        tv = jnp.take(t1, jnp.asarray(drow.reshape(-1)), axis=1)
        tv = jnp.where(row_ok.reshape(1, -1, 1, 1), tv, NEG_BIG)
        tv = tv.reshape(NA_HEADS, NA_QROWS, NA_KROWS, GRID_W, GRID_W).transpose(0, 1, 3, 2, 4)
        tabs.append(tv.reshape(NA_HEADS, NA_QROWS * GRID_W, NA_KROWS * GRID_W))
    return jnp.stack(tabs)


def _na_kernel(q_ref, k_ref, v_ref, bias_ref, o_ref, *, n_ctx, rows):
    rb = pl.program_id(2)
    nq = NA_QROWS * GRID_W
    nk = NA_KROWS * GRID_W
    kb = jnp.clip(rb * NA_QROWS - NA_KH // 2, 0, rows - NA_KROWS)
    start = pl.multiple_of(n_ctx + kb * GRID_W, GRID_W)
    lane = lax.broadcasted_iota(jnp.int32, (nq, LANES), 1)
    outs = []
    for j in range(NA_HPS):
        ls = slice((j // 2) * LANES, (j // 2 + 1) * LANES)
        q = q_ref[:, ls]
        qj = jnp.where((lane < NA_DH) == (j % 2 == 0), q, jnp.zeros_like(q))
        s_loc = _dot_nt(qj, k_ref[pl.ds(start, nk), ls]) + bias_ref[j]
        s_ctx = _dot_nt(qj, k_ref[0:n_ctx, ls])
        m = jnp.maximum(jnp.max(s_loc, axis=-1, keepdims=True), jnp.max(s_ctx, axis=-1, keepdims=True))
        p_loc = jnp.exp2(s_loc - m)
        p_ctx = jnp.exp2(s_ctx - m)
        denom = jnp.sum(p_loc, axis=-1, keepdims=True) + jnp.sum(p_ctx, axis=-1, keepdims=True)
        outs.append((_dot(p_loc.astype(BF16), v_ref[pl.ds(start, nk), ls])
                     + _dot(p_ctx.astype(BF16), v_ref[0:n_ctx, ls])) / denom)
    for j in range(NA_HPS // 2):
        o_ref[:, j * LANES:(j + 1) * LANES] = jnp.where(lane < NA_DH, outs[2 * j], outs[2 * j + 1]).astype(BF16)


def _na_attention(nq, nk, nv, bias, n_ctx, rows):
    bsz, ttot, _ = nq.shape
    nrb = rows // NA_QROWS
    qrows = NA_QROWS * GRID_W
    krows = NA_KROWS * GRID_W
    qoff = n_ctx // qrows
    hw = NA_HPS * NA_DH

    def variant(rb):
        return jnp.where(rb == 0, 0, jnp.where(rb == nrb - 1, 2, 1))

    return pl.pallas_call(
        functools.partial(_na_kernel, n_ctx=n_ctx, rows=rows),
        out_shape=jax.ShapeDtypeStruct((bsz, rows * GRID_W, NA_HEADS * NA_DH), BF16),
        grid=(bsz, NA_HEADS // NA_HPS, nrb),
        in_specs=[pl.BlockSpec((None, qrows, hw), lambda b, p, rb: (b, qoff + rb, p)),
                  pl.BlockSpec((None, ttot, hw), lambda b, p, rb: (b, 0, p)),
                  pl.BlockSpec((None, ttot, hw), lambda b, p, rb: (b, 0, p)),
                  pl.BlockSpec((None, NA_HPS, qrows, krows), lambda b, p, rb: (variant(rb), p, 0, 0))],
        out_specs=pl.BlockSpec((None, qrows, hw), lambda b, p, rb: (b, rb, p)),
        compiler_params=_params("parallel", "parallel", "parallel"),
        name="neighbourhood_attention",
    )(nq, nk, nv, bias)


def _out1_kernel(h_ref, gate_ref, of_ref, or_ref, na_ref, z_ref, gn_ref, w_ref, fn_ref, o_ref):
    f32 = lambda r: r[...].astype(F32)
    g = _head_rms(f32(of_ref) + f32(or_ref), GLA_HEADS) * gn_ref[...]
    cat = (jnp.concatenate([g, f32(na_ref)], axis=1) * f32(z_ref)).astype(BF16)
    h = h_ref[...] + gate_ref[...] * _dot(cat, w_ref[...])
    o_ref[...] = _rms(h) * fn_ref[...]


def _out1(h, gate, of, orv, na, z, gla_norm, w_out, final_norm, nct):
    bsz, ttot, _ = h.shape
    seq = na.shape[1]
    nc = GLA_HEADS * GLA_DV
    nd = NA_HEADS * NA_DH
    lat = lambda w: pl.BlockSpec((None, TM, w), lambda b, t: (b, t + nct, 0))
    return pl.pallas_call(
        _out1_kernel,
        out_shape=jax.ShapeDtypeStruct((bsz, seq, D_MODEL), F32),
        grid=(bsz, seq // TM),
        in_specs=[lat(D_MODEL), pl.BlockSpec((None, None, 1, D_MODEL), lambda b, t: (b, 1, 0, 0)),
                  lat(nc), lat(nc), pl.BlockSpec((None, TM, nd), lambda b, t: (b, t, 0)), lat(nc + nd),
                  _full((1, nc)), _full(w_out.shape), _full((1, D_MODEL))],
        out_specs=pl.BlockSpec((None, TM, D_MODEL), lambda b, t: (b, t, 0)),
        compiler_params=_params("parallel", "parallel"),
        name="out1",
    )(h, gate, of, orv, na, z, gla_norm[None], w_out.astype(BF16), final_norm[None])


def kernel(x, c, ctx, c_ctx, l0_norm, l0_w_mod, l0_b_mod, l0_w_in, l0_mla_q_norm, l0_mla_w_uq, l0_mla_kv_norm, l0_mla_w_ukv, l0_mlstm_conv_w, l0_mlstm_conv_b, l0_mlstm_b_i, l0_mlstm_b_f, l0_mlstm_norm, l0_w_out, l1_norm, l1_w_mod, l1_b_mod, l1_w_in, l1_gla_w_gate, l1_gla_b_gate, l1_gla_norm, l1_na_rpb, l1_w_out, final_norm):
    bsz, seq, d = x.shape
    n_ctx = ctx.shape[1]
    rows = seq // GRID_W
    assert d == D_MODEL and seq % TM == 0 and n_ctx % TM == 0
    assert rows % NA_QROWS == 0 and rows >= NA_KROWS and rows // NA_QROWS >= 3
    nct = n_ctx // TM

    h = jnp.concatenate([ctx, x], axis=1)

    shift, scale, gate = _mod_vectors(c, c_ctx, l0_w_mod, l0_b_mod)
    cos_t, sin_t = _rope_tables(n_ctx, seq)
    w0 = _proj0_weights(l0_w_in, l0_mla_q_norm, l0_mla_w_uq, l0_mla_kv_norm, l0_mla_w_ukv,
                        l0_mlstm_conv_w, l0_mlstm_conv_b, l0_mlstm_b_i, l0_mlstm_b_f)
    q, k, v, mq, mk, mv, g_rows, g_cols, mo, z = _proj0(h, scale, shift, l0_norm, cos_t, sin_t, w0, nct)
    a = _mla_attention(q, k, v, nct)
    hf, hr = _mlstm(mq, mk, mv, g_rows, g_cols, nct)
    h = _out0(h, gate, a, hf, hr, mo, z, l0_mlstm_norm, l0_w_out, nct)

    shift, scale, gate = _mod_vectors(c, c_ctx, l1_w_mod, l1_b_mod)
    w1 = _proj1_weights(l1_w_in, l1_gla_w_gate, l1_gla_b_gate)
    gq, gk, gv, bc, nq, nk, nv, z = _proj1(h, scale, shift, l1_norm, w1, nct)
    of, orv = _gla(gq, gk, gv, bc, nct)
    na = _na_attention(nq, nk, nv, _na_bias_tables(l1_na_rpb, rows), n_ctx, rows)
    return _out1(h, gate, of, orv, na, z, l1_gla_norm, l1_w_out, final_norm, nct)
```

```python
import functools

import jax
import jax.numpy as jnp
import numpy as np
from jax import lax
from jax.experimental import pallas as pl
from jax.experimental.pallas import tpu as pltpu

F32 = jnp.float32
BF16 = jnp.bfloat16

D_MODEL = 1024
GRID_W = 64
EPS = 1e-6
ROPE_BASE = 10000.0

MLA_HEADS = 8
MLA_Q_RANK = 384
MLA_KV_RANK = 256
MLA_NOPE = 64
MLA_ROPE = 32
MLA_V = 64
ROPE_PAIRS = MLA_ROPE // 4
MLSTM_HEADS = 4
MLSTM_DH = 128
MLSTM_CONV = 3
GLA_HEADS = 4
GLA_DK = 64
GLA_DV = 128
GLA_GATE_RANK = 16
GLA_TAU = 16.0
NA_HEADS = 8
NA_DH = 64
NA_KH = 8
NA_KW = 16

CHUNK = 64
TM = 256
LANES = 128
SUBLANES = 8
HALO = SUBLANES
NA_QROWS = 4
NA_KROWS = NA_QROWS + NA_KH - 1
GLA_LEAF = 16
MLA_HPS = 4
NA_HPS = 4
VMEM_LIMIT = 56 * 1024 * 1024
NEG_BIG = -1e30
LOG2E = 1.4426950408889634


def _dot(a, b):
    return jnp.dot(a, b, preferred_element_type=F32)


def _dot_nt(a, b):
    return lax.dot_general(a, b, (((1,), (1,)), ((), ())), preferred_element_type=F32)


def _dot_tn(a, b):
    return lax.dot_general(a, b, (((0,), (0,)), ((), ())), preferred_element_type=F32)


def _rms(x):
    return x * lax.rsqrt(jnp.mean(x * x, axis=-1, keepdims=True) + EPS)


def _sigmoid(x):
    return 1.0 / (1.0 + jnp.exp(-x))


def _silu(x):
    return x * _sigmoid(x)


def _log_sigmoid(x):
    return jnp.minimum(x, 0.0) - jnp.log(1.0 + jnp.exp(-jnp.abs(x)))


def _params(*sem):
    return pltpu.CompilerParams(dimension_semantics=sem, vmem_limit_bytes=VMEM_LIMIT)


def _full(shape):
    nd = len(shape)
    return pl.BlockSpec(shape, lambda *_: (0,) * nd)


def _mod_kernel(c_ref, w_ref, b_ref, o_ref):
    o_ref[...] = _dot(_silu(c_ref[...]).astype(BF16), w_ref[...]) + b_ref[...]


def _modulation(cc, w_mod, b_mod):
    rows, d = cc.shape
    n = w_mod.shape[1]
    return pl.pallas_call(
        _mod_kernel,
        out_shape=jax.ShapeDtypeStruct((rows, n), F32),
        grid=(n // d,),
        in_specs=[_full((rows, d)), pl.BlockSpec((d, d), lambda j: (0, j)), pl.BlockSpec((1, d), lambda j: (0, j))],
        out_specs=pl.BlockSpec((rows, d), lambda j: (0, j)),
        compiler_params=_params("parallel"),
        name="modulation",
    )(cc, w_mod.astype(BF16), b_mod[None])


def _mod_vectors(c, c_ctx, w_mod, b_mod):
    bsz = c.shape[0]
    rows = -(-(bsz + 1) // SUBLANES) * SUBLANES
    cc = jnp.zeros((rows, D_MODEL), F32).at[:bsz].set(c).at[bsz].set(c_ctx)
    mod = _modulation(cc, w_mod, b_mod)
    out = []
    for part in jnp.split(mod, 3, axis=-1):
        ctx_v = jnp.broadcast_to(part[bsz][None], (bsz, D_MODEL))
        out.append(jnp.stack([ctx_v, part[:bsz]], axis=1)[:, :, None, :])
    return out


def _modnorm(x, g, scale, shift):
    return _rms(x) * g * (1.0 + scale) + shift


def _seg_scan(x, reverse, op, fill):
    width = x.shape[1]
    pos = lax.broadcasted_iota(jnp.int32, x.shape, 1) % CHUNK
    k = 1
    while k < CHUNK:
        if reverse:
            x = op(x, jnp.where(pos < CHUNK - k, pltpu.roll(x, width - k, axis=1), fill))
        else:
            x = op(x, jnp.where(pos >= k, pltpu.roll(x, k, axis=1), fill))
        k *= 2
    return x


def _split3(x):
    hi = x.astype(BF16)
    r1 = x - hi.astype(F32)
    mid = r1.astype(BF16)
    return hi, mid, (r1 - mid.astype(F32)).astype(BF16)


def _proj0_kernel(ctx_ref, lat_ref, hp_ref, hn_ref, sc_ref, sh_ref, g_ref, cos_ref, sin_ref,
                  wcq_ref, wckv_ref, wkr_ref, wkrs_ref, wmqk_ref, wmv_ref, wmo_ref, wgt_ref, wz_ref,
                  qn_ref, wuq_ref, wuqs_ref, kvn_ref, wk_ref, wv_ref, cw_ref, cb_ref, gb_ref,
                  q_out, k_out, v_out, mq_out, mk_out, mv_out, gr_out, gc_out, mo_out, z_out,
                  pbuf, *, nct, nt):
    t = pl.program_id(1)
    h = jnp.where(t < nct, ctx_ref[...], lat_ref[...])
    x = jnp.concatenate([hp_ref[...], h, hn_ref[...]], axis=0)
    ub_ext = _modnorm(x, g_ref[...], sc_ref[...], sh_ref[...]).astype(BF16)
    ub = ub_ext[HALO:HALO + TM]

    gt = _dot_nt(wgt_ref[...], ub) + gb_ref[...]
    half = MLSTM_HEADS
    per_dir = []
    for d in range(2):
        gd = gt[d * SUBLANES:(d + 1) * SUBLANES]
        bcum = _seg_scan(_log_sigmoid(gd), d == 1, jnp.add, 0.0)
        b_top = pltpu.roll(bcum, half, axis=0)
        c8 = gd - b_top
        pm8 = _seg_scan(c8, d == 1, jnp.maximum, -jnp.inf)
        per_dir.append((c8, pm8, b_top))

    cos = cos_ref[...]
    sin = sin_ref[...]
    cos_all = jnp.concatenate([cos] * MLA_HEADS, axis=1)
    sin_all = jnp.concatenate([sin] * MLA_HEADS, axis=1)

    nq = (_rms(_dot(ub, wcq_ref[...])) * qn_ref[...]).astype(BF16)
    q = _dot(nq, wuq_ref[...]) * cos_all + _dot(nq, wuqs_ref[...]) * sin_all
    q_out[...] = (q * ((MLA_NOPE + MLA_ROPE) ** -0.5 * LOG2E)).astype(BF16)

    nkv = (_rms(_dot(ub, wckv_ref[...])) * kvn_ref[...]).astype(BF16)
    rot = _dot(ub, wkr_ref[...]) * cos + _dot(ub, wkrs_ref[...]) * sin
    k_out[...] = (_dot(nkv, wk_ref[...]) + jnp.concatenate([rot] * MLA_HEADS, axis=1)).astype(BF16)
    vlane = lax.broadcasted_iota(jnp.int32, (TM, MLA_HEADS * LANES), 1)
    is_value = ((vlane % LANES) < MLA_V) == ((vlane // LANES) % 2 == 0)
    v_out[...] = jnp.where(is_value, _dot(nkv, wv_ref[...]), 1.0).astype(BF16)

    pqk = _dot(ub_ext, wmqk_ref[...])
    prev_ok = t > nct
    next_ok = jnp.logical_and(t >= nct, t != nt - 1)
    row = lax.broadcasted_iota(jnp.int32, pqk.shape, 0)
    keep = jnp.logical_and(jnp.logical_or(row >= HALO, prev_ok), jnp.logical_or(row < HALO + TM, next_ok))
    pbuf[...] = jnp.where(keep, pqk, 0.0)
    cw = cw_ref[...]
    y = (pbuf[HALO - 1:HALO - 1 + TM, :] * cw[0:1] + pbuf[HALO:HALO + TM, :] * cw[1:2]
         + pbuf[HALO + 1:HALO + 1 + TM, :] * cw[2:3] + cb_ref[...])
    y = _silu(y)
    nb = MLSTM_HEADS * MLSTM_DH
    mq_out[...] = (y[:, :nb] * MLSTM_DH ** -0.5).astype(BF16)
    mk_out[...] = y[:, nb:].astype(BF16)
    mv_out[...] = _dot(ub, wmv_ref[...]).astype(BF16)
    mo_out[...] = _sigmoid(_dot(ub, wmo_ref[...])).astype(BF16)
    z_out[...] = _silu(_dot(ub, wz_ref[...])).astype(BF16)

    r8 =lax.broadcasted_iota(jnp.int32, (SUBLANES, TM), 0)
    both = lambda i: jnp.where(r8 < half, per_dir[0][i], pltpu.roll(per_dir[1][i], half, axis=0)) * LOG2E
    c_rows, pm_rows, b_rows = both(0), both(1), both(2)
    for ci in range(TM // CHUNK):
        gr_out[ci] = c_rows[:, ci * CHUNK:(ci + 1) * CHUNK]
    stack = jnp.concatenate([c_rows, pm_rows, b_rows, jnp.zeros((LANES - 3 * SUBLANES, TM), F32)], axis=0)
    er = lax.broadcasted_iota(jnp.int32, (TM, TM), 0)
    ec = lax.broadcasted_iota(jnp.int32, (TM, TM), 1)
    eye = jnp.where(er == ec, 1.0, 0.0).astype(BF16)
    gc_out[...] = sum(_dot_nt(eye, part) for part in _split3(stack))


def _rope_tables(n_ctx, seq):
    t = jnp.arange(seq)
    inv = 1.0 / (ROPE_BASE ** (jnp.arange(ROPE_PAIRS, dtype=F32) / ROPE_PAIRS))
    ang = jnp.concatenate([(t // GRID_W)[:, None] * inv, (t % GRID_W)[:, None] * inv], axis=-1)
    cos, sin = jnp.cos(ang), jnp.sin(ang)
    j = np.arange(MLA_ROPE)
    src = (j // (2 * ROPE_PAIRS)) * ROPE_PAIRS + (j % ROPE_PAIRS)
    sign = np.where((j % (2 * ROPE_PAIRS)) < ROPE_PAIRS, -1.0, 1.0).astype(np.float32)
    cos_full = cos[:, src]
    sin_full = sin[:, src] * sign
    cos_t = jnp.ones((n_ctx + seq, LANES), F32).at[n_ctx:, MLA_NOPE:MLA_NOPE + MLA_ROPE].set(cos_full)
    sin_t = jnp.zeros((n_ctx + seq, LANES), F32).at[n_ctx:, MLA_NOPE:MLA_NOPE + MLA_ROPE].set(sin_full)
    return cos_t, sin_t


def _rope_partner():
    j = np.arange(MLA_ROPE)
    return np.where((j % (2 * ROPE_PAIRS)) < ROPE_PAIRS, j + ROPE_PAIRS, j - ROPE_PAIRS)


def _proj0_weights(w_in, q_norm, w_uq, kv_norm, w_ukv, conv_w, conv_b, b_i, b_f):
    o = np.cumsum([0, MLA_Q_RANK, MLA_KV_RANK, MLA_ROPE] + [MLSTM_HEADS * MLSTM_DH] * 4 + [4 * MLSTM_HEADS, D_MODEL])
    part = _rope_partner()
    w_kr = w_in[:, o[2]:o[3]]
    pad_kr = jnp.zeros((D_MODEL, LANES), F32).at[:, MLA_NOPE:MLA_NOPE + MLA_ROPE]
    dq = MLA_NOPE + MLA_ROPE
    wuq = jnp.zeros((MLA_Q_RANK, MLA_HEADS, LANES), F32).at[:, :, :dq].set(w_uq.reshape(MLA_Q_RANK, MLA_HEADS, dq))
    wuq_sw = jnp.zeros((MLA_Q_RANK, MLA_HEADS, LANES), F32).at[:, :, MLA_NOPE:dq].set(
        w_uq.reshape(MLA_Q_RANK, MLA_HEADS, dq)[:, :, MLA_NOPE + part])
    wkv = w_ukv.reshape(MLA_KV_RANK, MLA_HEADS, MLA_NOPE + MLA_V)
    wk = jnp.zeros((MLA_KV_RANK, MLA_HEADS, LANES), F32).at[:, :, :MLA_NOPE].set(wkv[:, :, :MLA_NOPE])
    zv = jnp.zeros((MLA_KV_RANK, MLA_HEADS // 2, MLA_V), F32)
    wv = jnp.concatenate([wkv[:, 0::2, MLA_NOPE:], zv, zv, wkv[:, 1::2, MLA_NOPE:]], axis=-1)
    wv = wv.reshape(MLA_KV_RANK, MLA_HEADS * LANES)
    gbias = jnp.stack([b_i, b_f], axis=1).reshape(4 * MLSTM_HEADS, 1)
    bf = lambda a: a.astype(BF16)
    return dict(
        wcq=bf(w_in[:, o[0]:o[1]]), wckv=bf(w_in[:, o[1]:o[2]]),
        wkr=bf(pad_kr.set(w_kr)), wkrs=bf(pad_kr.set(w_kr[:, part])),
        wmqk=bf(w_in[:, o[3]:o[5]]), wmv=bf(w_in[:, o[5]:o[6]]), wmo=bf(w_in[:, o[6]:o[7]]),
        wgt=bf(w_in[:, o[7]:o[8]].T), wz=bf(w_in[:, o[8]:o[9]]),
        qn=q_norm[None], wuq=bf(wuq.reshape(MLA_Q_RANK, -1)), wuqs=bf(wuq_sw.reshape(MLA_Q_RANK, -1)),
        kvn=kv_norm[None], wk=bf(wk.reshape(MLA_KV_RANK, -1)), wv=bf(wv),
        cw=jnp.zeros((SUBLANES, conv_w.shape[1]), F32).at[:MLSTM_CONV].set(conv_w), cb=conv_b[None], gb=gbias)


def _tile_specs(nct, nt):
    tile = lambda w: pl.BlockSpec((None, TM, w), lambda b, t: (b, t, 0))
    per = TM // HALO
    nlat = nt - nct
    src_ctx = pl.BlockSpec((None, TM, D_MODEL), lambda b, t: (b, jnp.minimum(t, nct - 1), 0))
    src_lat = pl.BlockSpec((None, TM, D_MODEL), lambda b, t: (b, jnp.maximum(t - nct, 0), 0))
    prev = pl.BlockSpec((None, HALO, D_MODEL), lambda b, t: (b, jnp.maximum((t - nct) * per - 1, 0), 0))
    nxt = pl.BlockSpec((None, HALO, D_MODEL),
                       lambda b, t: (b, jnp.clip((t - nct + 1) * per, 0, nlat * per - 1), 0))
    mod = pl.BlockSpec((None, None, 1, D_MODEL), lambda b, t: (b, (t >= nct).astype(jnp.int32), 0, 0))
    return tile, src_ctx, src_lat, prev, nxt, mod


def _proj0(ctx, x, scale, shift, norm_g, cos_t, sin_t, w, nct):
    bsz = x.shape[0]
    ttot = ctx.shape[1] + x.shape[1]
    nt = ttot // TM
    nb = MLSTM_HEADS * MLSTM_DH
    tile, src_ctx, src_lat, prev, nxt, mod = _tile_specs(nct, nt)
    tab = pl.BlockSpec((TM, LANES), lambda b, t: (t, 0))
    wnames = ["wcq", "wckv", "wkr", "wkrs", "wmqk", "wmv", "wmo", "wgt", "wz",
              "qn", "wuq", "wuqs", "kvn", "wk", "wv", "cw", "cb", "gb"]
    ws = [w[n] for n in wnames]
    tok = lambda width, dt: jax.ShapeDtypeStruct((bsz, ttot, width), dt)
    out_shape = (tok(MLA_HEADS * LANES, BF16), tok(MLA_HEADS * LANES, BF16), tok(MLA_HEADS * LANES, BF16),
                 tok(nb, BF16), tok(nb, BF16), tok(nb, BF16),
                 jax.ShapeDtypeStruct((bsz, ttot // CHUNK, 2 * MLSTM_HEADS, CHUNK), F32), tok(LANES, F32),
                 tok(nb, BF16), tok(D_MODEL, BF16))
    out_specs = (tile(MLA_HEADS * LANES), tile(MLA_HEADS * LANES), tile(MLA_HEADS * LANES),
                 tile(nb), tile(nb), tile(nb),
                 pl.BlockSpec((None, TM // CHUNK, 2 * MLSTM_HEADS, CHUNK), lambda b, t: (b, t, 0, 0)), tile(LANES),
                 tile(nb), tile(D_MODEL))
    return pl.pallas_call(
        functools.partial(_proj0_kernel, nct=nct, nt=nt),
        out_shape=out_shape,
        grid=(bsz, nt),
        in_specs=[src_ctx, src_lat, prev, nxt, mod, mod, _full((1, D_MODEL)), tab, tab] + [_full(a.shape) for a in ws],
        out_specs=out_specs,
        scratch_shapes=[pltpu.VMEM((TM + 2 * HALO, 2 * nb), F32)],
        compiler_params=_params("parallel", "parallel"),
        name="proj0",
    )(ctx, x, x, x, scale, shift, norm_g[None], cos_t, sin_t, *ws)


def _mla_kernel(q_ref, k_ref, v_ref, o_ref, *, nct):
    t = pl.program_id(2)
    lane = lax.broadcasted_iota(jnp.int32, (TM, LANES), 1)

    def attend(nkeys):
        qk = lambda j: _dot_nt(q_ref[:, j * LANES:(j + 1) * LANES], k_ref[0:nkeys, j * LANES:(j + 1) * LANES])
        outs = []
        s_next = qk(0)
        for j in range(MLA_HPS):
            s = s_next
            if j + 1 < MLA_HPS:
                s_next = qk(j + 1)
            p = jnp.exp2(s - jnp.max(s, axis=-1, keepdims=True))
            o = _dot(p.astype(BF16), v_ref[0:nkeys, j * LANES:(j + 1) * LANES])
            outs.append(o / pltpu.roll(o, MLA_V, axis=1))
        for j in range(MLA_HPS // 2):
            o_ref[:, j * LANES:(j + 1) * LANES] = jnp.where(lane < MLA_V, outs[2 * j], outs[2 * j + 1]).astype(BF16)

    @pl.when(t < nct)
    def _():
        attend(nct * TM)

    @pl.when(t >= nct)
    def _():
        attend(k_ref.shape[0])


def _mla_attention(q, k, v, nct):
    bsz, ttot, _ = q.shape
    nt = ttot // TM
    hw = MLA_HPS * LANES
    return pl.pallas_call(
        functools.partial(_mla_kernel, nct=nct),
        out_shape=jax.ShapeDtypeStruct((bsz, ttot, MLA_HEADS * MLA_V), BF16),
        grid=(bsz, MLA_HEADS // MLA_HPS, nt),
        in_specs=[pl.BlockSpec((None, TM, hw), lambda b, p, t: (b, t, p)),
                  pl.BlockSpec((None, ttot, hw), lambda b, p, t: (b, 0, p)),
                  pl.BlockSpec((None, ttot, hw), lambda b, p, t: (b, 0, p))],
        out_specs=pl.BlockSpec((None, TM, hw // 2), lambda b, p, t: (b, t, p)),
        compiler_params=_params("parallel", "parallel", "parallel"),
        name="mla_attention",
    )(q, k, v)


def _scan_block(j, nct, nt):
    return jnp.where(j < nct, nct - 1 - j, nt - 1 - (j - nct))


def _mlstm_kernel(qf_ref, kf_ref, vf_ref, grf_ref, gcf_ref, qr_ref, kr_ref, vr_ref, grr_ref, gcr_ref,
                  hf_ref, hr_ref, c_scr, m_scr):
    @pl.when(pl.program_id(1) == 0)
    def _():
        c_scr[...] = jnp.zeros_like(c_scr)
        m_scr[...] = jnp.zeros_like(m_scr)

    L = CHUNK
    nchunk = TM // L
    nh = MLSTM_HEADS
    nchain = 2 * nh
    row = lax.broadcasted_iota(jnp.int32, (L, L), 0)
    col = lax.broadcasted_iota(jnp.int32, (L, L), 1)
    ones = jnp.ones((L, MLSTM_DH), BF16)
    wide = lambda a: jnp.broadcast_to(a, (L, MLSTM_DH))
    twice = lambda a: jnp.concatenate([a, a], axis=1)
    refs = ((qf_ref, kf_ref, vf_ref, grf_ref, gcf_ref, hf_ref), (qr_ref, kr_ref, vr_ref, grr_ref, gcr_ref, hr_ref))

    items = {}
    for ci in range(nchunk):
        for d, (q_ref, k_ref, v_ref, gr_ref, gc_ref, o_ref) in enumerate(refs):
            cd = ci if d == 0 else nchunk - 1 - ci
            rs = slice(cd * L, (cd + 1) * L)
            g_rows = gr_ref[cd]
            g_cols = gc_ref[rs, :]
            last = L - 1 if d == 0 else 0
            for hd in range(nh):
                ch = d * nh + hd
                hs = slice(hd * MLSTM_DH, (hd + 1) * MLSTM_DH)
                items[ci, ch] = dict(
                    rs=rs, hs=hs, o_ref=o_ref, causal=(col <= row) if d == 0 else (col >= row),
                    c_row=g_rows[ch:ch + 1], c_col=g_cols[:, ch:ch + 1],
                    pm_col=g_cols[:, nchain + ch:nchain + ch + 1], b_col=g_cols[:, 2 * nchain + ch:2 * nchain + ch + 1],
                    pm_last=g_cols[last:last + 1, nchain + ch:nchain + ch + 1],
                    b_last=g_cols[last:last + 1, 2 * nchain + ch:2 * nchain + ch + 1],
                    q=q_ref[rs, hs], k=k_ref[rs, hs],
                    v_ext=jnp.concatenate([v_ref[rs, hs], ones], axis=1))
    order = [items[ci, ch] for ci in range(nchunk) for ch in range(nchain)]
    for c in order:
        c["s"] = _dot_nt(c["q"], c["k"])
    for c in order:
        c["pm_w"] = wide(c["pm_col"])
        c["b_w"] = wide(c["b_col"])
        c["w"] = jnp.exp2(jnp.where(c["causal"], c["c_row"] - c["pm_w"][:, :L], -jnp.inf))
        c["kwf"] = jnp.exp2(wide(c["c_col"] - c["pm_last"]))
    for c in order:
        c["p"] = (c["s"] * c["w"]).astype(BF16)
        c["kw"] = (c["kwf"] * c["k"].astype(F32)).astype(BF16)
    for c in order:
        c["o1"] = _dot(c["p"], c["v_ext"])
        c["dc"] = _dot_tn(c["kw"], c["v_ext"])

    state = [c_scr[ch] for ch in range(nchain)]
    m = [m_scr[ch, 0:1, 0:1] for ch in range(nchain)]
    for ci in range(nchunk):
        cs = [items[ci, ch] for ch in range(nchain)]
        for ch, c in enumerate(cs):
            c["qc"] = _dot(c["q"], state[ch].astype(BF16))
        for ch, c in enumerate(cs):
            g_w = jnp.maximum(m[ch], c["pm_w"])
            c["e_intra"] = jnp.exp2(c["pm_w"] - g_w)
            c["e_inter"] = jnp.exp2(m[ch] - g_w)
            c["floor"] = jnp.exp2(-(c["b_w"] + g_w))
            g_last = jnp.maximum(m[ch], c["pm_last"])
            c["keep"] = jnp.exp2(m[ch] - g_last)
            c["gain"] = jnp.exp2(c["pm_last"] - g_last)
            c["m_new"] = c["b_last"] + g_last
        for ch, c in enumerate(cs):
            o = twice(c["e_intra"]) * c["o1"] + twice(c["e_inter"]) * c["qc"]
            h = o[:, :MLSTM_DH] / jnp.maximum(jnp.abs(o[:, MLSTM_DH:]), c["floor"])
            c["o_ref"][c["rs"], c["hs"]] = h.astype(BF16)
            state[ch] = c["keep"] * state[ch] + c["gain"] * c["dc"]
            m[ch] = c["m_new"]
    for ch in range(nchain):
        c_scr[ch] = state[ch]
        m_scr[ch] = jnp.broadcast_to(m[ch], m_scr.shape[1:])


def _mlstm(mq, mk, mv, g_rows, g_cols, nct):
    bsz, ttot, nb = mq.shape
    nt = ttot // TM
    per = TM // CHUNK
    fwd = lambda w: pl.BlockSpec((None, TM, w), lambda b, j: (b, j, 0))
    rev = lambda w: pl.BlockSpec((None, TM, w), lambda b, j: (b, _scan_block(j, nct, nt), 0))
    gshape = (None, per, 2 * MLSTM_HEADS, CHUNK)
    gfwd = pl.BlockSpec(gshape, lambda b, j: (b, j, 0, 0))
    grev = pl.BlockSpec(gshape, lambda b, j: (b, _scan_block(j, nct, nt), 0, 0))
    return pl.pallas_call(
        _mlstm_kernel,
        out_shape=(jax.ShapeDtypeStruct((bsz, ttot, nb), BF16),) * 2,
        grid=(bsz, nt),
        in_specs=[fwd(nb), fwd(nb), fwd(nb), gfwd, fwd(LANES), rev(nb), rev(nb), rev(nb), grev, rev(LANES)],
        out_specs=(fwd(nb), rev(nb)),
        scratch_shapes=[pltpu.VMEM((2 * MLSTM_HEADS, MLSTM_DH, 2 * MLSTM_DH), F32),
                        pltpu.VMEM((2 * MLSTM_HEADS, SUBLANES, LANES), F32)],
        compiler_params=_params("parallel", "arbitrary"),
        name="mlstm_scan",
    )(mq, mk, mv, g_rows, g_cols, mq, mk, mv, g_rows, g_cols)


def _head_rms(x, n_heads):
    w = x.shape[1] // n_heads
    return jnp.concatenate([_rms(x[:, i * w:(i + 1) * w]) for i in range(n_heads)], axis=1)


def _mid_kernel(ctx_ref, lat_ref, gate_ref, a_ref, hf_ref, hr_ref, mo_ref, z0_ref, hn_ref, wout_ref,
                sc_ref, sh_ref, g_ref, wq_ref, wk_ref, wv_ref, wga_ref, wnq_ref, wnk_ref, wnv_ref, wz_ref,
                wgate_ref, bgate_ref,
                h_out, gq_out, gk_out, gv_out, bc_out, nq_out, nk_out, nv_out, z_out, *, nct):
    f32 = lambda r: r[...].astype(F32)
    hm = _head_rms(f32(mo_ref) * (f32(hf_ref) + f32(hr_ref)), MLSTM_HEADS) * hn_ref[...]
    cat = (jnp.concatenate([f32(a_ref), hm], axis=1) * f32(z0_ref)).astype(BF16)
    h0 = jnp.where(pl.program_id(1) < nct, ctx_ref[...], lat_ref[...])
    h = h0 + gate_ref[...] * _dot(cat, wout_ref[...])
    h_out[...] = h

    ub = _modnorm(h, g_ref[...], sc_ref[...], sh_ref[...]).astype(BF16)
    gq_out[...] = (_dot(ub, wq_ref[...]) * GLA_DK ** -0.5).astype(BF16)
    gk_out[...] = _dot(ub, wk_ref[...]).astype(BF16)
    gv_out[...] = _dot(ub, wv_ref[...]).astype(BF16)
    nq_out[...] = (_dot(ub, wnq_ref[...]) * (NA_DH ** -0.5 * LOG2E)).astype(BF16)
    nk_out[...] = _dot(ub, wnk_ref[...]).astype(BF16)
    nv_out[...] = _dot(ub, wnv_ref[...]).astype(BF16)
    z_out[...] = _silu(_dot(ub, wz_ref[...])).astype(BF16)

    ga = _dot(ub, wga_ref[...]).astype(BF16)
    lg = _log_sigmoid(_dot(ga, wgate_ref[...]) + bgate_ref[...]) * (1.0 / GLA_TAU)
    hi, mid, lo = _split3(lg)
    row = lax.broadcasted_iota(jnp.int32, (TM, TM), 0)
    col = lax.broadcasted_iota(jnp.int32, (TM, TM), 1)
    same = (row // CHUNK) == (col // CHUNK)
    nk = GLA_HEADS * GLA_DK
    for d in range(2):
        tri = jnp.where(jnp.logical_and(same, (col <= row) if d == 0 else (col >= row)), 1.0, 0.0).astype(BF16)
        cols = slice(d * nk, (d + 1) * nk)
        bc_out[:, cols] = (_dot(tri, hi[:, cols]) + _dot(tri, mid[:, cols]) + _dot(tri, lo[:, cols])) * LOG2E


def _proj1_weights(w_in, w_gate, b_gate):
    nk = GLA_HEADS * GLA_DK
    nc = GLA_HEADS * GLA_DV
    nd = NA_HEADS * NA_DH
    o = np.cumsum([0, nk, nk, nc, 2 * GLA_GATE_RANK, nd, nd, nd, nc + nd])
    r = GLA_GATE_RANK
    wgate = jnp.zeros((2 * r, 2 * nk), F32).at[:r, :nk].set(w_gate[0]).at[r:, nk:].set(w_gate[1])
    bf = lambda a: a.astype(BF16)
    names = ["wq", "wk", "wv", "wga", "wnq", "wnk", "wnv", "wz"]
    w = {n: bf(w_in[:, o[i]:o[i + 1]]) for i, n in enumerate(names)}
    w["wgate"] = bf(wgate)
    w["bgate"] = b_gate.reshape(1, 2 * nk)
    return w


def _mid(ctx, x, gate0, a, hf, hr, mo, z0, h_norm, w_out, scale, shift, norm_g, w, nct):
    bsz = x.shape[0]
    ttot = ctx.shape[1] + x.shape[1]
    nt = ttot // TM
    tile, src_ctx, src_lat, _, _, mod = _tile_specs(nct, nt)
    nb = MLSTM_HEADS * MLSTM_DH
    nk = GLA_HEADS * GLA_DK
    nc = GLA_HEADS * GLA_DV
    nd = NA_HEADS * NA_DH
    wnames = ["wq", "wk", "wv", "wga", "wnq", "wnk", "wnv", "wz", "wgate", "bgate"]
    ws = [w[n] for n in wnames]
    tok = lambda width, dt: jax.ShapeDtypeStruct((bsz, ttot, width), dt)
    return pl.pallas_call(
        functools.partial(_mid_kernel, nct=nct),
        out_shape=(tok(D_MODEL, F32), tok(nk, BF16), tok(nk, BF16), tok(nc, BF16), tok(2 * nk, F32),
                   tok(nd, BF16), tok(nd, BF16), tok(nd, BF16), tok(nc + nd, BF16)),
        grid=(bsz, nt),
        in_specs=[src_ctx, src_lat, mod, tile(MLA_HEADS * MLA_V), tile(nb), tile(nb), tile(nb), tile(D_MODEL),
                  _full((1, nb)), _full(w_out.shape), mod, mod, _full((1, D_MODEL))] + [_full(a.shape) for a in ws],
        out_specs=(tile(D_MODEL), tile(nk), tile(nk), tile(nc), tile(2 * nk), tile(nd), tile(nd), tile(nd),
                   tile(nc + nd)),
        compiler_params=_params("parallel", "parallel"),
        name="out0_proj1",
    )(ctx, x, gate0, a, hf, hr, mo, z0, h_norm[None], w_out.astype(BF16), scale, shift, norm_g[None], *ws)


def _gla_kernel(qf_ref, kf_ref, vf_ref, bf_ref, qr_ref, kr_ref, vr_ref, br_ref, of_ref, or_ref, s_scr):
    @pl.when(pl.program_id(1) == 0)
    def _():
        s_scr[...] = jnp.zeros_like(s_scr)

    L = CHUNK
    nchunk = TM // L
    npair = GLA_HEADS // 2
    lane = lax.broadcasted_iota(jnp.int32, (L, LANES), 1)
    rowv = lax.broadcasted_iota(jnp.int32, (L, LANES), 0)
    srow = lax.broadcasted_iota(jnp.int32, (LANES, 2 * GLA_DV), 0)
    scol = lax.broadcasted_iota(jnp.int32, (LANES, 2 * GLA_DV), 1)
    own_block = (srow < GLA_DK) == (scol < GLA_DV)
    vcol = lax.broadcasted_iota(jnp.int32, (L, 2 * GLA_DV), 1)
    e_row = lax.broadcasted_iota(jnp.int32, (LANES, LANES), 0)
    e_col = lax.broadcasted_iota(jnp.int32, (LANES, LANES), 1)
    head_ones = jnp.where((e_row < GLA_DK) == (e_col < GLA_DK), 1.0, 0.0).astype(BF16)
    leaf_idx = lax.broadcasted_iota(jnp.int32, (GLA_LEAF * GLA_LEAF, LANES), 0)
    leaf_lane = lax.broadcasted_iota(jnp.int32, (GLA_LEAF * GLA_LEAF, LANES), 1) % L
    leaf_s, leaf_t = leaf_idx // GLA_LEAF, leaf_idx % GLA_LEAF
    rep = lambda a: jnp.concatenate(
        [jnp.broadcast_to(a[s:s + 1], (GLA_LEAF, a.shape[1])) for s in range(GLA_LEAF)], axis=0)
    til = lambda a: jnp.concatenate([a] * GLA_LEAF, axis=0)

    def chunk_step(ci, carry):
        for d, (q_ref, k_ref, v_ref, b_ref, o_ref) in enumerate(
                ((qf_ref, kf_ref, vf_ref, bf_ref, of_ref), (qr_ref, kr_ref, vr_ref, br_ref, or_ref))):
            cd = ci if d == 0 else nchunk - 1 - ci
            base = pl.multiple_of(cd * L, L)
            tau_v = rowv if d == 0 else L - 1 - rowv
            tau_s = (lane % L) if d == 0 else L - 1 - (lane % L)
            for p in range(npair):
                ls = slice(p * LANES, (p + 1) * LANES)
                q = q_ref[pl.ds(base, L), ls].astype(F32)
                k = k_ref[pl.ds(base, L), ls].astype(F32)
                bc = b_ref[pl.ds(base, L), ls]
                v = v_ref[pl.ds(base, L), p * 2 * GLA_DV:(p + 1) * 2 * GLA_DV]
                b_last = bc[L - 1:L] if d == 0 else bc[0:1]
                state = s_scr[d, p]

                o_pair = _dot((q * jnp.exp2(bc)).astype(BF16), state.astype(BF16))

                att = jnp.zeros((L, LANES), F32)
                bs = L // 2
                while bs >= GLA_LEAF:
                    later = (tau_v // bs) % 2 == 1
                    nblk = L // (2 * bs)
                    ref_b = None
                    for blk in range(nblk):
                        tau_ref = blk * 2 * bs + bs - 1
                        idx = tau_ref if d == 0 else L - 1 - tau_ref
                        r = jnp.broadcast_to(bc[idx:idx + 1], (L, LANES))
                        ref_b = r if ref_b is None else jnp.where(tau_v // (2 * bs) == blk, r, ref_b)
                    qs = jnp.where(later, q * jnp.exp2(jnp.where(later, bc - ref_b, 0.0)), 0.0).astype(BF16)
                    ks = jnp.where(later, 0.0, k * jnp.exp2(jnp.where(later, 0.0, ref_b - bc)))
                    ks2 = jnp.concatenate([jnp.where(lane < GLA_DK, ks, 0.0), jnp.where(lane < GLA_DK, 0.0, ks)],
                                          axis=0).astype(BF16)
                    same_parent = (tau_v // (2 * bs)) == (tau_s // (2 * bs))
                    att = att + jnp.where(same_parent, _dot_nt(qs, ks2), 0.0)
                    bs //= 2

                leaf = []
                for blk in range(L // GLA_LEAF):
                    rs = slice(blk * GLA_LEAF, (blk + 1) * GLA_LEAF)
                    qb, bb, kb = q[rs], bc[rs], k[rs]
                    ok = (leaf_t >= leaf_s) if d == 0 else (leaf_t <= leaf_s)
                    w = jnp.where(ok, til(qb) * jnp.exp2(til(bb) - rep(bb)) * rep(kb), 0.0)
                    a = jnp.where(leaf_lane == leaf_s + blk * GLA_LEAF, _dot(w.astype(BF16), head_ones), 0.0)
                    acc = a[0:GLA_LEAF]
                    for s in range(1, GLA_LEAF):
                        acc = acc + a[s * GLA_LEAF:(s + 1) * GLA_LEAF]
                    leaf.append(acc)
                att = (att + jnp.concatenate(leaf, axis=0)).astype(BF16)
                v_blk = jnp.concatenate([jnp.where(vcol < GLA_DV, v, jnp.zeros_like(v)),
                                         jnp.where(vcol < GLA_DV, jnp.zeros_like(v), v)], axis=0)
                o_ref[pl.ds(base, L), p * 2 * GLA_DV:(p + 1) * 2 * GLA_DV] = (o_pair + _dot(att, v_blk)).astype(BF16)

                ke = (k * jnp.exp2(b_last - bc)).astype(BF16)
                decay_col = jnp.sum(jnp.where(e_row == e_col, jnp.broadcast_to(jnp.exp2(b_last), (LANES, LANES)), 0.0),
                                    axis=1, keepdims=True)
                s_scr[d, p] = decay_col * state + jnp.where(own_block, _dot_tn(ke, v), 0.0)
        return carry

    lax.fori_loop(0, nchunk, chunk_step, 0)


def _gla(gq, gk, gv, bc, nct):
    bsz, ttot, nk = gq.shape
    nc = gv.shape[2]
    nt = ttot // TM
    fwd = lambda w: pl.BlockSpec((None, TM, w), lambda b, j: (b, j, 0))
    rev = lambda w: pl.BlockSpec((None, TM, w), lambda b, j: (b, _scan_block(j, nct, nt), 0))
    bfwd = pl.BlockSpec((None, TM, nk), lambda b, j: (b, j, 0))
    brev = pl.BlockSpec((None, TM, nk), lambda b, j: (b, _scan_block(j, nct, nt), 1))
    return pl.pallas_call(
        _gla_kernel,
        out_shape=(jax.ShapeDtypeStruct((bsz, ttot, nc), BF16),) * 2,
        grid=(bsz, nt),
        in_specs=[fwd(nk), fwd(nk), fwd(nc), bfwd, rev(nk), rev(nk), rev(nc), brev],
        out_specs=(fwd(nc), rev(nc)),
        scratch_shapes=[pltpu.VMEM((2, GLA_HEADS // 2, LANES, 2 * GLA_DV), F32)],
        compiler_params=_params("parallel", "arbitrary"),
        name="gla_scan",
    )(gq, gk, gv, bc, gq, gk, gv, bc)


def _na_bias_tables(rpb, rows):
    kh = NA_KH
    c = np.arange(GRID_W)
    cs = np.clip(c - NA_KW // 2, 0, GRID_W - NA_KW)
    col_ok = (c[None, :] >= cs[:, None]) & (c[None, :] < cs[:, None] + NA_KW)
    pad = GRID_W - NA_KW
    rpb_pad = jnp.pad(rpb.astype(F32) * LOG2E, ((0, 0), (0, 0), (pad, pad)))
    t1 = jnp.stack([rpb_pad[:, :, GRID_W - 1 - cq:2 * GRID_W - 1 - cq] for cq in range(GRID_W)], axis=2)
    t1 = jnp.where(col_ok, t1, NEG_BIG).transpose(0, 2, 1, 3)
    drows, oks = [], []
    for r0 in (0, NA_QROWS, rows - NA_QROWS):
        kb = int(np.clip(r0 - kh // 2, 0, rows - NA_KROWS))
        qr = r0 + np.arange(NA_QROWS)
        ws = np.clip(qr - kh // 2, 0, rows - kh)
        krow = kb + np.arange(NA_KROWS)
        oks.append((krow[None, :] >= ws[:, None]) & (krow[None, :] < ws[:, None] + kh))
        drows.append(np.clip(krow[None, :] - qr[:, None] + (NA_KH - 1), 0, 2 * NA_KH - 2))
    drow = np.stack(drows).reshape(-1)
    ok = np.stack(oks).reshape(1, 1, -1, 1)
    tv = jnp.where(ok, jnp.take(t1, jnp.asarray(drow), axis=2), NEG_BIG)
    tv = tv.reshape(NA_HEADS, GRID_W, 3, NA_QROWS, NA_KROWS * GRID_W).transpose(2, 0, 3, 1, 4)
    return tv.reshape(3, NA_HEADS, NA_QROWS * GRID_W, NA_KROWS * GRID_W)


def _na_kernel(q_ref, k_ref, v_ref, bias_ref, o_ref, *, n_ctx, rows):
    rb = pl.program_id(2)
    nq = NA_QROWS * GRID_W
    nk = NA_KROWS * GRID_W
    kb = jnp.clip(rb * NA_QROWS - NA_KH // 2, 0, rows - NA_KROWS)
    start = pl.multiple_of(n_ctx + kb * GRID_W, GRID_W)
    lane = lax.broadcasted_iota(jnp.int32, (nq, LANES), 1)
    heads = []
    for j in range(NA_HPS):
        ls = slice((j // 2) * LANES, (j // 2 + 1) * LANES)
        q = q_ref[:, ls]
        qj = jnp.where((lane < NA_DH) == (j % 2 == 0), q, jnp.zeros_like(q))
        heads.append(dict(ls=ls, s_loc=_dot_nt(qj, k_ref[pl.ds(start, nk), ls]), s_ctx=_dot_nt(qj, k_ref[0:n_ctx, ls])))
    for j, c in enumerate(heads):
        c["s_loc"] = c["s_loc"] + bias_ref[j]
        c["m"] = jnp.maximum(jnp.max(c["s_loc"], axis=-1, keepdims=True), jnp.max(c["s_ctx"], axis=-1, keepdims=True))
    for c in heads:
        c["p_loc"] = jnp.exp2(c["s_loc"] - c["m"])
        c["p_ctx"] = jnp.exp2(c["s_ctx"] - c["m"])
        c["den"] = jnp.sum(c["p_loc"], axis=-1, keepdims=True) + jnp.sum(c["p_ctx"], axis=-1, keepdims=True)
    for c in heads:
        c["o"] = (_dot(c["p_loc"].astype(BF16), v_ref[pl.ds(start, nk), c["ls"]])
                  + _dot(c["p_ctx"].astype(BF16), v_ref[0:n_ctx, c["ls"]]))
    outs = [c["o"] / c["den"] for c in heads]
    for j in range(NA_HPS // 2):
        o_ref[:, j * LANES:(j + 1) * LANES] = jnp.where(lane < NA_DH, outs[2 * j], outs[2 * j + 1]).astype(BF16)


def _na_attention(nq, nk, nv, bias, n_ctx, rows):
    bsz, ttot, _ = nq.shape
    nrb = rows // NA_QROWS
    qrows = NA_QROWS * GRID_W
    krows = NA_KROWS * GRID_W
    qoff = n_ctx // qrows
    hw = NA_HPS * NA_DH

    def variant(rb):
        return jnp.where(rb == 0, 0, jnp.where(rb == nrb - 1, 2, 1))

    return pl.pallas_call(
        functools.partial(_na_kernel, n_ctx=n_ctx, rows=rows),
        out_shape=jax.ShapeDtypeStruct((bsz, rows * GRID_W, NA_HEADS * NA_DH), BF16),
        grid=(bsz, NA_HEADS // NA_HPS, nrb),
        in_specs=[pl.BlockSpec((None, qrows, hw), lambda b, p, rb: (b, qoff + rb, p)),
                  pl.BlockSpec((None, ttot, hw), lambda b, p, rb: (b, 0, p)),
                  pl.BlockSpec((None, ttot, hw), lambda b, p, rb: (b, 0, p)),
                  pl.BlockSpec((None, NA_HPS, qrows, krows), lambda b, p, rb: (variant(rb), p, 0, 0))],
        out_specs=pl.BlockSpec((None, qrows, hw), lambda b, p, rb: (b, rb, p)),
        compiler_params=_params("parallel", "parallel", "parallel"),
        name="neighbourhood_attention",
    )(nq, nk, nv, bias)


def _out1_kernel(h_ref, gate_ref, of_ref, or_ref, na_ref, z_ref, gn_ref, w_ref, fn_ref, o_ref):
    f32 = lambda r: r[...].astype(F32)
    g = _head_rms(f32(of_ref) + f32(or_ref), GLA_HEADS) * gn_ref[...]
    cat = (jnp.concatenate([g, f32(na_ref)], axis=1) * f32(z_ref)).astype(BF16)
    h = h_ref[...] + gate_ref[...] * _dot(cat, w_ref[...])
    o_ref[...] = _rms(h) * fn_ref[...]


def _out1(h, gate, of, orv, na, z, gla_norm, w_out, final_norm, nct):
    bsz, ttot, _ = h.shape
    seq = na.shape[1]
    nc = GLA_HEADS * GLA_DV
    nd = NA_HEADS * NA_DH
    lat = lambda w: pl.BlockSpec((None, TM, w), lambda b, t: (b, t + nct, 0))
    return pl.pallas_call(
        _out1_kernel,
        out_shape=jax.ShapeDtypeStruct((bsz, seq, D_MODEL), F32),
        grid=(bsz, seq // TM),
        in_specs=[lat(D_MODEL), pl.BlockSpec((None, None, 1, D_MODEL), lambda b, t: (b, 1, 0, 0)),
                  lat(nc), lat(nc), pl.BlockSpec((None, TM, nd), lambda b, t: (b, t, 0)), lat(nc + nd),
                  _full((1, nc)), _full(w_out.shape), _full((1, D_MODEL))],
        out_specs=pl.BlockSpec((None, TM, D_MODEL), lambda b, t: (b, t, 0)),
        compiler_params=_params("parallel", "parallel"),
        name="out1",
    )(h, gate, of, orv, na, z, gla_norm[None], w_out.astype(BF16), final_norm[None])


def kernel(x, c, ctx, c_ctx, l0_norm, l0_w_mod, l0_b_mod, l0_w_in, l0_mla_q_norm, l0_mla_w_uq, l0_mla_kv_norm, l0_mla_w_ukv, l0_mlstm_conv_w, l0_mlstm_conv_b, l0_mlstm_b_i, l0_mlstm_b_f, l0_mlstm_norm, l0_w_out, l1_norm, l1_w_mod, l1_b_mod, l1_w_in, l1_gla_w_gate, l1_gla_b_gate, l1_gla_norm, l1_na_rpb, l1_w_out, final_norm):
    bsz, seq, d = x.shape
    n_ctx = ctx.shape[1]
    rows = seq // GRID_W
    assert d == D_MODEL and seq % TM == 0 and n_ctx == TM
    assert rows % NA_QROWS == 0 and rows >= NA_KROWS and rows // NA_QROWS >= 3
    nct = n_ctx // TM

    shift, scale, gate0 = _mod_vectors(c, c_ctx, l0_w_mod, l0_b_mod)
    cos_t, sin_t = _rope_tables(n_ctx, seq)
    w0 = _proj0_weights(l0_w_in, l0_mla_q_norm, l0_mla_w_uq, l0_mla_kv_norm, l0_mla_w_ukv,
                        l0_mlstm_conv_w, l0_mlstm_conv_b, l0_mlstm_b_i, l0_mlstm_b_f)
    q, k, v, mq, mk, mv, g_rows, g_cols, mo, z0 = _proj0(ctx, x, scale, shift, l0_norm, cos_t, sin_t, w0, nct)
    a = _mla_attention(q, k, v, nct)
    hf, hr = _mlstm(mq, mk, mv, g_rows, g_cols, nct)

    shift, scale, gate = _mod_vectors(c, c_ctx, l1_w_mod, l1_b_mod)
    w1 = _proj1_weights(l1_w_in, l1_gla_w_gate, l1_gla_b_gate)
    h, gq, gk, gv, bc, nq, nk, nv, z = _mid(ctx, x, gate0, a, hf, hr, mo, z0, l0_mlstm_norm, l0_w_out,
                                            scale, shift, l1_norm, w1, nct)
    of, orv = _gla(gq, gk, gv, bc, nct)
    na = _na_attention(nq, nk, nv, _na_bias_tables(l1_na_rpb, rows), n_ctx, rows)
    return _out1(h, gate, of, orv, na, z, l1_gla_norm, l1_w_out, final_norm, nct)
```

```python
import functools

import jax
import jax.numpy as jnp
import numpy as np
from jax import lax
from jax.experimental import pallas as pl
from jax.experimental.pallas import tpu as pltpu

F32 = jnp.float32
BF16 = jnp.bfloat16

D_MODEL = 1024
GRID_W = 64
EPS = 1e-6
ROPE_BASE = 10000.0

MLA_HEADS = 8
MLA_Q_RANK = 384
MLA_KV_RANK = 256
MLA_NOPE = 64
MLA_ROPE = 32
MLA_V = 64
ROPE_PAIRS = MLA_ROPE // 4
MLSTM_HEADS = 4
MLSTM_DH = 128
MLSTM_CONV = 3
GLA_HEADS = 4
GLA_DK = 64
GLA_DV = 128
GLA_GATE_RANK = 16
GLA_TAU = 16.0
NA_HEADS = 8
NA_DH = 64
NA_KH = 8
NA_KW = 16

CHUNK = 64
TM = 256
LANES = 128
SUBLANES = 8
HALO = SUBLANES
NA_QROWS = 4
NA_KROWS = NA_QROWS + NA_KH - 1
GLA_LEAF = 8
MLA_HPS = 4
NA_HPS = 4
VMEM_LIMIT = 56 * 1024 * 1024
NEG_BIG = -1e30
LOG2E = 1.4426950408889634


def _dot(a, b):
    return jnp.dot(a, b, preferred_element_type=F32)


def _dot_nt(a, b):
    return lax.dot_general(a, b, (((1,), (1,)), ((), ())), preferred_element_type=F32)


def _dot_tn(a, b):
    return lax.dot_general(a, b, (((0,), (0,)), ((), ())), preferred_element_type=F32)


def _rms(x):
    return x * lax.rsqrt(jnp.mean(x * x, axis=-1, keepdims=True) + EPS)


def _sigmoid(x):
    return 1.0 / (1.0 + jnp.exp(-x))


def _silu(x):
    return x * _sigmoid(x)


def _log_sigmoid(x):
    return jnp.minimum(x, 0.0) - jnp.log(1.0 + jnp.exp(-jnp.abs(x)))


def _params(*sem):
    return pltpu.CompilerParams(dimension_semantics=sem, vmem_limit_bytes=VMEM_LIMIT)


def _full(shape):
    nd = len(shape)
    return pl.BlockSpec(shape, lambda *_: (0,) * nd)


def _mod_kernel(c_ref, w_ref, b_ref, o_ref):
    o_ref[...] = _dot(_silu(c_ref[...]).astype(BF16), w_ref[...]) + b_ref[...]


def _modulation(cc, w_mod, b_mod):
    rows, d = cc.shape
    n = w_mod.shape[1]
    return pl.pallas_call(
        _mod_kernel,
        out_shape=jax.ShapeDtypeStruct((rows, n), F32),
        grid=(n // d,),
        in_specs=[_full((rows, d)), pl.BlockSpec((d, d), lambda j: (0, j)), pl.BlockSpec((1, d), lambda j: (0, j))],
        out_specs=pl.BlockSpec((rows, d), lambda j: (0, j)),
        compiler_params=_params("parallel"),
        name="modulation",
    )(cc, w_mod.astype(BF16), b_mod[None])


def _mod_vectors(c, c_ctx, w_mod, b_mod):
    bsz = c.shape[0]
    rows = -(-(bsz + 1) // SUBLANES) * SUBLANES
    cc = jnp.zeros((rows, D_MODEL), F32).at[:bsz].set(c).at[bsz].set(c_ctx)
    mod = _modulation(cc, w_mod, b_mod)
    out = []
    for part in jnp.split(mod, 3, axis=-1):
        ctx_v = jnp.broadcast_to(part[bsz][None], (bsz, D_MODEL))
        out.append(jnp.stack([ctx_v, part[:bsz]], axis=1)[:, :, None, :])
    return out


def _modnorm(x, g, scale, shift):
    return _rms(x) * g * (1.0 + scale) + shift


def _seg_scan(x, reverse, op, fill):
    width = x.shape[1]
    pos = lax.broadcasted_iota(jnp.int32, x.shape, 1) % CHUNK
    k = 1
    while k < CHUNK:
        if reverse:
            x = op(x, jnp.where(pos < CHUNK - k, pltpu.roll(x, width - k, axis=1), fill))
        else:
            x = op(x, jnp.where(pos >= k, pltpu.roll(x, k, axis=1), fill))
        k *= 2
    return x


def _split3(x):
    hi = x.astype(BF16)
    r1 = x - hi.astype(F32)
    mid = r1.astype(BF16)
    return hi, mid, (r1 - mid.astype(F32)).astype(BF16)


def _proj0_kernel(ctx_ref, lat_ref, hp_ref, hn_ref, sc_ref, sh_ref, g_ref, cos_ref, sin_ref,
                  wcq_ref, wckv_ref, wkr_ref, wmqk_ref, wmv_ref, wmo_ref, wgt_ref, wz_ref,
                  qn_ref, wuq_ref, kvn_ref, wk_ref, wv_ref, cw_ref, cb_ref, gb_ref,
                  q_out, k_out, v_out, mq_out, mk_out, mv_out, gr_out, gc_out, mo_out, z_out,
                  pbuf, *, nct, nt):
    t = pl.program_id(1)
    h = jnp.where(t < nct, ctx_ref[...], lat_ref[...])
    x = jnp.concatenate([hp_ref[...], h, hn_ref[...]], axis=0)
    ub_ext = _modnorm(x, g_ref[...], sc_ref[...], sh_ref[...]).astype(BF16)
    ub = ub_ext[HALO:HALO + TM]

    gt = _dot_nt(wgt_ref[...], ub) + gb_ref[...]
    half = MLSTM_HEADS
    per_dir = []
    for d in range(2):
        gd = gt[d * SUBLANES:(d + 1) * SUBLANES]
        bcum = _seg_scan(_log_sigmoid(gd), d == 1, jnp.add, 0.0)
        b_top = pltpu.roll(bcum, half, axis=0)
        c8 = gd - b_top
        pm8 = _seg_scan(c8, d == 1, jnp.maximum, -jnp.inf)
        per_dir.append((c8, pm8, b_top))

    cos = cos_ref[...]
    sin = sin_ref[...]
    cos_all = jnp.concatenate([cos] * MLA_HEADS, axis=1)
    sin_all = jnp.concatenate([sin] * MLA_HEADS, axis=1)

    def swap_halves(a):
        lane = lax.broadcasted_iota(jnp.int32, a.shape, 1)
        first = lane % (2 * ROPE_PAIRS) < ROPE_PAIRS
        return jnp.where(first, pltpu.roll(a, a.shape[1] - ROPE_PAIRS, axis=1), pltpu.roll(a, ROPE_PAIRS, axis=1))

    cq = _dot(ub, wcq_ref[...])
    ckv = _dot(ub, wckv_ref[...])
    kr = _dot(ub, wkr_ref[...])
    pqk = _dot(ub_ext, wmqk_ref[...])

    nq = (_rms(cq) * qn_ref[...]).astype(BF16)
    nkv = (_rms(ckv) * kvn_ref[...]).astype(BF16)

    q = _dot(nq, wuq_ref[...])
    kn = _dot(nkv, wk_ref[...])
    vv = _dot(nkv, wv_ref[...])

    prev_ok = t > nct
    next_ok = jnp.logical_and(t >= nct, t != nt - 1)
    row = lax.broadcasted_iota(jnp.int32, pqk.shape, 0)
    keep = jnp.logical_and(jnp.logical_or(row >= HALO, prev_ok), jnp.logical_or(row < HALO + TM, next_ok))
    pbuf[...] = jnp.where(keep, pqk, 0.0)
    cw = cw_ref[...]
    y = (pbuf[HALO - 1:HALO - 1 + TM, :] * cw[0:1] + pbuf[HALO:HALO + TM, :] * cw[1:2]
         + pbuf[HALO + 1:HALO + 1 + TM, :] * cw[2:3] + cb_ref[...])
    y = _silu(y)
    nb = MLSTM_HEADS * MLSTM_DH
    mq_out[...] = (y[:, :nb] * MLSTM_DH ** -0.5).astype(BF16)
    mk_out[...] = y[:, nb:].astype(BF16)

    mv = _dot(ub, wmv_ref[...])
    mo = _dot(ub, wmo_ref[...])

    q = q * cos_all + swap_halves(q) * sin_all
    q_out[...] = (q * ((MLA_NOPE + MLA_ROPE) ** -0.5 * LOG2E)).astype(BF16)
    rot = kr * cos + swap_halves(kr) * sin
    k_out[...] = (kn + jnp.concatenate([rot] * MLA_HEADS, axis=1)).astype(BF16)
    vlane = lax.broadcasted_iota(jnp.int32, (TM, MLA_HEADS * LANES), 1)
    is_value = ((vlane % LANES) < MLA_V) == ((vlane // LANES) % 2 == 0)
    v_out[...] = jnp.where(is_value, vv, 1.0).astype(BF16)

    zz = _dot(ub, wz_ref[...])
    mv_out[...] = mv.astype(BF16)
    mo_out[...] = _sigmoid(mo).astype(BF16)
    z_out[...] = _silu(zz).astype(BF16)

    r8 =lax.broadcasted_iota(jnp.int32, (SUBLANES, TM), 0)
    both = lambda i: jnp.where(r8 < half, per_dir[0][i], pltpu.roll(per_dir[1][i], half, axis=0)) * LOG2E
    c_rows, pm_rows, b_rows = both(0), both(1), both(2)
    for ci in range(TM // CHUNK):
        gr_out[ci] = c_rows[:, ci * CHUNK:(ci + 1) * CHUNK]
    stack = jnp.concatenate([c_rows, pm_rows, b_rows, jnp.zeros((LANES - 3 * SUBLANES, TM), F32)], axis=0)
    er = lax.broadcasted_iota(jnp.int32, (TM, TM), 0)
    ec = lax.broadcasted_iota(jnp.int32, (TM, TM), 1)
    eye = jnp.where(er == ec, 1.0, 0.0).astype(BF16)
    gc_out[...] = sum(_dot_nt(eye, part) for part in _split3(stack))


def _rope_tables(n_ctx, seq):
    t = jnp.arange(seq)
    inv = 1.0 / (ROPE_BASE ** (jnp.arange(ROPE_PAIRS, dtype=F32) / ROPE_PAIRS))
    ang = jnp.concatenate([(t // GRID_W)[:, None] * inv, (t % GRID_W)[:, None] * inv], axis=-1)
    cos, sin = jnp.cos(ang), jnp.sin(ang)
    j = np.arange(MLA_ROPE)
    src = (j // (2 * ROPE_PAIRS)) * ROPE_PAIRS + (j % ROPE_PAIRS)
    sign = np.where((j % (2 * ROPE_PAIRS)) < ROPE_PAIRS, -1.0, 1.0).astype(np.float32)
    cos_full = cos[:, src]
    sin_full = sin[:, src] * sign
    cos_t = jnp.ones((n_ctx + seq, LANES), F32).at[n_ctx:, MLA_NOPE:MLA_NOPE + MLA_ROPE].set(cos_full)
    sin_t = jnp.zeros((n_ctx + seq, LANES), F32).at[n_ctx:, MLA_NOPE:MLA_NOPE + MLA_ROPE].set(sin_full)
    return cos_t, sin_t


def _proj0_weights(w_in, q_norm, w_uq, kv_norm, w_ukv, conv_w, conv_b, b_i, b_f):
    o = np.cumsum([0, MLA_Q_RANK, MLA_KV_RANK, MLA_ROPE] + [MLSTM_HEADS * MLSTM_DH] * 4 + [4 * MLSTM_HEADS, D_MODEL])
    dq = MLA_NOPE + MLA_ROPE
    wkr = jnp.pad(w_in[:, o[2]:o[3]], ((0, 0), (MLA_NOPE, LANES - dq)))
    wuq = jnp.pad(w_uq.reshape(MLA_Q_RANK, MLA_HEADS, dq), ((0, 0), (0, 0), (0, LANES - dq)))
    wkv = w_ukv.reshape(MLA_KV_RANK, MLA_HEADS, MLA_NOPE + MLA_V)
    wk = jnp.pad(wkv[:, :, :MLA_NOPE], ((0, 0), (0, 0), (0, LANES - MLA_NOPE)))
    zv = jnp.zeros((MLA_KV_RANK, MLA_HEADS // 2, MLA_V), F32)
    wv = jnp.concatenate([wkv[:, 0::2, MLA_NOPE:], zv, zv, wkv[:, 1::2, MLA_NOPE:]], axis=-1)
    wv = wv.reshape(MLA_KV_RANK, MLA_HEADS * LANES)
    gbias = jnp.stack([b_i, b_f], axis=1).reshape(4 * MLSTM_HEADS, 1)
    bf = lambda a: a.astype(BF16)
    return dict(
        wcq=bf(w_in[:, o[0]:o[1]]), wckv=bf(w_in[:, o[1]:o[2]]),
        wkr=bf(wkr),
        wmqk=bf(w_in[:, o[3]:o[5]]), wmv=bf(w_in[:, o[5]:o[6]]), wmo=bf(w_in[:, o[6]:o[7]]),
        wgt=bf(w_in[:, o[7]:o[8]].T), wz=bf(w_in[:, o[8]:o[9]]),
        qn=q_norm[None], wuq=bf(wuq.reshape(MLA_Q_RANK, -1)),
        kvn=kv_norm[None], wk=bf(wk.reshape(MLA_KV_RANK, -1)), wv=bf(wv),
        cw=jnp.zeros((SUBLANES, conv_w.shape[1]), F32).at[:MLSTM_CONV].set(conv_w), cb=conv_b[None], gb=gbias)


def _tile_specs(nct, nt):
    tile = lambda w: pl.BlockSpec((None, TM, w), lambda b, t: (b, t, 0))
    per = TM // HALO
    nlat = nt - nct
    src_ctx = pl.BlockSpec((None, TM, D_MODEL), lambda b, t: (b, jnp.minimum(t, nct - 1), 0))
    src_lat = pl.BlockSpec((None, TM, D_MODEL), lambda b, t: (b, jnp.maximum(t - nct, 0), 0))
    prev = pl.BlockSpec((None, HALO, D_MODEL), lambda b, t: (b, jnp.maximum((t - nct) * per - 1, 0), 0))
    nxt = pl.BlockSpec((None, HALO, D_MODEL),
                       lambda b, t: (b, jnp.clip((t - nct + 1) * per, 0, nlat * per - 1), 0))
    mod = pl.BlockSpec((None, None, 1, D_MODEL), lambda b, t: (b, (t >= nct).astype(jnp.int32), 0, 0))
    return tile, src_ctx, src_lat, prev, nxt, mod


def _proj0(ctx, x, scale, shift, norm_g, cos_t, sin_t, w, nct):
    bsz = x.shape[0]
    ttot = ctx.shape[1] + x.shape[1]
    nt = ttot // TM
    nb = MLSTM_HEADS * MLSTM_DH
    tile, src_ctx, src_lat, prev, nxt, mod = _tile_specs(nct, nt)
    tab = pl.BlockSpec((TM, LANES), lambda b, t: (t, 0))
    wnames = ["wcq", "wckv", "wkr", "wmqk", "wmv", "wmo", "wgt", "wz",
              "qn", "wuq", "kvn", "wk", "wv", "cw", "cb", "gb"]
    ws = [w[n] for n in wnames]
    tok = lambda width, dt: jax.ShapeDtypeStruct((bsz, ttot, width), dt)
    out_shape = (tok(MLA_HEADS * LANES, BF16), tok(MLA_HEADS * LANES, BF16), tok(MLA_HEADS * LANES, BF16),
                 tok(nb, BF16), tok(nb, BF16), tok(nb, BF16),
                 jax.ShapeDtypeStruct((bsz, ttot // CHUNK, 2 * MLSTM_HEADS, CHUNK), F32), tok(LANES, F32),
                 tok(nb, BF16), tok(D_MODEL, BF16))
    out_specs = (tile(MLA_HEADS * LANES), tile(MLA_HEADS * LANES), tile(MLA_HEADS * LANES),
                 tile(nb), tile(nb), tile(nb),
                 pl.BlockSpec((None, TM // CHUNK, 2 * MLSTM_HEADS, CHUNK), lambda b, t: (b, t, 0, 0)), tile(LANES),
                 tile(nb), tile(D_MODEL))
    return pl.pallas_call(
        functools.partial(_proj0_kernel, nct=nct, nt=nt),
        out_shape=out_shape,
        grid=(bsz, nt),
        in_specs=[src_ctx, src_lat, prev, nxt, mod, mod, _full((1, D_MODEL)), tab, tab] + [_full(a.shape) for a in ws],
        out_specs=out_specs,
        scratch_shapes=[pltpu.VMEM((TM + 2 * HALO, 2 * nb), F32)],
        compiler_params=_params("parallel", "parallel"),
        name="proj0",
    )(ctx, x, x, x, scale, shift, norm_g[None], cos_t, sin_t, *ws)


def _mla_kernel(q_ref, k_ref, v_ref, o_ref, *, nct):
    t = pl.program_id(2)
    lane = lax.broadcasted_iota(jnp.int32, (TM, LANES), 1)

    def attend(nkeys):
        qk = lambda j: _dot_nt(q_ref[:, j * LANES:(j + 1) * LANES], k_ref[0:nkeys, j * LANES:(j + 1) * LANES])
        outs = []
        s_next = qk(0)
        for j in range(MLA_HPS):
            s = s_next
            if j + 1 < MLA_HPS:
                s_next = qk(j + 1)
            p = jnp.exp2(s - jnp.max(s, axis=-1, keepdims=True))
            o = _dot(p.astype(BF16), v_ref[0:nkeys, j * LANES:(j + 1) * LANES])
            outs.append(o / pltpu.roll(o, MLA_V, axis=1))
        for j in range(MLA_HPS // 2):
            o_ref[:, j * LANES:(j + 1) * LANES] = jnp.where(lane < MLA_V, outs[2 * j], outs[2 * j + 1]).astype(BF16)

    @pl.when(t < nct)
    def _():
        attend(nct * TM)

    @pl.when(t >= nct)
    def _():
        attend(k_ref.shape[0])


def _mla_attention(q, k, v, nct):
    bsz, ttot, _ = q.shape
    nt = ttot // TM
    hw = MLA_HPS * LANES
    return pl.pallas_call(
        functools.partial(_mla_kernel, nct=nct),
        out_shape=jax.ShapeDtypeStruct((bsz, ttot, MLA_HEADS * MLA_V), BF16),
        grid=(bsz, MLA_HEADS // MLA_HPS, nt),
        in_specs=[pl.BlockSpec((None, TM, hw), lambda b, p, t: (b, t, p)),
                  pl.BlockSpec((None, ttot, hw), lambda b, p, t: (b, 0, p)),
                  pl.BlockSpec((None, ttot, hw), lambda b, p, t: (b, 0, p))],
        out_specs=pl.BlockSpec((None, TM, hw // 2), lambda b, p, t: (b, t, p)),
        compiler_params=_params("parallel", "parallel", "parallel"),
        name="mla_attention",
    )(q, k, v)


def _scan_block(j, nct, nt):
    return jnp.where(j < nct, nct - 1 - j, nt - 1 - (j - nct))


def _mlstm_kernel(qf_ref, kf_ref, vf_ref, grf_ref, gcf_ref, qr_ref, kr_ref, vr_ref, grr_ref, gcr_ref,
                  hf_ref, hr_ref, c_scr, m_scr):
    @pl.when(pl.program_id(1) == 0)
    def _():
        c_scr[...] = jnp.zeros_like(c_scr)
        m_scr[...] = jnp.zeros_like(m_scr)

    L = CHUNK
    nchunk = TM // L
    nh = MLSTM_HEADS
    nchain = 2 * nh
    row = lax.broadcasted_iota(jnp.int32, (L, L), 0)
    col = lax.broadcasted_iota(jnp.int32, (L, L), 1)
    ones = jnp.ones((L, MLSTM_DH), BF16)
    wide = lambda a: jnp.broadcast_to(a, (L, MLSTM_DH))
    twice = lambda a: jnp.concatenate([a, a], axis=1)
    refs = ((qf_ref, kf_ref, vf_ref, grf_ref, gcf_ref, hf_ref), (qr_ref, kr_ref, vr_ref, grr_ref, gcr_ref, hr_ref))

    def load(ci):
        cs = []
        for d, (q_ref, k_ref, v_ref, gr_ref, gc_ref, o_ref) in enumerate(refs):
            cd = ci if d == 0 else nchunk - 1 - ci
            rs = slice(cd * L, (cd + 1) * L)
            g_rows = gr_ref[cd]
            g_cols = gc_ref[rs, :]
            last = L - 1 if d == 0 else 0
            for hd in range(nh):
                ch = d * nh + hd
                hs = slice(hd * MLSTM_DH, (hd + 1) * MLSTM_DH)
                cs.append(dict(
                    rs=rs, hs=hs, o_ref=o_ref, causal=(col <= row) if d == 0 else (col >= row),
                    c_row=g_rows[ch:ch + 1], c_col=g_cols[:, ch:ch + 1],
                    pm_col=g_cols[:, nchain + ch:nchain + ch + 1], b_col=g_cols[:, 2 * nchain + ch:2 * nchain + ch + 1],
                    pm_last=g_cols[last:last + 1, nchain + ch:nchain + ch + 1],
                    b_last=g_cols[last:last + 1, 2 * nchain + ch:2 * nchain + ch + 1],
                    q=q_ref[rs, hs], k=k_ref[rs, hs],
                    v_ext=jnp.concatenate([v_ref[rs, hs], ones], axis=1)))
        return cs

    def prep_scores(cs):
        for c in cs:
            c["s"] = _dot_nt(c["q"], c["k"])

    def prep_weights(cs):
        for c in cs:
            c["pm_w"] = wide(c["pm_col"])
            c["b_w"] = wide(c["b_col"])
            c["w"] = jnp.exp2(jnp.where(c["causal"], c["c_row"] - c["pm_w"][:, :L], -jnp.inf))
            c["kwf"] = jnp.exp2(wide(c["c_col"] - c["pm_last"]))

    def prep_operands(cs):
        for c in cs:
            c["p"] = (c["s"] * c["w"]).astype(BF16)
            c["kw"] = (c["kwf"] * c["k"].astype(F32)).astype(BF16)

    def prep_products(cs):
        for c in cs:
            c["o1"] = _dot(c["p"], c["v_ext"])
            c["dc"] = _dot_tn(c["kw"], c["v_ext"])

    def scan_read(cs):
        for ch, c in enumerate(cs):
            c["qc"] = _dot(c["q"], state[ch].astype(BF16))

    def scan_factors(cs):
        for ch, c in enumerate(cs):
            g_w = jnp.maximum(m[ch], c["pm_w"])
            c["e_intra"] = jnp.exp2(c["pm_w"] - g_w)
            c["e_inter"] = jnp.exp2(m[ch] - g_w)
            c["floor"] = jnp.exp2(-(c["b_w"] + g_w))
            g_last = jnp.maximum(m[ch], c["pm_last"])
            c["keep"] = jnp.exp2(m[ch] - g_last)
            c["gain"] = jnp.exp2(c["pm_last"] - g_last)
            c["m_new"] = c["b_last"] + g_last

    def scan_update(cs):
        for ch, c in enumerate(cs):
            o = twice(c["e_intra"]) * c["o1"] + twice(c["e_inter"]) * c["qc"]
            h = o[:, :MLSTM_DH] / jnp.maximum(jnp.abs(o[:, MLSTM_DH:]), c["floor"])
            c["o_ref"][c["rs"], c["hs"]] = h.astype(BF16)
            state[ch] = c["keep"] * state[ch] + c["gain"] * c["dc"]
            m[ch] = c["m_new"]

    state = [c_scr[ch] for ch in range(nchain)]
    m = [m_scr[ch, 0:1, 0:1] for ch in range(nchain)]
    chunks = [load(ci) for ci in range(nchunk)]
    everything = [c for cs in chunks for c in cs]
    for stage in (prep_scores, prep_weights, prep_operands):
        stage(everything)
    prep_products(chunks[0])
    for i, cs in enumerate(chunks):
        scan_read(cs)
        if i + 1 < nchunk:
            prep_products(chunks[i + 1])
        scan_factors(cs)
        scan_update(cs)
    for ch in range(nchain):
        c_scr[ch] = state[ch]
        m_scr[ch] = jnp.broadcast_to(m[ch], m_scr.shape[1:])


def _mlstm(mq, mk, mv, g_rows, g_cols, nct):
    bsz, ttot, nb = mq.shape
    nt = ttot // TM
    per = TM // CHUNK
    fwd = lambda w: pl.BlockSpec((None, TM, w), lambda b, j: (b, j, 0))
    rev = lambda w: pl.BlockSpec((None, TM, w), lambda b, j: (b, _scan_block(j, nct, nt), 0))
    gshape = (None, per, 2 * MLSTM_HEADS, CHUNK)
    gfwd = pl.BlockSpec(gshape, lambda b, j: (b, j, 0, 0))
    grev = pl.BlockSpec(gshape, lambda b, j: (b, _scan_block(j, nct, nt), 0, 0))
    return pl.pallas_call(
        _mlstm_kernel,
        out_shape=(jax.ShapeDtypeStruct((bsz, ttot, nb), BF16),) * 2,
        grid=(bsz, nt),
        in_specs=[fwd(nb), fwd(nb), fwd(nb), gfwd, fwd(LANES), rev(nb), rev(nb), rev(nb), grev, rev(LANES)],
        out_specs=(fwd(nb), rev(nb)),
        scratch_shapes=[pltpu.VMEM((2 * MLSTM_HEADS, MLSTM_DH, 2 * MLSTM_DH), F32),
                        pltpu.VMEM((2 * MLSTM_HEADS, SUBLANES, LANES), F32)],
        compiler_params=_params("parallel", "arbitrary"),
        name="mlstm_scan",
    )(mq, mk, mv, g_rows, g_cols, mq, mk, mv, g_rows, g_cols)


def _head_rms(x, n_heads):
    w = x.shape[1] // n_heads
    return jnp.concatenate([_rms(x[:, i * w:(i + 1) * w]) for i in range(n_heads)], axis=1)


def _mid_kernel(ctx_ref, lat_ref, gate_ref, a_ref, hf_ref, hr_ref, mo_ref, z0_ref, hn_ref, wout_ref,
                sc_ref, sh_ref, g_ref, wq_ref, wk_ref, wv_ref, wga_ref, wnq_ref, wnk_ref, wnv_ref, wz_ref,
                wgate_ref, bgate_ref,
                h_out, gq_out, gk_out, gv_out, bc_out, nq_out, nk_out, nv_out, z_out, *, nct):
    f32 = lambda r: r[...].astype(F32)
    hm = _head_rms(f32(mo_ref) * (f32(hf_ref) + f32(hr_ref)), MLSTM_HEADS) * hn_ref[...]
    cat = (jnp.concatenate([f32(a_ref), hm], axis=1) * f32(z0_ref)).astype(BF16)
    h0 = jnp.where(pl.program_id(1) < nct, ctx_ref[...], lat_ref[...])
    h = h0 + gate_ref[...] * _dot(cat, wout_ref[...])
    h_out[...] = h

    ub = _modnorm(h, g_ref[...], sc_ref[...], sh_ref[...]).astype(BF16)
    ga = _dot(ub, wga_ref[...]).astype(BF16)
    gate_pre = _dot(ga, wgate_ref[...])
    gq_out[...] = (_dot(ub, wq_ref[...]) * GLA_DK ** -0.5).astype(BF16)
    gk_out[...] = _dot(ub, wk_ref[...]).astype(BF16)
    gv_out[...] = _dot(ub, wv_ref[...]).astype(BF16)

    lg = _log_sigmoid(gate_pre + bgate_ref[...]) * (1.0 / GLA_TAU)
    hi, mid, lo = _split3(lg)
    nq_out[...] = (_dot(ub, wnq_ref[...]) * (NA_DH ** -0.5 * LOG2E)).astype(BF16)
    nk_out[...] = _dot(ub, wnk_ref[...]).astype(BF16)
    nv_out[...] = _dot(ub, wnv_ref[...]).astype(BF16)
    row = lax.broadcasted_iota(jnp.int32, (TM, TM), 0)
    col = lax.broadcasted_iota(jnp.int32, (TM, TM), 1)
    same = (row // CHUNK) == (col // CHUNK)
    nk = GLA_HEADS * GLA_DK
    for d in range(2):
        tri = jnp.where(jnp.logical_and(same, (col <= row) if d == 0 else (col >= row)), 1.0, 0.0).astype(BF16)
        cols = slice(d * nk, (d + 1) * nk)
        bc_out[:, cols] = (_dot(tri, hi[:, cols]) + _dot(tri, mid[:, cols]) + _dot(tri, lo[:, cols])) * LOG2E
    z_out[...] = _silu(_dot(ub, wz_ref[...])).astype(BF16)


def _proj1_weights(w_in, w_gate, b_gate):
    nk = GLA_HEADS * GLA_DK
    nc = GLA_HEADS * GLA_DV
    nd = NA_HEADS * NA_DH
    o = np.cumsum([0, nk, nk, nc, 2 * GLA_GATE_RANK, nd, nd, nd, nc + nd])
    r = GLA_GATE_RANK
    wgate = jnp.zeros((2 * r, 2 * nk), F32).at[:r, :nk].set(w_gate[0]).at[r:, nk:].set(w_gate[1])
    bf = lambda a: a.astype(BF16)
    names = ["wq", "wk", "wv", "wga", "wnq", "wnk", "wnv", "wz"]
    w = {n: bf(w_in[:, o[i]:o[i + 1]]) for i, n in enumerate(names)}
    w["wgate"] = bf(wgate)
    w["bgate"] = b_gate.reshape(1, 2 * nk)
    return w


def _mid(ctx, x, gate0, a, hf, hr, mo, z0, h_norm, w_out, scale, shift, norm_g, w, nct):
    bsz = x.shape[0]
    ttot = ctx.shape[1] + x.shape[1]
    nt = ttot // TM
    tile, src_ctx, src_lat, _, _, mod = _tile_specs(nct, nt)
    nb = MLSTM_HEADS * MLSTM_DH
    nk = GLA_HEADS * GLA_DK
    nc = GLA_HEADS * GLA_DV
    nd = NA_HEADS * NA_DH
    wnames = ["wq", "wk", "wv", "wga", "wnq", "wnk", "wnv", "wz", "wgate", "bgate"]
    ws = [w[n] for n in wnames]
    tok = lambda width, dt: jax.ShapeDtypeStruct((bsz, ttot, width), dt)
    return pl.pallas_call(
        functools.partial(_mid_kernel, nct=nct),
        out_shape=(tok(D_MODEL, F32), tok(nk, BF16), tok(nk, BF16), tok(nc, BF16), tok(2 * nk, F32),
                   tok(nd, BF16), tok(nd, BF16), tok(nd, BF16), tok(nc + nd, BF16)),
        grid=(bsz, nt),
        in_specs=[src_ctx, src_lat, mod, tile(MLA_HEADS * MLA_V), tile(nb), tile(nb), tile(nb), tile(D_MODEL),
                  _full((1, nb)), _full(w_out.shape), mod, mod, _full((1, D_MODEL))] + [_full(a.shape) for a in ws],
        out_specs=(tile(D_MODEL), tile(nk), tile(nk), tile(nc), tile(2 * nk), tile(nd), tile(nd), tile(nd),
                   tile(nc + nd)),
        compiler_params=_params("parallel", "parallel"),
        name="out0_proj1",
    )(ctx, x, gate0, a, hf, hr, mo, z0, h_norm[None], w_out.astype(BF16), scale, shift, norm_g[None], *ws)


def _gla_kernel(qf_ref, kf_ref, vf_ref, bf_ref, qr_ref, kr_ref, vr_ref, br_ref, of_ref, or_ref, s_scr):
    @pl.when(pl.program_id(1) == 0)
    def _():
        s_scr[...] = jnp.zeros_like(s_scr)

    L = CHUNK
    nchunk = TM // L
    npair = GLA_HEADS // 2
    lane = lax.broadcasted_iota(jnp.int32, (L, LANES), 1)
    rowv = lax.broadcasted_iota(jnp.int32, (L, LANES), 0)
    srow = lax.broadcasted_iota(jnp.int32, (LANES, 2 * GLA_DV), 0)
    scol = lax.broadcasted_iota(jnp.int32, (LANES, 2 * GLA_DV), 1)
    own_block = (srow < GLA_DK) == (scol < GLA_DV)
    vcol = lax.broadcasted_iota(jnp.int32, (L, 2 * GLA_DV), 1)
    e_row = lax.broadcasted_iota(jnp.int32, (LANES, LANES), 0)
    e_col = lax.broadcasted_iota(jnp.int32, (LANES, LANES), 1)
    head_ones = jnp.where((e_row < GLA_DK) == (e_col < GLA_DK), 1.0, 0.0).astype(BF16)
    leaf_idx = lax.broadcasted_iota(jnp.int32, (GLA_LEAF * GLA_LEAF, LANES), 0)
    leaf_lane = lax.broadcasted_iota(jnp.int32, (GLA_LEAF * GLA_LEAF, LANES), 1) % L
    leaf_s, leaf_t = leaf_idx // GLA_LEAF, leaf_idx % GLA_LEAF
    rep = lambda a: jnp.concatenate(
        [jnp.broadcast_to(a[s:s + 1], (GLA_LEAF, a.shape[1])) for s in range(GLA_LEAF)], axis=0)
    til = lambda a: jnp.concatenate([a] * GLA_LEAF, axis=0)

    def chunk_step(ci, carry):
        cs = []
        for d, (q_ref, k_ref, v_ref, b_ref, o_ref) in enumerate(
                ((qf_ref, kf_ref, vf_ref, bf_ref, of_ref), (qr_ref, kr_ref, vr_ref, br_ref, or_ref))):
            cd = ci if d == 0 else nchunk - 1 - ci
            base = pl.multiple_of(cd * L, L)
            for p in range(npair):
                ls = slice(p * LANES, (p + 1) * LANES)
                bc = b_ref[pl.ds(base, L), ls]
                cs.append(dict(
                    d=d, p=p, base=base, o_ref=o_ref, bc=bc,
                    tau_v=rowv if d == 0 else L - 1 - rowv,
                    tau_s=(lane % L) if d == 0 else L - 1 - (lane % L),
                    q=q_ref[pl.ds(base, L), ls].astype(F32),
                    k=k_ref[pl.ds(base, L), ls].astype(F32),
                    v=v_ref[pl.ds(base, L), p * 2 * GLA_DV:(p + 1) * 2 * GLA_DV],
                    b_last=bc[L - 1:L] if d == 0 else bc[0:1],
                    state=s_scr[d, p]))

        for c in cs:
            c["o_pair"] = _dot((c["q"] * jnp.exp2(c["bc"])).astype(BF16), c["state"].astype(BF16))

        for c in cs:
            c["att"] = jnp.zeros((L, LANES), F32)
        bs = L // 2
        while bs >= GLA_LEAF:
            for c in cs:
                d, bc, tau_v = c["d"], c["bc"], c["tau_v"]
                later = (tau_v // bs) % 2 == 1
                ref_b = None
                for blk in range(L // (2 * bs)):
                    tau_ref = blk * 2 * bs + bs - 1
                    idx = tau_ref if d == 0 else L - 1 - tau_ref
                    r = jnp.broadcast_to(bc[idx:idx + 1], (L, LANES))
                    ref_b = r if ref_b is None else jnp.where(tau_v // (2 * bs) == blk, r, ref_b)
                c["qs"] = jnp.where(later, c["q"] * jnp.exp2(jnp.where(later, bc - ref_b, 0.0)), 0.0).astype(BF16)
                ks = jnp.where(later, 0.0, c["k"] * jnp.exp2(jnp.where(later, 0.0, ref_b - bc)))
                c["ks2"] = jnp.concatenate([jnp.where(lane < GLA_DK, ks, 0.0), jnp.where(lane < GLA_DK, 0.0, ks)],
                                           axis=0).astype(BF16)
            for c in cs:
                same_parent = (c["tau_v"] // (2 * bs)) == (c["tau_s"] // (2 * bs))
                c["att"] = c["att"] + jnp.where(same_parent, _dot_nt(c["qs"], c["ks2"]), 0.0)
            bs //= 2

        for c in cs:
            c["leaf"] = []
        for blk in range(L // GLA_LEAF):
            rs = slice(blk * GLA_LEAF, (blk + 1) * GLA_LEAF)
            for c in cs:
                qb, bb, kb = c["q"][rs], c["bc"][rs], c["k"][rs]
                ok = (leaf_t >= leaf_s) if c["d"] == 0 else (leaf_t <= leaf_s)
                w = jnp.where(ok, til(qb) * jnp.exp2(til(bb) - rep(bb)) * rep(kb), 0.0)
                c["a"] = _dot(w.astype(BF16), head_ones)
            for c in cs:
                a = jnp.where(leaf_lane == leaf_s + blk * GLA_LEAF, c["a"], 0.0)
                acc = a[0:GLA_LEAF]
                for s in range(1, GLA_LEAF):
                    acc = acc + a[s * GLA_LEAF:(s + 1) * GLA_LEAF]
                c["leaf"].append(acc)

        for c in cs:
            v = c["v"]
            c["att"] = (c["att"] + jnp.concatenate(c["leaf"], axis=0)).astype(BF16)
            c["v_blk"] = jnp.concatenate([jnp.where(vcol < GLA_DV, v, jnp.zeros_like(v)),
                                          jnp.where(vcol < GLA_DV, jnp.zeros_like(v), v)], axis=0)
            c["ke"] = (c["k"] * jnp.exp2(c["b_last"] - c["bc"])).astype(BF16)
            c["decay_col"] = jnp.sum(
                jnp.where(e_row == e_col, jnp.broadcast_to(jnp.exp2(c["b_last"]), (LANES, LANES)), 0.0),
                axis=1, keepdims=True)
        for c in cs:
            c["o"] = c["o_pair"] + _dot(c["att"], c["v_blk"])
            c["ds"] = _dot_tn(c["ke"], c["v"])
        for c in cs:
            p = c["p"]
            c["o_ref"][pl.ds(c["base"], L), p * 2 * GLA_DV:(p + 1) * 2 * GLA_DV] = c["o"].astype(BF16)
            s_scr[c["d"], p] = c["decay_col"] * c["state"] + jnp.where(own_block, c["ds"], 0.0)
        return carry

    lax.fori_loop(0, nchunk, chunk_step, 0)


def _gla(gq, gk, gv, bc, nct):
    bsz, ttot, nk = gq.shape
    nc = gv.shape[2]
    nt = ttot // TM
    fwd = lambda w: pl.BlockSpec((None, TM, w), lambda b, j: (b, j, 0))
    rev = lambda w: pl.BlockSpec((None, TM, w), lambda b, j: (b, _scan_block(j, nct, nt), 0))
    bfwd = pl.BlockSpec((None, TM, nk), lambda b, j: (b, j, 0))
    brev = pl.BlockSpec((None, TM, nk), lambda b, j: (b, _scan_block(j, nct, nt), 1))
    return pl.pallas_call(
        _gla_kernel,
        out_shape=(jax.ShapeDtypeStruct((bsz, ttot, nc), BF16),) * 2,
        grid=(bsz, nt),
        in_specs=[fwd(nk), fwd(nk), fwd(nc), bfwd, rev(nk), rev(nk), rev(nc), brev],
        out_specs=(fwd(nc), rev(nc)),
        scratch_shapes=[pltpu.VMEM((2, GLA_HEADS // 2, LANES, 2 * GLA_DV), F32)],
        compiler_params=_params("parallel", "arbitrary"),
        name="gla_scan",
    )(gq, gk, gv, bc, gq, gk, gv, bc)


def _na_bias_tables(rpb, rows):
    kh = NA_KH
    c = np.arange(GRID_W)
    cs = np.clip(c - NA_KW // 2, 0, GRID_W - NA_KW)
    col_ok = (c[None, :] >= cs[:, None]) & (c[None, :] < cs[:, None] + NA_KW)
    pad = GRID_W - NA_KW
    rpb_pad = jnp.pad(rpb.astype(F32) * LOG2E, ((0, 0), (0, 0), (pad, pad)))
    t1 = jnp.stack([rpb_pad[:, :, GRID_W - 1 - cq:2 * GRID_W - 1 - cq] for cq in range(GRID_W)], axis=2)
    t1 = jnp.where(col_ok, t1, NEG_BIG).transpose(0, 2, 1, 3)
    drows, oks = [], []
    for r0 in (0, NA_QROWS, rows - NA_QROWS):
        kb = int(np.clip(r0 - kh // 2, 0, rows - NA_KROWS))
        qr = r0 + np.arange(NA_QROWS)
        ws = np.clip(qr - kh // 2, 0, rows - kh)
        krow = kb + np.arange(NA_KROWS)
        oks.append((krow[None, :] >= ws[:, None]) & (krow[None, :] < ws[:, None] + kh))
        drows.append(np.clip(krow[None, :] - qr[:, None] + (NA_KH - 1), 0, 2 * NA_KH - 2))
    drow = np.stack(drows).reshape(-1)
    ok = np.stack(oks).reshape(1, 1, -1, 1)
    tv = jnp.where(ok, jnp.take(t1, jnp.asarray(drow), axis=2), NEG_BIG)
    tv = tv.reshape(NA_HEADS, GRID_W, 3, NA_QROWS, NA_KROWS * GRID_W).transpose(2, 0, 3, 1, 4)
    return tv.reshape(3, NA_HEADS, NA_QROWS * GRID_W, NA_KROWS * GRID_W)


def _na_kernel(q_ref, k_ref, v_ref, bias_ref, o_ref, *, n_ctx, rows):
    rb = pl.program_id(2)
    nq = NA_QROWS * GRID_W
    nk = NA_KROWS * GRID_W
    kb = jnp.clip(rb * NA_QROWS - NA_KH // 2, 0, rows - NA_KROWS)
    start = pl.multiple_of(n_ctx + kb * GRID_W, GRID_W)
    lane = lax.broadcasted_iota(jnp.int32, (nq, LANES), 1)
    heads = []
    for j in range(NA_HPS):
        ls = slice((j // 2) * LANES, (j // 2 + 1) * LANES)
        q = q_ref[:, ls]
        qj = jnp.where((lane < NA_DH) == (j % 2 == 0), q, jnp.zeros_like(q))
        heads.append(dict(ls=ls, s_loc=_dot_nt(qj, k_ref[pl.ds(start, nk), ls]), s_ctx=_dot_nt(qj, k_ref[0:n_ctx, ls])))
    for j, c in enumerate(heads):
        c["s_loc"] = c["s_loc"] + bias_ref[j]
        c["m"] = jnp.maximum(jnp.max(c["s_loc"], axis=-1, keepdims=True), jnp.max(c["s_ctx"], axis=-1, keepdims=True))
    for c in heads:
        c["p_loc"] = jnp.exp2(c["s_loc"] - c["m"])
        c["p_ctx"] = jnp.exp2(c["s_ctx"] - c["m"])
        c["den"] = jnp.sum(c["p_loc"], axis=-1, keepdims=True) + jnp.sum(c["p_ctx"], axis=-1, keepdims=True)
    for c in heads:
        c["o"] = (_dot(c["p_loc"].astype(BF16), v_ref[pl.ds(start, nk), c["ls"]])
                  + _dot(c["p_ctx"].astype(BF16), v_ref[0:n_ctx, c["ls"]]))
    outs = [c["o"] / c["den"] for c in heads]
    for j in range(NA_HPS // 2):
        o_ref[:, j * LANES:(j + 1) * LANES] = jnp.where(lane < NA_DH, outs[2 * j], outs[2 * j + 1]).astype(BF16)


def _na_attention(nq, nk, nv, bias, n_ctx, rows):
    bsz, ttot, _ = nq.shape
    nrb = rows // NA_QROWS
    qrows = NA_QROWS * GRID_W
    krows = NA_KROWS * GRID_W
    qoff = n_ctx // qrows
    hw = NA_HPS * NA_DH

    def variant(rb):
        return jnp.where(rb == 0, 0, jnp.where(rb == nrb - 1, 2, 1))

    return pl.pallas_call(
        functools.partial(_na_kernel, n_ctx=n_ctx, rows=rows),
        out_shape=jax.ShapeDtypeStruct((bsz, rows * GRID_W, NA_HEADS * NA_DH), BF16),
        grid=(bsz, NA_HEADS // NA_HPS, nrb),
        in_specs=[pl.BlockSpec((None, qrows, hw), lambda b, p, rb: (b, qoff + rb, p)),
                  pl.BlockSpec((None, ttot, hw), lambda b, p, rb: (b, 0, p)),
                  pl.BlockSpec((None, ttot, hw), lambda b, p, rb: (b, 0, p)),
                  pl.BlockSpec((None, NA_HPS, qrows, krows), lambda b, p, rb: (variant(rb), p, 0, 0))],
        out_specs=pl.BlockSpec((None, qrows, hw), lambda b, p, rb: (b, rb, p)),
        compiler_params=_params("parallel", "parallel", "parallel"),
        name="neighbourhood_attention",
    )(nq, nk, nv, bias)


def _out1_kernel(h_ref, gate_ref, of_ref, or_ref, na_ref, z_ref, gn_ref, w_ref, fn_ref, o_ref):
    f32 = lambda r: r[...].astype(F32)
    g = _head_rms(f32(of_ref) + f32(or_ref), GLA_HEADS) * gn_ref[...]
    cat = (jnp.concatenate([g, f32(na_ref)], axis=1) * f32(z_ref)).astype(BF16)
    h = h_ref[...] + gate_ref[...] * _dot(cat, w_ref[...])
    o_ref[...] = _rms(h) * fn_ref[...]


def _out1(h, gate, of, orv, na, z, gla_norm, w_out, final_norm, nct):
    bsz, ttot, _ = h.shape
    seq = na.shape[1]
    nc = GLA_HEADS * GLA_DV
    nd = NA_HEADS * NA_DH
    lat = lambda w: pl.BlockSpec((None, TM, w), lambda b, t: (b, t + nct, 0))
    return pl.pallas_call(
        _out1_kernel,
        out_shape=jax.ShapeDtypeStruct((bsz, seq, D_MODEL), F32),
        grid=(bsz, seq // TM),
        in_specs=[lat(D_MODEL), pl.BlockSpec((None, None, 1, D_MODEL), lambda b, t: (b, 1, 0, 0)),
                  lat(nc), lat(nc), pl.BlockSpec((None, TM, nd), lambda b, t: (b, t, 0)), lat(nc + nd),
                  _full((1, nc)), _full(w_out.shape), _full((1, D_MODEL))],
        out_specs=pl.BlockSpec((None, TM, D_MODEL), lambda b, t: (b, t, 0)),
        compiler_params=_params("parallel", "parallel"),
        name="out1",
    )(h, gate, of, orv, na, z, gla_norm[None], w_out.astype(BF16), final_norm[None])


def kernel(x, c, ctx, c_ctx, l0_norm, l0_w_mod, l0_b_mod, l0_w_in, l0_mla_q_norm, l0_mla_w_uq, l0_mla_kv_norm, l0_mla_w_ukv, l0_mlstm_conv_w, l0_mlstm_conv_b, l0_mlstm_b_i, l0_mlstm_b_f, l0_mlstm_norm, l0_w_out, l1_norm, l1_w_mod, l1_b_mod, l1_w_in, l1_gla_w_gate, l1_gla_b_gate, l1_gla_norm, l1_na_rpb, l1_w_out, final_norm):
    bsz, seq, d = x.shape
    n_ctx = ctx.shape[1]
    rows = seq // GRID_W
    assert d == D_MODEL and seq % TM == 0 and n_ctx == TM
    assert rows % NA_QROWS == 0 and rows >= NA_KROWS and rows // NA_QROWS >= 3
    nct = n_ctx // TM

    shift, scale, gate0 = _mod_vectors(c, c_ctx, l0_w_mod, l0_b_mod)
    cos_t, sin_t = _rope_tables(n_ctx, seq)
    w0 = _proj0_weights(l0_w_in, l0_mla_q_norm, l0_mla_w_uq, l0_mla_kv_norm, l0_mla_w_ukv,
                        l0_mlstm_conv_w, l0_mlstm_conv_b, l0_mlstm_b_i, l0_mlstm_b_f)
    q, k, v, mq, mk, mv, g_rows, g_cols, mo, z0 = _proj0(ctx, x, scale, shift, l0_norm, cos_t, sin_t, w0, nct)
    a = _mla_attention(q, k, v, nct)
    hf, hr = _mlstm(mq, mk, mv, g_rows, g_cols, nct)

    shift, scale, gate = _mod_vectors(c, c_ctx, l1_w_mod, l1_b_mod)
    w1 = _proj1_weights(l1_w_in, l1_gla_w_gate, l1_gla_b_gate)
    h, gq, gk, gv, bc, nq, nk, nv, z = _mid(ctx, x, gate0, a, hf, hr, mo, z0, l0_mlstm_norm, l0_w_out,
                                            scale, shift, l1_norm, w1, nct)
    of, orv = _gla(gq, gk, gv, bc, nct)
    na = _na_attention(nq, nk, nv, _na_bias_tables(l1_na_rpb, rows), n_ctx, rows)
    return _out1(h, gate, of, orv, na, z, l1_gla_norm, l1_w_out, final_norm, nct)
```

```python
import functools

import jax
import jax.numpy as jnp
import numpy as np
from jax import lax
from jax.experimental import pallas as pl
from jax.experimental.pallas import tpu as pltpu

F32 = jnp.float32
BF16 = jnp.bfloat16

D_MODEL = 1024
GRID_W = 64
EPS = 1e-6
ROPE_BASE = 10000.0

MLA_HEADS = 8
MLA_Q_RANK = 384
MLA_KV_RANK = 256
MLA_NOPE = 64
MLA_ROPE = 32
MLA_V = 64
ROPE_PAIRS = MLA_ROPE // 4
MLSTM_HEADS = 4
MLSTM_DH = 128
MLSTM_CONV = 3
GLA_HEADS = 4
GLA_DK = 64
GLA_DV = 128
GLA_GATE_RANK = 16
GLA_TAU = 16.0
NA_HEADS = 8
NA_DH = 64
NA_KH = 8
NA_KW = 16

CHUNK = 64
TM = 256
LANES = 128
SUBLANES = 8
HALO = SUBLANES
NA_QROWS = 4
NA_KROWS = NA_QROWS + NA_KH - 1
GLA_LEAF = 8
MLA_HPS = 4
NA_HPS = 4
VMEM_LIMIT = 56 * 1024 * 1024
NEG_BIG = -1e30
LOG2E = 1.4426950408889634


def _dot(a, b):
    return jnp.dot(a, b, preferred_element_type=F32)


def _dot_nt(a, b):
    return lax.dot_general(a, b, (((1,), (1,)), ((), ())), preferred_element_type=F32)


def _dot_tn(a, b):
    return lax.dot_general(a, b, (((0,), (0,)), ((), ())), preferred_element_type=F32)


def _rms(x):
    return x * lax.rsqrt(jnp.mean(x * x, axis=-1, keepdims=True) + EPS)


def _sigmoid(x):
    return 1.0 / (1.0 + jnp.exp(-x))


def _silu(x):
    return x * _sigmoid(x)


def _log_sigmoid(x):
    return jnp.minimum(x, 0.0) - jnp.log(1.0 + jnp.exp(-jnp.abs(x)))


def _params(*sem):
    return pltpu.CompilerParams(dimension_semantics=sem, vmem_limit_bytes=VMEM_LIMIT)


def _full(shape):
    nd = len(shape)
    return pl.BlockSpec(shape, lambda *_: (0,) * nd)


def _mod_kernel(c_ref, w_ref, b_ref, o_ref):
    o_ref[...] = _dot(_silu(c_ref[...]).astype(BF16), w_ref[...]) + b_ref[...]


def _modulation(cc, w_mod, b_mod):
    rows, d = cc.shape
    n = w_mod.shape[1]
    return pl.pallas_call(
        _mod_kernel,
        out_shape=jax.ShapeDtypeStruct((rows, n), F32),
        grid=(n // d,),
        in_specs=[_full((rows, d)), pl.BlockSpec((d, d), lambda j: (0, j)), pl.BlockSpec((1, d), lambda j: (0, j))],
        out_specs=pl.BlockSpec((rows, d), lambda j: (0, j)),
        compiler_params=_params("parallel"),
        name="modulation",
    )(cc, w_mod.astype(BF16), b_mod[None])


def _mod_vectors(c, c_ctx, w_mod, b_mod):
    bsz = c.shape[0]
    rows = -(-(bsz + 1) // SUBLANES) * SUBLANES
    cc = jnp.zeros((rows, D_MODEL), F32).at[:bsz].set(c).at[bsz].set(c_ctx)
    mod = _modulation(cc, w_mod, b_mod)
    out = []
    for part in jnp.split(mod, 3, axis=-1):
        ctx_v = jnp.broadcast_to(part[bsz][None], (bsz, D_MODEL))
        out.append(jnp.stack([ctx_v, part[:bsz]], axis=1)[:, :, None, :])
    return out


def _modnorm(x, g, scale, shift):
    return _rms(x) * g * (1.0 + scale) + shift


def _seg_scan(x, reverse, op, fill):
    width = x.shape[1]
    pos = lax.broadcasted_iota(jnp.int32, x.shape, 1) % CHUNK
    k = 1
    while k < CHUNK:
        if reverse:
            x = op(x, jnp.where(pos < CHUNK - k, pltpu.roll(x, width - k, axis=1), fill))
        else:
            x = op(x, jnp.where(pos >= k, pltpu.roll(x, k, axis=1), fill))
        k *= 2
    return x


def _split3(x):
    hi = x.astype(BF16)
    r1 = x - hi.astype(F32)
    mid = r1.astype(BF16)
    return hi, mid, (r1 - mid.astype(F32)).astype(BF16)


def _proj0_kernel(ctx_ref, lat_ref, hp_ref, hn_ref, sc_ref, sh_ref, g_ref, cos_ref, sin_ref,
                  wcq_ref, wckv_ref, wkr_ref, wmqk_ref, wmv_ref, wmo_ref, wgt_ref, wz_ref,
                  qn_ref, wuq_ref, kvn_ref, wk_ref, wv_ref, cw_ref, cb_ref, gb_ref,
                  q_out, k_out, v_out, mq_out, mk_out, mv_out, gr_out, gc_out, mo_out, z_out,
                  pbuf, *, nct, nt):
    t = pl.program_id(1)
    h = jnp.where(t < nct, ctx_ref[...], lat_ref[...])
    x = jnp.concatenate([hp_ref[...], h, hn_ref[...]], axis=0)
    ub_ext = _modnorm(x, g_ref[...], sc_ref[...], sh_ref[...]).astype(BF16)
    ub = ub_ext[HALO:HALO + TM]

    gt = _dot_nt(wgt_ref[...], ub) + gb_ref[...]
    half = MLSTM_HEADS
    per_dir = []
    for d in range(2):
        gd = gt[d * SUBLANES:(d + 1) * SUBLANES]
        bcum = _seg_scan(_log_sigmoid(gd), d == 1, jnp.add, 0.0)
        b_top = pltpu.roll(bcum, half, axis=0)
        c8 = gd - b_top
        pm8 = _seg_scan(c8, d == 1, jnp.maximum, -jnp.inf)
        per_dir.append((c8, pm8, b_top))

    cos = cos_ref[...]
    sin = sin_ref[...]
    cos_all = jnp.concatenate([cos] * MLA_HEADS, axis=1)
    sin_all = jnp.concatenate([sin] * MLA_HEADS, axis=1)

    def swap_halves(a):
        lane = lax.broadcasted_iota(jnp.int32, a.shape, 1)
        first = lane % (2 * ROPE_PAIRS) < ROPE_PAIRS
        return jnp.where(first, pltpu.roll(a, a.shape[1] - ROPE_PAIRS, axis=1), pltpu.roll(a, ROPE_PAIRS, axis=1))

    cq = _dot(ub, wcq_ref[...])
    ckv = _dot(ub, wckv_ref[...])
    kr = _dot(ub, wkr_ref[...])
    pqk = _dot(ub_ext, wmqk_ref[...])

    nq = (_rms(cq) * qn_ref[...]).astype(BF16)
    nkv = (_rms(ckv) * kvn_ref[...]).astype(BF16)

    q = _dot(nq, wuq_ref[...])
    kn = _dot(nkv, wk_ref[...])
    vv = _dot(nkv, wv_ref[...])

    prev_ok = t > nct
    next_ok = jnp.logical_and(t >= nct, t != nt - 1)
    row = lax.broadcasted_iota(jnp.int32, pqk.shape, 0)
    keep = jnp.logical_and(jnp.logical_or(row >= HALO, prev_ok), jnp.logical_or(row < HALO + TM, next_ok))
    pbuf[...] = jnp.where(keep, pqk, 0.0)
    cw = cw_ref[...]
    y = (pbuf[HALO - 1:HALO - 1 + TM, :] * cw[0:1] + pbuf[HALO:HALO + TM, :] * cw[1:2]
         + pbuf[HALO + 1:HALO + 1 + TM, :] * cw[2:3] + cb_ref[...])
    y = _silu(y)
    nb = MLSTM_HEADS * MLSTM_DH
    mq_out[...] = (y[:, :nb] * MLSTM_DH ** -0.5).astype(BF16)
    mk_out[...] = y[:, nb:].astype(BF16)

    mv = _dot(ub, wmv_ref[...])
    mo = _dot(ub, wmo_ref[...])

    q = q * cos_all + swap_halves(q) * sin_all
    q_out[...] = (q * ((MLA_NOPE + MLA_ROPE) ** -0.5 * LOG2E)).astype(BF16)
    rot = kr * cos + swap_halves(kr) * sin
    k_out[...] = (kn + jnp.concatenate([rot] * MLA_HEADS, axis=1)).astype(BF16)
    vlane = lax.broadcasted_iota(jnp.int32, (TM, MLA_HEADS * LANES), 1)
    is_value = ((vlane % LANES) < MLA_V) == ((vlane // LANES) % 2 == 0)
    v_out[...] = jnp.where(is_value, vv, 1.0).astype(BF16)

    zz = _dot(ub, wz_ref[...])
    mv_out[...] = mv.astype(BF16)
    mo_out[...] = _sigmoid(mo).astype(BF16)
    z_out[...] = _silu(zz).astype(BF16)

    r8 =lax.broadcasted_iota(jnp.int32, (SUBLANES, TM), 0)
    both = lambda i: jnp.where(r8 < half, per_dir[0][i], pltpu.roll(per_dir[1][i], half, axis=0)) * LOG2E
    c_rows, pm_rows, b_rows = both(0), both(1), both(2)
    for ci in range(TM // CHUNK):
        gr_out[ci] = c_rows[:, ci * CHUNK:(ci + 1) * CHUNK]
    stack = jnp.concatenate([c_rows, pm_rows, b_rows, jnp.zeros((LANES - 3 * SUBLANES, TM), F32)], axis=0)
    er = lax.broadcasted_iota(jnp.int32, (TM, TM), 0)
    ec = lax.broadcasted_iota(jnp.int32, (TM, TM), 1)
    eye = jnp.where(er == ec, 1.0, 0.0).astype(BF16)
    gc_out[...] = sum(_dot_nt(eye, part) for part in _split3(stack))


def _rope_tables(n_ctx, seq):
    t = jnp.arange(seq)
    inv = 1.0 / (ROPE_BASE ** (jnp.arange(ROPE_PAIRS, dtype=F32) / ROPE_PAIRS))
    ang = jnp.concatenate([(t // GRID_W)[:, None] * inv, (t % GRID_W)[:, None] * inv], axis=-1)
    cos, sin = jnp.cos(ang), jnp.sin(ang)
    j = np.arange(MLA_ROPE)
    src = (j // (2 * ROPE_PAIRS)) * ROPE_PAIRS + (j % ROPE_PAIRS)
    sign = np.where((j % (2 * ROPE_PAIRS)) < ROPE_PAIRS, -1.0, 1.0).astype(np.float32)
    cos_full = cos[:, src]
    sin_full = sin[:, src] * sign
    cos_t = jnp.ones((n_ctx + seq, LANES), F32).at[n_ctx:, MLA_NOPE:MLA_NOPE + MLA_ROPE].set(cos_full)
    sin_t = jnp.zeros((n_ctx + seq, LANES), F32).at[n_ctx:, MLA_NOPE:MLA_NOPE + MLA_ROPE].set(sin_full)
    return cos_t, sin_t


def _proj0_weights(w_in, q_norm, w_uq, kv_norm, w_ukv, conv_w, conv_b, b_i, b_f):
    o = np.cumsum([0, MLA_Q_RANK, MLA_KV_RANK, MLA_ROPE] + [MLSTM_HEADS * MLSTM_DH] * 4 + [4 * MLSTM_HEADS, D_MODEL])
    dq = MLA_NOPE + MLA_ROPE
    wkr = jnp.pad(w_in[:, o[2]:o[3]], ((0, 0), (MLA_NOPE, LANES - dq)))
    wuq = jnp.pad(w_uq.reshape(MLA_Q_RANK, MLA_HEADS, dq), ((0, 0), (0, 0), (0, LANES - dq)))
    wkv = w_ukv.reshape(MLA_KV_RANK, MLA_HEADS, MLA_NOPE + MLA_V)
    wk = jnp.pad(wkv[:, :, :MLA_NOPE], ((0, 0), (0, 0), (0, LANES - MLA_NOPE)))
    zv = jnp.zeros((MLA_KV_RANK, MLA_HEADS // 2, MLA_V), F32)
    wv = jnp.concatenate([wkv[:, 0::2, MLA_NOPE:], zv, zv, wkv[:, 1::2, MLA_NOPE:]], axis=-1)
    wv = wv.reshape(MLA_KV_RANK, MLA_HEADS * LANES)
    gbias = jnp.stack([b_i, b_f], axis=1).reshape(4 * MLSTM_HEADS, 1)
    bf = lambda a: a.astype(BF16)
    return dict(
        wcq=bf(w_in[:, o[0]:o[1]]), wckv=bf(w_in[:, o[1]:o[2]]),
        wkr=bf(wkr),
        wmqk=bf(w_in[:, o[3]:o[5]]), wmv=bf(w_in[:, o[5]:o[6]]), wmo=bf(w_in[:, o[6]:o[7]]),
        wgt=bf(w_in[:, o[7]:o[8]].T), wz=bf(w_in[:, o[8]:o[9]]),
        qn=q_norm[None], wuq=bf(wuq.reshape(MLA_Q_RANK, -1)),
        kvn=kv_norm[None], wk=bf(wk.reshape(MLA_KV_RANK, -1)), wv=bf(wv),
        cw=jnp.zeros((SUBLANES, conv_w.shape[1]), F32).at[:MLSTM_CONV].set(conv_w), cb=conv_b[None], gb=gbias)


def _tile_specs(nct, nt):
    tile = lambda w: pl.BlockSpec((None, TM, w), lambda b, t: (b, t, 0))
    per = TM // HALO
    nlat = nt - nct
    src_ctx = pl.BlockSpec((None, TM, D_MODEL), lambda b, t: (b, jnp.minimum(t, nct - 1), 0))
    src_lat = pl.BlockSpec((None, TM, D_MODEL), lambda b, t: (b, jnp.maximum(t - nct, 0), 0))
    prev = pl.BlockSpec((None, HALO, D_MODEL), lambda b, t: (b, jnp.maximum((t - nct) * per - 1, 0), 0))
    nxt = pl.BlockSpec((None, HALO, D_MODEL),
                       lambda b, t: (b, jnp.clip((t - nct + 1) * per, 0, nlat * per - 1), 0))
    mod = pl.BlockSpec((None, None, 1, D_MODEL), lambda b, t: (b, (t >= nct).astype(jnp.int32), 0, 0))
    return tile, src_ctx, src_lat, prev, nxt, mod


def _proj0(ctx, x, scale, shift, norm_g, cos_t, sin_t, w, nct):
    bsz = x.shape[0]
    ttot = ctx.shape[1] + x.shape[1]
    nt = ttot // TM
    nb = MLSTM_HEADS * MLSTM_DH
    tile, src_ctx, src_lat, prev, nxt, mod = _tile_specs(nct, nt)
    tab = pl.BlockSpec((TM, LANES), lambda b, t: (t, 0))
    wnames = ["wcq", "wckv", "wkr", "wmqk", "wmv", "wmo", "wgt", "wz",
              "qn", "wuq", "kvn", "wk", "wv", "cw", "cb", "gb"]
    ws = [w[n] for n in wnames]
    tok = lambda width, dt: jax.ShapeDtypeStruct((bsz, ttot, width), dt)
    out_shape = (tok(MLA_HEADS * LANES, BF16), tok(MLA_HEADS * LANES, BF16), tok(MLA_HEADS * LANES, BF16),
                 tok(nb, BF16), tok(nb, BF16), tok(nb, BF16),
                 jax.ShapeDtypeStruct((bsz, ttot // CHUNK, 2 * MLSTM_HEADS, CHUNK), F32), tok(LANES, F32),
                 tok(nb, BF16), tok(D_MODEL, BF16))
    out_specs = (tile(MLA_HEADS * LANES), tile(MLA_HEADS * LANES), tile(MLA_HEADS * LANES),
                 tile(nb), tile(nb), tile(nb),
                 pl.BlockSpec((None, TM // CHUNK, 2 * MLSTM_HEADS, CHUNK), lambda b, t: (b, t, 0, 0)), tile(LANES),
                 tile(nb), tile(D_MODEL))
    return pl.pallas_call(
        functools.partial(_proj0_kernel, nct=nct, nt=nt),
        out_shape=out_shape,
        grid=(bsz, nt),
        in_specs=[src_ctx, src_lat, prev, nxt, mod, mod, _full((1, D_MODEL)), tab, tab] + [_full(a.shape) for a in ws],
        out_specs=out_specs,
        scratch_shapes=[pltpu.VMEM((TM + 2 * HALO, 2 * nb), F32)],
        compiler_params=_params("parallel", "parallel"),
        name="proj0",
    )(ctx, x, x, x, scale, shift, norm_g[None], cos_t, sin_t, *ws)


def _mla_kernel(q_ref, k_ref, v_ref, o_ref, *, nct):
    t = pl.program_id(2)
    lane = lax.broadcasted_iota(jnp.int32, (TM, LANES), 1)

    def attend(nkeys):
        qk = lambda j: _dot_nt(q_ref[:, j * LANES:(j + 1) * LANES], k_ref[0:nkeys, j * LANES:(j + 1) * LANES])
        outs = []
        s_next = qk(0)
        for j in range(MLA_HPS):
            s = s_next
            if j + 1 < MLA_HPS:
                s_next = qk(j + 1)
            p = jnp.exp2(s - jnp.max(s, axis=-1, keepdims=True))
            o = _dot(p.astype(BF16), v_ref[0:nkeys, j * LANES:(j + 1) * LANES])
            outs.append(o / pltpu.roll(o, MLA_V, axis=1))
        for j in range(MLA_HPS // 2):
            o_ref[:, j * LANES:(j + 1) * LANES] = jnp.where(lane < MLA_V, outs[2 * j], outs[2 * j + 1]).astype(BF16)

    @pl.when(t < nct)
    def _():
        attend(nct * TM)

    @pl.when(t >= nct)
    def _():
        attend(k_ref.shape[0])


def _mla_attention(q, k, v, nct):
    bsz, ttot, _ = q.shape
    nt = ttot // TM
    hw = MLA_HPS * LANES
    return pl.pallas_call(
        functools.partial(_mla_kernel, nct=nct),
        out_shape=jax.ShapeDtypeStruct((bsz, ttot, MLA_HEADS * MLA_V), BF16),
        grid=(bsz, MLA_HEADS // MLA_HPS, nt),
        in_specs=[pl.BlockSpec((None, TM, hw), lambda b, p, t: (b, t, p)),
                  pl.BlockSpec((None, ttot, hw), lambda b, p, t: (b, 0, p)),
                  pl.BlockSpec((None, ttot, hw), lambda b, p, t: (b, 0, p))],
        out_specs=pl.BlockSpec((None, TM, hw // 2), lambda b, p, t: (b, t, p)),
        compiler_params=_params("parallel", "parallel", "parallel"),
        name="mla_attention",
    )(q, k, v)


def _scan_block(j, nct, nt):
    return jnp.where(j < nct, nct - 1 - j, nt - 1 - (j - nct))


def _mlstm_kernel(qf_ref, kf_ref, vf_ref, grf_ref, gcf_ref, qr_ref, kr_ref, vr_ref, grr_ref, gcr_ref,
                  hf_ref, hr_ref, c_scr, m_scr):
    @pl.when(pl.program_id(1) == 0)
    def _():
        c_scr[...] = jnp.zeros_like(c_scr)
        m_scr[...] = jnp.zeros_like(m_scr)

    L = CHUNK
    nchunk = TM // L
    nh = MLSTM_HEADS
    nchain = 2 * nh
    row = lax.broadcasted_iota(jnp.int32, (L, L), 0)
    col = lax.broadcasted_iota(jnp.int32, (L, L), 1)
    ones = jnp.ones((L, MLSTM_DH), BF16)
    wide = lambda a: jnp.broadcast_to(a, (L, MLSTM_DH))
    twice = lambda a: jnp.concatenate([a, a], axis=1)
    refs = ((qf_ref, kf_ref, vf_ref, grf_ref, gcf_ref, hf_ref), (qr_ref, kr_ref, vr_ref, grr_ref, gcr_ref, hr_ref))

    def load(ci):
        cs = []
        for d, (q_ref, k_ref, v_ref, gr_ref, gc_ref, o_ref) in enumerate(refs):
            cd = ci if d == 0 else nchunk - 1 - ci
            rs = slice(cd * L, (cd + 1) * L)
            g_rows = gr_ref[cd]
            g_cols = gc_ref[rs, :]
            last = L - 1 if d == 0 else 0
            for hd in range(nh):
                ch = d * nh + hd
                hs = slice(hd * MLSTM_DH, (hd + 1) * MLSTM_DH)
                cs.append(dict(
                    rs=rs, hs=hs, o_ref=o_ref, causal=(col <= row) if d == 0 else (col >= row),
                    c_row=g_rows[ch:ch + 1], c_col=g_cols[:, ch:ch + 1],
                    pm_col=g_cols[:, nchain + ch:nchain + ch + 1], b_col=g_cols[:, 2 * nchain + ch:2 * nchain + ch + 1],
                    pm_last=g_cols[last:last + 1, nchain + ch:nchain + ch + 1],
                    b_last=g_cols[last:last + 1, 2 * nchain + ch:2 * nchain + ch + 1],
                    q=q_ref[rs, hs], k=k_ref[rs, hs],
                    v_ext=jnp.concatenate([v_ref[rs, hs], ones], axis=1)))
        return cs

    def prep_scores(cs):
        for c in cs:
            c["s"] = _dot_nt(c["q"], c["k"])

    def prep_weights(cs):
        for c in cs:
            c["pm_w"] = wide(c["pm_col"])
            c["b_w"] = wide(c["b_col"])
            c["w"] = jnp.exp2(jnp.where(c["causal"], c["c_row"] - c["pm_w"][:, :L], -jnp.inf))
            c["kwf"] = jnp.exp2(wide(c["c_col"] - c["pm_last"]))

    def prep_operands(cs):
        for c in cs:
            c["p"] = (c["s"] * c["w"]).astype(BF16)
            c["kw"] = (c["kwf"] * c["k"].astype(F32)).astype(BF16)

    def prep_products(cs):
        for c in cs:
            c["o1"] = _dot(c["p"], c["v_ext"])
            c["dc"] = _dot_tn(c["kw"], c["v_ext"])

    def scan_read(cs):
        for ch, c in enumerate(cs):
            c["qc"] = _dot(c["q"], state[ch].astype(BF16))

    def scan_factors(cs):
        for ch, c in enumerate(cs):
            g_w = jnp.maximum(m[ch], c["pm_w"])
            c["e_intra"] = jnp.exp2(c["pm_w"] - g_w)
            c["e_inter"] = jnp.exp2(m[ch] - g_w)
            c["floor"] = jnp.exp2(-(c["b_w"] + g_w))
            g_last = jnp.maximum(m[ch], c["pm_last"])
            c["keep"] = jnp.exp2(m[ch] - g_last)
            c["gain"] = jnp.exp2(c["pm_last"] - g_last)
            c["m_new"] = c["b_last"] + g_last

    def scan_update(cs):
        for ch, c in enumerate(cs):
            o = twice(c["e_intra"]) * c["o1"] + twice(c["e_inter"]) * c["qc"]
            h = o[:, :MLSTM_DH] / jnp.maximum(jnp.abs(o[:, MLSTM_DH:]), c["floor"])
            c["o_ref"][c["rs"], c["hs"]] = h.astype(BF16)
            state[ch] = c["keep"] * state[ch] + c["gain"] * c["dc"]
            m[ch] = c["m_new"]

    state = [c_scr[ch] for ch in range(nchain)]
    m = [m_scr[ch, 0:1, 0:1] for ch in range(nchain)]
    chunks = [load(ci) for ci in range(nchunk)]
    everything = [c for cs in chunks for c in cs]
    for stage in (prep_scores, prep_weights, prep_operands):
        stage(everything)
    prep_products(chunks[0])
    for i, cs in enumerate(chunks):
        scan_read(cs)
        if i + 1 < nchunk:
            prep_products(chunks[i + 1])
        scan_factors(cs)
        scan_update(cs)
    for ch in range(nchain):
        c_scr[ch] = state[ch]
        m_scr[ch] = jnp.broadcast_to(m[ch], m_scr.shape[1:])


def _mlstm(mq, mk, mv, g_rows, g_cols, nct):
    bsz, ttot, nb = mq.shape
    nt = ttot // TM
    per = TM // CHUNK
    fwd = lambda w: pl.BlockSpec((None, TM, w), lambda b, j: (b, j, 0))
    rev = lambda w: pl.BlockSpec((None, TM, w), lambda b, j: (b, _scan_block(j, nct, nt), 0))
    gshape = (None, per, 2 * MLSTM_HEADS, CHUNK)
    gfwd = pl.BlockSpec(gshape, lambda b, j: (b, j, 0, 0))
    grev = pl.BlockSpec(gshape, lambda b, j: (b, _scan_block(j, nct, nt), 0, 0))
    return pl.pallas_call(
        _mlstm_kernel,
        out_shape=(jax.ShapeDtypeStruct((bsz, ttot, nb), BF16),) * 2,
        grid=(bsz, nt),
        in_specs=[fwd(nb), fwd(nb), fwd(nb), gfwd, fwd(LANES), rev(nb), rev(nb), rev(nb), grev, rev(LANES)],
        out_specs=(fwd(nb), rev(nb)),
        scratch_shapes=[pltpu.VMEM((2 * MLSTM_HEADS, MLSTM_DH, 2 * MLSTM_DH), F32),
                        pltpu.VMEM((2 * MLSTM_HEADS, SUBLANES, LANES), F32)],
        compiler_params=_params("parallel", "arbitrary"),
        name="mlstm_scan",
    )(mq, mk, mv, g_rows, g_cols, mq, mk, mv, g_rows, g_cols)


def _head_rms(x, n_heads):
    w = x.shape[1] // n_heads
    return jnp.concatenate([_rms(x[:, i * w:(i + 1) * w]) for i in range(n_heads)], axis=1)


def _mid_kernel(ctx_ref, lat_ref, gate_ref, a_ref, hf_ref, hr_ref, mo_ref, z0_ref, hn_ref, wout_ref,
                sc_ref, sh_ref, g_ref, wq_ref, wk_ref, wv_ref, wga_ref, wnq_ref, wnk_ref, wnv_ref, wz_ref,
                wgate_ref, bgate_ref,
                h_out, gq_out, gk_out, gv_out, bc_out, nq_out, nk_out, nv_out, z_out, *, nct):
    f32 = lambda r: r[...].astype(F32)
    hm = _head_rms(f32(mo_ref) * (f32(hf_ref) + f32(hr_ref)), MLSTM_HEADS) * hn_ref[...]
    cat = (jnp.concatenate([f32(a_ref), hm], axis=1) * f32(z0_ref)).astype(BF16)
    h0 = jnp.where(pl.program_id(1) < nct, ctx_ref[...], lat_ref[...])
    h = h0 + gate_ref[...] * _dot(cat, wout_ref[...])
    h_out[...] = h

    ub = _modnorm(h, g_ref[...], sc_ref[...], sh_ref[...]).astype(BF16)
    ga = _dot(ub, wga_ref[...]).astype(BF16)
    gate_pre = _dot(ga, wgate_ref[...])
    gq_out[...] = (_dot(ub, wq_ref[...]) * GLA_DK ** -0.5).astype(BF16)
    gk_out[...] = _dot(ub, wk_ref[...]).astype(BF16)
    gv_out[...] = _dot(ub, wv_ref[...]).astype(BF16)

    lg = _log_sigmoid(gate_pre + bgate_ref[...]) * (1.0 / GLA_TAU)
    hi, mid, lo = _split3(lg)
    nq_out[...] = (_dot(ub, wnq_ref[...]) * (NA_DH ** -0.5 * LOG2E)).astype(BF16)
    nk_out[...] = _dot(ub, wnk_ref[...]).astype(BF16)
    nv_out[...] = _dot(ub, wnv_ref[...]).astype(BF16)
    row = lax.broadcasted_iota(jnp.int32, (TM, TM), 0)
    col = lax.broadcasted_iota(jnp.int32, (TM, TM), 1)
    same = (row // CHUNK) == (col // CHUNK)
    nk = GLA_HEADS * GLA_DK
    for d in range(2):
        tri = jnp.where(jnp.logical_and(same, (col <= row) if d == 0 else (col >= row)), 1.0, 0.0).astype(BF16)
        cols = slice(d * nk, (d + 1) * nk)
        bc_out[:, cols] = (_dot(tri, hi[:, cols]) + _dot(tri, mid[:, cols]) + _dot(tri, lo[:, cols])) * LOG2E
    z_out[...] = _silu(_dot(ub, wz_ref[...])).astype(BF16)


def _proj1_weights(w_in, w_gate, b_gate):
    nk = GLA_HEADS * GLA_DK
    nc = GLA_HEADS * GLA_DV
    nd = NA_HEADS * NA_DH
    o = np.cumsum([0, nk, nk, nc, 2 * GLA_GATE_RANK, nd, nd, nd, nc + nd])
    r = GLA_GATE_RANK
    wgate = jnp.zeros((2 * r, 2 * nk), F32).at[:r, :nk].set(w_gate[0]).at[r:, nk:].set(w_gate[1])
    bf = lambda a: a.astype(BF16)
    names = ["wq", "wk", "wv", "wga", "wnq", "wnk", "wnv", "wz"]
    w = {n: bf(w_in[:, o[i]:o[i + 1]]) for i, n in enumerate(names)}
    w["wgate"] = bf(wgate)
    w["bgate"] = b_gate.reshape(1, 2 * nk)
    return w


def _mid(ctx, x, gate0, a, hf, hr, mo, z0, h_norm, w_out, scale, shift, norm_g, w, nct):
    bsz = x.shape[0]
    ttot = ctx.shape[1] + x.shape[1]
    nt = ttot // TM
    tile, src_ctx, src_lat, _, _, mod = _tile_specs(nct, nt)
    nb = MLSTM_HEADS * MLSTM_DH
    nk = GLA_HEADS * GLA_DK
    nc = GLA_HEADS * GLA_DV
    nd = NA_HEADS * NA_DH
    wnames = ["wq", "wk", "wv", "wga", "wnq", "wnk", "wnv", "wz", "wgate", "bgate"]
    ws = [w[n] for n in wnames]
    tok = lambda width, dt: jax.ShapeDtypeStruct((bsz, ttot, width), dt)
    return pl.pallas_call(
        functools.partial(_mid_kernel, nct=nct),
        out_shape=(tok(D_MODEL, F32), tok(nk, BF16), tok(nk, BF16), tok(nc, BF16), tok(2 * nk, F32),
                   tok(nd, BF16), tok(nd, BF16), tok(nd, BF16), tok(nc + nd, BF16)),
        grid=(bsz, nt),
        in_specs=[src_ctx, src_lat, mod, tile(MLA_HEADS * MLA_V), tile(nb), tile(nb), tile(nb), tile(D_MODEL),
                  _full((1, nb)), _full(w_out.shape), mod, mod, _full((1, D_MODEL))] + [_full(a.shape) for a in ws],
        out_specs=(tile(D_MODEL), tile(nk), tile(nk), tile(nc), tile(2 * nk), tile(nd), tile(nd), tile(nd),
                   tile(nc + nd)),
        compiler_params=_params("parallel", "parallel"),
        name="out0_proj1",
    )(ctx, x, gate0, a, hf, hr, mo, z0, h_norm[None], w_out.astype(BF16), scale, shift, norm_g[None], *ws)


def _gla_kernel(qf_ref, kf_ref, vf_ref, bf_ref, qr_ref, kr_ref, vr_ref, br_ref, of_ref, or_ref, s_scr):
    @pl.when(pl.program_id(1) == 0)
    def _():
        s_scr[...] = jnp.zeros_like(s_scr)

    L = CHUNK
    nchunk = TM // L
    npair = GLA_HEADS // 2
    lane = lax.broadcasted_iota(jnp.int32, (L, LANES), 1)
    rowv = lax.broadcasted_iota(jnp.int32, (L, LANES), 0)
    srow = lax.broadcasted_iota(jnp.int32, (LANES, 2 * GLA_DV), 0)
    scol = lax.broadcasted_iota(jnp.int32, (LANES, 2 * GLA_DV), 1)
    own_block = (srow < GLA_DK) == (scol < GLA_DV)
    vcol = lax.broadcasted_iota(jnp.int32, (L, 2 * GLA_DV), 1)
    e_row = lax.broadcasted_iota(jnp.int32, (LANES, LANES), 0)
    e_col = lax.broadcasted_iota(jnp.int32, (LANES, LANES), 1)
    head_ones = jnp.where((e_row < GLA_DK) == (e_col < GLA_DK), 1.0, 0.0).astype(BF16)
    leaf_idx = lax.broadcasted_iota(jnp.int32, (GLA_LEAF * GLA_LEAF, LANES), 0)
    leaf_lane = lax.broadcasted_iota(jnp.int32, (GLA_LEAF * GLA_LEAF, LANES), 1) % L
    leaf_s, leaf_t = leaf_idx // GLA_LEAF, leaf_idx % GLA_LEAF
    rep = lambda a: jnp.concatenate(
        [jnp.broadcast_to(a[s:s + 1], (GLA_LEAF, a.shape[1])) for s in range(GLA_LEAF)], axis=0)
    til = lambda a: jnp.concatenate([a] * GLA_LEAF, axis=0)

    def chunk_step(ci, carry):
        cs = []
        for d, (q_ref, k_ref, v_ref, b_ref, o_ref) in enumerate(
                ((qf_ref, kf_ref, vf_ref, bf_ref, of_ref), (qr_ref, kr_ref, vr_ref, br_ref, or_ref))):
            cd = ci if d == 0 else nchunk - 1 - ci
            base = pl.multiple_of(cd * L, L)
            for p in range(npair):
                ls = slice(p * LANES, (p + 1) * LANES)
                bc = b_ref[pl.ds(base, L), ls]
                cs.append(dict(
                    d=d, p=p, base=base, o_ref=o_ref, bc=bc,
                    tau_v=rowv if d == 0 else L - 1 - rowv,
                    tau_s=(lane % L) if d == 0 else L - 1 - (lane % L),
                    q=q_ref[pl.ds(base, L), ls].astype(F32),
                    k=k_ref[pl.ds(base, L), ls].astype(F32),
                    v=v_ref[pl.ds(base, L), p * 2 * GLA_DV:(p + 1) * 2 * GLA_DV],
                    b_last=bc[L - 1:L] if d == 0 else bc[0:1],
                    state=s_scr[d, p]))

        for c in cs:
            c["o_pair"] = _dot((c["q"] * jnp.exp2(c["bc"])).astype(BF16), c["state"].astype(BF16))

        for c in cs:
            c["att"] = jnp.zeros((L, LANES), F32)
        bs = L // 2
        while bs >= GLA_LEAF:
            for c in cs:
                d, bc, tau_v = c["d"], c["bc"], c["tau_v"]
                later = (tau_v // bs) % 2 == 1
                ref_b = None
                for blk in range(L // (2 * bs)):
                    tau_ref = blk * 2 * bs + bs - 1
                    idx = tau_ref if d == 0 else L - 1 - tau_ref
                    r = jnp.broadcast_to(bc[idx:idx + 1], (L, LANES))
                    ref_b = r if ref_b is None else jnp.where(tau_v // (2 * bs) == blk, r, ref_b)
                c["qs"] = jnp.where(later, c["q"] * jnp.exp2(jnp.where(later, bc - ref_b, 0.0)), 0.0).astype(BF16)
                ks = jnp.where(later, 0.0, c["k"] * jnp.exp2(jnp.where(later, 0.0, ref_b - bc)))
                c["ks2"] = jnp.concatenate([jnp.where(lane < GLA_DK, ks, 0.0), jnp.where(lane < GLA_DK, 0.0, ks)],
                                           axis=0).astype(BF16)
            for c in cs:
                same_parent = (c["tau_v"] // (2 * bs)) == (c["tau_s"] // (2 * bs))
                c["att"] = c["att"] + jnp.where(same_parent, _dot_nt(c["qs"], c["ks2"]), 0.0)
            bs //= 2

        for c in cs:
            c["leaf"] = []
        for blk in range(L // GLA_LEAF):
            rs = slice(blk * GLA_LEAF, (blk + 1) * GLA_LEAF)
            for c in cs:
                qb, bb, kb = c["q"][rs], c["bc"][rs], c["k"][rs]
                ok = (leaf_t >= leaf_s) if c["d"] == 0 else (leaf_t <= leaf_s)
                w = jnp.where(ok, til(qb) * jnp.exp2(til(bb) - rep(bb)) * rep(kb), 0.0)
                c["a"] = _dot(w.astype(BF16), head_ones)
            for c in cs:
                a = jnp.where(leaf_lane == leaf_s + blk * GLA_LEAF, c["a"], 0.0)
                acc = a[0:GLA_LEAF]
                for s in range(1, GLA_LEAF):
                    acc = acc + a[s * GLA_LEAF:(s + 1) * GLA_LEAF]
                c["leaf"].append(acc)

        for c in cs:
            v = c["v"]
            c["att"] = (c["att"] + jnp.concatenate(c["leaf"], axis=0)).astype(BF16)
            c["v_blk"] = jnp.concatenate([jnp.where(vcol < GLA_DV, v, jnp.zeros_like(v)),
                                          jnp.where(vcol < GLA_DV, jnp.zeros_like(v), v)], axis=0)
            c["ke"] = (c["k"] * jnp.exp2(c["b_last"] - c["bc"])).astype(BF16)
            c["decay_col"] = jnp.sum(
                jnp.where(e_row == e_col, jnp.broadcast_to(jnp.exp2(c["b_last"]), (LANES, LANES)), 0.0),
                axis=1, keepdims=True)
        for c in cs:
            c["o"] = c["o_pair"] + _dot(c["att"], c["v_blk"])
            c["ds"] = _dot_tn(c["ke"], c["v"])
        for c in cs:
            p = c["p"]
            c["o_ref"][pl.ds(c["base"], L), p * 2 * GLA_DV:(p + 1) * 2 * GLA_DV] = c["o"].astype(BF16)
            s_scr[c["d"], p] = c["decay_col"] * c["state"] + jnp.where(own_block, c["ds"], 0.0)
        return carry

    lax.fori_loop(0, nchunk, chunk_step, 0)


def _gla(gq, gk, gv, bc, nct):
    bsz, ttot, nk = gq.shape
    nc = gv.shape[2]
    nt = ttot // TM
    fwd = lambda w: pl.BlockSpec((None, TM, w), lambda b, j: (b, j, 0))
    rev = lambda w: pl.BlockSpec((None, TM, w), lambda b, j: (b, _scan_block(j, nct, nt), 0))
    bfwd = pl.BlockSpec((None, TM, nk), lambda b, j: (b, j, 0))
    brev = pl.BlockSpec((None, TM, nk), lambda b, j: (b, _scan_block(j, nct, nt), 1))
    return pl.pallas_call(
        _gla_kernel,
        out_shape=(jax.ShapeDtypeStruct((bsz, ttot, nc), BF16),) * 2,
        grid=(bsz, nt),
        in_specs=[fwd(nk), fwd(nk), fwd(nc), bfwd, rev(nk), rev(nk), rev(nc), brev],
        out_specs=(fwd(nc), rev(nc)),
        scratch_shapes=[pltpu.VMEM((2, GLA_HEADS // 2, LANES, 2 * GLA_DV), F32)],
        compiler_params=_params("parallel", "arbitrary"),
        name="gla_scan",
    )(gq, gk, gv, bc, gq, gk, gv, bc)


def _na_bias_tables(rpb, rows):
    kh = NA_KH
    c = np.arange(GRID_W)
    cs = np.clip(c - NA_KW // 2, 0, GRID_W - NA_KW)
    col_ok = (c[None, :] >= cs[:, None]) & (c[None, :] < cs[:, None] + NA_KW)
    pad = GRID_W - NA_KW
    rpb_pad = jnp.pad(rpb.astype(F32) * LOG2E, ((0, 0), (0, 0), (pad, pad)))
    t1 = jnp.stack([rpb_pad[:, :, GRID_W - 1 - cq:2 * GRID_W - 1 - cq] for cq in range(GRID_W)], axis=2)
    t1 = jnp.where(col_ok, t1, NEG_BIG).transpose(0, 2, 1, 3)
    edge = NA_QROWS - 1
    t1 = jnp.pad(t1, ((0, 0), (0, 0), (edge, edge), (0, 0)), constant_values=NEG_BIG)
    tabs = []
    for r0 in (0, NA_QROWS, rows - NA_QROWS):
        kb = int(np.clip(r0 - kh // 2, 0, rows - NA_KROWS))
        krow = kb + np.arange(NA_KROWS)
        per_row = []
        for qrow in range(r0, r0 + NA_QROWS):
            first = kb - qrow + (NA_KH - 1) + edge
            assert 0 <= first and first + NA_KROWS <= 2 * NA_KH - 1 + 2 * edge
            ws = int(np.clip(qrow - kh // 2, 0, rows - kh))
            ok = (krow >= ws) & (krow < ws + kh)
            per_row.append(jnp.where(ok[None, None, :, None], t1[:, :, first:first + NA_KROWS, :], NEG_BIG))
        tabs.append(jnp.stack(per_row, axis=1))
    return jnp.stack(tabs).reshape(3, NA_HEADS, NA_QROWS * GRID_W, NA_KROWS * GRID_W)


def _na_out1_kernel(q_ref, k_ref, v_ref, bias_ref, h_ref, gate_ref, of_ref, or_ref, z_ref, gn_ref, w_ref, fn_ref,
                    o_ref, *, n_ctx, rows):
    rb = pl.program_id(1)
    nq = NA_QROWS * GRID_W
    nk = NA_KROWS * GRID_W
    kb = jnp.clip(rb * NA_QROWS - NA_KH // 2, 0, rows - NA_KROWS)
    start = pl.multiple_of(n_ctx + kb * GRID_W, GRID_W)
    lane = lax.broadcasted_iota(jnp.int32, (nq, LANES), 1)
    na = []
    for grp in range(NA_HEADS // NA_HPS):
        heads = []
        for j in range(grp * NA_HPS, (grp + 1) * NA_HPS):
            ls = slice((j // 2) * LANES, (j // 2 + 1) * LANES)
            q = q_ref[:, ls]
            qj = jnp.where((lane < NA_DH) == (j % 2 == 0), q, jnp.zeros_like(q))
            heads.append(dict(j=j, ls=ls, s_loc=_dot_nt(qj, k_ref[pl.ds(start, nk), ls]),
                              s_ctx=_dot_nt(qj, k_ref[0:n_ctx, ls])))
        for c in heads:
            c["s_loc"] = c["s_loc"] + bias_ref[c["j"]]
            c["m"] = jnp.maximum(jnp.max(c["s_loc"], axis=-1, keepdims=True),
                                 jnp.max(c["s_ctx"], axis=-1, keepdims=True))
        for c in heads:
            c["p_loc"] = jnp.exp2(c["s_loc"] - c["m"])
            c["p_ctx"] = jnp.exp2(c["s_ctx"] - c["m"])
            c["den"] = jnp.sum(c["p_loc"], axis=-1, keepdims=True) + jnp.sum(c["p_ctx"], axis=-1, keepdims=True)
        for c in heads:
            c["o"] = (_dot(c["p_loc"].astype(BF16), v_ref[pl.ds(start, nk), c["ls"]])
                      + _dot(c["p_ctx"].astype(BF16), v_ref[0:n_ctx, c["ls"]]))
        outs = [c["o"] / c["den"] for c in heads]
        na += [jnp.where(lane < NA_DH, outs[2 * i], outs[2 * i + 1]) for i in range(NA_HPS // 2)]

    f32 = lambda r: r[...].astype(F32)
    g = _head_rms(f32(of_ref) + f32(or_ref), GLA_HEADS) * gn_ref[...]
    cat = (jnp.concatenate([g] + na, axis=1) * f32(z_ref)).astype(BF16)
    h = h_ref[...] + gate_ref[...] * _dot(cat, w_ref[...])
    o_ref[...] = _rms(h) * fn_ref[...]


def _na_out1(nq, nk, nv, bias, h, gate, of, orv, z, gla_norm, w_out, final_norm, n_ctx, rows):
    bsz, ttot, nd = nq.shape
    nrb = rows // NA_QROWS
    qrows = NA_QROWS * GRID_W
    krows = NA_KROWS * GRID_W
    assert qrows == TM
    nct = n_ctx // TM
    nc = GLA_HEADS * GLA_DV

    def variant(rb):
        return jnp.where(rb == 0, 0, jnp.where(rb == nrb - 1, 2, 1))

    lat = lambda w: pl.BlockSpec((None, TM, w), lambda b, rb: (b, rb + nct, 0))
    whole = lambda w: pl.BlockSpec((None, ttot, w), lambda b, rb: (b, 0, 0))
    return pl.pallas_call(
        functools.partial(_na_out1_kernel, n_ctx=n_ctx, rows=rows),
        out_shape=jax.ShapeDtypeStruct((bsz, rows * GRID_W, D_MODEL), F32),
        grid=(bsz, nrb),
        in_specs=[lat(nd), whole(nd), whole(nd),
                  pl.BlockSpec((None, NA_HEADS, qrows, krows), lambda b, rb: (variant(rb), 0, 0, 0)),
                  lat(D_MODEL), pl.BlockSpec((None, None, 1, D_MODEL), lambda b, rb: (b, 1, 0, 0)),
                  lat(nc), lat(nc), lat(nc + nd), _full((1, nc)), _full(w_out.shape), _full((1, D_MODEL))],
        out_specs=pl.BlockSpec((None, TM, D_MODEL), lambda b, rb: (b, rb, 0)),
        compiler_params=_params("parallel", "parallel"),
        name="na_out1",
    )(nq, nk, nv, bias, h, gate, of, orv, z, gla_norm[None], w_out.astype(BF16), final_norm[None])


def kernel(x, c, ctx, c_ctx, l0_norm, l0_w_mod, l0_b_mod, l0_w_in, l0_mla_q_norm, l0_mla_w_uq, l0_mla_kv_norm, l0_mla_w_ukv, l0_mlstm_conv_w, l0_mlstm_conv_b, l0_mlstm_b_i, l0_mlstm_b_f, l0_mlstm_norm, l0_w_out, l1_norm, l1_w_mod, l1_b_mod, l1_w_in, l1_gla_w_gate, l1_gla_b_gate, l1_gla_norm, l1_na_rpb, l1_w_out, final_norm):
    bsz, seq, d = x.shape
    n_ctx = ctx.shape[1]
    rows = seq // GRID_W
    assert d == D_MODEL and seq % TM == 0 and n_ctx == TM
    assert rows % NA_QROWS == 0 and rows >= NA_KROWS and rows // NA_QROWS >= 3
    nct = n_ctx // TM

    shift, scale, gate0 = _mod_vectors(c, c_ctx, l0_w_mod, l0_b_mod)
    cos_t, sin_t = _rope_tables(n_ctx, seq)
    w0 = _proj0_weights(l0_w_in, l0_mla_q_norm, l0_mla_w_uq, l0_mla_kv_norm, l0_mla_w_ukv,
                        l0_mlstm_conv_w, l0_mlstm_conv_b, l0_mlstm_b_i, l0_mlstm_b_f)
    q, k, v, mq, mk, mv, g_rows, g_cols, mo, z0 = _proj0(ctx, x, scale, shift, l0_norm, cos_t, sin_t, w0, nct)
    a = _mla_attention(q, k, v, nct)
    hf, hr = _mlstm(mq, mk, mv, g_rows, g_cols, nct)

    shift, scale, gate = _mod_vectors(c, c_ctx, l1_w_mod, l1_b_mod)
    w1 = _proj1_weights(l1_w_in, l1_gla_w_gate, l1_gla_b_gate)
    h, gq, gk, gv, bc, nq, nk, nv, z = _mid(ctx, x, gate0, a, hf, hr, mo, z0, l0_mlstm_norm, l0_w_out,
                                            scale, shift, l1_norm, w1, nct)
    of, orv = _gla(gq, gk, gv, bc, nct)
    return _na_out1(nq, nk, nv, _na_bias_tables(l1_na_rpb, rows), h, gate, of, orv, z,
                    l1_gla_norm, l1_w_out, final_norm, n_ctx, rows)
```

```python
import functools

import jax
import jax.numpy as jnp
import numpy as np
from jax import lax
from jax.experimental import pallas as pl
from jax.experimental.pallas import tpu as pltpu

F32 = jnp.float32
BF16 = jnp.bfloat16

D_MODEL = 1024
GRID_W = 64
EPS = 1e-6
ROPE_BASE = 10000.0

MLA_HEADS = 8
MLA_Q_RANK = 384
MLA_KV_RANK = 256
MLA_NOPE = 64
MLA_ROPE = 32
MLA_V = 64
ROPE_PAIRS = MLA_ROPE // 4
MLSTM_HEADS = 4
MLSTM_DH = 128
MLSTM_CONV = 3
GLA_HEADS = 4
GLA_DK = 64
GLA_DV = 128
GLA_GATE_RANK = 16
GLA_TAU = 16.0
NA_HEADS = 8
NA_DH = 64
NA_KH = 8
NA_KW = 16

CHUNK = 64
TM = 256
LANES = 128
SUBLANES = 8
HALO = SUBLANES
NA_QROWS = 4
NA_KROWS = NA_QROWS + NA_KH - 1
GLA_LEAF = 8
MLA_HPS = 4
VMEM_LIMIT = 56 * 1024 * 1024
NEG_BIG = -1e30
LOG2E = 1.4426950408889634


def _dot(a, b):
    return jnp.dot(a, b, preferred_element_type=F32)


def _dot_nt(a, b):
    return lax.dot_general(a, b, (((1,), (1,)), ((), ())), preferred_element_type=F32)


def _dot_tn(a, b):
    return lax.dot_general(a, b, (((0,), (0,)), ((), ())), preferred_element_type=F32)


def _rms(x):
    return x * lax.rsqrt(jnp.mean(x * x, axis=-1, keepdims=True) + EPS)


def _sigmoid(x):
    return 1.0 / (1.0 + jnp.exp(-x))


def _silu(x):
    return x * _sigmoid(x)


def _log_sigmoid(x):
    return jnp.minimum(x, 0.0) - jnp.log(1.0 + jnp.exp(-jnp.abs(x)))


def _params(*sem):
    return pltpu.CompilerParams(dimension_semantics=sem, vmem_limit_bytes=VMEM_LIMIT)


def _full(shape):
    nd = len(shape)
    return pl.BlockSpec(shape, lambda *_: (0,) * nd)


def _mod_kernel(c_ref, w_ref, b_ref, o_ref):
    o_ref[...] = _dot(_silu(c_ref[...]).astype(BF16), w_ref[...].astype(BF16)) + b_ref[...]


def _modulation(cc, w_mod, b_mod):
    rows, d = cc.shape
    n = w_mod.shape[1]
    return pl.pallas_call(
        _mod_kernel,
        out_shape=jax.ShapeDtypeStruct((rows, n), F32),
        grid=(n // d,),
        in_specs=[_full((rows, d)), pl.BlockSpec((d, d), lambda j: (0, j)), pl.BlockSpec((1, d), lambda j: (0, j))],
        out_specs=pl.BlockSpec((rows, d), lambda j: (0, j)),
        compiler_params=_params("parallel"),
        name="modulation",
    )(cc, w_mod, b_mod[None])


def _mod_vectors(c, c_ctx, w_mod, b_mod):
    bsz = c.shape[0]
    rows = -(-(bsz + 1) // SUBLANES) * SUBLANES
    cc = jnp.zeros((rows, D_MODEL), F32).at[:bsz].set(c).at[bsz].set(c_ctx)
    mod = _modulation(cc, w_mod, b_mod)
    out = []
    for part in jnp.split(mod, 3, axis=-1):
        ctx_v = jnp.broadcast_to(part[bsz][None], (bsz, D_MODEL))
        out.append(jnp.stack([ctx_v, part[:bsz]], axis=1)[:, :, None, :])
    return out


def _modnorm(x, g, scale, shift):
    return _rms(x) * g * (1.0 + scale) + shift


def _seg_scan(x, reverse, op, fill):
    width = x.shape[1]
    pos = lax.broadcasted_iota(jnp.int32, x.shape, 1) % CHUNK
    k = 1
    while k < CHUNK:
        if reverse:
            x = op(x, jnp.where(pos < CHUNK - k, pltpu.roll(x, width - k, axis=1), fill))
        else:
            x = op(x, jnp.where(pos >= k, pltpu.roll(x, k, axis=1), fill))
        k *= 2
    return x


def _split3(x):
    hi = x.astype(BF16)
    r1 = x - hi.astype(F32)
    mid = r1.astype(BF16)
    return hi, mid, (r1 - mid.astype(F32)).astype(BF16)


def _proj0_kernel(ctx_ref, lat_ref, hp_ref, hn_ref, sc_ref, sh_ref, g_ref, cos_ref, sin_ref,
                  wcq_ref, wckv_ref, wkr_ref, wmqk_ref, wmv_ref, wmo_ref, wgt_ref, wz_ref,
                  qn_ref, wuq_ref, kvn_ref, wk_ref, wv_ref, cw_ref, cb_ref, gb_ref,
                  q_out, k_out, v_out, mq_out, mk_out, mv_out, gr_out, gc_out, mo_out, z_out,
                  pbuf, *, nct, nt):
    t = pl.program_id(1)
    h = jnp.where(t < nct, ctx_ref[...], lat_ref[...])
    x = jnp.concatenate([hp_ref[...], h, hn_ref[...]], axis=0)
    ub_ext = _modnorm(x, g_ref[...], sc_ref[...], sh_ref[...]).astype(BF16)
    ub = ub_ext[HALO:HALO + TM]

    gt = _dot_nt(wgt_ref[...], ub) + gb_ref[...]
    half = MLSTM_HEADS
    per_dir = []
    for d in range(2):
        gd = gt[d * SUBLANES:(d + 1) * SUBLANES]
        bcum = _seg_scan(_log_sigmoid(gd), d == 1, jnp.add, 0.0)
        b_top = pltpu.roll(bcum, half, axis=0)
        c8 = gd - b_top
        pm8 = _seg_scan(c8, d == 1, jnp.maximum, -jnp.inf)
        per_dir.append((c8, pm8, b_top))

    cos = cos_ref[...]
    sin = sin_ref[...]
    cos_all = jnp.concatenate([cos] * MLA_HEADS, axis=1)
    sin_all = jnp.concatenate([sin] * MLA_HEADS, axis=1)

    def swap_halves(a):
        lane = lax.broadcasted_iota(jnp.int32, a.shape, 1)
        first = lane % (2 * ROPE_PAIRS) < ROPE_PAIRS
        return jnp.where(first, pltpu.roll(a, a.shape[1] - ROPE_PAIRS, axis=1), pltpu.roll(a, ROPE_PAIRS, axis=1))

    cq = _dot(ub, wcq_ref[...])
    ckv = _dot(ub, wckv_ref[...])
    kr = _dot(ub, wkr_ref[...])
    pqk = _dot(ub_ext, wmqk_ref[...])

    nq = (_rms(cq) * qn_ref[...]).astype(BF16)
    nkv = (_rms(ckv) * kvn_ref[...]).astype(BF16)

    q = _dot(nq, wuq_ref[...])
    kn = _dot(nkv, wk_ref[...])
    vv = _dot(nkv, wv_ref[...])

    prev_ok = t > nct
    next_ok = jnp.logical_and(t >= nct, t != nt - 1)
    row = lax.broadcasted_iota(jnp.int32, pqk.shape, 0)
    keep = jnp.logical_and(jnp.logical_or(row >= HALO, prev_ok), jnp.logical_or(row < HALO + TM, next_ok))
    pbuf[...] = jnp.where(keep, pqk, 0.0)
    cw = cw_ref[...]
    y = (pbuf[HALO - 1:HALO - 1 + TM, :] * cw[0:1] + pbuf[HALO:HALO + TM, :] * cw[1:2]
         + pbuf[HALO + 1:HALO + 1 + TM, :] * cw[2:3] + cb_ref[...])
    y = _silu(y)
    nb = MLSTM_HEADS * MLSTM_DH
    mq_out[...] = (y[:, :nb] * MLSTM_DH ** -0.5).astype(BF16)
    mk_out[...] = y[:, nb:].astype(BF16)

    mv = _dot(ub, wmv_ref[...])
    mo = _dot(ub, wmo_ref[...])

    q = q * cos_all + swap_halves(q) * sin_all
    q_out[...] = (q * ((MLA_NOPE + MLA_ROPE) ** -0.5 * LOG2E)).astype(BF16)
    rot = kr * cos + swap_halves(kr) * sin
    k_out[...] = (kn + jnp.concatenate([rot] * MLA_HEADS, axis=1)).astype(BF16)
    vlane = lax.broadcasted_iota(jnp.int32, (TM, MLA_HEADS * LANES), 1)
    is_value = ((vlane % LANES) < MLA_V) == ((vlane // LANES) % 2 == 0)
    v_out[...] = jnp.where(is_value, vv, 1.0).astype(BF16)

    zz = _dot(ub, wz_ref[...])
    mv_out[...] = mv.astype(BF16)
    mo_out[...] = _sigmoid(mo).astype(BF16)
    z_out[...] = _silu(zz).astype(BF16)

    r8 =lax.broadcasted_iota(jnp.int32, (SUBLANES, TM), 0)
    both = lambda i: jnp.where(r8 < half, per_dir[0][i], pltpu.roll(per_dir[1][i], half, axis=0)) * LOG2E
    c_rows, pm_rows, b_rows = both(0), both(1), both(2)
    for ci in range(TM // CHUNK):
        gr_out[ci] = c_rows[:, ci * CHUNK:(ci + 1) * CHUNK]
    stack = jnp.concatenate([c_rows, pm_rows, b_rows, jnp.zeros((LANES - 3 * SUBLANES, TM), F32)], axis=0)
    er = lax.broadcasted_iota(jnp.int32, (TM, TM), 0)
    ec = lax.broadcasted_iota(jnp.int32, (TM, TM), 1)
    eye = jnp.where(er == ec, 1.0, 0.0).astype(BF16)
    gc_out[...] = sum(_dot_nt(eye, part) for part in _split3(stack))


def _rope_tables(n_ctx, seq):
    t = jnp.arange(seq)
    inv = 1.0 / (ROPE_BASE ** (jnp.arange(ROPE_PAIRS, dtype=F32) / ROPE_PAIRS))
    ang = jnp.concatenate([(t // GRID_W)[:, None] * inv, (t % GRID_W)[:, None] * inv], axis=-1)
    cos, sin = jnp.cos(ang), jnp.sin(ang)
    j = np.arange(MLA_ROPE)
    src = (j // (2 * ROPE_PAIRS)) * ROPE_PAIRS + (j % ROPE_PAIRS)
    sign = np.where((j % (2 * ROPE_PAIRS)) < ROPE_PAIRS, -1.0, 1.0).astype(np.float32)
    cos_full = cos[:, src]
    sin_full = sin[:, src] * sign
    cos_t = jnp.ones((n_ctx + seq, LANES), F32).at[n_ctx:, MLA_NOPE:MLA_NOPE + MLA_ROPE].set(cos_full)
    sin_t = jnp.zeros((n_ctx + seq, LANES), F32).at[n_ctx:, MLA_NOPE:MLA_NOPE + MLA_ROPE].set(sin_full)
    return cos_t, sin_t


def _proj0_weights(w_in, q_norm, w_uq, kv_norm, w_ukv, conv_w, conv_b, b_i, b_f):
    o = np.cumsum([0, MLA_Q_RANK, MLA_KV_RANK, MLA_ROPE] + [MLSTM_HEADS * MLSTM_DH] * 4 + [4 * MLSTM_HEADS, D_MODEL])
    dq = MLA_NOPE + MLA_ROPE
    wkr = jnp.pad(w_in[:, o[2]:o[3]], ((0, 0), (MLA_NOPE, LANES - dq)))
    wuq = jnp.pad(w_uq.reshape(MLA_Q_RANK, MLA_HEADS, dq), ((0, 0), (0, 0), (0, LANES - dq)))
    wkv = w_ukv.reshape(MLA_KV_RANK, MLA_HEADS, MLA_NOPE + MLA_V)
    wk = jnp.pad(wkv[:, :, :MLA_NOPE], ((0, 0), (0, 0), (0, LANES - MLA_NOPE)))
    zv = jnp.zeros((MLA_KV_RANK, MLA_HEADS // 2, MLA_V), F32)
    wv = jnp.concatenate([wkv[:, 0::2, MLA_NOPE:], zv, zv, wkv[:, 1::2, MLA_NOPE:]], axis=-1)
    wv = wv.reshape(MLA_KV_RANK, MLA_HEADS * LANES)
    gbias = jnp.stack([b_i, b_f], axis=1).reshape(4 * MLSTM_HEADS, 1)
    bf = lambda a: a.astype(BF16)
    return dict(
        wcq=bf(w_in[:, o[0]:o[1]]), wckv=bf(w_in[:, o[1]:o[2]]),
        wkr=bf(wkr),
        wmqk=bf(w_in[:, o[3]:o[5]]), wmv=bf(w_in[:, o[5]:o[6]]), wmo=bf(w_in[:, o[6]:o[7]]),
        wgt=bf(w_in[:, o[7]:o[8]].T), wz=bf(w_in[:, o[8]:o[9]]),
        qn=q_norm[None], wuq=bf(wuq.reshape(MLA_Q_RANK, -1)),
        kvn=kv_norm[None], wk=bf(wk.reshape(MLA_KV_RANK, -1)), wv=bf(wv),
        cw=jnp.zeros((SUBLANES, conv_w.shape[1]), F32).at[:MLSTM_CONV].set(conv_w), cb=conv_b[None], gb=gbias)


def _tile_specs(nct, nt):
    tile = lambda w: pl.BlockSpec((None, TM, w), lambda b, t: (b, t, 0))
    per = TM // HALO
    nlat = nt - nct
    src_ctx = pl.BlockSpec((None, TM, D_MODEL), lambda b, t: (b, jnp.minimum(t, nct - 1), 0))
    src_lat = pl.BlockSpec((None, TM, D_MODEL), lambda b, t: (b, jnp.maximum(t - nct, 0), 0))
    prev = pl.BlockSpec((None, HALO, D_MODEL), lambda b, t: (b, jnp.maximum((t - nct) * per - 1, 0), 0))
    nxt = pl.BlockSpec((None, HALO, D_MODEL),
                       lambda b, t: (b, jnp.clip((t - nct + 1) * per, 0, nlat * per - 1), 0))
    mod = pl.BlockSpec((None, None, 1, D_MODEL), lambda b, t: (b, (t >= nct).astype(jnp.int32), 0, 0))
    return tile, src_ctx, src_lat, prev, nxt, mod


def _proj0(ctx, x, scale, shift, norm_g, cos_t, sin_t, w, nct):
    bsz = x.shape[0]
    ttot = ctx.shape[1] + x.shape[1]
    nt = ttot // TM
    nb = MLSTM_HEADS * MLSTM_DH
    tile, src_ctx, src_lat, prev, nxt, mod = _tile_specs(nct, nt)
    tab = pl.BlockSpec((TM, LANES), lambda b, t: (t, 0))
    wnames = ["wcq", "wckv", "wkr", "wmqk", "wmv", "wmo", "wgt", "wz",
              "qn", "wuq", "kvn", "wk", "wv", "cw", "cb", "gb"]
    ws = [w[n] for n in wnames]
    tok = lambda width, dt: jax.ShapeDtypeStruct((bsz, ttot, width), dt)
    out_shape = (tok(MLA_HEADS * LANES, BF16), tok(MLA_HEADS * LANES, BF16), tok(MLA_HEADS * LANES, BF16),
                 tok(nb, BF16), tok(nb, BF16), tok(nb, BF16),
                 jax.ShapeDtypeStruct((bsz, ttot // CHUNK, 2 * MLSTM_HEADS, CHUNK), F32), tok(LANES, F32),
                 tok(nb, BF16), tok(D_MODEL, BF16))
    out_specs = (tile(MLA_HEADS * LANES), tile(MLA_HEADS * LANES), tile(MLA_HEADS * LANES),
                 tile(nb), tile(nb), tile(nb),
                 pl.BlockSpec((None, TM // CHUNK, 2 * MLSTM_HEADS, CHUNK), lambda b, t: (b, t, 0, 0)), tile(LANES),
                 tile(nb), tile(D_MODEL))
    return pl.pallas_call(
        functools.partial(_proj0_kernel, nct=nct, nt=nt),
        out_shape=out_shape,
        grid=(bsz, nt),
        in_specs=[src_ctx, src_lat, prev, nxt, mod, mod, _full((1, D_MODEL)), tab, tab] + [_full(a.shape) for a in ws],
        out_specs=out_specs,
        scratch_shapes=[pltpu.VMEM((TM + 2 * HALO, 2 * nb), F32)],
        compiler_params=_params("parallel", "parallel"),
        name="proj0",
    )(ctx, x, x, x, scale, shift, norm_g[None], cos_t, sin_t, *ws)


def _mla_kernel(q_ref, k_ref, v_ref, o_ref, *, nct):
    t = pl.program_id(2)
    lane = lax.broadcasted_iota(jnp.int32, (TM, LANES), 1)

    def attend(nkeys):
        qk = lambda j: _dot_nt(q_ref[:, j * LANES:(j + 1) * LANES], k_ref[0:nkeys, j * LANES:(j + 1) * LANES])
        outs = []
        s_next = qk(0)
        for j in range(MLA_HPS):
            s = s_next
            if j + 1 < MLA_HPS:
                s_next = qk(j + 1)
            p = jnp.exp2(s - jnp.max(s, axis=-1, keepdims=True))
            o = _dot(p.astype(BF16), v_ref[0:nkeys, j * LANES:(j + 1) * LANES])
            outs.append(o / pltpu.roll(o, MLA_V, axis=1))
        for j in range(MLA_HPS // 2):
            o_ref[:, j * LANES:(j + 1) * LANES] = jnp.where(lane < MLA_V, outs[2 * j], outs[2 * j + 1]).astype(BF16)

    @pl.when(t < nct)
    def _():
        attend(nct * TM)

    @pl.when(t >= nct)
    def _():
        attend(k_ref.shape[0])


def _mla_attention(q, k, v, nct):
    bsz, ttot, _ = q.shape
    nt = ttot // TM
    hw = MLA_HPS * LANES
    return pl.pallas_call(
        functools.partial(_mla_kernel, nct=nct),
        out_shape=jax.ShapeDtypeStruct((bsz, ttot, MLA_HEADS * MLA_V), BF16),
        grid=(bsz, MLA_HEADS // MLA_HPS, nt),
        in_specs=[pl.BlockSpec((None, TM, hw), lambda b, p, t: (b, t, p)),
                  pl.BlockSpec((None, ttot, hw), lambda b, p, t: (b, 0, p)),
                  pl.BlockSpec((None, ttot, hw), lambda b, p, t: (b, 0, p))],
        out_specs=pl.BlockSpec((None, TM, hw // 2), lambda b, p, t: (b, t, p)),
        compiler_params=_params("parallel", "parallel", "parallel"),
        name="mla_attention",
    )(q, k, v)


def _scan_block(j, nct, nt):
    return jnp.where(j < nct, nct - 1 - j, nt - 1 - (j - nct))


def _mlstm_kernel(qf_ref, kf_ref, vf_ref, grf_ref, gcf_ref, qr_ref, kr_ref, vr_ref, grr_ref, gcr_ref,
                  hf_ref, hr_ref, c_scr, m_scr):
    @pl.when(pl.program_id(1) == 0)
    def _():
        c_scr[...] = jnp.zeros_like(c_scr)
        m_scr[...] = jnp.zeros_like(m_scr)

    L = CHUNK
    nchunk = TM // L
    nh = MLSTM_HEADS
    nchain = 2 * nh
    row = lax.broadcasted_iota(jnp.int32, (L, L), 0)
    col = lax.broadcasted_iota(jnp.int32, (L, L), 1)
    ones = jnp.ones((L, MLSTM_DH), BF16)
    wide = lambda a: jnp.broadcast_to(a, (L, MLSTM_DH))
    twice = lambda a: jnp.concatenate([a, a], axis=1)
    refs = ((qf_ref, kf_ref, vf_ref, grf_ref, gcf_ref, hf_ref), (qr_ref, kr_ref, vr_ref, grr_ref, gcr_ref, hr_ref))

    def load(ci):
        cs = []
        for d, (q_ref, k_ref, v_ref, gr_ref, gc_ref, o_ref) in enumerate(refs):
            cd = ci if d == 0 else nchunk - 1 - ci
            rs = slice(cd * L, (cd + 1) * L)
            g_rows = gr_ref[cd]
            g_cols = gc_ref[rs, :]
            last = L - 1 if d == 0 else 0
            for hd in range(nh):
                ch = d * nh + hd
                hs = slice(hd * MLSTM_DH, (hd + 1) * MLSTM_DH)
                cs.append(dict(
                    rs=rs, hs=hs, o_ref=o_ref, causal=(col <= row) if d == 0 else (col >= row),
                    c_row=g_rows[ch:ch + 1], c_col=g_cols[:, ch:ch + 1],
                    pm_col=g_cols[:, nchain + ch:nchain + ch + 1], b_col=g_cols[:, 2 * nchain + ch:2 * nchain + ch + 1],
                    pm_last=g_cols[last:last + 1, nchain + ch:nchain + ch + 1],
                    b_last=g_cols[last:last + 1, 2 * nchain + ch:2 * nchain + ch + 1],
                    q=q_ref[rs, hs], k=k_ref[rs, hs],
                    v_ext=jnp.concatenate([v_ref[rs, hs], ones], axis=1)))
        return cs

    def prep_scores(cs):
        for c in cs:
            c["s"] = _dot_nt(c["q"], c["k"])

    def prep_weights(cs):
        for c in cs:
            c["pm_w"] = wide(c["pm_col"])
            c["b_w"] = wide(c["b_col"])
            c["w"] = jnp.exp2(jnp.where(c["causal"], c["c_row"] - c["pm_w"][:, :L], -jnp.inf))
            c["kwf"] = jnp.exp2(wide(c["c_col"] - c["pm_last"]))

    def prep_operands(cs):
        for c in cs:
            c["p"] = (c["s"] * c["w"]).astype(BF16)
            c["kw"] = (c["kwf"] * c["k"].astype(F32)).astype(BF16)

    def prep_products(cs):
        for c in cs:
            c["o1"] = _dot(c["p"], c["v_ext"])
            c["dc"] = _dot_tn(c["kw"], c["v_ext"])

    def scan_read(cs):
        for ch, c in enumerate(cs):
            c["qc"] = _dot(c["q"], state[ch].astype(BF16))

    def scan_factors(cs):
        for ch, c in enumerate(cs):
            g_w = jnp.maximum(m[ch], c["pm_w"])
            c["e_intra"] = jnp.exp2(c["pm_w"] - g_w)
            c["e_inter"] = jnp.exp2(m[ch] - g_w)
            c["floor"] = jnp.exp2(-(c["b_w"] + g_w))
            g_last = jnp.maximum(m[ch], c["pm_last"])
            c["keep"] = jnp.exp2(m[ch] - g_last)
            c["gain"] = jnp.exp2(c["pm_last"] - g_last)
            c["m_new"] = c["b_last"] + g_last

    def scan_update(cs):
        for ch, c in enumerate(cs):
            o = twice(c["e_intra"]) * c["o1"] + twice(c["e_inter"]) * c["qc"]
            h = o[:, :MLSTM_DH] / jnp.maximum(jnp.abs(o[:, MLSTM_DH:]), c["floor"])
            c["o_ref"][c["rs"], c["hs"]] = h.astype(BF16)
            state[ch] = c["keep"] * state[ch] + c["gain"] * c["dc"]
            m[ch] = c["m_new"]

    state = [c_scr[ch] for ch in range(nchain)]
    m = [m_scr[ch, 0:1, 0:1] for ch in range(nchain)]
    chunks = [load(ci) for ci in range(nchunk)]
    everything = [c for cs in chunks for c in cs]
    for stage in (prep_scores, prep_weights, prep_operands):
        stage(everything)
    prep_products(chunks[0])
    for i, cs in enumerate(chunks):
        scan_read(cs)
        if i + 1 < nchunk:
            prep_products(chunks[i + 1])
        scan_factors(cs)
        scan_update(cs)
    for ch in range(nchain):
        c_scr[ch] = state[ch]
        m_scr[ch] = jnp.broadcast_to(m[ch], m_scr.shape[1:])


def _mlstm(mq, mk, mv, g_rows, g_cols, nct):
    bsz, ttot, nb = mq.shape
    nt = ttot // TM
    per = TM // CHUNK
    fwd = lambda w: pl.BlockSpec((None, TM, w), lambda b, j: (b, j, 0))
    rev = lambda w: pl.BlockSpec((None, TM, w), lambda b, j: (b, _scan_block(j, nct, nt), 0))
    gshape = (None, per, 2 * MLSTM_HEADS, CHUNK)
    gfwd = pl.BlockSpec(gshape, lambda b, j: (b, j, 0, 0))
    grev = pl.BlockSpec(gshape, lambda b, j: (b, _scan_block(j, nct, nt), 0, 0))
    return pl.pallas_call(
        _mlstm_kernel,
        out_shape=(jax.ShapeDtypeStruct((bsz, ttot, nb), BF16),) * 2,
        grid=(bsz, nt),
        in_specs=[fwd(nb), fwd(nb), fwd(nb), gfwd, fwd(LANES), rev(nb), rev(nb), rev(nb), grev, rev(LANES)],
        out_specs=(fwd(nb), rev(nb)),
        scratch_shapes=[pltpu.VMEM((2 * MLSTM_HEADS, MLSTM_DH, 2 * MLSTM_DH), F32),
                        pltpu.VMEM((2 * MLSTM_HEADS, SUBLANES, LANES), F32)],
        compiler_params=_params("parallel", "arbitrary"),
        name="mlstm_scan",
    )(mq, mk, mv, g_rows, g_cols, mq, mk, mv, g_rows, g_cols)


def _head_rms(x, n_heads):
    w = x.shape[1] // n_heads
    return jnp.concatenate([_rms(x[:, i * w:(i + 1) * w]) for i in range(n_heads)], axis=1)


def _mid_kernel(ctx_ref, lat_ref, gate_ref, a_ref, hf_ref, hr_ref, mo_ref, z0_ref, hn_ref, wout_ref,
                sc_ref, sh_ref, g_ref, wq_ref, wk_ref, wv_ref, wga_ref, wnq_ref, wnk_ref, wnv_ref, wz_ref,
                wgate_ref, bgate_ref,
                h_out, gq_out, gk_out, gv_out, bc_out, nq_out, nk_out, nv_out, z_out, *, nct):
    f32 = lambda r: r[...].astype(F32)
    hm = _head_rms(f32(mo_ref) * (f32(hf_ref) + f32(hr_ref)), MLSTM_HEADS) * hn_ref[...]
    cat = (jnp.concatenate([f32(a_ref), hm], axis=1) * f32(z0_ref)).astype(BF16)
    h0 = jnp.where(pl.program_id(1) < nct, ctx_ref[...], lat_ref[...])
    h = h0 + gate_ref[...] * _dot(cat, wout_ref[...])
    h_out[...] = h

    ub = _modnorm(h, g_ref[...], sc_ref[...], sh_ref[...]).astype(BF16)
    ga = _dot(ub, wga_ref[...]).astype(BF16)
    gate_pre = _dot(ga, wgate_ref[...])
    gq_out[...] = (_dot(ub, wq_ref[...]) * GLA_DK ** -0.5).astype(BF16)
    gk_out[...] = _dot(ub, wk_ref[...]).astype(BF16)
    gv_out[...] = _dot(ub, wv_ref[...]).astype(BF16)

    lg = _log_sigmoid(gate_pre + bgate_ref[...]) * (1.0 / GLA_TAU)
    hi, mid, lo = _split3(lg)
    nq_out[...] = (_dot(ub, wnq_ref[...]) * (NA_DH ** -0.5 * LOG2E)).astype(BF16)
    nk_out[...] = _dot(ub, wnk_ref[...]).astype(BF16)
    nv_out[...] = _dot(ub, wnv_ref[...]).astype(BF16)
    row = lax.broadcasted_iota(jnp.int32, (TM, TM), 0)
    col = lax.broadcasted_iota(jnp.int32, (TM, TM), 1)
    same = (row // CHUNK) == (col // CHUNK)
    nk = GLA_HEADS * GLA_DK
    for d in range(2):
        tri = jnp.where(jnp.logical_and(same, (col <= row) if d == 0 else (col >= row)), 1.0, 0.0).astype(BF16)
        cols = slice(d * nk, (d + 1) * nk)
        bc_out[:, cols] = (_dot(tri, hi[:, cols]) + _dot(tri, mid[:, cols]) + _dot(tri, lo[:, cols])) * LOG2E
    z_out[...] = _silu(_dot(ub, wz_ref[...])).astype(BF16)


def _proj1_weights(w_in, w_gate, b_gate):
    nk = GLA_HEADS * GLA_DK
    nc = GLA_HEADS * GLA_DV
    nd = NA_HEADS * NA_DH
    o = np.cumsum([0, nk, nk, nc, 2 * GLA_GATE_RANK, nd, nd, nd, nc + nd])
    r = GLA_GATE_RANK
    wgate = jnp.zeros((2 * r, 2 * nk), F32).at[:r, :nk].set(w_gate[0]).at[r:, nk:].set(w_gate[1])
    bf = lambda a: a.astype(BF16)
    names = ["wq", "wk", "wv", "wga", "wnq", "wnk", "wnv", "wz"]
    w = {n: bf(w_in[:, o[i]:o[i + 1]]) for i, n in enumerate(names)}
    w["wgate"] = bf(wgate)
    w["bgate"] = b_gate.reshape(1, 2 * nk)
    return w


def _mid(ctx, x, gate0, a, hf, hr, mo, z0, h_norm, w_out, scale, shift, norm_g, w, nct):
    bsz = x.shape[0]
    ttot = ctx.shape[1] + x.shape[1]
    nt = ttot // TM
    tile, src_ctx, src_lat, _, _, mod = _tile_specs(nct, nt)
    nb = MLSTM_HEADS * MLSTM_DH
    nk = GLA_HEADS * GLA_DK
    nc = GLA_HEADS * GLA_DV
    nd = NA_HEADS * NA_DH
    wnames = ["wq", "wk", "wv", "wga", "wnq", "wnk", "wnv", "wz", "wgate", "bgate"]
    ws = [w[n] for n in wnames]
    tok = lambda width, dt: jax.ShapeDtypeStruct((bsz, ttot, width), dt)
    return pl.pallas_call(
        functools.partial(_mid_kernel, nct=nct),
        out_shape=(tok(D_MODEL, F32), tok(nk, BF16), tok(nk, BF16), tok(nc, BF16), tok(2 * nk, F32),
                   tok(nd, BF16), tok(nd, BF16), tok(nd, BF16), tok(nc + nd, BF16)),
        grid=(bsz, nt),
        in_specs=[src_ctx, src_lat, mod, tile(MLA_HEADS * MLA_V), tile(nb), tile(nb), tile(nb), tile(D_MODEL),
                  _full((1, nb)), _full(w_out.shape), mod, mod, _full((1, D_MODEL))] + [_full(a.shape) for a in ws],
        out_specs=(tile(D_MODEL), tile(nk), tile(nk), tile(nc), tile(2 * nk), tile(nd), tile(nd), tile(nd),
                   tile(nc + nd)),
        compiler_params=_params("parallel", "parallel"),
        name="out0_proj1",
    )(ctx, x, gate0, a, hf, hr, mo, z0, h_norm[None], w_out.astype(BF16), scale, shift, norm_g[None], *ws)


def _gla_kernel(qf_ref, kf_ref, vf_ref, bf_ref, qr_ref, kr_ref, vr_ref, br_ref, of_ref, or_ref, s_scr):
    @pl.when(pl.program_id(1) == 0)
    def _():
        s_scr[...] = jnp.zeros_like(s_scr)

    L = CHUNK
    nchunk = TM // L
    npair = GLA_HEADS // 2
    lane = lax.broadcasted_iota(jnp.int32, (L, LANES), 1)
    rowv = lax.broadcasted_iota(jnp.int32, (L, LANES), 0)
    srow = lax.broadcasted_iota(jnp.int32, (LANES, 2 * GLA_DV), 0)
    scol = lax.broadcasted_iota(jnp.int32, (LANES, 2 * GLA_DV), 1)
    own_block = (srow < GLA_DK) == (scol < GLA_DV)
    vcol = lax.broadcasted_iota(jnp.int32, (L, 2 * GLA_DV), 1)
    e_row = lax.broadcasted_iota(jnp.int32, (LANES, LANES), 0)
    e_col = lax.broadcasted_iota(jnp.int32, (LANES, LANES), 1)
    head_ones = jnp.where((e_row < GLA_DK) == (e_col < GLA_DK), 1.0, 0.0).astype(BF16)
    leaf_idx = lax.broadcasted_iota(jnp.int32, (GLA_LEAF * GLA_LEAF, LANES), 0)
    leaf_lane = lax.broadcasted_iota(jnp.int32, (GLA_LEAF * GLA_LEAF, LANES), 1) % L
    leaf_s, leaf_t = leaf_idx // GLA_LEAF, leaf_idx % GLA_LEAF
    rep = lambda a: jnp.concatenate(
        [jnp.broadcast_to(a[s:s + 1], (GLA_LEAF, a.shape[1])) for s in range(GLA_LEAF)], axis=0)
    til = lambda a: jnp.concatenate([a] * GLA_LEAF, axis=0)

    def chunk_step(ci, carry):
        cs = []
        for d, (q_ref, k_ref, v_ref, b_ref, o_ref) in enumerate(
                ((qf_ref, kf_ref, vf_ref, bf_ref, of_ref), (qr_ref, kr_ref, vr_ref, br_ref, or_ref))):
            cd = ci if d == 0 else nchunk - 1 - ci
            base = pl.multiple_of(cd * L, L)
            for p in range(npair):
                ls = slice(p * LANES, (p + 1) * LANES)
                bc = b_ref[pl.ds(base, L), ls]
                cs.append(dict(
                    d=d, p=p, base=base, o_ref=o_ref, bc=bc,
                    tau_v=rowv if d == 0 else L - 1 - rowv,
                    tau_s=(lane % L) if d == 0 else L - 1 - (lane % L),
                    q=q_ref[pl.ds(base, L), ls].astype(F32),
                    k=k_ref[pl.ds(base, L), ls].astype(F32),
                    v=v_ref[pl.ds(base, L), p * 2 * GLA_DV:(p + 1) * 2 * GLA_DV],
                    b_last=bc[L - 1:L] if d == 0 else bc[0:1],
                    state=s_scr[d, p]))

        for c in cs:
            c["o_pair"] = _dot((c["q"] * jnp.exp2(c["bc"])).astype(BF16), c["state"].astype(BF16))

        for c in cs:
            c["att"] = jnp.zeros((L, LANES), F32)
        bs = L // 2
        while bs >= GLA_LEAF:
            for c in cs:
                d, bc, tau_v = c["d"], c["bc"], c["tau_v"]
                later = (tau_v // bs) % 2 == 1
                ref_b = None
                for blk in range(L // (2 * bs)):
                    tau_ref = blk * 2 * bs + bs - 1
                    idx = tau_ref if d == 0 else L - 1 - tau_ref
                    r = jnp.broadcast_to(bc[idx:idx + 1], (L, LANES))
                    ref_b = r if ref_b is None else jnp.where(tau_v // (2 * bs) == blk, r, ref_b)
                c["qs"] = jnp.where(later, c["q"] * jnp.exp2(jnp.where(later, bc - ref_b, 0.0)), 0.0).astype(BF16)
                ks = jnp.where(later, 0.0, c["k"] * jnp.exp2(jnp.where(later, 0.0, ref_b - bc)))
                c["ks2"] = jnp.concatenate([jnp.where(lane < GLA_DK, ks, 0.0), jnp.where(lane < GLA_DK, 0.0, ks)],
                                           axis=0).astype(BF16)
            for c in cs:
                same_parent = (c["tau_v"] // (2 * bs)) == (c["tau_s"] // (2 * bs))
                c["att"] = c["att"] + jnp.where(same_parent, _dot_nt(c["qs"], c["ks2"]), 0.0)
            bs //= 2

        for c in cs:
            c["leaf"] = []
        for blk in range(L // GLA_LEAF):
            rs = slice(blk * GLA_LEAF, (blk + 1) * GLA_LEAF)
            for c in cs:
                qb, bb, kb = c["q"][rs], c["bc"][rs], c["k"][rs]
                ok = (leaf_t >= leaf_s) if c["d"] == 0 else (leaf_t <= leaf_s)
                w = jnp.where(ok, til(qb) * jnp.exp2(til(bb) - rep(bb)) * rep(kb), 0.0)
                c["a"] = _dot(w.astype(BF16), head_ones)
            for c in cs:
                a = jnp.where(leaf_lane == leaf_s + blk * GLA_LEAF, c["a"], 0.0)
                acc = a[0:GLA_LEAF]
                for s in range(1, GLA_LEAF):
                    acc = acc + a[s * GLA_LEAF:(s + 1) * GLA_LEAF]
                c["leaf"].append(acc)

        for c in cs:
            v = c["v"]
            c["att"] = (c["att"] + jnp.concatenate(c["leaf"], axis=0)).astype(BF16)
            c["v_blk"] = jnp.concatenate([jnp.where(vcol < GLA_DV, v, jnp.zeros_like(v)),
                                          jnp.where(vcol < GLA_DV, jnp.zeros_like(v), v)], axis=0)
            c["ke"] = (c["k"] * jnp.exp2(c["b_last"] - c["bc"])).astype(BF16)
            c["decay_col"] = jnp.sum(
                jnp.where(e_row == e_col, jnp.broadcast_to(jnp.exp2(c["b_last"]), (LANES, LANES)), 0.0),
                axis=1, keepdims=True)
        for c in cs:
            c["o"] = c["o_pair"] + _dot(c["att"], c["v_blk"])
            c["ds"] = _dot_tn(c["ke"], c["v"])
        for c in cs:
            p = c["p"]
            c["o_ref"][pl.ds(c["base"], L), p * 2 * GLA_DV:(p + 1) * 2 * GLA_DV] = c["o"].astype(BF16)
            s_scr[c["d"], p] = c["decay_col"] * c["state"] + jnp.where(own_block, c["ds"], 0.0)
        return carry

    lax.fori_loop(0, nchunk, chunk_step, 0)


def _gla(gq, gk, gv, bc, nct):
    bsz, ttot, nk = gq.shape
    nc = gv.shape[2]
    nt = ttot // TM
    fwd = lambda w: pl.BlockSpec((None, TM, w), lambda b, j: (b, j, 0))
    rev = lambda w: pl.BlockSpec((None, TM, w), lambda b, j: (b, _scan_block(j, nct, nt), 0))
    bfwd = pl.BlockSpec((None, TM, nk), lambda b, j: (b, j, 0))
    brev = pl.BlockSpec((None, TM, nk), lambda b, j: (b, _scan_block(j, nct, nt), 1))
    return pl.pallas_call(
        _gla_kernel,
        out_shape=(jax.ShapeDtypeStruct((bsz, ttot, nc), BF16),) * 2,
        grid=(bsz, nt),
        in_specs=[fwd(nk), fwd(nk), fwd(nc), bfwd, rev(nk), rev(nk), rev(nc), brev],
        out_specs=(fwd(nc), rev(nc)),
        scratch_shapes=[pltpu.VMEM((2, GLA_HEADS // 2, LANES, 2 * GLA_DV), F32)],
        compiler_params=_params("parallel", "arbitrary"),
        name="gla_scan",
    )(gq, gk, gv, bc, gq, gk, gv, bc)


def _na_bias_tables(rpb, rows):
    kh = NA_KH
    c = np.arange(GRID_W)
    cs = np.clip(c - NA_KW // 2, 0, GRID_W - NA_KW)
    col_ok = (c[None, :] >= cs[:, None]) & (c[None, :] < cs[:, None] + NA_KW)
    pad = GRID_W - NA_KW
    rpb_pad = jnp.pad(rpb.astype(F32) * LOG2E, ((0, 0), (0, 0), (pad, pad)))
    t1 = jnp.stack([rpb_pad[:, :, GRID_W - 1 - cq:2 * GRID_W - 1 - cq] for cq in range(GRID_W)], axis=2)
    t1 = jnp.where(col_ok, t1, NEG_BIG)
    masked = jnp.full((NA_HEADS, GRID_W, GRID_W), NEG_BIG, F32)
    tabs = []
    for r0 in (0, NA_QROWS, rows - NA_QROWS):
        kb = int(np.clip(r0 - kh // 2, 0, rows - NA_KROWS))
        per_row = []
        for qrow in range(r0, r0 + NA_QROWS):
            ws = int(np.clip(qrow - kh // 2, 0, rows - kh))
            per_row.append(jnp.concatenate(
                [t1[:, krow - qrow + NA_KH - 1] if ws <= krow < ws + kh else masked
                 for krow in range(kb, kb + NA_KROWS)], axis=-1))
        tabs.append(jnp.concatenate(per_row, axis=1))
    return jnp.stack(tabs)


def _na_out1_kernel(q_ref, k_ref, v_ref, bias_ref, h_ref, gate_ref, of_ref, or_ref, z_ref, gn_ref, w_ref, fn_ref,
                    o_ref, *, n_ctx, rows):
    rb = pl.program_id(1)
    nq = NA_QROWS * GRID_W
    nk = NA_KROWS * GRID_W
    kb = jnp.clip(rb * NA_QROWS - NA_KH // 2, 0, rows - NA_KROWS)
    start = pl.multiple_of(n_ctx + kb * GRID_W, GRID_W)
    lane = lax.broadcasted_iota(jnp.int32, (nq, LANES), 1)
    def scores(j):
        ls = slice((j // 2) * LANES, (j // 2 + 1) * LANES)
        q = q_ref[:, ls]
        qj = jnp.where((lane < NA_DH) == (j % 2 == 0), q, jnp.zeros_like(q))
        return _dot_nt(qj, k_ref[pl.ds(start, nk), ls]) + bias_ref[j], _dot_nt(qj, k_ref[0:n_ctx, ls])

    outs = []
    nxt = scores(0)
    for j in range(NA_HEADS):
        s_loc, s_ctx = nxt
        if j + 1 < NA_HEADS:
            nxt = scores(j + 1)
        ls = slice((j // 2) * LANES, (j // 2 + 1) * LANES)
        m = jnp.maximum(jnp.max(s_loc, axis=-1, keepdims=True), jnp.max(s_ctx, axis=-1, keepdims=True))
        p_loc = jnp.exp2(s_loc - m)
        p_ctx = jnp.exp2(s_ctx - m)
        den = jnp.sum(p_loc, axis=-1, keepdims=True) + jnp.sum(p_ctx, axis=-1, keepdims=True)
        outs.append((_dot(p_loc.astype(BF16), v_ref[pl.ds(start, nk), ls])
                     + _dot(p_ctx.astype(BF16), v_ref[0:n_ctx, ls])) / den)
    na = [jnp.where(lane < NA_DH, outs[2 * i], outs[2 * i + 1]) for i in range(NA_HEADS // 2)]

    f32 = lambda r: r[...].astype(F32)
    g = _head_rms(f32(of_ref) + f32(or_ref), GLA_HEADS) * gn_ref[...]
    cat = (jnp.concatenate([g] + na, axis=1) * f32(z_ref)).astype(BF16)
    h = h_ref[...] + gate_ref[...] * _dot(cat, w_ref[...])
    o_ref[...] = _rms(h) * fn_ref[...]


def _na_out1(nq, nk, nv, bias, h, gate, of, orv, z, gla_norm, w_out, final_norm, n_ctx, rows):
    bsz, ttot, nd = nq.shape
    nrb = rows // NA_QROWS
    qrows = NA_QROWS * GRID_W
    krows = NA_KROWS * GRID_W
    assert qrows == TM
    nct = n_ctx // TM
    nc = GLA_HEADS * GLA_DV

    def variant(rb):
        return jnp.where(rb == 0, 0, jnp.where(rb == nrb - 1, 2, 1))

    lat = lambda w: pl.BlockSpec((None, TM, w), lambda b, rb: (b, rb + nct, 0))
    whole = lambda w: pl.BlockSpec((None, ttot, w), lambda b, rb: (b, 0, 0))
    return pl.pallas_call(
        functools.partial(_na_out1_kernel, n_ctx=n_ctx, rows=rows),
        out_shape=jax.ShapeDtypeStruct((bsz, rows * GRID_W, D_MODEL), F32),
        grid=(bsz, nrb),
        in_specs=[lat(nd), whole(nd), whole(nd),
                  pl.BlockSpec((None, NA_HEADS, qrows, krows), lambda b, rb: (variant(rb), 0, 0, 0)),
                  lat(D_MODEL), pl.BlockSpec((None, None, 1, D_MODEL), lambda b, rb: (b, 1, 0, 0)),
                  lat(nc), lat(nc), lat(nc + nd), _full((1, nc)), _full(w_out.shape), _full((1, D_MODEL))],
        out_specs=pl.BlockSpec((None, TM, D_MODEL), lambda b, rb: (b, rb, 0)),
        compiler_params=_params("parallel", "parallel"),
        name="na_out1",
    )(nq, nk, nv, bias, h, gate, of, orv, z, gla_norm[None], w_out.astype(BF16), final_norm[None])


def kernel(x, c, ctx, c_ctx, l0_norm, l0_w_mod, l0_b_mod, l0_w_in, l0_mla_q_norm, l0_mla_w_uq, l0_mla_kv_norm, l0_mla_w_ukv, l0_mlstm_conv_w, l0_mlstm_conv_b, l0_mlstm_b_i, l0_mlstm_b_f, l0_mlstm_norm, l0_w_out, l1_norm, l1_w_mod, l1_b_mod, l1_w_in, l1_gla_w_gate, l1_gla_b_gate, l1_gla_norm, l1_na_rpb, l1_w_out, final_norm):
    bsz, seq, d = x.shape
    n_ctx = ctx.shape[1]
    rows = seq // GRID_W
    assert d == D_MODEL and seq % TM == 0 and n_ctx == TM
    assert rows % NA_QROWS == 0 and rows >= NA_KROWS and rows // NA_QROWS >= 3
    nct = n_ctx // TM

    shift, scale, gate0 = _mod_vectors(c, c_ctx, l0_w_mod, l0_b_mod)
    cos_t, sin_t = _rope_tables(n_ctx, seq)
    w0 = _proj0_weights(l0_w_in, l0_mla_q_norm, l0_mla_w_uq, l0_mla_kv_norm, l0_mla_w_ukv,
                        l0_mlstm_conv_w, l0_mlstm_conv_b, l0_mlstm_b_i, l0_mlstm_b_f)
    q, k, v, mq, mk, mv, g_rows, g_cols, mo, z0 = _proj0(ctx, x, scale, shift, l0_norm, cos_t, sin_t, w0, nct)
    a = _mla_attention(q, k, v, nct)
    hf, hr = _mlstm(mq, mk, mv, g_rows, g_cols, nct)

    shift, scale, gate = _mod_vectors(c, c_ctx, l1_w_mod, l1_b_mod)
    w1 = _proj1_weights(l1_w_in, l1_gla_w_gate, l1_gla_b_gate)
    h, gq, gk, gv, bc, nq, nk, nv, z = _mid(ctx, x, gate0, a, hf, hr, mo, z0, l0_mlstm_norm, l0_w_out,
                                            scale, shift, l1_norm, w1, nct)
    of, orv = _gla(gq, gk, gv, bc, nct)
    return _na_out1(nq, nk, nv, _na_bias_tables(l1_na_rpb, rows), h, gate, of, orv, z,
                    l1_gla_norm, l1_w_out, final_norm, n_ctx, rows)
```

```python
import functools

import jax
import jax.numpy as jnp
import numpy as np
from jax import lax
from jax.experimental import pallas as pl
from jax.experimental.pallas import tpu as pltpu

F32 = jnp.float32
BF16 = jnp.bfloat16

D_MODEL = 1024
GRID_W = 64
EPS = 1e-6
ROPE_BASE = 10000.0

MLA_HEADS = 8
MLA_Q_RANK = 384
MLA_KV_RANK = 256
MLA_NOPE = 64
MLA_ROPE = 32
MLA_V = 64
ROPE_PAIRS = MLA_ROPE // 4
MLSTM_HEADS = 4
MLSTM_DH = 128
MLSTM_CONV = 3
GLA_HEADS = 4
GLA_DK = 64
GLA_DV = 128
GLA_GATE_RANK = 16
GLA_TAU = 16.0
NA_HEADS = 8
NA_DH = 64
NA_KH = 8
NA_KW = 16

CHUNK = 64
MLSTM_CHUNK = 128
TM = 256
LANES = 128
SUBLANES = 8
HALO = SUBLANES
NA_QROWS = 4
NA_KROWS = NA_QROWS + NA_KH - 1
GLA_LEAF = 8
MLA_HPS = 4
VMEM_LIMIT = 56 * 1024 * 1024
NEG_BIG = -1e30
LOG2E = 1.4426950408889634


def _dot(a, b):
    return jnp.dot(a, b, preferred_element_type=F32)


def _dot_nt(a, b):
    return lax.dot_general(a, b, (((1,), (1,)), ((), ())), preferred_element_type=F32)


def _dot_tn(a, b):
    return lax.dot_general(a, b, (((0,), (0,)), ((), ())), preferred_element_type=F32)


def _rms(x):
    return x * lax.rsqrt(jnp.mean(x * x, axis=-1, keepdims=True) + EPS)


def _sigmoid(x):
    return 1.0 / (1.0 + jnp.exp(-x))


def _silu(x):
    return x * _sigmoid(x)


def _log_sigmoid(x):
    return jnp.minimum(x, 0.0) - jnp.log(1.0 + jnp.exp(-jnp.abs(x)))


def _params(*sem):
    return pltpu.CompilerParams(dimension_semantics=sem, vmem_limit_bytes=VMEM_LIMIT)


def _full(shape):
    nd = len(shape)
    return pl.BlockSpec(shape, lambda *_: (0,) * nd)


def _mod_kernel(c_ref, w_ref, b_ref, o_ref):
    o_ref[...] = _dot(_silu(c_ref[...]).astype(BF16), w_ref[...].astype(BF16)) + b_ref[...]


def _modulation(cc, w_mod, b_mod):
    rows, d = cc.shape
    n = w_mod.shape[1]
    return pl.pallas_call(
        _mod_kernel,
        out_shape=jax.ShapeDtypeStruct((rows, n), F32),
        grid=(n // d,),
        in_specs=[_full((rows, d)), pl.BlockSpec((d, d), lambda j: (0, j)), pl.BlockSpec((1, d), lambda j: (0, j))],
        out_specs=pl.BlockSpec((rows, d), lambda j: (0, j)),
        compiler_params=_params("parallel"),
        name="modulation",
    )(cc, w_mod, b_mod[None])


def _mod_vectors(c, c_ctx, w_mod, b_mod):
    bsz = c.shape[0]
    rows = -(-(bsz + 1) // SUBLANES) * SUBLANES
    cc = jnp.zeros((rows, D_MODEL), F32).at[:bsz].set(c).at[bsz].set(c_ctx)
    mod = _modulation(cc, w_mod, b_mod)
    out = []
    for part in jnp.split(mod, 3, axis=-1):
        ctx_v = jnp.broadcast_to(part[bsz][None], (bsz, D_MODEL))
        out.append(jnp.stack([ctx_v, part[:bsz]], axis=1)[:, :, None, :])
    return out


def _modnorm(x, g, scale, shift):
    return _rms(x) * g * (1.0 + scale) + shift


def _seg_scan(x, reverse, op, fill):
    width = x.shape[1]
    pos = lax.broadcasted_iota(jnp.int32, x.shape, 1) % MLSTM_CHUNK
    k = 1
    while k < MLSTM_CHUNK:
        if reverse:
            x = op(x, jnp.where(pos < MLSTM_CHUNK - k, pltpu.roll(x, width - k, axis=1), fill))
        else:
            x = op(x, jnp.where(pos >= k, pltpu.roll(x, k, axis=1), fill))
        k *= 2
    return x


def _split3(x):
    hi = x.astype(BF16)
    r1 = x - hi.astype(F32)
    mid = r1.astype(BF16)
    return hi, mid, (r1 - mid.astype(F32)).astype(BF16)


def _proj0_kernel(ctx_ref, lat_ref, hp_ref, hn_ref, sc_ref, sh_ref, g_ref, cos_ref, sin_ref,
                  wcq_ref, wckv_ref, wkr_ref, wmqk_ref, wmv_ref, wmo_ref, wgt_ref, wz_ref,
                  qn_ref, wuq_ref, kvn_ref, wk_ref, wv_ref, cw_ref, cb_ref, gb_ref,
                  q_out, k_out, v_out, mq_out, mk_out, mv_out, gr_out, gc_out, mo_out, z_out,
                  pbuf, *, nct, nt):
    t = pl.program_id(1)
    h = jnp.where(t < nct, ctx_ref[...], lat_ref[...])
    x = jnp.concatenate([hp_ref[...], h, hn_ref[...]], axis=0)
    ub_ext = _modnorm(x, g_ref[...], sc_ref[...], sh_ref[...]).astype(BF16)
    ub = ub_ext[HALO:HALO + TM]

    gt = _dot_nt(wgt_ref[...], ub) + gb_ref[...]
    half = MLSTM_HEADS
    per_dir = []
    for d in range(2):
        gd = gt[d * SUBLANES:(d + 1) * SUBLANES]
        bcum = _seg_scan(_log_sigmoid(gd), d == 1, jnp.add, 0.0)
        b_top = pltpu.roll(bcum, half, axis=0)
        c8 = gd - b_top
        pm8 = _seg_scan(c8, d == 1, jnp.maximum, -jnp.inf)
        per_dir.append((c8, pm8, b_top))

    cos = cos_ref[...]
    sin = sin_ref[...]
    cos_all = jnp.concatenate([cos] * MLA_HEADS, axis=1)
    sin_all = jnp.concatenate([sin] * MLA_HEADS, axis=1)

    def swap_halves(a):
        lane = lax.broadcasted_iota(jnp.int32, a.shape, 1)
        first = lane % (2 * ROPE_PAIRS) < ROPE_PAIRS
        return jnp.where(first, pltpu.roll(a, a.shape[1] - ROPE_PAIRS, axis=1), pltpu.roll(a, ROPE_PAIRS, axis=1))

    cq = _dot(ub, wcq_ref[...])
    ckv = _dot(ub, wckv_ref[...])
    kr = _dot(ub, wkr_ref[...])
    pqk = _dot(ub_ext, wmqk_ref[...])

    nq = (_rms(cq) * qn_ref[...]).astype(BF16)
    nkv = (_rms(ckv) * kvn_ref[...]).astype(BF16)

    q = _dot(nq, wuq_ref[...])
    kn = _dot(nkv, wk_ref[...])
    vv = _dot(nkv, wv_ref[...])

    prev_ok = t > nct
    next_ok = jnp.logical_and(t >= nct, t != nt - 1)
    row = lax.broadcasted_iota(jnp.int32, pqk.shape, 0)
    keep = jnp.logical_and(jnp.logical_or(row >= HALO, prev_ok), jnp.logical_or(row < HALO + TM, next_ok))
    pbuf[...] = jnp.where(keep, pqk, 0.0)
    cw = cw_ref[...]
    y = (pbuf[HALO - 1:HALO - 1 + TM, :] * cw[0:1] + pbuf[HALO:HALO + TM, :] * cw[1:2]
         + pbuf[HALO + 1:HALO + 1 + TM, :] * cw[2:3] + cb_ref[...])
    y = _silu(y)
    nb = MLSTM_HEADS * MLSTM_DH
    mq_out[...] = (y[:, :nb] * MLSTM_DH ** -0.5).astype(BF16)
    mk_out[...] = y[:, nb:].astype(BF16)

    mv = _dot(ub, wmv_ref[...])
    mo = _dot(ub, wmo_ref[...])

    q = q * cos_all + swap_halves(q) * sin_all
    q_out[...] = (q * ((MLA_NOPE + MLA_ROPE) ** -0.5 * LOG2E)).astype(BF16)
    rot = kr * cos + swap_halves(kr) * sin
    k_out[...] = (kn + jnp.concatenate([rot] * MLA_HEADS, axis=1)).astype(BF16)
    vlane = lax.broadcasted_iota(jnp.int32, (TM, MLA_HEADS * LANES), 1)
    is_value = ((vlane % LANES) < MLA_V) == ((vlane // LANES) % 2 == 0)
    v_out[...] = jnp.where(is_value, vv, 1.0).astype(BF16)

    zz = _dot(ub, wz_ref[...])
    mv_out[...] = mv.astype(BF16)
    mo_out[...] = _sigmoid(mo).astype(BF16)
    z_out[...] = _silu(zz).astype(BF16)

    r8 =lax.broadcasted_iota(jnp.int32, (SUBLANES, TM), 0)
    both = lambda i: jnp.where(r8 < half, per_dir[0][i], pltpu.roll(per_dir[1][i], half, axis=0)) * LOG2E
    c_rows, pm_rows, b_rows = both(0), both(1), both(2)
    for ci in range(TM // MLSTM_CHUNK):
        gr_out[ci] = c_rows[:, ci * MLSTM_CHUNK:(ci + 1) * MLSTM_CHUNK]
    stack = jnp.concatenate([c_rows, pm_rows, b_rows, jnp.zeros((LANES - 3 * SUBLANES, TM), F32)], axis=0)
    er = lax.broadcasted_iota(jnp.int32, (TM, TM), 0)
    ec = lax.broadcasted_iota(jnp.int32, (TM, TM), 1)
    eye = jnp.where(er == ec, 1.0, 0.0).astype(BF16)
    gc_out[...] = sum(_dot_nt(eye, part) for part in _split3(stack))


def _rope_tables(n_ctx, seq):
    t = jnp.arange(seq)
    inv = 1.0 / (ROPE_BASE ** (jnp.arange(ROPE_PAIRS, dtype=F32) / ROPE_PAIRS))
    ang = jnp.concatenate([(t // GRID_W)[:, None] * inv, (t % GRID_W)[:, None] * inv], axis=-1)
    cos, sin = jnp.cos(ang), jnp.sin(ang)
    j = np.arange(MLA_ROPE)
    src = (j // (2 * ROPE_PAIRS)) * ROPE_PAIRS + (j % ROPE_PAIRS)
    sign = np.where((j % (2 * ROPE_PAIRS)) < ROPE_PAIRS, -1.0, 1.0).astype(np.float32)
    cos_full = cos[:, src]
    sin_full = sin[:, src] * sign
    cos_t = jnp.ones((n_ctx + seq, LANES), F32).at[n_ctx:, MLA_NOPE:MLA_NOPE + MLA_ROPE].set(cos_full)
    sin_t = jnp.zeros((n_ctx + seq, LANES), F32).at[n_ctx:, MLA_NOPE:MLA_NOPE + MLA_ROPE].set(sin_full)
    return cos_t, sin_t


def _proj0_weights(w_in, q_norm, w_uq, kv_norm, w_ukv, conv_w, conv_b, b_i, b_f):
    o = np.cumsum([0, MLA_Q_RANK, MLA_KV_RANK, MLA_ROPE] + [MLSTM_HEADS * MLSTM_DH] * 4 + [4 * MLSTM_HEADS, D_MODEL])
    dq = MLA_NOPE + MLA_ROPE
    wkr = jnp.pad(w_in[:, o[2]:o[3]], ((0, 0), (MLA_NOPE, LANES - dq)))
    wuq = jnp.pad(w_uq.reshape(MLA_Q_RANK, MLA_HEADS, dq), ((0, 0), (0, 0), (0, LANES - dq)))
    wkv = w_ukv.reshape(MLA_KV_RANK, MLA_HEADS, MLA_NOPE + MLA_V)
    wk = jnp.pad(wkv[:, :, :MLA_NOPE], ((0, 0), (0, 0), (0, LANES - MLA_NOPE)))
    zv = jnp.zeros((MLA_KV_RANK, MLA_HEADS // 2, MLA_V), F32)
    wv = jnp.concatenate([wkv[:, 0::2, MLA_NOPE:], zv, zv, wkv[:, 1::2, MLA_NOPE:]], axis=-1)
    wv = wv.reshape(MLA_KV_RANK, MLA_HEADS * LANES)
    gbias = jnp.stack([b_i, b_f], axis=1).reshape(4 * MLSTM_HEADS, 1)
    bf = lambda a: a.astype(BF16)
    return dict(
        wcq=bf(w_in[:, o[0]:o[1]]), wckv=bf(w_in[:, o[1]:o[2]]),
        wkr=bf(wkr),
        wmqk=bf(w_in[:, o[3]:o[5]]), wmv=bf(w_in[:, o[5]:o[6]]), wmo=bf(w_in[:, o[6]:o[7]]),
        wgt=bf(w_in[:, o[7]:o[8]].T), wz=bf(w_in[:, o[8]:o[9]]),
        qn=q_norm[None], wuq=bf(wuq.reshape(MLA_Q_RANK, -1)),
        kvn=kv_norm[None], wk=bf(wk.reshape(MLA_KV_RANK, -1)), wv=bf(wv),
        cw=jnp.zeros((SUBLANES, conv_w.shape[1]), F32).at[:MLSTM_CONV].set(conv_w), cb=conv_b[None], gb=gbias)


def _tile_specs(nct, nt):
    tile = lambda w: pl.BlockSpec((None, TM, w), lambda b, t: (b, t, 0))
    per = TM // HALO
    nlat = nt - nct
    src_ctx = pl.BlockSpec((None, TM, D_MODEL), lambda b, t: (b, jnp.minimum(t, nct - 1), 0))
    src_lat = pl.BlockSpec((None, TM, D_MODEL), lambda b, t: (b, jnp.maximum(t - nct, 0), 0))
    prev = pl.BlockSpec((None, HALO, D_MODEL), lambda b, t: (b, jnp.maximum((t - nct) * per - 1, 0), 0))
    nxt = pl.BlockSpec((None, HALO, D_MODEL),
                       lambda b, t: (b, jnp.clip((t - nct + 1) * per, 0, nlat * per - 1), 0))
    mod = pl.BlockSpec((None, None, 1, D_MODEL), lambda b, t: (b, (t >= nct).astype(jnp.int32), 0, 0))
    return tile, src_ctx, src_lat, prev, nxt, mod


def _proj0(ctx, x, scale, shift, norm_g, cos_t, sin_t, w, nct):
    bsz = x.shape[0]
    ttot = ctx.shape[1] + x.shape[1]
    nt = ttot // TM
    nb = MLSTM_HEADS * MLSTM_DH
    tile, src_ctx, src_lat, prev, nxt, mod = _tile_specs(nct, nt)
    tab = pl.BlockSpec((TM, LANES), lambda b, t: (t, 0))
    wnames = ["wcq", "wckv", "wkr", "wmqk", "wmv", "wmo", "wgt", "wz",
              "qn", "wuq", "kvn", "wk", "wv", "cw", "cb", "gb"]
    ws = [w[n] for n in wnames]
    tok = lambda width, dt: jax.ShapeDtypeStruct((bsz, ttot, width), dt)
    out_shape = (tok(MLA_HEADS * LANES, BF16), tok(MLA_HEADS * LANES, BF16), tok(MLA_HEADS * LANES, BF16),
                 tok(nb, BF16), tok(nb, BF16), tok(nb, BF16),
                 jax.ShapeDtypeStruct((bsz, ttot // MLSTM_CHUNK, 2 * MLSTM_HEADS, MLSTM_CHUNK), F32), tok(LANES, F32),
                 tok(nb, BF16), tok(D_MODEL, BF16))
    out_specs = (tile(MLA_HEADS * LANES), tile(MLA_HEADS * LANES), tile(MLA_HEADS * LANES),
                 tile(nb), tile(nb), tile(nb),
                 pl.BlockSpec((None, TM // MLSTM_CHUNK, 2 * MLSTM_HEADS, MLSTM_CHUNK), lambda b, t: (b, t, 0, 0)),
                 tile(LANES),
                 tile(nb), tile(D_MODEL))
    return pl.pallas_call(
        functools.partial(_proj0_kernel, nct=nct, nt=nt),
        out_shape=out_shape,
        grid=(bsz, nt),
        in_specs=[src_ctx, src_lat, prev, nxt, mod, mod, _full((1, D_MODEL)), tab, tab] + [_full(a.shape) for a in ws],
        out_specs=out_specs,
        scratch_shapes=[pltpu.VMEM((TM + 2 * HALO, 2 * nb), F32)],
        compiler_params=_params("parallel", "parallel"),
        name="proj0",
    )(ctx, x, x, x, scale, shift, norm_g[None], cos_t, sin_t, *ws)


def _mla_kernel(q_ref, k_ref, v_ref, o_ref, *, nct):
    t = pl.program_id(2)
    lane = lax.broadcasted_iota(jnp.int32, (TM, LANES), 1)

    def attend(nkeys):
        qk = lambda j: _dot_nt(q_ref[:, j * LANES:(j + 1) * LANES], k_ref[0:nkeys, j * LANES:(j + 1) * LANES])
        outs = []
        s_next = qk(0)
        for j in range(MLA_HPS):
            s = s_next
            if j + 1 < MLA_HPS:
                s_next = qk(j + 1)
            p = jnp.exp2(s - jnp.max(s, axis=-1, keepdims=True))
            o = _dot(p.astype(BF16), v_ref[0:nkeys, j * LANES:(j + 1) * LANES])
            outs.append(o / pltpu.roll(o, MLA_V, axis=1))
        for j in range(MLA_HPS // 2):
            o_ref[:, j * LANES:(j + 1) * LANES] = jnp.where(lane < MLA_V, outs[2 * j], outs[2 * j + 1]).astype(BF16)

    @pl.when(t < nct)
    def _():
        attend(nct * TM)

    @pl.when(t >= nct)
    def _():
        attend(k_ref.shape[0])


def _mla_attention(q, k, v, nct):
    bsz, ttot, _ = q.shape
    nt = ttot // TM
    hw = MLA_HPS * LANES
    return pl.pallas_call(
        functools.partial(_mla_kernel, nct=nct),
        out_shape=jax.ShapeDtypeStruct((bsz, ttot, MLA_HEADS * MLA_V), BF16),
        grid=(bsz, MLA_HEADS // MLA_HPS, nt),
        in_specs=[pl.BlockSpec((None, TM, hw), lambda b, p, t: (b, t, p)),
                  pl.BlockSpec((None, ttot, hw), lambda b, p, t: (b, 0, p)),
                  pl.BlockSpec((None, ttot, hw), lambda b, p, t: (b, 0, p))],
        out_specs=pl.BlockSpec((None, TM, hw // 2), lambda b, p, t: (b, t, p)),
        compiler_params=_params("parallel", "parallel", "parallel"),
        name="mla_attention",
    )(q, k, v)


def _scan_block(j, nct, nt):
    return jnp.where(j < nct, nct - 1 - j, nt - 1 - (j - nct))


def _mlstm_kernel(qf_ref, kf_ref, vf_ref, grf_ref, gcf_ref, qr_ref, kr_ref, vr_ref, grr_ref, gcr_ref,
                  hf_ref, hr_ref, c_scr, m_scr):
    @pl.when(pl.program_id(1) == 0)
    def _():
        c_scr[...] = jnp.zeros_like(c_scr)
        m_scr[...] = jnp.zeros_like(m_scr)

    L = MLSTM_CHUNK
    nchunk = TM // L
    nh = MLSTM_HEADS
    nchain = 2 * nh
    row = lax.broadcasted_iota(jnp.int32, (L, L), 0)
    col = lax.broadcasted_iota(jnp.int32, (L, L), 1)
    ones = jnp.ones((L, MLSTM_DH), BF16)
    wide = lambda a: jnp.broadcast_to(a, (L, MLSTM_DH))
    twice = lambda a: jnp.concatenate([a, a], axis=1)
    refs = ((qf_ref, kf_ref, vf_ref, grf_ref, gcf_ref, hf_ref), (qr_ref, kr_ref, vr_ref, grr_ref, gcr_ref, hr_ref))

    def load(ci):
        cs = []
        for d, (q_ref, k_ref, v_ref, gr_ref, gc_ref, o_ref) in enumerate(refs):
            cd = ci if d == 0 else nchunk - 1 - ci
            rs = slice(cd * L, (cd + 1) * L)
            g_rows = gr_ref[cd]
            g_cols = gc_ref[rs, :]
            last = L - 1 if d == 0 else 0
            for hd in range(nh):
                ch = d * nh + hd
                hs = slice(hd * MLSTM_DH, (hd + 1) * MLSTM_DH)
                cs.append(dict(
                    rs=rs, hs=hs, o_ref=o_ref, causal=(col <= row) if d == 0 else (col >= row),
                    c_row=g_rows[ch:ch + 1], c_col=g_cols[:, ch:ch + 1],
                    pm_col=g_cols[:, nchain + ch:nchain + ch + 1], b_col=g_cols[:, 2 * nchain + ch:2 * nchain + ch + 1],
                    pm_last=g_cols[last:last + 1, nchain + ch:nchain + ch + 1],
                    b_last=g_cols[last:last + 1, 2 * nchain + ch:2 * nchain + ch + 1],
                    q=q_ref[rs, hs], k=k_ref[rs, hs],
                    v_ext=jnp.concatenate([v_ref[rs, hs], ones], axis=1)))
        return cs

    def prep_scores(cs):
        for c in cs:
            c["s"] = _dot_nt(c["q"], c["k"])

    def prep_weights(cs):
        for c in cs:
            c["pm_w"] = wide(c["pm_col"])
            c["b_w"] = wide(c["b_col"])
            c["w"] = jnp.exp2(jnp.where(c["causal"], c["c_row"] - c["pm_w"][:, :L], -jnp.inf))
            c["kwf"] = jnp.exp2(wide(c["c_col"] - c["pm_last"]))

    def prep_operands(cs):
        for c in cs:
            c["p"] = (c["s"] * c["w"]).astype(BF16)
            c["kw"] = (c["kwf"] * c["k"].astype(F32)).astype(BF16)

    def prep_products(cs):
        for c in cs:
            c["o1"] = _dot(c["p"], c["v_ext"])
            c["dc"] = _dot_tn(c["kw"], c["v_ext"])

    def scan_read(cs):
        for ch, c in enumerate(cs):
            c["qc"] = _dot(c["q"], state[ch].astype(BF16))

    def scan_factors(cs):
        for ch, c in enumerate(cs):
            g_w = jnp.maximum(m[ch], c["pm_w"])
            c["e_intra"] = jnp.exp2(c["pm_w"] - g_w)
            c["e_inter"] = jnp.exp2(m[ch] - g_w)
            c["floor"] = jnp.exp2(-(c["b_w"] + g_w))
            g_last = jnp.maximum(m[ch], c["pm_last"])
            c["keep"] = jnp.exp2(m[ch] - g_last)
            c["gain"] = jnp.exp2(c["pm_last"] - g_last)
            c["m_new"] = c["b_last"] + g_last

    def scan_update(cs):
        for ch, c in enumerate(cs):
            o = twice(c["e_intra"]) * c["o1"] + twice(c["e_inter"]) * c["qc"]
            h = o[:, :MLSTM_DH] / jnp.maximum(jnp.abs(o[:, MLSTM_DH:]), c["floor"])
            c["o_ref"][c["rs"], c["hs"]] = h.astype(BF16)
            state[ch] = c["keep"] * state[ch] + c["gain"] * c["dc"]
            m[ch] = c["m_new"]

    state = [c_scr[ch] for ch in range(nchain)]
    m = [m_scr[ch, 0:1, 0:1] for ch in range(nchain)]
    chunks = [load(ci) for ci in range(nchunk)]
    everything = [c for cs in chunks for c in cs]
    for stage in (prep_scores, prep_weights, prep_operands):
        stage(everything)
    prep_products(chunks[0])
    for i, cs in enumerate(chunks):
        scan_read(cs)
        if i + 1 < nchunk:
            prep_products(chunks[i + 1])
        scan_factors(cs)
        scan_update(cs)
    for ch in range(nchain):
        c_scr[ch] = state[ch]
        m_scr[ch] = jnp.broadcast_to(m[ch], m_scr.shape[1:])


def _mlstm(mq, mk, mv, g_rows, g_cols, nct):
    bsz, ttot, nb = mq.shape
    nt = ttot // TM
    per = TM // MLSTM_CHUNK
    fwd = lambda w: pl.BlockSpec((None, TM, w), lambda b, j: (b, j, 0))
    rev = lambda w: pl.BlockSpec((None, TM, w), lambda b, j: (b, _scan_block(j, nct, nt), 0))
    gshape = (None, per, 2 * MLSTM_HEADS, MLSTM_CHUNK)
    gfwd = pl.BlockSpec(gshape, lambda b, j: (b, j, 0, 0))
    grev = pl.BlockSpec(gshape, lambda b, j: (b, _scan_block(j, nct, nt), 0, 0))
    return pl.pallas_call(
        _mlstm_kernel,
        out_shape=(jax.ShapeDtypeStruct((bsz, ttot, nb), BF16),) * 2,
        grid=(bsz, nt),
        in_specs=[fwd(nb), fwd(nb), fwd(nb), gfwd, fwd(LANES), rev(nb), rev(nb), rev(nb), grev, rev(LANES)],
        out_specs=(fwd(nb), rev(nb)),
        scratch_shapes=[pltpu.VMEM((2 * MLSTM_HEADS, MLSTM_DH, 2 * MLSTM_DH), F32),
                        pltpu.VMEM((2 * MLSTM_HEADS, SUBLANES, LANES), F32)],
        compiler_params=_params("parallel", "arbitrary"),
        name="mlstm_scan",
    )(mq, mk, mv, g_rows, g_cols, mq, mk, mv, g_rows, g_cols)


def _head_rms(x, n_heads):
    w = x.shape[1] // n_heads
    return jnp.concatenate([_rms(x[:, i * w:(i + 1) * w]) for i in range(n_heads)], axis=1)


def _mid_kernel(ctx_ref, lat_ref, gate_ref, a_ref, hf_ref, hr_ref, mo_ref, z0_ref, hn_ref, wout_ref,
                sc_ref, sh_ref, g_ref, wq_ref, wk_ref, wv_ref, wga_ref, wnq_ref, wnk_ref, wnv_ref, wz_ref,
                wgate_ref, bgate_ref,
                h_out, gq_out, gk_out, gv_out, bc_out, nq_out, nk_out, nv_out, z_out, *, nct):
    f32 = lambda r: r[...].astype(F32)
    hm = _head_rms(f32(mo_ref) * (f32(hf_ref) + f32(hr_ref)), MLSTM_HEADS) * hn_ref[...]
    cat = (jnp.concatenate([f32(a_ref), hm], axis=1) * f32(z0_ref)).astype(BF16)
    h0 = jnp.where(pl.program_id(1) < nct, ctx_ref[...], lat_ref[...])
    h = h0 + gate_ref[...] * _dot(cat, wout_ref[...])
    h_out[...] = h

    ub = _modnorm(h, g_ref[...], sc_ref[...], sh_ref[...]).astype(BF16)
    ga = _dot(ub, wga_ref[...]).astype(BF16)
    gate_pre = _dot(ga, wgate_ref[...])
    gq_out[...] = (_dot(ub, wq_ref[...]) * GLA_DK ** -0.5).astype(BF16)
    gk_out[...] = _dot(ub, wk_ref[...]).astype(BF16)
    gv_out[...] = _dot(ub, wv_ref[...]).astype(BF16)

    lg = _log_sigmoid(gate_pre + bgate_ref[...]) * (1.0 / GLA_TAU)
    hi, mid, lo = _split3(lg)
    nq_out[...] = (_dot(ub, wnq_ref[...]) * (NA_DH ** -0.5 * LOG2E)).astype(BF16)
    nk_out[...] = _dot(ub, wnk_ref[...]).astype(BF16)
    nv_out[...] = _dot(ub, wnv_ref[...]).astype(BF16)
    row = lax.broadcasted_iota(jnp.int32, (TM, TM), 0)
    col = lax.broadcasted_iota(jnp.int32, (TM, TM), 1)
    same = (row // CHUNK) == (col // CHUNK)
    nk = GLA_HEADS * GLA_DK
    for d in range(2):
        tri = jnp.where(jnp.logical_and(same, (col <= row) if d == 0 else (col >= row)), 1.0, 0.0).astype(BF16)
        cols = slice(d * nk, (d + 1) * nk)
        bc_out[:, cols] = (_dot(tri, hi[:, cols]) + _dot(tri, mid[:, cols]) + _dot(tri, lo[:, cols])) * LOG2E
    z_out[...] = _silu(_dot(ub, wz_ref[...])).astype(BF16)


def _proj1_weights(w_in, w_gate, b_gate):
    nk = GLA_HEADS * GLA_DK
    nc = GLA_HEADS * GLA_DV
    nd = NA_HEADS * NA_DH
    o = np.cumsum([0, nk, nk, nc, 2 * GLA_GATE_RANK, nd, nd, nd, nc + nd])
    r = GLA_GATE_RANK
    wgate = jnp.zeros((2 * r, 2 * nk), F32).at[:r, :nk].set(w_gate[0]).at[r:, nk:].set(w_gate[1])
    bf = lambda a: a.astype(BF16)
    names = ["wq", "wk", "wv", "wga", "wnq", "wnk", "wnv", "wz"]
    w = {n: bf(w_in[:, o[i]:o[i + 1]]) for i, n in enumerate(names)}
    w["wgate"] = bf(wgate)
    w["bgate"] = b_gate.reshape(1, 2 * nk)
    return w


def _mid(ctx, x, gate0, a, hf, hr, mo, z0, h_norm, w_out, scale, shift, norm_g, w, nct):
    bsz = x.shape[0]
    ttot = ctx.shape[1] + x.shape[1]
    nt = ttot // TM
    tile, src_ctx, src_lat, _, _, mod = _tile_specs(nct, nt)
    nb = MLSTM_HEADS * MLSTM_DH
    nk = GLA_HEADS * GLA_DK
    nc = GLA_HEADS * GLA_DV
    nd = NA_HEADS * NA_DH
    wnames = ["wq", "wk", "wv", "wga", "wnq", "wnk", "wnv", "wz", "wgate", "bgate"]
    ws = [w[n] for n in wnames]
    tok = lambda width, dt: jax.ShapeDtypeStruct((bsz, ttot, width), dt)
    return pl.pallas_call(
        functools.partial(_mid_kernel, nct=nct),
        out_shape=(tok(D_MODEL, F32), tok(nk, BF16), tok(nk, BF16), tok(nc, BF16), tok(2 * nk, F32),
                   tok(nd, BF16), tok(nd, BF16), tok(nd, BF16), tok(nc + nd, BF16)),
        grid=(bsz, nt),
        in_specs=[src_ctx, src_lat, mod, tile(MLA_HEADS * MLA_V), tile(nb), tile(nb), tile(nb), tile(D_MODEL),
                  _full((1, nb)), _full(w_out.shape), mod, mod, _full((1, D_MODEL))] + [_full(a.shape) for a in ws],
        out_specs=(tile(D_MODEL), tile(nk), tile(nk), tile(nc), tile(2 * nk), tile(nd), tile(nd), tile(nd),
                   tile(nc + nd)),
        compiler_params=_params("parallel", "parallel"),
        name="out0_proj1",
    )(ctx, x, gate0, a, hf, hr, mo, z0, h_norm[None], w_out.astype(BF16), scale, shift, norm_g[None], *ws)


def _gla_kernel(qf_ref, kf_ref, vf_ref, bf_ref, qr_ref, kr_ref, vr_ref, br_ref, of_ref, or_ref, s_scr):
    @pl.when(pl.program_id(1) == 0)
    def _():
        s_scr[...] = jnp.zeros_like(s_scr)

    L = CHUNK
    nchunk = TM // L
    npair = GLA_HEADS // 2
    lane = lax.broadcasted_iota(jnp.int32, (L, LANES), 1)
    rowv = lax.broadcasted_iota(jnp.int32, (L, LANES), 0)
    srow = lax.broadcasted_iota(jnp.int32, (LANES, 2 * GLA_DV), 0)
    scol = lax.broadcasted_iota(jnp.int32, (LANES, 2 * GLA_DV), 1)
    own_block = (srow < GLA_DK) == (scol < GLA_DV)
    vcol = lax.broadcasted_iota(jnp.int32, (L, 2 * GLA_DV), 1)
    e_row = lax.broadcasted_iota(jnp.int32, (LANES, LANES), 0)
    e_col = lax.broadcasted_iota(jnp.int32, (LANES, LANES), 1)
    head_ones = jnp.where((e_row < GLA_DK) == (e_col < GLA_DK), 1.0, 0.0).astype(BF16)
    leaf_idx = lax.broadcasted_iota(jnp.int32, (GLA_LEAF * GLA_LEAF, LANES), 0)
    leaf_lane = lax.broadcasted_iota(jnp.int32, (GLA_LEAF * GLA_LEAF, LANES), 1) % L
    leaf_s, leaf_t = leaf_idx // GLA_LEAF, leaf_idx % GLA_LEAF
    rep = lambda a: jnp.concatenate(
        [jnp.broadcast_to(a[s:s + 1], (GLA_LEAF, a.shape[1])) for s in range(GLA_LEAF)], axis=0)
    til = lambda a: jnp.concatenate([a] * GLA_LEAF, axis=0)

    def chunk_step(ci, carry):
        cs = []
        for d, (q_ref, k_ref, v_ref, b_ref, o_ref) in enumerate(
                ((qf_ref, kf_ref, vf_ref, bf_ref, of_ref), (qr_ref, kr_ref, vr_ref, br_ref, or_ref))):
            cd = ci if d == 0 else nchunk - 1 - ci
            base = pl.multiple_of(cd * L, L)
            for p in range(npair):
                ls = slice(p * LANES, (p + 1) * LANES)
                bc = b_ref[pl.ds(base, L), ls]
                cs.append(dict(
                    d=d, p=p, base=base, o_ref=o_ref, bc=bc,
                    tau_v=rowv if d == 0 else L - 1 - rowv,
                    tau_s=(lane % L) if d == 0 else L - 1 - (lane % L),
                    q=q_ref[pl.ds(base, L), ls].astype(F32),
                    k=k_ref[pl.ds(base, L), ls].astype(F32),
                    v=v_ref[pl.ds(base, L), p * 2 * GLA_DV:(p + 1) * 2 * GLA_DV],
                    b_last=bc[L - 1:L] if d == 0 else bc[0:1],
                    state=s_scr[d, p]))

        for c in cs:
            c["o_pair"] = _dot((c["q"] * jnp.exp2(c["bc"])).astype(BF16), c["state"].astype(BF16))

        for c in cs:
            c["att"] = jnp.zeros((L, LANES), F32)
        bs = L // 2
        while bs >= GLA_LEAF:
            for c in cs:
                d, bc, tau_v = c["d"], c["bc"], c["tau_v"]
                later = (tau_v // bs) % 2 == 1
                ref_b = None
                for blk in range(L // (2 * bs)):
                    tau_ref = blk * 2 * bs + bs - 1
                    idx = tau_ref if d == 0 else L - 1 - tau_ref
                    r = jnp.broadcast_to(bc[idx:idx + 1], (L, LANES))
                    ref_b = r if ref_b is None else jnp.where(tau_v // (2 * bs) == blk, r, ref_b)
                c["qs"] = jnp.where(later, c["q"] * jnp.exp2(jnp.where(later, bc - ref_b, 0.0)), 0.0).astype(BF16)
                ks = jnp.where(later, 0.0, c["k"] * jnp.exp2(jnp.where(later, 0.0, ref_b - bc)))
                c["ks2"] = jnp.concatenate([jnp.where(lane < GLA_DK, ks, 0.0), jnp.where(lane < GLA_DK, 0.0, ks)],
                                           axis=0).astype(BF16)
            for c in cs:
                same_parent = (c["tau_v"] // (2 * bs)) == (c["tau_s"] // (2 * bs))
                c["att"] = c["att"] + jnp.where(same_parent, _dot_nt(c["qs"], c["ks2"]), 0.0)
            bs //= 2

        for c in cs:
            c["leaf"] = []
        for blk in range(L // GLA_LEAF):
            rs = slice(blk * GLA_LEAF, (blk + 1) * GLA_LEAF)
            for c in cs:
                qb, bb, kb = c["q"][rs], c["bc"][rs], c["k"][rs]
                ok = (leaf_t >= leaf_s) if c["d"] == 0 else (leaf_t <= leaf_s)
                w = jnp.where(ok, til(qb) * jnp.exp2(til(bb) - rep(bb)) * rep(kb), 0.0)
                c["a"] = _dot(w.astype(BF16), head_ones)
            for c in cs:
                a = jnp.where(leaf_lane == leaf_s + blk * GLA_LEAF, c["a"], 0.0)
                acc = a[0:GLA_LEAF]
                for s in range(1, GLA_LEAF):
                    acc = acc + a[s * GLA_LEAF:(s + 1) * GLA_LEAF]
                c["leaf"].append(acc)

        for c in cs:
            v = c["v"]
            c["att"] = (c["att"] + jnp.concatenate(c["leaf"], axis=0)).astype(BF16)
            c["v_blk"] = jnp.concatenate([jnp.where(vcol < GLA_DV, v, jnp.zeros_like(v)),
                                          jnp.where(vcol < GLA_DV, jnp.zeros_like(v), v)], axis=0)
            c["ke"] = (c["k"] * jnp.exp2(c["b_last"] - c["bc"])).astype(BF16)
            c["decay_col"] = jnp.sum(
                jnp.where(e_row == e_col, jnp.broadcast_to(jnp.exp2(c["b_last"]), (LANES, LANES)), 0.0),
                axis=1, keepdims=True)
        for c in cs:
            c["o"] = c["o_pair"] + _dot(c["att"], c["v_blk"])
            c["ds"] = _dot_tn(c["ke"], c["v"])
        for c in cs:
            p = c["p"]
            c["o_ref"][pl.ds(c["base"], L), p * 2 * GLA_DV:(p + 1) * 2 * GLA_DV] = c["o"].astype(BF16)
            s_scr[c["d"], p] = c["decay_col"] * c["state"] + jnp.where(own_block, c["ds"], 0.0)
        return carry

    lax.fori_loop(0, nchunk, chunk_step, 0)


def _gla(gq, gk, gv, bc, nct):
    bsz, ttot, nk = gq.shape
    nc = gv.shape[2]
    nt = ttot // TM
    fwd = lambda w: pl.BlockSpec((None, TM, w), lambda b, j: (b, j, 0))
    rev = lambda w: pl.BlockSpec((None, TM, w), lambda b, j: (b, _scan_block(j, nct, nt), 0))
    bfwd = pl.BlockSpec((None, TM, nk), lambda b, j: (b, j, 0))
    brev = pl.BlockSpec((None, TM, nk), lambda b, j: (b, _scan_block(j, nct, nt), 1))
    return pl.pallas_call(
        _gla_kernel,
        out_shape=(jax.ShapeDtypeStruct((bsz, ttot, nc), BF16),) * 2,
        grid=(bsz, nt),
        in_specs=[fwd(nk), fwd(nk), fwd(nc), bfwd, rev(nk), rev(nk), rev(nc), brev],
        out_specs=(fwd(nc), rev(nc)),
        scratch_shapes=[pltpu.VMEM((2, GLA_HEADS // 2, LANES, 2 * GLA_DV), F32)],
        compiler_params=_params("parallel", "arbitrary"),
        name="gla_scan",
    )(gq, gk, gv, bc, gq, gk, gv, bc)


def _na_bias_tables(rpb, rows):
    kh = NA_KH
    c = np.arange(GRID_W)
    cs = np.clip(c - NA_KW // 2, 0, GRID_W - NA_KW)
    col_ok = (c[None, :] >= cs[:, None]) & (c[None, :] < cs[:, None] + NA_KW)
    pad = GRID_W - NA_KW
    rpb_pad = jnp.pad(rpb.astype(F32) * LOG2E, ((0, 0), (0, 0), (pad, pad)))
    t1 = jnp.stack([rpb_pad[:, :, GRID_W - 1 - cq:2 * GRID_W - 1 - cq] for cq in range(GRID_W)], axis=2)
    t1 = jnp.where(col_ok, t1, NEG_BIG)
    masked = jnp.full((NA_HEADS, GRID_W, GRID_W), NEG_BIG, F32)
    per_row = []
    for r0 in (0, NA_QROWS, rows - NA_QROWS):
        kb = int(np.clip(r0 - kh // 2, 0, rows - NA_KROWS))
        for qrow in range(r0, r0 + NA_QROWS):
            ws = int(np.clip(qrow - kh // 2, 0, rows - kh))
            per_row.append(jnp.concatenate(
                [t1[:, krow - qrow + NA_KH - 1] if ws <= krow < ws + kh else masked
                 for krow in range(kb, kb + NA_KROWS)], axis=-1))
    return jnp.concatenate(per_row, axis=1)


def _na_out1_kernel(q_ref, k_ref, v_ref, bias_ref, h_ref, gate_ref, of_ref, or_ref, z_ref, gn_ref, w_ref, fn_ref,
                    o_ref, *, n_ctx, rows):
    rb = pl.program_id(1)
    nq = NA_QROWS * GRID_W
    nk = NA_KROWS * GRID_W
    kb = jnp.clip(rb * NA_QROWS - NA_KH // 2, 0, rows - NA_KROWS)
    start = pl.multiple_of(n_ctx + kb * GRID_W, GRID_W)
    lane = lax.broadcasted_iota(jnp.int32, (nq, LANES), 1)
    def scores(j):
        ls = slice((j // 2) * LANES, (j // 2 + 1) * LANES)
        q = q_ref[:, ls]
        qj = jnp.where((lane < NA_DH) == (j % 2 == 0), q, jnp.zeros_like(q))
        return _dot_nt(qj, k_ref[pl.ds(start, nk), ls]) + bias_ref[j], _dot_nt(qj, k_ref[0:n_ctx, ls])

    outs = []
    nxt = scores(0)
    for j in range(NA_HEADS):
        s_loc, s_ctx = nxt
        if j + 1 < NA_HEADS:
            nxt = scores(j + 1)
        ls = slice((j // 2) * LANES, (j // 2 + 1) * LANES)
        m = jnp.maximum(jnp.max(s_loc, axis=-1, keepdims=True), jnp.max(s_ctx, axis=-1, keepdims=True))
        p_loc = jnp.exp2(s_loc - m)
        p_ctx = jnp.exp2(s_ctx - m)
        den = jnp.sum(p_loc, axis=-1, keepdims=True) + jnp.sum(p_ctx, axis=-1, keepdims=True)
        outs.append((_dot(p_loc.astype(BF16), v_ref[pl.ds(start, nk), ls])
                     + _dot(p_ctx.astype(BF16), v_ref[0:n_ctx, ls])) / den)
    na = [jnp.where(lane < NA_DH, outs[2 * i], outs[2 * i + 1]) for i in range(NA_HEADS // 2)]

    f32 = lambda r: r[...].astype(F32)
    g = _head_rms(f32(of_ref) + f32(or_ref), GLA_HEADS) * gn_ref[...]
    cat = (jnp.concatenate([g] + na, axis=1) * f32(z_ref)).astype(BF16)
    h = h_ref[...] + gate_ref[...] * _dot(cat, w_ref[...])
    o_ref[...] = _rms(h) * fn_ref[...]


def _na_out1(nq, nk, nv, bias, h, gate, of, orv, z, gla_norm, w_out, final_norm, n_ctx, rows):
    bsz, ttot, nd = nq.shape
    nrb = rows // NA_QROWS
    qrows = NA_QROWS * GRID_W
    krows = NA_KROWS * GRID_W
    assert qrows == TM
    nct = n_ctx // TM
    nc = GLA_HEADS * GLA_DV

    def variant(rb):
        return jnp.where(rb == 0, 0, jnp.where(rb == nrb - 1, 2, 1))

    lat = lambda w: pl.BlockSpec((None, TM, w), lambda b, rb: (b, rb + nct, 0))
    whole = lambda w: pl.BlockSpec((None, ttot, w), lambda b, rb: (b, 0, 0))
    return pl.pallas_call(
        functools.partial(_na_out1_kernel, n_ctx=n_ctx, rows=rows),
        out_shape=jax.ShapeDtypeStruct((bsz, rows * GRID_W, D_MODEL), F32),
        grid=(bsz, nrb),
        in_specs=[lat(nd), whole(nd), whole(nd),
                  pl.BlockSpec((NA_HEADS, qrows, krows), lambda b, rb: (0, variant(rb), 0)),
                  lat(D_MODEL), pl.BlockSpec((None, None, 1, D_MODEL), lambda b, rb: (b, 1, 0, 0)),
                  lat(nc), lat(nc), lat(nc + nd), _full((1, nc)), _full(w_out.shape), _full((1, D_MODEL))],
        out_specs=pl.BlockSpec((None, TM, D_MODEL), lambda b, rb: (b, rb, 0)),
        compiler_params=_params("parallel", "parallel"),
        name="na_out1",
    )(nq, nk, nv, bias, h, gate, of, orv, z, gla_norm[None], w_out.astype(BF16), final_norm[None])


def kernel(x, c, ctx, c_ctx, l0_norm, l0_w_mod, l0_b_mod, l0_w_in, l0_mla_q_norm, l0_mla_w_uq, l0_mla_kv_norm, l0_mla_w_ukv, l0_mlstm_conv_w, l0_mlstm_conv_b, l0_mlstm_b_i, l0_mlstm_b_f, l0_mlstm_norm, l0_w_out, l1_norm, l1_w_mod, l1_b_mod, l1_w_in, l1_gla_w_gate, l1_gla_b_gate, l1_gla_norm, l1_na_rpb, l1_w_out, final_norm):
    bsz, seq, d = x.shape
    n_ctx = ctx.shape[1]
    rows = seq // GRID_W
    assert d == D_MODEL and seq % TM == 0 and n_ctx == TM
    assert rows % NA_QROWS == 0 and rows >= NA_KROWS and rows // NA_QROWS >= 3
    nct = n_ctx // TM

    shift, scale, gate0 = _mod_vectors(c, c_ctx, l0_w_mod, l0_b_mod)
    cos_t, sin_t = _rope_tables(n_ctx, seq)
    w0 = _proj0_weights(l0_w_in, l0_mla_q_norm, l0_mla_w_uq, l0_mla_kv_norm, l0_mla_w_ukv,
                        l0_mlstm_conv_w, l0_mlstm_conv_b, l0_mlstm_b_i, l0_mlstm_b_f)
    q, k, v, mq, mk, mv, g_rows, g_cols, mo, z0 = _proj0(ctx, x, scale, shift, l0_norm, cos_t, sin_t, w0, nct)
    a = _mla_attention(q, k, v, nct)
    hf, hr = _mlstm(mq, mk, mv, g_rows, g_cols, nct)

    shift, scale, gate = _mod_vectors(c, c_ctx, l1_w_mod, l1_b_mod)
    w1 = _proj1_weights(l1_w_in, l1_gla_w_gate, l1_gla_b_gate)
    h, gq, gk, gv, bc, nq, nk, nv, z = _mid(ctx, x, gate0, a, hf, hr, mo, z0, l0_mlstm_norm, l0_w_out,
                                            scale, shift, l1_norm, w1, nct)
    of, orv = _gla(gq, gk, gv, bc, nct)
    return _na_out1(nq, nk, nv, _na_bias_tables(l1_na_rpb, rows), h, gate, of, orv, z,
                    l1_gla_norm, l1_w_out, final_norm, n_ctx, rows)
```

```python
import functools

import jax
import jax.numpy as jnp
import numpy as np
from jax import lax
from jax.experimental import pallas as pl
from jax.experimental.pallas import tpu as pltpu

F32 = jnp.float32
BF16 = jnp.bfloat16

D_MODEL = 1024
GRID_W = 64
EPS = 1e-6
ROPE_BASE = 10000.0

MLA_HEADS = 8
MLA_Q_RANK = 384
MLA_KV_RANK = 256
MLA_NOPE = 64
MLA_ROPE = 32
MLA_V = 64
ROPE_PAIRS = MLA_ROPE // 4
MLSTM_HEADS = 4
MLSTM_DH = 128
MLSTM_CONV = 3
GLA_HEADS = 4
GLA_DK = 64
GLA_DV = 128
GLA_GATE_RANK = 16
GLA_TAU = 16.0
NA_HEADS = 8
NA_DH = 64
NA_KH = 8
NA_KW = 16

CHUNK = 64
MLSTM_CHUNK = 128
TM = 256
LANES = 128
SUBLANES = 8
HALO = SUBLANES
NA_QROWS = 4
NA_KROWS = NA_QROWS + NA_KH - 1
GLA_LEAF = 8
MLA_HPS = 4
VMEM_LIMIT = 56 * 1024 * 1024
NEG_BIG = -1e30
LOG2E = 1.4426950408889634


def _dot(a, b):
    return jnp.dot(a, b, preferred_element_type=F32)


def _dot_nt(a, b):
    return lax.dot_general(a, b, (((1,), (1,)), ((), ())), preferred_element_type=F32)


def _dot_tn(a, b):
    return lax.dot_general(a, b, (((0,), (0,)), ((), ())), preferred_element_type=F32)


def _rms(x):
    return x * lax.rsqrt(jnp.mean(x * x, axis=-1, keepdims=True) + EPS)


def _sigmoid(x):
    return 1.0 / (1.0 + jnp.exp(-x))


def _silu(x):
    return x * _sigmoid(x)


def _log_sigmoid(x):
    return jnp.minimum(x, 0.0) - jnp.log(1.0 + jnp.exp(-jnp.abs(x)))


def _params(*sem):
    return pltpu.CompilerParams(dimension_semantics=sem, vmem_limit_bytes=VMEM_LIMIT)


def _full(shape):
    nd = len(shape)
    return pl.BlockSpec(shape, lambda *_: (0,) * nd)


def _mod_kernel(c_ref, w0_ref, b0_ref, w1_ref, b1_ref, o0_ref, o1_ref):
    sc = _silu(c_ref[...]).astype(BF16)
    o0_ref[...] = _dot(sc, w0_ref[...].astype(BF16)) + b0_ref[...]
    o1_ref[...] = _dot(sc, w1_ref[...].astype(BF16)) + b1_ref[...]


def _mod_vectors(c, c_ctx, mods):
    bsz, d = c.shape
    rows = -(-(bsz + 1) // SUBLANES) * SUBLANES
    cc = jnp.zeros((rows, d), F32).at[:bsz].set(c).at[bsz].set(c_ctx)
    (w0, b0), (w1, b1) = mods
    n = w0.shape[1]
    wspec = pl.BlockSpec((d, d), lambda j: (0, j))
    vspec = pl.BlockSpec((1, d), lambda j: (0, j))
    ospec = pl.BlockSpec((rows, d), lambda j: (0, j))
    per_layer = pl.pallas_call(
        _mod_kernel,
        out_shape=(jax.ShapeDtypeStruct((rows, n), F32),) * 2,
        grid=(n // d,),
        in_specs=[_full((rows, d)), wspec, vspec, wspec, vspec],
        out_specs=(ospec, ospec),
        compiler_params=_params("parallel"),
        name="modulation",
    )(cc, w0, b0[None], w1, b1[None])
    out = []
    for mod in per_layer:
        vecs = []
        for part in jnp.split(mod, 3, axis=-1):
            ctx_v = jnp.broadcast_to(part[bsz][None], (bsz, d))
            vecs.append(jnp.stack([ctx_v, part[:bsz]], axis=1)[:, :, None, :])
        out.append(vecs)
    return out


def _modnorm(x, g, scale, shift):
    return _rms(x) * g * (1.0 + scale) + shift


def _seg_scan(x, reverse, op, fill):
    width = x.shape[1]
    pos = lax.broadcasted_iota(jnp.int32, x.shape, 1) % MLSTM_CHUNK
    k = 1
    while k < MLSTM_CHUNK:
        if reverse:
            x = op(x, jnp.where(pos < MLSTM_CHUNK - k, pltpu.roll(x, width - k, axis=1), fill))
        else:
            x = op(x, jnp.where(pos >= k, pltpu.roll(x, k, axis=1), fill))
        k *= 2
    return x


def _split3(x):
    hi = x.astype(BF16)
    r1 = x - hi.astype(F32)
    mid = r1.astype(BF16)
    return hi, mid, (r1 - mid.astype(F32)).astype(BF16)


def _proj0_kernel(ctx_ref, lat_ref, hp_ref, hn_ref, sc_ref, sh_ref, g_ref, cos_ref, sin_ref,
                  wcqkr_ref, wckv_ref, wmqk_ref, wmv_ref, wmo_ref, wgt_ref, wz_ref,
                  qn_ref, wuq_ref, kvn_ref, wk_ref, wv_ref, cw_ref, cb_ref, gb_ref,
                  q_out, k_out, v_out, mq_out, mk_out, mv_out, gr_out, gc_out, mo_out, z_out,
                  pbuf, *, nct, nt):
    t = pl.program_id(1)
    h = jnp.where(t < nct, ctx_ref[...], lat_ref[...])
    x = jnp.concatenate([hp_ref[...], h, hn_ref[...]], axis=0)
    ub_ext = _modnorm(x, g_ref[...], sc_ref[...], sh_ref[...]).astype(BF16)
    ub = ub_ext[HALO:HALO + TM]

    gt = _dot_nt(wgt_ref[...], ub) + gb_ref[...]
    half = MLSTM_HEADS
    per_dir = []
    for d in range(2):
        gd = gt[d * SUBLANES:(d + 1) * SUBLANES]
        bcum = _seg_scan(_log_sigmoid(gd), d == 1, jnp.add, 0.0)
        b_top = pltpu.roll(bcum, half, axis=0)
        c8 = gd - b_top
        pm8 = _seg_scan(c8, d == 1, jnp.maximum, -jnp.inf)
        per_dir.append((c8, pm8, b_top))

    cos = cos_ref[...]
    sin = sin_ref[...]
    cos_all = jnp.concatenate([cos] * MLA_HEADS, axis=1)
    sin_all = jnp.concatenate([sin] * MLA_HEADS, axis=1)

    def swap_halves(a):
        lane = lax.broadcasted_iota(jnp.int32, a.shape, 1)
        first = lane % (2 * ROPE_PAIRS) < ROPE_PAIRS
        return jnp.where(first, pltpu.roll(a, a.shape[1] - ROPE_PAIRS, axis=1), pltpu.roll(a, ROPE_PAIRS, axis=1))

    cqkr = _dot(ub, wcqkr_ref[...])
    cq = cqkr[:, :MLA_Q_RANK]
    kr = cqkr[:, MLA_Q_RANK:]
    ckv = _dot(ub, wckv_ref[...])
    pqk = _dot(ub_ext, wmqk_ref[...])

    nq = (_rms(cq) * qn_ref[...]).astype(BF16)
    nkv = (_rms(ckv) * kvn_ref[...]).astype(BF16)

    q = _dot(nq, wuq_ref[...])
    kn = _dot(nkv, wk_ref[...])
    vv = _dot(nkv, wv_ref[...])

    prev_ok = t > nct
    next_ok = jnp.logical_and(t >= nct, t != nt - 1)
    row = lax.broadcasted_iota(jnp.int32, pqk.shape, 0)
    keep = jnp.logical_and(jnp.logical_or(row >= HALO, prev_ok), jnp.logical_or(row < HALO + TM, next_ok))
    pbuf[...] = jnp.where(keep, pqk, 0.0)
    cw = cw_ref[...]
    y = (pbuf[HALO - 1:HALO - 1 + TM, :] * cw[0:1] + pbuf[HALO:HALO + TM, :] * cw[1:2]
         + pbuf[HALO + 1:HALO + 1 + TM, :] * cw[2:3] + cb_ref[...])
    y = _silu(y)
    nb = MLSTM_HEADS * MLSTM_DH
    mq_out[...] = (y[:, :nb] * MLSTM_DH ** -0.5).astype(BF16)
    mk_out[...] = y[:, nb:].astype(BF16)

    mv = _dot(ub, wmv_ref[...])
    mo = _dot(ub, wmo_ref[...])

    q = q * cos_all + swap_halves(q) * sin_all
    q_out[...] = (q * ((MLA_NOPE + MLA_ROPE) ** -0.5 * LOG2E)).astype(BF16)
    rot = kr * cos + swap_halves(kr) * sin
    k_out[...] = (kn + jnp.concatenate([rot] * MLA_HEADS, axis=1)).astype(BF16)
    vlane = lax.broadcasted_iota(jnp.int32, (TM, MLA_HEADS * LANES), 1)
    is_value = ((vlane % LANES) < MLA_V) == ((vlane // LANES) % 2 == 0)
    v_out[...] = jnp.where(is_value, vv, 1.0).astype(BF16)

    zz = _dot(ub, wz_ref[...])
    mv_out[...] = mv.astype(BF16)
    mo_out[...] = _sigmoid(mo).astype(BF16)
    z_out[...] = _silu(zz).astype(BF16)

    r8 =lax.broadcasted_iota(jnp.int32, (SUBLANES, TM), 0)
    both = lambda i: jnp.where(r8 < half, per_dir[0][i], pltpu.roll(per_dir[1][i], half, axis=0)) * LOG2E
    c_rows, pm_rows, b_rows = both(0), both(1), both(2)
    for ci in range(TM // MLSTM_CHUNK):
        gr_out[ci] = c_rows[:, ci * MLSTM_CHUNK:(ci + 1) * MLSTM_CHUNK]
    stack = jnp.concatenate([c_rows, pm_rows, b_rows, jnp.zeros((LANES - 3 * SUBLANES, TM), F32)], axis=0)
    er = lax.broadcasted_iota(jnp.int32, (TM, TM), 0)
    ec = lax.broadcasted_iota(jnp.int32, (TM, TM), 1)
    eye = jnp.where(er == ec, 1.0, 0.0).astype(BF16)
    gc_out[...] = sum(_dot_nt(eye, part) for part in _split3(stack))


def _rope_tables(n_ctx, seq):
    t = jnp.arange(seq)
    inv = 1.0 / (ROPE_BASE ** (jnp.arange(ROPE_PAIRS, dtype=F32) / ROPE_PAIRS))
    ang = jnp.concatenate([(t // GRID_W)[:, None] * inv, (t % GRID_W)[:, None] * inv], axis=-1)
    cos, sin = jnp.cos(ang), jnp.sin(ang)
    j = np.arange(MLA_ROPE)
    src = (j // (2 * ROPE_PAIRS)) * ROPE_PAIRS + (j % ROPE_PAIRS)
    sign = np.where((j % (2 * ROPE_PAIRS)) < ROPE_PAIRS, -1.0, 1.0).astype(np.float32)
    cos_full = cos[:, src]
    sin_full = sin[:, src] * sign
    cos_t = jnp.ones((n_ctx + seq, LANES), F32).at[n_ctx:, MLA_NOPE:MLA_NOPE + MLA_ROPE].set(cos_full)
    sin_t = jnp.zeros((n_ctx + seq, LANES), F32).at[n_ctx:, MLA_NOPE:MLA_NOPE + MLA_ROPE].set(sin_full)
    return cos_t, sin_t


def _proj0_weights(w_in, q_norm, w_uq, kv_norm, w_ukv, conv_w, conv_b, b_i, b_f):
    o = np.cumsum([0, MLA_Q_RANK, MLA_KV_RANK, MLA_ROPE] + [MLSTM_HEADS * MLSTM_DH] * 4 + [4 * MLSTM_HEADS, D_MODEL])
    dq = MLA_NOPE + MLA_ROPE
    wkr = jnp.pad(w_in[:, o[2]:o[3]], ((0, 0), (MLA_NOPE, LANES - dq)))
    wuq = jnp.pad(w_uq.reshape(MLA_Q_RANK, MLA_HEADS, dq), ((0, 0), (0, 0), (0, LANES - dq)))
    wkv = w_ukv.reshape(MLA_KV_RANK, MLA_HEADS, MLA_NOPE + MLA_V)
    wk = jnp.pad(wkv[:, :, :MLA_NOPE], ((0, 0), (0, 0), (0, LANES - MLA_NOPE)))
    zv = jnp.zeros((MLA_KV_RANK, MLA_HEADS // 2, MLA_V), F32)
    wv = jnp.concatenate([wkv[:, 0::2, MLA_NOPE:], zv, zv, wkv[:, 1::2, MLA_NOPE:]], axis=-1)
    wv = wv.reshape(MLA_KV_RANK, MLA_HEADS * LANES)
    gbias = jnp.stack([b_i, b_f], axis=1).reshape(4 * MLSTM_HEADS, 1)
    bf = lambda a: a.astype(BF16)
    return dict(
        wcqkr=bf(jnp.concatenate([w_in[:, o[0]:o[1]], wkr], axis=1)), wckv=bf(w_in[:, o[1]:o[2]]),
        wmqk=bf(w_in[:, o[3]:o[5]]), wmv=bf(w_in[:, o[5]:o[6]]), wmo=bf(w_in[:, o[6]:o[7]]),
        wgt=bf(w_in[:, o[7]:o[8]].T), wz=bf(w_in[:, o[8]:o[9]]),
        qn=q_norm[None], wuq=bf(wuq.reshape(MLA_Q_RANK, -1)),
        kvn=kv_norm[None], wk=bf(wk.reshape(MLA_KV_RANK, -1)), wv=bf(wv),
        cw=jnp.zeros((SUBLANES, conv_w.shape[1]), F32).at[:MLSTM_CONV].set(conv_w), cb=conv_b[None], gb=gbias)


def _tile_specs(nct, nt):
    tile = lambda w: pl.BlockSpec((None, TM, w), lambda b, t: (b, t, 0))
    per = TM // HALO
    nlat = nt - nct
    src_ctx = pl.BlockSpec((None, TM, D_MODEL), lambda b, t: (b, jnp.minimum(t, nct - 1), 0))
    src_lat = pl.BlockSpec((None, TM, D_MODEL), lambda b, t: (b, jnp.maximum(t - nct, 0), 0))
    prev = pl.BlockSpec((None, HALO, D_MODEL), lambda b, t: (b, jnp.maximum((t - nct) * per - 1, 0), 0))
    nxt = pl.BlockSpec((None, HALO, D_MODEL),
                       lambda b, t: (b, jnp.clip((t - nct + 1) * per, 0, nlat * per - 1), 0))
    mod = pl.BlockSpec((None, None, 1, D_MODEL), lambda b, t: (b, (t >= nct).astype(jnp.int32), 0, 0))
    return tile, src_ctx, src_lat, prev, nxt, mod


def _proj0(ctx, x, scale, shift, norm_g, cos_t, sin_t, w, nct):
    bsz = x.shape[0]
    ttot = ctx.shape[1] + x.shape[1]
    nt = ttot // TM
    nb = MLSTM_HEADS * MLSTM_DH
    tile, src_ctx, src_lat, prev, nxt, mod = _tile_specs(nct, nt)
    tab = pl.BlockSpec((TM, LANES), lambda b, t: (t, 0))
    wnames = ["wcqkr", "wckv", "wmqk", "wmv", "wmo", "wgt", "wz",
              "qn", "wuq", "kvn", "wk", "wv", "cw", "cb", "gb"]
    ws = [w[n] for n in wnames]
    tok = lambda width, dt: jax.ShapeDtypeStruct((bsz, ttot, width), dt)
    out_shape = (tok(MLA_HEADS * LANES, BF16), tok(MLA_HEADS * LANES, BF16), tok(MLA_HEADS * LANES, BF16),
                 tok(nb, BF16), tok(nb, BF16), tok(nb, BF16),
                 jax.ShapeDtypeStruct((bsz, ttot // MLSTM_CHUNK, 2 * MLSTM_HEADS, MLSTM_CHUNK), F32), tok(LANES, F32),
                 tok(nb, BF16), tok(D_MODEL, BF16))
    out_specs = (tile(MLA_HEADS * LANES), tile(MLA_HEADS * LANES), tile(MLA_HEADS * LANES),
                 tile(nb), tile(nb), tile(nb),
                 pl.BlockSpec((None, TM // MLSTM_CHUNK, 2 * MLSTM_HEADS, MLSTM_CHUNK), lambda b, t: (b, t, 0, 0)),
                 tile(LANES),
                 tile(nb), tile(D_MODEL))
    return pl.pallas_call(
        functools.partial(_proj0_kernel, nct=nct, nt=nt),
        out_shape=out_shape,
        grid=(bsz, nt),
        in_specs=[src_ctx, src_lat, prev, nxt, mod, mod, _full((1, D_MODEL)), tab, tab] + [_full(a.shape) for a in ws],
        out_specs=out_specs,
        scratch_shapes=[pltpu.VMEM((TM + 2 * HALO, 2 * nb), F32)],
        compiler_params=_params("parallel", "parallel"),
        name="proj0",
    )(ctx, x, x, x, scale, shift, norm_g[None], cos_t, sin_t, *ws)


def _mla_kernel(q_ref, k_ref, v_ref, o_ref, *, nct):
    t = pl.program_id(2)
    lane = lax.broadcasted_iota(jnp.int32, (TM, LANES), 1)

    def attend(nkeys):
        qk = lambda j: _dot_nt(q_ref[:, j * LANES:(j + 1) * LANES], k_ref[0:nkeys, j * LANES:(j + 1) * LANES])
        outs = []
        s_next = qk(0)
        for j in range(MLA_HPS):
            s = s_next
            if j + 1 < MLA_HPS:
                s_next = qk(j + 1)
            p = jnp.exp2(s - jnp.max(s, axis=-1, keepdims=True))
            o = _dot(p.astype(BF16), v_ref[0:nkeys, j * LANES:(j + 1) * LANES])
            outs.append(o / pltpu.roll(o, MLA_V, axis=1))
        for j in range(MLA_HPS // 2):
            o_ref[:, j * LANES:(j + 1) * LANES] = jnp.where(lane < MLA_V, outs[2 * j], outs[2 * j + 1]).astype(BF16)

    @pl.when(t < nct)
    def _():
        attend(nct * TM)

    @pl.when(t >= nct)
    def _():
        attend(k_ref.shape[0])


def _mla_attention(q, k, v, nct):
    bsz, ttot, _ = q.shape
    nt = ttot // TM
    hw = MLA_HPS * LANES
    return pl.pallas_call(
        functools.partial(_mla_kernel, nct=nct),
        out_shape=jax.ShapeDtypeStruct((bsz, ttot, MLA_HEADS * MLA_V), BF16),
        grid=(bsz, MLA_HEADS // MLA_HPS, nt),
        in_specs=[pl.BlockSpec((None, TM, hw), lambda b, p, t: (b, t, p)),
                  pl.BlockSpec((None, ttot, hw), lambda b, p, t: (b, 0, p)),
                  pl.BlockSpec((None, ttot, hw), lambda b, p, t: (b, 0, p))],
        out_specs=pl.BlockSpec((None, TM, hw // 2), lambda b, p, t: (b, t, p)),
        compiler_params=_params("parallel", "parallel", "parallel"),
        name="mla_attention",
    )(q, k, v)


def _scan_block(j, nct, nt):
    return jnp.where(j < nct, nct - 1 - j, nt - 1 - (j - nct))


def _mlstm_kernel(qf_ref, kf_ref, vf_ref, grf_ref, gcf_ref, qr_ref, kr_ref, vr_ref, grr_ref, gcr_ref,
                  hf_ref, hr_ref, c_scr, m_scr):
    @pl.when(pl.program_id(1) == 0)
    def _():
        c_scr[...] = jnp.zeros_like(c_scr)
        m_scr[...] = jnp.zeros_like(m_scr)

    L = MLSTM_CHUNK
    nchunk = TM // L
    nh = MLSTM_HEADS
    nchain = 2 * nh
    row = lax.broadcasted_iota(jnp.int32, (L, L), 0)
    col = lax.broadcasted_iota(jnp.int32, (L, L), 1)
    ones = jnp.ones((L, MLSTM_DH), BF16)
    wide = lambda a: jnp.broadcast_to(a, (L, MLSTM_DH))
    twice = lambda a: jnp.concatenate([a, a], axis=1)
    refs = ((qf_ref, kf_ref, vf_ref, grf_ref, gcf_ref, hf_ref), (qr_ref, kr_ref, vr_ref, grr_ref, gcr_ref, hr_ref))

    def load(ci):
        cs = []
        for d, (q_ref, k_ref, v_ref, gr_ref, gc_ref, o_ref) in enumerate(refs):
            cd = ci if d == 0 else nchunk - 1 - ci
            rs = slice(cd * L, (cd + 1) * L)
            g_rows = gr_ref[cd]
            g_cols = gc_ref[rs, :]
            last = L - 1 if d == 0 else 0
            for hd in range(nh):
                ch = d * nh + hd
                hs = slice(hd * MLSTM_DH, (hd + 1) * MLSTM_DH)
                cs.append(dict(
                    rs=rs, hs=hs, o_ref=o_ref, causal=(col <= row) if d == 0 else (col >= row),
                    c_row=g_rows[ch:ch + 1], c_col=g_cols[:, ch:ch + 1],
                    pm_col=g_cols[:, nchain + ch:nchain + ch + 1], b_col=g_cols[:, 2 * nchain + ch:2 * nchain + ch + 1],
                    pm_last=g_cols[last:last + 1, nchain + ch:nchain + ch + 1],
                    b_last=g_cols[last:last + 1, 2 * nchain + ch:2 * nchain + ch + 1],
                    q=q_ref[rs, hs], k=k_ref[rs, hs],
                    v_ext=jnp.concatenate([v_ref[rs, hs], ones], axis=1)))
        return cs

    def prep_scores(cs):
        for c in cs:
            c["s"] = _dot_nt(c["q"], c["k"])

    def prep_weights(cs):
        for c in cs:
            c["pm_w"] = wide(c["pm_col"])
            c["b_w"] = wide(c["b_col"])
            c["w"] = jnp.exp2(jnp.where(c["causal"], c["c_row"] - c["pm_w"][:, :L], -jnp.inf))
            c["kwf"] = jnp.exp2(wide(c["c_col"] - c["pm_last"]))

    def prep_operands(cs):
        for c in cs:
            c["p"] = (c["s"] * c["w"]).astype(BF16)
            c["kw"] = (c["kwf"] * c["k"].astype(F32)).astype(BF16)

    def prep_products(cs):
        for c in cs:
            c["o1"] = _dot(c["p"], c["v_ext"])
            c["dc"] = _dot_tn(c["kw"], c["v_ext"])

    def scan_read(cs):
        for ch, c in enumerate(cs):
            c["qc"] = _dot(c["q"], state[ch].astype(BF16))

    def scan_factors(cs):
        for ch, c in enumerate(cs):
            g_w = jnp.maximum(m[ch], c["pm_w"])
            c["e_intra"] = jnp.exp2(c["pm_w"] - g_w)
            c["e_inter"] = jnp.exp2(m[ch] - g_w)
            c["floor"] = jnp.exp2(-(c["b_w"] + g_w))
            g_last = jnp.maximum(m[ch], c["pm_last"])
            c["keep"] = jnp.exp2(m[ch] - g_last)
            c["gain"] = jnp.exp2(c["pm_last"] - g_last)
            c["m_new"] = c["b_last"] + g_last

    def scan_update(cs):
        for ch, c in enumerate(cs):
            o = twice(c["e_intra"]) * c["o1"] + twice(c["e_inter"]) * c["qc"]
            h = o[:, :MLSTM_DH] / jnp.maximum(jnp.abs(o[:, MLSTM_DH:]), c["floor"])
            c["o_ref"][c["rs"], c["hs"]] = h.astype(BF16)
            state[ch] = c["keep"] * state[ch] + c["gain"] * c["dc"]
            m[ch] = c["m_new"]

    state = [c_scr[ch] for ch in range(nchain)]
    m = [m_scr[ch, 0:1, 0:1] for ch in range(nchain)]
    chunks = [load(ci) for ci in range(nchunk)]
    everything = [c for cs in chunks for c in cs]
    for stage in (prep_scores, prep_weights, prep_operands):
        stage(everything)
    prep_products(chunks[0])
    for i, cs in enumerate(chunks):
        scan_read(cs)
        if i + 1 < nchunk:
            prep_products(chunks[i + 1])
        scan_factors(cs)
        scan_update(cs)
    for ch in range(nchain):
        c_scr[ch] = state[ch]
        m_scr[ch] = jnp.broadcast_to(m[ch], m_scr.shape[1:])


def _mlstm(mq, mk, mv, g_rows, g_cols, nct):
    bsz, ttot, nb = mq.shape
    nt = ttot // TM
    per = TM // MLSTM_CHUNK
    fwd = lambda w: pl.BlockSpec((None, TM, w), lambda b, j: (b, j, 0))
    rev = lambda w: pl.BlockSpec((None, TM, w), lambda b, j: (b, _scan_block(j, nct, nt), 0))
    gshape = (None, per, 2 * MLSTM_HEADS, MLSTM_CHUNK)
    gfwd = pl.BlockSpec(gshape, lambda b, j: (b, j, 0, 0))
    grev = pl.BlockSpec(gshape, lambda b, j: (b, _scan_block(j, nct, nt), 0, 0))
    return pl.pallas_call(
        _mlstm_kernel,
        out_shape=(jax.ShapeDtypeStruct((bsz, ttot, nb), BF16),) * 2,
        grid=(bsz, nt),
        in_specs=[fwd(nb), fwd(nb), fwd(nb), gfwd, fwd(LANES), rev(nb), rev(nb), rev(nb), grev, rev(LANES)],
        out_specs=(fwd(nb), rev(nb)),
        scratch_shapes=[pltpu.VMEM((2 * MLSTM_HEADS, MLSTM_DH, 2 * MLSTM_DH), F32),
                        pltpu.VMEM((2 * MLSTM_HEADS, SUBLANES, LANES), F32)],
        compiler_params=_params("parallel", "arbitrary"),
        name="mlstm_scan",
    )(mq, mk, mv, g_rows, g_cols, mq, mk, mv, g_rows, g_cols)


def _head_rms(x, n_heads):
    w = x.shape[1] // n_heads
    return jnp.concatenate([_rms(x[:, i * w:(i + 1) * w]) for i in range(n_heads)], axis=1)


def _mid_kernel(ctx_ref, lat_ref, gate_ref, a_ref, hf_ref, hr_ref, mo_ref, z0_ref, hn_ref, wout_ref,
                sc_ref, sh_ref, g_ref, wq_ref, wk_ref, wv_ref, wga_ref, wnq_ref, wnk_ref, wnv_ref, wz_ref,
                wgate_ref, bgate_ref,
                h_out, gq_out, gk_out, gv_out, bc_out, nq_out, nk_out, nv_out, z_out, *, nct):
    f32 = lambda r: r[...].astype(F32)
    hm = _head_rms(f32(mo_ref) * (f32(hf_ref) + f32(hr_ref)), MLSTM_HEADS) * hn_ref[...]
    cat = (jnp.concatenate([f32(a_ref), hm], axis=1) * f32(z0_ref)).astype(BF16)
    h0 = jnp.where(pl.program_id(1) < nct, ctx_ref[...], lat_ref[...])
    h = h0 + gate_ref[...] * _dot(cat, wout_ref[...])
    h_out[...] = h

    ub = _modnorm(h, g_ref[...], sc_ref[...], sh_ref[...]).astype(BF16)
    ga = _dot(ub, wga_ref[...]).astype(BF16)
    gate_pre = _dot(ga, wgate_ref[...])
    gq_out[...] = (_dot(ub, wq_ref[...]) * GLA_DK ** -0.5).astype(BF16)
    gk_out[...] = _dot(ub, wk_ref[...]).astype(BF16)
    gv_out[...] = _dot(ub, wv_ref[...]).astype(BF16)

    lg = _log_sigmoid(gate_pre + bgate_ref[...]) * (1.0 / GLA_TAU)
    hi, mid, lo = _split3(lg)
    nq_out[...] = (_dot(ub, wnq_ref[...]) * (NA_DH ** -0.5 * LOG2E)).astype(BF16)
    nk_out[...] = _dot(ub, wnk_ref[...]).astype(BF16)
    nv_out[...] = _dot(ub, wnv_ref[...]).astype(BF16)
    row = lax.broadcasted_iota(jnp.int32, (TM, TM), 0)
    col = lax.broadcasted_iota(jnp.int32, (TM, TM), 1)
    same = (row // CHUNK) == (col // CHUNK)
    nk = GLA_HEADS * GLA_DK
    for d in range(2):
        tri = jnp.where(jnp.logical_and(same, (col <= row) if d == 0 else (col >= row)), 1.0, 0.0).astype(BF16)
        cols = slice(d * nk, (d + 1) * nk)
        bc_out[:, cols] = (_dot(tri, hi[:, cols]) + _dot(tri, mid[:, cols]) + _dot(tri, lo[:, cols])) * LOG2E
    z_out[...] = _silu(_dot(ub, wz_ref[...])).astype(BF16)


def _proj1_weights(w_in, w_gate, b_gate):
    nk = GLA_HEADS * GLA_DK
    nc = GLA_HEADS * GLA_DV
    nd = NA_HEADS * NA_DH
    o = np.cumsum([0, nk, nk, nc, 2 * GLA_GATE_RANK, nd, nd, nd, nc + nd])
    r = GLA_GATE_RANK
    wgate = jnp.zeros((2 * r, 2 * nk), F32).at[:r, :nk].set(w_gate[0]).at[r:, nk:].set(w_gate[1])
    bf = lambda a: a.astype(BF16)
    names = ["wq", "wk", "wv", "wga", "wnq", "wnk", "wnv", "wz"]
    w = {n: bf(w_in[:, o[i]:o[i + 1]]) for i, n in enumerate(names)}
    w["wgate"] = bf(wgate)
    w["bgate"] = b_gate.reshape(1, 2 * nk)
    return w


def _mid(ctx, x, gate0, a, hf, hr, mo, z0, h_norm, w_out, scale, shift, norm_g, w, nct):
    bsz = x.shape[0]
    ttot = ctx.shape[1] + x.shape[1]
    nt = ttot // TM
    tile, src_ctx, src_lat, _, _, mod = _tile_specs(nct, nt)
    nb = MLSTM_HEADS * MLSTM_DH
    nk = GLA_HEADS * GLA_DK
    nc = GLA_HEADS * GLA_DV
    nd = NA_HEADS * NA_DH
    wnames = ["wq", "wk", "wv", "wga", "wnq", "wnk", "wnv", "wz", "wgate", "bgate"]
    ws = [w[n] for n in wnames]
    tok = lambda width, dt: jax.ShapeDtypeStruct((bsz, ttot, width), dt)
    return pl.pallas_call(
        functools.partial(_mid_kernel, nct=nct),
        out_shape=(tok(D_MODEL, F32), tok(nk, BF16), tok(nk, BF16), tok(nc, BF16), tok(2 * nk, F32),
                   tok(nd, BF16), tok(nd, BF16), tok(nd, BF16), tok(nc + nd, BF16)),
        grid=(bsz, nt),
        in_specs=[src_ctx, src_lat, mod, tile(MLA_HEADS * MLA_V), tile(nb), tile(nb), tile(nb), tile(D_MODEL),
                  _full((1, nb)), _full(w_out.shape), mod, mod, _full((1, D_MODEL))] + [_full(a.shape) for a in ws],
        out_specs=(tile(D_MODEL), tile(nk), tile(nk), tile(nc), tile(2 * nk), tile(nd), tile(nd), tile(nd),
                   tile(nc + nd)),
        compiler_params=_params("parallel", "parallel"),
        name="out0_proj1",
    )(ctx, x, gate0, a, hf, hr, mo, z0, h_norm[None], w_out.astype(BF16), scale, shift, norm_g[None], *ws)


def _gla_kernel(qf_ref, kf_ref, vf_ref, bf_ref, qr_ref, kr_ref, vr_ref, br_ref, of_ref, or_ref, s_scr):
    @pl.when(pl.program_id(1) == 0)
    def _():
        s_scr[...] = jnp.zeros_like(s_scr)

    L = CHUNK
    nchunk = TM // L
    npair = GLA_HEADS // 2
    lane = lax.broadcasted_iota(jnp.int32, (L, LANES), 1)
    rowv = lax.broadcasted_iota(jnp.int32, (L, LANES), 0)
    srow = lax.broadcasted_iota(jnp.int32, (LANES, 2 * GLA_DV), 0)
    scol = lax.broadcasted_iota(jnp.int32, (LANES, 2 * GLA_DV), 1)
    own_block = (srow < GLA_DK) == (scol < GLA_DV)
    vcol = lax.broadcasted_iota(jnp.int32, (L, 2 * GLA_DV), 1)
    e_row = lax.broadcasted_iota(jnp.int32, (LANES, LANES), 0)
    e_col = lax.broadcasted_iota(jnp.int32, (LANES, LANES), 1)
    head_ones = jnp.where((e_row < GLA_DK) == (e_col < GLA_DK), 1.0, 0.0).astype(BF16)
    leaf_idx = lax.broadcasted_iota(jnp.int32, (GLA_LEAF * GLA_LEAF, LANES), 0)
    leaf_lane = lax.broadcasted_iota(jnp.int32, (GLA_LEAF * GLA_LEAF, LANES), 1) % L
    leaf_s, leaf_t = leaf_idx // GLA_LEAF, leaf_idx % GLA_LEAF
    rep = lambda a: jnp.concatenate(
        [jnp.broadcast_to(a[s:s + 1], (GLA_LEAF, a.shape[1])) for s in range(GLA_LEAF)], axis=0)
    til = lambda a: jnp.concatenate([a] * GLA_LEAF, axis=0)

    def chunk_step(ci, carry):
        cs = []
        for d, (q_ref, k_ref, v_ref, b_ref, o_ref) in enumerate(
                ((qf_ref, kf_ref, vf_ref, bf_ref, of_ref), (qr_ref, kr_ref, vr_ref, br_ref, or_ref))):
            cd = ci if d == 0 else nchunk - 1 - ci
            base = pl.multiple_of(cd * L, L)
            for p in range(npair):
                ls = slice(p * LANES, (p + 1) * LANES)
                bc = b_ref[pl.ds(base, L), ls]
                cs.append(dict(
                    d=d, p=p, base=base, o_ref=o_ref, bc=bc,
                    tau_v=rowv if d == 0 else L - 1 - rowv,
                    tau_s=(lane % L) if d == 0 else L - 1 - (lane % L),
                    q=q_ref[pl.ds(base, L), ls].astype(F32),
                    k=k_ref[pl.ds(base, L), ls].astype(F32),
                    v=v_ref[pl.ds(base, L), p * 2 * GLA_DV:(p + 1) * 2 * GLA_DV],
                    b_last=bc[L - 1:L] if d == 0 else bc[0:1],
                    state=s_scr[d, p]))

        for c in cs:
            c["o_pair"] = _dot((c["q"] * jnp.exp2(c["bc"])).astype(BF16), c["state"].astype(BF16))

        for c in cs:
            c["att"] = jnp.zeros((L, LANES), F32)
        bs = L // 2
        while bs >= GLA_LEAF:
            for c in cs:
                d, bc, tau_v = c["d"], c["bc"], c["tau_v"]
                later = (tau_v // bs) % 2 == 1
                ref_b = None
                for blk in range(L // (2 * bs)):
                    tau_ref = blk * 2 * bs + bs - 1
                    idx = tau_ref if d == 0 else L - 1 - tau_ref
                    r = jnp.broadcast_to(bc[idx:idx + 1], (L, LANES))
                    ref_b = r if ref_b is None else jnp.where(tau_v // (2 * bs) == blk, r, ref_b)
                c["qs"] = jnp.where(later, c["q"] * jnp.exp2(jnp.where(later, bc - ref_b, 0.0)), 0.0).astype(BF16)
                ks = jnp.where(later, 0.0, c["k"] * jnp.exp2(jnp.where(later, 0.0, ref_b - bc)))
                c["ks2"] = jnp.concatenate([jnp.where(lane < GLA_DK, ks, 0.0), jnp.where(lane < GLA_DK, 0.0, ks)],
                                           axis=0).astype(BF16)
            for c in cs:
                same_parent = (c["tau_v"] // (2 * bs)) == (c["tau_s"] // (2 * bs))
                c["att"] = c["att"] + jnp.where(same_parent, _dot_nt(c["qs"], c["ks2"]), 0.0)
            bs //= 2

        for c in cs:
            c["leaf"] = []
        for blk in range(L // GLA_LEAF):
            rs = slice(blk * GLA_LEAF, (blk + 1) * GLA_LEAF)
            for c in cs:
                qb, bb, kb = c["q"][rs], c["bc"][rs], c["k"][rs]
                ok = (leaf_t >= leaf_s) if c["d"] == 0 else (leaf_t <= leaf_s)
                w = jnp.where(ok, til(qb) * jnp.exp2(til(bb) - rep(bb)) * rep(kb), 0.0)
                c["a"] = _dot(w.astype(BF16), head_ones)
            for c in cs:
                a = jnp.where(leaf_lane == leaf_s + blk * GLA_LEAF, c["a"], 0.0)
                acc = a[0:GLA_LEAF]
                for s in range(1, GLA_LEAF):
                    acc = acc + a[s * GLA_LEAF:(s + 1) * GLA_LEAF]
                c["leaf"].append(acc)

        for c in cs:
            v = c["v"]
            c["att"] = (c["att"] + jnp.concatenate(c["leaf"], axis=0)).astype(BF16)
            c["v_blk"] = jnp.concatenate([jnp.where(vcol < GLA_DV, v, jnp.zeros_like(v)),
                                          jnp.where(vcol < GLA_DV, jnp.zeros_like(v), v)], axis=0)
            c["ke"] = (c["k"] * jnp.exp2(c["b_last"] - c["bc"])).astype(BF16)
            c["decay_col"] = jnp.sum(
                jnp.where(e_row == e_col, jnp.broadcast_to(jnp.exp2(c["b_last"]), (LANES, LANES)), 0.0),
                axis=1, keepdims=True)
        for c in cs:
            c["o"] = c["o_pair"] + _dot(c["att"], c["v_blk"])
            c["ds"] = _dot_tn(c["ke"], c["v"])
        for c in cs:
            p = c["p"]
            c["o_ref"][pl.ds(c["base"], L), p * 2 * GLA_DV:(p + 1) * 2 * GLA_DV] = c["o"].astype(BF16)
            s_scr[c["d"], p] = c["decay_col"] * c["state"] + jnp.where(own_block, c["ds"], 0.0)
        return carry

    lax.fori_loop(0, nchunk, chunk_step, 0)


def _gla(gq, gk, gv, bc, nct):
    bsz, ttot, nk = gq.shape
    nc = gv.shape[2]
    nt = ttot // TM
    fwd = lambda w: pl.BlockSpec((None, TM, w), lambda b, j: (b, j, 0))
    rev = lambda w: pl.BlockSpec((None, TM, w), lambda b, j: (b, _scan_block(j, nct, nt), 0))
    bfwd = pl.BlockSpec((None, TM, nk), lambda b, j: (b, j, 0))
    brev = pl.BlockSpec((None, TM, nk), lambda b, j: (b, _scan_block(j, nct, nt), 1))
    return pl.pallas_call(
        _gla_kernel,
        out_shape=(jax.ShapeDtypeStruct((bsz, ttot, nc), BF16),) * 2,
        grid=(bsz, nt),
        in_specs=[fwd(nk), fwd(nk), fwd(nc), bfwd, rev(nk), rev(nk), rev(nc), brev],
        out_specs=(fwd(nc), rev(nc)),
        scratch_shapes=[pltpu.VMEM((2, GLA_HEADS // 2, LANES, 2 * GLA_DV), F32)],
        compiler_params=_params("parallel", "arbitrary"),
        name="gla_scan",
    )(gq, gk, gv, bc, gq, gk, gv, bc)


def _na_bias_tables(rpb, rows):
    kh = NA_KH
    c = np.arange(GRID_W)
    cs = np.clip(c - NA_KW // 2, 0, GRID_W - NA_KW)
    col_ok = (c[None, :] >= cs[:, None]) & (c[None, :] < cs[:, None] + NA_KW)
    pad = GRID_W - NA_KW
    rpb_pad = jnp.pad(rpb.astype(F32) * LOG2E, ((0, 0), (0, 0), (pad, pad)))
    t1 = jnp.stack([rpb_pad[:, :, GRID_W - 1 - cq:2 * GRID_W - 1 - cq] for cq in range(GRID_W)], axis=2)
    t1 = jnp.where(col_ok, t1, NEG_BIG)
    masked = jnp.full((NA_HEADS, GRID_W, GRID_W), NEG_BIG, F32)
    per_row = []
    for r0 in (0, NA_QROWS, rows - NA_QROWS):
        kb = int(np.clip(r0 - kh // 2, 0, rows - NA_KROWS))
        for qrow in range(r0, r0 + NA_QROWS):
            ws = int(np.clip(qrow - kh // 2, 0, rows - kh))
            per_row.append(jnp.concatenate(
                [t1[:, krow - qrow + NA_KH - 1] if ws <= krow < ws + kh else masked
                 for krow in range(kb, kb + NA_KROWS)], axis=-1))
    return jnp.concatenate(per_row, axis=1)


def _na_out1_kernel(q_ref, k_ref, v_ref, bias_ref, h_ref, gate_ref, of_ref, or_ref, z_ref, gn_ref, w_ref, fn_ref,
                    o_ref, *, n_ctx, rows):
    rb = pl.program_id(1)
    nq = NA_QROWS * GRID_W
    nk = NA_KROWS * GRID_W
    kb = jnp.clip(rb * NA_QROWS - NA_KH // 2, 0, rows - NA_KROWS)
    start = pl.multiple_of(n_ctx + kb * GRID_W, GRID_W)
    lane = lax.broadcasted_iota(jnp.int32, (nq, LANES), 1)
    def scores(j):
        ls = slice((j // 2) * LANES, (j // 2 + 1) * LANES)
        q = q_ref[:, ls]
        qj = jnp.where((lane < NA_DH) == (j % 2 == 0), q, jnp.zeros_like(q))
        return _dot_nt(qj, k_ref[pl.ds(start, nk), ls]) + bias_ref[j], _dot_nt(qj, k_ref[0:n_ctx, ls])

    outs = []
    nxt = scores(0)
    for j in range(NA_HEADS):
        s_loc, s_ctx = nxt
        if j + 1 < NA_HEADS:
            nxt = scores(j + 1)
        ls = slice((j // 2) * LANES, (j // 2 + 1) * LANES)
        m = jnp.maximum(jnp.max(s_loc, axis=-1, keepdims=True), jnp.max(s_ctx, axis=-1, keepdims=True))
        p_loc = jnp.exp2(s_loc - m)
        p_ctx = jnp.exp2(s_ctx - m)
        den = jnp.sum(p_loc, axis=-1, keepdims=True) + jnp.sum(p_ctx, axis=-1, keepdims=True)
        outs.append((_dot(p_loc.astype(BF16), v_ref[pl.ds(start, nk), ls])
                     + _dot(p_ctx.astype(BF16), v_ref[0:n_ctx, ls])) / den)
    na = [jnp.where(lane < NA_DH, outs[2 * i], outs[2 * i + 1]) for i in range(NA_HEADS // 2)]

    f32 = lambda r: r[...].astype(F32)
    g = _head_rms(f32(of_ref) + f32(or_ref), GLA_HEADS) * gn_ref[...]
    cat = (jnp.concatenate([g] + na, axis=1) * f32(z_ref)).astype(BF16)
    h = h_ref[...] + gate_ref[...] * _dot(cat, w_ref[...])
    o_ref[...] = _rms(h) * fn_ref[...]


def _na_out1(nq, nk, nv, bias, h, gate, of, orv, z, gla_norm, w_out, final_norm, n_ctx, rows):
    bsz, ttot, nd = nq.shape
    nrb = rows // NA_QROWS
    qrows = NA_QROWS * GRID_W
    krows = NA_KROWS * GRID_W
    assert qrows == TM
    nct = n_ctx // TM
    nc = GLA_HEADS * GLA_DV

    def variant(rb):
        return jnp.where(rb == 0, 0, jnp.where(rb == nrb - 1, 2, 1))

    lat = lambda w: pl.BlockSpec((None, TM, w), lambda b, rb: (b, rb + nct, 0))
    whole = lambda w: pl.BlockSpec((None, ttot, w), lambda b, rb: (b, 0, 0))
    return pl.pallas_call(
        functools.partial(_na_out1_kernel, n_ctx=n_ctx, rows=rows),
        out_shape=jax.ShapeDtypeStruct((bsz, rows * GRID_W, D_MODEL), F32),
        grid=(bsz, nrb),
        in_specs=[lat(nd), whole(nd), whole(nd),
                  pl.BlockSpec((NA_HEADS, qrows, krows), lambda b, rb: (0, variant(rb), 0)),
                  lat(D_MODEL), pl.BlockSpec((None, None, 1, D_MODEL), lambda b, rb: (b, 1, 0, 0)),
                  lat(nc), lat(nc), lat(nc + nd), _full((1, nc)), _full(w_out.shape), _full((1, D_MODEL))],
        out_specs=pl.BlockSpec((None, TM, D_MODEL), lambda b, rb: (b, rb, 0)),
        compiler_params=_params("parallel", "parallel"),
        name="na_out1",
    )(nq, nk, nv, bias, h, gate, of, orv, z, gla_norm[None], w_out.astype(BF16), final_norm[None])


def kernel(x, c, ctx, c_ctx, l0_norm, l0_w_mod, l0_b_mod, l0_w_in, l0_mla_q_norm, l0_mla_w_uq, l0_mla_kv_norm, l0_mla_w_ukv, l0_mlstm_conv_w, l0_mlstm_conv_b, l0_mlstm_b_i, l0_mlstm_b_f, l0_mlstm_norm, l0_w_out, l1_norm, l1_w_mod, l1_b_mod, l1_w_in, l1_gla_w_gate, l1_gla_b_gate, l1_gla_norm, l1_na_rpb, l1_w_out, final_norm):
    bsz, seq, d = x.shape
    n_ctx = ctx.shape[1]
    rows = seq // GRID_W
    assert d == D_MODEL and seq % TM == 0 and n_ctx == TM
    assert rows % NA_QROWS == 0 and rows >= NA_KROWS and rows // NA_QROWS >= 3
    nct = n_ctx // TM

    (shift0, scale0, gate0), (shift, scale, gate) = _mod_vectors(
        c, c_ctx, ((l0_w_mod, l0_b_mod), (l1_w_mod, l1_b_mod)))

    cos_t, sin_t = _rope_tables(n_ctx, seq)
    w0 = _proj0_weights(l0_w_in, l0_mla_q_norm, l0_mla_w_uq, l0_mla_kv_norm, l0_mla_w_ukv,
                        l0_mlstm_conv_w, l0_mlstm_conv_b, l0_mlstm_b_i, l0_mlstm_b_f)
    q, k, v, mq, mk, mv, g_rows, g_cols, mo, z0 = _proj0(ctx, x, scale0, shift0, l0_norm, cos_t, sin_t, w0, nct)
    a = _mla_attention(q, k, v, nct)
    hf, hr = _mlstm(mq, mk, mv, g_rows, g_cols, nct)

    w1 = _proj1_weights(l1_w_in, l1_gla_w_gate, l1_gla_b_gate)
    h, gq, gk, gv, bc, nq, nk, nv, z = _mid(ctx, x, gate0, a, hf, hr, mo, z0, l0_mlstm_norm, l0_w_out,
                                            scale, shift, l1_norm, w1, nct)
    of, orv = _gla(gq, gk, gv, bc, nct)
    return _na_out1(nq, nk, nv, _na_bias_tables(l1_na_rpb, rows), h, gate, of, orv, z,
                    l1_gla_norm, l1_w_out, final_norm, n_ctx, rows)
```

```python
import functools
import math

import jax
import jax.numpy as jnp
import numpy as np
from jax import lax
from jax.experimental import pallas as pl
from jax.experimental.pallas import tpu as pltpu

F32 = jnp.float32
BF16 = jnp.bfloat16

D_MODEL = 1024
GRID_W = 64
EPS = 1e-6
ROPE_BASE = 10000.0

MLA_HEADS = 8
MLA_Q_RANK = 384
MLA_KV_RANK = 256
MLA_NOPE = 64
MLA_ROPE = 32
MLA_V = 64
ROPE_PAIRS = MLA_ROPE // 4
MLSTM_HEADS = 4
MLSTM_DH = 128
MLSTM_CONV = 3
GLA_HEADS = 4
GLA_DK = 64
GLA_DV = 128
GLA_GATE_RANK = 16
GLA_TAU = 16.0
NA_HEADS = 8
NA_DH = 64
NA_KH = 8
NA_KW = 16

CHUNK = 64
MLSTM_CHUNK = 128
TM = 256
LANES = 128
SUBLANES = 8
HALO = SUBLANES
NA_QROWS = 4
NA_KROWS = NA_QROWS + NA_KH - 1
GLA_LEAF = 8
MLA_HPS = 4
MLSTM_BATCH = 2
GLA_BATCH = 4
VMEM_LIMIT = 56 * 1024 * 1024
NEG_BIG = -1e30
LOG2E = 1.4426950408889634


def _dot(a, b):
    return jnp.dot(a, b, preferred_element_type=F32)


def _dot_nt(a, b):
    return lax.dot_general(a, b, (((1,), (1,)), ((), ())), preferred_element_type=F32)


def _dot_tn(a, b):
    return lax.dot_general(a, b, (((0,), (0,)), ((), ())), preferred_element_type=F32)


def _rms(x):
    return x * lax.rsqrt(jnp.mean(x * x, axis=-1, keepdims=True) + EPS)


def _sigmoid(x):
    return 1.0 / (1.0 + jnp.exp(-x))


def _silu(x):
    return x * _sigmoid(x)


def _log_sigmoid(x):
    return jnp.minimum(x, 0.0) - jnp.log(1.0 + jnp.exp(-jnp.abs(x)))


def _params(*sem):
    return pltpu.CompilerParams(dimension_semantics=sem, vmem_limit_bytes=VMEM_LIMIT)


def _full(shape):
    nd = len(shape)
    return pl.BlockSpec(shape, lambda *_: (0,) * nd)


def _mod_kernel(c_ref, w0_ref, b0_ref, w1_ref, b1_ref, o0_ref, o1_ref):
    sc = _silu(c_ref[...]).astype(BF16)
    o0_ref[...] = _dot(sc, w0_ref[...].astype(BF16)) + b0_ref[...]
    o1_ref[...] = _dot(sc, w1_ref[...].astype(BF16)) + b1_ref[...]


def _mod_vectors(c, c_ctx, mods):
    bsz, d = c.shape
    rows = -(-(bsz + 1) // SUBLANES) * SUBLANES
    cc = jnp.zeros((rows, d), F32).at[:bsz].set(c).at[bsz].set(c_ctx)
    (w0, b0), (w1, b1) = mods
    n = w0.shape[1]
    wspec = pl.BlockSpec((d, d), lambda j: (0, j))
    vspec = pl.BlockSpec((1, d), lambda j: (0, j))
    ospec = pl.BlockSpec((rows, d), lambda j: (0, j))
    per_layer = pl.pallas_call(
        _mod_kernel,
        out_shape=(jax.ShapeDtypeStruct((rows, n), F32),) * 2,
        grid=(n // d,),
        in_specs=[_full((rows, d)), wspec, vspec, wspec, vspec],
        out_specs=(ospec, ospec),
        compiler_params=_params("parallel"),
        name="modulation",
    )(cc, w0, b0[None], w1, b1[None])
    out = []
    for mod in per_layer:
        vecs = []
        for part in jnp.split(mod, 3, axis=-1):
            ctx_v = jnp.broadcast_to(part[bsz][None], (bsz, d))
            vecs.append(jnp.stack([ctx_v, part[:bsz]], axis=1)[:, :, None, :])
        out.append(vecs)
    return out


def _modnorm(x, g, scale, shift):
    return _rms(x) * g * (1.0 + scale) + shift


def _seg_scan(x, reverse, op, fill):
    width = x.shape[1]
    pos = lax.broadcasted_iota(jnp.int32, x.shape, 1) % MLSTM_CHUNK
    k = 1
    while k < MLSTM_CHUNK:
        if reverse:
            x = op(x, jnp.where(pos < MLSTM_CHUNK - k, pltpu.roll(x, width - k, axis=1), fill))
        else:
            x = op(x, jnp.where(pos >= k, pltpu.roll(x, k, axis=1), fill))
        k *= 2
    return x


def _split3(x):
    hi = x.astype(BF16)
    r1 = x - hi.astype(F32)
    mid = r1.astype(BF16)
    return hi, mid, (r1 - mid.astype(F32)).astype(BF16)


def _proj0_kernel(ctx_ref, lat_ref, hp_ref, hn_ref, sc_ref, sh_ref, g_ref, cos_ref, sin_ref,
                  wcqkr_ref, wckv_ref, wmqk_ref, wmv_ref, wmo_ref, wgt_ref, wz_ref,
                  qn_ref, wuq_ref, kvn_ref, wk_ref, wv_ref, cw_ref, cb_ref, gb_ref,
                  q_out, k_out, v_out, mq_out, mk_out, mv_out, gr_out, gc_out, mo_out, z_out,
                  pbuf, *, nct, nt):
    t = pl.program_id(1)
    h = jnp.where(t < nct, ctx_ref[...], lat_ref[...])
    x = jnp.concatenate([hp_ref[...], h, hn_ref[...]], axis=0)
    ub_ext = _modnorm(x, g_ref[...], sc_ref[...], sh_ref[...]).astype(BF16)
    ub = ub_ext[HALO:HALO + TM]

    gt = _dot_nt(wgt_ref[...], ub) + gb_ref[...]
    half = MLSTM_HEADS
    per_dir = []
    for d in range(2):
        gd = gt[d * SUBLANES:(d + 1) * SUBLANES]
        bcum = _seg_scan(_log_sigmoid(gd), d == 1, jnp.add, 0.0)
        b_top = pltpu.roll(bcum, half, axis=0)
        c8 = gd - b_top
        pm8 = _seg_scan(c8, d == 1, jnp.maximum, -jnp.inf)
        per_dir.append((c8, pm8, b_top))

    cos = cos_ref[...]
    sin = sin_ref[...]
    cos_all = jnp.concatenate([cos] * MLA_HEADS, axis=1)
    sin_all = jnp.concatenate([sin] * MLA_HEADS, axis=1)

    def swap_halves(a):
        lane = lax.broadcasted_iota(jnp.int32, a.shape, 1)
        first = lane % (2 * ROPE_PAIRS) < ROPE_PAIRS
        return jnp.where(first, pltpu.roll(a, a.shape[1] - ROPE_PAIRS, axis=1), pltpu.roll(a, ROPE_PAIRS, axis=1))

    cqkr = _dot(ub, wcqkr_ref[...])
    cq = cqkr[:, :MLA_Q_RANK]
    kr = cqkr[:, MLA_Q_RANK:]
    ckv = _dot(ub, wckv_ref[...])
    pqk = _dot(ub_ext, wmqk_ref[...])

    nq = (_rms(cq) * qn_ref[...]).astype(BF16)
    nkv = (_rms(ckv) * kvn_ref[...]).astype(BF16)

    q = _dot(nq, wuq_ref[...])
    kn = _dot(nkv, wk_ref[...])
    vv = _dot(nkv, wv_ref[...])

    prev_ok = t > nct
    next_ok = jnp.logical_and(t >= nct, t != nt - 1)
    row = lax.broadcasted_iota(jnp.int32, pqk.shape, 0)
    keep = jnp.logical_and(jnp.logical_or(row >= HALO, prev_ok), jnp.logical_or(row < HALO + TM, next_ok))
    pbuf[...] = jnp.where(keep, pqk, 0.0)
    cw = cw_ref[...]
    y = (pbuf[HALO - 1:HALO - 1 + TM, :] * cw[0:1] + pbuf[HALO:HALO + TM, :] * cw[1:2]
         + pbuf[HALO + 1:HALO + 1 + TM, :] * cw[2:3] + cb_ref[...])
    y = _silu(y)
    nb = MLSTM_HEADS * MLSTM_DH
    mq_out[...] = (y[:, :nb] * MLSTM_DH ** -0.5).astype(BF16)
    mk_out[...] = y[:, nb:].astype(BF16)

    mv = _dot(ub, wmv_ref[...])
    mo = _dot(ub, wmo_ref[...])

    q = q * cos_all + swap_halves(q) * sin_all
    q_out[...] = (q * ((MLA_NOPE + MLA_ROPE) ** -0.5 * LOG2E)).astype(BF16)
    rot = kr * cos + swap_halves(kr) * sin
    k_out[...] = (kn + jnp.concatenate([rot] * MLA_HEADS, axis=1)).astype(BF16)
    vlane = lax.broadcasted_iota(jnp.int32, (TM, MLA_HEADS * LANES), 1)
    is_value = ((vlane % LANES) < MLA_V) == ((vlane // LANES) % 2 == 0)
    v_out[...] = jnp.where(is_value, vv, 1.0).astype(BF16)

    zz = _dot(ub, wz_ref[...])
    mv_out[...] = mv.astype(BF16)
    mo_out[...] = _sigmoid(mo).astype(BF16)
    z_out[...] = _silu(zz).astype(BF16)

    r8 =lax.broadcasted_iota(jnp.int32, (SUBLANES, TM), 0)
    both = lambda i: jnp.where(r8 < half, per_dir[0][i], pltpu.roll(per_dir[1][i], half, axis=0)) * LOG2E
    c_rows, pm_rows, b_rows = both(0), both(1), both(2)
    for ci in range(TM // MLSTM_CHUNK):
        gr_out[ci] = c_rows[:, ci * MLSTM_CHUNK:(ci + 1) * MLSTM_CHUNK]
    stack = jnp.concatenate([c_rows, pm_rows, b_rows, jnp.zeros((LANES - 3 * SUBLANES, TM), F32)], axis=0)
    er = lax.broadcasted_iota(jnp.int32, (TM, TM), 0)
    ec = lax.broadcasted_iota(jnp.int32, (TM, TM), 1)
    eye = jnp.where(er == ec, 1.0, 0.0).astype(BF16)
    gc_out[...] = sum(_dot_nt(eye, part) for part in _split3(stack))


def _rope_tables(n_ctx, seq):
    t = jnp.arange(seq)
    inv = 1.0 / (ROPE_BASE ** (jnp.arange(ROPE_PAIRS, dtype=F32) / ROPE_PAIRS))
    ang = jnp.concatenate([(t // GRID_W)[:, None] * inv, (t % GRID_W)[:, None] * inv], axis=-1)
    cos, sin = jnp.cos(ang), jnp.sin(ang)
    j = np.arange(MLA_ROPE)
    src = (j // (2 * ROPE_PAIRS)) * ROPE_PAIRS + (j % ROPE_PAIRS)
    sign = np.where((j % (2 * ROPE_PAIRS)) < ROPE_PAIRS, -1.0, 1.0).astype(np.float32)
    cos_full = cos[:, src]
    sin_full = sin[:, src] * sign
    cos_t = jnp.ones((n_ctx + seq, LANES), F32).at[n_ctx:, MLA_NOPE:MLA_NOPE + MLA_ROPE].set(cos_full)
    sin_t = jnp.zeros((n_ctx + seq, LANES), F32).at[n_ctx:, MLA_NOPE:MLA_NOPE + MLA_ROPE].set(sin_full)
    return cos_t, sin_t


def _proj0_weights(w_in, q_norm, w_uq, kv_norm, w_ukv, conv_w, conv_b, b_i, b_f):
    o = np.cumsum([0, MLA_Q_RANK, MLA_KV_RANK, MLA_ROPE] + [MLSTM_HEADS * MLSTM_DH] * 4 + [4 * MLSTM_HEADS, D_MODEL])
    dq = MLA_NOPE + MLA_ROPE
    wkr = jnp.pad(w_in[:, o[2]:o[3]], ((0, 0), (MLA_NOPE, LANES - dq)))
    wuq = jnp.pad(w_uq.reshape(MLA_Q_RANK, MLA_HEADS, dq), ((0, 0), (0, 0), (0, LANES - dq)))
    wkv = w_ukv.reshape(MLA_KV_RANK, MLA_HEADS, MLA_NOPE + MLA_V)
    wk = jnp.pad(wkv[:, :, :MLA_NOPE], ((0, 0), (0, 0), (0, LANES - MLA_NOPE)))
    zv = jnp.zeros((MLA_KV_RANK, MLA_HEADS // 2, MLA_V), F32)
    wv = jnp.concatenate([wkv[:, 0::2, MLA_NOPE:], zv, zv, wkv[:, 1::2, MLA_NOPE:]], axis=-1)
    wv = wv.reshape(MLA_KV_RANK, MLA_HEADS * LANES)
    gbias = jnp.stack([b_i, b_f], axis=1).reshape(4 * MLSTM_HEADS, 1)
    bf = lambda a: a.astype(BF16)
    return dict(
        wcqkr=bf(jnp.concatenate([w_in[:, o[0]:o[1]], wkr], axis=1)), wckv=bf(w_in[:, o[1]:o[2]]),
        wmqk=bf(w_in[:, o[3]:o[5]]), wmv=bf(w_in[:, o[5]:o[6]]), wmo=bf(w_in[:, o[6]:o[7]]),
        wgt=bf(w_in[:, o[7]:o[8]].T), wz=bf(w_in[:, o[8]:o[9]]),
        qn=q_norm[None], wuq=bf(wuq.reshape(MLA_Q_RANK, -1)),
        kvn=kv_norm[None], wk=bf(wk.reshape(MLA_KV_RANK, -1)), wv=bf(wv),
        cw=jnp.zeros((SUBLANES, conv_w.shape[1]), F32).at[:MLSTM_CONV].set(conv_w), cb=conv_b[None], gb=gbias)


def _tile_specs(nct, nt):
    tile = lambda w: pl.BlockSpec((None, TM, w), lambda b, t: (b, t, 0))
    per = TM // HALO
    nlat = nt - nct
    src_ctx = pl.BlockSpec((None, TM, D_MODEL), lambda b, t: (b, jnp.minimum(t, nct - 1), 0))
    src_lat = pl.BlockSpec((None, TM, D_MODEL), lambda b, t: (b, jnp.maximum(t - nct, 0), 0))
    prev = pl.BlockSpec((None, HALO, D_MODEL), lambda b, t: (b, jnp.maximum((t - nct) * per - 1, 0), 0))
    nxt = pl.BlockSpec((None, HALO, D_MODEL),
                       lambda b, t: (b, jnp.clip((t - nct + 1) * per, 0, nlat * per - 1), 0))
    mod = pl.BlockSpec((None, None, 1, D_MODEL), lambda b, t: (b, (t >= nct).astype(jnp.int32), 0, 0))
    return tile, src_ctx, src_lat, prev, nxt, mod


def _proj0(ctx, x, scale, shift, norm_g, cos_t, sin_t, w, nct):
    bsz = x.shape[0]
    ttot = ctx.shape[1] + x.shape[1]
    nt = ttot // TM
    nb = MLSTM_HEADS * MLSTM_DH
    tile, src_ctx, src_lat, prev, nxt, mod = _tile_specs(nct, nt)
    tab = pl.BlockSpec((TM, LANES), lambda b, t: (t, 0))
    wnames = ["wcqkr", "wckv", "wmqk", "wmv", "wmo", "wgt", "wz",
              "qn", "wuq", "kvn", "wk", "wv", "cw", "cb", "gb"]
    ws = [w[n] for n in wnames]
    tok = lambda width, dt: jax.ShapeDtypeStruct((bsz, ttot, width), dt)
    out_shape = (tok(MLA_HEADS * LANES, BF16), tok(MLA_HEADS * LANES, BF16), tok(MLA_HEADS * LANES, BF16),
                 tok(nb, BF16), tok(nb, BF16), tok(nb, BF16),
                 jax.ShapeDtypeStruct((bsz, ttot // MLSTM_CHUNK, 2 * MLSTM_HEADS, MLSTM_CHUNK), F32), tok(LANES, F32),
                 tok(nb, BF16), tok(D_MODEL, BF16))
    out_specs = (tile(MLA_HEADS * LANES), tile(MLA_HEADS * LANES), tile(MLA_HEADS * LANES),
                 tile(nb), tile(nb), tile(nb),
                 pl.BlockSpec((None, TM // MLSTM_CHUNK, 2 * MLSTM_HEADS, MLSTM_CHUNK), lambda b, t: (b, t, 0, 0)),
                 tile(LANES),
                 tile(nb), tile(D_MODEL))
    return pl.pallas_call(
        functools.partial(_proj0_kernel, nct=nct, nt=nt),
        out_shape=out_shape,
        grid=(bsz, nt),
        in_specs=[src_ctx, src_lat, prev, nxt, mod, mod, _full((1, D_MODEL)), tab, tab] + [_full(a.shape) for a in ws],
        out_specs=out_specs,
        scratch_shapes=[pltpu.VMEM((TM + 2 * HALO, 2 * nb), F32)],
        compiler_params=_params("parallel", "parallel"),
        name="proj0",
    )(ctx, x, x, x, scale, shift, norm_g[None], cos_t, sin_t, *ws)


def _mla_kernel(q_ref, k_ref, v_ref, o_ref, *, nct):
    t = pl.program_id(2)
    lane = lax.broadcasted_iota(jnp.int32, (TM, LANES), 1)

    def attend(nkeys):
        qk = lambda j: _dot_nt(q_ref[:, j * LANES:(j + 1) * LANES], k_ref[0:nkeys, j * LANES:(j + 1) * LANES])
        outs = []
        s_next = qk(0)
        for j in range(MLA_HPS):
            s = s_next
            if j + 1 < MLA_HPS:
                s_next = qk(j + 1)
            p = jnp.exp2(s - jnp.max(s, axis=-1, keepdims=True))
            o = _dot(p.astype(BF16), v_ref[0:nkeys, j * LANES:(j + 1) * LANES])
            outs.append(o / pltpu.roll(o, MLA_V, axis=1))
        for j in range(MLA_HPS // 2):
            o_ref[:, j * LANES:(j + 1) * LANES] = jnp.where(lane < MLA_V, outs[2 * j], outs[2 * j + 1]).astype(BF16)

    @pl.when(t < nct)
    def _():
        attend(nct * TM)

    @pl.when(t >= nct)
    def _():
        attend(k_ref.shape[0])


def _mla_attention(q, k, v, nct):
    bsz, ttot, _ = q.shape
    nt = ttot // TM
    hw = MLA_HPS * LANES
    return pl.pallas_call(
        functools.partial(_mla_kernel, nct=nct),
        out_shape=jax.ShapeDtypeStruct((bsz, ttot, MLA_HEADS * MLA_V), BF16),
        grid=(bsz, MLA_HEADS // MLA_HPS, nt),
        in_specs=[pl.BlockSpec((None, TM, hw), lambda b, p, t: (b, t, p)),
                  pl.BlockSpec((None, ttot, hw), lambda b, p, t: (b, 0, p)),
                  pl.BlockSpec((None, ttot, hw), lambda b, p, t: (b, 0, p))],
        out_specs=pl.BlockSpec((None, TM, hw // 2), lambda b, p, t: (b, t, p)),
        compiler_params=_params("parallel", "parallel", "parallel"),
        name="mla_attention",
    )(q, k, v)


def _scan_block(j, nct, nt):
    return jnp.where(j < nct, nct - 1 - j, nt - 1 - (j - nct))


def _mlstm_kernel(qf_ref, kf_ref, vf_ref, grf_ref, gcf_ref, qr_ref, kr_ref, vr_ref, grr_ref, gcr_ref,
                  hf_ref, hr_ref, c_scr, m_scr):
    @pl.when(pl.program_id(1) == 0)
    def _():
        c_scr[...] = jnp.zeros_like(c_scr)
        m_scr[...] = jnp.zeros_like(m_scr)

    L = MLSTM_CHUNK
    nchunk = TM // L
    nh = MLSTM_HEADS
    nchain = 2 * nh
    row = lax.broadcasted_iota(jnp.int32, (L, L), 0)
    col = lax.broadcasted_iota(jnp.int32, (L, L), 1)
    ones = jnp.ones((L, MLSTM_DH), BF16)
    wide = lambda a: jnp.broadcast_to(a, (L, MLSTM_DH))
    twice = lambda a: jnp.concatenate([a, a], axis=1)
    refs = ((qf_ref, kf_ref, vf_ref, grf_ref, gcf_ref, hf_ref), (qr_ref, kr_ref, vr_ref, grr_ref, gcr_ref, hr_ref))
    nbatch = qf_ref.shape[0]

    def load(ci):
        cs = []
        for bi, d in [(bi, d) for bi in range(nbatch) for d in range(2)]:
            q_ref, k_ref, v_ref, gr_ref, gc_ref, o_ref = (r.at[bi] for r in refs[d])
            cd = ci if d == 0 else nchunk - 1 - ci
            rs = slice(cd * L, (cd + 1) * L)
            g_rows = gr_ref[cd]
            g_cols = gc_ref[rs, :]
            last = L - 1 if d == 0 else 0
            for hd in range(nh):
                ch = d * nh + hd
                hs = slice(hd * MLSTM_DH, (hd + 1) * MLSTM_DH)
                cs.append(dict(
                    rs=rs, hs=hs, o_ref=o_ref, causal=(col <= row) if d == 0 else (col >= row),
                    c_row=g_rows[ch:ch + 1], c_col=g_cols[:, ch:ch + 1],
                    pm_col=g_cols[:, nchain + ch:nchain + ch + 1], b_col=g_cols[:, 2 * nchain + ch:2 * nchain + ch + 1],
                    pm_last=g_cols[last:last + 1, nchain + ch:nchain + ch + 1],
                    b_last=g_cols[last:last + 1, 2 * nchain + ch:2 * nchain + ch + 1],
                    q=q_ref[rs, hs], k=k_ref[rs, hs],
                    v_ext=jnp.concatenate([v_ref[rs, hs], ones], axis=1)))
        return cs

    def prep_scores(cs):
        for c in cs:
            c["s"] = _dot_nt(c["q"], c["k"])

    def prep_weights(cs):
        for c in cs:
            c["pm_w"] = wide(c["pm_col"])
            c["b_w"] = wide(c["b_col"])
            c["w"] = jnp.exp2(jnp.where(c["causal"], c["c_row"] - c["pm_w"][:, :L], -jnp.inf))
            c["kwf"] = jnp.exp2(wide(c["c_col"] - c["pm_last"]))

    def prep_operands(cs):
        for c in cs:
            c["p"] = (c["s"] * c["w"]).astype(BF16)
            c["kw"] = (c["kwf"] * c["k"].astype(F32)).astype(BF16)

    def prep_products(cs):
        for c in cs:
            c["o1"] = _dot(c["p"], c["v_ext"])
            c["dc"] = _dot_tn(c["kw"], c["v_ext"])

    def scan_read(cs):
        for ch, c in enumerate(cs):
            c["qc"] = _dot(c["q"], state[ch].astype(BF16))

    def scan_factors(cs):
        for ch, c in enumerate(cs):
            g_w = jnp.maximum(m[ch], c["pm_w"])
            c["e_intra"] = jnp.exp2(c["pm_w"] - g_w)
            c["e_inter"] = jnp.exp2(m[ch] - g_w)
            c["floor"] = jnp.exp2(-(c["b_w"] + g_w))
            g_last = jnp.maximum(m[ch], c["pm_last"])
            c["keep"] = jnp.exp2(m[ch] - g_last)
            c["gain"] = jnp.exp2(c["pm_last"] - g_last)
            c["m_new"] = c["b_last"] + g_last

    def scan_update(cs):
        for ch, c in enumerate(cs):
            o = twice(c["e_intra"]) * c["o1"] + twice(c["e_inter"]) * c["qc"]
            h = o[:, :MLSTM_DH] / jnp.maximum(jnp.abs(o[:, MLSTM_DH:]), c["floor"])
            c["o_ref"][c["rs"], c["hs"]] = h.astype(BF16)
            state[ch] = c["keep"] * state[ch] + c["gain"] * c["dc"]
            m[ch] = c["m_new"]

    nstate = nbatch * nchain
    state = [c_scr[ch] for ch in range(nstate)]
    m = [m_scr[ch, 0:1, 0:1] for ch in range(nstate)]
    chunks = [load(ci) for ci in range(nchunk)]
    everything = [c for cs in chunks for c in cs]
    for stage in (prep_scores, prep_weights, prep_operands):
        stage(everything)
    prep_products(chunks[0])
    for i, cs in enumerate(chunks):
        scan_read(cs)
        if i + 1 < nchunk:
            prep_products(chunks[i + 1])
        scan_factors(cs)
        scan_update(cs)
    for ch in range(nstate):
        c_scr[ch] = state[ch]
        m_scr[ch] = jnp.broadcast_to(m[ch], m_scr.shape[1:])


def _mlstm(mq, mk, mv, g_rows, g_cols, nct):
    bsz, ttot, nb = mq.shape
    nt = ttot // TM
    per = TM // MLSTM_CHUNK
    sb = math.gcd(bsz, MLSTM_BATCH)
    fwd = lambda w: pl.BlockSpec((sb, TM, w), lambda b, j: (b, j, 0))
    rev = lambda w: pl.BlockSpec((sb, TM, w), lambda b, j: (b, _scan_block(j, nct, nt), 0))
    gshape = (sb, per, 2 * MLSTM_HEADS, MLSTM_CHUNK)
    gfwd = pl.BlockSpec(gshape, lambda b, j: (b, j, 0, 0))
    grev = pl.BlockSpec(gshape, lambda b, j: (b, _scan_block(j, nct, nt), 0, 0))
    nstate = sb * 2 * MLSTM_HEADS
    return pl.pallas_call(
        _mlstm_kernel,
        out_shape=(jax.ShapeDtypeStruct((bsz, ttot, nb), BF16),) * 2,
        grid=(bsz // sb, nt),
        in_specs=[fwd(nb), fwd(nb), fwd(nb), gfwd, fwd(LANES), rev(nb), rev(nb), rev(nb), grev, rev(LANES)],
        out_specs=(fwd(nb), rev(nb)),
        scratch_shapes=[pltpu.VMEM((nstate, MLSTM_DH, 2 * MLSTM_DH), F32),
                        pltpu.VMEM((nstate, SUBLANES, LANES), F32)],
        compiler_params=_params("parallel", "arbitrary"),
        name="mlstm_scan",
    )(mq, mk, mv, g_rows, g_cols, mq, mk, mv, g_rows, g_cols)


def _head_rms(x, n_heads):
    w = x.shape[1] // n_heads
    return jnp.concatenate([_rms(x[:, i * w:(i + 1) * w]) for i in range(n_heads)], axis=1)


def _mid_kernel(ctx_ref, lat_ref, gate_ref, a_ref, hf_ref, hr_ref, mo_ref, z0_ref, hn_ref, wout_ref,
                sc_ref, sh_ref, g_ref, wq_ref, wk_ref, wv_ref, wga_ref, wnq_ref, wnk_ref, wnv_ref, wz_ref,
                wgate_ref, bgate_ref,
                h_out, gq_out, gk_out, gv_out, bc_out, nq_out, nk_out, nv_out, z_out, *, nct):
    f32 = lambda r: r[...].astype(F32)
    hm = _head_rms(f32(mo_ref) * (f32(hf_ref) + f32(hr_ref)), MLSTM_HEADS) * hn_ref[...]
    cat = (jnp.concatenate([f32(a_ref), hm], axis=1) * f32(z0_ref)).astype(BF16)
    h0 = jnp.where(pl.program_id(1) < nct, ctx_ref[...], lat_ref[...])
    h = h0 + gate_ref[...] * _dot(cat, wout_ref[...])
    h_out[...] = h

    ub = _modnorm(h, g_ref[...], sc_ref[...], sh_ref[...]).astype(BF16)
    ga = _dot(ub, wga_ref[...]).astype(BF16)
    gate_pre = _dot(ga, wgate_ref[...])
    gq_out[...] = (_dot(ub, wq_ref[...]) * GLA_DK ** -0.5).astype(BF16)
    gk_out[...] = _dot(ub, wk_ref[...]).astype(BF16)
    gv_out[...] = _dot(ub, wv_ref[...]).astype(BF16)

    lg = _log_sigmoid(gate_pre + bgate_ref[...]) * (1.0 / GLA_TAU)
    hi, mid, lo = _split3(lg)
    nq_out[...] = (_dot(ub, wnq_ref[...]) * (NA_DH ** -0.5 * LOG2E)).astype(BF16)
    nk_out[...] = _dot(ub, wnk_ref[...]).astype(BF16)
    nv_out[...] = _dot(ub, wnv_ref[...]).astype(BF16)
    row = lax.broadcasted_iota(jnp.int32, (TM, TM), 0)
    col = lax.broadcasted_iota(jnp.int32, (TM, TM), 1)
    same = (row // CHUNK) == (col // CHUNK)
    nk = GLA_HEADS * GLA_DK
    for d in range(2):
        tri = jnp.where(jnp.logical_and(same, (col <= row) if d == 0 else (col >= row)), 1.0, 0.0).astype(BF16)
        cols = slice(d * nk, (d + 1) * nk)
        bc_out[:, cols] = (_dot(tri, hi[:, cols]) + _dot(tri, mid[:, cols]) + _dot(tri, lo[:, cols])) * LOG2E
    z_out[...] = _silu(_dot(ub, wz_ref[...])).astype(BF16)


def _proj1_weights(w_in, w_gate, b_gate):
    nk = GLA_HEADS * GLA_DK
    nc = GLA_HEADS * GLA_DV
    nd = NA_HEADS * NA_DH
    o = np.cumsum([0, nk, nk, nc, 2 * GLA_GATE_RANK, nd, nd, nd, nc + nd])
    r = GLA_GATE_RANK
    wgate = jnp.zeros((2 * r, 2 * nk), F32).at[:r, :nk].set(w_gate[0]).at[r:, nk:].set(w_gate[1])
    bf = lambda a: a.astype(BF16)
    names = ["wq", "wk", "wv", "wga", "wnq", "wnk", "wnv", "wz"]
    w = {n: bf(w_in[:, o[i]:o[i + 1]]) for i, n in enumerate(names)}
    w["wgate"] = bf(wgate)
    w["bgate"] = b_gate.reshape(1, 2 * nk)
    return w


def _mid(ctx, x, gate0, a, hf, hr, mo, z0, h_norm, w_out, scale, shift, norm_g, w, nct):
    bsz = x.shape[0]
    ttot = ctx.shape[1] + x.shape[1]
    nt = ttot // TM
    tile, src_ctx, src_lat, _, _, mod = _tile_specs(nct, nt)
    nb = MLSTM_HEADS * MLSTM_DH
    nk = GLA_HEADS * GLA_DK
    nc = GLA_HEADS * GLA_DV
    nd = NA_HEADS * NA_DH
    wnames = ["wq", "wk", "wv", "wga", "wnq", "wnk", "wnv", "wz", "wgate", "bgate"]
    ws = [w[n] for n in wnames]
    tok = lambda width, dt: jax.ShapeDtypeStruct((bsz, ttot, width), dt)
    return pl.pallas_call(
        functools.partial(_mid_kernel, nct=nct),
        out_shape=(tok(D_MODEL, F32), tok(nk, BF16), tok(nk, BF16), tok(nc, BF16), tok(2 * nk, F32),
                   tok(nd, BF16), tok(nd, BF16), tok(nd, BF16), tok(nc + nd, BF16)),
        grid=(bsz, nt),
        in_specs=[src_ctx, src_lat, mod, tile(MLA_HEADS * MLA_V), tile(nb), tile(nb), tile(nb), tile(D_MODEL),
                  _full((1, nb)), _full(w_out.shape), mod, mod, _full((1, D_MODEL))] + [_full(a.shape) for a in ws],
        out_specs=(tile(D_MODEL), tile(nk), tile(nk), tile(nc), tile(2 * nk), tile(nd), tile(nd), tile(nd),
                   tile(nc + nd)),
        compiler_params=_params("parallel", "parallel"),
        name="out0_proj1",
    )(ctx, x, gate0, a, hf, hr, mo, z0, h_norm[None], w_out.astype(BF16), scale, shift, norm_g[None], *ws)


def _gla_kernel(qf_ref, kf_ref, vf_ref, bf_ref, qr_ref, kr_ref, vr_ref, br_ref, of_ref, or_ref, s_scr):
    @pl.when(pl.program_id(1) == 0)
    def _():
        s_scr[...] = jnp.zeros_like(s_scr)

    L = CHUNK
    nchunk = TM // L
    npair = GLA_HEADS // 2
    lane = lax.broadcasted_iota(jnp.int32, (L, LANES), 1)
    rowv = lax.broadcasted_iota(jnp.int32, (L, LANES), 0)
    srow = lax.broadcasted_iota(jnp.int32, (LANES, 2 * GLA_DV), 0)
    scol = lax.broadcasted_iota(jnp.int32, (LANES, 2 * GLA_DV), 1)
    own_block = (srow < GLA_DK) == (scol < GLA_DV)
    vcol = lax.broadcasted_iota(jnp.int32, (L, 2 * GLA_DV), 1)
    e_row = lax.broadcasted_iota(jnp.int32, (LANES, LANES), 0)
    e_col = lax.broadcasted_iota(jnp.int32, (LANES, LANES), 1)
    head_ones = jnp.where((e_row < GLA_DK) == (e_col < GLA_DK), 1.0, 0.0).astype(BF16)
    leaf_idx = lax.broadcasted_iota(jnp.int32, (GLA_LEAF * GLA_LEAF, LANES), 0)
    leaf_lane = lax.broadcasted_iota(jnp.int32, (GLA_LEAF * GLA_LEAF, LANES), 1) % L
    leaf_s, leaf_t = leaf_idx // GLA_LEAF, leaf_idx % GLA_LEAF
    rep = lambda a: jnp.concatenate(
        [jnp.broadcast_to(a[s:s + 1], (GLA_LEAF, a.shape[1])) for s in range(GLA_LEAF)], axis=0)
    til = lambda a: jnp.concatenate([a] * GLA_LEAF, axis=0)

    def chunk_step(ci, carry):
        cs = []
        for bi, d in [(bi, d) for bi in range(qf_ref.shape[0]) for d in range(2)]:
            q_ref, k_ref, v_ref, b_ref, o_ref = (
                r.at[bi] for r in ((qf_ref, kf_ref, vf_ref, bf_ref, of_ref), (qr_ref, kr_ref, vr_ref, br_ref, or_ref))[d])
            cd = ci if d == 0 else nchunk - 1 - ci
            base = pl.multiple_of(cd * L, L)
            for p in range(npair):
                ls = slice(p * LANES, (p + 1) * LANES)
                bc = b_ref[pl.ds(base, L), ls]
                cs.append(dict(
                    d=d, p=p, bi=bi, base=base, o_ref=o_ref, bc=bc,
                    tau_v=rowv if d == 0 else L - 1 - rowv,
                    tau_s=(lane % L) if d == 0 else L - 1 - (lane % L),
                    q=q_ref[pl.ds(base, L), ls].astype(F32),
                    k=k_ref[pl.ds(base, L), ls].astype(F32),
                    v=v_ref[pl.ds(base, L), p * 2 * GLA_DV:(p + 1) * 2 * GLA_DV],
                    b_last=bc[L - 1:L] if d == 0 else bc[0:1],
                    state=s_scr[bi, d, p]))

        for c in cs:
            c["o_pair"] = _dot((c["q"] * jnp.exp2(c["bc"])).astype(BF16), c["state"].astype(BF16))

        for c in cs:
            c["att"] = jnp.zeros((L, LANES), F32)
        bs = L // 2
        while bs >= GLA_LEAF:
            for c in cs:
                d, bc, tau_v = c["d"], c["bc"], c["tau_v"]
                later = (tau_v // bs) % 2 == 1
                ref_b = None
                for blk in range(L // (2 * bs)):
                    tau_ref = blk * 2 * bs + bs - 1
                    idx = tau_ref if d == 0 else L - 1 - tau_ref
                    r = jnp.broadcast_to(bc[idx:idx + 1], (L, LANES))
                    ref_b = r if ref_b is None else jnp.where(tau_v // (2 * bs) == blk, r, ref_b)
                c["qs"] = jnp.where(later, c["q"] * jnp.exp2(jnp.where(later, bc - ref_b, 0.0)), 0.0).astype(BF16)
                ks = jnp.where(later, 0.0, c["k"] * jnp.exp2(jnp.where(later, 0.0, ref_b - bc)))
                c["ks2"] = jnp.concatenate([jnp.where(lane < GLA_DK, ks, 0.0), jnp.where(lane < GLA_DK, 0.0, ks)],
                                           axis=0).astype(BF16)
            for c in cs:
                same_parent = (c["tau_v"] // (2 * bs)) == (c["tau_s"] // (2 * bs))
                c["att"] = c["att"] + jnp.where(same_parent, _dot_nt(c["qs"], c["ks2"]), 0.0)
            bs //= 2

        for c in cs:
            c["leaf"] = []
        for blk in range(L // GLA_LEAF):
            rs = slice(blk * GLA_LEAF, (blk + 1) * GLA_LEAF)
            for c in cs:
                qb, bb, kb = c["q"][rs], c["bc"][rs], c["k"][rs]
                ok = (leaf_t >= leaf_s) if c["d"] == 0 else (leaf_t <= leaf_s)
                w = jnp.where(ok, til(qb) * jnp.exp2(til(bb) - rep(bb)) * rep(kb), 0.0)
                c["a"] = _dot(w.astype(BF16), head_ones)
            for c in cs:
                a = jnp.where(leaf_lane == leaf_s + blk * GLA_LEAF, c["a"], 0.0)
                acc = a[0:GLA_LEAF]
                for s in range(1, GLA_LEAF):
                    acc = acc + a[s * GLA_LEAF:(s + 1) * GLA_LEAF]
                c["leaf"].append(acc)

        for c in cs:
            v = c["v"]
            c["att"] = (c["att"] + jnp.concatenate(c["leaf"], axis=0)).astype(BF16)
            c["v_blk"] = jnp.concatenate([jnp.where(vcol < GLA_DV, v, jnp.zeros_like(v)),
                                          jnp.where(vcol < GLA_DV, jnp.zeros_like(v), v)], axis=0)
            c["ke"] = (c["k"] * jnp.exp2(c["b_last"] - c["bc"])).astype(BF16)
            c["decay_col"] = jnp.sum(
                jnp.where(e_row == e_col, jnp.broadcast_to(jnp.exp2(c["b_last"]), (LANES, LANES)), 0.0),
                axis=1, keepdims=True)
        for c in cs:
            c["o"] = c["o_pair"] + _dot(c["att"], c["v_blk"])
            c["ds"] = _dot_tn(c["ke"], c["v"])
        for c in cs:
            p = c["p"]
            c["o_ref"][pl.ds(c["base"], L), p * 2 * GLA_DV:(p + 1) * 2 * GLA_DV] = c["o"].astype(BF16)
            s_scr[c["bi"], c["d"], p] = c["decay_col"] * c["state"] + jnp.where(own_block, c["ds"], 0.0)
        return carry

    lax.fori_loop(0, nchunk, chunk_step, 0)


def _gla(gq, gk, gv, bc, nct):
    bsz, ttot, nk = gq.shape
    nc = gv.shape[2]
    nt = ttot // TM
    sb = math.gcd(bsz, GLA_BATCH)
    fwd = lambda w: pl.BlockSpec((sb, TM, w), lambda b, j: (b, j, 0))
    rev = lambda w: pl.BlockSpec((sb, TM, w), lambda b, j: (b, _scan_block(j, nct, nt), 0))
    bfwd = pl.BlockSpec((sb, TM, nk), lambda b, j: (b, j, 0))
    brev = pl.BlockSpec((sb, TM, nk), lambda b, j: (b, _scan_block(j, nct, nt), 1))
    return pl.pallas_call(
        _gla_kernel,
        out_shape=(jax.ShapeDtypeStruct((bsz, ttot, nc), BF16),) * 2,
        grid=(bsz // sb, nt),
        in_specs=[fwd(nk), fwd(nk), fwd(nc), bfwd, rev(nk), rev(nk), rev(nc), brev],
        out_specs=(fwd(nc), rev(nc)),
        scratch_shapes=[pltpu.VMEM((sb, 2, GLA_HEADS // 2, LANES, 2 * GLA_DV), F32)],
        compiler_params=_params("parallel", "arbitrary"),
        name="gla_scan",
    )(gq, gk, gv, bc, gq, gk, gv, bc)


def _na_bias_tables(rpb, rows):
    kh = NA_KH
    c = np.arange(GRID_W)
    cs = np.clip(c - NA_KW // 2, 0, GRID_W - NA_KW)
    col_ok = (c[None, :] >= cs[:, None]) & (c[None, :] < cs[:, None] + NA_KW)
    pad = GRID_W - NA_KW
    rpb_pad = jnp.pad(rpb.astype(F32) * LOG2E, ((0, 0), (0, 0), (pad, pad)))
    t1 = jnp.stack([rpb_pad[:, :, GRID_W - 1 - cq:2 * GRID_W - 1 - cq] for cq in range(GRID_W)], axis=2)
    t1 = jnp.where(col_ok, t1, NEG_BIG)
    masked = jnp.full((NA_HEADS, GRID_W, GRID_W), NEG_BIG, F32)
    per_row = []
    for r0 in (0, NA_QROWS, rows - NA_QROWS):
        kb = int(np.clip(r0 - kh // 2, 0, rows - NA_KROWS))
        for qrow in range(r0, r0 + NA_QROWS):
            ws = int(np.clip(qrow - kh // 2, 0, rows - kh))
            per_row.append(jnp.concatenate(
                [t1[:, krow - qrow + NA_KH - 1] if ws <= krow < ws + kh else masked
                 for krow in range(kb, kb + NA_KROWS)], axis=-1))
    return jnp.concatenate(per_row, axis=1)


def _na_out1_kernel(q_ref, k_ref, v_ref, bias_ref, h_ref, gate_ref, of_ref, or_ref, z_ref, gn_ref, w_ref, fn_ref,
                    o_ref, *, n_ctx, rows):
    rb = pl.program_id(1)
    nq = NA_QROWS * GRID_W
    nk = NA_KROWS * GRID_W
    kb = jnp.clip(rb * NA_QROWS - NA_KH // 2, 0, rows - NA_KROWS)
    start = pl.multiple_of(n_ctx + kb * GRID_W, GRID_W)
    lane = lax.broadcasted_iota(jnp.int32, (nq, LANES), 1)
    def scores(j):
        ls = slice((j // 2) * LANES, (j // 2 + 1) * LANES)
        q = q_ref[:, ls]
        qj = jnp.where((lane < NA_DH) == (j % 2 == 0), q, jnp.zeros_like(q))
        return _dot_nt(qj, k_ref[pl.ds(start, nk), ls]) + bias_ref[j], _dot_nt(qj, k_ref[0:n_ctx, ls])

    outs = []
    nxt = scores(0)
    for j in range(NA_HEADS):
        s_loc, s_ctx = nxt
        if j + 1 < NA_HEADS:
            nxt = scores(j + 1)
        ls = slice((j // 2) * LANES, (j // 2 + 1) * LANES)
        m = jnp.maximum(jnp.max(s_loc, axis=-1, keepdims=True), jnp.max(s_ctx, axis=-1, keepdims=True))
        p_loc = jnp.exp2(s_loc - m)
        p_ctx = jnp.exp2(s_ctx - m)
        den = jnp.sum(p_loc, axis=-1, keepdims=True) + jnp.sum(p_ctx, axis=-1, keepdims=True)
        outs.append((_dot(p_loc.astype(BF16), v_ref[pl.ds(start, nk), ls])
                     + _dot(p_ctx.astype(BF16), v_ref[0:n_ctx, ls])) / den)
    na = [jnp.where(lane < NA_DH, outs[2 * i], outs[2 * i + 1]) for i in range(NA_HEADS // 2)]

    f32 = lambda r: r[...].astype(F32)
    g = _head_rms(f32(of_ref) + f32(or_ref), GLA_HEADS) * gn_ref[...]
    cat = (jnp.concatenate([g] + na, axis=1) * f32(z_ref)).astype(BF16)
    h = h_ref[...] + gate_ref[...] * _dot(cat, w_ref[...])
    o_ref[...] = _rms(h) * fn_ref[...]


def _na_out1(nq, nk, nv, bias, h, gate, of, orv, z, gla_norm, w_out, final_norm, n_ctx, rows):
    bsz, ttot, nd = nq.shape
    nrb = rows // NA_QROWS
    qrows = NA_QROWS * GRID_W
    krows = NA_KROWS * GRID_W
    assert qrows == TM
    nct = n_ctx // TM
    nc = GLA_HEADS * GLA_DV

    def variant(rb):
        return jnp.where(rb == 0, 0, jnp.where(rb == nrb - 1, 2, 1))

    lat = lambda w: pl.BlockSpec((None, TM, w), lambda b, rb: (b, rb + nct, 0))
    whole = lambda w: pl.BlockSpec((None, ttot, w), lambda b, rb: (b, 0, 0))
    return pl.pallas_call(
        functools.partial(_na_out1_kernel, n_ctx=n_ctx, rows=rows),
        out_shape=jax.ShapeDtypeStruct((bsz, rows * GRID_W, D_MODEL), F32),
        grid=(bsz, nrb),
        in_specs=[lat(nd), whole(nd), whole(nd),
                  pl.BlockSpec((NA_HEADS, qrows, krows), lambda b, rb: (0, variant(rb), 0)),
                  lat(D_MODEL), pl.BlockSpec((None, None, 1, D_MODEL), lambda b, rb: (b, 1, 0, 0)),
                  lat(nc), lat(nc), lat(nc + nd), _full((1, nc)), _full(w_out.shape), _full((1, D_MODEL))],
        out_specs=pl.BlockSpec((None, TM, D_MODEL), lambda b, rb: (b, rb, 0)),
        compiler_params=_params("parallel", "parallel"),
        name="na_out1",
    )(nq, nk, nv, bias, h, gate, of, orv, z, gla_norm[None], w_out.astype(BF16), final_norm[None])


def kernel(x, c, ctx, c_ctx, l0_norm, l0_w_mod, l0_b_mod, l0_w_in, l0_mla_q_norm, l0_mla_w_uq, l0_mla_kv_norm, l0_mla_w_ukv, l0_mlstm_conv_w, l0_mlstm_conv_b, l0_mlstm_b_i, l0_mlstm_b_f, l0_mlstm_norm, l0_w_out, l1_norm, l1_w_mod, l1_b_mod, l1_w_in, l1_gla_w_gate, l1_gla_b_gate, l1_gla_norm, l1_na_rpb, l1_w_out, final_norm):
    bsz, seq, d = x.shape
    n_ctx = ctx.shape[1]
    rows = seq // GRID_W
    assert d == D_MODEL and seq % TM == 0 and n_ctx == TM
    assert rows % NA_QROWS == 0 and rows >= NA_KROWS and rows // NA_QROWS >= 3
    nct = n_ctx // TM

    (shift0, scale0, gate0), (shift, scale, gate) = _mod_vectors(
        c, c_ctx, ((l0_w_mod, l0_b_mod), (l1_w_mod, l1_b_mod)))

    cos_t, sin_t = _rope_tables(n_ctx, seq)
    w0 = _proj0_weights(l0_w_in, l0_mla_q_norm, l0_mla_w_uq, l0_mla_kv_norm, l0_mla_w_ukv,
                        l0_mlstm_conv_w, l0_mlstm_conv_b, l0_mlstm_b_i, l0_mlstm_b_f)
    q, k, v, mq, mk, mv, g_rows, g_cols, mo, z0 = _proj0(ctx, x, scale0, shift0, l0_norm, cos_t, sin_t, w0, nct)
    a = _mla_attention(q, k, v, nct)
    hf, hr = _mlstm(mq, mk, mv, g_rows, g_cols, nct)

    w1 = _proj1_weights(l1_w_in, l1_gla_w_gate, l1_gla_b_gate)
    h, gq, gk, gv, bc, nq, nk, nv, z = _mid(ctx, x, gate0, a, hf, hr, mo, z0, l0_mlstm_norm, l0_w_out,
                                            scale, shift, l1_norm, w1, nct)
    of, orv = _gla(gq, gk, gv, bc, nct)
    return _na_out1(nq, nk, nv, _na_bias_tables(l1_na_rpb, rows), h, gate, of, orv, z,
                    l1_gla_norm, l1_w_out, final_norm, n_ctx, rows)
```

```python
import functools
import math

import jax
import jax.numpy as jnp
import numpy as np
from jax import lax
from jax.experimental import pallas as pl
from jax.experimental.pallas import tpu as pltpu

F32 = jnp.float32
BF16 = jnp.bfloat16

D_MODEL = 1024
GRID_W = 64
EPS = 1e-6
ROPE_BASE = 10000.0

MLA_HEADS = 8
MLA_Q_RANK = 384
MLA_KV_RANK = 256
MLA_NOPE = 64
MLA_ROPE = 32
MLA_V = 64
ROPE_PAIRS = MLA_ROPE // 4
MLSTM_HEADS = 4
MLSTM_DH = 128
MLSTM_CONV = 3
GLA_HEADS = 4
GLA_DK = 64
GLA_DV = 128
GLA_GATE_RANK = 16
GLA_TAU = 16.0
NA_HEADS = 8
NA_DH = 64
NA_KH = 8
NA_KW = 16

CHUNK = 64
MLSTM_CHUNK = 128
TM = 256
LANES = 128
SUBLANES = 8
HALO = SUBLANES
NA_QROWS = 4
NA_KROWS = NA_QROWS + NA_KH - 1
GLA_LEAF = 8
MLA_HPS = 4
TILE_BATCH = 2
MLSTM_BATCH = 2
GLA_BATCH = 4
VMEM_LIMIT = 56 * 1024 * 1024
NEG_BIG = -1e30
LOG2E = 1.4426950408889634


def _dot(a, b):
    return jnp.dot(a, b, preferred_element_type=F32)


def _dot_nt(a, b):
    return lax.dot_general(a, b, (((1,), (1,)), ((), ())), preferred_element_type=F32)


def _dot_tn(a, b):
    return lax.dot_general(a, b, (((0,), (0,)), ((), ())), preferred_element_type=F32)


def _rms(x):
    return x * lax.rsqrt(jnp.mean(x * x, axis=-1, keepdims=True) + EPS)


def _sigmoid(x):
    return 1.0 / (1.0 + jnp.exp(-x))


def _silu(x):
    return x * _sigmoid(x)


def _log_sigmoid(x):
    return jnp.minimum(x, 0.0) - jnp.log(1.0 + jnp.exp(-jnp.abs(x)))


def _params(*sem):
    return pltpu.CompilerParams(dimension_semantics=sem, vmem_limit_bytes=VMEM_LIMIT)


def _full(shape):
    nd = len(shape)
    return pl.BlockSpec(shape, lambda *_: (0,) * nd)


def _mod_kernel(c_ref, w0_ref, b0_ref, w1_ref, b1_ref, o0_ref, o1_ref):
    sc = _silu(c_ref[...]).astype(BF16)
    o0_ref[...] = _dot(sc, w0_ref[...].astype(BF16)) + b0_ref[...]
    o1_ref[...] = _dot(sc, w1_ref[...].astype(BF16)) + b1_ref[...]


def _mod_vectors(c, c_ctx, mods):
    bsz, d = c.shape
    rows = -(-(bsz + 1) // SUBLANES) * SUBLANES
    cc = jnp.zeros((rows, d), F32).at[:bsz].set(c).at[bsz].set(c_ctx)
    (w0, b0), (w1, b1) = mods
    n = w0.shape[1]
    wspec = pl.BlockSpec((d, d), lambda j: (0, j))
    vspec = pl.BlockSpec((1, d), lambda j: (0, j))
    ospec = pl.BlockSpec((rows, d), lambda j: (0, j))
    per_layer = pl.pallas_call(
        _mod_kernel,
        out_shape=(jax.ShapeDtypeStruct((rows, n), F32),) * 2,
        grid=(n // d,),
        in_specs=[_full((rows, d)), wspec, vspec, wspec, vspec],
        out_specs=(ospec, ospec),
        compiler_params=_params("parallel"),
        name="modulation",
    )(cc, w0, b0[None], w1, b1[None])
    out = []
    for mod in per_layer:
        vecs = []
        for part in jnp.split(mod, 3, axis=-1):
            ctx_v = jnp.broadcast_to(part[bsz][None], (bsz, d))
            vecs.append(jnp.stack([ctx_v, part[:bsz]], axis=1)[:, :, None, :])
        out.append(vecs)
    return out


def _modnorm(x, g, scale, shift):
    return _rms(x) * g * (1.0 + scale) + shift


def _seg_scan(x, reverse, op, fill):
    width = x.shape[1]
    pos = lax.broadcasted_iota(jnp.int32, x.shape, 1) % MLSTM_CHUNK
    k = 1
    while k < MLSTM_CHUNK:
        if reverse:
            x = op(x, jnp.where(pos < MLSTM_CHUNK - k, pltpu.roll(x, width - k, axis=1), fill))
        else:
            x = op(x, jnp.where(pos >= k, pltpu.roll(x, k, axis=1), fill))
        k *= 2
    return x


def _split3(x):
    hi = x.astype(BF16)
    r1 = x - hi.astype(F32)
    mid = r1.astype(BF16)
    return hi, mid, (r1 - mid.astype(F32)).astype(BF16)


def _proj0_kernel(ctx_ref, lat_ref, hp_ref, hn_ref, sc_ref, sh_ref, g_ref, cos_ref, sin_ref,
                  wcqkr_ref, wckv_ref, wmqk_ref, wmv_ref, wmo_ref, wgt_ref, wz_ref,
                  qn_ref, wuq_ref, kvn_ref, wk_ref, wv_ref, cw_ref, cb_ref, gb_ref,
                  q_out, k_out, v_out, mq_out, mk_out, mv_out, gr_out, gc_out, mo_out, z_out,
                  pbuf, *, nct, nt):
    t = pl.program_id(1)
    h = jnp.where(t < nct, ctx_ref[...], lat_ref[...])
    x = jnp.concatenate([hp_ref[...], h, hn_ref[...]], axis=0)
    ub_ext = _modnorm(x, g_ref[...], sc_ref[...], sh_ref[...]).astype(BF16)
    ub = ub_ext[HALO:HALO + TM]

    gt = _dot_nt(wgt_ref[...], ub) + gb_ref[...]
    half = MLSTM_HEADS
    per_dir = []
    for d in range(2):
        gd = gt[d * SUBLANES:(d + 1) * SUBLANES]
        bcum = _seg_scan(_log_sigmoid(gd), d == 1, jnp.add, 0.0)
        b_top = pltpu.roll(bcum, half, axis=0)
        c8 = gd - b_top
        pm8 = _seg_scan(c8, d == 1, jnp.maximum, -jnp.inf)
        per_dir.append((c8, pm8, b_top))

    cos = cos_ref[...]
    sin = sin_ref[...]
    cos_all = jnp.concatenate([cos] * MLA_HEADS, axis=1)
    sin_all = jnp.concatenate([sin] * MLA_HEADS, axis=1)

    def swap_halves(a):
        lane = lax.broadcasted_iota(jnp.int32, a.shape, 1)
        first = lane % (2 * ROPE_PAIRS) < ROPE_PAIRS
        return jnp.where(first, pltpu.roll(a, a.shape[1] - ROPE_PAIRS, axis=1), pltpu.roll(a, ROPE_PAIRS, axis=1))

    cqkr = _dot(ub, wcqkr_ref[...])
    cq = cqkr[:, :MLA_Q_RANK]
    kr = cqkr[:, MLA_Q_RANK:]
    ckv = _dot(ub, wckv_ref[...])
    pqk = _dot(ub_ext, wmqk_ref[...])

    nq = (_rms(cq) * qn_ref[...]).astype(BF16)
    nkv = (_rms(ckv) * kvn_ref[...]).astype(BF16)

    q = _dot(nq, wuq_ref[...])
    kn = _dot(nkv, wk_ref[...])
    vv = _dot(nkv, wv_ref[...])

    prev_ok = t > nct
    next_ok = jnp.logical_and(t >= nct, t != nt - 1)
    row = lax.broadcasted_iota(jnp.int32, pqk.shape, 0)
    keep = jnp.logical_and(jnp.logical_or(row >= HALO, prev_ok), jnp.logical_or(row < HALO + TM, next_ok))
    pbuf[...] = jnp.where(keep, pqk, 0.0)
    cw = cw_ref[...]
    y = (pbuf[HALO - 1:HALO - 1 + TM, :] * cw[0:1] + pbuf[HALO:HALO + TM, :] * cw[1:2]
         + pbuf[HALO + 1:HALO + 1 + TM, :] * cw[2:3] + cb_ref[...])
    y = _silu(y)
    nb = MLSTM_HEADS * MLSTM_DH
    mq_out[...] = (y[:, :nb] * MLSTM_DH ** -0.5).astype(BF16)
    mk_out[...] = y[:, nb:].astype(BF16)

    mv = _dot(ub, wmv_ref[...])
    mo = _dot(ub, wmo_ref[...])

    q = q * cos_all + swap_halves(q) * sin_all
    q_out[...] = (q * ((MLA_NOPE + MLA_ROPE) ** -0.5 * LOG2E)).astype(BF16)
    rot = kr * cos + swap_halves(kr) * sin
    k_out[...] = (kn + jnp.concatenate([rot] * MLA_HEADS, axis=1)).astype(BF16)
    vlane = lax.broadcasted_iota(jnp.int32, (TM, MLA_HEADS * LANES), 1)
    is_value = ((vlane % LANES) < MLA_V) == ((vlane // LANES) % 2 == 0)
    v_out[...] = jnp.where(is_value, vv, 1.0).astype(BF16)

    zz = _dot(ub, wz_ref[...])
    mv_out[...] = mv.astype(BF16)
    mo_out[...] = _sigmoid(mo).astype(BF16)
    z_out[...] = _silu(zz).astype(BF16)

    r8 =lax.broadcasted_iota(jnp.int32, (SUBLANES, TM), 0)
    both = lambda i: jnp.where(r8 < half, per_dir[0][i], pltpu.roll(per_dir[1][i], half, axis=0)) * LOG2E
    c_rows, pm_rows, b_rows = both(0), both(1), both(2)
    for ci in range(TM // MLSTM_CHUNK):
        gr_out[ci] = c_rows[:, ci * MLSTM_CHUNK:(ci + 1) * MLSTM_CHUNK]
    stack = jnp.concatenate([c_rows, pm_rows, b_rows, jnp.zeros((LANES - 3 * SUBLANES, TM), F32)], axis=0)
    er = lax.broadcasted_iota(jnp.int32, (TM, TM), 0)
    ec = lax.broadcasted_iota(jnp.int32, (TM, TM), 1)
    eye = jnp.where(er == ec, 1.0, 0.0).astype(BF16)
    gc_out[...] = sum(_dot_nt(eye, part) for part in _split3(stack))


def _rope_tables(n_ctx, seq):
    t = jnp.arange(seq)
    inv = 1.0 / (ROPE_BASE ** (jnp.arange(ROPE_PAIRS, dtype=F32) / ROPE_PAIRS))
    ang = jnp.concatenate([(t // GRID_W)[:, None] * inv, (t % GRID_W)[:, None] * inv], axis=-1)
    cos, sin = jnp.cos(ang), jnp.sin(ang)
    j = np.arange(MLA_ROPE)
    src = (j // (2 * ROPE_PAIRS)) * ROPE_PAIRS + (j % ROPE_PAIRS)
    sign = np.where((j % (2 * ROPE_PAIRS)) < ROPE_PAIRS, -1.0, 1.0).astype(np.float32)
    cos_full = cos[:, src]
    sin_full = sin[:, src] * sign
    cos_t = jnp.ones((n_ctx + seq, LANES), F32).at[n_ctx:, MLA_NOPE:MLA_NOPE + MLA_ROPE].set(cos_full)
    sin_t = jnp.zeros((n_ctx + seq, LANES), F32).at[n_ctx:, MLA_NOPE:MLA_NOPE + MLA_ROPE].set(sin_full)
    return cos_t, sin_t


def _proj0_weights(w_in, q_norm, w_uq, kv_norm, w_ukv, conv_w, conv_b, b_i, b_f):
    o = np.cumsum([0, MLA_Q_RANK, MLA_KV_RANK, MLA_ROPE] + [MLSTM_HEADS * MLSTM_DH] * 4 + [4 * MLSTM_HEADS, D_MODEL])
    dq = MLA_NOPE + MLA_ROPE
    wkr = jnp.pad(w_in[:, o[2]:o[3]], ((0, 0), (MLA_NOPE, LANES - dq)))
    wuq = jnp.pad(w_uq.reshape(MLA_Q_RANK, MLA_HEADS, dq), ((0, 0), (0, 0), (0, LANES - dq)))
    wkv = w_ukv.reshape(MLA_KV_RANK, MLA_HEADS, MLA_NOPE + MLA_V)
    wk = jnp.pad(wkv[:, :, :MLA_NOPE], ((0, 0), (0, 0), (0, LANES - MLA_NOPE)))
    zv = jnp.zeros((MLA_KV_RANK, MLA_HEADS // 2, MLA_V), F32)
    wv = jnp.concatenate([wkv[:, 0::2, MLA_NOPE:], zv, zv, wkv[:, 1::2, MLA_NOPE:]], axis=-1)
    wv = wv.reshape(MLA_KV_RANK, MLA_HEADS * LANES)
    gbias = jnp.stack([b_i, b_f], axis=1).reshape(4 * MLSTM_HEADS, 1)
    bf = lambda a: a.astype(BF16)
    return dict(
        wcqkr=bf(jnp.concatenate([w_in[:, o[0]:o[1]], wkr], axis=1)), wckv=bf(w_in[:, o[1]:o[2]]),
        wmqk=bf(w_in[:, o[3]:o[5]]), wmv=bf(w_in[:, o[5]:o[6]]), wmo=bf(w_in[:, o[6]:o[7]]),
        wgt=bf(w_in[:, o[7]:o[8]].T), wz=bf(w_in[:, o[8]:o[9]]),
        qn=q_norm[None], wuq=bf(wuq.reshape(MLA_Q_RANK, -1)),
        kvn=kv_norm[None], wk=bf(wk.reshape(MLA_KV_RANK, -1)), wv=bf(wv),
        cw=jnp.zeros((SUBLANES, conv_w.shape[1]), F32).at[:MLSTM_CONV].set(conv_w), cb=conv_b[None], gb=gbias)


def _tile_specs(nct, nt, nb):
    tile = lambda w: pl.BlockSpec((nb, TM, w), lambda b, t: (b, t, 0))
    per = TM // HALO
    nlat = nt - nct
    src_ctx = pl.BlockSpec((nb, TM, D_MODEL), lambda b, t: (b, jnp.minimum(t, nct - 1), 0))
    src_lat = pl.BlockSpec((nb, TM, D_MODEL), lambda b, t: (b, jnp.maximum(t - nct, 0), 0))
    prev = pl.BlockSpec((nb, HALO, D_MODEL), lambda b, t: (b, jnp.maximum((t - nct) * per - 1, 0), 0))
    nxt = pl.BlockSpec((nb, HALO, D_MODEL),
                       lambda b, t: (b, jnp.clip((t - nct + 1) * per, 0, nlat * per - 1), 0))
    mod = pl.BlockSpec((nb, None, 1, D_MODEL), lambda b, t: (b, (t >= nct).astype(jnp.int32), 0, 0))
    return tile, src_ctx, src_lat, prev, nxt, mod


def _per_element(tile_kernel, shared):
    def kernel(*refs, **kw):
        for bi in range(refs[0].shape[0]):
            tile_kernel(*[r if i in shared else r.at[bi] for i, r in enumerate(refs)], **kw)
    return kernel


def _proj0(ctx, x, scale, shift, norm_g, cos_t, sin_t, w, nct):
    bsz = x.shape[0]
    ttot = ctx.shape[1] + x.shape[1]
    nt = ttot // TM
    nb = MLSTM_HEADS * MLSTM_DH
    sb = math.gcd(bsz, TILE_BATCH)
    tile, src_ctx, src_lat, prev, nxt, mod = _tile_specs(nct, nt, sb)
    tab = pl.BlockSpec((TM, LANES), lambda b, t: (t, 0))
    wnames = ["wcqkr", "wckv", "wmqk", "wmv", "wmo", "wgt", "wz",
              "qn", "wuq", "kvn", "wk", "wv", "cw", "cb", "gb"]
    ws = [w[n] for n in wnames]
    tok = lambda width, dt: jax.ShapeDtypeStruct((bsz, ttot, width), dt)
    out_shape = (tok(MLA_HEADS * LANES, BF16), tok(MLA_HEADS * LANES, BF16), tok(MLA_HEADS * LANES, BF16),
                 tok(nb, BF16), tok(nb, BF16), tok(nb, BF16),
                 jax.ShapeDtypeStruct((bsz, ttot // MLSTM_CHUNK, 2 * MLSTM_HEADS, MLSTM_CHUNK), F32), tok(LANES, F32),
                 tok(nb, BF16), tok(D_MODEL, BF16))
    out_specs = (tile(MLA_HEADS * LANES), tile(MLA_HEADS * LANES), tile(MLA_HEADS * LANES),
                 tile(nb), tile(nb), tile(nb),
                 pl.BlockSpec((sb, TM // MLSTM_CHUNK, 2 * MLSTM_HEADS, MLSTM_CHUNK), lambda b, t: (b, t, 0, 0)),
                 tile(LANES),
                 tile(nb), tile(D_MODEL))
    n_blocked, n_out = 6, len(out_shape)
    n_in = n_blocked + 3 + len(ws)
    shared = set(range(n_blocked, n_in)) | {n_in + n_out}
    return pl.pallas_call(
        functools.partial(_per_element(_proj0_kernel, shared), nct=nct, nt=nt),
        out_shape=out_shape,
        grid=(bsz // sb, nt),
        in_specs=[src_ctx, src_lat, prev, nxt, mod, mod, _full((1, D_MODEL)), tab, tab] + [_full(a.shape) for a in ws],
        out_specs=out_specs,
        scratch_shapes=[pltpu.VMEM((TM + 2 * HALO, 2 * nb), F32)],
        compiler_params=_params("parallel", "parallel"),
        name="proj0",
    )(ctx, x, x, x, scale, shift, norm_g[None], cos_t, sin_t, *ws)


def _mla_kernel(q_ref, k_ref, v_ref, o_ref, *, nct):
    t = pl.program_id(2)
    lane = lax.broadcasted_iota(jnp.int32, (TM, LANES), 1)

    def attend(nkeys):
        qk = lambda j: _dot_nt(q_ref[:, j * LANES:(j + 1) * LANES], k_ref[0:nkeys, j * LANES:(j + 1) * LANES])
        outs = []
        s_next = qk(0)
        for j in range(MLA_HPS):
            s = s_next
            if j + 1 < MLA_HPS:
                s_next = qk(j + 1)
            p = jnp.exp2(s - jnp.max(s, axis=-1, keepdims=True))
            o = _dot(p.astype(BF16), v_ref[0:nkeys, j * LANES:(j + 1) * LANES])
            outs.append(o / pltpu.roll(o, MLA_V, axis=1))
        for j in range(MLA_HPS // 2):
            o_ref[:, j * LANES:(j + 1) * LANES] = jnp.where(lane < MLA_V, outs[2 * j], outs[2 * j + 1]).astype(BF16)

    @pl.when(t < nct)
    def _():
        attend(nct * TM)

    @pl.when(t >= nct)
    def _():
        attend(k_ref.shape[0])


def _mla_attention(q, k, v, nct):
    bsz, ttot, _ = q.shape
    nt = ttot // TM
    hw = MLA_HPS * LANES
    return pl.pallas_call(
        functools.partial(_mla_kernel, nct=nct),
        out_shape=jax.ShapeDtypeStruct((bsz, ttot, MLA_HEADS * MLA_V), BF16),
        grid=(bsz, MLA_HEADS // MLA_HPS, nt),
        in_specs=[pl.BlockSpec((None, TM, hw), lambda b, p, t: (b, t, p)),
                  pl.BlockSpec((None, ttot, hw), lambda b, p, t: (b, 0, p)),
                  pl.BlockSpec((None, ttot, hw), lambda b, p, t: (b, 0, p))],
        out_specs=pl.BlockSpec((None, TM, hw // 2), lambda b, p, t: (b, t, p)),
        compiler_params=_params("parallel", "parallel", "parallel"),
        name="mla_attention",
    )(q, k, v)


def _scan_block(j, nct, nt):
    return jnp.where(j < nct, nct - 1 - j, nt - 1 - (j - nct))


def _mlstm_kernel(qf_ref, kf_ref, vf_ref, grf_ref, gcf_ref, qr_ref, kr_ref, vr_ref, grr_ref, gcr_ref,
                  hf_ref, hr_ref, c_scr, m_scr):
    @pl.when(pl.program_id(1) == 0)
    def _():
        c_scr[...] = jnp.zeros_like(c_scr)
        m_scr[...] = jnp.zeros_like(m_scr)

    L = MLSTM_CHUNK
    nchunk = TM // L
    nh = MLSTM_HEADS
    nchain = 2 * nh
    row = lax.broadcasted_iota(jnp.int32, (L, L), 0)
    col = lax.broadcasted_iota(jnp.int32, (L, L), 1)
    ones = jnp.ones((L, MLSTM_DH), BF16)
    wide = lambda a: jnp.broadcast_to(a, (L, MLSTM_DH))
    twice = lambda a: jnp.concatenate([a, a], axis=1)
    refs = ((qf_ref, kf_ref, vf_ref, grf_ref, gcf_ref, hf_ref), (qr_ref, kr_ref, vr_ref, grr_ref, gcr_ref, hr_ref))
    nbatch = qf_ref.shape[0]

    def load(ci):
        cs = []
        for bi, d in [(bi, d) for bi in range(nbatch) for d in range(2)]:
            q_ref, k_ref, v_ref, gr_ref, gc_ref, o_ref = (r.at[bi] for r in refs[d])
            cd = ci if d == 0 else nchunk - 1 - ci
            rs = slice(cd * L, (cd + 1) * L)
            g_rows = gr_ref[cd]
            g_cols = gc_ref[rs, :]
            last = L - 1 if d == 0 else 0
            for hd in range(nh):
                ch = d * nh + hd
                hs = slice(hd * MLSTM_DH, (hd + 1) * MLSTM_DH)
                cs.append(dict(
                    rs=rs, hs=hs, o_ref=o_ref, causal=(col <= row) if d == 0 else (col >= row),
                    c_row=g_rows[ch:ch + 1], c_col=g_cols[:, ch:ch + 1],
                    pm_col=g_cols[:, nchain + ch:nchain + ch + 1], b_col=g_cols[:, 2 * nchain + ch:2 * nchain + ch + 1],
                    pm_last=g_cols[last:last + 1, nchain + ch:nchain + ch + 1],
                    b_last=g_cols[last:last + 1, 2 * nchain + ch:2 * nchain + ch + 1],
                    q=q_ref[rs, hs], k=k_ref[rs, hs],
                    v_ext=jnp.concatenate([v_ref[rs, hs], ones], axis=1)))
        return cs

    def prep_scores(cs):
        for c in cs:
            c["s"] = _dot_nt(c["q"], c["k"])

    def prep_weights(cs):
        for c in cs:
            c["pm_w"] = wide(c["pm_col"])
            c["b_w"] = wide(c["b_col"])
            c["w"] = jnp.exp2(jnp.where(c["causal"], c["c_row"] - c["pm_w"][:, :L], -jnp.inf))
            c["kwf"] = jnp.exp2(wide(c["c_col"] - c["pm_last"]))

    def prep_operands(cs):
        for c in cs:
            c["p"] = (c["s"] * c["w"]).astype(BF16)
            c["kw"] = (c["kwf"] * c["k"].astype(F32)).astype(BF16)

    def prep_products(cs):
        for c in cs:
            c["o1"] = _dot(c["p"], c["v_ext"])
            c["dc"] = _dot_tn(c["kw"], c["v_ext"])

    def scan_read(cs):
        for ch, c in enumerate(cs):
            c["qc"] = _dot(c["q"], state[ch].astype(BF16))

    def scan_factors(cs):
        for ch, c in enumerate(cs):
            g_w = jnp.maximum(m[ch], c["pm_w"])
            c["e_intra"] = jnp.exp2(c["pm_w"] - g_w)
            c["e_inter"] = jnp.exp2(m[ch] - g_w)
            c["floor"] = jnp.exp2(-(c["b_w"] + g_w))
            g_last = jnp.maximum(m[ch], c["pm_last"])
            c["keep"] = jnp.exp2(m[ch] - g_last)
            c["gain"] = jnp.exp2(c["pm_last"] - g_last)
            c["m_new"] = c["b_last"] + g_last

    def scan_update(cs):
        for ch, c in enumerate(cs):
            o = twice(c["e_intra"]) * c["o1"] + twice(c["e_inter"]) * c["qc"]
            h = o[:, :MLSTM_DH] / jnp.maximum(jnp.abs(o[:, MLSTM_DH:]), c["floor"])
            c["o_ref"][c["rs"], c["hs"]] = h.astype(BF16)
            state[ch] = c["keep"] * state[ch] + c["gain"] * c["dc"]
            m[ch] = c["m_new"]

    nstate = nbatch * nchain
    state = [c_scr[ch] for ch in range(nstate)]
    m = [m_scr[ch, 0:1, 0:1] for ch in range(nstate)]
    chunks = [load(ci) for ci in range(nchunk)]
    everything = [c for cs in chunks for c in cs]
    for stage in (prep_scores, prep_weights, prep_operands):
        stage(everything)
    prep_products(chunks[0])
    for i, cs in enumerate(chunks):
        scan_read(cs)
        if i + 1 < nchunk:
            prep_products(chunks[i + 1])
        scan_factors(cs)
        scan_update(cs)
    for ch in range(nstate):
        c_scr[ch] = state[ch]
        m_scr[ch] = jnp.broadcast_to(m[ch], m_scr.shape[1:])


def _mlstm(mq, mk, mv, g_rows, g_cols, nct):
    bsz, ttot, nb = mq.shape
    nt = ttot // TM
    per = TM // MLSTM_CHUNK
    sb = math.gcd(bsz, MLSTM_BATCH)
    fwd = lambda w: pl.BlockSpec((sb, TM, w), lambda b, j: (b, j, 0))
    rev = lambda w: pl.BlockSpec((sb, TM, w), lambda b, j: (b, _scan_block(j, nct, nt), 0))
    gshape = (sb, per, 2 * MLSTM_HEADS, MLSTM_CHUNK)
    gfwd = pl.BlockSpec(gshape, lambda b, j: (b, j, 0, 0))
    grev = pl.BlockSpec(gshape, lambda b, j: (b, _scan_block(j, nct, nt), 0, 0))
    nstate = sb * 2 * MLSTM_HEADS
    return pl.pallas_call(
        _mlstm_kernel,
        out_shape=(jax.ShapeDtypeStruct((bsz, ttot, nb), BF16),) * 2,
        grid=(bsz // sb, nt),
        in_specs=[fwd(nb), fwd(nb), fwd(nb), gfwd, fwd(LANES), rev(nb), rev(nb), rev(nb), grev, rev(LANES)],
        out_specs=(fwd(nb), rev(nb)),
        scratch_shapes=[pltpu.VMEM((nstate, MLSTM_DH, 2 * MLSTM_DH), F32),
                        pltpu.VMEM((nstate, SUBLANES, LANES), F32)],
        compiler_params=_params("parallel", "arbitrary"),
        name="mlstm_scan",
    )(mq, mk, mv, g_rows, g_cols, mq, mk, mv, g_rows, g_cols)


def _head_rms(x, n_heads):
    w = x.shape[1] // n_heads
    return jnp.concatenate([_rms(x[:, i * w:(i + 1) * w]) for i in range(n_heads)], axis=1)


def _mid_kernel(ctx_ref, lat_ref, gate_ref, a_ref, hf_ref, hr_ref, mo_ref, z0_ref, hn_ref, wout_ref,
                sc_ref, sh_ref, g_ref, wq_ref, wk_ref, wv_ref, wga_ref, wnq_ref, wnk_ref, wnv_ref, wz_ref,
                wgate_ref, bgate_ref,
                h_out, gq_out, gk_out, gv_out, bc_out, nq_out, nk_out, nv_out, z_out, *, nct):
    f32 = lambda r: r[...].astype(F32)
    hm = _head_rms(f32(mo_ref) * (f32(hf_ref) + f32(hr_ref)), MLSTM_HEADS) * hn_ref[...]
    cat = (jnp.concatenate([f32(a_ref), hm], axis=1) * f32(z0_ref)).astype(BF16)
    h0 = jnp.where(pl.program_id(1) < nct, ctx_ref[...], lat_ref[...])
    h = h0 + gate_ref[...] * _dot(cat, wout_ref[...])
    h_out[...] = h

    ub = _modnorm(h, g_ref[...], sc_ref[...], sh_ref[...]).astype(BF16)
    ga = _dot(ub, wga_ref[...]).astype(BF16)
    gate_pre = _dot(ga, wgate_ref[...])
    gq_out[...] = (_dot(ub, wq_ref[...]) * GLA_DK ** -0.5).astype(BF16)
    gk_out[...] = _dot(ub, wk_ref[...]).astype(BF16)
    gv_out[...] = _dot(ub, wv_ref[...]).astype(BF16)

    lg = _log_sigmoid(gate_pre + bgate_ref[...]) * (1.0 / GLA_TAU)
    hi, mid, lo = _split3(lg)
    nq_out[...] = (_dot(ub, wnq_ref[...]) * (NA_DH ** -0.5 * LOG2E)).astype(BF16)
    nk_out[...] = _dot(ub, wnk_ref[...]).astype(BF16)
    nv_out[...] = _dot(ub, wnv_ref[...]).astype(BF16)
    row = lax.broadcasted_iota(jnp.int32, (TM, TM), 0)
    col = lax.broadcasted_iota(jnp.int32, (TM, TM), 1)
    same = (row // CHUNK) == (col // CHUNK)
    nk = GLA_HEADS * GLA_DK
    for d in range(2):
        tri = jnp.where(jnp.logical_and(same, (col <= row) if d == 0 else (col >= row)), 1.0, 0.0).astype(BF16)
        cols = slice(d * nk, (d + 1) * nk)
        bc_out[:, cols] = (_dot(tri, hi[:, cols]) + _dot(tri, mid[:, cols]) + _dot(tri, lo[:, cols])) * LOG2E
    z_out[...] = _silu(_dot(ub, wz_ref[...])).astype(BF16)


def _proj1_weights(w_in, w_gate, b_gate):
    nk = GLA_HEADS * GLA_DK
    nc = GLA_HEADS * GLA_DV
    nd = NA_HEADS * NA_DH
    o = np.cumsum([0, nk, nk, nc, 2 * GLA_GATE_RANK, nd, nd, nd, nc + nd])
    r = GLA_GATE_RANK
    wgate = jnp.zeros((2 * r, 2 * nk), F32).at[:r, :nk].set(w_gate[0]).at[r:, nk:].set(w_gate[1])
    bf = lambda a: a.astype(BF16)
    names = ["wq", "wk", "wv", "wga", "wnq", "wnk", "wnv", "wz"]
    w = {n: bf(w_in[:, o[i]:o[i + 1]]) for i, n in enumerate(names)}
    w["wgate"] = bf(wgate)
    w["bgate"] = b_gate.reshape(1, 2 * nk)
    return w


def _mid(ctx, x, gate0, a, hf, hr, mo, z0, h_norm, w_out, scale, shift, norm_g, w, nct):
    bsz = x.shape[0]
    ttot = ctx.shape[1] + x.shape[1]
    nt = ttot // TM
    sb = math.gcd(bsz, TILE_BATCH)
    tile, src_ctx, src_lat, _, _, mod = _tile_specs(nct, nt, sb)
    nb = MLSTM_HEADS * MLSTM_DH
    nk = GLA_HEADS * GLA_DK
    nc = GLA_HEADS * GLA_DV
    nd = NA_HEADS * NA_DH
    wnames = ["wq", "wk", "wv", "wga", "wnq", "wnk", "wnv", "wz", "wgate", "bgate"]
    ws = [w[n] for n in wnames]
    tok = lambda width, dt: jax.ShapeDtypeStruct((bsz, ttot, width), dt)
    shared = {8, 9} | set(range(12, 13 + len(ws)))
    return pl.pallas_call(
        functools.partial(_per_element(_mid_kernel, shared), nct=nct),
        out_shape=(tok(D_MODEL, F32), tok(nk, BF16), tok(nk, BF16), tok(nc, BF16), tok(2 * nk, F32),
                   tok(nd, BF16), tok(nd, BF16), tok(nd, BF16), tok(nc + nd, BF16)),
        grid=(bsz // sb, nt),
        in_specs=[src_ctx, src_lat, mod, tile(MLA_HEADS * MLA_V), tile(nb), tile(nb), tile(nb), tile(D_MODEL),
                  _full((1, nb)), _full(w_out.shape), mod, mod, _full((1, D_MODEL))] + [_full(a.shape) for a in ws],
        out_specs=(tile(D_MODEL), tile(nk), tile(nk), tile(nc), tile(2 * nk), tile(nd), tile(nd), tile(nd),
                   tile(nc + nd)),
        compiler_params=_params("parallel", "parallel"),
        name="out0_proj1",
    )(ctx, x, gate0, a, hf, hr, mo, z0, h_norm[None], w_out.astype(BF16), scale, shift, norm_g[None], *ws)


def _gla_kernel(qf_ref, kf_ref, vf_ref, bf_ref, qr_ref, kr_ref, vr_ref, br_ref, of_ref, or_ref, s_scr):
    @pl.when(pl.program_id(1) == 0)
    def _():
        s_scr[...] = jnp.zeros_like(s_scr)

    L = CHUNK
    nchunk = TM // L
    npair = GLA_HEADS // 2
    lane = lax.broadcasted_iota(jnp.int32, (L, LANES), 1)
    rowv = lax.broadcasted_iota(jnp.int32, (L, LANES), 0)
    srow = lax.broadcasted_iota(jnp.int32, (LANES, 2 * GLA_DV), 0)
    scol = lax.broadcasted_iota(jnp.int32, (LANES, 2 * GLA_DV), 1)
    own_block = (srow < GLA_DK) == (scol < GLA_DV)
    vcol = lax.broadcasted_iota(jnp.int32, (L, 2 * GLA_DV), 1)
    e_row = lax.broadcasted_iota(jnp.int32, (LANES, LANES), 0)
    e_col = lax.broadcasted_iota(jnp.int32, (LANES, LANES), 1)
    head_ones = jnp.where((e_row < GLA_DK) == (e_col < GLA_DK), 1.0, 0.0).astype(BF16)
    leaf_idx = lax.broadcasted_iota(jnp.int32, (GLA_LEAF * GLA_LEAF, LANES), 0)
    leaf_lane = lax.broadcasted_iota(jnp.int32, (GLA_LEAF * GLA_LEAF, LANES), 1) % L
    leaf_s, leaf_t = leaf_idx // GLA_LEAF, leaf_idx % GLA_LEAF
    rep = lambda a: jnp.concatenate(
        [jnp.broadcast_to(a[s:s + 1], (GLA_LEAF, a.shape[1])) for s in range(GLA_LEAF)], axis=0)
    til = lambda a: jnp.concatenate([a] * GLA_LEAF, axis=0)

    def chunk_step(ci, carry):
        cs = []
        for bi, d in [(bi, d) for bi in range(qf_ref.shape[0]) for d in range(2)]:
            q_ref, k_ref, v_ref, b_ref, o_ref = (
                r.at[bi] for r in ((qf_ref, kf_ref, vf_ref, bf_ref, of_ref), (qr_ref, kr_ref, vr_ref, br_ref, or_ref))[d])
            cd = ci if d == 0 else nchunk - 1 - ci
            base = pl.multiple_of(cd * L, L)
            for p in range(npair):
                ls = slice(p * LANES, (p + 1) * LANES)
                bc = b_ref[pl.ds(base, L), ls]
                cs.append(dict(
                    d=d, p=p, bi=bi, base=base, o_ref=o_ref, bc=bc,
                    tau_v=rowv if d == 0 else L - 1 - rowv,
                    tau_s=(lane % L) if d == 0 else L - 1 - (lane % L),
                    q=q_ref[pl.ds(base, L), ls].astype(F32),
                    k=k_ref[pl.ds(base, L), ls].astype(F32),
                    v=v_ref[pl.ds(base, L), p * 2 * GLA_DV:(p + 1) * 2 * GLA_DV],
                    b_last=bc[L - 1:L] if d == 0 else bc[0:1],
                    state=s_scr[bi, d, p]))

        for c in cs:
            c["o_pair"] = _dot((c["q"] * jnp.exp2(c["bc"])).astype(BF16), c["state"].astype(BF16))

        for c in cs:
            c["att"] = jnp.zeros((L, LANES), F32)
        bs = L // 2
        while bs >= GLA_LEAF:
            for c in cs:
                d, bc, tau_v = c["d"], c["bc"], c["tau_v"]
                later = (tau_v // bs) % 2 == 1
                ref_b = None
                for blk in range(L // (2 * bs)):
                    tau_ref = blk * 2 * bs + bs - 1
                    idx = tau_ref if d == 0 else L - 1 - tau_ref
                    r = jnp.broadcast_to(bc[idx:idx + 1], (L, LANES))
                    ref_b = r if ref_b is None else jnp.where(tau_v // (2 * bs) == blk, r, ref_b)
                c["qs"] = jnp.where(later, c["q"] * jnp.exp2(jnp.where(later, bc - ref_b, 0.0)), 0.0).astype(BF16)
                ks = jnp.where(later, 0.0, c["k"] * jnp.exp2(jnp.where(later, 0.0, ref_b - bc)))
                c["ks2"] = jnp.concatenate([jnp.where(lane < GLA_DK, ks, 0.0), jnp.where(lane < GLA_DK, 0.0, ks)],
                                           axis=0).astype(BF16)
            for c in cs:
                same_parent = (c["tau_v"] // (2 * bs)) == (c["tau_s"] // (2 * bs))
                c["att"] = c["att"] + jnp.where(same_parent, _dot_nt(c["qs"], c["ks2"]), 0.0)
            bs //= 2

        for c in cs:
            c["leaf"] = []
        for blk in range(L // GLA_LEAF):
            rs = slice(blk * GLA_LEAF, (blk + 1) * GLA_LEAF)
            for c in cs:
                qb, bb, kb = c["q"][rs], c["bc"][rs], c["k"][rs]
                ok = (leaf_t >= leaf_s) if c["d"] == 0 else (leaf_t <= leaf_s)
                w = jnp.where(ok, til(qb) * jnp.exp2(til(bb) - rep(bb)) * rep(kb), 0.0)
                c["a"] = _dot(w.astype(BF16), head_ones)
            for c in cs:
                a = jnp.where(leaf_lane == leaf_s + blk * GLA_LEAF, c["a"], 0.0)
                acc = a[0:GLA_LEAF]
                for s in range(1, GLA_LEAF):
                    acc = acc + a[s * GLA_LEAF:(s + 1) * GLA_LEAF]
                c["leaf"].append(acc)

        for c in cs:
            v = c["v"]
            c["att"] = (c["att"] + jnp.concatenate(c["leaf"], axis=0)).astype(BF16)
            c["v_blk"] = jnp.concatenate([jnp.where(vcol < GLA_DV, v, jnp.zeros_like(v)),
                                          jnp.where(vcol < GLA_DV, jnp.zeros_like(v), v)], axis=0)
            c["ke"] = (c["k"] * jnp.exp2(c["b_last"] - c["bc"])).astype(BF16)
            c["decay_col"] = jnp.sum(
                jnp.where(e_row == e_col, jnp.broadcast_to(jnp.exp2(c["b_last"]), (LANES, LANES)), 0.0),
                axis=1, keepdims=True)
        for c in cs:
            c["o"] = c["o_pair"] + _dot(c["att"], c["v_blk"])
            c["ds"] = _dot_tn(c["ke"], c["v"])
        for c in cs:
            p = c["p"]
            c["o_ref"][pl.ds(c["base"], L), p * 2 * GLA_DV:(p + 1) * 2 * GLA_DV] = c["o"].astype(BF16)
            s_scr[c["bi"], c["d"], p] = c["decay_col"] * c["state"] + jnp.where(own_block, c["ds"], 0.0)
        return carry

    lax.fori_loop(0, nchunk, chunk_step, 0)


def _gla(gq, gk, gv, bc, nct):
    bsz, ttot, nk = gq.shape
    nc = gv.shape[2]
    nt = ttot // TM
    sb = math.gcd(bsz, GLA_BATCH)
    fwd = lambda w: pl.BlockSpec((sb, TM, w), lambda b, j: (b, j, 0))
    rev = lambda w: pl.BlockSpec((sb, TM, w), lambda b, j: (b, _scan_block(j, nct, nt), 0))
    bfwd = pl.BlockSpec((sb, TM, nk), lambda b, j: (b, j, 0))
    brev = pl.BlockSpec((sb, TM, nk), lambda b, j: (b, _scan_block(j, nct, nt), 1))
    return pl.pallas_call(
        _gla_kernel,
        out_shape=(jax.ShapeDtypeStruct((bsz, ttot, nc), BF16),) * 2,
        grid=(bsz // sb, nt),
        in_specs=[fwd(nk), fwd(nk), fwd(nc), bfwd, rev(nk), rev(nk), rev(nc), brev],
        out_specs=(fwd(nc), rev(nc)),
        scratch_shapes=[pltpu.VMEM((sb, 2, GLA_HEADS // 2, LANES, 2 * GLA_DV), F32)],
        compiler_params=_params("parallel", "arbitrary"),
        name="gla_scan",
    )(gq, gk, gv, bc, gq, gk, gv, bc)


def _na_bias_tables(rpb, rows):
    kh = NA_KH
    c = np.arange(GRID_W)
    cs = np.clip(c - NA_KW // 2, 0, GRID_W - NA_KW)
    col_ok = (c[None, :] >= cs[:, None]) & (c[None, :] < cs[:, None] + NA_KW)
    pad = GRID_W - NA_KW
    rpb_pad = jnp.pad(rpb.astype(F32) * LOG2E, ((0, 0), (0, 0), (pad, pad)))
    t1 = jnp.stack([rpb_pad[:, :, GRID_W - 1 - cq:2 * GRID_W - 1 - cq] for cq in range(GRID_W)], axis=2)
    t1 = jnp.where(col_ok, t1, NEG_BIG)
    masked = jnp.full((NA_HEADS, GRID_W, GRID_W), NEG_BIG, F32)
    per_row = []
    for r0 in (0, NA_QROWS, rows - NA_QROWS):
        kb = int(np.clip(r0 - kh // 2, 0, rows - NA_KROWS))
        for qrow in range(r0, r0 + NA_QROWS):
            ws = int(np.clip(qrow - kh // 2, 0, rows - kh))
            per_row.append(jnp.concatenate(
                [t1[:, krow - qrow + NA_KH - 1] if ws <= krow < ws + kh else masked
                 for krow in range(kb, kb + NA_KROWS)], axis=-1))
    return jnp.concatenate(per_row, axis=1)


def _na_out1_kernel(q_ref, k_ref, v_ref, bias_ref, h_ref, gate_ref, of_ref, or_ref, z_ref, gn_ref, w_ref, fn_ref,
                    o_ref, *, n_ctx, rows):
    rb = pl.program_id(1)
    nq = NA_QROWS * GRID_W
    nk = NA_KROWS * GRID_W
    kb = jnp.clip(rb * NA_QROWS - NA_KH // 2, 0, rows - NA_KROWS)
    start = pl.multiple_of(n_ctx + kb * GRID_W, GRID_W)
    lane = lax.broadcasted_iota(jnp.int32, (nq, LANES), 1)
    def scores(j):
        ls = slice((j // 2) * LANES, (j // 2 + 1) * LANES)
        q = q_ref[:, ls]
        qj = jnp.where((lane < NA_DH) == (j % 2 == 0), q, jnp.zeros_like(q))
        return _dot_nt(qj, k_ref[pl.ds(start, nk), ls]) + bias_ref[j], _dot_nt(qj, k_ref[0:n_ctx, ls])

    outs = []
    nxt = scores(0)
    for j in range(NA_HEADS):
        s_loc, s_ctx = nxt
        if j + 1 < NA_HEADS:
            nxt = scores(j + 1)
        ls = slice((j // 2) * LANES, (j // 2 + 1) * LANES)
        m = jnp.maximum(jnp.max(s_loc, axis=-1, keepdims=True), jnp.max(s_ctx, axis=-1, keepdims=True))
        p_loc = jnp.exp2(s_loc - m)
        p_ctx = jnp.exp2(s_ctx - m)
        den = jnp.sum(p_loc, axis=-1, keepdims=True) + jnp.sum(p_ctx, axis=-1, keepdims=True)
        outs.append((_dot(p_loc.astype(BF16), v_ref[pl.ds(start, nk), ls])
                     + _dot(p_ctx.astype(BF16), v_ref[0:n_ctx, ls])) / den)
    na = [jnp.where(lane < NA_DH, outs[2 * i], outs[2 * i + 1]) for i in range(NA_HEADS // 2)]

    f32 = lambda r: r[...].astype(F32)
    g = _head_rms(f32(of_ref) + f32(or_ref), GLA_HEADS) * gn_ref[...]
    cat = (jnp.concatenate([g] + na, axis=1) * f32(z_ref)).astype(BF16)
    h = h_ref[...] + gate_ref[...] * _dot(cat, w_ref[...])
    o_ref[...] = _rms(h) * fn_ref[...]


def _na_out1(nq, nk, nv, bias, h, gate, of, orv, z, gla_norm, w_out, final_norm, n_ctx, rows):
    bsz, ttot, nd = nq.shape
    nrb = rows // NA_QROWS
    qrows = NA_QROWS * GRID_W
    krows = NA_KROWS * GRID_W
    assert qrows == TM
    nct = n_ctx // TM
    nc = GLA_HEADS * GLA_DV

    def variant(rb):
        return jnp.where(rb == 0, 0, jnp.where(rb == nrb - 1, 2, 1))

    lat = lambda w: pl.BlockSpec((None, TM, w), lambda b, rb: (b, rb + nct, 0))
    whole = lambda w: pl.BlockSpec((None, ttot, w), lambda b, rb: (b, 0, 0))
    return pl.pallas_call(
        functools.partial(_na_out1_kernel, n_ctx=n_ctx, rows=rows),
        out_shape=jax.ShapeDtypeStruct((bsz, rows * GRID_W, D_MODEL), F32),
        grid=(bsz, nrb),
        in_specs=[lat(nd), whole(nd), whole(nd),
                  pl.BlockSpec((NA_HEADS, qrows, krows), lambda b, rb: (0, variant(rb), 0)),
                  lat(D_MODEL), pl.BlockSpec((None, None, 1, D_MODEL), lambda b, rb: (b, 1, 0, 0)),
                  lat(nc), lat(nc), lat(nc + nd), _full((1, nc)), _full(w_out.shape), _full((1, D_MODEL))],
        out_specs=pl.BlockSpec((None, TM, D_MODEL), lambda b, rb: (b, rb, 0)),
        compiler_params=_params("parallel", "parallel"),
        name="na_out1",
    )(nq, nk, nv, bias, h, gate, of, orv, z, gla_norm[None], w_out.astype(BF16), final_norm[None])


def kernel(x, c, ctx, c_ctx, l0_norm, l0_w_mod, l0_b_mod, l0_w_in, l0_mla_q_norm, l0_mla_w_uq, l0_mla_kv_norm, l0_mla_w_ukv, l0_mlstm_conv_w, l0_mlstm_conv_b, l0_mlstm_b_i, l0_mlstm_b_f, l0_mlstm_norm, l0_w_out, l1_norm, l1_w_mod, l1_b_mod, l1_w_in, l1_gla_w_gate, l1_gla_b_gate, l1_gla_norm, l1_na_rpb, l1_w_out, final_norm):
    bsz, seq, d = x.shape
    n_ctx = ctx.shape[1]
    rows = seq // GRID_W
    assert d == D_MODEL and seq % TM == 0 and n_ctx == TM
    assert rows % NA_QROWS == 0 and rows >= NA_KROWS and rows // NA_QROWS >= 3
    nct = n_ctx // TM

    (shift0, scale0, gate0), (shift, scale, gate) = _mod_vectors(
        c, c_ctx, ((l0_w_mod, l0_b_mod), (l1_w_mod, l1_b_mod)))

    cos_t, sin_t = _rope_tables(n_ctx, seq)
    w0 = _proj0_weights(l0_w_in, l0_mla_q_norm, l0_mla_w_uq, l0_mla_kv_norm, l0_mla_w_ukv,
                        l0_mlstm_conv_w, l0_mlstm_conv_b, l0_mlstm_b_i, l0_mlstm_b_f)
    q, k, v, mq, mk, mv, g_rows, g_cols, mo, z0 = _proj0(ctx, x, scale0, shift0, l0_norm, cos_t, sin_t, w0, nct)
    a = _mla_attention(q, k, v, nct)
    hf, hr = _mlstm(mq, mk, mv, g_rows, g_cols, nct)

    w1 = _proj1_weights(l1_w_in, l1_gla_w_gate, l1_gla_b_gate)
    h, gq, gk, gv, bc, nq, nk, nv, z = _mid(ctx, x, gate0, a, hf, hr, mo, z0, l0_mlstm_norm, l0_w_out,
                                            scale, shift, l1_norm, w1, nct)
    of, orv = _gla(gq, gk, gv, bc, nct)
    return _na_out1(nq, nk, nv, _na_bias_tables(l1_na_rpb, rows), h, gate, of, orv, z,
                    l1_gla_norm, l1_w_out, final_norm, n_ctx, rows)
```

```python
import functools
import math

import jax
import jax.numpy as jnp
import numpy as np
from jax import lax
from jax.experimental import pallas as pl
from jax.experimental.pallas import tpu as pltpu

F32 = jnp.float32
BF16 = jnp.bfloat16

D_MODEL = 1024
GRID_W = 64
EPS = 1e-6
ROPE_BASE = 10000.0

MLA_HEADS = 8
MLA_Q_RANK = 384
MLA_KV_RANK = 256
MLA_NOPE = 64
MLA_ROPE = 32
MLA_V = 64
ROPE_PAIRS = MLA_ROPE // 4
MLSTM_HEADS = 4
MLSTM_DH = 128
MLSTM_CONV = 3
GLA_HEADS = 4
GLA_DK = 64
GLA_DV = 128
GLA_GATE_RANK = 16
GLA_TAU = 16.0
NA_HEADS = 8
NA_DH = 64
NA_KH = 8
NA_KW = 16

CHUNK = 64
MLSTM_CHUNK = 128
TM = 256
LANES = 128
SUBLANES = 8
HALO = SUBLANES
NA_QROWS = 4
NA_KROWS = NA_QROWS + NA_KH - 1
GLA_LEAF = 8
MLA_HPS = 4
TILE_BATCH = 2
MLSTM_BATCH = 2
GLA_BATCH = 8
VMEM_LIMIT = 56 * 1024 * 1024
NEG_BIG = -1e30
LOG2E = 1.4426950408889634


def _dot(a, b):
    return jnp.dot(a, b, preferred_element_type=F32)


def _dot_nt(a, b):
    return lax.dot_general(a, b, (((1,), (1,)), ((), ())), preferred_element_type=F32)


def _dot_tn(a, b):
    return lax.dot_general(a, b, (((0,), (0,)), ((), ())), preferred_element_type=F32)


def _rms(x):
    return x * lax.rsqrt(jnp.mean(x * x, axis=-1, keepdims=True) + EPS)


def _sigmoid(x):
    return 1.0 / (1.0 + jnp.exp(-x))


def _silu(x):
    return x * _sigmoid(x)


def _log_sigmoid(x):
    return jnp.minimum(x, 0.0) - jnp.log(1.0 + jnp.exp(-jnp.abs(x)))


def _params(*sem):
    return pltpu.CompilerParams(dimension_semantics=sem, vmem_limit_bytes=VMEM_LIMIT)


def _full(shape):
    nd = len(shape)
    return pl.BlockSpec(shape, lambda *_: (0,) * nd)


def _mod_kernel(c_ref, w0_ref, b0_ref, w1_ref, b1_ref, o0_ref, o1_ref):
    sc = _silu(c_ref[...]).astype(BF16)
    o0_ref[...] = _dot(sc, w0_ref[...].astype(BF16)) + b0_ref[...]
    o1_ref[...] = _dot(sc, w1_ref[...].astype(BF16)) + b1_ref[...]


def _mod_vectors(c, c_ctx, mods):
    bsz, d = c.shape
    rows = -(-(bsz + 1) // SUBLANES) * SUBLANES
    cc = jnp.zeros((rows, d), F32).at[:bsz].set(c).at[bsz].set(c_ctx)
    (w0, b0), (w1, b1) = mods
    n = w0.shape[1]
    wspec = pl.BlockSpec((d, d), lambda j: (0, j))
    vspec = pl.BlockSpec((1, d), lambda j: (0, j))
    ospec = pl.BlockSpec((rows, d), lambda j: (0, j))
    per_layer = pl.pallas_call(
        _mod_kernel,
        out_shape=(jax.ShapeDtypeStruct((rows, n), F32),) * 2,
        grid=(n // d,),
        in_specs=[_full((rows, d)), wspec, vspec, wspec, vspec],
        out_specs=(ospec, ospec),
        compiler_params=_params("parallel"),
        name="modulation",
    )(cc, w0, b0[None], w1, b1[None])
    out = []
    for mod in per_layer:
        vecs = []
        for part in jnp.split(mod, 3, axis=-1):
            ctx_v = jnp.broadcast_to(part[bsz][None], (bsz, d))
            vecs.append(jnp.stack([ctx_v, part[:bsz]], axis=1)[:, :, None, :])
        out.append(vecs)
    return out


def _modnorm(x, g, scale, shift):
    return _rms(x) * g * (1.0 + scale) + shift


def _seg_scan(x, reverse, op, fill):
    width = x.shape[1]
    pos = lax.broadcasted_iota(jnp.int32, x.shape, 1) % MLSTM_CHUNK
    k = 1
    while k < MLSTM_CHUNK:
        if reverse:
            x = op(x, jnp.where(pos < MLSTM_CHUNK - k, pltpu.roll(x, width - k, axis=1), fill))
        else:
            x = op(x, jnp.where(pos >= k, pltpu.roll(x, k, axis=1), fill))
        k *= 2
    return x


def _split3(x):
    hi = x.astype(BF16)
    r1 = x - hi.astype(F32)
    mid = r1.astype(BF16)
    return hi, mid, (r1 - mid.astype(F32)).astype(BF16)


def _proj0_kernel(ctx_ref, lat_ref, hp_ref, hn_ref, sc_ref, sh_ref, g_ref, cos_ref, sin_ref,
                  wcqkr_ref, wckv_ref, wmqk_ref, wmv_ref, wmo_ref, wgt_ref, wz_ref,
                  qn_ref, wuq_ref, kvn_ref, wk_ref, wv_ref, cw_ref, cb_ref, gb_ref,
                  q_out, k_out, v_out, mq_out, mk_out, mv_out, gr_out, gc_out, mo_out, z_out,
                  pbuf, *, nct, nt):
    t = pl.program_id(1)
    h = jnp.where(t < nct, ctx_ref[...], lat_ref[...])
    x = jnp.concatenate([hp_ref[...], h, hn_ref[...]], axis=0)
    ub_ext = _modnorm(x, g_ref[...], sc_ref[...], sh_ref[...]).astype(BF16)
    ub = ub_ext[HALO:HALO + TM]

    gt = _dot_nt(wgt_ref[...], ub) + gb_ref[...]
    half = MLSTM_HEADS
    per_dir = []
    for d in range(2):
        gd = gt[d * SUBLANES:(d + 1) * SUBLANES]
        bcum = _seg_scan(_log_sigmoid(gd), d == 1, jnp.add, 0.0)
        b_top = pltpu.roll(bcum, half, axis=0)
        c8 = gd - b_top
        pm8 = _seg_scan(c8, d == 1, jnp.maximum, -jnp.inf)
        per_dir.append((c8, pm8, b_top))

    cos = cos_ref[...]
    sin = sin_ref[...]
    cos_all = jnp.concatenate([cos] * MLA_HEADS, axis=1)
    sin_all = jnp.concatenate([sin] * MLA_HEADS, axis=1)

    def swap_halves(a):
        lane = lax.broadcasted_iota(jnp.int32, a.shape, 1)
        first = lane % (2 * ROPE_PAIRS) < ROPE_PAIRS
        return jnp.where(first, pltpu.roll(a, a.shape[1] - ROPE_PAIRS, axis=1), pltpu.roll(a, ROPE_PAIRS, axis=1))

    cqkr = _dot(ub, wcqkr_ref[...])
    cq = cqkr[:, :MLA_Q_RANK]
    kr = cqkr[:, MLA_Q_RANK:]
    ckv = _dot(ub, wckv_ref[...])
    pqk = _dot(ub_ext, wmqk_ref[...])

    nq = (_rms(cq) * qn_ref[...]).astype(BF16)
    nkv = (_rms(ckv) * kvn_ref[...]).astype(BF16)

    q = _dot(nq, wuq_ref[...])
    kn = _dot(nkv, wk_ref[...])
    vv = _dot(nkv, wv_ref[...])

    prev_ok = t > nct
    next_ok = jnp.logical_and(t >= nct, t != nt - 1)
    row = lax.broadcasted_iota(jnp.int32, pqk.shape, 0)
    keep = jnp.logical_and(jnp.logical_or(row >= HALO, prev_ok), jnp.logical_or(row < HALO + TM, next_ok))
    pbuf[...] = jnp.where(keep, pqk, 0.0)
    cw = cw_ref[...]
    y = (pbuf[HALO - 1:HALO - 1 + TM, :] * cw[0:1] + pbuf[HALO:HALO + TM, :] * cw[1:2]
         + pbuf[HALO + 1:HALO + 1 + TM, :] * cw[2:3] + cb_ref[...])
    y = _silu(y)
    nb = MLSTM_HEADS * MLSTM_DH
    mq_out[...] = (y[:, :nb] * MLSTM_DH ** -0.5).astype(BF16)
    mk_out[...] = y[:, nb:].astype(BF16)

    mv = _dot(ub, wmv_ref[...])
    mo = _dot(ub, wmo_ref[...])

    q = q * cos_all + swap_halves(q) * sin_all
    q_out[...] = (q * ((MLA_NOPE + MLA_ROPE) ** -0.5 * LOG2E)).astype(BF16)
    rot = kr * cos + swap_halves(kr) * sin
    k_out[...] = (kn + jnp.concatenate([rot] * MLA_HEADS, axis=1)).astype(BF16)
    vlane = lax.broadcasted_iota(jnp.int32, (TM, MLA_HEADS * LANES), 1)
    is_value = ((vlane % LANES) < MLA_V) == ((vlane // LANES) % 2 == 0)
    v_out[...] = jnp.where(is_value, vv, 1.0).astype(BF16)

    zz = _dot(ub, wz_ref[...])
    mv_out[...] = mv.astype(BF16)
    mo_out[...] = _sigmoid(mo).astype(BF16)
    z_out[...] = _silu(zz).astype(BF16)

    r8 =lax.broadcasted_iota(jnp.int32, (SUBLANES, TM), 0)
    both = lambda i: jnp.where(r8 < half, per_dir[0][i], pltpu.roll(per_dir[1][i], half, axis=0)) * LOG2E
    c_rows, pm_rows, b_rows = both(0), both(1), both(2)
    for ci in range(TM // MLSTM_CHUNK):
        gr_out[ci] = c_rows[:, ci * MLSTM_CHUNK:(ci + 1) * MLSTM_CHUNK]
    stack = jnp.concatenate([c_rows, pm_rows, b_rows, jnp.zeros((LANES - 3 * SUBLANES, TM), F32)], axis=0)
    er = lax.broadcasted_iota(jnp.int32, (TM, TM), 0)
    ec = lax.broadcasted_iota(jnp.int32, (TM, TM), 1)
    eye = jnp.where(er == ec, 1.0, 0.0).astype(BF16)
    gc_out[...] = sum(_dot_nt(eye, part) for part in _split3(stack))


def _rope_tables(n_ctx, seq):
    t = jnp.arange(seq)
    inv = 1.0 / (ROPE_BASE ** (jnp.arange(ROPE_PAIRS, dtype=F32) / ROPE_PAIRS))
    ang = jnp.concatenate([(t // GRID_W)[:, None] * inv, (t % GRID_W)[:, None] * inv], axis=-1)
    cos, sin = jnp.cos(ang), jnp.sin(ang)
    j = np.arange(MLA_ROPE)
    src = (j // (2 * ROPE_PAIRS)) * ROPE_PAIRS + (j % ROPE_PAIRS)
    sign = np.where((j % (2 * ROPE_PAIRS)) < ROPE_PAIRS, -1.0, 1.0).astype(np.float32)
    cos_full = cos[:, src]
    sin_full = sin[:, src] * sign
    cos_t = jnp.ones((n_ctx + seq, LANES), F32).at[n_ctx:, MLA_NOPE:MLA_NOPE + MLA_ROPE].set(cos_full)
    sin_t = jnp.zeros((n_ctx + seq, LANES), F32).at[n_ctx:, MLA_NOPE:MLA_NOPE + MLA_ROPE].set(sin_full)
    return cos_t, sin_t


def _proj0_weights(w_in, q_norm, w_uq, kv_norm, w_ukv, conv_w, conv_b, b_i, b_f):
    o = np.cumsum([0, MLA_Q_RANK, MLA_KV_RANK, MLA_ROPE] + [MLSTM_HEADS * MLSTM_DH] * 4 + [4 * MLSTM_HEADS, D_MODEL])
    dq = MLA_NOPE + MLA_ROPE
    wkr = jnp.pad(w_in[:, o[2]:o[3]], ((0, 0), (MLA_NOPE, LANES - dq)))
    wuq = jnp.pad(w_uq.reshape(MLA_Q_RANK, MLA_HEADS, dq), ((0, 0), (0, 0), (0, LANES - dq)))
    wkv = w_ukv.reshape(MLA_KV_RANK, MLA_HEADS, MLA_NOPE + MLA_V)
    wk = jnp.pad(wkv[:, :, :MLA_NOPE], ((0, 0), (0, 0), (0, LANES - MLA_NOPE)))
    zv = jnp.zeros((MLA_KV_RANK, MLA_HEADS // 2, MLA_V), F32)
    wv = jnp.concatenate([wkv[:, 0::2, MLA_NOPE:], zv, zv, wkv[:, 1::2, MLA_NOPE:]], axis=-1)
    wv = wv.reshape(MLA_KV_RANK, MLA_HEADS * LANES)
    gbias = jnp.stack([b_i, b_f], axis=1).reshape(4 * MLSTM_HEADS, 1)
    bf = lambda a: a.astype(BF16)
    return dict(
        wcqkr=bf(jnp.concatenate([w_in[:, o[0]:o[1]], wkr], axis=1)), wckv=bf(w_in[:, o[1]:o[2]]),
        wmqk=bf(w_in[:, o[3]:o[5]]), wmv=bf(w_in[:, o[5]:o[6]]), wmo=bf(w_in[:, o[6]:o[7]]),
        wgt=bf(w_in[:, o[7]:o[8]].T), wz=bf(w_in[:, o[8]:o[9]]),
        qn=q_norm[None], wuq=bf(wuq.reshape(MLA_Q_RANK, -1)),
        kvn=kv_norm[None], wk=bf(wk.reshape(MLA_KV_RANK, -1)), wv=bf(wv),
        cw=jnp.zeros((SUBLANES, conv_w.shape[1]), F32).at[:MLSTM_CONV].set(conv_w), cb=conv_b[None], gb=gbias)


def _tile_specs(nct, nt, nb):
    tile = lambda w: pl.BlockSpec((nb, TM, w), lambda b, t: (b, t, 0))
    per = TM // HALO
    nlat = nt - nct
    src_ctx = pl.BlockSpec((nb, TM, D_MODEL), lambda b, t: (b, jnp.minimum(t, nct - 1), 0))
    src_lat = pl.BlockSpec((nb, TM, D_MODEL), lambda b, t: (b, jnp.maximum(t - nct, 0), 0))
    prev = pl.BlockSpec((nb, HALO, D_MODEL), lambda b, t: (b, jnp.maximum((t - nct) * per - 1, 0), 0))
    nxt = pl.BlockSpec((nb, HALO, D_MODEL),
                       lambda b, t: (b, jnp.clip((t - nct + 1) * per, 0, nlat * per - 1), 0))
    mod = pl.BlockSpec((nb, None, 1, D_MODEL), lambda b, t: (b, (t >= nct).astype(jnp.int32), 0, 0))
    return tile, src_ctx, src_lat, prev, nxt, mod


def _per_element(tile_kernel, shared):
    def kernel(*refs, **kw):
        for bi in range(refs[0].shape[0]):
            tile_kernel(*[r if i in shared else r.at[bi] for i, r in enumerate(refs)], **kw)
    return kernel


def _proj0(ctx, x, scale, shift, norm_g, cos_t, sin_t, w, nct):
    bsz = x.shape[0]
    ttot = ctx.shape[1] + x.shape[1]
    nt = ttot // TM
    nb = MLSTM_HEADS * MLSTM_DH
    sb = math.gcd(bsz, TILE_BATCH)
    tile, src_ctx, src_lat, prev, nxt, mod = _tile_specs(nct, nt, sb)
    tab = pl.BlockSpec((TM, LANES), lambda b, t: (t, 0))
    wnames = ["wcqkr", "wckv", "wmqk", "wmv", "wmo", "wgt", "wz",
              "qn", "wuq", "kvn", "wk", "wv", "cw", "cb", "gb"]
    ws = [w[n] for n in wnames]
    tok = lambda width, dt: jax.ShapeDtypeStruct((bsz, ttot, width), dt)
    out_shape = (tok(MLA_HEADS * LANES, BF16), tok(MLA_HEADS * LANES, BF16), tok(MLA_HEADS * LANES, BF16),
                 tok(nb, BF16), tok(nb, BF16), tok(nb, BF16),
                 jax.ShapeDtypeStruct((bsz, ttot // MLSTM_CHUNK, 2 * MLSTM_HEADS, MLSTM_CHUNK), F32), tok(LANES, F32),
                 tok(nb, BF16), tok(D_MODEL, BF16))
    out_specs = (tile(MLA_HEADS * LANES), tile(MLA_HEADS * LANES), tile(MLA_HEADS * LANES),
                 tile(nb), tile(nb), tile(nb),
                 pl.BlockSpec((sb, TM // MLSTM_CHUNK, 2 * MLSTM_HEADS, MLSTM_CHUNK), lambda b, t: (b, t, 0, 0)),
                 tile(LANES),
                 tile(nb), tile(D_MODEL))
    n_blocked, n_out = 6, len(out_shape)
    n_in = n_blocked + 3 + len(ws)
    shared = set(range(n_blocked, n_in)) | {n_in + n_out}
    return pl.pallas_call(
        functools.partial(_per_element(_proj0_kernel, shared), nct=nct, nt=nt),
        out_shape=out_shape,
        grid=(bsz // sb, nt),
        in_specs=[src_ctx, src_lat, prev, nxt, mod, mod, _full((1, D_MODEL)), tab, tab] + [_full(a.shape) for a in ws],
        out_specs=out_specs,
        scratch_shapes=[pltpu.VMEM((TM + 2 * HALO, 2 * nb), F32)],
        compiler_params=_params("parallel", "parallel"),
        name="proj0",
    )(ctx, x, x, x, scale, shift, norm_g[None], cos_t, sin_t, *ws)


def _mla_kernel(q_ref, k_ref, v_ref, o_ref, *, nct):
    t = pl.program_id(2)
    lane = lax.broadcasted_iota(jnp.int32, (TM, LANES), 1)

    def attend(nkeys):
        qk = lambda j: _dot_nt(q_ref[:, j * LANES:(j + 1) * LANES], k_ref[0:nkeys, j * LANES:(j + 1) * LANES])
        outs = []
        s_next = qk(0)
        for j in range(MLA_HPS):
            s = s_next
            if j + 1 < MLA_HPS:
                s_next = qk(j + 1)
            p = jnp.exp2(s - jnp.max(s, axis=-1, keepdims=True))
            o = _dot(p.astype(BF16), v_ref[0:nkeys, j * LANES:(j + 1) * LANES])
            outs.append(o / pltpu.roll(o, MLA_V, axis=1))
        for j in range(MLA_HPS // 2):
            o_ref[:, j * LANES:(j + 1) * LANES] = jnp.where(lane < MLA_V, outs[2 * j], outs[2 * j + 1]).astype(BF16)

    @pl.when(t < nct)
    def _():
        attend(nct * TM)

    @pl.when(t >= nct)
    def _():
        attend(k_ref.shape[0])


def _mla_attention(q, k, v, nct):
    bsz, ttot, _ = q.shape
    nt = ttot // TM
    hw = MLA_HPS * LANES
    return pl.pallas_call(
        functools.partial(_mla_kernel, nct=nct),
        out_shape=jax.ShapeDtypeStruct((bsz, ttot, MLA_HEADS * MLA_V), BF16),
        grid=(bsz, MLA_HEADS // MLA_HPS, nt),
        in_specs=[pl.BlockSpec((None, TM, hw), lambda b, p, t: (b, t, p)),
                  pl.BlockSpec((None, ttot, hw), lambda b, p, t: (b, 0, p)),
                  pl.BlockSpec((None, ttot, hw), lambda b, p, t: (b, 0, p))],
        out_specs=pl.BlockSpec((None, TM, hw // 2), lambda b, p, t: (b, t, p)),
        compiler_params=_params("parallel", "parallel", "parallel"),
        name="mla_attention",
    )(q, k, v)


def _scan_block(j, nct, nt):
    return jnp.where(j < nct, nct - 1 - j, nt - 1 - (j - nct))


def _mlstm_kernel(qf_ref, kf_ref, vf_ref, grf_ref, gcf_ref, qr_ref, kr_ref, vr_ref, grr_ref, gcr_ref,
                  hf_ref, hr_ref, c_scr, m_scr):
    @pl.when(pl.program_id(1) == 0)
    def _():
        c_scr[...] = jnp.zeros_like(c_scr)
        m_scr[...] = jnp.zeros_like(m_scr)

    L = MLSTM_CHUNK
    nchunk = TM // L
    nh = MLSTM_HEADS
    nchain = 2 * nh
    row = lax.broadcasted_iota(jnp.int32, (L, L), 0)
    col = lax.broadcasted_iota(jnp.int32, (L, L), 1)
    ones = jnp.ones((L, MLSTM_DH), BF16)
    wide = lambda a: jnp.broadcast_to(a, (L, MLSTM_DH))
    twice = lambda a: jnp.concatenate([a, a], axis=1)
    refs = ((qf_ref, kf_ref, vf_ref, grf_ref, gcf_ref, hf_ref), (qr_ref, kr_ref, vr_ref, grr_ref, gcr_ref, hr_ref))
    nbatch = qf_ref.shape[0]

    def load(ci):
        cs = []
        for bi, d in [(bi, d) for bi in range(nbatch) for d in range(2)]:
            q_ref, k_ref, v_ref, gr_ref, gc_ref, o_ref = (r.at[bi] for r in refs[d])
            cd = ci if d == 0 else nchunk - 1 - ci
            rs = slice(cd * L, (cd + 1) * L)
            g_rows = gr_ref[cd]
            g_cols = gc_ref[rs, :]
            last = L - 1 if d == 0 else 0
            for hd in range(nh):
                ch = d * nh + hd
                hs = slice(hd * MLSTM_DH, (hd + 1) * MLSTM_DH)
                cs.append(dict(
                    rs=rs, hs=hs, o_ref=o_ref, causal=(col <= row) if d == 0 else (col >= row),
                    c_row=g_rows[ch:ch + 1], c_col=g_cols[:, ch:ch + 1],
                    pm_col=g_cols[:, nchain + ch:nchain + ch + 1], b_col=g_cols[:, 2 * nchain + ch:2 * nchain + ch + 1],
                    pm_last=g_cols[last:last + 1, nchain + ch:nchain + ch + 1],
                    b_last=g_cols[last:last + 1, 2 * nchain + ch:2 * nchain + ch + 1],
                    q=q_ref[rs, hs], k=k_ref[rs, hs],
                    v_ext=jnp.concatenate([v_ref[rs, hs], ones], axis=1)))
        return cs

    def prep_scores(cs):
        for c in cs:
            c["s"] = _dot_nt(c["q"], c["k"])

    def prep_weights(cs):
        for c in cs:
            c["pm_w"] = wide(c["pm_col"])
            c["b_w"] = wide(c["b_col"])
            c["w"] = jnp.exp2(jnp.where(c["causal"], c["c_row"] - c["pm_w"][:, :L], -jnp.inf))
            c["kwf"] = jnp.exp2(wide(c["c_col"] - c["pm_last"]))

    def prep_operands(cs):
        for c in cs:
            c["p"] = (c["s"] * c["w"]).astype(BF16)
            c["kw"] = (c["kwf"] * c["k"].astype(F32)).astype(BF16)

    def prep_products(cs):
        for c in cs:
            c["o1"] = _dot(c["p"], c["v_ext"])
            c["dc"] = _dot_tn(c["kw"], c["v_ext"])

    def scan_read(cs):
        for ch, c in enumerate(cs):
            c["qc"] = _dot(c["q"], state[ch].astype(BF16))

    def scan_factors(cs):
        for ch, c in enumerate(cs):
            g_w = jnp.maximum(m[ch], c["pm_w"])
            c["e_intra"] = jnp.exp2(c["pm_w"] - g_w)
            c["e_inter"] = jnp.exp2(m[ch] - g_w)
            c["floor"] = jnp.exp2(-(c["b_w"] + g_w))
            g_last = jnp.maximum(m[ch], c["pm_last"])
            c["keep"] = jnp.exp2(m[ch] - g_last)
            c["gain"] = jnp.exp2(c["pm_last"] - g_last)
            c["m_new"] = c["b_last"] + g_last

    def scan_update(cs):
        for ch, c in enumerate(cs):
            o = twice(c["e_intra"]) * c["o1"] + twice(c["e_inter"]) * c["qc"]
            h = o[:, :MLSTM_DH] / jnp.maximum(jnp.abs(o[:, MLSTM_DH:]), c["floor"])
            c["o_ref"][c["rs"], c["hs"]] = h.astype(BF16)
            state[ch] = c["keep"] * state[ch] + c["gain"] * c["dc"]
            m[ch] = c["m_new"]

    nstate = nbatch * nchain
    state = [c_scr[ch] for ch in range(nstate)]
    m = [m_scr[ch, 0:1, 0:1] for ch in range(nstate)]
    chunks = [load(ci) for ci in range(nchunk)]
    everything = [c for cs in chunks for c in cs]
    for stage in (prep_scores, prep_weights, prep_operands):
        stage(everything)
    prep_products(chunks[0])
    for i, cs in enumerate(chunks):
        scan_read(cs)
        if i + 1 < nchunk:
            prep_products(chunks[i + 1])
        scan_factors(cs)
        scan_update(cs)
    for ch in range(nstate):
        c_scr[ch] = state[ch]
        m_scr[ch] = jnp.broadcast_to(m[ch], m_scr.shape[1:])


def _mlstm(mq, mk, mv, g_rows, g_cols, nct):
    bsz, ttot, nb = mq.shape
    nt = ttot // TM
    per = TM // MLSTM_CHUNK
    sb = math.gcd(bsz, MLSTM_BATCH)
    fwd = lambda w: pl.BlockSpec((sb, TM, w), lambda b, j: (b, j, 0))
    rev = lambda w: pl.BlockSpec((sb, TM, w), lambda b, j: (b, _scan_block(j, nct, nt), 0))
    gshape = (sb, per, 2 * MLSTM_HEADS, MLSTM_CHUNK)
    gfwd = pl.BlockSpec(gshape, lambda b, j: (b, j, 0, 0))
    grev = pl.BlockSpec(gshape, lambda b, j: (b, _scan_block(j, nct, nt), 0, 0))
    nstate = sb * 2 * MLSTM_HEADS
    return pl.pallas_call(
        _mlstm_kernel,
        out_shape=(jax.ShapeDtypeStruct((bsz, ttot, nb), BF16),) * 2,
        grid=(bsz // sb, nt),
        in_specs=[fwd(nb), fwd(nb), fwd(nb), gfwd, fwd(LANES), rev(nb), rev(nb), rev(nb), grev, rev(LANES)],
        out_specs=(fwd(nb), rev(nb)),
        scratch_shapes=[pltpu.VMEM((nstate, MLSTM_DH, 2 * MLSTM_DH), F32),
                        pltpu.VMEM((nstate, SUBLANES, LANES), F32)],
        compiler_params=_params("parallel", "arbitrary"),
        name="mlstm_scan",
    )(mq, mk, mv, g_rows, g_cols, mq, mk, mv, g_rows, g_cols)


def _head_rms(x, n_heads):
    w = x.shape[1] // n_heads
    return jnp.concatenate([_rms(x[:, i * w:(i + 1) * w]) for i in range(n_heads)], axis=1)


def _mid_kernel(ctx_ref, lat_ref, gate_ref, a_ref, hf_ref, hr_ref, mo_ref, z0_ref, hn_ref, wout_ref,
                sc_ref, sh_ref, g_ref, wq_ref, wk_ref, wv_ref, wga_ref, wnq_ref, wnk_ref, wnv_ref, wz_ref,
                wgate_ref, bgate_ref,
                h_out, gq_out, gk_out, gv_out, bc_out, nq_out, nk_out, nv_out, z_out, *, nct):
    f32 = lambda r: r[...].astype(F32)
    hm = _head_rms(f32(mo_ref) * (f32(hf_ref) + f32(hr_ref)), MLSTM_HEADS) * hn_ref[...]
    cat = (jnp.concatenate([f32(a_ref), hm], axis=1) * f32(z0_ref)).astype(BF16)
    h0 = jnp.where(pl.program_id(1) < nct, ctx_ref[...], lat_ref[...])
    h = h0 + gate_ref[...] * _dot(cat, wout_ref[...])
    h_out[...] = h

    ub = _modnorm(h, g_ref[...], sc_ref[...], sh_ref[...]).astype(BF16)
    ga = _dot(ub, wga_ref[...]).astype(BF16)
    gate_pre = _dot(ga, wgate_ref[...])
    gq_out[...] = (_dot(ub, wq_ref[...]) * GLA_DK ** -0.5).astype(BF16)
    gk_out[...] = _dot(ub, wk_ref[...]).astype(BF16)
    gv_out[...] = _dot(ub, wv_ref[...]).astype(BF16)

    lg = _log_sigmoid(gate_pre + bgate_ref[...]) * (1.0 / GLA_TAU)
    hi, mid, lo = _split3(lg)
    nq_out[...] = (_dot(ub, wnq_ref[...]) * (NA_DH ** -0.5 * LOG2E)).astype(BF16)
    nk_out[...] = _dot(ub, wnk_ref[...]).astype(BF16)
    nv_out[...] = _dot(ub, wnv_ref[...]).astype(BF16)
    row = lax.broadcasted_iota(jnp.int32, (TM, TM), 0)
    col = lax.broadcasted_iota(jnp.int32, (TM, TM), 1)
    same = (row // CHUNK) == (col // CHUNK)
    nk = GLA_HEADS * GLA_DK
    for d in range(2):
        tri = jnp.where(jnp.logical_and(same, (col <= row) if d == 0 else (col >= row)), 1.0, 0.0).astype(BF16)
        cols = slice(d * nk, (d + 1) * nk)
        bc_out[:, cols] = (_dot(tri, hi[:, cols]) + _dot(tri, mid[:, cols]) + _dot(tri, lo[:, cols])) * LOG2E
    z_out[...] = _silu(_dot(ub, wz_ref[...])).astype(BF16)


def _proj1_weights(w_in, w_gate, b_gate):
    nk = GLA_HEADS * GLA_DK
    nc = GLA_HEADS * GLA_DV
    nd = NA_HEADS * NA_DH
    o = np.cumsum([0, nk, nk, nc, 2 * GLA_GATE_RANK, nd, nd, nd, nc + nd])
    r = GLA_GATE_RANK
    wgate = jnp.zeros((2 * r, 2 * nk), F32).at[:r, :nk].set(w_gate[0]).at[r:, nk:].set(w_gate[1])
    bf = lambda a: a.astype(BF16)
    names = ["wq", "wk", "wv", "wga", "wnq", "wnk", "wnv", "wz"]
    w = {n: bf(w_in[:, o[i]:o[i + 1]]) for i, n in enumerate(names)}
    w["wgate"] = bf(wgate)
    w["bgate"] = b_gate.reshape(1, 2 * nk)
    return w


def _mid(ctx, x, gate0, a, hf, hr, mo, z0, h_norm, w_out, scale, shift, norm_g, w, nct):
    bsz = x.shape[0]
    ttot = ctx.shape[1] + x.shape[1]
    nt = ttot // TM
    sb = math.gcd(bsz, TILE_BATCH)
    tile, src_ctx, src_lat, _, _, mod = _tile_specs(nct, nt, sb)
    nb = MLSTM_HEADS * MLSTM_DH
    nk = GLA_HEADS * GLA_DK
    nc = GLA_HEADS * GLA_DV
    nd = NA_HEADS * NA_DH
    wnames = ["wq", "wk", "wv", "wga", "wnq", "wnk", "wnv", "wz", "wgate", "bgate"]
    ws = [w[n] for n in wnames]
    tok = lambda width, dt: jax.ShapeDtypeStruct((bsz, ttot, width), dt)
    shared = {8, 9} | set(range(12, 13 + len(ws)))
    return pl.pallas_call(
        functools.partial(_per_element(_mid_kernel, shared), nct=nct),
        out_shape=(tok(D_MODEL, F32), tok(nk, BF16), tok(nk, BF16), tok(nc, BF16), tok(2 * nk, F32),
                   tok(nd, BF16), tok(nd, BF16), tok(nd, BF16), tok(nc + nd, BF16)),
        grid=(bsz // sb, nt),
        in_specs=[src_ctx, src_lat, mod, tile(MLA_HEADS * MLA_V), tile(nb), tile(nb), tile(nb), tile(D_MODEL),
                  _full((1, nb)), _full(w_out.shape), mod, mod, _full((1, D_MODEL))] + [_full(a.shape) for a in ws],
        out_specs=(tile(D_MODEL), tile(nk), tile(nk), tile(nc), tile(2 * nk), tile(nd), tile(nd), tile(nd),
                   tile(nc + nd)),
        compiler_params=_params("parallel", "parallel"),
        name="out0_proj1",
    )(ctx, x, gate0, a, hf, hr, mo, z0, h_norm[None], w_out.astype(BF16), scale, shift, norm_g[None], *ws)


def _gla_kernel(qf_ref, kf_ref, vf_ref, bf_ref, qr_ref, kr_ref, vr_ref, br_ref, of_ref, or_ref, s_scr):
    @pl.when(pl.program_id(1) == 0)
    def _():
        s_scr[...] = jnp.zeros_like(s_scr)

    L = CHUNK
    nchunk = TM // L
    npair = GLA_HEADS // 2
    lane = lax.broadcasted_iota(jnp.int32, (L, LANES), 1)
    rowv = lax.broadcasted_iota(jnp.int32, (L, LANES), 0)
    srow = lax.broadcasted_iota(jnp.int32, (LANES, 2 * GLA_DV), 0)
    scol = lax.broadcasted_iota(jnp.int32, (LANES, 2 * GLA_DV), 1)
    own_block = (srow < GLA_DK) == (scol < GLA_DV)
    vcol = lax.broadcasted_iota(jnp.int32, (L, 2 * GLA_DV), 1)
    e_row = lax.broadcasted_iota(jnp.int32, (LANES, LANES), 0)
    e_col = lax.broadcasted_iota(jnp.int32, (LANES, LANES), 1)
    head_ones = jnp.where((e_row < GLA_DK) == (e_col < GLA_DK), 1.0, 0.0).astype(BF16)
    leaf_idx = lax.broadcasted_iota(jnp.int32, (GLA_LEAF * GLA_LEAF, LANES), 0)
    leaf_lane = lax.broadcasted_iota(jnp.int32, (GLA_LEAF * GLA_LEAF, LANES), 1) % L
    leaf_s, leaf_t = leaf_idx // GLA_LEAF, leaf_idx % GLA_LEAF
    rep = lambda a: jnp.concatenate(
        [jnp.broadcast_to(a[s:s + 1], (GLA_LEAF, a.shape[1])) for s in range(GLA_LEAF)], axis=0)
    til = lambda a: jnp.concatenate([a] * GLA_LEAF, axis=0)

    def chunk_step(ci, carry):
        cs = []
        for bi, d in [(bi, d) for bi in range(qf_ref.shape[0]) for d in range(2)]:
            q_ref, k_ref, v_ref, b_ref, o_ref = (
                r.at[bi] for r in ((qf_ref, kf_ref, vf_ref, bf_ref, of_ref), (qr_ref, kr_ref, vr_ref, br_ref, or_ref))[d])
            cd = ci if d == 0 else nchunk - 1 - ci
            base = pl.multiple_of(cd * L, L)
            for p in range(npair):
                ls = slice(p * LANES, (p + 1) * LANES)
                bc = b_ref[pl.ds(base, L), ls]
                cs.append(dict(
                    d=d, p=p, bi=bi, base=base, o_ref=o_ref, bc=bc,
                    tau_v=rowv if d == 0 else L - 1 - rowv,
                    tau_s=(lane % L) if d == 0 else L - 1 - (lane % L),
                    q=q_ref[pl.ds(base, L), ls].astype(F32),
                    k=k_ref[pl.ds(base, L), ls].astype(F32),
                    v=v_ref[pl.ds(base, L), p * 2 * GLA_DV:(p + 1) * 2 * GLA_DV],
                    b_last=bc[L - 1:L] if d == 0 else bc[0:1],
                    state=s_scr[bi, d, p]))

        for c in cs:
            c["o_pair"] = _dot((c["q"] * jnp.exp2(c["bc"])).astype(BF16), c["state"].astype(BF16))

        for c in cs:
            c["att"] = jnp.zeros((L, LANES), F32)
        bs = L // 2
        while bs >= GLA_LEAF:
            for c in cs:
                d, bc, tau_v = c["d"], c["bc"], c["tau_v"]
                later = (tau_v // bs) % 2 == 1
                ref_b = None
                for blk in range(L // (2 * bs)):
                    tau_ref = blk * 2 * bs + bs - 1
                    idx = tau_ref if d == 0 else L - 1 - tau_ref
                    r = jnp.broadcast_to(bc[idx:idx + 1], (L, LANES))
                    ref_b = r if ref_b is None else jnp.where(tau_v // (2 * bs) == blk, r, ref_b)
                c["qs"] = jnp.where(later, c["q"] * jnp.exp2(jnp.where(later, bc - ref_b, 0.0)), 0.0).astype(BF16)
                ks = jnp.where(later, 0.0, c["k"] * jnp.exp2(jnp.where(later, 0.0, ref_b - bc)))
                c["ks2"] = jnp.concatenate([jnp.where(lane < GLA_DK, ks, 0.0), jnp.where(lane < GLA_DK, 0.0, ks)],
                                           axis=0).astype(BF16)
            for c in cs:
                same_parent = (c["tau_v"] // (2 * bs)) == (c["tau_s"] // (2 * bs))
                c["att"] = c["att"] + jnp.where(same_parent, _dot_nt(c["qs"], c["ks2"]), 0.0)
            bs //= 2

        for c in cs:
            c["leaf"] = []
        for blk in range(L // GLA_LEAF):
            rs = slice(blk * GLA_LEAF, (blk + 1) * GLA_LEAF)
            for c in cs:
                qb, bb, kb = c["q"][rs], c["bc"][rs], c["k"][rs]
                ok = (leaf_t >= leaf_s) if c["d"] == 0 else (leaf_t <= leaf_s)
                w = jnp.where(ok, til(qb) * jnp.exp2(til(bb) - rep(bb)) * rep(kb), 0.0)
                c["a"] = _dot(w.astype(BF16), head_ones)
            for c in cs:
                a = jnp.where(leaf_lane == leaf_s + blk * GLA_LEAF, c["a"], 0.0)
                acc = a[0:GLA_LEAF]
                for s in range(1, GLA_LEAF):
                    acc = acc + a[s * GLA_LEAF:(s + 1) * GLA_LEAF]
                c["leaf"].append(acc)

        for c in cs:
            v = c["v"]
            c["att"] = (c["att"] + jnp.concatenate(c["leaf"], axis=0)).astype(BF16)
            c["v_blk"] = jnp.concatenate([jnp.where(vcol < GLA_DV, v, jnp.zeros_like(v)),
                                          jnp.where(vcol < GLA_DV, jnp.zeros_like(v), v)], axis=0)
            c["ke"] = (c["k"] * jnp.exp2(c["b_last"] - c["bc"])).astype(BF16)
            c["decay_col"] = jnp.sum(
                jnp.where(e_row == e_col, jnp.broadcast_to(jnp.exp2(c["b_last"]), (LANES, LANES)), 0.0),
                axis=1, keepdims=True)
        for c in cs:
            c["o"] = c["o_pair"] + _dot(c["att"], c["v_blk"])
            c["ds"] = _dot_tn(c["ke"], c["v"])
        for c in cs:
            p = c["p"]
            c["o_ref"][pl.ds(c["base"], L), p * 2 * GLA_DV:(p + 1) * 2 * GLA_DV] = c["o"].astype(BF16)
            s_scr[c["bi"], c["d"], p] = c["decay_col"] * c["state"] + jnp.where(own_block, c["ds"], 0.0)
        return carry

    lax.fori_loop(0, nchunk, chunk_step, 0)


def _gla(gq, gk, gv, bc, nct):
    bsz, ttot, nk = gq.shape
    nc = gv.shape[2]
    nt = ttot // TM
    sb = math.gcd(bsz, GLA_BATCH)
    fwd = lambda w: pl.BlockSpec((sb, TM, w), lambda b, j: (b, j, 0))
    rev = lambda w: pl.BlockSpec((sb, TM, w), lambda b, j: (b, _scan_block(j, nct, nt), 0))
    bfwd = pl.BlockSpec((sb, TM, nk), lambda b, j: (b, j, 0))
    brev = pl.BlockSpec((sb, TM, nk), lambda b, j: (b, _scan_block(j, nct, nt), 1))
    return pl.pallas_call(
        _gla_kernel,
        out_shape=(jax.ShapeDtypeStruct((bsz, ttot, nc), BF16),) * 2,
        grid=(bsz // sb, nt),
        in_specs=[fwd(nk), fwd(nk), fwd(nc), bfwd, rev(nk), rev(nk), rev(nc), brev],
        out_specs=(fwd(nc), rev(nc)),
        scratch_shapes=[pltpu.VMEM((sb, 2, GLA_HEADS // 2, LANES, 2 * GLA_DV), F32)],
        compiler_params=_params("parallel", "arbitrary"),
        name="gla_scan",
    )(gq, gk, gv, bc, gq, gk, gv, bc)


def _na_bias_tables(rpb, rows):
    kh = NA_KH
    c = np.arange(GRID_W)
    cs = np.clip(c - NA_KW // 2, 0, GRID_W - NA_KW)
    col_ok = (c[None, :] >= cs[:, None]) & (c[None, :] < cs[:, None] + NA_KW)
    pad = GRID_W - NA_KW
    rpb_pad = jnp.pad(rpb.astype(F32) * LOG2E, ((0, 0), (0, 0), (pad, pad)))
    t1 = jnp.stack([rpb_pad[:, :, GRID_W - 1 - cq:2 * GRID_W - 1 - cq] for cq in range(GRID_W)], axis=2)
    t1 = jnp.where(col_ok, t1, NEG_BIG)
    masked = jnp.full((NA_HEADS, GRID_W, GRID_W), NEG_BIG, F32)
    per_row = []
    for r0 in (0, NA_QROWS, rows - NA_QROWS):
        kb = int(np.clip(r0 - kh // 2, 0, rows - NA_KROWS))
        for qrow in range(r0, r0 + NA_QROWS):
            ws = int(np.clip(qrow - kh // 2, 0, rows - kh))
            per_row.append(jnp.concatenate(
                [t1[:, krow - qrow + NA_KH - 1] if ws <= krow < ws + kh else masked
                 for krow in range(kb, kb + NA_KROWS)], axis=-1))
    return jnp.concatenate(per_row, axis=1)


def _na_out1_kernel(q_ref, k_ref, v_ref, bias_ref, h_ref, gate_ref, of_ref, or_ref, z_ref, gn_ref, w_ref, fn_ref,
                    o_ref, *, n_ctx, rows):
    rb = pl.program_id(1)
    nq = NA_QROWS * GRID_W
    nk = NA_KROWS * GRID_W
    kb = jnp.clip(rb * NA_QROWS - NA_KH // 2, 0, rows - NA_KROWS)
    start = pl.multiple_of(n_ctx + kb * GRID_W, GRID_W)
    lane = lax.broadcasted_iota(jnp.int32, (nq, LANES), 1)
    def scores(j):
        ls = slice((j // 2) * LANES, (j // 2 + 1) * LANES)
        q = q_ref[:, ls]
        qj = jnp.where((lane < NA_DH) == (j % 2 == 0), q, jnp.zeros_like(q))
        return _dot_nt(qj, k_ref[pl.ds(start, nk), ls]) + bias_ref[j], _dot_nt(qj, k_ref[0:n_ctx, ls])

    outs = []
    nxt = scores(0)
    for j in range(NA_HEADS):
        s_loc, s_ctx = nxt
        if j + 1 < NA_HEADS:
            nxt = scores(j + 1)
        ls = slice((j // 2) * LANES, (j // 2 + 1) * LANES)
        m = jnp.maximum(jnp.max(s_loc, axis=-1, keepdims=True), jnp.max(s_ctx, axis=-1, keepdims=True))
        p_loc = jnp.exp2(s_loc - m)
        p_ctx = jnp.exp2(s_ctx - m)
        den = jnp.sum(p_loc, axis=-1, keepdims=True) + jnp.sum(p_ctx, axis=-1, keepdims=True)
        outs.append((_dot(p_loc.astype(BF16), v_ref[pl.ds(start, nk), ls])
                     + _dot(p_ctx.astype(BF16), v_ref[0:n_ctx, ls])) / den)
    na = [jnp.where(lane < NA_DH, outs[2 * i], outs[2 * i + 1]) for i in range(NA_HEADS // 2)]

    f32 = lambda r: r[...].astype(F32)
    g = _head_rms(f32(of_ref) + f32(or_ref), GLA_HEADS) * gn_ref[...]
    cat = (jnp.concatenate([g] + na, axis=1) * f32(z_ref)).astype(BF16)
    h = h_ref[...] + gate_ref[...] * _dot(cat, w_ref[...])
    o_ref[...] = _rms(h) * fn_ref[...]


def _na_out1(nq, nk, nv, bias, h, gate, of, orv, z, gla_norm, w_out, final_norm, n_ctx, rows):
    bsz, ttot, nd = nq.shape
    nrb = rows // NA_QROWS
    qrows = NA_QROWS * GRID_W
    krows = NA_KROWS * GRID_W
    assert qrows == TM
    nct = n_ctx // TM
    nc = GLA_HEADS * GLA_DV

    def variant(rb):
        return jnp.where(rb == 0, 0, jnp.where(rb == nrb - 1, 2, 1))

    lat = lambda w: pl.BlockSpec((None, TM, w), lambda b, rb: (b, rb + nct, 0))
    whole = lambda w: pl.BlockSpec((None, ttot, w), lambda b, rb: (b, 0, 0))
    return pl.pallas_call(
        functools.partial(_na_out1_kernel, n_ctx=n_ctx, rows=rows),
        out_shape=jax.ShapeDtypeStruct((bsz, rows * GRID_W, D_MODEL), F32),
        grid=(bsz, nrb),
        in_specs=[lat(nd), whole(nd), whole(nd),
                  pl.BlockSpec((NA_HEADS, qrows, krows), lambda b, rb: (0, variant(rb), 0)),
                  lat(D_MODEL), pl.BlockSpec((None, None, 1, D_MODEL), lambda b, rb: (b, 1, 0, 0)),
                  lat(nc), lat(nc), lat(nc + nd), _full((1, nc)), _full(w_out.shape), _full((1, D_MODEL))],
        out_specs=pl.BlockSpec((None, TM, D_MODEL), lambda b, rb: (b, rb, 0)),
        compiler_params=_params("parallel", "parallel"),
        name="na_out1",
    )(nq, nk, nv, bias, h, gate, of, orv, z, gla_norm[None], w_out.astype(BF16), final_norm[None])


def kernel(x, c, ctx, c_ctx, l0_norm, l0_w_mod, l0_b_mod, l0_w_in, l0_mla_q_norm, l0_mla_w_uq, l0_mla_kv_norm, l0_mla_w_ukv, l0_mlstm_conv_w, l0_mlstm_conv_b, l0_mlstm_b_i, l0_mlstm_b_f, l0_mlstm_norm, l0_w_out, l1_norm, l1_w_mod, l1_b_mod, l1_w_in, l1_gla_w_gate, l1_gla_b_gate, l1_gla_norm, l1_na_rpb, l1_w_out, final_norm):
    bsz, seq, d = x.shape
    n_ctx = ctx.shape[1]
    rows = seq // GRID_W
    assert d == D_MODEL and seq % TM == 0 and n_ctx == TM
    assert rows % NA_QROWS == 0 and rows >= NA_KROWS and rows // NA_QROWS >= 3
    nct = n_ctx // TM

    (shift0, scale0, gate0), (shift, scale, gate) = _mod_vectors(
        c, c_ctx, ((l0_w_mod, l0_b_mod), (l1_w_mod, l1_b_mod)))

    cos_t, sin_t = _rope_tables(n_ctx, seq)
    w0 = _proj0_weights(l0_w_in, l0_mla_q_norm, l0_mla_w_uq, l0_mla_kv_norm, l0_mla_w_ukv,
                        l0_mlstm_conv_w, l0_mlstm_conv_b, l0_mlstm_b_i, l0_mlstm_b_f)
    q, k, v, mq, mk, mv, g_rows, g_cols, mo, z0 = _proj0(ctx, x, scale0, shift0, l0_norm, cos_t, sin_t, w0, nct)
    a = _mla_attention(q, k, v, nct)
    hf, hr = _mlstm(mq, mk, mv, g_rows, g_cols, nct)

    w1 = _proj1_weights(l1_w_in, l1_gla_w_gate, l1_gla_b_gate)
    h, gq, gk, gv, bc, nq, nk, nv, z = _mid(ctx, x, gate0, a, hf, hr, mo, z0, l0_mlstm_norm, l0_w_out,
                                            scale, shift, l1_norm, w1, nct)
    of, orv = _gla(gq, gk, gv, bc, nct)
    return _na_out1(nq, nk, nv, _na_bias_tables(l1_na_rpb, rows), h, gate, of, orv, z,
                    l1_gla_norm, l1_w_out, final_norm, n_ctx, rows)
```

```python
import functools
import math

import jax
import jax.numpy as jnp
import numpy as np
from jax import lax
from jax.experimental import pallas as pl
from jax.experimental.pallas import tpu as pltpu

F32 = jnp.float32
BF16 = jnp.bfloat16

D_MODEL = 1024
GRID_W = 64
EPS = 1e-6
ROPE_BASE = 10000.0

MLA_HEADS = 8
MLA_Q_RANK = 384
MLA_KV_RANK = 256
MLA_NOPE = 64
MLA_ROPE = 32
MLA_V = 64
ROPE_PAIRS = MLA_ROPE // 4
MLSTM_HEADS = 4
MLSTM_DH = 128
MLSTM_CONV = 3
GLA_HEADS = 4
GLA_DK = 64
GLA_DV = 128
GLA_GATE_RANK = 16
GLA_TAU = 16.0
NA_HEADS = 8
NA_DH = 64
NA_KH = 8
NA_KW = 16

CHUNK = 64
MLSTM_CHUNK = 128
TM = 256
LANES = 128
SUBLANES = 8
HALO = SUBLANES
NA_QROWS = 4
NA_KROWS = NA_QROWS + NA_KH - 1
GLA_LEAF = 8
MLA_HPS = 4
TILE_BATCH = 2
MLSTM_BATCH = 2
GLA_BATCH = 8
VMEM_LIMIT = 56 * 1024 * 1024
NEG_BIG = -1e30
LOG2E = 1.4426950408889634


def _dot(a, b):
    return jnp.dot(a, b, preferred_element_type=F32)


def _dot_nt(a, b):
    return lax.dot_general(a, b, (((1,), (1,)), ((), ())), preferred_element_type=F32)


def _dot_tn(a, b):
    return lax.dot_general(a, b, (((0,), (0,)), ((), ())), preferred_element_type=F32)


def _rms(x):
    return x * lax.rsqrt(jnp.mean(x * x, axis=-1, keepdims=True) + EPS)


def _sigmoid(x):
    return 1.0 / (1.0 + jnp.exp(-x))


def _silu(x):
    return x * _sigmoid(x)


def _log_sigmoid(x):
    return jnp.minimum(x, 0.0) - jnp.log(1.0 + jnp.exp(-jnp.abs(x)))


def _params(*sem):
    return pltpu.CompilerParams(dimension_semantics=sem, vmem_limit_bytes=VMEM_LIMIT)


def _full(shape):
    nd = len(shape)
    return pl.BlockSpec(shape, lambda *_: (0,) * nd)


def _mod_kernel(c_ref, w0_ref, b0_ref, w1_ref, b1_ref, o0_ref, o1_ref):
    sc = _silu(c_ref[...]).astype(BF16)
    o0_ref[...] = _dot(sc, w0_ref[...].astype(BF16)) + b0_ref[...]
    o1_ref[...] = _dot(sc, w1_ref[...].astype(BF16)) + b1_ref[...]


def _mod_vectors(c, c_ctx, mods):
    bsz, d = c.shape
    rows = -(-(bsz + 1) // SUBLANES) * SUBLANES
    cc = jnp.zeros((rows, d), F32).at[:bsz].set(c).at[bsz].set(c_ctx)
    (w0, b0), (w1, b1) = mods
    n = w0.shape[1]
    wspec = pl.BlockSpec((d, d), lambda j: (0, j))
    vspec = pl.BlockSpec((1, d), lambda j: (0, j))
    ospec = pl.BlockSpec((rows, d), lambda j: (0, j))
    per_layer = pl.pallas_call(
        _mod_kernel,
        out_shape=(jax.ShapeDtypeStruct((rows, n), F32),) * 2,
        grid=(n // d,),
        in_specs=[_full((rows, d)), wspec, vspec, wspec, vspec],
        out_specs=(ospec, ospec),
        compiler_params=_params("parallel"),
        name="modulation",
    )(cc, w0, b0[None], w1, b1[None])
    out = []
    for mod in per_layer:
        vecs = []
        for part in jnp.split(mod, 3, axis=-1):
            ctx_v = jnp.broadcast_to(part[bsz][None], (bsz, d))
            vecs.append(jnp.stack([ctx_v, part[:bsz]], axis=1)[:, :, None, :])
        out.append(vecs)
    return out


def _modnorm(x, g, scale, shift):
    return _rms(x) * g * (1.0 + scale) + shift


def _seg_scan(x, reverse, op, fill):
    width = x.shape[1]
    pos = lax.broadcasted_iota(jnp.int32, x.shape, 1) % MLSTM_CHUNK
    k = 1
    while k < MLSTM_CHUNK:
        if reverse:
            x = op(x, jnp.where(pos < MLSTM_CHUNK - k, pltpu.roll(x, width - k, axis=1), fill))
        else:
            x = op(x, jnp.where(pos >= k, pltpu.roll(x, k, axis=1), fill))
        k *= 2
    return x


def _split3(x):
    hi = x.astype(BF16)
    r1 = x - hi.astype(F32)
    mid = r1.astype(BF16)
    return hi, mid, (r1 - mid.astype(F32)).astype(BF16)


def _proj0_kernel(ctx_ref, lat_ref, hp_ref, hn_ref, sc_ref, sh_ref, g_ref, cos_ref, sin_ref,
                  wcqkr_ref, wckv_ref, wmqk_ref, wmv_ref, wmo_ref, wgt_ref, wz_ref,
                  qn_ref, wuq_ref, kvn_ref, wk_ref, wv_ref, cw_ref, cb_ref, gb_ref,
                  q_out, k_out, v_out, mq_out, mk_out, mv_out, gr_out, gc_out, mo_out, z_out,
                  pbuf, *, nct, nt):
    t = pl.program_id(1)
    h = jnp.where(t < nct, ctx_ref[...], lat_ref[...])
    x = jnp.concatenate([hp_ref[...], h, hn_ref[...]], axis=0)
    ub_ext = _modnorm(x, g_ref[...], sc_ref[...], sh_ref[...]).astype(BF16)
    ub = ub_ext[HALO:HALO + TM]

    gt = _dot_nt(wgt_ref[...], ub) + gb_ref[...]
    half = MLSTM_HEADS
    per_dir = []
    for d in range(2):
        gd = gt[d * SUBLANES:(d + 1) * SUBLANES]
        bcum = _seg_scan(_log_sigmoid(gd), d == 1, jnp.add, 0.0)
        b_top = pltpu.roll(bcum, half, axis=0)
        c8 = gd - b_top
        pm8 = _seg_scan(c8, d == 1, jnp.maximum, -jnp.inf)
        per_dir.append((c8, pm8, b_top))

    cos = cos_ref[...]
    sin = sin_ref[...]
    cos_all = jnp.concatenate([cos] * MLA_HEADS, axis=1)
    sin_all = jnp.concatenate([sin] * MLA_HEADS, axis=1)

    def swap_halves(a):
        lane = lax.broadcasted_iota(jnp.int32, a.shape, 1)
        first = lane % (2 * ROPE_PAIRS) < ROPE_PAIRS
        return jnp.where(first, pltpu.roll(a, a.shape[1] - ROPE_PAIRS, axis=1), pltpu.roll(a, ROPE_PAIRS, axis=1))

    cqkr = _dot(ub, wcqkr_ref[...])
    cq = cqkr[:, :MLA_Q_RANK]
    kr = cqkr[:, MLA_Q_RANK:]
    ckv = _dot(ub, wckv_ref[...])
    pqk = _dot(ub_ext, wmqk_ref[...])

    nq = (_rms(cq) * qn_ref[...]).astype(BF16)
    nkv = (_rms(ckv) * kvn_ref[...]).astype(BF16)

    q = _dot(nq, wuq_ref[...])
    kn = _dot(nkv, wk_ref[...])
    vv = _dot(nkv, wv_ref[...])

    prev_ok = t > nct
    next_ok = jnp.logical_and(t >= nct, t != nt - 1)
    row = lax.broadcasted_iota(jnp.int32, pqk.shape, 0)
    keep = jnp.logical_and(jnp.logical_or(row >= HALO, prev_ok), jnp.logical_or(row < HALO + TM, next_ok))
    pbuf[...] = jnp.where(keep, pqk, 0.0)
    cw = cw_ref[...]
    y = (pbuf[HALO - 1:HALO - 1 + TM, :] * cw[0:1] + pbuf[HALO:HALO + TM, :] * cw[1:2]
         + pbuf[HALO + 1:HALO + 1 + TM, :] * cw[2:3] + cb_ref[...])
    y = _silu(y)
    nb = MLSTM_HEADS * MLSTM_DH
    mq_out[...] = (y[:, :nb] * MLSTM_DH ** -0.5).astype(BF16)
    mk_out[...] = y[:, nb:].astype(BF16)

    mv = _dot(ub, wmv_ref[...])
    mo = _dot(ub, wmo_ref[...])

    q = q * cos_all + swap_halves(q) * sin_all
    q_out[...] = (q * ((MLA_NOPE + MLA_ROPE) ** -0.5 * LOG2E)).astype(BF16)
    rot = kr * cos + swap_halves(kr) * sin
    k_out[...] = (kn + jnp.concatenate([rot] * MLA_HEADS, axis=1)).astype(BF16)
    vlane = lax.broadcasted_iota(jnp.int32, (TM, MLA_HEADS * LANES), 1)
    is_value = ((vlane % LANES) < MLA_V) == ((vlane // LANES) % 2 == 0)
    v_out[...] = jnp.where(is_value, vv, 1.0).astype(BF16)

    zz = _dot(ub, wz_ref[...])
    mv_out[...] = mv.astype(BF16)
    mo_out[...] = _sigmoid(mo).astype(BF16)
    z_out[...] = _silu(zz).astype(BF16)

    r8 =lax.broadcasted_iota(jnp.int32, (SUBLANES, TM), 0)
    both = lambda i: jnp.where(r8 < half, per_dir[0][i], pltpu.roll(per_dir[1][i], half, axis=0)) * LOG2E
    c_rows, pm_rows, b_rows = both(0), both(1), both(2)
    for ci in range(TM // MLSTM_CHUNK):
        gr_out[ci] = c_rows[:, ci * MLSTM_CHUNK:(ci + 1) * MLSTM_CHUNK]
    stack = jnp.concatenate([c_rows, pm_rows, b_rows, jnp.zeros((LANES - 3 * SUBLANES, TM), F32)], axis=0)
    er = lax.broadcasted_iota(jnp.int32, (TM, TM), 0)
    ec = lax.broadcasted_iota(jnp.int32, (TM, TM), 1)
    eye = jnp.where(er == ec, 1.0, 0.0).astype(BF16)
    gc_out[...] = sum(_dot_nt(eye, part) for part in _split3(stack))


def _rope_tables(n_ctx, seq):
    t = jnp.arange(seq)
    inv = 1.0 / (ROPE_BASE ** (jnp.arange(ROPE_PAIRS, dtype=F32) / ROPE_PAIRS))
    ang = jnp.concatenate([(t // GRID_W)[:, None] * inv, (t % GRID_W)[:, None] * inv], axis=-1)
    cos, sin = jnp.cos(ang), jnp.sin(ang)
    j = np.arange(MLA_ROPE)
    src = (j // (2 * ROPE_PAIRS)) * ROPE_PAIRS + (j % ROPE_PAIRS)
    sign = np.where((j % (2 * ROPE_PAIRS)) < ROPE_PAIRS, -1.0, 1.0).astype(np.float32)
    cos_full = cos[:, src]
    sin_full = sin[:, src] * sign
    cos_t = jnp.ones((n_ctx + seq, LANES), F32).at[n_ctx:, MLA_NOPE:MLA_NOPE + MLA_ROPE].set(cos_full)
    sin_t = jnp.zeros((n_ctx + seq, LANES), F32).at[n_ctx:, MLA_NOPE:MLA_NOPE + MLA_ROPE].set(sin_full)
    return cos_t, sin_t


def _proj0_weights(w_in, q_norm, w_uq, kv_norm, w_ukv, conv_w, conv_b, b_i, b_f):
    o = np.cumsum([0, MLA_Q_RANK, MLA_KV_RANK, MLA_ROPE] + [MLSTM_HEADS * MLSTM_DH] * 4 + [4 * MLSTM_HEADS, D_MODEL])
    dq = MLA_NOPE + MLA_ROPE
    wkr = jnp.pad(w_in[:, o[2]:o[3]], ((0, 0), (MLA_NOPE, LANES - dq)))
    wuq = jnp.pad(w_uq.reshape(MLA_Q_RANK, MLA_HEADS, dq), ((0, 0), (0, 0), (0, LANES - dq)))
    wkv = w_ukv.reshape(MLA_KV_RANK, MLA_HEADS, MLA_NOPE + MLA_V)
    wk = jnp.pad(wkv[:, :, :MLA_NOPE], ((0, 0), (0, 0), (0, LANES - MLA_NOPE)))
    zv = jnp.zeros((MLA_KV_RANK, MLA_HEADS // 2, MLA_V), F32)
    wv = jnp.concatenate([wkv[:, 0::2, MLA_NOPE:], zv, zv, wkv[:, 1::2, MLA_NOPE:]], axis=-1)
    wv = wv.reshape(MLA_KV_RANK, MLA_HEADS * LANES)
    gbias = jnp.stack([b_i, b_f], axis=1).reshape(4 * MLSTM_HEADS, 1)
    bf = lambda a: a.astype(BF16)
    return dict(
        wcqkr=bf(jnp.concatenate([w_in[:, o[0]:o[1]], wkr], axis=1)), wckv=bf(w_in[:, o[1]:o[2]]),
        wmqk=bf(w_in[:, o[3]:o[5]]), wmv=bf(w_in[:, o[5]:o[6]]), wmo=bf(w_in[:, o[6]:o[7]]),
        wgt=bf(w_in[:, o[7]:o[8]].T), wz=bf(w_in[:, o[8]:o[9]]),
        qn=q_norm[None], wuq=bf(wuq.reshape(MLA_Q_RANK, -1)),
        kvn=kv_norm[None], wk=bf(wk.reshape(MLA_KV_RANK, -1)), wv=bf(wv),
        cw=jnp.zeros((SUBLANES, conv_w.shape[1]), F32).at[:MLSTM_CONV].set(conv_w), cb=conv_b[None], gb=gbias)


def _tile_specs(nct, nt, nb):
    tile = lambda w: pl.BlockSpec((nb, TM, w), lambda b, t: (b, t, 0))
    per = TM // HALO
    nlat = nt - nct
    src_ctx = pl.BlockSpec((nb, TM, D_MODEL), lambda b, t: (b, jnp.minimum(t, nct - 1), 0))
    src_lat = pl.BlockSpec((nb, TM, D_MODEL), lambda b, t: (b, jnp.maximum(t - nct, 0), 0))
    prev = pl.BlockSpec((nb, HALO, D_MODEL), lambda b, t: (b, jnp.maximum((t - nct) * per - 1, 0), 0))
    nxt = pl.BlockSpec((nb, HALO, D_MODEL),
                       lambda b, t: (b, jnp.clip((t - nct + 1) * per, 0, nlat * per - 1), 0))
    mod = pl.BlockSpec((nb, None, 1, D_MODEL), lambda b, t: (b, (t >= nct).astype(jnp.int32), 0, 0))
    return tile, src_ctx, src_lat, prev, nxt, mod


def _per_element(tile_kernel, shared):
    def kernel(*refs, **kw):
        for bi in range(refs[0].shape[0]):
            tile_kernel(*[r if i in shared else r.at[bi] for i, r in enumerate(refs)], **kw)
    return kernel


def _proj0(ctx, x, scale, shift, norm_g, cos_t, sin_t, w, nct):
    bsz = x.shape[0]
    ttot = ctx.shape[1] + x.shape[1]
    nt = ttot // TM
    nb = MLSTM_HEADS * MLSTM_DH
    sb = math.gcd(bsz, TILE_BATCH)
    tile, src_ctx, src_lat, prev, nxt, mod = _tile_specs(nct, nt, sb)
    tab = pl.BlockSpec((TM, LANES), lambda b, t: (t, 0))
    wnames = ["wcqkr", "wckv", "wmqk", "wmv", "wmo", "wgt", "wz",
              "qn", "wuq", "kvn", "wk", "wv", "cw", "cb", "gb"]
    ws = [w[n] for n in wnames]
    tok = lambda width, dt: jax.ShapeDtypeStruct((bsz, ttot, width), dt)
    out_shape = (tok(MLA_HEADS * LANES, BF16), tok(MLA_HEADS * LANES, BF16), tok(MLA_HEADS * LANES, BF16),
                 tok(nb, BF16), tok(nb, BF16), tok(nb, BF16),
                 jax.ShapeDtypeStruct((bsz, ttot // MLSTM_CHUNK, 2 * MLSTM_HEADS, MLSTM_CHUNK), F32), tok(LANES, F32),
                 tok(nb, BF16), tok(D_MODEL, BF16))
    out_specs = (tile(MLA_HEADS * LANES), tile(MLA_HEADS * LANES), tile(MLA_HEADS * LANES),
                 tile(nb), tile(nb), tile(nb),
                 pl.BlockSpec((sb, TM // MLSTM_CHUNK, 2 * MLSTM_HEADS, MLSTM_CHUNK), lambda b, t: (b, t, 0, 0)),
                 tile(LANES),
                 tile(nb), tile(D_MODEL))
    n_blocked, n_out = 6, len(out_shape)
    n_in = n_blocked + 3 + len(ws)
    shared = set(range(n_blocked, n_in)) | {n_in + n_out}
    return pl.pallas_call(
        functools.partial(_per_element(_proj0_kernel, shared), nct=nct, nt=nt),
        out_shape=out_shape,
        grid=(bsz // sb, nt),
        in_specs=[src_ctx, src_lat, prev, nxt, mod, mod, _full((1, D_MODEL)), tab, tab] + [_full(a.shape) for a in ws],
        out_specs=out_specs,
        scratch_shapes=[pltpu.VMEM((TM + 2 * HALO, 2 * nb), F32)],
        compiler_params=_params("parallel", "parallel"),
        name="proj0",
    )(ctx, x, x, x, scale, shift, norm_g[None], cos_t, sin_t, *ws)


def _mla_kernel(q_ref, k_ref, v_ref, o_ref, *, nct):
    t = pl.program_id(2)
    lane = lax.broadcasted_iota(jnp.int32, (TM, LANES), 1)

    def attend(nkeys):
        qk = lambda j: _dot_nt(q_ref[:, j * LANES:(j + 1) * LANES], k_ref[0:nkeys, j * LANES:(j + 1) * LANES])
        outs = []
        s_next = qk(0)
        for j in range(MLA_HPS):
            s = s_next
            if j + 1 < MLA_HPS:
                s_next = qk(j + 1)
            p = jnp.exp2(s - jnp.max(s, axis=-1, keepdims=True))
            o = _dot(p.astype(BF16), v_ref[0:nkeys, j * LANES:(j + 1) * LANES])
            outs.append(o / pltpu.roll(o, MLA_V, axis=1))
        for j in range(MLA_HPS // 2):
            o_ref[:, j * LANES:(j + 1) * LANES] = jnp.where(lane < MLA_V, outs[2 * j], outs[2 * j + 1]).astype(BF16)

    @pl.when(t < nct)
    def _():
        attend(nct * TM)

    @pl.when(t >= nct)
    def _():
        attend(k_ref.shape[0])


def _mla_attention(q, k, v, nct):
    bsz, ttot, _ = q.shape
    nt = ttot // TM
    hw = MLA_HPS * LANES
    return pl.pallas_call(
        functools.partial(_mla_kernel, nct=nct),
        out_shape=jax.ShapeDtypeStruct((bsz, ttot, MLA_HEADS * MLA_V), BF16),
        grid=(bsz, MLA_HEADS // MLA_HPS, nt),
        in_specs=[pl.BlockSpec((None, TM, hw), lambda b, p, t: (b, t, p)),
                  pl.BlockSpec((None, ttot, hw), lambda b, p, t: (b, 0, p)),
                  pl.BlockSpec((None, ttot, hw), lambda b, p, t: (b, 0, p))],
        out_specs=pl.BlockSpec((None, TM, hw // 2), lambda b, p, t: (b, t, p)),
        compiler_params=_params("parallel", "parallel", "parallel"),
        name="mla_attention",
    )(q, k, v)


def _scan_block(j, nct, nt):
    return jnp.where(j < nct, nct - 1 - j, nt - 1 - (j - nct))


def _mlstm_kernel(qf_ref, kf_ref, vf_ref, grf_ref, gcf_ref, qr_ref, kr_ref, vr_ref, grr_ref, gcr_ref,
                  hf_ref, hr_ref, c_scr, m_scr):
    @pl.when(pl.program_id(1) == 0)
    def _():
        c_scr[...] = jnp.zeros_like(c_scr)
        m_scr[...] = jnp.zeros_like(m_scr)

    L = MLSTM_CHUNK
    nchunk = TM // L
    nh = MLSTM_HEADS
    nchain = 2 * nh
    row = lax.broadcasted_iota(jnp.int32, (L, L), 0)
    col = lax.broadcasted_iota(jnp.int32, (L, L), 1)
    ones = jnp.ones((L, MLSTM_DH), BF16)
    wide = lambda a: jnp.broadcast_to(a, (L, MLSTM_DH))
    twice = lambda a: jnp.concatenate([a, a], axis=1)
    refs = ((qf_ref, kf_ref, vf_ref, grf_ref, gcf_ref, hf_ref), (qr_ref, kr_ref, vr_ref, grr_ref, gcr_ref, hr_ref))
    nbatch = qf_ref.shape[0]

    def load(ci):
        cs = []
        for bi, d in [(bi, d) for bi in range(nbatch) for d in range(2)]:
            q_ref, k_ref, v_ref, gr_ref, gc_ref, o_ref = (r.at[bi] for r in refs[d])
            cd = ci if d == 0 else nchunk - 1 - ci
            rs = slice(cd * L, (cd + 1) * L)
            g_rows = gr_ref[cd]
            g_cols = gc_ref[rs, :]
            last = L - 1 if d == 0 else 0
            for hd in range(nh):
                ch = d * nh + hd
                hs = slice(hd * MLSTM_DH, (hd + 1) * MLSTM_DH)
                cs.append(dict(
                    rs=rs, hs=hs, o_ref=o_ref, causal=(col <= row) if d == 0 else (col >= row),
                    c_row=g_rows[ch:ch + 1], c_col=g_cols[:, ch:ch + 1],
                    pm_col=g_cols[:, nchain + ch:nchain + ch + 1], b_col=g_cols[:, 2 * nchain + ch:2 * nchain + ch + 1],
                    pm_last=g_cols[last:last + 1, nchain + ch:nchain + ch + 1],
                    b_last=g_cols[last:last + 1, 2 * nchain + ch:2 * nchain + ch + 1],
                    q=q_ref[rs, hs], k=k_ref[rs, hs],
                    v_ext=jnp.concatenate([v_ref[rs, hs], ones], axis=1)))
        return cs

    def prep_scores(cs):
        for c in cs:
            c["s"] = _dot_nt(c["q"], c["k"])

    def prep_weights(cs):
        for c in cs:
            c["pm_w"] = wide(c["pm_col"])
            c["b_w"] = wide(c["b_col"])
            c["w"] = jnp.exp2(jnp.where(c["causal"], c["c_row"] - c["pm_w"][:, :L], -jnp.inf))
            c["kwf"] = jnp.exp2(wide(c["c_col"] - c["pm_last"]))

    def prep_operands(cs):
        for c in cs:
            c["p"] = (c["s"] * c["w"]).astype(BF16)
            c["kw"] = (c["kwf"] * c["k"].astype(F32)).astype(BF16)

    def prep_products(cs):
        for c in cs:
            c["o1"] = _dot(c["p"], c["v_ext"])
            c["dc"] = _dot_tn(c["kw"], c["v_ext"])

    def scan_read(cs):
        for ch, c in enumerate(cs):
            c["qc"] = _dot(c["q"], state[ch].astype(BF16))

    def scan_factors(cs):
        for ch, c in enumerate(cs):
            g_w = jnp.maximum(m[ch], c["pm_w"])
            c["e_intra"] = jnp.exp2(c["pm_w"] - g_w)
            c["e_inter"] = jnp.exp2(m[ch] - g_w)
            c["floor"] = jnp.exp2(-(c["b_w"] + g_w))
            g_last = jnp.maximum(m[ch], c["pm_last"])
            c["keep"] = jnp.exp2(m[ch] - g_last)
            c["gain"] = jnp.exp2(c["pm_last"] - g_last)
            c["m_new"] = c["b_last"] + g_last

    def scan_update(cs):
        for ch, c in enumerate(cs):
            o = twice(c["e_intra"]) * c["o1"] + twice(c["e_inter"]) * c["qc"]
            h = o[:, :MLSTM_DH] / jnp.maximum(jnp.abs(o[:, MLSTM_DH:]), c["floor"])
            c["o_ref"][c["rs"], c["hs"]] = h.astype(BF16)
            state[ch] = c["keep"] * state[ch] + c["gain"] * c["dc"]
            m[ch] = c["m_new"]

    nstate = nbatch * nchain
    state = [c_scr[ch] for ch in range(nstate)]
    m = [m_scr[ch, 0:1, 0:1] for ch in range(nstate)]
    chunks = [load(ci) for ci in range(nchunk)]
    everything = [c for cs in chunks for c in cs]
    for stage in (prep_scores, prep_weights, prep_operands):
        stage(everything)
    prep_products(chunks[0])
    for i, cs in enumerate(chunks):
        scan_read(cs)
        if i + 1 < nchunk:
            prep_products(chunks[i + 1])
        scan_factors(cs)
        scan_update(cs)
    for ch in range(nstate):
        c_scr[ch] = state[ch]
        m_scr[ch] = jnp.broadcast_to(m[ch], m_scr.shape[1:])


def _mlstm(mq, mk, mv, g_rows, g_cols, nct):
    bsz, ttot, nb = mq.shape
    nt = ttot // TM
    per = TM // MLSTM_CHUNK
    sb = math.gcd(bsz, MLSTM_BATCH)
    fwd = lambda w: pl.BlockSpec((sb, TM, w), lambda b, j: (b, j, 0))
    rev = lambda w: pl.BlockSpec((sb, TM, w), lambda b, j: (b, _scan_block(j, nct, nt), 0))
    gshape = (sb, per, 2 * MLSTM_HEADS, MLSTM_CHUNK)
    gfwd = pl.BlockSpec(gshape, lambda b, j: (b, j, 0, 0))
    grev = pl.BlockSpec(gshape, lambda b, j: (b, _scan_block(j, nct, nt), 0, 0))
    nstate = sb * 2 * MLSTM_HEADS
    return pl.pallas_call(
        _mlstm_kernel,
        out_shape=(jax.ShapeDtypeStruct((bsz, ttot, nb), BF16),) * 2,
        grid=(bsz // sb, nt),
        in_specs=[fwd(nb), fwd(nb), fwd(nb), gfwd, fwd(LANES), rev(nb), rev(nb), rev(nb), grev, rev(LANES)],
        out_specs=(fwd(nb), rev(nb)),
        scratch_shapes=[pltpu.VMEM((nstate, MLSTM_DH, 2 * MLSTM_DH), F32),
                        pltpu.VMEM((nstate, SUBLANES, LANES), F32)],
        compiler_params=_params("parallel", "arbitrary"),
        name="mlstm_scan",
    )(mq, mk, mv, g_rows, g_cols, mq, mk, mv, g_rows, g_cols)


def _head_rms(x, n_heads):
    w = x.shape[1] // n_heads
    return jnp.concatenate([_rms(x[:, i * w:(i + 1) * w]) for i in range(n_heads)], axis=1)


def _mid_kernel(ctx_ref, lat_ref, gate_ref, a_ref, hf_ref, hr_ref, mo_ref, z0_ref, hn_ref, wout_ref,
                sc_ref, sh_ref, g_ref, wq_ref, wk_ref, wv_ref, wga_ref, wnq_ref, wnk_ref, wnv_ref, wz_ref,
                wgate_ref, bgate_ref,
                h_out, gq_out, gk_out, gv_out, bc_out, nq_out, nk_out, nv_out, z_out, *, nct):
    f32 = lambda r: r[...].astype(F32)
    hm = _head_rms(f32(mo_ref) * (f32(hf_ref) + f32(hr_ref)), MLSTM_HEADS) * hn_ref[...]
    cat = (jnp.concatenate([f32(a_ref), hm], axis=1) * f32(z0_ref)).astype(BF16)
    h0 = jnp.where(pl.program_id(1) < nct, ctx_ref[...], lat_ref[...])
    h = h0 + gate_ref[...] * _dot(cat, wout_ref[...])
    h_out[...] = h

    ub = _modnorm(h, g_ref[...], sc_ref[...], sh_ref[...]).astype(BF16)
    ga = _dot(ub, wga_ref[...]).astype(BF16)
    gate_pre = _dot(ga, wgate_ref[...])
    gq_out[...] = (_dot(ub, wq_ref[...]) * GLA_DK ** -0.5).astype(BF16)
    gk_out[...] = _dot(ub, wk_ref[...]).astype(BF16)
    gv_out[...] = _dot(ub, wv_ref[...]).astype(BF16)

    lg = _log_sigmoid(gate_pre + bgate_ref[...]) * (1.0 / GLA_TAU)
    hi, mid, lo = _split3(lg)
    nq_out[...] = (_dot(ub, wnq_ref[...]) * (NA_DH ** -0.5 * LOG2E)).astype(BF16)
    nk_out[...] = _dot(ub, wnk_ref[...]).astype(BF16)
    nv_out[...] = _dot(ub, wnv_ref[...]).astype(BF16)
    row = lax.broadcasted_iota(jnp.int32, (TM, TM), 0)
    col = lax.broadcasted_iota(jnp.int32, (TM, TM), 1)
    same = (row // CHUNK) == (col // CHUNK)
    nk = GLA_HEADS * GLA_DK
    for d in range(2):
        tri = jnp.where(jnp.logical_and(same, (col <= row) if d == 0 else (col >= row)), 1.0, 0.0).astype(BF16)
        cols = slice(d * nk, (d + 1) * nk)
        bc_out[:, cols] = (_dot(tri, hi[:, cols]) + _dot(tri, mid[:, cols]) + _dot(tri, lo[:, cols])) * LOG2E
    z_out[...] = _silu(_dot(ub, wz_ref[...])).astype(BF16)


def _proj1_weights(w_in, w_gate, b_gate):
    nk = GLA_HEADS * GLA_DK
    nc = GLA_HEADS * GLA_DV
    nd = NA_HEADS * NA_DH
    o = np.cumsum([0, nk, nk, nc, 2 * GLA_GATE_RANK, nd, nd, nd, nc + nd])
    r = GLA_GATE_RANK
    wgate = jnp.zeros((2 * r, 2 * nk), F32).at[:r, :nk].set(w_gate[0]).at[r:, nk:].set(w_gate[1])
    bf = lambda a: a.astype(BF16)
    names = ["wq", "wk", "wv", "wga", "wnq", "wnk", "wnv", "wz"]
    w = {n: bf(w_in[:, o[i]:o[i + 1]]) for i, n in enumerate(names)}
    w["wgate"] = bf(wgate)
    w["bgate"] = b_gate.reshape(1, 2 * nk)
    return w


def _mid(ctx, x, gate0, a, hf, hr, mo, z0, h_norm, w_out, scale, shift, norm_g, w, nct):
    bsz = x.shape[0]
    ttot = ctx.shape[1] + x.shape[1]
    nt = ttot // TM
    sb = math.gcd(bsz, TILE_BATCH)
    tile, src_ctx, src_lat, _, _, mod = _tile_specs(nct, nt, sb)
    nb = MLSTM_HEADS * MLSTM_DH
    nk = GLA_HEADS * GLA_DK
    nc = GLA_HEADS * GLA_DV
    nd = NA_HEADS * NA_DH
    wnames = ["wq", "wk", "wv", "wga", "wnq", "wnk", "wnv", "wz", "wgate", "bgate"]
    ws = [w[n] for n in wnames]
    tok = lambda width, dt: jax.ShapeDtypeStruct((bsz, ttot, width), dt)
    shared = {8, 9} | set(range(12, 13 + len(ws)))
    return pl.pallas_call(
        functools.partial(_per_element(_mid_kernel, shared), nct=nct),
        out_shape=(tok(D_MODEL, F32), tok(nk, BF16), tok(nk, BF16), tok(nc, BF16), tok(2 * nk, F32),
                   tok(nd, BF16), tok(nd, BF16), tok(nd, BF16), tok(nc + nd, BF16)),
        grid=(bsz // sb, nt),
        in_specs=[src_ctx, src_lat, mod, tile(MLA_HEADS * MLA_V), tile(nb), tile(nb), tile(nb), tile(D_MODEL),
                  _full((1, nb)), _full(w_out.shape), mod, mod, _full((1, D_MODEL))] + [_full(a.shape) for a in ws],
        out_specs=(tile(D_MODEL), tile(nk), tile(nk), tile(nc), tile(2 * nk), tile(nd), tile(nd), tile(nd),
                   tile(nc + nd)),
        compiler_params=_params("parallel", "parallel"),
        name="out0_proj1",
    )(ctx, x, gate0, a, hf, hr, mo, z0, h_norm[None], w_out.astype(BF16), scale, shift, norm_g[None], *ws)


def _gla_kernel(qf_ref, kf_ref, vf_ref, bf_ref, qr_ref, kr_ref, vr_ref, br_ref, of_ref, or_ref, s_scr):
    @pl.when(pl.program_id(1) == 0)
    def _():
        s_scr[...] = jnp.zeros_like(s_scr)

    L = CHUNK
    nchunk = TM // L
    npair = GLA_HEADS // 2
    lane = lax.broadcasted_iota(jnp.int32, (L, LANES), 1)
    rowv = lax.broadcasted_iota(jnp.int32, (L, LANES), 0)
    srow = lax.broadcasted_iota(jnp.int32, (LANES, 2 * GLA_DV), 0)
    scol = lax.broadcasted_iota(jnp.int32, (LANES, 2 * GLA_DV), 1)
    own_block = (srow < GLA_DK) == (scol < GLA_DV)
    vcol = lax.broadcasted_iota(jnp.int32, (L, 2 * GLA_DV), 1)
    e_row = lax.broadcasted_iota(jnp.int32, (LANES, LANES), 0)
    e_col = lax.broadcasted_iota(jnp.int32, (LANES, LANES), 1)
    head_ones = jnp.where((e_row < GLA_DK) == (e_col < GLA_DK), 1.0, 0.0).astype(BF16)
    leaf_idx = lax.broadcasted_iota(jnp.int32, (GLA_LEAF * GLA_LEAF, LANES), 0)
    leaf_lane = lax.broadcasted_iota(jnp.int32, (GLA_LEAF * GLA_LEAF, LANES), 1) % L
    leaf_s, leaf_t = leaf_idx // GLA_LEAF, leaf_idx % GLA_LEAF
    rep = lambda a: jnp.concatenate(
        [jnp.broadcast_to(a[s:s + 1], (GLA_LEAF, a.shape[1])) for s in range(GLA_LEAF)], axis=0)
    til = lambda a: jnp.concatenate([a] * GLA_LEAF, axis=0)

    def chunk_step(ci, carry):
        cs = []
        for bi, d in [(bi, d) for bi in range(qf_ref.shape[0]) for d in range(2)]:
            q_ref, k_ref, v_ref, b_ref, o_ref = (
                r.at[bi] for r in ((qf_ref, kf_ref, vf_ref, bf_ref, of_ref), (qr_ref, kr_ref, vr_ref, br_ref, or_ref))[d])
            cd = ci if d == 0 else nchunk - 1 - ci
            base = pl.multiple_of(cd * L, L)
            for p in range(npair):
                ls = slice(p * LANES, (p + 1) * LANES)
                bc = b_ref[pl.ds(base, L), ls]
                cs.append(dict(
                    d=d, p=p, bi=bi, base=base, o_ref=o_ref, bc=bc,
                    tau_v=rowv if d == 0 else L - 1 - rowv,
                    tau_s=(lane % L) if d == 0 else L - 1 - (lane % L),
                    q=q_ref[pl.ds(base, L), ls].astype(F32),
                    k=k_ref[pl.ds(base, L), ls].astype(F32),
                    v=v_ref[pl.ds(base, L), p * 2 * GLA_DV:(p + 1) * 2 * GLA_DV],
                    b_last=bc[L - 1:L] if d == 0 else bc[0:1],
                    state=s_scr[bi, d, p]))

        for c in cs:
            c["o_pair"] = _dot((c["q"] * jnp.exp2(c["bc"])).astype(BF16), c["state"].astype(BF16))

        for c in cs:
            c["att"] = jnp.zeros((L, LANES), F32)
        bs = L // 2
        while bs >= GLA_LEAF:
            for c in cs:
                d, bc, tau_v = c["d"], c["bc"], c["tau_v"]
                later = (tau_v // bs) % 2 == 1
                ref_b = None
                for blk in range(L // (2 * bs)):
                    tau_ref = blk * 2 * bs + bs - 1
                    idx = tau_ref if d == 0 else L - 1 - tau_ref
                    r = jnp.broadcast_to(bc[idx:idx + 1], (L, LANES))
                    ref_b = r if ref_b is None else jnp.where(tau_v // (2 * bs) == blk, r, ref_b)
                c["qs"] = jnp.where(later, c["q"] * jnp.exp2(jnp.where(later, bc - ref_b, 0.0)), 0.0).astype(BF16)
                ks = jnp.where(later, 0.0, c["k"] * jnp.exp2(jnp.where(later, 0.0, ref_b - bc)))
                c["ks2"] = jnp.concatenate([jnp.where(lane < GLA_DK, ks, 0.0), jnp.where(lane < GLA_DK, 0.0, ks)],
                                           axis=0).astype(BF16)
            for c in cs:
                same_parent = (c["tau_v"] // (2 * bs)) == (c["tau_s"] // (2 * bs))
                c["att"] = c["att"] + jnp.where(same_parent, _dot_nt(c["qs"], c["ks2"]), 0.0)
            bs //= 2

        for c in cs:
            c["leaf"] = []
        for blk in range(L // GLA_LEAF):
            rs = slice(blk * GLA_LEAF, (blk + 1) * GLA_LEAF)
            for c in cs:
                qb, bb, kb = c["q"][rs], c["bc"][rs], c["k"][rs]
                ok = (leaf_t >= leaf_s) if c["d"] == 0 else (leaf_t <= leaf_s)
                w = jnp.where(ok, til(qb) * jnp.exp2(til(bb) - rep(bb)) * rep(kb), 0.0)
                c["a"] = _dot(w.astype(BF16), head_ones)
            for c in cs:
                a = jnp.where(leaf_lane == leaf_s + blk * GLA_LEAF, c["a"], 0.0)
                acc = a[0:GLA_LEAF]
                for s in range(1, GLA_LEAF):
                    acc = acc + a[s * GLA_LEAF:(s + 1) * GLA_LEAF]
                c["leaf"].append(acc)

        for c in cs:
            v = c["v"]
            c["att"] = (c["att"] + jnp.concatenate(c["leaf"], axis=0)).astype(BF16)
            c["v_blk"] = jnp.concatenate([jnp.where(vcol < GLA_DV, v, jnp.zeros_like(v)),
                                          jnp.where(vcol < GLA_DV, jnp.zeros_like(v), v)], axis=0)
            c["ke"] = (c["k"] * jnp.exp2(c["b_last"] - c["bc"])).astype(BF16)
            c["decay_col"] = jnp.sum(
                jnp.where(e_row == e_col, jnp.broadcast_to(jnp.exp2(c["b_last"]), (LANES, LANES)), 0.0),
                axis=1, keepdims=True)
        for c in cs:
            c["o"] = c["o_pair"] + _dot(c["att"], c["v_blk"])
            c["ds"] = _dot_tn(c["ke"], c["v"])
        for c in cs:
            p = c["p"]
            c["o_ref"][pl.ds(c["base"], L), p * 2 * GLA_DV:(p + 1) * 2 * GLA_DV] = c["o"].astype(BF16)
            s_scr[c["bi"], c["d"], p] = c["decay_col"] * c["state"] + jnp.where(own_block, c["ds"], 0.0)
        return carry

    lax.fori_loop(0, nchunk, chunk_step, 0)


def _gla(gq, gk, gv, bc, nct):
    bsz, ttot, nk = gq.shape
    nc = gv.shape[2]
    nt = ttot // TM
    sb = math.gcd(bsz, GLA_BATCH)
    fwd = lambda w: pl.BlockSpec((sb, TM, w), lambda b, j: (b, j, 0))
    rev = lambda w: pl.BlockSpec((sb, TM, w), lambda b, j: (b, _scan_block(j, nct, nt), 0))
    bfwd = pl.BlockSpec((sb, TM, nk), lambda b, j: (b, j, 0))
    brev = pl.BlockSpec((sb, TM, nk), lambda b, j: (b, _scan_block(j, nct, nt), 1))
    return pl.pallas_call(
        _gla_kernel,
        out_shape=(jax.ShapeDtypeStruct((bsz, ttot, nc), BF16),) * 2,
        grid=(bsz // sb, nt),
        in_specs=[fwd(nk), fwd(nk), fwd(nc), bfwd, rev(nk), rev(nk), rev(nc), brev],
        out_specs=(fwd(nc), rev(nc)),
        scratch_shapes=[pltpu.VMEM((sb, 2, GLA_HEADS // 2, LANES, 2 * GLA_DV), F32)],
        compiler_params=_params("parallel", "arbitrary"),
        name="gla_scan",
    )(gq, gk, gv, bc, gq, gk, gv, bc)


def _na_bias_tables(rpb, rows):
    kh = NA_KH
    c = np.arange(GRID_W)
    cs = np.clip(c - NA_KW // 2, 0, GRID_W - NA_KW)
    col_ok = (c[None, :] >= cs[:, None]) & (c[None, :] < cs[:, None] + NA_KW)
    pad = GRID_W - NA_KW
    rpb_pad = jnp.pad(rpb.astype(F32) * LOG2E, ((0, 0), (0, 0), (pad, pad)))
    t1 = jnp.stack([rpb_pad[:, :, GRID_W - 1 - cq:2 * GRID_W - 1 - cq] for cq in range(GRID_W)], axis=2)
    t1 = jnp.where(col_ok, t1, NEG_BIG)
    masked = jnp.full((NA_HEADS, GRID_W, GRID_W), NEG_BIG, F32)
    per_row = []
    for r0 in (0, NA_QROWS, rows - NA_QROWS):
        kb = int(np.clip(r0 - kh // 2, 0, rows - NA_KROWS))
        for qrow in range(r0, r0 + NA_QROWS):
            ws = int(np.clip(qrow - kh // 2, 0, rows - kh))
            per_row.append(jnp.concatenate(
                [t1[:, krow - qrow + NA_KH - 1] if ws <= krow < ws + kh else masked
                 for krow in range(kb, kb + NA_KROWS)], axis=-1))
    return jnp.concatenate(per_row, axis=1)


def _na_tile(q_ref, kw_ref, vw_ref, kc_ref, vc_ref, bias_ref, h_ref, gate_ref, of_ref, or_ref, z_ref,
             gn_ref, w_ref, fn_ref, o_ref):
    nq = NA_QROWS * GRID_W
    lane = lax.broadcasted_iota(jnp.int32, (nq, LANES), 1)
    def scores(j):
        ls = slice((j // 2) * LANES, (j // 2 + 1) * LANES)
        q = q_ref[:, ls]
        qj = jnp.where((lane < NA_DH) == (j % 2 == 0), q, jnp.zeros_like(q))
        return _dot_nt(qj, kw_ref[:, ls]) + bias_ref[j], _dot_nt(qj, kc_ref[:, ls])

    outs = []
    nxt = scores(0)
    for j in range(NA_HEADS):
        s_loc, s_ctx = nxt
        if j + 1 < NA_HEADS:
            nxt = scores(j + 1)
        ls = slice((j // 2) * LANES, (j // 2 + 1) * LANES)
        m = jnp.maximum(jnp.max(s_loc, axis=-1, keepdims=True), jnp.max(s_ctx, axis=-1, keepdims=True))
        p_loc = jnp.exp2(s_loc - m)
        p_ctx = jnp.exp2(s_ctx - m)
        den = jnp.sum(p_loc, axis=-1, keepdims=True) + jnp.sum(p_ctx, axis=-1, keepdims=True)
        outs.append((_dot(p_loc.astype(BF16), vw_ref[:, ls]) + _dot(p_ctx.astype(BF16), vc_ref[:, ls])) / den)
    na = [jnp.where(lane < NA_DH, outs[2 * i], outs[2 * i + 1]) for i in range(NA_HEADS // 2)]

    f32 = lambda r: r[...].astype(F32)
    g = _head_rms(f32(of_ref) + f32(or_ref), GLA_HEADS) * gn_ref[...]
    cat = (jnp.concatenate([g] + na, axis=1) * f32(z_ref)).astype(BF16)
    h = h_ref[...] + gate_ref[...] * _dot(cat, w_ref[...])
    o_ref[...] = _rms(h) * fn_ref[...]


def _na_out1_kernel(*refs, nb):
    kv = refs[:4 * nb]
    q_ref, bias_ref, h_ref, gate_ref, of_ref, or_ref, z_ref, gn_ref, w_ref, fn_ref, o_ref = refs[4 * nb:]
    for bi in range(nb):
        _na_tile(q_ref.at[bi], *kv[4 * bi:4 * bi + 4], bias_ref, h_ref.at[bi], gate_ref.at[bi], of_ref.at[bi],
                 or_ref.at[bi], z_ref.at[bi], gn_ref, w_ref, fn_ref, o_ref.at[bi])


def _na_out1(nq, nk, nv, bias, h, gate, of, orv, z, gla_norm, w_out, final_norm, n_ctx, rows):
    bsz, ttot, nd = nq.shape
    nrb = rows // NA_QROWS
    qrows = NA_QROWS * GRID_W
    krows = NA_KROWS * GRID_W
    assert qrows == TM
    nct = n_ctx // TM
    nc = GLA_HEADS * GLA_DV
    sb = math.gcd(bsz, TILE_BATCH)

    def variant(rb):
        return jnp.where(rb == 0, 0, jnp.where(rb == nrb - 1, 2, 1))

    def key_start(rb):
        first_row = jnp.clip(rb * NA_QROWS - NA_KH // 2, 0, rows - NA_KROWS)
        return pl.multiple_of(n_ctx + first_row * GRID_W, GRID_W)

    window = lambda i: pl.BlockSpec((None, pl.Element(krows), pl.Element(nd)), lambda b, rb: (b * sb + i, key_start(rb), 0))
    context = lambda i: pl.BlockSpec((None, pl.Element(n_ctx), pl.Element(nd)), lambda b, rb: (b * sb + i, 0, 0))
    kv_specs = [spec(i) for i in range(sb) for spec in (window, window, context, context)]
    lat = lambda w: pl.BlockSpec((sb, TM, w), lambda b, rb: (b, rb + nct, 0))
    return pl.pallas_call(
        functools.partial(_na_out1_kernel, nb=sb),
        out_shape=jax.ShapeDtypeStruct((bsz, rows * GRID_W, D_MODEL), F32),
        grid=(bsz // sb, nrb),
        in_specs=kv_specs + [
            lat(nd), pl.BlockSpec((NA_HEADS, qrows, krows), lambda b, rb: (0, variant(rb), 0)),
            lat(D_MODEL), pl.BlockSpec((sb, None, 1, D_MODEL), lambda b, rb: (b, 1, 0, 0)),
            lat(nc), lat(nc), lat(nc + nd), _full((1, nc)), _full(w_out.shape), _full((1, D_MODEL))],
        out_specs=pl.BlockSpec((sb, TM, D_MODEL), lambda b, rb: (b, rb, 0)),
        compiler_params=_params("parallel", "parallel"),
        name="na_out1",
    )(*([nk, nv, nk, nv] * sb), nq, bias, h, gate, of, orv, z, gla_norm[None], w_out.astype(BF16), final_norm[None])


def kernel(x, c, ctx, c_ctx, l0_norm, l0_w_mod, l0_b_mod, l0_w_in, l0_mla_q_norm, l0_mla_w_uq, l0_mla_kv_norm, l0_mla_w_ukv, l0_mlstm_conv_w, l0_mlstm_conv_b, l0_mlstm_b_i, l0_mlstm_b_f, l0_mlstm_norm, l0_w_out, l1_norm, l1_w_mod, l1_b_mod, l1_w_in, l1_gla_w_gate, l1_gla_b_gate, l1_gla_norm, l1_na_rpb, l1_w_out, final_norm):
    bsz, seq, d = x.shape
    n_ctx = ctx.shape[1]
    rows = seq // GRID_W
    assert d == D_MODEL and seq % TM == 0 and n_ctx == TM
    assert rows % NA_QROWS == 0 and rows >= NA_KROWS and rows // NA_QROWS >= 3
    nct = n_ctx // TM

    (shift0, scale0, gate0), (shift, scale, gate) = _mod_vectors(
        c, c_ctx, ((l0_w_mod, l0_b_mod), (l1_w_mod, l1_b_mod)))

    cos_t, sin_t = _rope_tables(n_ctx, seq)
    w0 = _proj0_weights(l0_w_in, l0_mla_q_norm, l0_mla_w_uq, l0_mla_kv_norm, l0_mla_w_ukv,
                        l0_mlstm_conv_w, l0_mlstm_conv_b, l0_mlstm_b_i, l0_mlstm_b_f)
    q, k, v, mq, mk, mv, g_rows, g_cols, mo, z0 = _proj0(ctx, x, scale0, shift0, l0_norm, cos_t, sin_t, w0, nct)
    a = _mla_attention(q, k, v, nct)
    hf, hr = _mlstm(mq, mk, mv, g_rows, g_cols, nct)

    w1 = _proj1_weights(l1_w_in, l1_gla_w_gate, l1_gla_b_gate)
    h, gq, gk, gv, bc, nq, nk, nv, z = _mid(ctx, x, gate0, a, hf, hr, mo, z0, l0_mlstm_norm, l0_w_out,
                                            scale, shift, l1_norm, w1, nct)
    of, orv = _gla(gq, gk, gv, bc, nct)
    return _na_out1(nq, nk, nv, _na_bias_tables(l1_na_rpb, rows), h, gate, of, orv, z,
                    l1_gla_norm, l1_w_out, final_norm, n_ctx, rows)
```

```python
import functools
import math

import jax
import jax.numpy as jnp
import numpy as np
from jax import lax
from jax.experimental import pallas as pl
from jax.experimental.pallas import tpu as pltpu

F32 = jnp.float32
BF16 = jnp.bfloat16

D_MODEL = 1024
GRID_W = 64
EPS = 1e-6
ROPE_BASE = 10000.0

MLA_HEADS = 8
MLA_Q_RANK = 384
MLA_KV_RANK = 256
MLA_NOPE = 64
MLA_ROPE = 32
MLA_V = 64
ROPE_PAIRS = MLA_ROPE // 4
MLSTM_HEADS = 4
MLSTM_DH = 128
MLSTM_CONV = 3
GLA_HEADS = 4
GLA_DK = 64
GLA_DV = 128
GLA_GATE_RANK = 16
GLA_TAU = 16.0
NA_HEADS = 8
NA_DH = 64
NA_KH = 8
NA_KW = 16

CHUNK = 64
MLSTM_CHUNK = 128
TM = 256
LANES = 128
SUBLANES = 8
HALO = SUBLANES
NA_QROWS = 4
NA_KROWS = NA_QROWS + NA_KH - 1
GLA_LEAF = 8
MLA_HPS = 4
TILE_BATCH = 2
MLSTM_BATCH = 2
GLA_BATCH = 8
VMEM_LIMIT = 56 * 1024 * 1024
NEG_BIG = -1e30
LOG2E = 1.4426950408889634


def _dot(a, b):
    return jnp.dot(a, b, preferred_element_type=F32)


def _dot_nt(a, b):
    return lax.dot_general(a, b, (((1,), (1,)), ((), ())), preferred_element_type=F32)


def _dot_tn(a, b):
    return lax.dot_general(a, b, (((0,), (0,)), ((), ())), preferred_element_type=F32)


def _rms(x):
    return x * lax.rsqrt(jnp.mean(x * x, axis=-1, keepdims=True) + EPS)


def _sigmoid(x):
    return 1.0 / (1.0 + jnp.exp(-x))


def _silu(x):
    return x * _sigmoid(x)


def _log_sigmoid(x):
    return jnp.minimum(x, 0.0) - jnp.log(1.0 + jnp.exp(-jnp.abs(x)))


def _params(*sem):
    return pltpu.CompilerParams(dimension_semantics=sem, vmem_limit_bytes=VMEM_LIMIT)


def _full(shape):
    nd = len(shape)
    return pl.BlockSpec(shape, lambda *_: (0,) * nd)


def _mod_kernel(c_ref, w0_ref, b0_ref, w1_ref, b1_ref, o0_ref, o1_ref):
    sc = _silu(c_ref[...]).astype(BF16)
    o0_ref[...] = _dot(sc, w0_ref[...].astype(BF16)) + b0_ref[...]
    o1_ref[...] = _dot(sc, w1_ref[...].astype(BF16)) + b1_ref[...]


def _mod_vectors(c, c_ctx, mods):
    bsz, d = c.shape
    rows = -(-(bsz + 1) // SUBLANES) * SUBLANES
    cc = jnp.zeros((rows, d), F32).at[:bsz].set(c).at[bsz].set(c_ctx)
    (w0, b0), (w1, b1) = mods
    n = w0.shape[1]
    wspec = pl.BlockSpec((d, d), lambda j: (0, j))
    vspec = pl.BlockSpec((1, d), lambda j: (0, j))
    ospec = pl.BlockSpec((rows, d), lambda j: (0, j))
    per_layer = pl.pallas_call(
        _mod_kernel,
        out_shape=(jax.ShapeDtypeStruct((rows, n), F32),) * 2,
        grid=(n // d,),
        in_specs=[_full((rows, d)), wspec, vspec, wspec, vspec],
        out_specs=(ospec, ospec),
        compiler_params=_params("parallel"),
        name="modulation",
    )(cc, w0, b0[None], w1, b1[None])
    out = []
    for mod in per_layer:
        vecs = []
        for part in jnp.split(mod, 3, axis=-1):
            ctx_v = jnp.broadcast_to(part[bsz][None], (bsz, d))
            vecs.append(jnp.stack([ctx_v, part[:bsz]], axis=1)[:, :, None, :])
        out.append(vecs)
    return out


def _modnorm(x, g, scale, shift):
    return _rms(x) * g * (1.0 + scale) + shift


def _seg_scan(x, reverse, op, fill):
    width = x.shape[1]
    pos = lax.broadcasted_iota(jnp.int32, x.shape, 1) % MLSTM_CHUNK
    k = 1
    while k < MLSTM_CHUNK:
        if reverse:
            x = op(x, jnp.where(pos < MLSTM_CHUNK - k, pltpu.roll(x, width - k, axis=1), fill))
        else:
            x = op(x, jnp.where(pos >= k, pltpu.roll(x, k, axis=1), fill))
        k *= 2
    return x


def _split3(x):
    hi = x.astype(BF16)
    r1 = x - hi.astype(F32)
    mid = r1.astype(BF16)
    return hi, mid, (r1 - mid.astype(F32)).astype(BF16)


def _proj0_kernel(ctx_ref, lat_ref, hp_ref, hn_ref, sc_ref, sh_ref, g_ref, cos_ref, sin_ref,
                  wall_ref, wgt_ref, qn_ref, wuq_ref, kvn_ref, wk_ref, wv_ref, cw_ref, cb_ref, gb_ref,
                  q_out, k_out, v_out, mq_out, mk_out, mv_out, gr_out, gc_out, mo_out, z_out,
                  pbuf, *, nct, nt):
    t = pl.program_id(1)
    h = jnp.where(t < nct, ctx_ref[...], lat_ref[...])
    x = jnp.concatenate([hp_ref[...], h, hn_ref[...]], axis=0)
    ub_ext = _modnorm(x, g_ref[...], sc_ref[...], sh_ref[...]).astype(BF16)
    ub = ub_ext[HALO:HALO + TM]

    gt = _dot_nt(wgt_ref[...], ub) + gb_ref[...]
    half = MLSTM_HEADS
    per_dir = []
    for d in range(2):
        gd = gt[d * SUBLANES:(d + 1) * SUBLANES]
        bcum = _seg_scan(_log_sigmoid(gd), d == 1, jnp.add, 0.0)
        b_top = pltpu.roll(bcum, half, axis=0)
        c8 = gd - b_top
        pm8 = _seg_scan(c8, d == 1, jnp.maximum, -jnp.inf)
        per_dir.append((c8, pm8, b_top))

    cos = cos_ref[...]
    sin = sin_ref[...]
    cos_all = jnp.concatenate([cos] * MLA_HEADS, axis=1)
    sin_all = jnp.concatenate([sin] * MLA_HEADS, axis=1)

    def swap_halves(a):
        lane = lax.broadcasted_iota(jnp.int32, a.shape, 1)
        first = lane % (2 * ROPE_PAIRS) < ROPE_PAIRS
        return jnp.where(first, pltpu.roll(a, a.shape[1] - ROPE_PAIRS, axis=1), pltpu.roll(a, ROPE_PAIRS, axis=1))

    wseg = lambda name: wall_ref[:, PROJ0_COLS[name][0]:PROJ0_COLS[name][1]]
    cqkr = _dot(ub, wseg("cqkr"))
    cq = cqkr[:, :MLA_Q_RANK]
    kr = cqkr[:, MLA_Q_RANK:]
    ckv = _dot(ub, wseg("ckv"))
    pqk = _dot(ub_ext, wseg("mqk"))

    nq = (_rms(cq) * qn_ref[...]).astype(BF16)
    nkv = (_rms(ckv) * kvn_ref[...]).astype(BF16)

    q = _dot(nq, wuq_ref[...])
    kn = _dot(nkv, wk_ref[...])
    vv = _dot(nkv, wv_ref[...])

    prev_ok = t > nct
    next_ok = jnp.logical_and(t >= nct, t != nt - 1)
    row = lax.broadcasted_iota(jnp.int32, pqk.shape, 0)
    keep = jnp.logical_and(jnp.logical_or(row >= HALO, prev_ok), jnp.logical_or(row < HALO + TM, next_ok))
    pbuf[...] = jnp.where(keep, pqk, 0.0)
    cw = cw_ref[...]
    y = (pbuf[HALO - 1:HALO - 1 + TM, :] * cw[0:1] + pbuf[HALO:HALO + TM, :] * cw[1:2]
         + pbuf[HALO + 1:HALO + 1 + TM, :] * cw[2:3] + cb_ref[...])
    y = _silu(y)
    nb = MLSTM_HEADS * MLSTM_DH
    mq_out[...] = (y[:, :nb] * MLSTM_DH ** -0.5).astype(BF16)
    mk_out[...] = y[:, nb:].astype(BF16)

    mv = _dot(ub, wseg("mv"))
    mo = _dot(ub, wseg("mo"))

    q = q * cos_all + swap_halves(q) * sin_all
    q_out[...] = (q * ((MLA_NOPE + MLA_ROPE) ** -0.5 * LOG2E)).astype(BF16)
    rot = kr * cos + swap_halves(kr) * sin
    k_out[...] = (kn + jnp.concatenate([rot] * MLA_HEADS, axis=1)).astype(BF16)
    vlane = lax.broadcasted_iota(jnp.int32, (TM, MLA_HEADS * LANES), 1)
    is_value = ((vlane % LANES) < MLA_V) == ((vlane // LANES) % 2 == 0)
    v_out[...] = jnp.where(is_value, vv, 1.0).astype(BF16)

    zz = _dot(ub, wseg("z"))
    mv_out[...] = mv.astype(BF16)
    mo_out[...] = _sigmoid(mo).astype(BF16)
    z_out[...] = _silu(zz).astype(BF16)

    r8 =lax.broadcasted_iota(jnp.int32, (SUBLANES, TM), 0)
    both = lambda i: jnp.where(r8 < half, per_dir[0][i], pltpu.roll(per_dir[1][i], half, axis=0)) * LOG2E
    c_rows, pm_rows, b_rows = both(0), both(1), both(2)
    for ci in range(TM // MLSTM_CHUNK):
        gr_out[ci] = c_rows[:, ci * MLSTM_CHUNK:(ci + 1) * MLSTM_CHUNK]
    stack = jnp.concatenate([c_rows, pm_rows, b_rows, jnp.zeros((LANES - 3 * SUBLANES, TM), F32)], axis=0)
    er = lax.broadcasted_iota(jnp.int32, (TM, TM), 0)
    ec = lax.broadcasted_iota(jnp.int32, (TM, TM), 1)
    eye = jnp.where(er == ec, 1.0, 0.0).astype(BF16)
    gc_out[...] = sum(_dot_nt(eye, part) for part in _split3(stack))


def _rope_tables(n_ctx, seq):
    t = jnp.arange(seq)
    inv = 1.0 / (ROPE_BASE ** (jnp.arange(ROPE_PAIRS, dtype=F32) / ROPE_PAIRS))
    ang = jnp.concatenate([(t // GRID_W)[:, None] * inv, (t % GRID_W)[:, None] * inv], axis=-1)
    cos, sin = jnp.cos(ang), jnp.sin(ang)
    j = np.arange(MLA_ROPE)
    src = (j // (2 * ROPE_PAIRS)) * ROPE_PAIRS + (j % ROPE_PAIRS)
    sign = np.where((j % (2 * ROPE_PAIRS)) < ROPE_PAIRS, -1.0, 1.0).astype(np.float32)
    cos_full = cos[:, src]
    sin_full = sin[:, src] * sign
    cos_t = jnp.ones((n_ctx + seq, LANES), F32).at[n_ctx:, MLA_NOPE:MLA_NOPE + MLA_ROPE].set(cos_full)
    sin_t = jnp.zeros((n_ctx + seq, LANES), F32).at[n_ctx:, MLA_NOPE:MLA_NOPE + MLA_ROPE].set(sin_full)
    return cos_t, sin_t


def _segments(widths):
    out, start = {}, 0
    for name, width in widths:
        out[name] = (start, start + width)
        start += width
    return out


PROJ0_COLS = _segments([("cqkr", MLA_Q_RANK + LANES), ("ckv", MLA_KV_RANK), ("mqk", 2 * MLSTM_HEADS * MLSTM_DH),
                        ("mv", MLSTM_HEADS * MLSTM_DH), ("mo", MLSTM_HEADS * MLSTM_DH), ("z", D_MODEL)])
PROJ1_COLS = _segments([("q", GLA_HEADS * GLA_DK), ("k", GLA_HEADS * GLA_DK), ("v", GLA_HEADS * GLA_DV),
                        ("ga", LANES), ("nq", NA_HEADS * NA_DH), ("nk", NA_HEADS * NA_DH), ("nv", NA_HEADS * NA_DH),
                        ("z", GLA_HEADS * GLA_DV + NA_HEADS * NA_DH)])


def _proj0_weights(w_in, q_norm, w_uq, kv_norm, w_ukv, conv_w, conv_b, b_i, b_f):
    o = np.cumsum([0, MLA_Q_RANK, MLA_KV_RANK, MLA_ROPE] + [MLSTM_HEADS * MLSTM_DH] * 4 + [4 * MLSTM_HEADS, D_MODEL])
    dq = MLA_NOPE + MLA_ROPE
    wkr = jnp.pad(w_in[:, o[2]:o[3]], ((0, 0), (MLA_NOPE, LANES - dq)))
    wuq = jnp.pad(w_uq.reshape(MLA_Q_RANK, MLA_HEADS, dq), ((0, 0), (0, 0), (0, LANES - dq)))
    wkv = w_ukv.reshape(MLA_KV_RANK, MLA_HEADS, MLA_NOPE + MLA_V)
    wk = jnp.pad(wkv[:, :, :MLA_NOPE], ((0, 0), (0, 0), (0, LANES - MLA_NOPE)))
    zv = jnp.zeros((MLA_KV_RANK, MLA_HEADS // 2, MLA_V), F32)
    wv = jnp.concatenate([wkv[:, 0::2, MLA_NOPE:], zv, zv, wkv[:, 1::2, MLA_NOPE:]], axis=-1)
    wv = wv.reshape(MLA_KV_RANK, MLA_HEADS * LANES)
    gbias = jnp.stack([b_i, b_f], axis=1).reshape(4 * MLSTM_HEADS, 1)
    bf = lambda a: a.astype(BF16)
    wall = jnp.concatenate([w_in[:, o[0]:o[1]], wkr, w_in[:, o[1]:o[2]], w_in[:, o[3]:o[7]], w_in[:, o[8]:o[9]]], axis=1)
    assert wall.shape[1] == PROJ0_COLS["z"][1]
    return dict(
        wall=bf(wall), wgt=bf(w_in[:, o[7]:o[8]].T),
        qn=q_norm[None], wuq=bf(wuq.reshape(MLA_Q_RANK, -1)),
        kvn=kv_norm[None], wk=bf(wk.reshape(MLA_KV_RANK, -1)), wv=bf(wv),
        cw=jnp.zeros((SUBLANES, conv_w.shape[1]), F32).at[:MLSTM_CONV].set(conv_w), cb=conv_b[None], gb=gbias)


def _tile_specs(nct, nt, nb):
    tile = lambda w: pl.BlockSpec((nb, TM, w), lambda b, t: (b, t, 0))
    per = TM // HALO
    nlat = nt - nct
    src_ctx = pl.BlockSpec((nb, TM, D_MODEL), lambda b, t: (b, jnp.minimum(t, nct - 1), 0))
    src_lat = pl.BlockSpec((nb, TM, D_MODEL), lambda b, t: (b, jnp.maximum(t - nct, 0), 0))
    prev = pl.BlockSpec((nb, HALO, D_MODEL), lambda b, t: (b, jnp.maximum((t - nct) * per - 1, 0), 0))
    nxt = pl.BlockSpec((nb, HALO, D_MODEL),
                       lambda b, t: (b, jnp.clip((t - nct + 1) * per, 0, nlat * per - 1), 0))
    mod = pl.BlockSpec((nb, None, 1, D_MODEL), lambda b, t: (b, (t >= nct).astype(jnp.int32), 0, 0))
    return tile, src_ctx, src_lat, prev, nxt, mod


def _per_element(tile_kernel, shared):
    def kernel(*refs, **kw):
        for bi in range(refs[0].shape[0]):
            tile_kernel(*[r if i in shared else r.at[bi] for i, r in enumerate(refs)], **kw)
    return kernel


def _proj0(ctx, x, scale, shift, norm_g, cos_t, sin_t, w, nct):
    bsz = x.shape[0]
    ttot = ctx.shape[1] + x.shape[1]
    nt = ttot // TM
    nb = MLSTM_HEADS * MLSTM_DH
    sb = math.gcd(bsz, TILE_BATCH)
    tile, src_ctx, src_lat, prev, nxt, mod = _tile_specs(nct, nt, sb)
    tab = pl.BlockSpec((TM, LANES), lambda b, t: (t, 0))
    wnames = ["wall", "wgt", "qn", "wuq", "kvn", "wk", "wv", "cw", "cb", "gb"]
    ws = [w[n] for n in wnames]
    tok = lambda width, dt: jax.ShapeDtypeStruct((bsz, ttot, width), dt)
    out_shape = (tok(MLA_HEADS * LANES, BF16), tok(MLA_HEADS * LANES, BF16), tok(MLA_HEADS * LANES, BF16),
                 tok(nb, BF16), tok(nb, BF16), tok(nb, BF16),
                 jax.ShapeDtypeStruct((bsz, ttot // MLSTM_CHUNK, 2 * MLSTM_HEADS, MLSTM_CHUNK), F32), tok(LANES, F32),
                 tok(nb, BF16), tok(D_MODEL, BF16))
    out_specs = (tile(MLA_HEADS * LANES), tile(MLA_HEADS * LANES), tile(MLA_HEADS * LANES),
                 tile(nb), tile(nb), tile(nb),
                 pl.BlockSpec((sb, TM // MLSTM_CHUNK, 2 * MLSTM_HEADS, MLSTM_CHUNK), lambda b, t: (b, t, 0, 0)),
                 tile(LANES),
                 tile(nb), tile(D_MODEL))
    n_blocked, n_out = 6, len(out_shape)
    n_in = n_blocked + 3 + len(ws)
    shared = set(range(n_blocked, n_in)) | {n_in + n_out}
    return pl.pallas_call(
        functools.partial(_per_element(_proj0_kernel, shared), nct=nct, nt=nt),
        out_shape=out_shape,
        grid=(bsz // sb, nt),
        in_specs=[src_ctx, src_lat, prev, nxt, mod, mod, _full((1, D_MODEL)), tab, tab] + [_full(a.shape) for a in ws],
        out_specs=out_specs,
        scratch_shapes=[pltpu.VMEM((TM + 2 * HALO, 2 * nb), F32)],
        compiler_params=_params("parallel", "parallel"),
        name="proj0",
    )(ctx, x, x, x, scale, shift, norm_g[None], cos_t, sin_t, *ws)


def _mla_kernel(q_ref, k_ref, v_ref, o_ref, *, nct):
    t = pl.program_id(2)
    lane = lax.broadcasted_iota(jnp.int32, (TM, LANES), 1)

    def attend(nkeys):
        qk = lambda j: _dot_nt(q_ref[:, j * LANES:(j + 1) * LANES], k_ref[0:nkeys, j * LANES:(j + 1) * LANES])
        outs = []
        s_next = qk(0)
        for j in range(MLA_HPS):
            s = s_next
            if j + 1 < MLA_HPS:
                s_next = qk(j + 1)
            p = jnp.exp2(s - jnp.max(s, axis=-1, keepdims=True))
            o = _dot(p.astype(BF16), v_ref[0:nkeys, j * LANES:(j + 1) * LANES])
            outs.append(o / pltpu.roll(o, MLA_V, axis=1))
        for j in range(MLA_HPS // 2):
            o_ref[:, j * LANES:(j + 1) * LANES] = jnp.where(lane < MLA_V, outs[2 * j], outs[2 * j + 1]).astype(BF16)

    @pl.when(t < nct)
    def _():
        attend(nct * TM)

    @pl.when(t >= nct)
    def _():
        attend(k_ref.shape[0])


def _mla_attention(q, k, v, nct):
    bsz, ttot, _ = q.shape
    nt = ttot // TM
    hw = MLA_HPS * LANES
    return pl.pallas_call(
        functools.partial(_mla_kernel, nct=nct),
        out_shape=jax.ShapeDtypeStruct((bsz, ttot, MLA_HEADS * MLA_V), BF16),
        grid=(bsz, MLA_HEADS // MLA_HPS, nt),
        in_specs=[pl.BlockSpec((None, TM, hw), lambda b, p, t: (b, t, p)),
                  pl.BlockSpec((None, ttot, hw), lambda b, p, t: (b, 0, p)),
                  pl.BlockSpec((None, ttot, hw), lambda b, p, t: (b, 0, p))],
        out_specs=pl.BlockSpec((None, TM, hw // 2), lambda b, p, t: (b, t, p)),
        compiler_params=_params("parallel", "parallel", "parallel"),
        name="mla_attention",
    )(q, k, v)


def _scan_block(j, nct, nt):
    return jnp.where(j < nct, nct - 1 - j, nt - 1 - (j - nct))


def _mlstm_kernel(qf_ref, kf_ref, vf_ref, grf_ref, gcf_ref, qr_ref, kr_ref, vr_ref, grr_ref, gcr_ref,
                  hf_ref, hr_ref, c_scr, m_scr):
    @pl.when(pl.program_id(1) == 0)
    def _():
        c_scr[...] = jnp.zeros_like(c_scr)
        m_scr[...] = jnp.zeros_like(m_scr)

    L = MLSTM_CHUNK
    nchunk = TM // L
    nh = MLSTM_HEADS
    nchain = 2 * nh
    row = lax.broadcasted_iota(jnp.int32, (L, L), 0)
    col = lax.broadcasted_iota(jnp.int32, (L, L), 1)
    ones = jnp.ones((L, MLSTM_DH), BF16)
    wide = lambda a: jnp.broadcast_to(a, (L, MLSTM_DH))
    twice = lambda a: jnp.concatenate([a, a], axis=1)
    refs = ((qf_ref, kf_ref, vf_ref, grf_ref, gcf_ref, hf_ref), (qr_ref, kr_ref, vr_ref, grr_ref, gcr_ref, hr_ref))
    nbatch = qf_ref.shape[0]

    def load(ci):
        cs = []
        for bi, d in [(bi, d) for bi in range(nbatch) for d in range(2)]:
            q_ref, k_ref, v_ref, gr_ref, gc_ref, o_ref = (r.at[bi] for r in refs[d])
            cd = ci if d == 0 else nchunk - 1 - ci
            rs = slice(cd * L, (cd + 1) * L)
            g_rows = gr_ref[cd]
            g_cols = gc_ref[rs, :]
            last = L - 1 if d == 0 else 0
            for hd in range(nh):
                ch = d * nh + hd
                hs = slice(hd * MLSTM_DH, (hd + 1) * MLSTM_DH)
                cs.append(dict(
                    rs=rs, hs=hs, o_ref=o_ref, causal=(col <= row) if d == 0 else (col >= row),
                    c_row=g_rows[ch:ch + 1], c_col=g_cols[:, ch:ch + 1],
                    pm_col=g_cols[:, nchain + ch:nchain + ch + 1], b_col=g_cols[:, 2 * nchain + ch:2 * nchain + ch + 1],
                    pm_last=g_cols[last:last + 1, nchain + ch:nchain + ch + 1],
                    b_last=g_cols[last:last + 1, 2 * nchain + ch:2 * nchain + ch + 1],
                    q=q_ref[rs, hs], k=k_ref[rs, hs],
                    v_ext=jnp.concatenate([v_ref[rs, hs], ones], axis=1)))
        return cs

    def prep_scores(cs):
        for c in cs:
            c["s"] = _dot_nt(c["q"], c["k"])

    def prep_weights(cs):
        for c in cs:
            c["pm_w"] = wide(c["pm_col"])
            c["b_w"] = wide(c["b_col"])
            c["w"] = jnp.exp2(jnp.where(c["causal"], c["c_row"] - c["pm_w"][:, :L], -jnp.inf))
            c["kwf"] = jnp.exp2(wide(c["c_col"] - c["pm_last"]))

    def prep_operands(cs):
        for c in cs:
            c["p"] = (c["s"] * c["w"]).astype(BF16)
            c["kw"] = (c["kwf"] * c["k"].astype(F32)).astype(BF16)

    def prep_products(cs):
        for c in cs:
            c["o1"] = _dot(c["p"], c["v_ext"])
            c["dc"] = _dot_tn(c["kw"], c["v_ext"])

    def scan_read(cs):
        for ch, c in enumerate(cs):
            c["qc"] = _dot(c["q"], state[ch].astype(BF16))

    def scan_factors(cs):
        for ch, c in enumerate(cs):
            g_w = jnp.maximum(m[ch], c["pm_w"])
            c["e_intra"] = jnp.exp2(c["pm_w"] - g_w)
            c["e_inter"] = jnp.exp2(m[ch] - g_w)
            c["floor"] = jnp.exp2(-(c["b_w"] + g_w))
            g_last = jnp.maximum(m[ch], c["pm_last"])
            c["keep"] = jnp.exp2(m[ch] - g_last)
            c["gain"] = jnp.exp2(c["pm_last"] - g_last)
            c["m_new"] = c["b_last"] + g_last

    def scan_update(cs):
        for ch, c in enumerate(cs):
            o = twice(c["e_intra"]) * c["o1"] + twice(c["e_inter"]) * c["qc"]
            h = o[:, :MLSTM_DH] / jnp.maximum(jnp.abs(o[:, MLSTM_DH:]), c["floor"])
            c["o_ref"][c["rs"], c["hs"]] = h.astype(BF16)
            state[ch] = c["keep"] * state[ch] + c["gain"] * c["dc"]
            m[ch] = c["m_new"]

    nstate = nbatch * nchain
    state = [c_scr[ch] for ch in range(nstate)]
    m = [m_scr[ch, 0:1, 0:1] for ch in range(nstate)]
    chunks = [load(ci) for ci in range(nchunk)]
    everything = [c for cs in chunks for c in cs]
    for stage in (prep_scores, prep_weights, prep_operands):
        stage(everything)
    prep_products(chunks[0])
    for i, cs in enumerate(chunks):
        scan_read(cs)
        if i + 1 < nchunk:
            prep_products(chunks[i + 1])
        scan_factors(cs)
        scan_update(cs)
    for ch in range(nstate):
        c_scr[ch] = state[ch]
        m_scr[ch] = jnp.broadcast_to(m[ch], m_scr.shape[1:])


def _mlstm(mq, mk, mv, g_rows, g_cols, nct):
    bsz, ttot, nb = mq.shape
    nt = ttot // TM
    per = TM // MLSTM_CHUNK
    sb = math.gcd(bsz, MLSTM_BATCH)
    fwd = lambda w: pl.BlockSpec((sb, TM, w), lambda b, j: (b, j, 0))
    rev = lambda w: pl.BlockSpec((sb, TM, w), lambda b, j: (b, _scan_block(j, nct, nt), 0))
    gshape = (sb, per, 2 * MLSTM_HEADS, MLSTM_CHUNK)
    gfwd = pl.BlockSpec(gshape, lambda b, j: (b, j, 0, 0))
    grev = pl.BlockSpec(gshape, lambda b, j: (b, _scan_block(j, nct, nt), 0, 0))
    nstate = sb * 2 * MLSTM_HEADS
    return pl.pallas_call(
        _mlstm_kernel,
        out_shape=(jax.ShapeDtypeStruct((bsz, ttot, nb), BF16),) * 2,
        grid=(bsz // sb, nt),
        in_specs=[fwd(nb), fwd(nb), fwd(nb), gfwd, fwd(LANES), rev(nb), rev(nb), rev(nb), grev, rev(LANES)],
        out_specs=(fwd(nb), rev(nb)),
        scratch_shapes=[pltpu.VMEM((nstate, MLSTM_DH, 2 * MLSTM_DH), F32),
                        pltpu.VMEM((nstate, SUBLANES, LANES), F32)],
        compiler_params=_params("parallel", "arbitrary"),
        name="mlstm_scan",
    )(mq, mk, mv, g_rows, g_cols, mq, mk, mv, g_rows, g_cols)


def _head_rms(x, n_heads):
    w = x.shape[1] // n_heads
    return jnp.concatenate([_rms(x[:, i * w:(i + 1) * w]) for i in range(n_heads)], axis=1)


def _mid_kernel(ctx_ref, lat_ref, gate_ref, a_ref, hf_ref, hr_ref, mo_ref, z0_ref, hn_ref, wout_ref,
                sc_ref, sh_ref, g_ref, wall_ref, wgate_ref, bgate_ref,
                h_out, gq_out, gk_out, gv_out, bc_out, nq_out, nk_out, nv_out, z_out, *, nct):
    f32 = lambda r: r[...].astype(F32)
    hm = _head_rms(f32(mo_ref) * (f32(hf_ref) + f32(hr_ref)), MLSTM_HEADS) * hn_ref[...]
    cat = (jnp.concatenate([f32(a_ref), hm], axis=1) * f32(z0_ref)).astype(BF16)
    h0 = jnp.where(pl.program_id(1) < nct, ctx_ref[...], lat_ref[...])
    h = h0 + gate_ref[...] * _dot(cat, wout_ref[...])
    h_out[...] = h

    ub = _modnorm(h, g_ref[...], sc_ref[...], sh_ref[...]).astype(BF16)
    wseg = lambda name: wall_ref[:, PROJ1_COLS[name][0]:PROJ1_COLS[name][1]]
    ga = _dot(ub, wseg("ga")).astype(BF16)
    gate_pre = _dot(ga, wgate_ref[...])
    gq_out[...] = (_dot(ub, wseg("q")) * GLA_DK ** -0.5).astype(BF16)
    gk_out[...] = _dot(ub, wseg("k")).astype(BF16)
    gv_out[...] = _dot(ub, wseg("v")).astype(BF16)

    lg = _log_sigmoid(gate_pre + bgate_ref[...]) * (1.0 / GLA_TAU)
    hi, mid, lo = _split3(lg)
    nq_out[...] = (_dot(ub, wseg("nq")) * (NA_DH ** -0.5 * LOG2E)).astype(BF16)
    nk_out[...] = _dot(ub, wseg("nk")).astype(BF16)
    nv_out[...] = _dot(ub, wseg("nv")).astype(BF16)
    row = lax.broadcasted_iota(jnp.int32, (TM, TM), 0)
    col = lax.broadcasted_iota(jnp.int32, (TM, TM), 1)
    same = (row // CHUNK) == (col // CHUNK)
    nk = GLA_HEADS * GLA_DK
    for d in range(2):
        tri = jnp.where(jnp.logical_and(same, (col <= row) if d == 0 else (col >= row)), 1.0, 0.0).astype(BF16)
        cols = slice(d * nk, (d + 1) * nk)
        bc_out[:, cols] = (_dot(tri, hi[:, cols]) + _dot(tri, mid[:, cols]) + _dot(tri, lo[:, cols])) * LOG2E
    z_out[...] = _silu(_dot(ub, wseg("z"))).astype(BF16)


def _proj1_weights(w_in, w_gate, b_gate):
    nk = GLA_HEADS * GLA_DK
    nc = GLA_HEADS * GLA_DV
    nd = NA_HEADS * NA_DH
    o = np.cumsum([0, nk, nk, nc, 2 * GLA_GATE_RANK, nd, nd, nd, nc + nd])
    r = GLA_GATE_RANK
    gpad = LANES - 2 * r
    wall = jnp.concatenate([w_in[:, :o[4]], jnp.zeros((w_in.shape[0], gpad), F32), w_in[:, o[4]:]], axis=1)
    assert wall.shape[1] == PROJ1_COLS["z"][1]
    wgate = jnp.zeros((LANES, 2 * nk), F32).at[:r, :nk].set(w_gate[0]).at[r:2 * r, nk:].set(w_gate[1])
    return dict(wall=wall.astype(BF16), wgate=wgate.astype(BF16), bgate=b_gate.reshape(1, 2 * nk))


def _mid(ctx, x, gate0, a, hf, hr, mo, z0, h_norm, w_out, scale, shift, norm_g, w, nct):
    bsz = x.shape[0]
    ttot = ctx.shape[1] + x.shape[1]
    nt = ttot // TM
    sb = math.gcd(bsz, TILE_BATCH)
    tile, src_ctx, src_lat, _, _, mod = _tile_specs(nct, nt, sb)
    nb = MLSTM_HEADS * MLSTM_DH
    nk = GLA_HEADS * GLA_DK
    nc = GLA_HEADS * GLA_DV
    nd = NA_HEADS * NA_DH
    ws = [w[n] for n in ("wall", "wgate", "bgate")]
    tok = lambda width, dt: jax.ShapeDtypeStruct((bsz, ttot, width), dt)
    shared = {8, 9} | set(range(12, 13 + len(ws)))
    return pl.pallas_call(
        functools.partial(_per_element(_mid_kernel, shared), nct=nct),
        out_shape=(tok(D_MODEL, F32), tok(nk, BF16), tok(nk, BF16), tok(nc, BF16), tok(2 * nk, F32),
                   tok(nd, BF16), tok(nd, BF16), tok(nd, BF16), tok(nc + nd, BF16)),
        grid=(bsz // sb, nt),
        in_specs=[src_ctx, src_lat, mod, tile(MLA_HEADS * MLA_V), tile(nb), tile(nb), tile(nb), tile(D_MODEL),
                  _full((1, nb)), _full(w_out.shape), mod, mod, _full((1, D_MODEL))] + [_full(a.shape) for a in ws],
        out_specs=(tile(D_MODEL), tile(nk), tile(nk), tile(nc), tile(2 * nk), tile(nd), tile(nd), tile(nd),
                   tile(nc + nd)),
        compiler_params=_params("parallel", "parallel"),
        name="out0_proj1",
    )(ctx, x, gate0, a, hf, hr, mo, z0, h_norm[None], w_out.astype(BF16), scale, shift, norm_g[None], *ws)


def _gla_kernel(qf_ref, kf_ref, vf_ref, bf_ref, qr_ref, kr_ref, vr_ref, br_ref, of_ref, or_ref, s_scr):
    @pl.when(pl.program_id(1) == 0)
    def _():
        s_scr[...] = jnp.zeros_like(s_scr)

    L = CHUNK
    nchunk = TM // L
    npair = GLA_HEADS // 2
    lane = lax.broadcasted_iota(jnp.int32, (L, LANES), 1)
    rowv = lax.broadcasted_iota(jnp.int32, (L, LANES), 0)
    srow = lax.broadcasted_iota(jnp.int32, (LANES, 2 * GLA_DV), 0)
    scol = lax.broadcasted_iota(jnp.int32, (LANES, 2 * GLA_DV), 1)
    own_block = (srow < GLA_DK) == (scol < GLA_DV)
    vcol = lax.broadcasted_iota(jnp.int32, (L, 2 * GLA_DV), 1)
    e_row = lax.broadcasted_iota(jnp.int32, (LANES, LANES), 0)
    e_col = lax.broadcasted_iota(jnp.int32, (LANES, LANES), 1)
    head_ones = jnp.where((e_row < GLA_DK) == (e_col < GLA_DK), 1.0, 0.0).astype(BF16)
    leaf_idx = lax.broadcasted_iota(jnp.int32, (GLA_LEAF * GLA_LEAF, LANES), 0)
    leaf_lane = lax.broadcasted_iota(jnp.int32, (GLA_LEAF * GLA_LEAF, LANES), 1) % L
    leaf_s, leaf_t = leaf_idx // GLA_LEAF, leaf_idx % GLA_LEAF
    rep = lambda a: jnp.concatenate(
        [jnp.broadcast_to(a[s:s + 1], (GLA_LEAF, a.shape[1])) for s in range(GLA_LEAF)], axis=0)
    til = lambda a: jnp.concatenate([a] * GLA_LEAF, axis=0)

    def chunk_step(ci, carry):
        cs = []
        for bi, d in [(bi, d) for bi in range(qf_ref.shape[0]) for d in range(2)]:
            q_ref, k_ref, v_ref, b_ref, o_ref = (
                r.at[bi] for r in ((qf_ref, kf_ref, vf_ref, bf_ref, of_ref), (qr_ref, kr_ref, vr_ref, br_ref, or_ref))[d])
            cd = ci if d == 0 else nchunk - 1 - ci
            base = pl.multiple_of(cd * L, L)
            for p in range(npair):
                ls = slice(p * LANES, (p + 1) * LANES)
                bc = b_ref[pl.ds(base, L), ls]
                cs.append(dict(
                    d=d, p=p, bi=bi, base=base, o_ref=o_ref, bc=bc,
                    tau_v=rowv if d == 0 else L - 1 - rowv,
                    tau_s=(lane % L) if d == 0 else L - 1 - (lane % L),
                    q=q_ref[pl.ds(base, L), ls].astype(F32),
                    k=k_ref[pl.ds(base, L), ls].astype(F32),
                    v=v_ref[pl.ds(base, L), p * 2 * GLA_DV:(p + 1) * 2 * GLA_DV],
                    b_last=bc[L - 1:L] if d == 0 else bc[0:1],
                    state=s_scr[bi, d, p]))

        for c in cs:
            c["o_pair"] = _dot((c["q"] * jnp.exp2(c["bc"])).astype(BF16), c["state"].astype(BF16))

        for c in cs:
            c["att"] = jnp.zeros((L, LANES), F32)
        bs = L // 2
        while bs >= GLA_LEAF:
            for c in cs:
                d, bc, tau_v = c["d"], c["bc"], c["tau_v"]
                later = (tau_v // bs) % 2 == 1
                ref_b = None
                for blk in range(L // (2 * bs)):
                    tau_ref = blk * 2 * bs + bs - 1
                    idx = tau_ref if d == 0 else L - 1 - tau_ref
                    r = jnp.broadcast_to(bc[idx:idx + 1], (L, LANES))
                    ref_b = r if ref_b is None else jnp.where(tau_v // (2 * bs) == blk, r, ref_b)
                c["qs"] = jnp.where(later, c["q"] * jnp.exp2(jnp.where(later, bc - ref_b, 0.0)), 0.0).astype(BF16)
                ks = jnp.where(later, 0.0, c["k"] * jnp.exp2(jnp.where(later, 0.0, ref_b - bc)))
                c["ks2"] = jnp.concatenate([jnp.where(lane < GLA_DK, ks, 0.0), jnp.where(lane < GLA_DK, 0.0, ks)],
                                           axis=0).astype(BF16)
            for c in cs:
                same_parent = (c["tau_v"] // (2 * bs)) == (c["tau_s"] // (2 * bs))
                c["att"] = c["att"] + jnp.where(same_parent, _dot_nt(c["qs"], c["ks2"]), 0.0)
            bs //= 2

        for c in cs:
            c["leaf"] = []
        for blk in range(L // GLA_LEAF):
            rs = slice(blk * GLA_LEAF, (blk + 1) * GLA_LEAF)
            for c in cs:
                qb, bb, kb = c["q"][rs], c["bc"][rs], c["k"][rs]
                ok = (leaf_t >= leaf_s) if c["d"] == 0 else (leaf_t <= leaf_s)
                w = jnp.where(ok, til(qb) * jnp.exp2(til(bb) - rep(bb)) * rep(kb), 0.0)
                c["a"] = _dot(w.astype(BF16), head_ones)
            for c in cs:
                a = jnp.where(leaf_lane == leaf_s + blk * GLA_LEAF, c["a"], 0.0)
                acc = a[0:GLA_LEAF]
                for s in range(1, GLA_LEAF):
                    acc = acc + a[s * GLA_LEAF:(s + 1) * GLA_LEAF]
                c["leaf"].append(acc)

        for c in cs:
            v = c["v"]
            c["att"] = (c["att"] + jnp.concatenate(c["leaf"], axis=0)).astype(BF16)
            c["v_blk"] = jnp.concatenate([jnp.where(vcol < GLA_DV, v, jnp.zeros_like(v)),
                                          jnp.where(vcol < GLA_DV, jnp.zeros_like(v), v)], axis=0)
            c["ke"] = (c["k"] * jnp.exp2(c["b_last"] - c["bc"])).astype(BF16)
            c["decay_col"] = jnp.sum(
                jnp.where(e_row == e_col, jnp.broadcast_to(jnp.exp2(c["b_last"]), (LANES, LANES)), 0.0),
                axis=1, keepdims=True)
        for c in cs:
            c["o"] = c["o_pair"] + _dot(c["att"], c["v_blk"])
            c["ds"] = _dot_tn(c["ke"], c["v"])
        for c in cs:
            p = c["p"]
            c["o_ref"][pl.ds(c["base"], L), p * 2 * GLA_DV:(p + 1) * 2 * GLA_DV] = c["o"].astype(BF16)
            s_scr[c["bi"], c["d"], p] = c["decay_col"] * c["state"] + jnp.where(own_block, c["ds"], 0.0)
        return carry

    lax.fori_loop(0, nchunk, chunk_step, 0)


def _gla(gq, gk, gv, bc, nct):
    bsz, ttot, nk = gq.shape
    nc = gv.shape[2]
    nt = ttot // TM
    sb = math.gcd(bsz, GLA_BATCH)
    fwd = lambda w: pl.BlockSpec((sb, TM, w), lambda b, j: (b, j, 0))
    rev = lambda w: pl.BlockSpec((sb, TM, w), lambda b, j: (b, _scan_block(j, nct, nt), 0))
    bfwd = pl.BlockSpec((sb, TM, nk), lambda b, j: (b, j, 0))
    brev = pl.BlockSpec((sb, TM, nk), lambda b, j: (b, _scan_block(j, nct, nt), 1))
    return pl.pallas_call(
        _gla_kernel,
        out_shape=(jax.ShapeDtypeStruct((bsz, ttot, nc), BF16),) * 2,
        grid=(bsz // sb, nt),
        in_specs=[fwd(nk), fwd(nk), fwd(nc), bfwd, rev(nk), rev(nk), rev(nc), brev],
        out_specs=(fwd(nc), rev(nc)),
        scratch_shapes=[pltpu.VMEM((sb, 2, GLA_HEADS // 2, LANES, 2 * GLA_DV), F32)],
        compiler_params=_params("parallel", "arbitrary"),
        name="gla_scan",
    )(gq, gk, gv, bc, gq, gk, gv, bc)


def _na_bias_tables(rpb, rows):
    kh = NA_KH
    c = np.arange(GRID_W)
    cs = np.clip(c - NA_KW // 2, 0, GRID_W - NA_KW)
    col_ok = (c[None, :] >= cs[:, None]) & (c[None, :] < cs[:, None] + NA_KW)
    pad = GRID_W - NA_KW
    rpb_pad = jnp.pad(rpb.astype(F32) * LOG2E, ((0, 0), (0, 0), (pad, pad)))
    period = 2 * GRID_W
    flat = jnp.tile(jnp.pad(rpb_pad, ((0, 0), (0, 0), (0, 1))), (1, 1, GRID_W))[:, :, :GRID_W * (period - 1)]
    t1 = flat.reshape(NA_HEADS, 2 * NA_KH - 1, GRID_W, period - 1)[..., GRID_W - 1:]
    t1 = jnp.where(col_ok, t1, NEG_BIG)
    masked = jnp.full((NA_HEADS, GRID_W, GRID_W), NEG_BIG, F32)
    per_row = []
    for r0 in (0, NA_QROWS, rows - NA_QROWS):
        kb = int(np.clip(r0 - kh // 2, 0, rows - NA_KROWS))
        for qrow in range(r0, r0 + NA_QROWS):
            ws = int(np.clip(qrow - kh // 2, 0, rows - kh))
            per_row.append(jnp.concatenate(
                [t1[:, krow - qrow + NA_KH - 1] if ws <= krow < ws + kh else masked
                 for krow in range(kb, kb + NA_KROWS)], axis=-1))
    return jnp.concatenate(per_row, axis=1)


def _na_tile(q_ref, kw_ref, vw_ref, kc_ref, vc_ref, bias_ref, h_ref, gate_ref, of_ref, or_ref, z_ref,
             gn_ref, w_ref, fn_ref, o_ref):
    nq = NA_QROWS * GRID_W
    lane = lax.broadcasted_iota(jnp.int32, (nq, LANES), 1)
    def scores(j):
        ls = slice((j // 2) * LANES, (j // 2 + 1) * LANES)
        q = q_ref[:, ls]
        qj = jnp.where((lane < NA_DH) == (j % 2 == 0), q, jnp.zeros_like(q))
        return _dot_nt(qj, kw_ref[:, ls]) + bias_ref[j], _dot_nt(qj, kc_ref[:, ls])

    outs = []
    nxt = scores(0)
    for j in range(NA_HEADS):
        s_loc, s_ctx = nxt
        if j + 1 < NA_HEADS:
            nxt = scores(j + 1)
        ls = slice((j // 2) * LANES, (j // 2 + 1) * LANES)
        m = jnp.maximum(jnp.max(s_loc, axis=-1, keepdims=True), jnp.max(s_ctx, axis=-1, keepdims=True))
        p_loc = jnp.exp2(s_loc - m)
        p_ctx = jnp.exp2(s_ctx - m)
        den = jnp.sum(p_loc, axis=-1, keepdims=True) + jnp.sum(p_ctx, axis=-1, keepdims=True)
        outs.append((_dot(p_loc.astype(BF16), vw_ref[:, ls]) + _dot(p_ctx.astype(BF16), vc_ref[:, ls])) / den)
    na = [jnp.where(lane < NA_DH, outs[2 * i], outs[2 * i + 1]) for i in range(NA_HEADS // 2)]

    f32 = lambda r: r[...].astype(F32)
    g = _head_rms(f32(of_ref) + f32(or_ref), GLA_HEADS) * gn_ref[...]
    cat = (jnp.concatenate([g] + na, axis=1) * f32(z_ref)).astype(BF16)
    h = h_ref[...] + gate_ref[...] * _dot(cat, w_ref[...])
    o_ref[...] = _rms(h) * fn_ref[...]


def _na_out1_kernel(*refs, nb):
    kv = refs[:4 * nb]
    q_ref, bias_ref, h_ref, gate_ref, of_ref, or_ref, z_ref, gn_ref, w_ref, fn_ref, o_ref = refs[4 * nb:]
    for bi in range(nb):
        _na_tile(q_ref.at[bi], *kv[4 * bi:4 * bi + 4], bias_ref, h_ref.at[bi], gate_ref.at[bi], of_ref.at[bi],
                 or_ref.at[bi], z_ref.at[bi], gn_ref, w_ref, fn_ref, o_ref.at[bi])


def _na_out1(nq, nk, nv, bias, h, gate, of, orv, z, gla_norm, w_out, final_norm, n_ctx, rows):
    bsz, ttot, nd = nq.shape
    nrb = rows // NA_QROWS
    qrows = NA_QROWS * GRID_W
    krows = NA_KROWS * GRID_W
    assert qrows == TM
    nct = n_ctx // TM
    nc = GLA_HEADS * GLA_DV
    sb = math.gcd(bsz, TILE_BATCH)

    def variant(rb):
        return jnp.where(rb == 0, 0, jnp.where(rb == nrb - 1, 2, 1))

    def key_start(rb):
        first_row = jnp.clip(rb * NA_QROWS - NA_KH // 2, 0, rows - NA_KROWS)
        return pl.multiple_of(n_ctx + first_row * GRID_W, GRID_W)

    window = lambda i: pl.BlockSpec((None, pl.Element(krows), pl.Element(nd)), lambda b, rb: (b * sb + i, key_start(rb), 0))
    context = lambda i: pl.BlockSpec((None, pl.Element(n_ctx), pl.Element(nd)), lambda b, rb: (b * sb + i, 0, 0))
    kv_specs = [spec(i) for i in range(sb) for spec in (window, window, context, context)]
    lat = lambda w: pl.BlockSpec((sb, TM, w), lambda b, rb: (b, rb + nct, 0))
    return pl.pallas_call(
        functools.partial(_na_out1_kernel, nb=sb),
        out_shape=jax.ShapeDtypeStruct((bsz, rows * GRID_W, D_MODEL), F32),
        grid=(bsz // sb, nrb),
        in_specs=kv_specs + [
            lat(nd), pl.BlockSpec((NA_HEADS, qrows, krows), lambda b, rb: (0, variant(rb), 0)),
            lat(D_MODEL), pl.BlockSpec((sb, None, 1, D_MODEL), lambda b, rb: (b, 1, 0, 0)),
            lat(nc), lat(nc), lat(nc + nd), _full((1, nc)), _full(w_out.shape), _full((1, D_MODEL))],
        out_specs=pl.BlockSpec((sb, TM, D_MODEL), lambda b, rb: (b, rb, 0)),
        compiler_params=_params("parallel", "parallel"),
        name="na_out1",
    )(*([nk, nv, nk, nv] * sb), nq, bias, h, gate, of, orv, z, gla_norm[None], w_out.astype(BF16), final_norm[None])


def kernel(x, c, ctx, c_ctx, l0_norm, l0_w_mod, l0_b_mod, l0_w_in, l0_mla_q_norm, l0_mla_w_uq, l0_mla_kv_norm, l0_mla_w_ukv, l0_mlstm_conv_w, l0_mlstm_conv_b, l0_mlstm_b_i, l0_mlstm_b_f, l0_mlstm_norm, l0_w_out, l1_norm, l1_w_mod, l1_b_mod, l1_w_in, l1_gla_w_gate, l1_gla_b_gate, l1_gla_norm, l1_na_rpb, l1_w_out, final_norm):
    bsz, seq, d = x.shape
    n_ctx = ctx.shape[1]
    rows = seq // GRID_W
    assert d == D_MODEL and seq % TM == 0 and n_ctx == TM
    assert rows % NA_QROWS == 0 and rows >= NA_KROWS and rows // NA_QROWS >= 3
    nct = n_ctx // TM

    (shift0, scale0, gate0), (shift, scale, gate) = _mod_vectors(
        c, c_ctx, ((l0_w_mod, l0_b_mod), (l1_w_mod, l1_b_mod)))

    cos_t, sin_t = _rope_tables(n_ctx, seq)
    w0 = _proj0_weights(l0_w_in, l0_mla_q_norm, l0_mla_w_uq, l0_mla_kv_norm, l0_mla_w_ukv,
                        l0_mlstm_conv_w, l0_mlstm_conv_b, l0_mlstm_b_i, l0_mlstm_b_f)
    q, k, v, mq, mk, mv, g_rows, g_cols, mo, z0 = _proj0(ctx, x, scale0, shift0, l0_norm, cos_t, sin_t, w0, nct)
    a = _mla_attention(q, k, v, nct)
    hf, hr = _mlstm(mq, mk, mv, g_rows, g_cols, nct)

    w1 = _proj1_weights(l1_w_in, l1_gla_w_gate, l1_gla_b_gate)
    h, gq, gk, gv, bc, nq, nk, nv, z = _mid(ctx, x, gate0, a, hf, hr, mo, z0, l0_mlstm_norm, l0_w_out,
                                            scale, shift, l1_norm, w1, nct)
    of, orv = _gla(gq, gk, gv, bc, nct)
    return _na_out1(nq, nk, nv, _na_bias_tables(l1_na_rpb, rows), h, gate, of, orv, z,
                    l1_gla_norm, l1_w_out, final_norm, n_ctx, rows)
```

```python
import functools
import math

import jax
import jax.numpy as jnp
import numpy as np
from jax import lax
from jax.experimental import pallas as pl
from jax.experimental.pallas import tpu as pltpu

F32 = jnp.float32
BF16 = jnp.bfloat16

D_MODEL = 1024
GRID_W = 64
EPS = 1e-6
ROPE_BASE = 10000.0

MLA_HEADS = 8
MLA_Q_RANK = 384
MLA_KV_RANK = 256
MLA_NOPE = 64
MLA_ROPE = 32
MLA_V = 64
ROPE_PAIRS = MLA_ROPE // 4
MLSTM_HEADS = 4
MLSTM_DH = 128
MLSTM_CONV = 3
GLA_HEADS = 4
GLA_DK = 64
GLA_DV = 128
GLA_GATE_RANK = 16
GLA_TAU = 16.0
NA_HEADS = 8
NA_DH = 64
NA_KH = 8
NA_KW = 16

CHUNK = 64
MLSTM_CHUNK = 128
TM = 256
LANES = 128
SUBLANES = 8
HALO = SUBLANES
NA_QROWS = 4
NA_KROWS = NA_QROWS + NA_KH - 1
GLA_LEAF = 8
MLA_HPS = 4
TILE_BATCH = 2
MLSTM_BATCH = 2
GLA_BATCH = 8
VMEM_LIMIT = 56 * 1024 * 1024
NEG_BIG = -1e30
LOG2E = 1.4426950408889634


def _dot(a, b):
    return jnp.dot(a, b, preferred_element_type=F32)


def _dot_nt(a, b):
    return lax.dot_general(a, b, (((1,), (1,)), ((), ())), preferred_element_type=F32)


def _dot_tn(a, b):
    return lax.dot_general(a, b, (((0,), (0,)), ((), ())), preferred_element_type=F32)


def _rms(x):
    return x * lax.rsqrt(jnp.mean(x * x, axis=-1, keepdims=True) + EPS)


def _sigmoid(x):
    return 1.0 / (1.0 + jnp.exp(-x))


def _silu(x):
    return x * _sigmoid(x)


def _log_sigmoid(x):
    return jnp.minimum(x, 0.0) - jnp.log(1.0 + jnp.exp(-jnp.abs(x)))


def _params(*sem):
    return pltpu.CompilerParams(dimension_semantics=sem, vmem_limit_bytes=VMEM_LIMIT)


def _full(shape):
    nd = len(shape)
    return pl.BlockSpec(shape, lambda *_: (0,) * nd)


def _mod_kernel(c_ref, w0_ref, b0_ref, w1_ref, b1_ref, o0_ref, o1_ref):
    sc = _silu(c_ref[...]).astype(BF16)
    o0_ref[...] = _dot(sc, w0_ref[...].astype(BF16)) + b0_ref[...]
    o1_ref[...] = _dot(sc, w1_ref[...].astype(BF16)) + b1_ref[...]


def _mod_vectors(c, c_ctx, mods):
    bsz, d = c.shape
    rows = -(-(bsz + 1) // SUBLANES) * SUBLANES
    cc = jnp.zeros((rows, d), F32).at[:bsz].set(c).at[bsz].set(c_ctx)
    (w0, b0), (w1, b1) = mods
    n = w0.shape[1]
    wspec = pl.BlockSpec((d, d), lambda j: (0, j))
    vspec = pl.BlockSpec((1, d), lambda j: (0, j))
    ospec = pl.BlockSpec((rows, d), lambda j: (0, j))
    per_layer = pl.pallas_call(
        _mod_kernel,
        out_shape=(jax.ShapeDtypeStruct((rows, n), F32),) * 2,
        grid=(n // d,),
        in_specs=[_full((rows, d)), wspec, vspec, wspec, vspec],
        out_specs=(ospec, ospec),
        compiler_params=_params("parallel"),
        name="modulation",
    )(cc, w0, b0[None], w1, b1[None])
    out = []
    for mod in per_layer:
        vecs = []
        for part in jnp.split(mod, 3, axis=-1):
            ctx_v = jnp.broadcast_to(part[bsz][None], (bsz, d))
            vecs.append(jnp.stack([ctx_v, part[:bsz]], axis=1)[:, :, None, :])
        out.append(vecs)
    return out


def _modnorm(x, g, scale, shift):
    return _rms(x) * g * (1.0 + scale) + shift


def _seg_scan(x, reverse, op, fill):
    width = x.shape[1]
    pos = lax.broadcasted_iota(jnp.int32, x.shape, 1) % MLSTM_CHUNK
    k = 1
    while k < MLSTM_CHUNK:
        if reverse:
            x = op(x, jnp.where(pos < MLSTM_CHUNK - k, pltpu.roll(x, width - k, axis=1), fill))
        else:
            x = op(x, jnp.where(pos >= k, pltpu.roll(x, k, axis=1), fill))
        k *= 2
    return x


def _split3(x):
    hi = x.astype(BF16)
    r1 = x - hi.astype(F32)
    mid = r1.astype(BF16)
    return hi, mid, (r1 - mid.astype(F32)).astype(BF16)


def _proj0_kernel(ctx_ref, lat_ref, hp_ref, hn_ref, sc_ref, sh_ref, g_ref, cos_ref, sin_ref,
                  wall_ref, wgt_ref, qn_ref, wuq_ref, kvn_ref, wk_ref, wv_ref, cw_ref, cb_ref, gb_ref,
                  q_out, k_out, v_out, mq_out, mk_out, mv_out, gr_out, gc_out, mo_out, z_out,
                  pbuf, *, nct, nt):
    t = pl.program_id(1)
    h = jnp.where(t < nct, ctx_ref[...], lat_ref[...])
    x = jnp.concatenate([hp_ref[...], h, hn_ref[...]], axis=0)
    ub_ext = _modnorm(x, g_ref[...], sc_ref[...], sh_ref[...]).astype(BF16)
    ub = ub_ext[HALO:HALO + TM]

    gt = _dot_nt(wgt_ref[...], ub) + gb_ref[...]
    half = MLSTM_HEADS
    per_dir = []
    for d in range(2):
        gd = gt[d * SUBLANES:(d + 1) * SUBLANES]
        bcum = _seg_scan(_log_sigmoid(gd), d == 1, jnp.add, 0.0)
        b_top = pltpu.roll(bcum, half, axis=0)
        c8 = gd - b_top
        pm8 = _seg_scan(c8, d == 1, jnp.maximum, -jnp.inf)
        per_dir.append((c8, pm8, b_top))

    cos = cos_ref[...]
    sin = sin_ref[...]
    cos_all = jnp.concatenate([cos] * MLA_HEADS, axis=1)
    sin_all = jnp.concatenate([sin] * MLA_HEADS, axis=1)

    def swap_halves(a):
        lane = lax.broadcasted_iota(jnp.int32, a.shape, 1)
        first = lane % (2 * ROPE_PAIRS) < ROPE_PAIRS
        return jnp.where(first, pltpu.roll(a, a.shape[1] - ROPE_PAIRS, axis=1), pltpu.roll(a, ROPE_PAIRS, axis=1))

    wseg = lambda name: wall_ref[:, PROJ0_COLS[name][0]:PROJ0_COLS[name][1]]
    cqkr = _dot(ub, wseg("cqkr"))
    cq = cqkr[:, :MLA_Q_RANK]
    kr = cqkr[:, MLA_Q_RANK:]
    ckv = _dot(ub, wseg("ckv"))
    pqk = _dot(ub_ext, wseg("mqk"))

    nq = (_rms(cq) * qn_ref[...]).astype(BF16)
    nkv = (_rms(ckv) * kvn_ref[...]).astype(BF16)

    q = _dot(nq, wuq_ref[...])
    kn = _dot(nkv, wk_ref[...])
    vv = _dot(nkv, wv_ref[...])

    prev_ok = t > nct
    next_ok = jnp.logical_and(t >= nct, t != nt - 1)
    row = lax.broadcasted_iota(jnp.int32, pqk.shape, 0)
    keep = jnp.logical_and(jnp.logical_or(row >= HALO, prev_ok), jnp.logical_or(row < HALO + TM, next_ok))
    pbuf[...] = jnp.where(keep, pqk, 0.0)
    cw = cw_ref[...]
    y = (pbuf[HALO - 1:HALO - 1 + TM, :] * cw[0:1] + pbuf[HALO:HALO + TM, :] * cw[1:2]
         + pbuf[HALO + 1:HALO + 1 + TM, :] * cw[2:3] + cb_ref[...])
    y = _silu(y)
    nb = MLSTM_HEADS * MLSTM_DH
    mq_out[...] = (y[:, :nb] * MLSTM_DH ** -0.5).astype(BF16)
    mk_out[...] = y[:, nb:].astype(BF16)

    mv = _dot(ub, wseg("mv"))
    mo = _dot(ub, wseg("mo"))

    q = q * cos_all + swap_halves(q) * sin_all
    q_out[...] = (q * ((MLA_NOPE + MLA_ROPE) ** -0.5 * LOG2E)).astype(BF16)
    rot = kr * cos + swap_halves(kr) * sin
    k_out[...] = (kn + jnp.concatenate([rot] * MLA_HEADS, axis=1)).astype(BF16)
    vlane = lax.broadcasted_iota(jnp.int32, (TM, MLA_HEADS * LANES), 1)
    is_value = ((vlane % LANES) < MLA_V) == ((vlane // LANES) % 2 == 0)
    v_out[...] = jnp.where(is_value, vv, 1.0).astype(BF16)

    zz = _dot(ub, wseg("z"))
    mv_out[...] = mv.astype(BF16)
    mo_out[...] = _sigmoid(mo).astype(BF16)
    z_out[...] = _silu(zz).astype(BF16)

    r8 =lax.broadcasted_iota(jnp.int32, (SUBLANES, TM), 0)
    both = lambda i: jnp.where(r8 < half, per_dir[0][i], pltpu.roll(per_dir[1][i], half, axis=0)) * LOG2E
    c_rows, pm_rows, b_rows = both(0), both(1), both(2)
    for ci in range(TM // MLSTM_CHUNK):
        gr_out[ci] = c_rows[:, ci * MLSTM_CHUNK:(ci + 1) * MLSTM_CHUNK]
    stack = jnp.concatenate([c_rows, pm_rows, b_rows, jnp.zeros((LANES - 3 * SUBLANES, TM), F32)], axis=0)
    er = lax.broadcasted_iota(jnp.int32, (TM, TM), 0)
    ec = lax.broadcasted_iota(jnp.int32, (TM, TM), 1)
    eye = jnp.where(er == ec, 1.0, 0.0).astype(BF16)
    gc_out[...] = sum(_dot_nt(eye, part) for part in _split3(stack))


def _rope_tables(n_ctx, seq):
    t = jnp.arange(seq)
    inv = 1.0 / (ROPE_BASE ** (jnp.arange(ROPE_PAIRS, dtype=F32) / ROPE_PAIRS))
    ang = jnp.concatenate([(t // GRID_W)[:, None] * inv, (t % GRID_W)[:, None] * inv], axis=-1)
    cos, sin = jnp.cos(ang), jnp.sin(ang)
    j = np.arange(MLA_ROPE)
    src = (j // (2 * ROPE_PAIRS)) * ROPE_PAIRS + (j % ROPE_PAIRS)
    sign = np.where((j % (2 * ROPE_PAIRS)) < ROPE_PAIRS, -1.0, 1.0).astype(np.float32)
    cos_full = cos[:, src]
    sin_full = sin[:, src] * sign
    cos_t = jnp.ones((n_ctx + seq, LANES), F32).at[n_ctx:, MLA_NOPE:MLA_NOPE + MLA_ROPE].set(cos_full)
    sin_t = jnp.zeros((n_ctx + seq, LANES), F32).at[n_ctx:, MLA_NOPE:MLA_NOPE + MLA_ROPE].set(sin_full)
    return cos_t, sin_t


def _segments(widths):
    out, start = {}, 0
    for name, width in widths:
        out[name] = (start, start + width)
        start += width
    return out


PROJ0_COLS = _segments([("cqkr", MLA_Q_RANK + LANES), ("ckv", MLA_KV_RANK), ("mqk", 2 * MLSTM_HEADS * MLSTM_DH),
                        ("mv", MLSTM_HEADS * MLSTM_DH), ("mo", MLSTM_HEADS * MLSTM_DH), ("z", D_MODEL)])
PROJ1_COLS = _segments([("q", GLA_HEADS * GLA_DK), ("k", GLA_HEADS * GLA_DK), ("v", GLA_HEADS * GLA_DV),
                        ("ga", LANES), ("nq", NA_HEADS * NA_DH), ("nk", NA_HEADS * NA_DH), ("nv", NA_HEADS * NA_DH),
                        ("z", GLA_HEADS * GLA_DV + NA_HEADS * NA_DH)])


def _proj0_weights(w_in, q_norm, w_uq, kv_norm, w_ukv, conv_w, conv_b, b_i, b_f):
    o = np.cumsum([0, MLA_Q_RANK, MLA_KV_RANK, MLA_ROPE] + [MLSTM_HEADS * MLSTM_DH] * 4 + [4 * MLSTM_HEADS, D_MODEL])
    dq = MLA_NOPE + MLA_ROPE
    wkr = jnp.pad(w_in[:, o[2]:o[3]], ((0, 0), (MLA_NOPE, LANES - dq)))
    wuq = jnp.pad(w_uq.reshape(MLA_Q_RANK, MLA_HEADS, dq), ((0, 0), (0, 0), (0, LANES - dq)))
    wkv = w_ukv.reshape(MLA_KV_RANK, MLA_HEADS, MLA_NOPE + MLA_V)
    wk = jnp.pad(wkv[:, :, :MLA_NOPE], ((0, 0), (0, 0), (0, LANES - MLA_NOPE)))
    zv = jnp.zeros((MLA_KV_RANK, MLA_HEADS // 2, MLA_V), F32)
    wv = jnp.concatenate([wkv[:, 0::2, MLA_NOPE:], zv, zv, wkv[:, 1::2, MLA_NOPE:]], axis=-1)
    wv = wv.reshape(MLA_KV_RANK, MLA_HEADS * LANES)
    gbias = jnp.stack([b_i, b_f], axis=1).reshape(4 * MLSTM_HEADS, 1)
    bf = lambda a: a.astype(BF16)
    wall = jnp.concatenate([w_in[:, o[0]:o[1]], wkr, w_in[:, o[1]:o[2]], w_in[:, o[3]:o[7]], w_in[:, o[8]:o[9]]], axis=1)
    assert wall.shape[1] == PROJ0_COLS["z"][1]
    return dict(
        wall=bf(wall), wgt=bf(w_in[:, o[7]:o[8]].T),
        qn=q_norm[None], wuq=bf(wuq.reshape(MLA_Q_RANK, -1)),
        kvn=kv_norm[None], wk=bf(wk.reshape(MLA_KV_RANK, -1)), wv=bf(wv),
        cw=jnp.zeros((SUBLANES, conv_w.shape[1]), F32).at[:MLSTM_CONV].set(conv_w), cb=conv_b[None], gb=gbias)


def _tile_specs(nct, nt, nb):
    tile = lambda w: pl.BlockSpec((nb, TM, w), lambda b, t: (b, t, 0))
    per = TM // HALO
    nlat = nt - nct
    src_ctx = pl.BlockSpec((nb, TM, D_MODEL), lambda b, t: (b, jnp.minimum(t, nct - 1), 0))
    src_lat = pl.BlockSpec((nb, TM, D_MODEL), lambda b, t: (b, jnp.maximum(t - nct, 0), 0))
    prev = pl.BlockSpec((nb, HALO, D_MODEL), lambda b, t: (b, jnp.maximum((t - nct) * per - 1, 0), 0))
    nxt = pl.BlockSpec((nb, HALO, D_MODEL),
                       lambda b, t: (b, jnp.clip((t - nct + 1) * per, 0, nlat * per - 1), 0))
    mod = pl.BlockSpec((nb, None, 1, D_MODEL), lambda b, t: (b, (t >= nct).astype(jnp.int32), 0, 0))
    return tile, src_ctx, src_lat, prev, nxt, mod


def _per_element(tile_kernel, shared):
    def kernel(*refs, **kw):
        for bi in range(refs[0].shape[0]):
            tile_kernel(*[r if i in shared else r.at[bi] for i, r in enumerate(refs)], **kw)
    return kernel


def _proj0(ctx, x, scale, shift, norm_g, cos_t, sin_t, w, nct):
    bsz = x.shape[0]
    ttot = ctx.shape[1] + x.shape[1]
    nt = ttot // TM
    nb = MLSTM_HEADS * MLSTM_DH
    sb = math.gcd(bsz, TILE_BATCH)
    tile, src_ctx, src_lat, prev, nxt, mod = _tile_specs(nct, nt, sb)
    tab = pl.BlockSpec((TM, LANES), lambda b, t: (t, 0))
    wnames = ["wall", "wgt", "qn", "wuq", "kvn", "wk", "wv", "cw", "cb", "gb"]
    ws = [w[n] for n in wnames]
    tok = lambda width, dt: jax.ShapeDtypeStruct((bsz, ttot, width), dt)
    out_shape = (tok(MLA_HEADS * LANES, BF16), tok(MLA_HEADS * LANES, BF16), tok(MLA_HEADS * LANES, BF16),
                 tok(nb, BF16), tok(nb, BF16), tok(nb, BF16),
                 jax.ShapeDtypeStruct((bsz, ttot // MLSTM_CHUNK, 2 * MLSTM_HEADS, MLSTM_CHUNK), F32), tok(LANES, F32),
                 tok(nb, BF16), tok(D_MODEL, BF16))
    out_specs = (tile(MLA_HEADS * LANES), tile(MLA_HEADS * LANES), tile(MLA_HEADS * LANES),
                 tile(nb), tile(nb), tile(nb),
                 pl.BlockSpec((sb, TM // MLSTM_CHUNK, 2 * MLSTM_HEADS, MLSTM_CHUNK), lambda b, t: (b, t, 0, 0)),
                 tile(LANES),
                 tile(nb), tile(D_MODEL))
    n_blocked, n_out = 6, len(out_shape)
    n_in = n_blocked + 3 + len(ws)
    shared = set(range(n_blocked, n_in)) | {n_in + n_out}
    return pl.pallas_call(
        functools.partial(_per_element(_proj0_kernel, shared), nct=nct, nt=nt),
        out_shape=out_shape,
        grid=(bsz // sb, nt),
        in_specs=[src_ctx, src_lat, prev, nxt, mod, mod, _full((1, D_MODEL)), tab, tab] + [_full(a.shape) for a in ws],
        out_specs=out_specs,
        scratch_shapes=[pltpu.VMEM((TM + 2 * HALO, 2 * nb), F32)],
        compiler_params=_params("parallel", "parallel"),
        name="proj0",
    )(ctx, x, x, x, scale, shift, norm_g[None], cos_t, sin_t, *ws)


def _mla_kernel(q_ref, k_ref, v_ref, o_ref, *, nct):
    t = pl.program_id(2)
    lane = lax.broadcasted_iota(jnp.int32, (TM, LANES), 1)

    def attend(nkeys):
        qk = lambda j: _dot_nt(q_ref[:, j * LANES:(j + 1) * LANES], k_ref[0:nkeys, j * LANES:(j + 1) * LANES])
        outs = []
        s_next = qk(0)
        for j in range(MLA_HPS):
            s = s_next
            if j + 1 < MLA_HPS:
                s_next = qk(j + 1)
            p = jnp.exp2(s - jnp.max(s, axis=-1, keepdims=True))
            o = _dot(p.astype(BF16), v_ref[0:nkeys, j * LANES:(j + 1) * LANES])
            outs.append(o / pltpu.roll(o, MLA_V, axis=1))
        for j in range(MLA_HPS // 2):
            o_ref[:, j * LANES:(j + 1) * LANES] = jnp.where(lane < MLA_V, outs[2 * j], outs[2 * j + 1]).astype(BF16)

    @pl.when(t < nct)
    def _():
        attend(nct * TM)

    @pl.when(t >= nct)
    def _():
        attend(k_ref.shape[0])


def _mla_attention(q, k, v, nct):
    bsz, ttot, _ = q.shape
    nt = ttot // TM
    hw = MLA_HPS * LANES
    return pl.pallas_call(
        functools.partial(_mla_kernel, nct=nct),
        out_shape=jax.ShapeDtypeStruct((bsz, ttot, MLA_HEADS * MLA_V), BF16),
        grid=(bsz, MLA_HEADS // MLA_HPS, nt),
        in_specs=[pl.BlockSpec((None, TM, hw), lambda b, p, t: (b, t, p)),
                  pl.BlockSpec((None, ttot, hw), lambda b, p, t: (b, 0, p)),
                  pl.BlockSpec((None, ttot, hw), lambda b, p, t: (b, 0, p))],
        out_specs=pl.BlockSpec((None, TM, hw // 2), lambda b, p, t: (b, t, p)),
        compiler_params=_params("parallel", "parallel", "parallel"),
        name="mla_attention",
    )(q, k, v)


def _scan_block(j, nct, nt):
    return jnp.where(j < nct, nct - 1 - j, nt - 1 - (j - nct))


def _mlstm_kernel(qf_ref, kf_ref, vf_ref, grf_ref, gcf_ref, qr_ref, kr_ref, vr_ref, grr_ref, gcr_ref,
                  hf_ref, hr_ref, c_scr, m_scr):
    @pl.when(pl.program_id(1) == 0)
    def _():
        c_scr[...] = jnp.zeros_like(c_scr)
        m_scr[...] = jnp.zeros_like(m_scr)

    L = MLSTM_CHUNK
    nchunk = TM // L
    nh = MLSTM_HEADS
    nchain = 2 * nh
    row = lax.broadcasted_iota(jnp.int32, (L, L), 0)
    col = lax.broadcasted_iota(jnp.int32, (L, L), 1)
    ones = jnp.ones((L, MLSTM_DH), BF16)
    wide = lambda a: jnp.broadcast_to(a, (L, MLSTM_DH))
    twice = lambda a: jnp.concatenate([a, a], axis=1)
    refs = ((qf_ref, kf_ref, vf_ref, grf_ref, gcf_ref, hf_ref), (qr_ref, kr_ref, vr_ref, grr_ref, gcr_ref, hr_ref))
    nbatch = qf_ref.shape[0]

    def load(ci):
        cs = []
        for bi, d in [(bi, d) for bi in range(nbatch) for d in range(2)]:
            q_ref, k_ref, v_ref, gr_ref, gc_ref, o_ref = (r.at[bi] for r in refs[d])
            cd = ci if d == 0 else nchunk - 1 - ci
            rs = slice(cd * L, (cd + 1) * L)
            g_rows = gr_ref[cd]
            g_cols = gc_ref[rs, :]
            last = L - 1 if d == 0 else 0
            for hd in range(nh):
                ch = d * nh + hd
                hs = slice(hd * MLSTM_DH, (hd + 1) * MLSTM_DH)
                cs.append(dict(
                    rs=rs, hs=hs, o_ref=o_ref, causal=(col <= row) if d == 0 else (col >= row),
                    c_row=g_rows[ch:ch + 1], c_col=g_cols[:, ch:ch + 1],
                    pm_col=g_cols[:, nchain + ch:nchain + ch + 1], b_col=g_cols[:, 2 * nchain + ch:2 * nchain + ch + 1],
                    pm_last=g_cols[last:last + 1, nchain + ch:nchain + ch + 1],
                    b_last=g_cols[last:last + 1, 2 * nchain + ch:2 * nchain + ch + 1],
                    q=q_ref[rs, hs], k=k_ref[rs, hs],
                    v_ext=jnp.concatenate([v_ref[rs, hs], ones], axis=1)))
        return cs

    def prep_scores(cs):
        for c in cs:
            c["s"] = _dot_nt(c["q"], c["k"])

    def prep_weights(cs):
        for c in cs:
            c["pm_w"] = wide(c["pm_col"])
            c["b_w"] = wide(c["b_col"])
            c["w"] = jnp.exp2(jnp.where(c["causal"], c["c_row"] - c["pm_w"][:, :L], -jnp.inf))
            c["kwf"] = jnp.exp2(wide(c["c_col"] - c["pm_last"]))

    def prep_operands(cs):
        for c in cs:
            c["p"] = (c["s"] * c["w"]).astype(BF16)
            c["kw"] = (c["kwf"] * c["k"].astype(F32)).astype(BF16)

    def prep_products(cs):
        for c in cs:
            c["o1"] = _dot(c["p"], c["v_ext"])
            c["dc"] = _dot_tn(c["kw"], c["v_ext"])

    def scan_read(cs):
        for ch, c in enumerate(cs):
            c["qc"] = _dot(c["q"], state[ch].astype(BF16))

    def scan_factors(cs):
        for ch, c in enumerate(cs):
            g_w = jnp.maximum(m[ch], c["pm_w"])
            c["e_intra"] = jnp.exp2(c["pm_w"] - g_w)
            c["e_inter"] = jnp.exp2(m[ch] - g_w)
            c["floor"] = jnp.exp2(-(c["b_w"] + g_w))
            g_last = jnp.maximum(m[ch], c["pm_last"])
            c["keep"] = jnp.exp2(m[ch] - g_last)
            c["gain"] = jnp.exp2(c["pm_last"] - g_last)
            c["m_new"] = c["b_last"] + g_last

    def scan_update(cs):
        for ch, c in enumerate(cs):
            o = twice(c["e_intra"]) * c["o1"] + twice(c["e_inter"]) * c["qc"]
            h = o[:, :MLSTM_DH] / jnp.maximum(jnp.abs(o[:, MLSTM_DH:]), c["floor"])
            c["o_ref"][c["rs"], c["hs"]] = h.astype(BF16)
            state[ch] = c["keep"] * state[ch] + c["gain"] * c["dc"]
            m[ch] = c["m_new"]

    nstate = nbatch * nchain
    state = [c_scr[ch] for ch in range(nstate)]
    m = [m_scr[ch, 0:1, 0:1] for ch in range(nstate)]
    chunks = [load(ci) for ci in range(nchunk)]
    everything = [c for cs in chunks for c in cs]
    for stage in (prep_scores, prep_weights, prep_operands):
        stage(everything)
    prep_products(chunks[0])
    for i, cs in enumerate(chunks):
        scan_read(cs)
        if i + 1 < nchunk:
            prep_products(chunks[i + 1])
        scan_factors(cs)
        scan_update(cs)
    for ch in range(nstate):
        c_scr[ch] = state[ch]
        m_scr[ch] = jnp.broadcast_to(m[ch], m_scr.shape[1:])


def _mlstm(mq, mk, mv, g_rows, g_cols, nct):
    bsz, ttot, nb = mq.shape
    nt = ttot // TM
    per = TM // MLSTM_CHUNK
    sb = math.gcd(bsz, MLSTM_BATCH)
    fwd = lambda w: pl.BlockSpec((sb, TM, w), lambda b, j: (b, j, 0))
    rev = lambda w: pl.BlockSpec((sb, TM, w), lambda b, j: (b, _scan_block(j, nct, nt), 0))
    gshape = (sb, per, 2 * MLSTM_HEADS, MLSTM_CHUNK)
    gfwd = pl.BlockSpec(gshape, lambda b, j: (b, j, 0, 0))
    grev = pl.BlockSpec(gshape, lambda b, j: (b, _scan_block(j, nct, nt), 0, 0))
    nstate = sb * 2 * MLSTM_HEADS
    return pl.pallas_call(
        _mlstm_kernel,
        out_shape=(jax.ShapeDtypeStruct((bsz, ttot, nb), BF16),) * 2,
        grid=(bsz // sb, nt),
        in_specs=[fwd(nb), fwd(nb), fwd(nb), gfwd, fwd(LANES), rev(nb), rev(nb), rev(nb), grev, rev(LANES)],
        out_specs=(fwd(nb), rev(nb)),
        scratch_shapes=[pltpu.VMEM((nstate, MLSTM_DH, 2 * MLSTM_DH), F32),
                        pltpu.VMEM((nstate, SUBLANES, LANES), F32)],
        compiler_params=_params("parallel", "arbitrary"),
        name="mlstm_scan",
    )(mq, mk, mv, g_rows, g_cols, mq, mk, mv, g_rows, g_cols)


def _head_rms(x, n_heads):
    w = x.shape[1] // n_heads
    return jnp.concatenate([_rms(x[:, i * w:(i + 1) * w]) for i in range(n_heads)], axis=1)


def _mid_kernel(ctx_ref, lat_ref, gate_ref, a_ref, hf_ref, hr_ref, mo_ref, z0_ref, hn_ref, wout_ref,
                sc_ref, sh_ref, g_ref, wall_ref, wgate_ref, bgate_ref,
                h_out, gq_out, gk_out, gv_out, bc_out, nq_out, nk_out, nv_out, z_out, *, nct):
    nbt = ctx_ref.shape[0]
    rows = lambda f: jnp.concatenate([f(bi) for bi in range(nbt)], axis=0)
    f32 = lambda r, bi: r[bi].astype(F32)

    def put(ref, val):
        for bi in range(nbt):
            ref[bi] = val[bi * TM:(bi + 1) * TM]

    hm = rows(lambda bi: _head_rms(f32(mo_ref, bi) * (f32(hf_ref, bi) + f32(hr_ref, bi)), MLSTM_HEADS) * hn_ref[...])
    cat = (jnp.concatenate([rows(lambda bi: f32(a_ref, bi)), hm], axis=1) * rows(lambda bi: f32(z0_ref, bi))).astype(BF16)
    y = _dot(cat, wout_ref[...])
    is_ctx = pl.program_id(1) < nct
    hs = [jnp.where(is_ctx, ctx_ref[bi], lat_ref[bi]) + gate_ref[bi] * y[bi * TM:(bi + 1) * TM] for bi in range(nbt)]
    for bi in range(nbt):
        h_out[bi] = hs[bi]

    ub = rows(lambda bi: _modnorm(hs[bi], g_ref[...], sc_ref[bi], sh_ref[bi])).astype(BF16)
    wseg = lambda name: wall_ref[:, PROJ1_COLS[name][0]:PROJ1_COLS[name][1]]
    ga = _dot(ub, wseg("ga")).astype(BF16)
    gate_pre = _dot(ga, wgate_ref[...])
    put(gq_out, (_dot(ub, wseg("q")) * GLA_DK ** -0.5).astype(BF16))
    put(gk_out, _dot(ub, wseg("k")).astype(BF16))
    put(gv_out, _dot(ub, wseg("v")).astype(BF16))

    lg = _log_sigmoid(gate_pre + bgate_ref[...]) * (1.0 / GLA_TAU)
    hi, mid, lo = _split3(lg)
    put(nq_out, (_dot(ub, wseg("nq")) * (NA_DH ** -0.5 * LOG2E)).astype(BF16))
    put(nk_out, _dot(ub, wseg("nk")).astype(BF16))
    put(nv_out, _dot(ub, wseg("nv")).astype(BF16))
    row = lax.broadcasted_iota(jnp.int32, (TM, TM), 0)
    col = lax.broadcasted_iota(jnp.int32, (TM, TM), 1)
    same = (row // CHUNK) == (col // CHUNK)
    nk = GLA_HEADS * GLA_DK
    for d in range(2):
        tri = jnp.where(jnp.logical_and(same, (col <= row) if d == 0 else (col >= row)), 1.0, 0.0).astype(BF16)
        cols = slice(d * nk, (d + 1) * nk)
        for bi in range(nbt):
            rs = slice(bi * TM, (bi + 1) * TM)
            bc_out[bi, :, cols] = (_dot(tri, hi[rs, cols]) + _dot(tri, mid[rs, cols]) + _dot(tri, lo[rs, cols])) * LOG2E
    put(z_out, _silu(_dot(ub, wseg("z"))).astype(BF16))


def _proj1_weights(w_in, w_gate, b_gate):
    nk = GLA_HEADS * GLA_DK
    nc = GLA_HEADS * GLA_DV
    nd = NA_HEADS * NA_DH
    o = np.cumsum([0, nk, nk, nc, 2 * GLA_GATE_RANK, nd, nd, nd, nc + nd])
    r = GLA_GATE_RANK
    gpad = LANES - 2 * r
    wall = jnp.concatenate([w_in[:, :o[4]], jnp.zeros((w_in.shape[0], gpad), F32), w_in[:, o[4]:]], axis=1)
    assert wall.shape[1] == PROJ1_COLS["z"][1]
    wgate = jnp.zeros((LANES, 2 * nk), F32).at[:r, :nk].set(w_gate[0]).at[r:2 * r, nk:].set(w_gate[1])
    return dict(wall=wall.astype(BF16), wgate=wgate.astype(BF16), bgate=b_gate.reshape(1, 2 * nk))


def _mid(ctx, x, gate0, a, hf, hr, mo, z0, h_norm, w_out, scale, shift, norm_g, w, nct):
    bsz = x.shape[0]
    ttot = ctx.shape[1] + x.shape[1]
    nt = ttot // TM
    sb = math.gcd(bsz, TILE_BATCH)
    tile, src_ctx, src_lat, _, _, mod = _tile_specs(nct, nt, sb)
    nb = MLSTM_HEADS * MLSTM_DH
    nk = GLA_HEADS * GLA_DK
    nc = GLA_HEADS * GLA_DV
    nd = NA_HEADS * NA_DH
    ws = [w[n] for n in ("wall", "wgate", "bgate")]
    tok = lambda width, dt: jax.ShapeDtypeStruct((bsz, ttot, width), dt)
    return pl.pallas_call(
        functools.partial(_mid_kernel, nct=nct),
        out_shape=(tok(D_MODEL, F32), tok(nk, BF16), tok(nk, BF16), tok(nc, BF16), tok(2 * nk, F32),
                   tok(nd, BF16), tok(nd, BF16), tok(nd, BF16), tok(nc + nd, BF16)),
        grid=(bsz // sb, nt),
        in_specs=[src_ctx, src_lat, mod, tile(MLA_HEADS * MLA_V), tile(nb), tile(nb), tile(nb), tile(D_MODEL),
                  _full((1, nb)), _full(w_out.shape), mod, mod, _full((1, D_MODEL))] + [_full(a.shape) for a in ws],
        out_specs=(tile(D_MODEL), tile(nk), tile(nk), tile(nc), tile(2 * nk), tile(nd), tile(nd), tile(nd),
                   tile(nc + nd)),
        compiler_params=_params("parallel", "parallel"),
        name="out0_proj1",
    )(ctx, x, gate0, a, hf, hr, mo, z0, h_norm[None], w_out.astype(BF16), scale, shift, norm_g[None], *ws)


def _gla_kernel(qf_ref, kf_ref, vf_ref, bf_ref, qr_ref, kr_ref, vr_ref, br_ref, of_ref, or_ref, s_scr):
    @pl.when(pl.program_id(1) == 0)
    def _():
        s_scr[...] = jnp.zeros_like(s_scr)

    L = CHUNK
    nchunk = TM // L
    npair = GLA_HEADS // 2
    lane = lax.broadcasted_iota(jnp.int32, (L, LANES), 1)
    rowv = lax.broadcasted_iota(jnp.int32, (L, LANES), 0)
    srow = lax.broadcasted_iota(jnp.int32, (LANES, 2 * GLA_DV), 0)
    scol = lax.broadcasted_iota(jnp.int32, (LANES, 2 * GLA_DV), 1)
    own_block = (srow < GLA_DK) == (scol < GLA_DV)
    vcol = lax.broadcasted_iota(jnp.int32, (L, 2 * GLA_DV), 1)
    e_row = lax.broadcasted_iota(jnp.int32, (LANES, LANES), 0)
    e_col = lax.broadcasted_iota(jnp.int32, (LANES, LANES), 1)
    head_ones = jnp.where((e_row < GLA_DK) == (e_col < GLA_DK), 1.0, 0.0).astype(BF16)
    leaf_idx = lax.broadcasted_iota(jnp.int32, (GLA_LEAF * GLA_LEAF, LANES), 0)
    leaf_lane = lax.broadcasted_iota(jnp.int32, (GLA_LEAF * GLA_LEAF, LANES), 1) % L
    leaf_s, leaf_t = leaf_idx // GLA_LEAF, leaf_idx % GLA_LEAF
    rep = lambda a: jnp.concatenate(
        [jnp.broadcast_to(a[s:s + 1], (GLA_LEAF, a.shape[1])) for s in range(GLA_LEAF)], axis=0)
    til = lambda a: jnp.concatenate([a] * GLA_LEAF, axis=0)

    def chunk_step(ci, carry):
        cs = []
        for bi, d in [(bi, d) for bi in range(qf_ref.shape[0]) for d in range(2)]:
            q_ref, k_ref, v_ref, b_ref, o_ref = (
                r.at[bi] for r in ((qf_ref, kf_ref, vf_ref, bf_ref, of_ref), (qr_ref, kr_ref, vr_ref, br_ref, or_ref))[d])
            cd = ci if d == 0 else nchunk - 1 - ci
            base = pl.multiple_of(cd * L, L)
            for p in range(npair):
                ls = slice(p * LANES, (p + 1) * LANES)
                bc = b_ref[pl.ds(base, L), ls]
                cs.append(dict(
                    d=d, p=p, bi=bi, base=base, o_ref=o_ref, bc=bc,
                    tau_v=rowv if d == 0 else L - 1 - rowv,
                    tau_s=(lane % L) if d == 0 else L - 1 - (lane % L),
                    q=q_ref[pl.ds(base, L), ls].astype(F32),
                    k=k_ref[pl.ds(base, L), ls].astype(F32),
                    v=v_ref[pl.ds(base, L), p * 2 * GLA_DV:(p + 1) * 2 * GLA_DV],
                    b_last=bc[L - 1:L] if d == 0 else bc[0:1],
                    state=s_scr[bi, d, p]))

        for c in cs:
            c["o_pair"] = _dot((c["q"] * jnp.exp2(c["bc"])).astype(BF16), c["state"].astype(BF16))

        for c in cs:
            c["att"] = jnp.zeros((L, LANES), F32)
        bs = L // 2
        while bs >= GLA_LEAF:
            for c in cs:
                d, bc, tau_v = c["d"], c["bc"], c["tau_v"]
                later = (tau_v // bs) % 2 == 1
                ref_b = None
                for blk in range(L // (2 * bs)):
                    tau_ref = blk * 2 * bs + bs - 1
                    idx = tau_ref if d == 0 else L - 1 - tau_ref
                    r = jnp.broadcast_to(bc[idx:idx + 1], (L, LANES))
                    ref_b = r if ref_b is None else jnp.where(tau_v // (2 * bs) == blk, r, ref_b)
                c["qs"] = jnp.where(later, c["q"] * jnp.exp2(jnp.where(later, bc - ref_b, 0.0)), 0.0).astype(BF16)
                ks = jnp.where(later, 0.0, c["k"] * jnp.exp2(jnp.where(later, 0.0, ref_b - bc)))
                c["ks2"] = jnp.concatenate([jnp.where(lane < GLA_DK, ks, 0.0), jnp.where(lane < GLA_DK, 0.0, ks)],
                                           axis=0).astype(BF16)
            for c in cs:
                same_parent = (c["tau_v"] // (2 * bs)) == (c["tau_s"] // (2 * bs))
                c["att"] = c["att"] + jnp.where(same_parent, _dot_nt(c["qs"], c["ks2"]), 0.0)
            bs //= 2

        for c in cs:
            c["leaf"] = []
        for blk in range(L // GLA_LEAF):
            rs = slice(blk * GLA_LEAF, (blk + 1) * GLA_LEAF)
            for c in cs:
                qb, bb, kb = c["q"][rs], c["bc"][rs], c["k"][rs]
                ok = (leaf_t >= leaf_s) if c["d"] == 0 else (leaf_t <= leaf_s)
                w = jnp.where(ok, til(qb) * jnp.exp2(til(bb) - rep(bb)) * rep(kb), 0.0)
                c["a"] = _dot(w.astype(BF16), head_ones)
            for c in cs:
                a = jnp.where(leaf_lane == leaf_s + blk * GLA_LEAF, c["a"], 0.0)
                acc = a[0:GLA_LEAF]
                for s in range(1, GLA_LEAF):
                    acc = acc + a[s * GLA_LEAF:(s + 1) * GLA_LEAF]
                c["leaf"].append(acc)

        for c in cs:
            v = c["v"]
            c["att"] = (c["att"] + jnp.concatenate(c["leaf"], axis=0)).astype(BF16)
            c["v_blk"] = jnp.concatenate([jnp.where(vcol < GLA_DV, v, jnp.zeros_like(v)),
                                          jnp.where(vcol < GLA_DV, jnp.zeros_like(v), v)], axis=0)
            c["ke"] = (c["k"] * jnp.exp2(c["b_last"] - c["bc"])).astype(BF16)
            c["decay_col"] = jnp.sum(
                jnp.where(e_row == e_col, jnp.broadcast_to(jnp.exp2(c["b_last"]), (LANES, LANES)), 0.0),
                axis=1, keepdims=True)
        for c in cs:
            c["o"] = c["o_pair"] + _dot(c["att"], c["v_blk"])
            c["ds"] = _dot_tn(c["ke"], c["v"])
        for c in cs:
            p = c["p"]
            c["o_ref"][pl.ds(c["base"], L), p * 2 * GLA_DV:(p + 1) * 2 * GLA_DV] = c["o"].astype(BF16)
            s_scr[c["bi"], c["d"], p] = c["decay_col"] * c["state"] + jnp.where(own_block, c["ds"], 0.0)
        return carry

    lax.fori_loop(0, nchunk, chunk_step, 0)


def _gla(gq, gk, gv, bc, nct):
    bsz, ttot, nk = gq.shape
    nc = gv.shape[2]
    nt = ttot // TM
    sb = math.gcd(bsz, GLA_BATCH)
    fwd = lambda w: pl.BlockSpec((sb, TM, w), lambda b, j: (b, j, 0))
    rev = lambda w: pl.BlockSpec((sb, TM, w), lambda b, j: (b, _scan_block(j, nct, nt), 0))
    bfwd = pl.BlockSpec((sb, TM, nk), lambda b, j: (b, j, 0))
    brev = pl.BlockSpec((sb, TM, nk), lambda b, j: (b, _scan_block(j, nct, nt), 1))
    return pl.pallas_call(
        _gla_kernel,
        out_shape=(jax.ShapeDtypeStruct((bsz, ttot, nc), BF16),) * 2,
        grid=(bsz // sb, nt),
        in_specs=[fwd(nk), fwd(nk), fwd(nc), bfwd, rev(nk), rev(nk), rev(nc), brev],
        out_specs=(fwd(nc), rev(nc)),
        scratch_shapes=[pltpu.VMEM((sb, 2, GLA_HEADS // 2, LANES, 2 * GLA_DV), F32)],
        compiler_params=_params("parallel", "arbitrary"),
        name="gla_scan",
    )(gq, gk, gv, bc, gq, gk, gv, bc)


def _na_bias_tables(rpb, rows):
    kh = NA_KH
    c = np.arange(GRID_W)
    cs = np.clip(c - NA_KW // 2, 0, GRID_W - NA_KW)
    col_ok = (c[None, :] >= cs[:, None]) & (c[None, :] < cs[:, None] + NA_KW)
    pad = GRID_W - NA_KW
    rpb_pad = jnp.pad(rpb.astype(F32) * LOG2E, ((0, 0), (0, 0), (pad, pad)))
    period = 2 * GRID_W
    flat = jnp.tile(jnp.pad(rpb_pad, ((0, 0), (0, 0), (0, 1))), (1, 1, GRID_W))[:, :, :GRID_W * (period - 1)]
    t1 = flat.reshape(NA_HEADS, 2 * NA_KH - 1, GRID_W, period - 1)[..., GRID_W - 1:]
    t1 = jnp.where(col_ok, t1, NEG_BIG)
    masked = jnp.full((NA_HEADS, GRID_W, GRID_W), NEG_BIG, F32)
    per_row = []
    for r0 in (0, NA_QROWS, rows - NA_QROWS):
        kb = int(np.clip(r0 - kh // 2, 0, rows - NA_KROWS))
        for qrow in range(r0, r0 + NA_QROWS):
            ws = int(np.clip(qrow - kh // 2, 0, rows - kh))
            per_row.append(jnp.concatenate(
                [t1[:, krow - qrow + NA_KH - 1] if ws <= krow < ws + kh else masked
                 for krow in range(kb, kb + NA_KROWS)], axis=-1))
    return jnp.concatenate(per_row, axis=1)


def _na_tile(q_ref, kw_ref, vw_ref, kc_ref, vc_ref, bias_ref, h_ref, gate_ref, of_ref, or_ref, z_ref,
             gn_ref, w_ref, fn_ref, o_ref):
    nq = NA_QROWS * GRID_W
    lane = lax.broadcasted_iota(jnp.int32, (nq, LANES), 1)
    def scores(j):
        ls = slice((j // 2) * LANES, (j // 2 + 1) * LANES)
        q = q_ref[:, ls]
        qj = jnp.where((lane < NA_DH) == (j % 2 == 0), q, jnp.zeros_like(q))
        return _dot_nt(qj, kw_ref[:, ls]) + bias_ref[j], _dot_nt(qj, kc_ref[:, ls])

    outs = []
    nxt = scores(0)
    for j in range(NA_HEADS):
        s_loc, s_ctx = nxt
        if j + 1 < NA_HEADS:
            nxt = scores(j + 1)
        ls = slice((j // 2) * LANES, (j // 2 + 1) * LANES)
        m = jnp.maximum(jnp.max(s_loc, axis=-1, keepdims=True), jnp.max(s_ctx, axis=-1, keepdims=True))
        p_loc = jnp.exp2(s_loc - m)
        p_ctx = jnp.exp2(s_ctx - m)
        den = jnp.sum(p_loc, axis=-1, keepdims=True) + jnp.sum(p_ctx, axis=-1, keepdims=True)
        outs.append((_dot(p_loc.astype(BF16), vw_ref[:, ls]) + _dot(p_ctx.astype(BF16), vc_ref[:, ls])) / den)
    na = [jnp.where(lane < NA_DH, outs[2 * i], outs[2 * i + 1]) for i in range(NA_HEADS // 2)]

    f32 = lambda r: r[...].astype(F32)
    g = _head_rms(f32(of_ref) + f32(or_ref), GLA_HEADS) * gn_ref[...]
    cat = (jnp.concatenate([g] + na, axis=1) * f32(z_ref)).astype(BF16)
    h = h_ref[...] + gate_ref[...] * _dot(cat, w_ref[...])
    o_ref[...] = _rms(h) * fn_ref[...]


def _na_out1_kernel(*refs, nb):
    kv = refs[:4 * nb]
    q_ref, bias_ref, h_ref, gate_ref, of_ref, or_ref, z_ref, gn_ref, w_ref, fn_ref, o_ref = refs[4 * nb:]
    for bi in range(nb):
        _na_tile(q_ref.at[bi], *kv[4 * bi:4 * bi + 4], bias_ref, h_ref.at[bi], gate_ref.at[bi], of_ref.at[bi],
                 or_ref.at[bi], z_ref.at[bi], gn_ref, w_ref, fn_ref, o_ref.at[bi])


def _na_out1(nq, nk, nv, bias, h, gate, of, orv, z, gla_norm, w_out, final_norm, n_ctx, rows):
    bsz, ttot, nd = nq.shape
    nrb = rows // NA_QROWS
    qrows = NA_QROWS * GRID_W
    krows = NA_KROWS * GRID_W
    assert qrows == TM
    nct = n_ctx // TM
    nc = GLA_HEADS * GLA_DV
    sb = math.gcd(bsz, TILE_BATCH)

    def variant(rb):
        return jnp.where(rb == 0, 0, jnp.where(rb == nrb - 1, 2, 1))

    def key_start(rb):
        first_row = jnp.clip(rb * NA_QROWS - NA_KH // 2, 0, rows - NA_KROWS)
        return pl.multiple_of(n_ctx + first_row * GRID_W, GRID_W)

    window = lambda i: pl.BlockSpec((None, pl.Element(krows), pl.Element(nd)), lambda b, rb: (b * sb + i, key_start(rb), 0))
    context = lambda i: pl.BlockSpec((None, pl.Element(n_ctx), pl.Element(nd)), lambda b, rb: (b * sb + i, 0, 0))
    kv_specs = [spec(i) for i in range(sb) for spec in (window, window, context, context)]
    lat = lambda w: pl.BlockSpec((sb, TM, w), lambda b, rb: (b, rb + nct, 0))
    return pl.pallas_call(
        functools.partial(_na_out1_kernel, nb=sb),
        out_shape=jax.ShapeDtypeStruct((bsz, rows * GRID_W, D_MODEL), F32),
        grid=(bsz // sb, nrb),
        in_specs=kv_specs + [
            lat(nd), pl.BlockSpec((NA_HEADS, qrows, krows), lambda b, rb: (0, variant(rb), 0)),
            lat(D_MODEL), pl.BlockSpec((sb, None, 1, D_MODEL), lambda b, rb: (b, 1, 0, 0)),
            lat(nc), lat(nc), lat(nc + nd), _full((1, nc)), _full(w_out.shape), _full((1, D_MODEL))],
        out_specs=pl.BlockSpec((sb, TM, D_MODEL), lambda b, rb: (b, rb, 0)),
        compiler_params=_params("parallel", "parallel"),
        name="na_out1",
    )(*([nk, nv, nk, nv] * sb), nq, bias, h, gate, of, orv, z, gla_norm[None], w_out.astype(BF16), final_norm[None])


def kernel(x, c, ctx, c_ctx, l0_norm, l0_w_mod, l0_b_mod, l0_w_in, l0_mla_q_norm, l0_mla_w_uq, l0_mla_kv_norm, l0_mla_w_ukv, l0_mlstm_conv_w, l0_mlstm_conv_b, l0_mlstm_b_i, l0_mlstm_b_f, l0_mlstm_norm, l0_w_out, l1_norm, l1_w_mod, l1_b_mod, l1_w_in, l1_gla_w_gate, l1_gla_b_gate, l1_gla_norm, l1_na_rpb, l1_w_out, final_norm):
    bsz, seq, d = x.shape
    n_ctx = ctx.shape[1]
    rows = seq // GRID_W
    assert d == D_MODEL and seq % TM == 0 and n_ctx == TM
    assert rows % NA_QROWS == 0 and rows >= NA_KROWS and rows // NA_QROWS >= 3
    nct = n_ctx // TM

    (shift0, scale0, gate0), (shift, scale, gate) = _mod_vectors(
        c, c_ctx, ((l0_w_mod, l0_b_mod), (l1_w_mod, l1_b_mod)))

    cos_t, sin_t = _rope_tables(n_ctx, seq)
    w0 = _proj0_weights(l0_w_in, l0_mla_q_norm, l0_mla_w_uq, l0_mla_kv_norm, l0_mla_w_ukv,
                        l0_mlstm_conv_w, l0_mlstm_conv_b, l0_mlstm_b_i, l0_mlstm_b_f)
    q, k, v, mq, mk, mv, g_rows, g_cols, mo, z0 = _proj0(ctx, x, scale0, shift0, l0_norm, cos_t, sin_t, w0, nct)
    a = _mla_attention(q, k, v, nct)
    hf, hr = _mlstm(mq, mk, mv, g_rows, g_cols, nct)

    w1 = _proj1_weights(l1_w_in, l1_gla_w_gate, l1_gla_b_gate)
    h, gq, gk, gv, bc, nq, nk, nv, z = _mid(ctx, x, gate0, a, hf, hr, mo, z0, l0_mlstm_norm, l0_w_out,
                                            scale, shift, l1_norm, w1, nct)
    of, orv = _gla(gq, gk, gv, bc, nct)
    return _na_out1(nq, nk, nv, _na_bias_tables(l1_na_rpb, rows), h, gate, of, orv, z,
                    l1_gla_norm, l1_w_out, final_norm, n_ctx, rows)
```

```python
import functools
import math

import jax
import jax.numpy as jnp
import numpy as np
from jax import lax
from jax.experimental import pallas as pl
from jax.experimental.pallas import tpu as pltpu

F32 = jnp.float32
BF16 = jnp.bfloat16

D_MODEL = 1024
GRID_W = 64
EPS = 1e-6
ROPE_BASE = 10000.0

MLA_HEADS = 8
MLA_Q_RANK = 384
MLA_KV_RANK = 256
MLA_NOPE = 64
MLA_ROPE = 32
MLA_V = 64
ROPE_PAIRS = MLA_ROPE // 4
MLSTM_HEADS = 4
MLSTM_DH = 128
MLSTM_CONV = 3
GLA_HEADS = 4
GLA_DK = 64
GLA_DV = 128
GLA_GATE_RANK = 16
GLA_TAU = 16.0
NA_HEADS = 8
NA_DH = 64
NA_KH = 8
NA_KW = 16

CHUNK = 64
MLSTM_CHUNK = 128
TM = 256
LANES = 128
SUBLANES = 8
HALO = SUBLANES
NA_QROWS = 4
NA_KROWS = NA_QROWS + NA_KH - 1
GLA_LEAF = 8
MLA_HPS = 4
TILE_BATCH = 2
MLSTM_BATCH = 2
GLA_BATCH = 8
VMEM_LIMIT = 56 * 1024 * 1024
NEG_BIG = -1e30
LOG2E = 1.4426950408889634


def _dot(a, b):
    return jnp.dot(a, b, preferred_element_type=F32)


def _dot_nt(a, b):
    return lax.dot_general(a, b, (((1,), (1,)), ((), ())), preferred_element_type=F32)


def _dot_tn(a, b):
    return lax.dot_general(a, b, (((0,), (0,)), ((), ())), preferred_element_type=F32)


def _rms(x):
    return x * lax.rsqrt(jnp.mean(x * x, axis=-1, keepdims=True) + EPS)


def _sigmoid(x):
    return 1.0 / (1.0 + jnp.exp(-x))


def _silu(x):
    return x * _sigmoid(x)


def _log_sigmoid(x):
    return jnp.minimum(x, 0.0) - jnp.log(1.0 + jnp.exp(-jnp.abs(x)))


def _params(*sem):
    return pltpu.CompilerParams(dimension_semantics=sem, vmem_limit_bytes=VMEM_LIMIT)


def _full(shape):
    nd = len(shape)
    return pl.BlockSpec(shape, lambda *_: (0,) * nd)


def _mod_kernel(c_ref, w0_ref, b0_ref, w1_ref, b1_ref, o0_ref, o1_ref):
    sc = _silu(c_ref[...]).astype(BF16)
    o0_ref[...] = _dot(sc, w0_ref[...].astype(BF16)) + b0_ref[...]
    o1_ref[...] = _dot(sc, w1_ref[...].astype(BF16)) + b1_ref[...]


def _mod_vectors(c, c_ctx, mods):
    bsz, d = c.shape
    rows = -(-(bsz + 1) // SUBLANES) * SUBLANES
    cc = jnp.zeros((rows, d), F32).at[:bsz].set(c).at[bsz].set(c_ctx)
    (w0, b0), (w1, b1) = mods
    n = w0.shape[1]
    wspec = pl.BlockSpec((d, d), lambda j: (0, j))
    vspec = pl.BlockSpec((1, d), lambda j: (0, j))
    ospec = pl.BlockSpec((rows, d), lambda j: (0, j))
    per_layer = pl.pallas_call(
        _mod_kernel,
        out_shape=(jax.ShapeDtypeStruct((rows, n), F32),) * 2,
        grid=(n // d,),
        in_specs=[_full((rows, d)), wspec, vspec, wspec, vspec],
        out_specs=(ospec, ospec),
        compiler_params=_params("parallel"),
        name="modulation",
    )(cc, w0, b0[None], w1, b1[None])
    out = []
    for mod in per_layer:
        vecs = []
        for part in jnp.split(mod, 3, axis=-1):
            ctx_v = jnp.broadcast_to(part[bsz][None], (bsz, d))
            vecs.append(jnp.stack([ctx_v, part[:bsz]], axis=1)[:, :, None, :])
        out.append(vecs)
    return out


def _modnorm(x, g, scale, shift):
    return _rms(x) * g * (1.0 + scale) + shift


def _seg_scan(x, reverse, op, fill):
    width = x.shape[1]
    pos = lax.broadcasted_iota(jnp.int32, x.shape, 1) % MLSTM_CHUNK
    k = 1
    while k < MLSTM_CHUNK:
        if reverse:
            x = op(x, jnp.where(pos < MLSTM_CHUNK - k, pltpu.roll(x, width - k, axis=1), fill))
        else:
            x = op(x, jnp.where(pos >= k, pltpu.roll(x, k, axis=1), fill))
        k *= 2
    return x


def _split3(x):
    hi = x.astype(BF16)
    r1 = x - hi.astype(F32)
    mid = r1.astype(BF16)
    return hi, mid, (r1 - mid.astype(F32)).astype(BF16)


def _proj0_kernel(ctx_ref, lat_ref, hp_ref, hn_ref, sc_ref, sh_ref, g_ref, cos_ref, sin_ref,
                  wall_ref, wgt_ref, qn_ref, wuq_ref, kvn_ref, wk_ref, wv_ref, cw_ref, cb_ref, gb_ref,
                  q_out, k_out, v_out, mq_out, mk_out, mv_out, gr_out, gc_out, mo_out, z_out,
                  pbuf, *, nct, nt):
    t = pl.program_id(1)
    h = jnp.where(t < nct, ctx_ref[...], lat_ref[...])
    x = jnp.concatenate([hp_ref[...], h, hn_ref[...]], axis=0)
    ub_ext = _modnorm(x, g_ref[...], sc_ref[...], sh_ref[...]).astype(BF16)
    ub = ub_ext[HALO:HALO + TM]

    gt = _dot_nt(wgt_ref[...], ub) + gb_ref[...]
    half = MLSTM_HEADS
    per_dir = []
    for d in range(2):
        gd = gt[d * SUBLANES:(d + 1) * SUBLANES]
        bcum = _seg_scan(_log_sigmoid(gd), d == 1, jnp.add, 0.0)
        b_top = pltpu.roll(bcum, half, axis=0)
        c8 = gd - b_top
        pm8 = _seg_scan(c8, d == 1, jnp.maximum, -jnp.inf)
        per_dir.append((c8, pm8, b_top))

    cos = cos_ref[...]
    sin = sin_ref[...]
    cos_all = jnp.concatenate([cos] * MLA_HEADS, axis=1)
    sin_all = jnp.concatenate([sin] * MLA_HEADS, axis=1)

    def swap_halves(a):
        lane = lax.broadcasted_iota(jnp.int32, a.shape, 1)
        first = lane % (2 * ROPE_PAIRS) < ROPE_PAIRS
        return jnp.where(first, pltpu.roll(a, a.shape[1] - ROPE_PAIRS, axis=1), pltpu.roll(a, ROPE_PAIRS, axis=1))

    wseg = lambda name: wall_ref[:, PROJ0_COLS[name][0]:PROJ0_COLS[name][1]]
    cqkr = _dot(ub, wseg("cqkr"))
    cq = cqkr[:, :MLA_Q_RANK]
    kr = cqkr[:, MLA_Q_RANK:]
    ckv = _dot(ub, wseg("ckv"))
    pqk = _dot(ub_ext, wseg("mqk"))

    nq = (_rms(cq) * qn_ref[...]).astype(BF16)
    nkv = (_rms(ckv) * kvn_ref[...]).astype(BF16)

    q = _dot(nq, wuq_ref[...])
    kn = _dot(nkv, wk_ref[...])
    vv = _dot(nkv, wv_ref[...])

    prev_ok = t > nct
    next_ok = jnp.logical_and(t >= nct, t != nt - 1)
    row = lax.broadcasted_iota(jnp.int32, pqk.shape, 0)
    keep = jnp.logical_and(jnp.logical_or(row >= HALO, prev_ok), jnp.logical_or(row < HALO + TM, next_ok))
    pbuf[...] = jnp.where(keep, pqk, 0.0)
    cw = cw_ref[...]
    y = (pbuf[HALO - 1:HALO - 1 + TM, :] * cw[0:1] + pbuf[HALO:HALO + TM, :] * cw[1:2]
         + pbuf[HALO + 1:HALO + 1 + TM, :] * cw[2:3] + cb_ref[...])
    y = _silu(y)
    nb = MLSTM_HEADS * MLSTM_DH
    mq_out[...] = (y[:, :nb] * MLSTM_DH ** -0.5).astype(BF16)
    mk_out[...] = y[:, nb:].astype(BF16)

    mv = _dot(ub, wseg("mv"))
    mo = _dot(ub, wseg("mo"))

    q = q * cos_all + swap_halves(q) * sin_all
    q_out[...] = (q * ((MLA_NOPE + MLA_ROPE) ** -0.5 * LOG2E)).astype(BF16)
    rot = kr * cos + swap_halves(kr) * sin
    k_out[...] = (kn + jnp.concatenate([rot] * MLA_HEADS, axis=1)).astype(BF16)
    vlane = lax.broadcasted_iota(jnp.int32, (TM, MLA_HEADS * LANES), 1)
    is_value = ((vlane % LANES) < MLA_V) == ((vlane // LANES) % 2 == 0)
    v_out[...] = jnp.where(is_value, vv, 1.0).astype(BF16)

    zz = _dot(ub, wseg("z"))
    mv_out[...] = mv.astype(BF16)
    mo_out[...] = _sigmoid(mo).astype(BF16)
    z_out[...] = _silu(zz).astype(BF16)

    r8 =lax.broadcasted_iota(jnp.int32, (SUBLANES, TM), 0)
    both = lambda i: jnp.where(r8 < half, per_dir[0][i], pltpu.roll(per_dir[1][i], half, axis=0)) * LOG2E
    c_rows, pm_rows, b_rows = both(0), both(1), both(2)
    for ci in range(TM // MLSTM_CHUNK):
        gr_out[ci] = c_rows[:, ci * MLSTM_CHUNK:(ci + 1) * MLSTM_CHUNK]
    stack = jnp.concatenate([c_rows, pm_rows, b_rows, jnp.zeros((LANES - 3 * SUBLANES, TM), F32)], axis=0)
    er = lax.broadcasted_iota(jnp.int32, (TM, TM), 0)
    ec = lax.broadcasted_iota(jnp.int32, (TM, TM), 1)
    eye = jnp.where(er == ec, 1.0, 0.0).astype(BF16)
    gc_out[...] = sum(_dot_nt(eye, part) for part in _split3(stack))


def _rope_tables(n_ctx, seq):
    t = jnp.arange(seq)
    inv = 1.0 / (ROPE_BASE ** (jnp.arange(ROPE_PAIRS, dtype=F32) / ROPE_PAIRS))
    ang = jnp.concatenate([(t // GRID_W)[:, None] * inv, (t % GRID_W)[:, None] * inv], axis=-1)
    cos, sin = jnp.cos(ang), jnp.sin(ang)
    j = np.arange(MLA_ROPE)
    src = (j // (2 * ROPE_PAIRS)) * ROPE_PAIRS + (j % ROPE_PAIRS)
    sign = np.where((j % (2 * ROPE_PAIRS)) < ROPE_PAIRS, -1.0, 1.0).astype(np.float32)
    cos_full = cos[:, src]
    sin_full = sin[:, src] * sign
    cos_t = jnp.ones((n_ctx + seq, LANES), F32).at[n_ctx:, MLA_NOPE:MLA_NOPE + MLA_ROPE].set(cos_full)
    sin_t = jnp.zeros((n_ctx + seq, LANES), F32).at[n_ctx:, MLA_NOPE:MLA_NOPE + MLA_ROPE].set(sin_full)
    return cos_t, sin_t


def _segments(widths):
    out, start = {}, 0
    for name, width in widths:
        out[name] = (start, start + width)
        start += width
    return out


PROJ0_COLS = _segments([("cqkr", MLA_Q_RANK + LANES), ("ckv", MLA_KV_RANK), ("mqk", 2 * MLSTM_HEADS * MLSTM_DH),
                        ("mv", MLSTM_HEADS * MLSTM_DH), ("mo", MLSTM_HEADS * MLSTM_DH), ("z", D_MODEL)])
PROJ1_COLS = _segments([("q", GLA_HEADS * GLA_DK), ("k", GLA_HEADS * GLA_DK), ("v", GLA_HEADS * GLA_DV),
                        ("ga", LANES), ("nq", NA_HEADS * NA_DH), ("nk", NA_HEADS * NA_DH), ("nv", NA_HEADS * NA_DH),
                        ("z", GLA_HEADS * GLA_DV + NA_HEADS * NA_DH)])


def _proj0_weights(w_in, q_norm, w_uq, kv_norm, w_ukv, conv_w, conv_b, b_i, b_f):
    o = np.cumsum([0, MLA_Q_RANK, MLA_KV_RANK, MLA_ROPE] + [MLSTM_HEADS * MLSTM_DH] * 4 + [4 * MLSTM_HEADS, D_MODEL])
    dq = MLA_NOPE + MLA_ROPE
    wkr = jnp.pad(w_in[:, o[2]:o[3]], ((0, 0), (MLA_NOPE, LANES - dq)))
    wuq = jnp.pad(w_uq.reshape(MLA_Q_RANK, MLA_HEADS, dq), ((0, 0), (0, 0), (0, LANES - dq)))
    wkv = w_ukv.reshape(MLA_KV_RANK, MLA_HEADS, MLA_NOPE + MLA_V)
    wk = jnp.pad(wkv[:, :, :MLA_NOPE], ((0, 0), (0, 0), (0, LANES - MLA_NOPE)))
    zv = jnp.zeros((MLA_KV_RANK, MLA_HEADS // 2, MLA_V), F32)
    wv = jnp.concatenate([wkv[:, 0::2, MLA_NOPE:], zv, zv, wkv[:, 1::2, MLA_NOPE:]], axis=-1)
    wv = wv.reshape(MLA_KV_RANK, MLA_HEADS * LANES)
    gbias = jnp.stack([b_i, b_f], axis=1).reshape(4 * MLSTM_HEADS, 1)
    bf = lambda a: a.astype(BF16)
    wall = jnp.concatenate([w_in[:, o[0]:o[1]], wkr, w_in[:, o[1]:o[2]], w_in[:, o[3]:o[7]], w_in[:, o[8]:o[9]]], axis=1)
    assert wall.shape[1] == PROJ0_COLS["z"][1]
    return dict(
        wall=bf(wall), wgt=bf(w_in[:, o[7]:o[8]].T),
        qn=q_norm[None], wuq=bf(wuq.reshape(MLA_Q_RANK, -1)),
        kvn=kv_norm[None], wk=bf(wk.reshape(MLA_KV_RANK, -1)), wv=bf(wv),
        cw=jnp.zeros((SUBLANES, conv_w.shape[1]), F32).at[:MLSTM_CONV].set(conv_w), cb=conv_b[None], gb=gbias)


def _tile_specs(nct, nt, nb):
    tile = lambda w: pl.BlockSpec((nb, TM, w), lambda b, t: (b, t, 0))
    per = TM // HALO
    nlat = nt - nct
    src_ctx = pl.BlockSpec((nb, TM, D_MODEL), lambda b, t: (b, jnp.minimum(t, nct - 1), 0))
    src_lat = pl.BlockSpec((nb, TM, D_MODEL), lambda b, t: (b, jnp.maximum(t - nct, 0), 0))
    prev = pl.BlockSpec((nb, HALO, D_MODEL), lambda b, t: (b, jnp.maximum((t - nct) * per - 1, 0), 0))
    nxt = pl.BlockSpec((nb, HALO, D_MODEL),
                       lambda b, t: (b, jnp.clip((t - nct + 1) * per, 0, nlat * per - 1), 0))
    mod = pl.BlockSpec((nb, None, 1, D_MODEL), lambda b, t: (b, (t >= nct).astype(jnp.int32), 0, 0))
    return tile, src_ctx, src_lat, prev, nxt, mod


def _per_element(tile_kernel, shared):
    def kernel(*refs, **kw):
        for bi in range(refs[0].shape[0]):
            tile_kernel(*[r if i in shared else r.at[bi] for i, r in enumerate(refs)], **kw)
    return kernel


def _proj0(ctx, x, scale, shift, norm_g, cos_t, sin_t, w, nct):
    bsz = x.shape[0]
    ttot = ctx.shape[1] + x.shape[1]
    nt = ttot // TM
    nb = MLSTM_HEADS * MLSTM_DH
    sb = math.gcd(bsz, TILE_BATCH)
    tile, src_ctx, src_lat, prev, nxt, mod = _tile_specs(nct, nt, sb)
    tab = pl.BlockSpec((TM, LANES), lambda b, t: (t, 0))
    wnames = ["wall", "wgt", "qn", "wuq", "kvn", "wk", "wv", "cw", "cb", "gb"]
    ws = [w[n] for n in wnames]
    tok = lambda width, dt: jax.ShapeDtypeStruct((bsz, ttot, width), dt)
    out_shape = (tok(MLA_HEADS * LANES, BF16), tok(MLA_HEADS * LANES, BF16), tok(MLA_HEADS * LANES, BF16),
                 tok(nb, BF16), tok(nb, BF16), tok(nb, BF16),
                 jax.ShapeDtypeStruct((bsz, ttot // MLSTM_CHUNK, 2 * MLSTM_HEADS, MLSTM_CHUNK), F32), tok(LANES, F32),
                 tok(nb, BF16), tok(D_MODEL, BF16))
    out_specs = (tile(MLA_HEADS * LANES), tile(MLA_HEADS * LANES), tile(MLA_HEADS * LANES),
                 tile(nb), tile(nb), tile(nb),
                 pl.BlockSpec((sb, TM // MLSTM_CHUNK, 2 * MLSTM_HEADS, MLSTM_CHUNK), lambda b, t: (b, t, 0, 0)),
                 tile(LANES),
                 tile(nb), tile(D_MODEL))
    n_blocked, n_out = 6, len(out_shape)
    n_in = n_blocked + 3 + len(ws)
    shared = set(range(n_blocked, n_in)) | {n_in + n_out}
    return pl.pallas_call(
        functools.partial(_per_element(_proj0_kernel, shared), nct=nct, nt=nt),
        out_shape=out_shape,
        grid=(bsz // sb, nt),
        in_specs=[src_ctx, src_lat, prev, nxt, mod, mod, _full((1, D_MODEL)), tab, tab] + [_full(a.shape) for a in ws],
        out_specs=out_specs,
        scratch_shapes=[pltpu.VMEM((TM + 2 * HALO, 2 * nb), F32)],
        compiler_params=_params("parallel", "parallel"),
        name="proj0",
    )(ctx, x, x, x, scale, shift, norm_g[None], cos_t, sin_t, *ws)


def _mla_kernel(q_ref, k_ref, v_ref, o_ref, *, nct):
    t = pl.program_id(2)
    lane = lax.broadcasted_iota(jnp.int32, (TM, LANES), 1)

    def attend(nkeys):
        qk = lambda j: _dot_nt(q_ref[:, j * LANES:(j + 1) * LANES], k_ref[0:nkeys, j * LANES:(j + 1) * LANES])
        outs = []
        s_next = qk(0)
        for j in range(MLA_HPS):
            s = s_next
            if j + 1 < MLA_HPS:
                s_next = qk(j + 1)
            p = jnp.exp2(s - jnp.max(s, axis=-1, keepdims=True))
            o = _dot(p.astype(BF16), v_ref[0:nkeys, j * LANES:(j + 1) * LANES])
            outs.append(o / pltpu.roll(o, MLA_V, axis=1))
        for j in range(MLA_HPS // 2):
            o_ref[:, j * LANES:(j + 1) * LANES] = jnp.where(lane < MLA_V, outs[2 * j], outs[2 * j + 1]).astype(BF16)

    @pl.when(t < nct)
    def _():
        attend(nct * TM)

    @pl.when(t >= nct)
    def _():
        attend(k_ref.shape[0])


def _mla_attention(q, k, v, nct):
    bsz, ttot, _ = q.shape
    nt = ttot // TM
    hw = MLA_HPS * LANES
    return pl.pallas_call(
        functools.partial(_mla_kernel, nct=nct),
        out_shape=jax.ShapeDtypeStruct((bsz, ttot, MLA_HEADS * MLA_V), BF16),
        grid=(bsz, MLA_HEADS // MLA_HPS, nt),
        in_specs=[pl.BlockSpec((None, TM, hw), lambda b, p, t: (b, t, p)),
                  pl.BlockSpec((None, ttot, hw), lambda b, p, t: (b, 0, p)),
                  pl.BlockSpec((None, ttot, hw), lambda b, p, t: (b, 0, p))],
        out_specs=pl.BlockSpec((None, TM, hw // 2), lambda b, p, t: (b, t, p)),
        compiler_params=_params("parallel", "parallel", "parallel"),
        name="mla_attention",
    )(q, k, v)


def _scan_block(j, nct, nt):
    return jnp.where(j < nct, nct - 1 - j, nt - 1 - (j - nct))


def _mlstm_kernel(qf_ref, kf_ref, vf_ref, grf_ref, gcf_ref, qr_ref, kr_ref, vr_ref, grr_ref, gcr_ref,
                  hf_ref, hr_ref, c_scr, m_scr):
    @pl.when(pl.program_id(1) == 0)
    def _():
        c_scr[...] = jnp.zeros_like(c_scr)
        m_scr[...] = jnp.zeros_like(m_scr)

    L = MLSTM_CHUNK
    nchunk = TM // L
    nh = MLSTM_HEADS
    nchain = 2 * nh
    row = lax.broadcasted_iota(jnp.int32, (L, L), 0)
    col = lax.broadcasted_iota(jnp.int32, (L, L), 1)
    ones = jnp.ones((L, MLSTM_DH), BF16)
    wide = lambda a: jnp.broadcast_to(a, (L, MLSTM_DH))
    twice = lambda a: jnp.concatenate([a, a], axis=1)
    refs = ((qf_ref, kf_ref, vf_ref, grf_ref, gcf_ref, hf_ref), (qr_ref, kr_ref, vr_ref, grr_ref, gcr_ref, hr_ref))
    nbatch = qf_ref.shape[0]

    def load(ci):
        cs = []
        for bi, d in [(bi, d) for bi in range(nbatch) for d in range(2)]:
            q_ref, k_ref, v_ref, gr_ref, gc_ref, o_ref = (r.at[bi] for r in refs[d])
            cd = ci if d == 0 else nchunk - 1 - ci
            rs = slice(cd * L, (cd + 1) * L)
            g_rows = gr_ref[cd]
            g_cols = gc_ref[rs, :]
            last = L - 1 if d == 0 else 0
            for hd in range(nh):
                ch = d * nh + hd
                hs = slice(hd * MLSTM_DH, (hd + 1) * MLSTM_DH)
                cs.append(dict(
                    rs=rs, hs=hs, o_ref=o_ref, causal=(col <= row) if d == 0 else (col >= row),
                    c_row=g_rows[ch:ch + 1], c_col=g_cols[:, ch:ch + 1],
                    pm_col=g_cols[:, nchain + ch:nchain + ch + 1], b_col=g_cols[:, 2 * nchain + ch:2 * nchain + ch + 1],
                    pm_last=g_cols[last:last + 1, nchain + ch:nchain + ch + 1],
                    b_last=g_cols[last:last + 1, 2 * nchain + ch:2 * nchain + ch + 1],
                    q=q_ref[rs, hs], k=k_ref[rs, hs],
                    v_ext=jnp.concatenate([v_ref[rs, hs], ones], axis=1)))
        return cs

    def prep_scores(cs):
        for c in cs:
            c["s"] = _dot_nt(c["q"], c["k"])

    def prep_weights(cs):
        for c in cs:
            c["pm_w"] = wide(c["pm_col"])
            c["b_w"] = wide(c["b_col"])
            c["w"] = jnp.exp2(jnp.where(c["causal"], c["c_row"] - c["pm_w"][:, :L], -jnp.inf))
            c["kwf"] = jnp.exp2(wide(c["c_col"] - c["pm_last"]))

    def prep_operands(cs):
        for c in cs:
            c["p"] = (c["s"] * c["w"]).astype(BF16)
            c["kw"] = (c["kwf"] * c["k"].astype(F32)).astype(BF16)

    def prep_products(cs):
        for c in cs:
            c["o1"] = _dot(c["p"], c["v_ext"])
            c["dc"] = _dot_tn(c["kw"], c["v_ext"])

    def scan_read(cs):
        for ch, c in enumerate(cs):
            c["qc"] = _dot(c["q"], state[ch].astype(BF16))

    def scan_factors(cs):
        for ch, c in enumerate(cs):
            g_w = jnp.maximum(m[ch], c["pm_w"])
            c["e_intra"] = jnp.exp2(c["pm_w"] - g_w)
            c["e_inter"] = jnp.exp2(m[ch] - g_w)
            c["floor"] = jnp.exp2(-(c["b_w"] + g_w))
            g_last = jnp.maximum(m[ch], c["pm_last"])
            c["keep"] = jnp.exp2(m[ch] - g_last)
            c["gain"] = jnp.exp2(c["pm_last"] - g_last)
            c["m_new"] = c["b_last"] + g_last

    def scan_update(cs):
        for ch, c in enumerate(cs):
            o = twice(c["e_intra"]) * c["o1"] + twice(c["e_inter"]) * c["qc"]
            h = o[:, :MLSTM_DH] / jnp.maximum(jnp.abs(o[:, MLSTM_DH:]), c["floor"])
            c["o_ref"][c["rs"], c["hs"]] = h.astype(BF16)
            state[ch] = c["keep"] * state[ch] + c["gain"] * c["dc"]
            m[ch] = c["m_new"]

    nstate = nbatch * nchain
    state = [c_scr[ch] for ch in range(nstate)]
    m = [m_scr[ch, 0:1, 0:1] for ch in range(nstate)]
    chunks = [load(ci) for ci in range(nchunk)]
    everything = [c for cs in chunks for c in cs]
    for stage in (prep_scores, prep_weights, prep_operands):
        stage(everything)
    prep_products(chunks[0])
    for i, cs in enumerate(chunks):
        scan_read(cs)
        if i + 1 < nchunk:
            prep_products(chunks[i + 1])
        scan_factors(cs)
        scan_update(cs)
    for ch in range(nstate):
        c_scr[ch] = state[ch]
        m_scr[ch] = jnp.broadcast_to(m[ch], m_scr.shape[1:])


def _mlstm(mq, mk, mv, g_rows, g_cols, nct):
    bsz, ttot, nb = mq.shape
    nt = ttot // TM
    per = TM // MLSTM_CHUNK
    sb = math.gcd(bsz, MLSTM_BATCH)
    fwd = lambda w: pl.BlockSpec((sb, TM, w), lambda b, j: (b, j, 0))
    rev = lambda w: pl.BlockSpec((sb, TM, w), lambda b, j: (b, _scan_block(j, nct, nt), 0))
    gshape = (sb, per, 2 * MLSTM_HEADS, MLSTM_CHUNK)
    gfwd = pl.BlockSpec(gshape, lambda b, j: (b, j, 0, 0))
    grev = pl.BlockSpec(gshape, lambda b, j: (b, _scan_block(j, nct, nt), 0, 0))
    nstate = sb * 2 * MLSTM_HEADS
    return pl.pallas_call(
        _mlstm_kernel,
        out_shape=(jax.ShapeDtypeStruct((bsz, ttot, nb), BF16),) * 2,
        grid=(bsz // sb, nt),
        in_specs=[fwd(nb), fwd(nb), fwd(nb), gfwd, fwd(LANES), rev(nb), rev(nb), rev(nb), grev, rev(LANES)],
        out_specs=(fwd(nb), rev(nb)),
        scratch_shapes=[pltpu.VMEM((nstate, MLSTM_DH, 2 * MLSTM_DH), F32),
                        pltpu.VMEM((nstate, SUBLANES, LANES), F32)],
        compiler_params=_params("parallel", "arbitrary"),
        name="mlstm_scan",
    )(mq, mk, mv, g_rows, g_cols, mq, mk, mv, g_rows, g_cols)


def _head_rms(x, n_heads):
    w = x.shape[1] // n_heads
    return jnp.concatenate([_rms(x[:, i * w:(i + 1) * w]) for i in range(n_heads)], axis=1)


def _mid_kernel(ctx_ref, lat_ref, gate_ref, a_ref, hf_ref, hr_ref, mo_ref, z0_ref, hn_ref, wout_ref,
                sc_ref, sh_ref, g_ref, wall_ref, wgate_ref, bgate_ref,
                h_out, gq_out, gk_out, gv_out, bc_out, nq_out, nk_out, nv_out, z_out, *, nct):
    nbt = ctx_ref.shape[0]
    rows = lambda f: jnp.concatenate([f(bi) for bi in range(nbt)], axis=0)
    f32 = lambda r, bi: r[bi].astype(F32)

    def put(ref, val):
        for bi in range(nbt):
            ref[bi] = val[bi * TM:(bi + 1) * TM]

    hm = rows(lambda bi: _head_rms(f32(mo_ref, bi) * (f32(hf_ref, bi) + f32(hr_ref, bi)), MLSTM_HEADS) * hn_ref[...])
    cat = (jnp.concatenate([rows(lambda bi: f32(a_ref, bi)), hm], axis=1) * rows(lambda bi: f32(z0_ref, bi))).astype(BF16)
    y = _dot(cat, wout_ref[...])
    is_ctx = pl.program_id(1) < nct
    hs = [jnp.where(is_ctx, ctx_ref[bi], lat_ref[bi]) + gate_ref[bi] * y[bi * TM:(bi + 1) * TM] for bi in range(nbt)]
    for bi in range(nbt):
        h_out[bi] = hs[bi]

    ub = rows(lambda bi: _modnorm(hs[bi], g_ref[...], sc_ref[bi], sh_ref[bi])).astype(BF16)
    wseg = lambda name: wall_ref[:, PROJ1_COLS[name][0]:PROJ1_COLS[name][1]]
    ga = _dot(ub, wseg("ga")).astype(BF16)
    gate_pre = _dot(ga, wgate_ref[...])
    put(gq_out, (_dot(ub, wseg("q")) * GLA_DK ** -0.5).astype(BF16))
    put(gk_out, _dot(ub, wseg("k")).astype(BF16))
    put(gv_out, _dot(ub, wseg("v")).astype(BF16))

    lg = _log_sigmoid(gate_pre + bgate_ref[...]) * (1.0 / GLA_TAU)
    hi, mid, lo = _split3(lg)
    put(nq_out, (_dot(ub, wseg("nq")) * (NA_DH ** -0.5 * LOG2E)).astype(BF16))
    put(nk_out, _dot(ub, wseg("nk")).astype(BF16))
    put(nv_out, _dot(ub, wseg("nv")).astype(BF16))
    row = lax.broadcasted_iota(jnp.int32, (TM, TM), 0)
    col = lax.broadcasted_iota(jnp.int32, (TM, TM), 1)
    same = (row // CHUNK) == (col // CHUNK)
    nk = GLA_HEADS * GLA_DK
    for d in range(2):
        tri = jnp.where(jnp.logical_and(same, (col <= row) if d == 0 else (col >= row)), 1.0, 0.0).astype(BF16)
        cols = slice(d * nk, (d + 1) * nk)
        for bi in range(nbt):
            rs = slice(bi * TM, (bi + 1) * TM)
            bc_out[bi, :, cols] = (_dot(tri, hi[rs, cols]) + _dot(tri, mid[rs, cols]) + _dot(tri, lo[rs, cols])) * LOG2E
    put(z_out, _silu(_dot(ub, wseg("z"))).astype(BF16))


def _proj1_weights(w_in, w_gate, b_gate):
    nk = GLA_HEADS * GLA_DK
    nc = GLA_HEADS * GLA_DV
    nd = NA_HEADS * NA_DH
    o = np.cumsum([0, nk, nk, nc, 2 * GLA_GATE_RANK, nd, nd, nd, nc + nd])
    r = GLA_GATE_RANK
    gpad = LANES - 2 * r
    wall = jnp.concatenate([w_in[:, :o[4]], jnp.zeros((w_in.shape[0], gpad), F32), w_in[:, o[4]:]], axis=1)
    assert wall.shape[1] == PROJ1_COLS["z"][1]
    wgate = jnp.zeros((LANES, 2 * nk), F32).at[:r, :nk].set(w_gate[0]).at[r:2 * r, nk:].set(w_gate[1])
    return dict(wall=wall.astype(BF16), wgate=wgate.astype(BF16), bgate=b_gate.reshape(1, 2 * nk))


def _mid(ctx, x, gate0, a, hf, hr, mo, z0, h_norm, w_out, scale, shift, norm_g, w, nct):
    bsz = x.shape[0]
    ttot = ctx.shape[1] + x.shape[1]
    nt = ttot // TM
    sb = math.gcd(bsz, TILE_BATCH)
    tile, src_ctx, src_lat, _, _, mod = _tile_specs(nct, nt, sb)
    nb = MLSTM_HEADS * MLSTM_DH
    nk = GLA_HEADS * GLA_DK
    nc = GLA_HEADS * GLA_DV
    nd = NA_HEADS * NA_DH
    ws = [w[n] for n in ("wall", "wgate", "bgate")]
    tok = lambda width, dt: jax.ShapeDtypeStruct((bsz, ttot, width), dt)
    return pl.pallas_call(
        functools.partial(_mid_kernel, nct=nct),
        out_shape=(tok(D_MODEL, F32), tok(nk, BF16), tok(nk, BF16), tok(nc, BF16), tok(2 * nk, F32),
                   tok(nd, BF16), tok(nd, BF16), tok(nd, BF16), tok(nc + nd, BF16)),
        grid=(bsz // sb, nt),
        in_specs=[src_ctx, src_lat, mod, tile(MLA_HEADS * MLA_V), tile(nb), tile(nb), tile(nb), tile(D_MODEL),
                  _full((1, nb)), _full(w_out.shape), mod, mod, _full((1, D_MODEL))] + [_full(a.shape) for a in ws],
        out_specs=(tile(D_MODEL), tile(nk), tile(nk), tile(nc), tile(2 * nk), tile(nd), tile(nd), tile(nd),
                   tile(nc + nd)),
        compiler_params=_params("parallel", "parallel"),
        name="out0_proj1",
    )(ctx, x, gate0, a, hf, hr, mo, z0, h_norm[None], w_out.astype(BF16), scale, shift, norm_g[None], *ws)


def _gla_kernel(qf_ref, kf_ref, vf_ref, bf_ref, qr_ref, kr_ref, vr_ref, br_ref, of_ref, or_ref, s_scr):
    @pl.when(pl.program_id(1) == 0)
    def _():
        s_scr[...] = jnp.zeros_like(s_scr)

    L = CHUNK
    nchunk = TM // L
    npair = GLA_HEADS // 2
    lane = lax.broadcasted_iota(jnp.int32, (L, LANES), 1)
    rowv = lax.broadcasted_iota(jnp.int32, (L, LANES), 0)
    srow = lax.broadcasted_iota(jnp.int32, (LANES, 2 * GLA_DV), 0)
    scol = lax.broadcasted_iota(jnp.int32, (LANES, 2 * GLA_DV), 1)
    own_block = (srow < GLA_DK) == (scol < GLA_DV)
    vcol = lax.broadcasted_iota(jnp.int32, (L, 2 * GLA_DV), 1)
    e_row = lax.broadcasted_iota(jnp.int32, (LANES, LANES), 0)
    e_col = lax.broadcasted_iota(jnp.int32, (LANES, LANES), 1)
    head_ones = jnp.where((e_row < GLA_DK) == (e_col < GLA_DK), 1.0, 0.0).astype(BF16)
    leaf_idx = lax.broadcasted_iota(jnp.int32, (GLA_LEAF * GLA_LEAF, LANES), 0)
    leaf_lane = lax.broadcasted_iota(jnp.int32, (GLA_LEAF * GLA_LEAF, LANES), 1) % L
    leaf_s, leaf_t = leaf_idx // GLA_LEAF, leaf_idx % GLA_LEAF
    rep = lambda a: jnp.concatenate(
        [jnp.broadcast_to(a[s:s + 1], (GLA_LEAF, a.shape[1])) for s in range(GLA_LEAF)], axis=0)
    til = lambda a: jnp.concatenate([a] * GLA_LEAF, axis=0)

    def chunk_step(ci, carry):
        cs = []
        for bi, d in [(bi, d) for bi in range(qf_ref.shape[0]) for d in range(2)]:
            q_ref, k_ref, v_ref, b_ref, o_ref = (
                r.at[bi] for r in ((qf_ref, kf_ref, vf_ref, bf_ref, of_ref), (qr_ref, kr_ref, vr_ref, br_ref, or_ref))[d])
            cd = ci if d == 0 else nchunk - 1 - ci
            base = pl.multiple_of(cd * L, L)
            for p in range(npair):
                ls = slice(p * LANES, (p + 1) * LANES)
                bc = b_ref[pl.ds(base, L), ls]
                cs.append(dict(
                    d=d, p=p, bi=bi, base=base, o_ref=o_ref, bc=bc,
                    tau_v=rowv if d == 0 else L - 1 - rowv,
                    tau_s=(lane % L) if d == 0 else L - 1 - (lane % L),
                    q=q_ref[pl.ds(base, L), ls].astype(F32),
                    k=k_ref[pl.ds(base, L), ls].astype(F32),
                    v=v_ref[pl.ds(base, L), p * 2 * GLA_DV:(p + 1) * 2 * GLA_DV],
                    b_last=bc[L - 1:L] if d == 0 else bc[0:1],
                    state=s_scr[bi, d, p]))

        for c in cs:
            c["o_pair"] = _dot((c["q"] * jnp.exp2(c["bc"])).astype(BF16), c["state"].astype(BF16))

        for c in cs:
            c["att"] = jnp.zeros((L, LANES), F32)
        bs = L // 2
        while bs >= GLA_LEAF:
            for c in cs:
                d, bc, tau_v = c["d"], c["bc"], c["tau_v"]
                later = (tau_v // bs) % 2 == 1
                ref_b = None
                for blk in range(L // (2 * bs)):
                    tau_ref = blk * 2 * bs + bs - 1
                    idx = tau_ref if d == 0 else L - 1 - tau_ref
                    r = jnp.broadcast_to(bc[idx:idx + 1], (L, LANES))
                    ref_b = r if ref_b is None else jnp.where(tau_v // (2 * bs) == blk, r, ref_b)
                c["qs"] = jnp.where(later, c["q"] * jnp.exp2(jnp.where(later, bc - ref_b, 0.0)), 0.0).astype(BF16)
                ks = jnp.where(later, 0.0, c["k"] * jnp.exp2(jnp.where(later, 0.0, ref_b - bc)))
                c["ks2"] = jnp.concatenate([jnp.where(lane < GLA_DK, ks, 0.0), jnp.where(lane < GLA_DK, 0.0, ks)],
                                           axis=0).astype(BF16)
            for c in cs:
                same_parent = (c["tau_v"] // (2 * bs)) == (c["tau_s"] // (2 * bs))
                c["att"] = c["att"] + jnp.where(same_parent, _dot_nt(c["qs"], c["ks2"]), 0.0)
            bs //= 2

        for c in cs:
            c["leaf"] = []
        for blk in range(L // GLA_LEAF):
            rs = slice(blk * GLA_LEAF, (blk + 1) * GLA_LEAF)
            for c in cs:
                qb, bb, kb = c["q"][rs], c["bc"][rs], c["k"][rs]
                ok = (leaf_t >= leaf_s) if c["d"] == 0 else (leaf_t <= leaf_s)
                w = jnp.where(ok, til(qb) * jnp.exp2(til(bb) - rep(bb)) * rep(kb), 0.0)
                c["a"] = _dot(w.astype(BF16), head_ones)
            for c in cs:
                a = jnp.where(leaf_lane == leaf_s + blk * GLA_LEAF, c["a"], 0.0)
                acc = a[0:GLA_LEAF]
                for s in range(1, GLA_LEAF):
                    acc = acc + a[s * GLA_LEAF:(s + 1) * GLA_LEAF]
                c["leaf"].append(acc)

        for c in cs:
            v = c["v"]
            c["att"] = (c["att"] + jnp.concatenate(c["leaf"], axis=0)).astype(BF16)
            c["v_blk"] = jnp.concatenate([jnp.where(vcol < GLA_DV, v, jnp.zeros_like(v)),
                                          jnp.where(vcol < GLA_DV, jnp.zeros_like(v), v)], axis=0)
            c["ke"] = (c["k"] * jnp.exp2(c["b_last"] - c["bc"])).astype(BF16)
            c["decay_col"] = jnp.sum(
                jnp.where(e_row == e_col, jnp.broadcast_to(jnp.exp2(c["b_last"]), (LANES, LANES)), 0.0),
                axis=1, keepdims=True)
        for c in cs:
            c["o"] = c["o_pair"] + _dot(c["att"], c["v_blk"])
            c["ds"] = _dot_tn(c["ke"], c["v"])
        for c in cs:
            p = c["p"]
            c["o_ref"][pl.ds(c["base"], L), p * 2 * GLA_DV:(p + 1) * 2 * GLA_DV] = c["o"].astype(BF16)
            s_scr[c["bi"], c["d"], p] = c["decay_col"] * c["state"] + jnp.where(own_block, c["ds"], 0.0)
        return carry

    lax.fori_loop(0, nchunk, chunk_step, 0)


def _gla(gq, gk, gv, bc, nct):
    bsz, ttot, nk = gq.shape
    nc = gv.shape[2]
    nt = ttot // TM
    sb = math.gcd(bsz, GLA_BATCH)
    fwd = lambda w: pl.BlockSpec((sb, TM, w), lambda b, j: (b, j, 0))
    rev = lambda w: pl.BlockSpec((sb, TM, w), lambda b, j: (b, _scan_block(j, nct, nt), 0))
    bfwd = pl.BlockSpec((sb, TM, nk), lambda b, j: (b, j, 0))
    brev = pl.BlockSpec((sb, TM, nk), lambda b, j: (b, _scan_block(j, nct, nt), 1))
    return pl.pallas_call(
        _gla_kernel,
        out_shape=(jax.ShapeDtypeStruct((bsz, ttot, nc), BF16),) * 2,
        grid=(bsz // sb, nt),
        in_specs=[fwd(nk), fwd(nk), fwd(nc), bfwd, rev(nk), rev(nk), rev(nc), brev],
        out_specs=(fwd(nc), rev(nc)),
        scratch_shapes=[pltpu.VMEM((sb, 2, GLA_HEADS // 2, LANES, 2 * GLA_DV), F32)],
        compiler_params=_params("parallel", "arbitrary"),
        name="gla_scan",
    )(gq, gk, gv, bc, gq, gk, gv, bc)


def _na_bias_tables(rpb, rows):
    kh = NA_KH
    c = np.arange(GRID_W)
    cs = np.clip(c - NA_KW // 2, 0, GRID_W - NA_KW)
    col_ok = (c[None, :] >= cs[:, None]) & (c[None, :] < cs[:, None] + NA_KW)
    pad = GRID_W - NA_KW
    rpb_pad = jnp.pad(rpb.astype(F32) * LOG2E, ((0, 0), (0, 0), (pad, pad)))
    period = 2 * GRID_W
    flat = jnp.tile(jnp.pad(rpb_pad, ((0, 0), (0, 0), (0, 1))), (1, 1, GRID_W))[:, :, :GRID_W * (period - 1)]
    t1 = flat.reshape(NA_HEADS, 2 * NA_KH - 1, GRID_W, period - 1)[..., GRID_W - 1:]
    t1 = jnp.where(col_ok, t1, NEG_BIG)
    masked = jnp.full((NA_HEADS, GRID_W, GRID_W), NEG_BIG, F32)
    per_row = []
    for r0 in (0, NA_QROWS, rows - NA_QROWS):
        kb = int(np.clip(r0 - kh // 2, 0, rows - NA_KROWS))
        for qrow in range(r0, r0 + NA_QROWS):
            ws = int(np.clip(qrow - kh // 2, 0, rows - kh))
            per_row.append(jnp.concatenate(
                [t1[:, krow - qrow + NA_KH - 1] if ws <= krow < ws + kh else masked
                 for krow in range(kb, kb + NA_KROWS)], axis=-1))
    return jnp.concatenate(per_row, axis=1)


def _na_heads(q_ref, kw_ref, vw_ref, kc_ref, vc_ref, bias_ref):
    nq = NA_QROWS * GRID_W
    lane = lax.broadcasted_iota(jnp.int32, (nq, LANES), 1)
    def scores(j):
        ls = slice((j // 2) * LANES, (j // 2 + 1) * LANES)
        q = q_ref[:, ls]
        qj = jnp.where((lane < NA_DH) == (j % 2 == 0), q, jnp.zeros_like(q))
        return _dot_nt(qj, kw_ref[:, ls]) + bias_ref[j], _dot_nt(qj, kc_ref[:, ls])

    outs = []
    nxt = scores(0)
    for j in range(NA_HEADS):
        s_loc, s_ctx = nxt
        if j + 1 < NA_HEADS:
            nxt = scores(j + 1)
        ls = slice((j // 2) * LANES, (j // 2 + 1) * LANES)
        m = jnp.maximum(jnp.max(s_loc, axis=-1, keepdims=True), jnp.max(s_ctx, axis=-1, keepdims=True))
        p_loc = jnp.exp2(s_loc - m)
        p_ctx = jnp.exp2(s_ctx - m)
        den = jnp.sum(p_loc, axis=-1, keepdims=True) + jnp.sum(p_ctx, axis=-1, keepdims=True)
        outs.append((_dot(p_loc.astype(BF16), vw_ref[:, ls]) + _dot(p_ctx.astype(BF16), vc_ref[:, ls])) / den)
    return jnp.concatenate([jnp.where(lane < NA_DH, outs[2 * i], outs[2 * i + 1]) for i in range(NA_HEADS // 2)], axis=1)


def _na_out1_kernel(*refs, nb):
    kv = refs[:4 * nb]
    q_ref, bias_ref, h_ref, gate_ref, of_ref, or_ref, z_ref, gn_ref, w_ref, fn_ref, o_ref = refs[4 * nb:]
    rows = lambda f: jnp.concatenate([f(bi) for bi in range(nb)], axis=0)
    f32 = lambda r, bi: r[bi].astype(F32)
    na = rows(lambda bi: _na_heads(q_ref.at[bi], *kv[4 * bi:4 * bi + 4], bias_ref))
    g = rows(lambda bi: _head_rms(f32(of_ref, bi) + f32(or_ref, bi), GLA_HEADS) * gn_ref[...])
    cat = (jnp.concatenate([g, na], axis=1) * rows(lambda bi: f32(z_ref, bi))).astype(BF16)
    y = _dot(cat, w_ref[...])
    for bi in range(nb):
        h = h_ref[bi] + gate_ref[bi] * y[bi * TM:(bi + 1) * TM]
        o_ref[bi] = _rms(h) * fn_ref[...]


def _na_out1(nq, nk, nv, bias, h, gate, of, orv, z, gla_norm, w_out, final_norm, n_ctx, rows):
    bsz, ttot, nd = nq.shape
    nrb = rows // NA_QROWS
    qrows = NA_QROWS * GRID_W
    krows = NA_KROWS * GRID_W
    assert qrows == TM
    nct = n_ctx // TM
    nc = GLA_HEADS * GLA_DV
    sb = math.gcd(bsz, TILE_BATCH)

    def variant(rb):
        return jnp.where(rb == 0, 0, jnp.where(rb == nrb - 1, 2, 1))

    def key_start(rb):
        first_row = jnp.clip(rb * NA_QROWS - NA_KH // 2, 0, rows - NA_KROWS)
        return pl.multiple_of(n_ctx + first_row * GRID_W, GRID_W)

    window = lambda i: pl.BlockSpec((None, pl.Element(krows), pl.Element(nd)), lambda b, rb: (b * sb + i, key_start(rb), 0))
    context = lambda i: pl.BlockSpec((None, pl.Element(n_ctx), pl.Element(nd)), lambda b, rb: (b * sb + i, 0, 0))
    kv_specs = [spec(i) for i in range(sb) for spec in (window, window, context, context)]
    lat = lambda w: pl.BlockSpec((sb, TM, w), lambda b, rb: (b, rb + nct, 0))
    return pl.pallas_call(
        functools.partial(_na_out1_kernel, nb=sb),
        out_shape=jax.ShapeDtypeStruct((bsz, rows * GRID_W, D_MODEL), F32),
        grid=(bsz // sb, nrb),
        in_specs=kv_specs + [
            lat(nd), pl.BlockSpec((NA_HEADS, qrows, krows), lambda b, rb: (0, variant(rb), 0)),
            lat(D_MODEL), pl.BlockSpec((sb, None, 1, D_MODEL), lambda b, rb: (b, 1, 0, 0)),
            lat(nc), lat(nc), lat(nc + nd), _full((1, nc)), _full(w_out.shape), _full((1, D_MODEL))],
        out_specs=pl.BlockSpec((sb, TM, D_MODEL), lambda b, rb: (b, rb, 0)),
        compiler_params=_params("parallel", "parallel"),
        name="na_out1",
    )(*([nk, nv, nk, nv] * sb), nq, bias, h, gate, of, orv, z, gla_norm[None], w_out.astype(BF16), final_norm[None])


def kernel(x, c, ctx, c_ctx, l0_norm, l0_w_mod, l0_b_mod, l0_w_in, l0_mla_q_norm, l0_mla_w_uq, l0_mla_kv_norm, l0_mla_w_ukv, l0_mlstm_conv_w, l0_mlstm_conv_b, l0_mlstm_b_i, l0_mlstm_b_f, l0_mlstm_norm, l0_w_out, l1_norm, l1_w_mod, l1_b_mod, l1_w_in, l1_gla_w_gate, l1_gla_b_gate, l1_gla_norm, l1_na_rpb, l1_w_out, final_norm):
    bsz, seq, d = x.shape
    n_ctx = ctx.shape[1]
    rows = seq // GRID_W
    assert d == D_MODEL and seq % TM == 0 and n_ctx == TM
    assert rows % NA_QROWS == 0 and rows >= NA_KROWS and rows // NA_QROWS >= 3
    nct = n_ctx // TM

    (shift0, scale0, gate0), (shift, scale, gate) = _mod_vectors(
        c, c_ctx, ((l0_w_mod, l0_b_mod), (l1_w_mod, l1_b_mod)))

    cos_t, sin_t = _rope_tables(n_ctx, seq)
    w0 = _proj0_weights(l0_w_in, l0_mla_q_norm, l0_mla_w_uq, l0_mla_kv_norm, l0_mla_w_ukv,
                        l0_mlstm_conv_w, l0_mlstm_conv_b, l0_mlstm_b_i, l0_mlstm_b_f)
    q, k, v, mq, mk, mv, g_rows, g_cols, mo, z0 = _proj0(ctx, x, scale0, shift0, l0_norm, cos_t, sin_t, w0, nct)
    a = _mla_attention(q, k, v, nct)
    hf, hr = _mlstm(mq, mk, mv, g_rows, g_cols, nct)

    w1 = _proj1_weights(l1_w_in, l1_gla_w_gate, l1_gla_b_gate)
    h, gq, gk, gv, bc, nq, nk, nv, z = _mid(ctx, x, gate0, a, hf, hr, mo, z0, l0_mlstm_norm, l0_w_out,
                                            scale, shift, l1_norm, w1, nct)
    of, orv = _gla(gq, gk, gv, bc, nct)
    return _na_out1(nq, nk, nv, _na_bias_tables(l1_na_rpb, rows), h, gate, of, orv, z,
                    l1_gla_norm, l1_w_out, final_norm, n_ctx, rows)
```

```python
import functools
import math

import jax
import jax.numpy as jnp
import numpy as np
from jax import lax
from jax.experimental import pallas as pl
from jax.experimental.pallas import tpu as pltpu

F32 = jnp.float32
BF16 = jnp.bfloat16

D_MODEL = 1024
GRID_W = 64
EPS = 1e-6
ROPE_BASE = 10000.0

MLA_HEADS = 8
MLA_Q_RANK = 384
MLA_KV_RANK = 256
MLA_NOPE = 64
MLA_ROPE = 32
MLA_V = 64
ROPE_PAIRS = MLA_ROPE // 4
MLSTM_HEADS = 4
MLSTM_DH = 128
MLSTM_CONV = 3
GLA_HEADS = 4
GLA_DK = 64
GLA_DV = 128
GLA_GATE_RANK = 16
GLA_TAU = 16.0
NA_HEADS = 8
NA_DH = 64
NA_KH = 8
NA_KW = 16

CHUNK = 64
MLSTM_CHUNK = 128
TM = 256
LANES = 128
SUBLANES = 8
HALO = SUBLANES
NA_QROWS = 4
NA_KROWS = NA_QROWS + NA_KH - 1
GLA_LEAF = 8
MLA_HPS = 4
MLA_QTILES = 2
MLA_KBLOCKS = 2
TILE_BATCH = 2
MLSTM_BATCH = 2
GLA_BATCH = 8
VMEM_LIMIT = 56 * 1024 * 1024
NEG_BIG = -1e30
LOG2E = 1.4426950408889634


def _dot(a, b):
    return jnp.dot(a, b, preferred_element_type=F32)


def _dot_nt(a, b):
    return lax.dot_general(a, b, (((1,), (1,)), ((), ())), preferred_element_type=F32)


def _dot_tn(a, b):
    return lax.dot_general(a, b, (((0,), (0,)), ((), ())), preferred_element_type=F32)


def _rms(x):
    return x * lax.rsqrt(jnp.mean(x * x, axis=-1, keepdims=True) + EPS)


def _sigmoid(x):
    return 1.0 / (1.0 + jnp.exp(-x))


def _silu(x):
    return x * _sigmoid(x)


def _log_sigmoid(x):
    return jnp.minimum(x, 0.0) - jnp.log(1.0 + jnp.exp(-jnp.abs(x)))


def _params(*sem):
    return pltpu.CompilerParams(dimension_semantics=sem, vmem_limit_bytes=VMEM_LIMIT)


def _full(shape):
    nd = len(shape)
    return pl.BlockSpec(shape, lambda *_: (0,) * nd)


def _mod_kernel(c_ref, w0_ref, b0_ref, w1_ref, b1_ref, o0_ref, o1_ref):
    sc = _silu(c_ref[...]).astype(BF16)
    o0_ref[...] = _dot(sc, w0_ref[...].astype(BF16)) + b0_ref[...]
    o1_ref[...] = _dot(sc, w1_ref[...].astype(BF16)) + b1_ref[...]


def _mod_vectors(c, c_ctx, mods):
    bsz, d = c.shape
    rows = -(-(bsz + 1) // SUBLANES) * SUBLANES
    cc = jnp.zeros((rows, d), F32).at[:bsz].set(c).at[bsz].set(c_ctx)
    (w0, b0), (w1, b1) = mods
    n = w0.shape[1]
    wspec = pl.BlockSpec((d, d), lambda j: (0, j))
    vspec = pl.BlockSpec((1, d), lambda j: (0, j))
    ospec = pl.BlockSpec((rows, d), lambda j: (0, j))
    per_layer = pl.pallas_call(
        _mod_kernel,
        out_shape=(jax.ShapeDtypeStruct((rows, n), F32),) * 2,
        grid=(n // d,),
        in_specs=[_full((rows, d)), wspec, vspec, wspec, vspec],
        out_specs=(ospec, ospec),
        compiler_params=_params("parallel"),
        name="modulation",
    )(cc, w0, b0[None], w1, b1[None])
    out = []
    for mod in per_layer:
        vecs = []
        for part in jnp.split(mod, 3, axis=-1):
            ctx_v = jnp.broadcast_to(part[bsz][None], (bsz, d))
            vecs.append(jnp.stack([ctx_v, part[:bsz]], axis=1)[:, :, None, :])
        out.append(vecs)
    return out


def _modnorm(x, g, scale, shift):
    return _rms(x) * g * (1.0 + scale) + shift


def _seg_scan(x, reverse, op, fill):
    width = x.shape[1]
    pos = lax.broadcasted_iota(jnp.int32, x.shape, 1) % MLSTM_CHUNK
    k = 1
    while k < MLSTM_CHUNK:
        if reverse:
            x = op(x, jnp.where(pos < MLSTM_CHUNK - k, pltpu.roll(x, width - k, axis=1), fill))
        else:
            x = op(x, jnp.where(pos >= k, pltpu.roll(x, k, axis=1), fill))
        k *= 2
    return x


def _split3(x):
    hi = x.astype(BF16)
    r1 = x - hi.astype(F32)
    mid = r1.astype(BF16)
    return hi, mid, (r1 - mid.astype(F32)).astype(BF16)


def _proj0_kernel(ctx_ref, lat_ref, hp_ref, hn_ref, sc_ref, sh_ref, g_ref, cos_ref, sin_ref,
                  wall_ref, wgt_ref, qn_ref, wuq_ref, kvn_ref, wk_ref, wv_ref, cw_ref, cb_ref, gb_ref,
                  q_out, k_out, v_out, mq_out, mk_out, mv_out, gr_out, gc_out, mo_out, z_out,
                  pbuf, *, nct, nt):
    t = pl.program_id(1)
    h = jnp.where(t < nct, ctx_ref[...], lat_ref[...])
    x = jnp.concatenate([hp_ref[...], h, hn_ref[...]], axis=0)
    ub_ext = _modnorm(x, g_ref[...], sc_ref[...], sh_ref[...]).astype(BF16)
    ub = ub_ext[HALO:HALO + TM]

    gt = _dot_nt(wgt_ref[...], ub) + gb_ref[...]
    half = MLSTM_HEADS
    per_dir = []
    for d in range(2):
        gd = gt[d * SUBLANES:(d + 1) * SUBLANES]
        bcum = _seg_scan(_log_sigmoid(gd), d == 1, jnp.add, 0.0)
        b_top = pltpu.roll(bcum, half, axis=0)
        c8 = gd - b_top
        pm8 = _seg_scan(c8, d == 1, jnp.maximum, -jnp.inf)
        per_dir.append((c8, pm8, b_top))

    cos = cos_ref[...]
    sin = sin_ref[...]
    cos_all = jnp.concatenate([cos] * MLA_HEADS, axis=1)
    sin_all = jnp.concatenate([sin] * MLA_HEADS, axis=1)

    def swap_halves(a):
        lane = lax.broadcasted_iota(jnp.int32, a.shape, 1)
        first = lane % (2 * ROPE_PAIRS) < ROPE_PAIRS
        return jnp.where(first, pltpu.roll(a, a.shape[1] - ROPE_PAIRS, axis=1), pltpu.roll(a, ROPE_PAIRS, axis=1))

    wseg = lambda name: wall_ref[:, PROJ0_COLS[name][0]:PROJ0_COLS[name][1]]
    cqkr = _dot(ub, wseg("cqkr"))
    cq = cqkr[:, :MLA_Q_RANK]
    kr = cqkr[:, MLA_Q_RANK:]
    ckv = _dot(ub, wseg("ckv"))
    pqk = _dot(ub_ext, wseg("mqk"))

    nq = (_rms(cq) * qn_ref[...]).astype(BF16)
    nkv = (_rms(ckv) * kvn_ref[...]).astype(BF16)

    q = _dot(nq, wuq_ref[...])
    kn = _dot(nkv, wk_ref[...])
    vv = _dot(nkv, wv_ref[...])

    prev_ok = t > nct
    next_ok = jnp.logical_and(t >= nct, t != nt - 1)
    row = lax.broadcasted_iota(jnp.int32, pqk.shape, 0)
    keep = jnp.logical_and(jnp.logical_or(row >= HALO, prev_ok), jnp.logical_or(row < HALO + TM, next_ok))
    pbuf[...] = jnp.where(keep, pqk, 0.0)
    cw = cw_ref[...]
    y = (pbuf[HALO - 1:HALO - 1 + TM, :] * cw[0:1] + pbuf[HALO:HALO + TM, :] * cw[1:2]
         + pbuf[HALO + 1:HALO + 1 + TM, :] * cw[2:3] + cb_ref[...])
    y = _silu(y)
    nb = MLSTM_HEADS * MLSTM_DH
    mq_out[...] = (y[:, :nb] * MLSTM_DH ** -0.5).astype(BF16)
    mk_out[...] = y[:, nb:].astype(BF16)

    mv = _dot(ub, wseg("mv"))
    mo = _dot(ub, wseg("mo"))

    q = q * cos_all + swap_halves(q) * sin_all
    q_out[...] = (q * ((MLA_NOPE + MLA_ROPE) ** -0.5 * LOG2E)).astype(BF16)
    rot = kr * cos + swap_halves(kr) * sin
    k_out[...] = (kn + jnp.concatenate([rot] * MLA_HEADS, axis=1)).astype(BF16)
    vlane = lax.broadcasted_iota(jnp.int32, (TM, MLA_HEADS * LANES), 1)
    is_value = ((vlane % LANES) < MLA_V) == ((vlane // LANES) % 2 == 0)
    v_out[...] = jnp.where(is_value, vv, 1.0).astype(BF16)

    zz = _dot(ub, wseg("z"))
    mv_out[...] = mv.astype(BF16)
    mo_out[...] = _sigmoid(mo).astype(BF16)
    z_out[...] = _silu(zz).astype(BF16)

    r8 =lax.broadcasted_iota(jnp.int32, (SUBLANES, TM), 0)
    both = lambda i: jnp.where(r8 < half, per_dir[0][i], pltpu.roll(per_dir[1][i], half, axis=0)) * LOG2E
    c_rows, pm_rows, b_rows = both(0), both(1), both(2)
    for ci in range(TM // MLSTM_CHUNK):
        gr_out[ci] = c_rows[:, ci * MLSTM_CHUNK:(ci + 1) * MLSTM_CHUNK]
    stack = jnp.concatenate([c_rows, pm_rows, b_rows, jnp.zeros((LANES - 3 * SUBLANES, TM), F32)], axis=0)
    er = lax.broadcasted_iota(jnp.int32, (TM, TM), 0)
    ec = lax.broadcasted_iota(jnp.int32, (TM, TM), 1)
    eye = jnp.where(er == ec, 1.0, 0.0).astype(BF16)
    gc_out[...] = sum(_dot_nt(eye, part) for part in _split3(stack))


def _rope_tables(n_ctx, seq):
    t = jnp.arange(seq)
    inv = 1.0 / (ROPE_BASE ** (jnp.arange(ROPE_PAIRS, dtype=F32) / ROPE_PAIRS))
    ang = jnp.concatenate([(t // GRID_W)[:, None] * inv, (t % GRID_W)[:, None] * inv], axis=-1)
    cos, sin = jnp.cos(ang), jnp.sin(ang)
    j = np.arange(MLA_ROPE)
    src = (j // (2 * ROPE_PAIRS)) * ROPE_PAIRS + (j % ROPE_PAIRS)
    sign = np.where((j % (2 * ROPE_PAIRS)) < ROPE_PAIRS, -1.0, 1.0).astype(np.float32)
    cos_full = cos[:, src]
    sin_full = sin[:, src] * sign
    cos_t = jnp.ones((n_ctx + seq, LANES), F32).at[n_ctx:, MLA_NOPE:MLA_NOPE + MLA_ROPE].set(cos_full)
    sin_t = jnp.zeros((n_ctx + seq, LANES), F32).at[n_ctx:, MLA_NOPE:MLA_NOPE + MLA_ROPE].set(sin_full)
    return cos_t, sin_t


def _segments(widths):
    out, start = {}, 0
    for name, width in widths:
        out[name] = (start, start + width)
        start += width
    return out


PROJ0_COLS = _segments([("cqkr", MLA_Q_RANK + LANES), ("ckv", MLA_KV_RANK), ("mqk", 2 * MLSTM_HEADS * MLSTM_DH),
                        ("mv", MLSTM_HEADS * MLSTM_DH), ("mo", MLSTM_HEADS * MLSTM_DH), ("z", D_MODEL)])
PROJ1_COLS = _segments([("q", GLA_HEADS * GLA_DK), ("k", GLA_HEADS * GLA_DK), ("v", GLA_HEADS * GLA_DV),
                        ("ga", LANES), ("nq", NA_HEADS * NA_DH), ("nk", NA_HEADS * NA_DH), ("nv", NA_HEADS * NA_DH),
                        ("z", GLA_HEADS * GLA_DV + NA_HEADS * NA_DH)])


def _proj0_weights(w_in, q_norm, w_uq, kv_norm, w_ukv, conv_w, conv_b, b_i, b_f):
    o = np.cumsum([0, MLA_Q_RANK, MLA_KV_RANK, MLA_ROPE] + [MLSTM_HEADS * MLSTM_DH] * 4 + [4 * MLSTM_HEADS, D_MODEL])
    dq = MLA_NOPE + MLA_ROPE
    wkr = jnp.pad(w_in[:, o[2]:o[3]], ((0, 0), (MLA_NOPE, LANES - dq)))
    wuq = jnp.pad(w_uq.reshape(MLA_Q_RANK, MLA_HEADS, dq), ((0, 0), (0, 0), (0, LANES - dq)))
    wkv = w_ukv.reshape(MLA_KV_RANK, MLA_HEADS, MLA_NOPE + MLA_V)
    wk = jnp.pad(wkv[:, :, :MLA_NOPE], ((0, 0), (0, 0), (0, LANES - MLA_NOPE)))
    zv = jnp.zeros((MLA_KV_RANK, MLA_HEADS // 2, MLA_V), F32)
    wv = jnp.concatenate([wkv[:, 0::2, MLA_NOPE:], zv, zv, wkv[:, 1::2, MLA_NOPE:]], axis=-1)
    wv = wv.reshape(MLA_KV_RANK, MLA_HEADS * LANES)
    gbias = jnp.stack([b_i, b_f], axis=1).reshape(4 * MLSTM_HEADS, 1)
    bf = lambda a: a.astype(BF16)
    wall = jnp.concatenate([w_in[:, o[0]:o[1]], wkr, w_in[:, o[1]:o[2]], w_in[:, o[3]:o[7]], w_in[:, o[8]:o[9]]], axis=1)
    assert wall.shape[1] == PROJ0_COLS["z"][1]
    return dict(
        wall=bf(wall), wgt=bf(w_in[:, o[7]:o[8]].T),
        qn=q_norm[None], wuq=bf(wuq.reshape(MLA_Q_RANK, -1)),
        kvn=kv_norm[None], wk=bf(wk.reshape(MLA_KV_RANK, -1)), wv=bf(wv),
        cw=jnp.zeros((SUBLANES, conv_w.shape[1]), F32).at[:MLSTM_CONV].set(conv_w), cb=conv_b[None], gb=gbias)


def _tile_specs(nct, nt, nb):
    tile = lambda w: pl.BlockSpec((nb, TM, w), lambda b, t: (b, t, 0))
    per = TM // HALO
    nlat = nt - nct
    src_ctx = pl.BlockSpec((nb, TM, D_MODEL), lambda b, t: (b, jnp.minimum(t, nct - 1), 0))
    src_lat = pl.BlockSpec((nb, TM, D_MODEL), lambda b, t: (b, jnp.maximum(t - nct, 0), 0))
    prev = pl.BlockSpec((nb, HALO, D_MODEL), lambda b, t: (b, jnp.maximum((t - nct) * per - 1, 0), 0))
    nxt = pl.BlockSpec((nb, HALO, D_MODEL),
                       lambda b, t: (b, jnp.clip((t - nct + 1) * per, 0, nlat * per - 1), 0))
    mod = pl.BlockSpec((nb, None, 1, D_MODEL), lambda b, t: (b, (t >= nct).astype(jnp.int32), 0, 0))
    return tile, src_ctx, src_lat, prev, nxt, mod


def _per_element(tile_kernel, shared):
    def kernel(*refs, **kw):
        for bi in range(refs[0].shape[0]):
            tile_kernel(*[r if i in shared else r.at[bi] for i, r in enumerate(refs)], **kw)
    return kernel


def _proj0(ctx, x, scale, shift, norm_g, cos_t, sin_t, w, nct):
    bsz = x.shape[0]
    ttot = ctx.shape[1] + x.shape[1]
    nt = ttot // TM
    nb = MLSTM_HEADS * MLSTM_DH
    sb = math.gcd(bsz, TILE_BATCH)
    tile, src_ctx, src_lat, prev, nxt, mod = _tile_specs(nct, nt, sb)
    tab = pl.BlockSpec((TM, LANES), lambda b, t: (t, 0))
    wnames = ["wall", "wgt", "qn", "wuq", "kvn", "wk", "wv", "cw", "cb", "gb"]
    ws = [w[n] for n in wnames]
    tok = lambda width, dt: jax.ShapeDtypeStruct((bsz, ttot, width), dt)
    out_shape = (tok(MLA_HEADS * LANES, BF16), tok(MLA_HEADS * LANES, BF16), tok(MLA_HEADS * LANES, BF16),
                 tok(nb, BF16), tok(nb, BF16), tok(nb, BF16),
                 jax.ShapeDtypeStruct((bsz, ttot // MLSTM_CHUNK, 2 * MLSTM_HEADS, MLSTM_CHUNK), F32), tok(LANES, F32),
                 tok(nb, BF16), tok(D_MODEL, BF16))
    out_specs = (tile(MLA_HEADS * LANES), tile(MLA_HEADS * LANES), tile(MLA_HEADS * LANES),
                 tile(nb), tile(nb), tile(nb),
                 pl.BlockSpec((sb, TM // MLSTM_CHUNK, 2 * MLSTM_HEADS, MLSTM_CHUNK), lambda b, t: (b, t, 0, 0)),
                 tile(LANES),
                 tile(nb), tile(D_MODEL))
    n_blocked, n_out = 6, len(out_shape)
    n_in = n_blocked + 3 + len(ws)
    shared = set(range(n_blocked, n_in)) | {n_in + n_out}
    return pl.pallas_call(
        functools.partial(_per_element(_proj0_kernel, shared), nct=nct, nt=nt),
        out_shape=out_shape,
        grid=(bsz // sb, nt),
        in_specs=[src_ctx, src_lat, prev, nxt, mod, mod, _full((1, D_MODEL)), tab, tab] + [_full(a.shape) for a in ws],
        out_specs=out_specs,
        scratch_shapes=[pltpu.VMEM((TM + 2 * HALO, 2 * nb), F32)],
        compiler_params=_params("parallel", "parallel"),
        name="proj0",
    )(ctx, x, x, x, scale, shift, norm_g[None], cos_t, sin_t, *ws)


def _mla_kernel(qa_ref, qb_ref, k_ref, v_ref, o_ref, *, nkeys_ctx):
    step = pl.program_id(2)

    def store(rows, outs):
        lane = lax.broadcasted_iota(jnp.int32, outs[0].shape, 1)
        for j in range(MLA_HPS // 2):
            o_ref[rows, j * LANES:(j + 1) * LANES] = jnp.where(lane < MLA_V, outs[2 * j], outs[2 * j + 1]).astype(BF16)

    def attend(queries, blocks):
        items = [(j, kb) for j in range(MLA_HPS) for kb in blocks]
        qk = lambda j, kb: _dot_nt(queries(j), k_ref[kb, j * LANES:(j + 1) * LANES])
        outs = []
        s_next = qk(*items[0])
        for i, (j, kb) in enumerate(items):
            s = s_next
            if i + 1 < len(items):
                s_next = qk(*items[i + 1])
            m = jnp.max(s, axis=-1, keepdims=True)
            p = jnp.exp2(s - m)
            o = _dot(p.astype(BF16), v_ref[kb, j * LANES:(j + 1) * LANES])
            if kb is not blocks[0]:
                m_all = jnp.maximum(m_acc, m)
                o = o_acc * jnp.exp2(m_acc - m_all) + o * jnp.exp2(m - m_all)
                m = m_all
            o_acc, m_acc = o, m
            if kb is blocks[-1]:
                outs.append(o / pltpu.roll(o, MLA_V, axis=1))
        return outs

    @pl.when(step == 0)
    def _():
        outs = attend(lambda j: qa_ref[:, j * LANES:(j + 1) * LANES], [slice(0, nkeys_ctx)])
        store(slice(0, TM), outs)
        o_ref[TM:, :] = jnp.zeros((o_ref.shape[0] - TM, o_ref.shape[1]), BF16)

    @pl.when(step > 0)
    def _():
        stacked = {}

        def queries(j):
            if j not in stacked:
                hs = slice(j * LANES, (j + 1) * LANES)
                stacked[j] = jnp.concatenate([qa_ref[:, hs], qb_ref[:, hs]], axis=0)
            return stacked[j]

        nblock = k_ref.shape[0] // MLA_KBLOCKS
        outs = attend(queries, [slice(b * nblock, (b + 1) * nblock) for b in range(MLA_KBLOCKS)])
        store(slice(None), outs)


def _mla_attention(q, k, v, nct):
    bsz, ttot, _ = q.shape
    nt = ttot // TM
    assert nct == 1 and MLA_QTILES == 2 and (nt - nct) % MLA_QTILES == 0
    assert ttot % (MLA_KBLOCKS * 2 * SUBLANES) == 0
    hw = MLA_HPS * LANES
    nsteps = 1 + (nt - nct) // MLA_QTILES
    return pl.pallas_call(
        functools.partial(_mla_kernel, nkeys_ctx=nct * TM),
        out_shape=jax.ShapeDtypeStruct((bsz, nsteps * MLA_QTILES * TM, MLA_HEADS * MLA_V), BF16),
        grid=(bsz, MLA_HEADS // MLA_HPS, nsteps),
        in_specs=[pl.BlockSpec((None, TM, hw), lambda b, p, s: (b, jnp.maximum(2 * s - 1, 0), p)),
                  pl.BlockSpec((None, TM, hw), lambda b, p, s: (b, jnp.maximum(2 * s, 1), p)),
                  pl.BlockSpec((None, ttot, hw), lambda b, p, s: (b, 0, p)),
                  pl.BlockSpec((None, ttot, hw), lambda b, p, s: (b, 0, p))],
        out_specs=pl.BlockSpec((None, MLA_QTILES * TM, hw // 2), lambda b, p, s: (b, s, p)),
        compiler_params=_params("parallel", "parallel", "parallel"),
        name="mla_attention",
    )(q, q, k, v)


def _mla_tile(t, nct):
    return t + (t >= nct).astype(jnp.int32)


def _scan_block(j, nct, nt):
    return jnp.where(j < nct, nct - 1 - j, nt - 1 - (j - nct))


def _mlstm_kernel(qf_ref, kf_ref, vf_ref, grf_ref, gcf_ref, qr_ref, kr_ref, vr_ref, grr_ref, gcr_ref,
                  hf_ref, hr_ref, c_scr, m_scr):
    @pl.when(pl.program_id(1) == 0)
    def _():
        c_scr[...] = jnp.zeros_like(c_scr)
        m_scr[...] = jnp.zeros_like(m_scr)

    L = MLSTM_CHUNK
    nchunk = TM // L
    nh = MLSTM_HEADS
    nchain = 2 * nh
    row = lax.broadcasted_iota(jnp.int32, (L, L), 0)
    col = lax.broadcasted_iota(jnp.int32, (L, L), 1)
    ones = jnp.ones((L, MLSTM_DH), BF16)
    wide = lambda a: jnp.broadcast_to(a, (L, MLSTM_DH))
    twice = lambda a: jnp.concatenate([a, a], axis=1)
    refs = ((qf_ref, kf_ref, vf_ref, grf_ref, gcf_ref, hf_ref), (qr_ref, kr_ref, vr_ref, grr_ref, gcr_ref, hr_ref))
    nbatch = qf_ref.shape[0]

    def load(ci):
        cs = []
        for bi, d in [(bi, d) for bi in range(nbatch) for d in range(2)]:
            q_ref, k_ref, v_ref, gr_ref, gc_ref, o_ref = (r.at[bi] for r in refs[d])
            cd = ci if d == 0 else nchunk - 1 - ci
            rs = slice(cd * L, (cd + 1) * L)
            g_rows = gr_ref[cd]
            g_cols = gc_ref[rs, :]
            last = L - 1 if d == 0 else 0
            for hd in range(nh):
                ch = d * nh + hd
                hs = slice(hd * MLSTM_DH, (hd + 1) * MLSTM_DH)
                cs.append(dict(
                    rs=rs, hs=hs, o_ref=o_ref, causal=(col <= row) if d == 0 else (col >= row),
                    c_row=g_rows[ch:ch + 1], c_col=g_cols[:, ch:ch + 1],
                    pm_col=g_cols[:, nchain + ch:nchain + ch + 1], b_col=g_cols[:, 2 * nchain + ch:2 * nchain + ch + 1],
                    pm_last=g_cols[last:last + 1, nchain + ch:nchain + ch + 1],
                    b_last=g_cols[last:last + 1, 2 * nchain + ch:2 * nchain + ch + 1],
                    q=q_ref[rs, hs], k=k_ref[rs, hs],
                    v_ext=jnp.concatenate([v_ref[rs, hs], ones], axis=1)))
        return cs

    def prep_scores(cs):
        for c in cs:
            c["s"] = _dot_nt(c["q"], c["k"])

    def prep_weights(cs):
        for c in cs:
            c["pm_w"] = wide(c["pm_col"])
            c["b_w"] = wide(c["b_col"])
            c["w"] = jnp.exp2(jnp.where(c["causal"], c["c_row"] - c["pm_w"][:, :L], -jnp.inf))
            c["kwf"] = jnp.exp2(wide(c["c_col"] - c["pm_last"]))

    def prep_operands(cs):
        for c in cs:
            c["p"] = (c["s"] * c["w"]).astype(BF16)
            c["kw"] = (c["kwf"] * c["k"].astype(F32)).astype(BF16)

    def prep_products(cs):
        for c in cs:
            c["o1"] = _dot(c["p"], c["v_ext"])
            c["dc"] = _dot_tn(c["kw"], c["v_ext"])

    def scan_read(cs):
        for ch, c in enumerate(cs):
            c["qc"] = _dot(c["q"], state[ch].astype(BF16))

    def scan_factors(cs):
        for ch, c in enumerate(cs):
            g_w = jnp.maximum(m[ch], c["pm_w"])
            c["e_intra"] = jnp.exp2(c["pm_w"] - g_w)
            c["e_inter"] = jnp.exp2(m[ch] - g_w)
            c["floor"] = jnp.exp2(-(c["b_w"] + g_w))
            g_last = jnp.maximum(m[ch], c["pm_last"])
            c["keep"] = jnp.exp2(m[ch] - g_last)
            c["gain"] = jnp.exp2(c["pm_last"] - g_last)
            c["m_new"] = c["b_last"] + g_last

    def scan_update(cs):
        for ch, c in enumerate(cs):
            o = twice(c["e_intra"]) * c["o1"] + twice(c["e_inter"]) * c["qc"]
            h = o[:, :MLSTM_DH] / jnp.maximum(jnp.abs(o[:, MLSTM_DH:]), c["floor"])
            c["o_ref"][c["rs"], c["hs"]] = h.astype(BF16)
            state[ch] = c["keep"] * state[ch] + c["gain"] * c["dc"]
            m[ch] = c["m_new"]

    nstate = nbatch * nchain
    state = [c_scr[ch] for ch in range(nstate)]
    m = [m_scr[ch, 0:1, 0:1] for ch in range(nstate)]
    chunks = [load(ci) for ci in range(nchunk)]
    everything = [c for cs in chunks for c in cs]
    for stage in (prep_scores, prep_weights, prep_operands):
        stage(everything)
    prep_products(chunks[0])
    for i, cs in enumerate(chunks):
        scan_read(cs)
        if i + 1 < nchunk:
            prep_products(chunks[i + 1])
        scan_factors(cs)
        scan_update(cs)
    for ch in range(nstate):
        c_scr[ch] = state[ch]
        m_scr[ch] = jnp.broadcast_to(m[ch], m_scr.shape[1:])


def _mlstm(mq, mk, mv, g_rows, g_cols, nct):
    bsz, ttot, nb = mq.shape
    nt = ttot // TM
    per = TM // MLSTM_CHUNK
    sb = math.gcd(bsz, MLSTM_BATCH)
    fwd = lambda w: pl.BlockSpec((sb, TM, w), lambda b, j: (b, j, 0))
    rev = lambda w: pl.BlockSpec((sb, TM, w), lambda b, j: (b, _scan_block(j, nct, nt), 0))
    gshape = (sb, per, 2 * MLSTM_HEADS, MLSTM_CHUNK)
    gfwd = pl.BlockSpec(gshape, lambda b, j: (b, j, 0, 0))
    grev = pl.BlockSpec(gshape, lambda b, j: (b, _scan_block(j, nct, nt), 0, 0))
    nstate = sb * 2 * MLSTM_HEADS
    return pl.pallas_call(
        _mlstm_kernel,
        out_shape=(jax.ShapeDtypeStruct((bsz, ttot, nb), BF16),) * 2,
        grid=(bsz // sb, nt),
        in_specs=[fwd(nb), fwd(nb), fwd(nb), gfwd, fwd(LANES), rev(nb), rev(nb), rev(nb), grev, rev(LANES)],
        out_specs=(fwd(nb), rev(nb)),
        scratch_shapes=[pltpu.VMEM((nstate, MLSTM_DH, 2 * MLSTM_DH), F32),
                        pltpu.VMEM((nstate, SUBLANES, LANES), F32)],
        compiler_params=_params("parallel", "arbitrary"),
        name="mlstm_scan",
    )(mq, mk, mv, g_rows, g_cols, mq, mk, mv, g_rows, g_cols)


def _head_rms(x, n_heads):
    w = x.shape[1] // n_heads
    return jnp.concatenate([_rms(x[:, i * w:(i + 1) * w]) for i in range(n_heads)], axis=1)


def _mid_kernel(ctx_ref, lat_ref, gate_ref, a_ref, hf_ref, hr_ref, mo_ref, z0_ref, hn_ref, wout_ref,
                sc_ref, sh_ref, g_ref, wall_ref, wgate_ref, bgate_ref,
                h_out, gq_out, gk_out, gv_out, bc_out, nq_out, nk_out, nv_out, z_out, *, nct):
    nbt = ctx_ref.shape[0]
    rows = lambda f: jnp.concatenate([f(bi) for bi in range(nbt)], axis=0)
    f32 = lambda r, bi: r[bi].astype(F32)

    def put(ref, val):
        for bi in range(nbt):
            ref[bi] = val[bi * TM:(bi + 1) * TM]

    hm = rows(lambda bi: _head_rms(f32(mo_ref, bi) * (f32(hf_ref, bi) + f32(hr_ref, bi)), MLSTM_HEADS) * hn_ref[...])
    cat = (jnp.concatenate([rows(lambda bi: f32(a_ref, bi)), hm], axis=1) * rows(lambda bi: f32(z0_ref, bi))).astype(BF16)
    y = _dot(cat, wout_ref[...])
    is_ctx = pl.program_id(1) < nct
    hs = [jnp.where(is_ctx, ctx_ref[bi], lat_ref[bi]) + gate_ref[bi] * y[bi * TM:(bi + 1) * TM] for bi in range(nbt)]
    for bi in range(nbt):
        h_out[bi] = hs[bi]

    ub = rows(lambda bi: _modnorm(hs[bi], g_ref[...], sc_ref[bi], sh_ref[bi])).astype(BF16)
    wseg = lambda name: wall_ref[:, PROJ1_COLS[name][0]:PROJ1_COLS[name][1]]
    ga = _dot(ub, wseg("ga")).astype(BF16)
    gate_pre = _dot(ga, wgate_ref[...])
    put(gq_out, (_dot(ub, wseg("q")) * GLA_DK ** -0.5).astype(BF16))
    put(gk_out, _dot(ub, wseg("k")).astype(BF16))
    put(gv_out, _dot(ub, wseg("v")).astype(BF16))

    lg = _log_sigmoid(gate_pre + bgate_ref[...]) * (1.0 / GLA_TAU)
    hi, mid, lo = _split3(lg)
    put(nq_out, (_dot(ub, wseg("nq")) * (NA_DH ** -0.5 * LOG2E)).astype(BF16))
    put(nk_out, _dot(ub, wseg("nk")).astype(BF16))
    put(nv_out, _dot(ub, wseg("nv")).astype(BF16))
    row = lax.broadcasted_iota(jnp.int32, (TM, TM), 0)
    col = lax.broadcasted_iota(jnp.int32, (TM, TM), 1)
    same = (row // CHUNK) == (col // CHUNK)
    nk = GLA_HEADS * GLA_DK
    for d in range(2):
        tri = jnp.where(jnp.logical_and(same, (col <= row) if d == 0 else (col >= row)), 1.0, 0.0).astype(BF16)
        cols = slice(d * nk, (d + 1) * nk)
        for bi in range(nbt):
            rs = slice(bi * TM, (bi + 1) * TM)
            bc_out[bi, :, cols] = (_dot(tri, hi[rs, cols]) + _dot(tri, mid[rs, cols]) + _dot(tri, lo[rs, cols])) * LOG2E
    put(z_out, _silu(_dot(ub, wseg("z"))).astype(BF16))


def _proj1_weights(w_in, w_gate, b_gate):
    nk = GLA_HEADS * GLA_DK
    nc = GLA_HEADS * GLA_DV
    nd = NA_HEADS * NA_DH
    o = np.cumsum([0, nk, nk, nc, 2 * GLA_GATE_RANK, nd, nd, nd, nc + nd])
    r = GLA_GATE_RANK
    gpad = LANES - 2 * r
    wall = jnp.concatenate([w_in[:, :o[4]], jnp.zeros((w_in.shape[0], gpad), F32), w_in[:, o[4]:]], axis=1)
    assert wall.shape[1] == PROJ1_COLS["z"][1]
    wgate = jnp.zeros((LANES, 2 * nk), F32).at[:r, :nk].set(w_gate[0]).at[r:2 * r, nk:].set(w_gate[1])
    return dict(wall=wall.astype(BF16), wgate=wgate.astype(BF16), bgate=b_gate.reshape(1, 2 * nk))


def _mid(ctx, x, gate0, a, hf, hr, mo, z0, h_norm, w_out, scale, shift, norm_g, w, nct):
    bsz = x.shape[0]
    ttot = ctx.shape[1] + x.shape[1]
    nt = ttot // TM
    sb = math.gcd(bsz, TILE_BATCH)
    tile, src_ctx, src_lat, _, _, mod = _tile_specs(nct, nt, sb)
    nb = MLSTM_HEADS * MLSTM_DH
    nk = GLA_HEADS * GLA_DK
    nc = GLA_HEADS * GLA_DV
    nd = NA_HEADS * NA_DH
    ws = [w[n] for n in ("wall", "wgate", "bgate")]
    tok = lambda width, dt: jax.ShapeDtypeStruct((bsz, ttot, width), dt)
    return pl.pallas_call(
        functools.partial(_mid_kernel, nct=nct),
        out_shape=(tok(D_MODEL, F32), tok(nk, BF16), tok(nk, BF16), tok(nc, BF16), tok(2 * nk, F32),
                   tok(nd, BF16), tok(nd, BF16), tok(nd, BF16), tok(nc + nd, BF16)),
        grid=(bsz // sb, nt),
        in_specs=[src_ctx, src_lat, mod,
                  pl.BlockSpec((sb, TM, MLA_HEADS * MLA_V), lambda b, t: (b, _mla_tile(t, nct), 0)),
                  tile(nb), tile(nb), tile(nb), tile(D_MODEL),
                  _full((1, nb)), _full(w_out.shape), mod, mod, _full((1, D_MODEL))] + [_full(a.shape) for a in ws],
        out_specs=(tile(D_MODEL), tile(nk), tile(nk), tile(nc), tile(2 * nk), tile(nd), tile(nd), tile(nd),
                   tile(nc + nd)),
        compiler_params=_params("parallel", "parallel"),
        name="out0_proj1",
    )(ctx, x, gate0, a, hf, hr, mo, z0, h_norm[None], w_out.astype(BF16), scale, shift, norm_g[None], *ws)


def _gla_kernel(qf_ref, kf_ref, vf_ref, bf_ref, qr_ref, kr_ref, vr_ref, br_ref, of_ref, or_ref, s_scr):
    @pl.when(pl.program_id(1) == 0)
    def _():
        s_scr[...] = jnp.zeros_like(s_scr)

    L = CHUNK
    nchunk = TM // L
    npair = GLA_HEADS // 2
    lane = lax.broadcasted_iota(jnp.int32, (L, LANES), 1)
    rowv = lax.broadcasted_iota(jnp.int32, (L, LANES), 0)
    srow = lax.broadcasted_iota(jnp.int32, (LANES, 2 * GLA_DV), 0)
    scol = lax.broadcasted_iota(jnp.int32, (LANES, 2 * GLA_DV), 1)
    own_block = (srow < GLA_DK) == (scol < GLA_DV)
    vcol = lax.broadcasted_iota(jnp.int32, (L, 2 * GLA_DV), 1)
    e_row = lax.broadcasted_iota(jnp.int32, (LANES, LANES), 0)
    e_col = lax.broadcasted_iota(jnp.int32, (LANES, LANES), 1)
    head_ones = jnp.where((e_row < GLA_DK) == (e_col < GLA_DK), 1.0, 0.0).astype(BF16)
    leaf_idx = lax.broadcasted_iota(jnp.int32, (GLA_LEAF * GLA_LEAF, LANES), 0)
    leaf_lane = lax.broadcasted_iota(jnp.int32, (GLA_LEAF * GLA_LEAF, LANES), 1) % L
    leaf_s, leaf_t = leaf_idx // GLA_LEAF, leaf_idx % GLA_LEAF
    rep = lambda a: jnp.concatenate(
        [jnp.broadcast_to(a[s:s + 1], (GLA_LEAF, a.shape[1])) for s in range(GLA_LEAF)], axis=0)
    til = lambda a: jnp.concatenate([a] * GLA_LEAF, axis=0)

    def chunk_step(ci, carry):
        cs = []
        for bi, d in [(bi, d) for bi in range(qf_ref.shape[0]) for d in range(2)]:
            q_ref, k_ref, v_ref, b_ref, o_ref = (
                r.at[bi] for r in ((qf_ref, kf_ref, vf_ref, bf_ref, of_ref), (qr_ref, kr_ref, vr_ref, br_ref, or_ref))[d])
            cd = ci if d == 0 else nchunk - 1 - ci
            base = pl.multiple_of(cd * L, L)
            for p in range(npair):
                ls = slice(p * LANES, (p + 1) * LANES)
                bc = b_ref[pl.ds(base, L), ls]
                cs.append(dict(
                    d=d, p=p, bi=bi, base=base, o_ref=o_ref, bc=bc,
                    tau_v=rowv if d == 0 else L - 1 - rowv,
                    tau_s=(lane % L) if d == 0 else L - 1 - (lane % L),
                    q=q_ref[pl.ds(base, L), ls].astype(F32),
                    k=k_ref[pl.ds(base, L), ls].astype(F32),
                    v=v_ref[pl.ds(base, L), p * 2 * GLA_DV:(p + 1) * 2 * GLA_DV],
                    b_last=bc[L - 1:L] if d == 0 else bc[0:1],
                    state=s_scr[bi, d, p]))

        for c in cs:
            c["o_pair"] = _dot((c["q"] * jnp.exp2(c["bc"])).astype(BF16), c["state"].astype(BF16))

        for c in cs:
            c["att"] = jnp.zeros((L, LANES), F32)
        bs = L // 2
        while bs >= GLA_LEAF:
            for c in cs:
                d, bc, tau_v = c["d"], c["bc"], c["tau_v"]
                later = (tau_v // bs) % 2 == 1
                ref_b = None
                for blk in range(L // (2 * bs)):
                    tau_ref = blk * 2 * bs + bs - 1
                    idx = tau_ref if d == 0 else L - 1 - tau_ref
                    r = jnp.broadcast_to(bc[idx:idx + 1], (L, LANES))
                    ref_b = r if ref_b is None else jnp.where(tau_v // (2 * bs) == blk, r, ref_b)
                c["qs"] = jnp.where(later, c["q"] * jnp.exp2(jnp.where(later, bc - ref_b, 0.0)), 0.0).astype(BF16)
                ks = jnp.where(later, 0.0, c["k"] * jnp.exp2(jnp.where(later, 0.0, ref_b - bc)))
                c["ks2"] = jnp.concatenate([jnp.where(lane < GLA_DK, ks, 0.0), jnp.where(lane < GLA_DK, 0.0, ks)],
                                           axis=0).astype(BF16)
            for c in cs:
                same_parent = (c["tau_v"] // (2 * bs)) == (c["tau_s"] // (2 * bs))
                c["att"] = c["att"] + jnp.where(same_parent, _dot_nt(c["qs"], c["ks2"]), 0.0)
            bs //= 2

        for c in cs:
            c["leaf"] = []
        for blk in range(L // GLA_LEAF):
            rs = slice(blk * GLA_LEAF, (blk + 1) * GLA_LEAF)
            for c in cs:
                qb, bb, kb = c["q"][rs], c["bc"][rs], c["k"][rs]
                ok = (leaf_t >= leaf_s) if c["d"] == 0 else (leaf_t <= leaf_s)
                w = jnp.where(ok, til(qb) * jnp.exp2(til(bb) - rep(bb)) * rep(kb), 0.0)
                c["a"] = _dot(w.astype(BF16), head_ones)
            for c in cs:
                a = jnp.where(leaf_lane == leaf_s + blk * GLA_LEAF, c["a"], 0.0)
                acc = a[0:GLA_LEAF]
                for s in range(1, GLA_LEAF):
                    acc = acc + a[s * GLA_LEAF:(s + 1) * GLA_LEAF]
                c["leaf"].append(acc)

        for c in cs:
            v = c["v"]
            c["att"] = (c["att"] + jnp.concatenate(c["leaf"], axis=0)).astype(BF16)
            c["v_blk"] = jnp.concatenate([jnp.where(vcol < GLA_DV, v, jnp.zeros_like(v)),
                                          jnp.where(vcol < GLA_DV, jnp.zeros_like(v), v)], axis=0)
            c["ke"] = (c["k"] * jnp.exp2(c["b_last"] - c["bc"])).astype(BF16)
            c["decay_col"] = jnp.sum(
                jnp.where(e_row == e_col, jnp.broadcast_to(jnp.exp2(c["b_last"]), (LANES, LANES)), 0.0),
                axis=1, keepdims=True)
        for c in cs:
            c["o"] = c["o_pair"] + _dot(c["att"], c["v_blk"])
            c["ds"] = _dot_tn(c["ke"], c["v"])
        for c in cs:
            p = c["p"]
            c["o_ref"][pl.ds(c["base"], L), p * 2 * GLA_DV:(p + 1) * 2 * GLA_DV] = c["o"].astype(BF16)
            s_scr[c["bi"], c["d"], p] = c["decay_col"] * c["state"] + jnp.where(own_block, c["ds"], 0.0)
        return carry

    lax.fori_loop(0, nchunk, chunk_step, 0)


def _gla(gq, gk, gv, bc, nct):
    bsz, ttot, nk = gq.shape
    nc = gv.shape[2]
    nt = ttot // TM
    sb = math.gcd(bsz, GLA_BATCH)
    fwd = lambda w: pl.BlockSpec((sb, TM, w), lambda b, j: (b, j, 0))
    rev = lambda w: pl.BlockSpec((sb, TM, w), lambda b, j: (b, _scan_block(j, nct, nt), 0))
    bfwd = pl.BlockSpec((sb, TM, nk), lambda b, j: (b, j, 0))
    brev = pl.BlockSpec((sb, TM, nk), lambda b, j: (b, _scan_block(j, nct, nt), 1))
    return pl.pallas_call(
        _gla_kernel,
        out_shape=(jax.ShapeDtypeStruct((bsz, ttot, nc), BF16),) * 2,
        grid=(bsz // sb, nt),
        in_specs=[fwd(nk), fwd(nk), fwd(nc), bfwd, rev(nk), rev(nk), rev(nc), brev],
        out_specs=(fwd(nc), rev(nc)),
        scratch_shapes=[pltpu.VMEM((sb, 2, GLA_HEADS // 2, LANES, 2 * GLA_DV), F32)],
        compiler_params=_params("parallel", "arbitrary"),
        name="gla_scan",
    )(gq, gk, gv, bc, gq, gk, gv, bc)


def _na_bias_tables(rpb, rows):
    kh = NA_KH
    c = np.arange(GRID_W)
    cs = np.clip(c - NA_KW // 2, 0, GRID_W - NA_KW)
    col_ok = (c[None, :] >= cs[:, None]) & (c[None, :] < cs[:, None] + NA_KW)
    pad = GRID_W - NA_KW
    rpb_pad = jnp.pad(rpb.astype(F32) * LOG2E, ((0, 0), (0, 0), (pad, pad)))
    period = 2 * GRID_W
    flat = jnp.tile(jnp.pad(rpb_pad, ((0, 0), (0, 0), (0, 1))), (1, 1, GRID_W))[:, :, :GRID_W * (period - 1)]
    t1 = flat.reshape(NA_HEADS, 2 * NA_KH - 1, GRID_W, period - 1)[..., GRID_W - 1:]
    t1 = jnp.where(col_ok, t1, NEG_BIG)
    masked = jnp.full((NA_HEADS, GRID_W, GRID_W), NEG_BIG, F32)
    per_row = []
    for r0 in (0, NA_QROWS, rows - NA_QROWS):
        kb = int(np.clip(r0 - kh // 2, 0, rows - NA_KROWS))
        for qrow in range(r0, r0 + NA_QROWS):
            ws = int(np.clip(qrow - kh // 2, 0, rows - kh))
            per_row.append(jnp.concatenate(
                [t1[:, krow - qrow + NA_KH - 1] if ws <= krow < ws + kh else masked
                 for krow in range(kb, kb + NA_KROWS)], axis=-1))
    return jnp.concatenate(per_row, axis=1)


def _na_heads(q_ref, kw_ref, vw_ref, kc_ref, vc_ref, bias_ref):
    nq = NA_QROWS * GRID_W
    lane = lax.broadcasted_iota(jnp.int32, (nq, LANES), 1)
    def scores(j):
        ls = slice((j // 2) * LANES, (j // 2 + 1) * LANES)
        q = q_ref[:, ls]
        qj = jnp.where((lane < NA_DH) == (j % 2 == 0), q, jnp.zeros_like(q))
        return _dot_nt(qj, kw_ref[:, ls]) + bias_ref[j], _dot_nt(qj, kc_ref[:, ls])

    outs = []
    nxt = scores(0)
    for j in range(NA_HEADS):
        s_loc, s_ctx = nxt
        if j + 1 < NA_HEADS:
            nxt = scores(j + 1)
        ls = slice((j // 2) * LANES, (j // 2 + 1) * LANES)
        m = jnp.maximum(jnp.max(s_loc, axis=-1, keepdims=True), jnp.max(s_ctx, axis=-1, keepdims=True))
        p_loc = jnp.exp2(s_loc - m)
        p_ctx = jnp.exp2(s_ctx - m)
        den = jnp.sum(p_loc, axis=-1, keepdims=True) + jnp.sum(p_ctx, axis=-1, keepdims=True)
        outs.append((_dot(p_loc.astype(BF16), vw_ref[:, ls]) + _dot(p_ctx.astype(BF16), vc_ref[:, ls])) / den)
    return jnp.concatenate([jnp.where(lane < NA_DH, outs[2 * i], outs[2 * i + 1]) for i in range(NA_HEADS // 2)], axis=1)


def _na_out1_kernel(*refs, nb):
    kv = refs[:4 * nb]
    q_ref, bias_ref, h_ref, gate_ref, of_ref, or_ref, z_ref, gn_ref, w_ref, fn_ref, o_ref = refs[4 * nb:]
    rows = lambda f: jnp.concatenate([f(bi) for bi in range(nb)], axis=0)
    f32 = lambda r, bi: r[bi].astype(F32)
    na = rows(lambda bi: _na_heads(q_ref.at[bi], *kv[4 * bi:4 * bi + 4], bias_ref))
    g = rows(lambda bi: _head_rms(f32(of_ref, bi) + f32(or_ref, bi), GLA_HEADS) * gn_ref[...])
    cat = (jnp.concatenate([g, na], axis=1) * rows(lambda bi: f32(z_ref, bi))).astype(BF16)
    y = _dot(cat, w_ref[...])
    for bi in range(nb):
        h = h_ref[bi] + gate_ref[bi] * y[bi * TM:(bi + 1) * TM]
        o_ref[bi] = _rms(h) * fn_ref[...]


def _na_out1(nq, nk, nv, bias, h, gate, of, orv, z, gla_norm, w_out, final_norm, n_ctx, rows):
    bsz, ttot, nd = nq.shape
    nrb = rows // NA_QROWS
    qrows = NA_QROWS * GRID_W
    krows = NA_KROWS * GRID_W
    assert qrows == TM
    nct = n_ctx // TM
    nc = GLA_HEADS * GLA_DV
    sb = math.gcd(bsz, TILE_BATCH)

    def variant(rb):
        return jnp.where(rb == 0, 0, jnp.where(rb == nrb - 1, 2, 1))

    def key_start(rb):
        first_row = jnp.clip(rb * NA_QROWS - NA_KH // 2, 0, rows - NA_KROWS)
        return pl.multiple_of(n_ctx + first_row * GRID_W, GRID_W)

    window = lambda i: pl.BlockSpec((None, pl.Element(krows), pl.Element(nd)), lambda b, rb: (b * sb + i, key_start(rb), 0))
    context = lambda i: pl.BlockSpec((None, pl.Element(n_ctx), pl.Element(nd)), lambda b, rb: (b * sb + i, 0, 0))
    kv_specs = [spec(i) for i in range(sb) for spec in (window, window, context, context)]
    lat = lambda w: pl.BlockSpec((sb, TM, w), lambda b, rb: (b, rb + nct, 0))
    return pl.pallas_call(
        functools.partial(_na_out1_kernel, nb=sb),
        out_shape=jax.ShapeDtypeStruct((bsz, rows * GRID_W, D_MODEL), F32),
        grid=(bsz // sb, nrb),
        in_specs=kv_specs + [
            lat(nd), pl.BlockSpec((NA_HEADS, qrows, krows), lambda b, rb: (0, variant(rb), 0)),
            lat(D_MODEL), pl.BlockSpec((sb, None, 1, D_MODEL), lambda b, rb: (b, 1, 0, 0)),
            lat(nc), lat(nc), lat(nc + nd), _full((1, nc)), _full(w_out.shape), _full((1, D_MODEL))],
        out_specs=pl.BlockSpec((sb, TM, D_MODEL), lambda b, rb: (b, rb, 0)),
        compiler_params=_params("parallel", "parallel"),
        name="na_out1",
    )(*([nk, nv, nk, nv] * sb), nq, bias, h, gate, of, orv, z, gla_norm[None], w_out.astype(BF16), final_norm[None])


def kernel(x, c, ctx, c_ctx, l0_norm, l0_w_mod, l0_b_mod, l0_w_in, l0_mla_q_norm, l0_mla_w_uq, l0_mla_kv_norm, l0_mla_w_ukv, l0_mlstm_conv_w, l0_mlstm_conv_b, l0_mlstm_b_i, l0_mlstm_b_f, l0_mlstm_norm, l0_w_out, l1_norm, l1_w_mod, l1_b_mod, l1_w_in, l1_gla_w_gate, l1_gla_b_gate, l1_gla_norm, l1_na_rpb, l1_w_out, final_norm):
    bsz, seq, d = x.shape
    n_ctx = ctx.shape[1]
    rows = seq // GRID_W
    assert d == D_MODEL and seq % TM == 0 and n_ctx == TM
    assert rows % NA_QROWS == 0 and rows >= NA_KROWS and rows // NA_QROWS >= 3
    nct = n_ctx // TM

    (shift0, scale0, gate0), (shift, scale, gate) = _mod_vectors(
        c, c_ctx, ((l0_w_mod, l0_b_mod), (l1_w_mod, l1_b_mod)))

    cos_t, sin_t = _rope_tables(n_ctx, seq)
    w0 = _proj0_weights(l0_w_in, l0_mla_q_norm, l0_mla_w_uq, l0_mla_kv_norm, l0_mla_w_ukv,
                        l0_mlstm_conv_w, l0_mlstm_conv_b, l0_mlstm_b_i, l0_mlstm_b_f)
    q, k, v, mq, mk, mv, g_rows, g_cols, mo, z0 = _proj0(ctx, x, scale0, shift0, l0_norm, cos_t, sin_t, w0, nct)
    a = _mla_attention(q, k, v, nct)
    hf, hr = _mlstm(mq, mk, mv, g_rows, g_cols, nct)

    w1 = _proj1_weights(l1_w_in, l1_gla_w_gate, l1_gla_b_gate)
    h, gq, gk, gv, bc, nq, nk, nv, z = _mid(ctx, x, gate0, a, hf, hr, mo, z0, l0_mlstm_norm, l0_w_out,
                                            scale, shift, l1_norm, w1, nct)
    of, orv = _gla(gq, gk, gv, bc, nct)
    return _na_out1(nq, nk, nv, _na_bias_tables(l1_na_rpb, rows), h, gate, of, orv, z,
                    l1_gla_norm, l1_w_out, final_norm, n_ctx, rows)
```

```python
import functools
import math

import jax
import jax.numpy as jnp
import numpy as np
from jax import lax
from jax.experimental import pallas as pl
from jax.experimental.pallas import tpu as pltpu

F32 = jnp.float32
BF16 = jnp.bfloat16

D_MODEL = 1024
GRID_W = 64
EPS = 1e-6
ROPE_BASE = 10000.0

MLA_HEADS = 8
MLA_Q_RANK = 384
MLA_KV_RANK = 256
MLA_NOPE = 64
MLA_ROPE = 32
MLA_V = 64
ROPE_PAIRS = MLA_ROPE // 4
MLSTM_HEADS = 4
MLSTM_DH = 128
MLSTM_CONV = 3
GLA_HEADS = 4
GLA_DK = 64
GLA_DV = 128
GLA_GATE_RANK = 16
GLA_TAU = 16.0
NA_HEADS = 8
NA_DH = 64
NA_KH = 8
NA_KW = 16

CHUNK = 64
MLSTM_CHUNK = 128
TM = 256
LANES = 128
SUBLANES = 8
HALO = SUBLANES
NA_QROWS = 4
NA_KROWS = NA_QROWS + NA_KH - 1
GLA_LEAF = 8
MLA_HPS = 4
MLA_QTILES = 2
TILE_BATCH = 2
MLSTM_BATCH = 2
GLA_BATCH = 8
VMEM_LIMIT = 56 * 1024 * 1024
NEG_BIG = -1e30
LOG2E = 1.4426950408889634


def _dot(a, b):
    return jnp.dot(a, b, preferred_element_type=F32)


def _dot_nt(a, b):
    return lax.dot_general(a, b, (((1,), (1,)), ((), ())), preferred_element_type=F32)


def _dot_tn(a, b):
    return lax.dot_general(a, b, (((0,), (0,)), ((), ())), preferred_element_type=F32)


def _rms(x):
    return x * lax.rsqrt(jnp.mean(x * x, axis=-1, keepdims=True) + EPS)


def _sigmoid(x):
    return 1.0 / (1.0 + jnp.exp(-x))


def _silu(x):
    return x * _sigmoid(x)


def _log_sigmoid(x):
    return jnp.minimum(x, 0.0) - jnp.log(1.0 + jnp.exp(-jnp.abs(x)))


def _params(*sem):
    return pltpu.CompilerParams(dimension_semantics=sem, vmem_limit_bytes=VMEM_LIMIT)


def _full(shape):
    nd = len(shape)
    return pl.BlockSpec(shape, lambda *_: (0,) * nd)


def _mod_kernel(c_ref, w0_ref, b0_ref, w1_ref, b1_ref, o0_ref, o1_ref):
    sc = _silu(c_ref[...]).astype(BF16)
    o0_ref[...] = _dot(sc, w0_ref[...].astype(BF16)) + b0_ref[...]
    o1_ref[...] = _dot(sc, w1_ref[...].astype(BF16)) + b1_ref[...]


def _mod_vectors(c, c_ctx, mods):
    bsz, d = c.shape
    rows = -(-(bsz + 1) // SUBLANES) * SUBLANES
    cc = jnp.zeros((rows, d), F32).at[:bsz].set(c).at[bsz].set(c_ctx)
    (w0, b0), (w1, b1) = mods
    n = w0.shape[1]
    wspec = pl.BlockSpec((d, d), lambda j: (0, j))
    vspec = pl.BlockSpec((1, d), lambda j: (0, j))
    ospec = pl.BlockSpec((rows, d), lambda j: (0, j))
    per_layer = pl.pallas_call(
        _mod_kernel,
        out_shape=(jax.ShapeDtypeStruct((rows, n), F32),) * 2,
        grid=(n // d,),
        in_specs=[_full((rows, d)), wspec, vspec, wspec, vspec],
        out_specs=(ospec, ospec),
        compiler_params=_params("parallel"),
        name="modulation",
    )(cc, w0, b0[None], w1, b1[None])
    out = []
    for mod in per_layer:
        vecs = []
        for part in jnp.split(mod, 3, axis=-1):
            ctx_v = jnp.broadcast_to(part[bsz][None], (bsz, d))
            vecs.append(jnp.stack([ctx_v, part[:bsz]], axis=1)[:, :, None, :])
        out.append(vecs)
    return out


def _modnorm(x, g, scale, shift):
    return _rms(x) * g * (1.0 + scale) + shift


def _seg_scan(x, reverse, op, fill):
    width = x.shape[1]
    pos = lax.broadcasted_iota(jnp.int32, x.shape, 1) % MLSTM_CHUNK
    k = 1
    while k < MLSTM_CHUNK:
        if reverse:
            x = op(x, jnp.where(pos < MLSTM_CHUNK - k, pltpu.roll(x, width - k, axis=1), fill))
        else:
            x = op(x, jnp.where(pos >= k, pltpu.roll(x, k, axis=1), fill))
        k *= 2
    return x


def _split3(x):
    hi = x.astype(BF16)
    r1 = x - hi.astype(F32)
    mid = r1.astype(BF16)
    return hi, mid, (r1 - mid.astype(F32)).astype(BF16)


def _proj0_kernel(ctx_ref, lat_ref, hp_ref, hn_ref, sc_ref, sh_ref, g_ref, cos_ref, sin_ref,
                  wall_ref, wgt_ref, qn_ref, wuq_ref, kvn_ref, wk_ref, wv_ref, cw_ref, cb_ref, gb_ref,
                  q_out, k_out, v_out, mq_out, mk_out, mv_out, gr_out, gc_out, mo_out, z_out,
                  pbuf, *, nct, nt):
    t = pl.program_id(1)
    h = jnp.where(t < nct, ctx_ref[...], lat_ref[...])
    x = jnp.concatenate([hp_ref[...], h, hn_ref[...]], axis=0)
    ub_ext = _modnorm(x, g_ref[...], sc_ref[...], sh_ref[...]).astype(BF16)
    ub = ub_ext[HALO:HALO + TM]

    gt = _dot_nt(wgt_ref[...], ub) + gb_ref[...]
    half = MLSTM_HEADS
    per_dir = []
    for d in range(2):
        gd = gt[d * SUBLANES:(d + 1) * SUBLANES]
        bcum = _seg_scan(_log_sigmoid(gd), d == 1, jnp.add, 0.0)
        b_top = pltpu.roll(bcum, half, axis=0)
        c8 = gd - b_top
        pm8 = _seg_scan(c8, d == 1, jnp.maximum, -jnp.inf)
        per_dir.append((c8, pm8, b_top))

    cos = cos_ref[...]
    sin = sin_ref[...]
    cos_all = jnp.concatenate([cos] * MLA_HEADS, axis=1)
    sin_all = jnp.concatenate([sin] * MLA_HEADS, axis=1)

    def swap_halves(a):
        lane = lax.broadcasted_iota(jnp.int32, a.shape, 1)
        first = lane % (2 * ROPE_PAIRS) < ROPE_PAIRS
        return jnp.where(first, pltpu.roll(a, a.shape[1] - ROPE_PAIRS, axis=1), pltpu.roll(a, ROPE_PAIRS, axis=1))

    wseg = lambda name: wall_ref[:, PROJ0_COLS[name][0]:PROJ0_COLS[name][1]]
    cqkr = _dot(ub, wseg("cqkr"))
    cq = cqkr[:, :MLA_Q_RANK]
    kr = cqkr[:, MLA_Q_RANK:]
    ckv = _dot(ub, wseg("ckv"))
    pqk = _dot(ub_ext, wseg("mqk"))

    nq = (_rms(cq) * qn_ref[...]).astype(BF16)
    nkv = (_rms(ckv) * kvn_ref[...]).astype(BF16)

    q = _dot(nq, wuq_ref[...])
    kn = _dot(nkv, wk_ref[...])
    vv = _dot(nkv, wv_ref[...])

    prev_ok = t > nct
    next_ok = jnp.logical_and(t >= nct, t != nt - 1)
    row = lax.broadcasted_iota(jnp.int32, pqk.shape, 0)
    keep = jnp.logical_and(jnp.logical_or(row >= HALO, prev_ok), jnp.logical_or(row < HALO + TM, next_ok))
    pbuf[...] = jnp.where(keep, pqk, 0.0)
    cw = cw_ref[...]
    y = (pbuf[HALO - 1:HALO - 1 + TM, :] * cw[0:1] + pbuf[HALO:HALO + TM, :] * cw[1:2]
         + pbuf[HALO + 1:HALO + 1 + TM, :] * cw[2:3] + cb_ref[...])
    y = _silu(y)
    nb = MLSTM_HEADS * MLSTM_DH
    mq_out[...] = (y[:, :nb] * MLSTM_DH ** -0.5).astype(BF16)
    mk_out[...] = y[:, nb:].astype(BF16)

    mv = _dot(ub, wseg("mv"))
    mo = _dot(ub, wseg("mo"))

    q = q * cos_all + swap_halves(q) * sin_all
    q_out[...] = (q * ((MLA_NOPE + MLA_ROPE) ** -0.5 * LOG2E)).astype(BF16)
    rot = kr * cos + swap_halves(kr) * sin
    k_out[...] = (kn + jnp.concatenate([rot] * MLA_HEADS, axis=1)).astype(BF16)
    vlane = lax.broadcasted_iota(jnp.int32, (TM, MLA_HEADS * LANES), 1)
    is_value = ((vlane % LANES) < MLA_V) == ((vlane // LANES) % 2 == 0)
    v_out[...] = jnp.where(is_value, vv, 1.0).astype(BF16)

    zz = _dot(ub, wseg("z"))
    mv_out[...] = mv.astype(BF16)
    mo_out[...] = _sigmoid(mo).astype(BF16)
    z_out[...] = _silu(zz).astype(BF16)

    r8 =lax.broadcasted_iota(jnp.int32, (SUBLANES, TM), 0)
    both = lambda i: jnp.where(r8 < half, per_dir[0][i], pltpu.roll(per_dir[1][i], half, axis=0)) * LOG2E
    c_rows, pm_rows, b_rows = both(0), both(1), both(2)
    for ci in range(TM // MLSTM_CHUNK):
        gr_out[ci] = c_rows[:, ci * MLSTM_CHUNK:(ci + 1) * MLSTM_CHUNK]
    stack = jnp.concatenate([c_rows, pm_rows, b_rows, jnp.zeros((LANES - 3 * SUBLANES, TM), F32)], axis=0)
    er = lax.broadcasted_iota(jnp.int32, (TM, TM), 0)
    ec = lax.broadcasted_iota(jnp.int32, (TM, TM), 1)
    eye = jnp.where(er == ec, 1.0, 0.0).astype(BF16)
    gc_out[...] = sum(_dot_nt(eye, part) for part in _split3(stack))


def _rope_tables(n_ctx, seq):
    t = jnp.arange(seq)
    inv = 1.0 / (ROPE_BASE ** (jnp.arange(ROPE_PAIRS, dtype=F32) / ROPE_PAIRS))
    ang = jnp.concatenate([(t // GRID_W)[:, None] * inv, (t % GRID_W)[:, None] * inv], axis=-1)
    cos, sin = jnp.cos(ang), jnp.sin(ang)
    j = np.arange(MLA_ROPE)
    src = (j // (2 * ROPE_PAIRS)) * ROPE_PAIRS + (j % ROPE_PAIRS)
    sign = np.where((j % (2 * ROPE_PAIRS)) < ROPE_PAIRS, -1.0, 1.0).astype(np.float32)
    cos_full = cos[:, src]
    sin_full = sin[:, src] * sign
    cos_t = jnp.ones((n_ctx + seq, LANES), F32).at[n_ctx:, MLA_NOPE:MLA_NOPE + MLA_ROPE].set(cos_full)
    sin_t = jnp.zeros((n_ctx + seq, LANES), F32).at[n_ctx:, MLA_NOPE:MLA_NOPE + MLA_ROPE].set(sin_full)
    return cos_t, sin_t


def _segments(widths):
    out, start = {}, 0
    for name, width in widths:
        out[name] = (start, start + width)
        start += width
    return out


PROJ0_COLS = _segments([("cqkr", MLA_Q_RANK + LANES), ("ckv", MLA_KV_RANK), ("mqk", 2 * MLSTM_HEADS * MLSTM_DH),
                        ("mv", MLSTM_HEADS * MLSTM_DH), ("mo", MLSTM_HEADS * MLSTM_DH), ("z", D_MODEL)])
PROJ1_COLS = _segments([("q", GLA_HEADS * GLA_DK), ("k", GLA_HEADS * GLA_DK), ("v", GLA_HEADS * GLA_DV),
                        ("ga", LANES), ("nq", NA_HEADS * NA_DH), ("nk", NA_HEADS * NA_DH), ("nv", NA_HEADS * NA_DH),
                        ("z", GLA_HEADS * GLA_DV + NA_HEADS * NA_DH)])


def _proj0_weights(w_in, q_norm, w_uq, kv_norm, w_ukv, conv_w, conv_b, b_i, b_f):
    o = np.cumsum([0, MLA_Q_RANK, MLA_KV_RANK, MLA_ROPE] + [MLSTM_HEADS * MLSTM_DH] * 4 + [4 * MLSTM_HEADS, D_MODEL])
    dq = MLA_NOPE + MLA_ROPE
    wkr = jnp.pad(w_in[:, o[2]:o[3]], ((0, 0), (MLA_NOPE, LANES - dq)))
    wuq = jnp.pad(w_uq.reshape(MLA_Q_RANK, MLA_HEADS, dq), ((0, 0), (0, 0), (0, LANES - dq)))
    wkv = w_ukv.reshape(MLA_KV_RANK, MLA_HEADS, MLA_NOPE + MLA_V)
    wk = jnp.pad(wkv[:, :, :MLA_NOPE], ((0, 0), (0, 0), (0, LANES - MLA_NOPE)))
    zv = jnp.zeros((MLA_KV_RANK, MLA_HEADS // 2, MLA_V), F32)
    wv = jnp.concatenate([wkv[:, 0::2, MLA_NOPE:], zv, zv, wkv[:, 1::2, MLA_NOPE:]], axis=-1)
    wv = wv.reshape(MLA_KV_RANK, MLA_HEADS * LANES)
    gbias = jnp.stack([b_i, b_f], axis=1).reshape(4 * MLSTM_HEADS, 1)
    bf = lambda a: a.astype(BF16)
    wall = jnp.concatenate([w_in[:, o[0]:o[1]], wkr, w_in[:, o[1]:o[2]], w_in[:, o[3]:o[7]], w_in[:, o[8]:o[9]]], axis=1)
    assert wall.shape[1] == PROJ0_COLS["z"][1]
    return dict(
        wall=bf(wall), wgt=bf(w_in[:, o[7]:o[8]].T),
        qn=q_norm[None], wuq=bf(wuq.reshape(MLA_Q_RANK, -1)),
        kvn=kv_norm[None], wk=bf(wk.reshape(MLA_KV_RANK, -1)), wv=bf(wv),
        cw=jnp.zeros((SUBLANES, conv_w.shape[1]), F32).at[:MLSTM_CONV].set(conv_w), cb=conv_b[None], gb=gbias)


def _tile_specs(nct, nt, nb):
    tile = lambda w: pl.BlockSpec((nb, TM, w), lambda b, t: (b, t, 0))
    per = TM // HALO
    nlat = nt - nct
    src_ctx = pl.BlockSpec((nb, TM, D_MODEL), lambda b, t: (b, jnp.minimum(t, nct - 1), 0))
    src_lat = pl.BlockSpec((nb, TM, D_MODEL), lambda b, t: (b, jnp.maximum(t - nct, 0), 0))
    prev = pl.BlockSpec((nb, HALO, D_MODEL), lambda b, t: (b, jnp.maximum((t - nct) * per - 1, 0), 0))
    nxt = pl.BlockSpec((nb, HALO, D_MODEL),
                       lambda b, t: (b, jnp.clip((t - nct + 1) * per, 0, nlat * per - 1), 0))
    mod = pl.BlockSpec((nb, None, 1, D_MODEL), lambda b, t: (b, (t >= nct).astype(jnp.int32), 0, 0))
    return tile, src_ctx, src_lat, prev, nxt, mod


def _per_element(tile_kernel, shared):
    def kernel(*refs, **kw):
        for bi in range(refs[0].shape[0]):
            tile_kernel(*[r if i in shared else r.at[bi] for i, r in enumerate(refs)], **kw)
    return kernel


def _proj0(ctx, x, scale, shift, norm_g, cos_t, sin_t, w, nct):
    bsz = x.shape[0]
    ttot = ctx.shape[1] + x.shape[1]
    nt = ttot // TM
    nb = MLSTM_HEADS * MLSTM_DH
    sb = math.gcd(bsz, TILE_BATCH)
    tile, src_ctx, src_lat, prev, nxt, mod = _tile_specs(nct, nt, sb)
    tab = pl.BlockSpec((TM, LANES), lambda b, t: (t, 0))
    wnames = ["wall", "wgt", "qn", "wuq", "kvn", "wk", "wv", "cw", "cb", "gb"]
    ws = [w[n] for n in wnames]
    tok = lambda width, dt: jax.ShapeDtypeStruct((bsz, ttot, width), dt)
    out_shape = (tok(MLA_HEADS * LANES, BF16), tok(MLA_HEADS * LANES, BF16), tok(MLA_HEADS * LANES, BF16),
                 tok(nb, BF16), tok(nb, BF16), tok(nb, BF16),
                 jax.ShapeDtypeStruct((bsz, ttot // MLSTM_CHUNK, 2 * MLSTM_HEADS, MLSTM_CHUNK), F32), tok(LANES, F32),
                 tok(nb, BF16), tok(D_MODEL, BF16))
    out_specs = (tile(MLA_HEADS * LANES), tile(MLA_HEADS * LANES), tile(MLA_HEADS * LANES),
                 tile(nb), tile(nb), tile(nb),
                 pl.BlockSpec((sb, TM // MLSTM_CHUNK, 2 * MLSTM_HEADS, MLSTM_CHUNK), lambda b, t: (b, t, 0, 0)),
                 tile(LANES),
                 tile(nb), tile(D_MODEL))
    n_blocked, n_out = 6, len(out_shape)
    n_in = n_blocked + 3 + len(ws)
    shared = set(range(n_blocked, n_in)) | {n_in + n_out}
    return pl.pallas_call(
        functools.partial(_per_element(_proj0_kernel, shared), nct=nct, nt=nt),
        out_shape=out_shape,
        grid=(bsz // sb, nt),
        in_specs=[src_ctx, src_lat, prev, nxt, mod, mod, _full((1, D_MODEL)), tab, tab] + [_full(a.shape) for a in ws],
        out_specs=out_specs,
        scratch_shapes=[pltpu.VMEM((TM + 2 * HALO, 2 * nb), F32)],
        compiler_params=_params("parallel", "parallel"),
        name="proj0",
    )(ctx, x, x, x, scale, shift, norm_g[None], cos_t, sin_t, *ws)


def _mla_kernel(qa_ref, qb_ref, k_ref, v_ref, o_ref, *, nkeys_ctx):
    step = pl.program_id(2)

    def attend(queries, nkeys, rows):
        qk = lambda j: _dot_nt(queries(j), k_ref[0:nkeys, j * LANES:(j + 1) * LANES])
        outs = []
        s_next = qk(0)
        for j in range(MLA_HPS):
            s = s_next
            if j + 1 < MLA_HPS:
                s_next = qk(j + 1)
            p = jnp.exp2(s - jnp.max(s, axis=-1, keepdims=True))
            o = _dot(p.astype(BF16), v_ref[0:nkeys, j * LANES:(j + 1) * LANES])
            outs.append(o / pltpu.roll(o, MLA_V, axis=1))
        lane = lax.broadcasted_iota(jnp.int32, outs[0].shape, 1)
        for j in range(MLA_HPS // 2):
            o_ref[rows, j * LANES:(j + 1) * LANES] = jnp.where(lane < MLA_V, outs[2 * j], outs[2 * j + 1]).astype(BF16)

    @pl.when(step == 0)
    def _():
        attend(lambda j: qa_ref[:, j * LANES:(j + 1) * LANES], nkeys_ctx, slice(0, TM))
        o_ref[TM:, :] = jnp.zeros((o_ref.shape[0] - TM, o_ref.shape[1]), BF16)

    @pl.when(step > 0)
    def _():
        stacked = lambda j: jnp.concatenate([qa_ref[:, j * LANES:(j + 1) * LANES],
                                             qb_ref[:, j * LANES:(j + 1) * LANES]], axis=0)
        attend(stacked, k_ref.shape[0], slice(None))


def _mla_attention(q, k, v, nct):
    bsz, ttot, _ = q.shape
    nt = ttot // TM
    assert nct == 1 and MLA_QTILES == 2 and (nt - nct) % MLA_QTILES == 0
    hw = MLA_HPS * LANES
    nsteps = 1 + (nt - nct) // MLA_QTILES
    return pl.pallas_call(
        functools.partial(_mla_kernel, nkeys_ctx=nct * TM),
        out_shape=jax.ShapeDtypeStruct((bsz, nsteps * MLA_QTILES * TM, MLA_HEADS * MLA_V), BF16),
        grid=(bsz, MLA_HEADS // MLA_HPS, nsteps),
        in_specs=[pl.BlockSpec((None, TM, hw), lambda b, p, s: (b, jnp.maximum(2 * s - 1, 0), p)),
                  pl.BlockSpec((None, TM, hw), lambda b, p, s: (b, jnp.maximum(2 * s, 1), p)),
                  pl.BlockSpec((None, ttot, hw), lambda b, p, s: (b, 0, p)),
                  pl.BlockSpec((None, ttot, hw), lambda b, p, s: (b, 0, p))],
        out_specs=pl.BlockSpec((None, MLA_QTILES * TM, hw // 2), lambda b, p, s: (b, s, p)),
        compiler_params=_params("parallel", "parallel", "parallel"),
        name="mla_attention",
    )(q, q, k, v)


def _mla_tile(t, nct):
    return t + (t >= nct).astype(jnp.int32)


def _scan_block(j, nct, nt):
    return jnp.where(j < nct, nct - 1 - j, nt - 1 - (j - nct))


def _mlstm_kernel(qf_ref, kf_ref, vf_ref, grf_ref, gcf_ref, qr_ref, kr_ref, vr_ref, grr_ref, gcr_ref,
                  hf_ref, hr_ref, c_scr, m_scr):
    @pl.when(pl.program_id(1) == 0)
    def _():
        c_scr[...] = jnp.zeros_like(c_scr)
        m_scr[...] = jnp.zeros_like(m_scr)

    L = MLSTM_CHUNK
    nchunk = TM // L
    nh = MLSTM_HEADS
    nchain = 2 * nh
    row = lax.broadcasted_iota(jnp.int32, (L, L), 0)
    col = lax.broadcasted_iota(jnp.int32, (L, L), 1)
    ones = jnp.ones((L, MLSTM_DH), BF16)
    wide = lambda a: jnp.broadcast_to(a, (L, MLSTM_DH))
    twice = lambda a: jnp.concatenate([a, a], axis=1)
    refs = ((qf_ref, kf_ref, vf_ref, grf_ref, gcf_ref, hf_ref), (qr_ref, kr_ref, vr_ref, grr_ref, gcr_ref, hr_ref))
    nbatch = qf_ref.shape[0]

    def load(ci):
        cs = []
        for bi, d in [(bi, d) for bi in range(nbatch) for d in range(2)]:
            q_ref, k_ref, v_ref, gr_ref, gc_ref, o_ref = (r.at[bi] for r in refs[d])
            cd = ci if d == 0 else nchunk - 1 - ci
            rs = slice(cd * L, (cd + 1) * L)
            g_rows = gr_ref[cd]
            g_cols = gc_ref[rs, :]
            last = L - 1 if d == 0 else 0
            for hd in range(nh):
                ch = d * nh + hd
                hs = slice(hd * MLSTM_DH, (hd + 1) * MLSTM_DH)
                cs.append(dict(
                    rs=rs, hs=hs, o_ref=o_ref, causal=(col <= row) if d == 0 else (col >= row),
                    c_row=g_rows[ch:ch + 1], c_col=g_cols[:, ch:ch + 1],
                    pm_col=g_cols[:, nchain + ch:nchain + ch + 1], b_col=g_cols[:, 2 * nchain + ch:2 * nchain + ch + 1],
                    pm_last=g_cols[last:last + 1, nchain + ch:nchain + ch + 1],
                    b_last=g_cols[last:last + 1, 2 * nchain + ch:2 * nchain + ch + 1],
                    q=q_ref[rs, hs], k=k_ref[rs, hs],
                    v_ext=jnp.concatenate([v_ref[rs, hs], ones], axis=1)))
        return cs

    def prep_scores(cs):
        for c in cs:
            c["s"] = _dot_nt(c["q"], c["k"])

    def prep_weights(cs):
        for c in cs:
            c["pm_w"] = wide(c["pm_col"])
            c["b_w"] = wide(c["b_col"])
            c["w"] = jnp.exp2(jnp.where(c["causal"], c["c_row"] - c["pm_w"][:, :L], -jnp.inf))
            c["kwf"] = jnp.exp2(wide(c["c_col"] - c["pm_last"]))

    def prep_operands(cs):
        for c in cs:
            c["p"] = (c["s"] * c["w"]).astype(BF16)
            c["kw"] = (c["kwf"] * c["k"].astype(F32)).astype(BF16)

    def prep_products(cs):
        for c in cs:
            c["o1"] = _dot(c["p"], c["v_ext"])
            c["dc"] = _dot_tn(c["kw"], c["v_ext"])

    def scan_read(cs):
        for ch, c in enumerate(cs):
            c["qc"] = _dot(c["q"], state[ch].astype(BF16))

    def scan_factors(cs):
        for ch, c in enumerate(cs):
            g_w = jnp.maximum(m[ch], c["pm_w"])
            c["e_intra"] = jnp.exp2(c["pm_w"] - g_w)
            c["e_inter"] = jnp.exp2(m[ch] - g_w)
            c["floor"] = jnp.exp2(-(c["b_w"] + g_w))
            g_last = jnp.maximum(m[ch], c["pm_last"])
            c["keep"] = jnp.exp2(m[ch] - g_last)
            c["gain"] = jnp.exp2(c["pm_last"] - g_last)
            c["m_new"] = c["b_last"] + g_last

    def scan_update(cs):
        for ch, c in enumerate(cs):
            o = twice(c["e_intra"]) * c["o1"] + twice(c["e_inter"]) * c["qc"]
            h = o[:, :MLSTM_DH] / jnp.maximum(jnp.abs(o[:, MLSTM_DH:]), c["floor"])
            c["o_ref"][c["rs"], c["hs"]] = h.astype(BF16)
            state[ch] = c["keep"] * state[ch] + c["gain"] * c["dc"]
            m[ch] = c["m_new"]

    nstate = nbatch * nchain
    state = [c_scr[ch] for ch in range(nstate)]
    m = [m_scr[ch, 0:1, 0:1] for ch in range(nstate)]
    chunks = [load(ci) for ci in range(nchunk)]
    everything = [c for cs in chunks for c in cs]
    for stage in (prep_scores, prep_weights, prep_operands):
        stage(everything)
    prep_products(chunks[0])
    for i, cs in enumerate(chunks):
        scan_read(cs)
        if i + 1 < nchunk:
            prep_products(chunks[i + 1])
        scan_factors(cs)
        scan_update(cs)
    for ch in range(nstate):
        c_scr[ch] = state[ch]
        m_scr[ch] = jnp.broadcast_to(m[ch], m_scr.shape[1:])


def _mlstm(mq, mk, mv, g_rows, g_cols, nct):
    bsz, ttot, nb = mq.shape
    nt = ttot // TM
    per = TM // MLSTM_CHUNK
    sb = math.gcd(bsz, MLSTM_BATCH)
    fwd = lambda w: pl.BlockSpec((sb, TM, w), lambda b, j: (b, j, 0))
    rev = lambda w: pl.BlockSpec((sb, TM, w), lambda b, j: (b, _scan_block(j, nct, nt), 0))
    gshape = (sb, per, 2 * MLSTM_HEADS, MLSTM_CHUNK)
    gfwd = pl.BlockSpec(gshape, lambda b, j: (b, j, 0, 0))
    grev = pl.BlockSpec(gshape, lambda b, j: (b, _scan_block(j, nct, nt), 0, 0))
    nstate = sb * 2 * MLSTM_HEADS
    return pl.pallas_call(
        _mlstm_kernel,
        out_shape=(jax.ShapeDtypeStruct((bsz, ttot, nb), BF16),) * 2,
        grid=(bsz // sb, nt),
        in_specs=[fwd(nb), fwd(nb), fwd(nb), gfwd, fwd(LANES), rev(nb), rev(nb), rev(nb), grev, rev(LANES)],
        out_specs=(fwd(nb), rev(nb)),
        scratch_shapes=[pltpu.VMEM((nstate, MLSTM_DH, 2 * MLSTM_DH), F32),
                        pltpu.VMEM((nstate, SUBLANES, LANES), F32)],
        compiler_params=_params("parallel", "arbitrary"),
        name="mlstm_scan",
    )(mq, mk, mv, g_rows, g_cols, mq, mk, mv, g_rows, g_cols)


def _head_rms(x, n_heads):
    w = x.shape[1] // n_heads
    return jnp.concatenate([_rms(x[:, i * w:(i + 1) * w]) for i in range(n_heads)], axis=1)


def _mid_kernel(ctx_ref, lat_ref, gate_ref, a_ref, hf_ref, hr_ref, mo_ref, z0_ref, hn_ref, wout_ref,
                sc_ref, sh_ref, g_ref, wall_ref, wgate_ref, bgate_ref,
                h_out, gq_out, gk_out, gv_out, bc_out, nq_out, nk_out, nv_out, z_out, *, nct):
    nbt = ctx_ref.shape[0]
    rows = lambda f: jnp.concatenate([f(bi) for bi in range(nbt)], axis=0)
    f32 = lambda r, bi: r[bi].astype(F32)

    def put(ref, val):
        for bi in range(nbt):
            ref[bi] = val[bi * TM:(bi + 1) * TM]

    hm = rows(lambda bi: _head_rms(f32(mo_ref, bi) * (f32(hf_ref, bi) + f32(hr_ref, bi)), MLSTM_HEADS) * hn_ref[...])
    cat = (jnp.concatenate([rows(lambda bi: f32(a_ref, bi)), hm], axis=1) * rows(lambda bi: f32(z0_ref, bi))).astype(BF16)
    y = _dot(cat, wout_ref[...])
    is_ctx = pl.program_id(1) < nct
    hs = [jnp.where(is_ctx, ctx_ref[bi], lat_ref[bi]) + gate_ref[bi] * y[bi * TM:(bi + 1) * TM] for bi in range(nbt)]
    for bi in range(nbt):
        h_out[bi] = hs[bi]

    ub = rows(lambda bi: _modnorm(hs[bi], g_ref[...], sc_ref[bi], sh_ref[bi])).astype(BF16)
    wseg = lambda name: wall_ref[:, PROJ1_COLS[name][0]:PROJ1_COLS[name][1]]
    ga = _dot(ub, wseg("ga")).astype(BF16)
    gate_pre = _dot(ga, wgate_ref[...])
    put(gq_out, (_dot(ub, wseg("q")) * GLA_DK ** -0.5).astype(BF16))
    put(gk_out, _dot(ub, wseg("k")).astype(BF16))
    put(gv_out, _dot(ub, wseg("v")).astype(BF16))

    lg = _log_sigmoid(gate_pre + bgate_ref[...]) * (1.0 / GLA_TAU)
    hi, mid, lo = _split3(lg)
    put(nq_out, (_dot(ub, wseg("nq")) * (NA_DH ** -0.5 * LOG2E)).astype(BF16))
    put(nk_out, _dot(ub, wseg("nk")).astype(BF16))
    put(nv_out, _dot(ub, wseg("nv")).astype(BF16))
    row = lax.broadcasted_iota(jnp.int32, (TM, TM), 0)
    col = lax.broadcasted_iota(jnp.int32, (TM, TM), 1)
    same = (row // CHUNK) == (col // CHUNK)
    nk = GLA_HEADS * GLA_DK
    for d in range(2):
        tri = jnp.where(jnp.logical_and(same, (col <= row) if d == 0 else (col >= row)), 1.0, 0.0).astype(BF16)
        cols = slice(d * nk, (d + 1) * nk)
        for bi in range(nbt):
            rs = slice(bi * TM, (bi + 1) * TM)
            bc_out[bi, :, cols] = (_dot(tri, hi[rs, cols]) + _dot(tri, mid[rs, cols]) + _dot(tri, lo[rs, cols])) * LOG2E
    put(z_out, _silu(_dot(ub, wseg("z"))).astype(BF16))


def _proj1_weights(w_in, w_gate, b_gate):
    nk = GLA_HEADS * GLA_DK
    nc = GLA_HEADS * GLA_DV
    nd = NA_HEADS * NA_DH
    o = np.cumsum([0, nk, nk, nc, 2 * GLA_GATE_RANK, nd, nd, nd, nc + nd])
    r = GLA_GATE_RANK
    gpad = LANES - 2 * r
    wall = jnp.concatenate([w_in[:, :o[4]], jnp.zeros((w_in.shape[0], gpad), F32), w_in[:, o[4]:]], axis=1)
    assert wall.shape[1] == PROJ1_COLS["z"][1]
    wgate = jnp.zeros((LANES, 2 * nk), F32).at[:r, :nk].set(w_gate[0]).at[r:2 * r, nk:].set(w_gate[1])
    return dict(wall=wall.astype(BF16), wgate=wgate.astype(BF16), bgate=b_gate.reshape(1, 2 * nk))


def _mid(ctx, x, gate0, a, hf, hr, mo, z0, h_norm, w_out, scale, shift, norm_g, w, nct):
    bsz = x.shape[0]
    ttot = ctx.shape[1] + x.shape[1]
    nt = ttot // TM
    sb = math.gcd(bsz, TILE_BATCH)
    tile, src_ctx, src_lat, _, _, mod = _tile_specs(nct, nt, sb)
    nb = MLSTM_HEADS * MLSTM_DH
    nk = GLA_HEADS * GLA_DK
    nc = GLA_HEADS * GLA_DV
    nd = NA_HEADS * NA_DH
    ws = [w[n] for n in ("wall", "wgate", "bgate")]
    tok = lambda width, dt: jax.ShapeDtypeStruct((bsz, ttot, width), dt)
    return pl.pallas_call(
        functools.partial(_mid_kernel, nct=nct),
        out_shape=(tok(D_MODEL, F32), tok(nk, BF16), tok(nk, BF16), tok(nc, BF16), tok(2 * nk, F32),
                   tok(nd, BF16), tok(nd, BF16), tok(nd, BF16), tok(nc + nd, BF16)),
        grid=(bsz // sb, nt),
        in_specs=[src_ctx, src_lat, mod,
                  pl.BlockSpec((sb, TM, MLA_HEADS * MLA_V), lambda b, t: (b, _mla_tile(t, nct), 0)),
                  tile(nb), tile(nb), tile(nb), tile(D_MODEL),
                  _full((1, nb)), _full(w_out.shape), mod, mod, _full((1, D_MODEL))] + [_full(a.shape) for a in ws],
        out_specs=(tile(D_MODEL), tile(nk), tile(nk), tile(nc), tile(2 * nk), tile(nd), tile(nd), tile(nd),
                   tile(nc + nd)),
        compiler_params=_params("parallel", "parallel"),
        name="out0_proj1",
    )(ctx, x, gate0, a, hf, hr, mo, z0, h_norm[None], w_out.astype(BF16), scale, shift, norm_g[None], *ws)


def _gla_kernel(qf_ref, kf_ref, vf_ref, bf_ref, qr_ref, kr_ref, vr_ref, br_ref, of_ref, or_ref, s_scr):
    @pl.when(pl.program_id(1) == 0)
    def _():
        s_scr[...] = jnp.zeros_like(s_scr)

    L = CHUNK
    nchunk = TM // L
    npair = GLA_HEADS // 2
    lane = lax.broadcasted_iota(jnp.int32, (L, LANES), 1)
    rowv = lax.broadcasted_iota(jnp.int32, (L, LANES), 0)
    srow = lax.broadcasted_iota(jnp.int32, (LANES, 2 * GLA_DV), 0)
    scol = lax.broadcasted_iota(jnp.int32, (LANES, 2 * GLA_DV), 1)
    own_block = (srow < GLA_DK) == (scol < GLA_DV)
    vcol = lax.broadcasted_iota(jnp.int32, (L, 2 * GLA_DV), 1)
    e_row = lax.broadcasted_iota(jnp.int32, (LANES, LANES), 0)
    e_col = lax.broadcasted_iota(jnp.int32, (LANES, LANES), 1)
    head_ones = jnp.where((e_row < GLA_DK) == (e_col < GLA_DK), 1.0, 0.0).astype(BF16)
    leaf_idx = lax.broadcasted_iota(jnp.int32, (GLA_LEAF * GLA_LEAF, LANES), 0)
    leaf_lane = lax.broadcasted_iota(jnp.int32, (GLA_LEAF * GLA_LEAF, LANES), 1) % L
    leaf_s, leaf_t = leaf_idx // GLA_LEAF, leaf_idx % GLA_LEAF
    rep = lambda a: jnp.concatenate(
        [jnp.broadcast_to(a[s:s + 1], (GLA_LEAF, a.shape[1])) for s in range(GLA_LEAF)], axis=0)
    til = lambda a: jnp.concatenate([a] * GLA_LEAF, axis=0)

    def chunk_step(ci, carry):
        cs = []
        for bi, d in [(bi, d) for bi in range(qf_ref.shape[0]) for d in range(2)]:
            q_ref, k_ref, v_ref, b_ref, o_ref = (
                r.at[bi] for r in ((qf_ref, kf_ref, vf_ref, bf_ref, of_ref), (qr_ref, kr_ref, vr_ref, br_ref, or_ref))[d])
            cd = ci if d == 0 else nchunk - 1 - ci
            base = pl.multiple_of(cd * L, L)
            for p in range(npair):
                ls = slice(p * LANES, (p + 1) * LANES)
                bc = b_ref[pl.ds(base, L), ls]
                cs.append(dict(
                    d=d, p=p, bi=bi, base=base, o_ref=o_ref, bc=bc,
                    tau_v=rowv if d == 0 else L - 1 - rowv,
                    tau_s=(lane % L) if d == 0 else L - 1 - (lane % L),
                    q=q_ref[pl.ds(base, L), ls].astype(F32),
                    k=k_ref[pl.ds(base, L), ls].astype(F32),
                    v=v_ref[pl.ds(base, L), p * 2 * GLA_DV:(p + 1) * 2 * GLA_DV],
                    b_last=bc[L - 1:L] if d == 0 else bc[0:1],
                    state=s_scr[bi, d, p]))

        for c in cs:
            c["o_pair"] = _dot((c["q"] * jnp.exp2(c["bc"])).astype(BF16), c["state"].astype(BF16))

        for c in cs:
            c["att"] = jnp.zeros((L, LANES), F32)
        bs = L // 2
        while bs >= GLA_LEAF:
            for c in cs:
                d, bc, tau_v = c["d"], c["bc"], c["tau_v"]
                later = (tau_v // bs) % 2 == 1
                ref_b = None
                for blk in range(L // (2 * bs)):
                    tau_ref = blk * 2 * bs + bs - 1
                    idx = tau_ref if d == 0 else L - 1 - tau_ref
                    r = jnp.broadcast_to(bc[idx:idx + 1], (L, LANES))
                    ref_b = r if ref_b is None else jnp.where(tau_v // (2 * bs) == blk, r, ref_b)
                c["qs"] = jnp.where(later, c["q"] * jnp.exp2(jnp.where(later, bc - ref_b, 0.0)), 0.0).astype(BF16)
                ks = jnp.where(later, 0.0, c["k"] * jnp.exp2(jnp.where(later, 0.0, ref_b - bc)))
                c["ks2"] = jnp.concatenate([jnp.where(lane < GLA_DK, ks, 0.0), jnp.where(lane < GLA_DK, 0.0, ks)],
                                           axis=0).astype(BF16)
            for c in cs:
                same_parent = (c["tau_v"] // (2 * bs)) == (c["tau_s"] // (2 * bs))
                c["att"] = c["att"] + jnp.where(same_parent, _dot_nt(c["qs"], c["ks2"]), 0.0)
            bs //= 2

        for c in cs:
            c["leaf"] = []
        for blk in range(L // GLA_LEAF):
            rs = slice(blk * GLA_LEAF, (blk + 1) * GLA_LEAF)
            for c in cs:
                qb, bb, kb = c["q"][rs], c["bc"][rs], c["k"][rs]
                ok = (leaf_t >= leaf_s) if c["d"] == 0 else (leaf_t <= leaf_s)
                w = jnp.where(ok, til(qb) * jnp.exp2(til(bb) - rep(bb)) * rep(kb), 0.0)
                c["a"] = _dot(w.astype(BF16), head_ones)
            for c in cs:
                a = jnp.where(leaf_lane == leaf_s + blk * GLA_LEAF, c["a"], 0.0)
                acc = a[0:GLA_LEAF]
                for s in range(1, GLA_LEAF):
                    acc = acc + a[s * GLA_LEAF:(s + 1) * GLA_LEAF]
                c["leaf"].append(acc)

        for c in cs:
            v = c["v"]
            c["att"] = (c["att"] + jnp.concatenate(c["leaf"], axis=0)).astype(BF16)
            c["v_blk"] = jnp.concatenate([jnp.where(vcol < GLA_DV, v, jnp.zeros_like(v)),
                                          jnp.where(vcol < GLA_DV, jnp.zeros_like(v), v)], axis=0)
            c["ke"] = (c["k"] * jnp.exp2(c["b_last"] - c["bc"])).astype(BF16)
            c["decay_col"] = jnp.sum(
                jnp.where(e_row == e_col, jnp.broadcast_to(jnp.exp2(c["b_last"]), (LANES, LANES)), 0.0),
                axis=1, keepdims=True)
        for c in cs:
            c["o"] = c["o_pair"] + _dot(c["att"], c["v_blk"])
            c["ds"] = _dot_tn(c["ke"], c["v"])
        for c in cs:
            p = c["p"]
            c["o_ref"][pl.ds(c["base"], L), p * 2 * GLA_DV:(p + 1) * 2 * GLA_DV] = c["o"].astype(BF16)
            s_scr[c["bi"], c["d"], p] = c["decay_col"] * c["state"] + jnp.where(own_block, c["ds"], 0.0)
        return carry

    lax.fori_loop(0, nchunk, chunk_step, 0)


def _gla(gq, gk, gv, bc, nct):
    bsz, ttot, nk = gq.shape
    nc = gv.shape[2]
    nt = ttot // TM
    sb = math.gcd(bsz, GLA_BATCH)
    fwd = lambda w: pl.BlockSpec((sb, TM, w), lambda b, j: (b, j, 0))
    rev = lambda w: pl.BlockSpec((sb, TM, w), lambda b, j: (b, _scan_block(j, nct, nt), 0))
    bfwd = pl.BlockSpec((sb, TM, nk), lambda b, j: (b, j, 0))
    brev = pl.BlockSpec((sb, TM, nk), lambda b, j: (b, _scan_block(j, nct, nt), 1))
    return pl.pallas_call(
        _gla_kernel,
        out_shape=(jax.ShapeDtypeStruct((bsz, ttot, nc), BF16),) * 2,
        grid=(bsz // sb, nt),
        in_specs=[fwd(nk), fwd(nk), fwd(nc), bfwd, rev(nk), rev(nk), rev(nc), brev],
        out_specs=(fwd(nc), rev(nc)),
        scratch_shapes=[pltpu.VMEM((sb, 2, GLA_HEADS // 2, LANES, 2 * GLA_DV), F32)],
        compiler_params=_params("parallel", "arbitrary"),
        name="gla_scan",
    )(gq, gk, gv, bc, gq, gk, gv, bc)


def _na_bias_tables(rpb, rows):
    kh = NA_KH
    c = np.arange(GRID_W)
    cs = np.clip(c - NA_KW // 2, 0, GRID_W - NA_KW)
    col_ok = (c[None, :] >= cs[:, None]) & (c[None, :] < cs[:, None] + NA_KW)
    pad = GRID_W - NA_KW
    rpb_pad = jnp.pad(rpb.astype(F32) * LOG2E, ((0, 0), (0, 0), (pad, pad)))
    period = 2 * GRID_W
    flat = jnp.tile(jnp.pad(rpb_pad, ((0, 0), (0, 0), (0, 1))), (1, 1, GRID_W))[:, :, :GRID_W * (period - 1)]
    t1 = flat.reshape(NA_HEADS, 2 * NA_KH - 1, GRID_W, period - 1)[..., GRID_W - 1:]
    t1 = jnp.where(col_ok, t1, NEG_BIG)
    masked = jnp.full((NA_HEADS, GRID_W, GRID_W), NEG_BIG, F32)
    per_row = []
    for r0 in (0, NA_QROWS, rows - NA_QROWS):
        kb = int(np.clip(r0 - kh // 2, 0, rows - NA_KROWS))
        for qrow in range(r0, r0 + NA_QROWS):
            ws = int(np.clip(qrow - kh // 2, 0, rows - kh))
            per_row.append(jnp.concatenate(
                [t1[:, krow - qrow + NA_KH - 1] if ws <= krow < ws + kh else masked
                 for krow in range(kb, kb + NA_KROWS)], axis=-1))
    return jnp.concatenate(per_row, axis=1)


def _na_heads(q_ref, kw_ref, vw_ref, kc_ref, vc_ref, bias_ref):
    nq = NA_QROWS * GRID_W
    lane = lax.broadcasted_iota(jnp.int32, (nq, LANES), 1)
    def scores(j):
        ls = slice((j // 2) * LANES, (j // 2 + 1) * LANES)
        q = q_ref[:, ls]
        qj = jnp.where((lane < NA_DH) == (j % 2 == 0), q, jnp.zeros_like(q))
        return _dot_nt(qj, kw_ref[:, ls]) + bias_ref[j], _dot_nt(qj, kc_ref[:, ls])

    outs = []
    nxt = scores(0)
    for j in range(NA_HEADS):
        s_loc, s_ctx = nxt
        if j + 1 < NA_HEADS:
            nxt = scores(j + 1)
        ls = slice((j // 2) * LANES, (j // 2 + 1) * LANES)
        m = jnp.maximum(jnp.max(s_loc, axis=-1, keepdims=True), jnp.max(s_ctx, axis=-1, keepdims=True))
        p_loc = jnp.exp2(s_loc - m)
        p_ctx = jnp.exp2(s_ctx - m)
        den = jnp.sum(p_loc, axis=-1, keepdims=True) + jnp.sum(p_ctx, axis=-1, keepdims=True)
        outs.append((_dot(p_loc.astype(BF16), vw_ref[:, ls]) + _dot(p_ctx.astype(BF16), vc_ref[:, ls])) / den)
    return jnp.concatenate([jnp.where(lane < NA_DH, outs[2 * i], outs[2 * i + 1]) for i in range(NA_HEADS // 2)], axis=1)


def _na_out1_kernel(*refs, nb):
    kv = refs[:4 * nb]
    q_ref, bias_ref, h_ref, gate_ref, of_ref, or_ref, z_ref, gn_ref, w_ref, fn_ref, o_ref = refs[4 * nb:]
    rows = lambda f: jnp.concatenate([f(bi) for bi in range(nb)], axis=0)
    f32 = lambda r, bi: r[bi].astype(F32)
    na = rows(lambda bi: _na_heads(q_ref.at[bi], *kv[4 * bi:4 * bi + 4], bias_ref))
    g = rows(lambda bi: _head_rms(f32(of_ref, bi) + f32(or_ref, bi), GLA_HEADS) * gn_ref[...])
    cat = (jnp.concatenate([g, na], axis=1) * rows(lambda bi: f32(z_ref, bi))).astype(BF16)
    y = _dot(cat, w_ref[...])
    for bi in range(nb):
        h = h_ref[bi] + gate_ref[bi] * y[bi * TM:(bi + 1) * TM]
        o_ref[bi] = _rms(h) * fn_ref[...]


def _na_out1(nq, nk, nv, bias, h, gate, of, orv, z, gla_norm, w_out, final_norm, n_ctx, rows):
    bsz, ttot, nd = nq.shape
    nrb = rows // NA_QROWS
    qrows = NA_QROWS * GRID_W
    krows = NA_KROWS * GRID_W
    assert qrows == TM
    nct = n_ctx // TM
    nc = GLA_HEADS * GLA_DV
    sb = math.gcd(bsz, TILE_BATCH)

    def variant(rb):
        return jnp.where(rb == 0, 0, jnp.where(rb == nrb - 1, 2, 1))

    def key_start(rb):
        first_row = jnp.clip(rb * NA_QROWS - NA_KH // 2, 0, rows - NA_KROWS)
        return pl.multiple_of(n_ctx + first_row * GRID_W, GRID_W)

    window = lambda i: pl.BlockSpec((None, pl.Element(krows), pl.Element(nd)), lambda b, rb: (b * sb + i, key_start(rb), 0))
    context = lambda i: pl.BlockSpec((None, pl.Element(n_ctx), pl.Element(nd)), lambda b, rb: (b * sb + i, 0, 0))
    kv_specs = [spec(i) for i in range(sb) for spec in (window, window, context, context)]
    lat = lambda w: pl.BlockSpec((sb, TM, w), lambda b, rb: (b, rb + nct, 0))
    return pl.pallas_call(
        functools.partial(_na_out1_kernel, nb=sb),
        out_shape=jax.ShapeDtypeStruct((bsz, rows * GRID_W, D_MODEL), F32),
        grid=(bsz // sb, nrb),
        in_specs=kv_specs + [
            lat(nd), pl.BlockSpec((NA_HEADS, qrows, krows), lambda b, rb: (0, variant(rb), 0)),
            lat(D_MODEL), pl.BlockSpec((sb, None, 1, D_MODEL), lambda b, rb: (b, 1, 0, 0)),
            lat(nc), lat(nc), lat(nc + nd), _full((1, nc)), _full(w_out.shape), _full((1, D_MODEL))],
        out_specs=pl.BlockSpec((sb, TM, D_MODEL), lambda b, rb: (b, rb, 0)),
        compiler_params=_params("parallel", "parallel"),
        name="na_out1",
    )(*([nk, nv, nk, nv] * sb), nq, bias, h, gate, of, orv, z, gla_norm[None], w_out.astype(BF16), final_norm[None])


def kernel(x, c, ctx, c_ctx, l0_norm, l0_w_mod, l0_b_mod, l0_w_in, l0_mla_q_norm, l0_mla_w_uq, l0_mla_kv_norm, l0_mla_w_ukv, l0_mlstm_conv_w, l0_mlstm_conv_b, l0_mlstm_b_i, l0_mlstm_b_f, l0_mlstm_norm, l0_w_out, l1_norm, l1_w_mod, l1_b_mod, l1_w_in, l1_gla_w_gate, l1_gla_b_gate, l1_gla_norm, l1_na_rpb, l1_w_out, final_norm):
    bsz, seq, d = x.shape
    n_ctx = ctx.shape[1]
    rows = seq // GRID_W
    assert d == D_MODEL and seq % TM == 0 and n_ctx == TM
    assert rows % NA_QROWS == 0 and rows >= NA_KROWS and rows // NA_QROWS >= 3
    nct = n_ctx // TM

    (shift0, scale0, gate0), (shift, scale, gate) = _mod_vectors(
        c, c_ctx, ((l0_w_mod, l0_b_mod), (l1_w_mod, l1_b_mod)))

    cos_t, sin_t = _rope_tables(n_ctx, seq)
    w0 = _proj0_weights(l0_w_in, l0_mla_q_norm, l0_mla_w_uq, l0_mla_kv_norm, l0_mla_w_ukv,
                        l0_mlstm_conv_w, l0_mlstm_conv_b, l0_mlstm_b_i, l0_mlstm_b_f)
    q, k, v, mq, mk, mv, g_rows, g_cols, mo, z0 = _proj0(ctx, x, scale0, shift0, l0_norm, cos_t, sin_t, w0, nct)
    a = _mla_attention(q, k, v, nct)
    hf, hr = _mlstm(mq, mk, mv, g_rows, g_cols, nct)

    w1 = _proj1_weights(l1_w_in, l1_gla_w_gate, l1_gla_b_gate)
    h, gq, gk, gv, bc, nq, nk, nv, z = _mid(ctx, x, gate0, a, hf, hr, mo, z0, l0_mlstm_norm, l0_w_out,
                                            scale, shift, l1_norm, w1, nct)
    of, orv = _gla(gq, gk, gv, bc, nct)
    return _na_out1(nq, nk, nv, _na_bias_tables(l1_na_rpb, rows), h, gate, of, orv, z,
                    l1_gla_norm, l1_w_out, final_norm, n_ctx, rows)
```

```python
import functools
import math

import jax
import jax.numpy as jnp
import numpy as np
from jax import lax
from jax.experimental import pallas as pl
from jax.experimental.pallas import tpu as pltpu

F32 = jnp.float32
BF16 = jnp.bfloat16

D_MODEL = 1024
GRID_W = 64
EPS = 1e-6
ROPE_BASE = 10000.0

MLA_HEADS = 8
MLA_Q_RANK = 384
MLA_KV_RANK = 256
MLA_NOPE = 64
MLA_ROPE = 32
MLA_V = 64
ROPE_PAIRS = MLA_ROPE // 4
MLSTM_HEADS = 4
MLSTM_DH = 128
MLSTM_CONV = 3
GLA_HEADS = 4
GLA_DK = 64
GLA_DV = 128
GLA_GATE_RANK = 16
GLA_TAU = 16.0
NA_HEADS = 8
NA_DH = 64
NA_KH = 8
NA_KW = 16

CHUNK = 64
MLSTM_CHUNK = 128
TM = 256
LANES = 128
SUBLANES = 8
HALO = SUBLANES
NA_QROWS = 4
NA_KROWS = NA_QROWS + NA_KH - 1
GLA_LEAF = 8
MLA_HPS = 4
MLA_QTILES = 2
TILE_BATCH = 2
MLSTM_BATCH = 2
GLA_BATCH = 8
VMEM_LIMIT = 56 * 1024 * 1024
NEG_BIG = -1e30
LOG2E = 1.4426950408889634


def _dot(a, b):
    return jnp.dot(a, b, preferred_element_type=F32)


def _dot_nt(a, b):
    return lax.dot_general(a, b, (((1,), (1,)), ((), ())), preferred_element_type=F32)


def _dot_tn(a, b):
    return lax.dot_general(a, b, (((0,), (0,)), ((), ())), preferred_element_type=F32)


def _rms(x):
    return x * lax.rsqrt(jnp.mean(x * x, axis=-1, keepdims=True) + EPS)


def _sigmoid(x):
    return 1.0 / (1.0 + jnp.exp(-x))


def _silu(x):
    return x * _sigmoid(x)


def _log_sigmoid(x):
    return jnp.minimum(x, 0.0) - jnp.log(1.0 + jnp.exp(-jnp.abs(x)))


def _params(*sem):
    return pltpu.CompilerParams(dimension_semantics=sem, vmem_limit_bytes=VMEM_LIMIT)


def _full(shape):
    nd = len(shape)
    return pl.BlockSpec(shape, lambda *_: (0,) * nd)


def _mod_kernel(c_ref, w0_ref, b0_ref, w1_ref, b1_ref, o0_ref, o1_ref):
    sc = _silu(c_ref[...]).astype(BF16)
    o0_ref[...] = _dot(sc, w0_ref[...].astype(BF16)) + b0_ref[...]
    o1_ref[...] = _dot(sc, w1_ref[...].astype(BF16)) + b1_ref[...]


def _mod_vectors(c, c_ctx, mods):
    bsz, d = c.shape
    rows = -(-(bsz + 1) // SUBLANES) * SUBLANES
    cc = jnp.zeros((rows, d), F32).at[:bsz].set(c).at[bsz].set(c_ctx)
    (w0, b0), (w1, b1) = mods
    n = w0.shape[1]
    wspec = pl.BlockSpec((d, d), lambda j: (0, j))
    vspec = pl.BlockSpec((1, d), lambda j: (0, j))
    ospec = pl.BlockSpec((rows, d), lambda j: (0, j))
    per_layer = pl.pallas_call(
        _mod_kernel,
        out_shape=(jax.ShapeDtypeStruct((rows, n), F32),) * 2,
        grid=(n // d,),
        in_specs=[_full((rows, d)), wspec, vspec, wspec, vspec],
        out_specs=(ospec, ospec),
        compiler_params=_params("parallel"),
        name="modulation",
    )(cc, w0, b0[None], w1, b1[None])
    out = []
    for mod in per_layer:
        vecs = []
        for part in jnp.split(mod, 3, axis=-1):
            ctx_v = jnp.broadcast_to(part[bsz][None], (bsz, d))
            vecs.append(jnp.stack([ctx_v, part[:bsz]], axis=1)[:, :, None, :])
        out.append(vecs)
    return out


def _modnorm(x, g, scale, shift):
    return _rms(x) * g * (1.0 + scale) + shift


def _seg_scan(x, reverse, op, fill):
    width = x.shape[1]
    pos = lax.broadcasted_iota(jnp.int32, x.shape, 1) % MLSTM_CHUNK
    k = 1
    while k < MLSTM_CHUNK:
        if reverse:
            x = op(x, jnp.where(pos < MLSTM_CHUNK - k, pltpu.roll(x, width - k, axis=1), fill))
        else:
            x = op(x, jnp.where(pos >= k, pltpu.roll(x, k, axis=1), fill))
        k *= 2
    return x


def _split3(x):
    hi = x.astype(BF16)
    r1 = x - hi.astype(F32)
    mid = r1.astype(BF16)
    return hi, mid, (r1 - mid.astype(F32)).astype(BF16)


def _proj0_kernel(ctx_ref, lat_ref, hp_ref, hn_ref, sc_ref, sh_ref, g_ref, cos_ref, sin_ref,
                  wall_ref, wgt_ref, qn_ref, wuq_ref, kvn_ref, wk_ref, wv_ref, cw_ref, cb_ref, gb_ref,
                  q_out, k_out, v_out, mq_out, mk_out, mv_out, gr_out, gc_out, mo_out, z_out,
                  pbuf, *, nct, nt):
    t = pl.program_id(1)
    h = jnp.where(t < nct, ctx_ref[...], lat_ref[...])
    x = jnp.concatenate([hp_ref[...], h, hn_ref[...]], axis=0)
    ub_ext = _modnorm(x, g_ref[...], sc_ref[...], sh_ref[...]).astype(BF16)
    ub = ub_ext[HALO:HALO + TM]

    gt = _dot_nt(wgt_ref[...], ub) + gb_ref[...]
    half = MLSTM_HEADS
    per_dir = []
    for d in range(2):
        gd = gt[d * SUBLANES:(d + 1) * SUBLANES]
        bcum = _seg_scan(_log_sigmoid(gd), d == 1, jnp.add, 0.0)
        b_top = pltpu.roll(bcum, half, axis=0)
        c8 = gd - b_top
        pm8 = _seg_scan(c8, d == 1, jnp.maximum, -jnp.inf)
        per_dir.append((c8, pm8, b_top))

    cos = cos_ref[...]
    sin = sin_ref[...]
    cos_all = jnp.concatenate([cos] * MLA_HEADS, axis=1)
    sin_all = jnp.concatenate([sin] * MLA_HEADS, axis=1)

    def swap_halves(a):
        lane = lax.broadcasted_iota(jnp.int32, a.shape, 1)
        first = lane % (2 * ROPE_PAIRS) < ROPE_PAIRS
        return jnp.where(first, pltpu.roll(a, a.shape[1] - ROPE_PAIRS, axis=1), pltpu.roll(a, ROPE_PAIRS, axis=1))

    wseg = lambda name: wall_ref[:, PROJ0_COLS[name][0]:PROJ0_COLS[name][1]]
    cqkr = _dot(ub, wseg("cqkr"))
    cq = cqkr[:, :MLA_Q_RANK]
    kr = cqkr[:, MLA_Q_RANK:]
    ckv = _dot(ub, wseg("ckv"))
    pqk = _dot(ub_ext, wseg("mqk"))

    nq = (_rms(cq) * qn_ref[...]).astype(BF16)
    nkv = (_rms(ckv) * kvn_ref[...]).astype(BF16)

    q = _dot(nq, wuq_ref[...])
    kn = _dot(nkv, wk_ref[...])
    vv = _dot(nkv, wv_ref[...])

    prev_ok = t > nct
    next_ok = jnp.logical_and(t >= nct, t != nt - 1)
    row = lax.broadcasted_iota(jnp.int32, pqk.shape, 0)
    keep = jnp.logical_and(jnp.logical_or(row >= HALO, prev_ok), jnp.logical_or(row < HALO + TM, next_ok))
    pbuf[...] = jnp.where(keep, pqk, 0.0)
    cw = cw_ref[...]
    y = (pbuf[HALO - 1:HALO - 1 + TM, :] * cw[0:1] + pbuf[HALO:HALO + TM, :] * cw[1:2]
         + pbuf[HALO + 1:HALO + 1 + TM, :] * cw[2:3] + cb_ref[...])
    y = _silu(y)
    nb = MLSTM_HEADS * MLSTM_DH
    mq_out[...] = (y[:, :nb] * MLSTM_DH ** -0.5).astype(BF16)
    mk_out[...] = y[:, nb:].astype(BF16)

    mv = _dot(ub, wseg("mv"))
    mo = _dot(ub, wseg("mo"))

    q = q * cos_all + swap_halves(q) * sin_all
    q_out[...] = (q * ((MLA_NOPE + MLA_ROPE) ** -0.5 * LOG2E)).astype(BF16)
    rot = kr * cos + swap_halves(kr) * sin
    k_out[...] = (kn + jnp.concatenate([rot] * MLA_HEADS, axis=1)).astype(BF16)
    vlane = lax.broadcasted_iota(jnp.int32, (TM, MLA_HEADS * LANES), 1)
    is_value = ((vlane % LANES) < MLA_V) == ((vlane // LANES) % 2 == 0)
    v_out[...] = jnp.where(is_value, vv, 1.0).astype(BF16)

    zz = _dot(ub, wseg("z"))
    mv_out[...] = mv.astype(BF16)
    mo_out[...] = _sigmoid(mo).astype(BF16)
    z_out[...] = _silu(zz).astype(BF16)

    r8 =lax.broadcasted_iota(jnp.int32, (SUBLANES, TM), 0)
    both = lambda i: jnp.where(r8 < half, per_dir[0][i], pltpu.roll(per_dir[1][i], half, axis=0)) * LOG2E
    c_rows, pm_rows, b_rows = both(0), both(1), both(2)
    for ci in range(TM // MLSTM_CHUNK):
        gr_out[ci] = c_rows[:, ci * MLSTM_CHUNK:(ci + 1) * MLSTM_CHUNK]
    stack = jnp.concatenate([c_rows, pm_rows, b_rows, jnp.zeros((LANES - 3 * SUBLANES, TM), F32)], axis=0)
    er = lax.broadcasted_iota(jnp.int32, (TM, TM), 0)
    ec = lax.broadcasted_iota(jnp.int32, (TM, TM), 1)
    eye = jnp.where(er == ec, 1.0, 0.0).astype(BF16)
    gc_out[...] = sum(_dot_nt(eye, part) for part in _split3(stack))


def _rope_tables(n_ctx, seq):
    t = jnp.arange(seq)
    inv = 1.0 / (ROPE_BASE ** (jnp.arange(ROPE_PAIRS, dtype=F32) / ROPE_PAIRS))
    ang = jnp.concatenate([(t // GRID_W)[:, None] * inv, (t % GRID_W)[:, None] * inv], axis=-1)
    cos, sin = jnp.cos(ang), jnp.sin(ang)
    j = np.arange(MLA_ROPE)
    src = (j // (2 * ROPE_PAIRS)) * ROPE_PAIRS + (j % ROPE_PAIRS)
    sign = np.where((j % (2 * ROPE_PAIRS)) < ROPE_PAIRS, -1.0, 1.0).astype(np.float32)
    cos_full = cos[:, src]
    sin_full = sin[:, src] * sign
    cos_t = jnp.ones((n_ctx + seq, LANES), F32).at[n_ctx:, MLA_NOPE:MLA_NOPE + MLA_ROPE].set(cos_full)
    sin_t = jnp.zeros((n_ctx + seq, LANES), F32).at[n_ctx:, MLA_NOPE:MLA_NOPE + MLA_ROPE].set(sin_full)
    return cos_t, sin_t


def _relayout_kernel(w_ref, o_ref, *, moves):
    end = 0
    for src, width, dst in moves:
        if dst > end:
            o_ref[:, end:dst] = jnp.zeros((o_ref.shape[0], dst - end), o_ref.dtype)
        o_ref[:, dst:dst + width] = w_ref[:, src:src + width].astype(o_ref.dtype)
        end = dst + width
    if end < o_ref.shape[1]:
        o_ref[:, end:] = jnp.zeros((o_ref.shape[0], o_ref.shape[1] - end), o_ref.dtype)


def _relayout(w, moves, width):
    rows = w.shape[0]
    rb = math.gcd(rows, TM)
    return pl.pallas_call(
        functools.partial(_relayout_kernel, moves=tuple(moves)),
        out_shape=jax.ShapeDtypeStruct((rows, width), BF16),
        grid=(rows // rb,),
        in_specs=[pl.BlockSpec((rb, w.shape[1]), lambda i: (i, 0))],
        out_specs=pl.BlockSpec((rb, width), lambda i: (i, 0)),
        compiler_params=_params("parallel"),
        name="weight_relayout",
    )(w)


def _segments(widths):
    out, start = {}, 0
    for name, width in widths:
        out[name] = (start, start + width)
        start += width
    return out


PROJ0_COLS = _segments([("cqkr", MLA_Q_RANK + LANES), ("ckv", MLA_KV_RANK), ("mqk", 2 * MLSTM_HEADS * MLSTM_DH),
                        ("mv", MLSTM_HEADS * MLSTM_DH), ("mo", MLSTM_HEADS * MLSTM_DH), ("z", D_MODEL)])
PROJ1_COLS = _segments([("q", GLA_HEADS * GLA_DK), ("k", GLA_HEADS * GLA_DK), ("v", GLA_HEADS * GLA_DV),
                        ("ga", LANES), ("nq", NA_HEADS * NA_DH), ("nk", NA_HEADS * NA_DH), ("nv", NA_HEADS * NA_DH),
                        ("z", GLA_HEADS * GLA_DV + NA_HEADS * NA_DH)])


def _proj0_weights(w_in, q_norm, w_uq, kv_norm, w_ukv, conv_w, conv_b, b_i, b_f):
    o = np.cumsum([0, MLA_Q_RANK, MLA_KV_RANK, MLA_ROPE] + [MLSTM_HEADS * MLSTM_DH] * 4 + [4 * MLSTM_HEADS, D_MODEL])
    dq = MLA_NOPE + MLA_ROPE
    wuq = jnp.pad(w_uq.reshape(MLA_Q_RANK, MLA_HEADS, dq), ((0, 0), (0, 0), (0, LANES - dq)))
    wkv = w_ukv.reshape(MLA_KV_RANK, MLA_HEADS, MLA_NOPE + MLA_V)
    wk = jnp.pad(wkv[:, :, :MLA_NOPE], ((0, 0), (0, 0), (0, LANES - MLA_NOPE)))
    zv = jnp.zeros((MLA_KV_RANK, MLA_HEADS // 2, MLA_V), F32)
    wv = jnp.concatenate([wkv[:, 0::2, MLA_NOPE:], zv, zv, wkv[:, 1::2, MLA_NOPE:]], axis=-1)
    wv = wv.reshape(MLA_KV_RANK, MLA_HEADS * LANES)
    gbias = jnp.stack([b_i, b_f], axis=1).reshape(4 * MLSTM_HEADS, 1)
    bf = lambda a: a.astype(BF16)
    c = PROJ0_COLS
    wall = _relayout(w_in, [(o[0], MLA_Q_RANK, c["cqkr"][0]), (o[2], MLA_ROPE, c["cqkr"][0] + MLA_Q_RANK + MLA_NOPE),
                            (o[1], MLA_KV_RANK, c["ckv"][0]), (o[3], o[7] - o[3], c["mqk"][0]),
                            (o[8], D_MODEL, c["z"][0])], c["z"][1])
    return dict(
        wall=wall, wgt=bf(w_in[:, o[7]:o[8]].T),
        qn=q_norm[None], wuq=bf(wuq.reshape(MLA_Q_RANK, -1)),
        kvn=kv_norm[None], wk=bf(wk.reshape(MLA_KV_RANK, -1)), wv=bf(wv),
        cw=jnp.zeros((SUBLANES, conv_w.shape[1]), F32).at[:MLSTM_CONV].set(conv_w), cb=conv_b[None], gb=gbias)


def _tile_specs(nct, nt, nb):
    tile = lambda w: pl.BlockSpec((nb, TM, w), lambda b, t: (b, t, 0))
    per = TM // HALO
    nlat = nt - nct
    src_ctx = pl.BlockSpec((nb, TM, D_MODEL), lambda b, t: (b, jnp.minimum(t, nct - 1), 0))
    src_lat = pl.BlockSpec((nb, TM, D_MODEL), lambda b, t: (b, jnp.maximum(t - nct, 0), 0))
    prev = pl.BlockSpec((nb, HALO, D_MODEL), lambda b, t: (b, jnp.maximum((t - nct) * per - 1, 0), 0))
    nxt = pl.BlockSpec((nb, HALO, D_MODEL),
                       lambda b, t: (b, jnp.clip((t - nct + 1) * per, 0, nlat * per - 1), 0))
    mod = pl.BlockSpec((nb, None, 1, D_MODEL), lambda b, t: (b, (t >= nct).astype(jnp.int32), 0, 0))
    return tile, src_ctx, src_lat, prev, nxt, mod


def _per_element(tile_kernel, shared):
    def kernel(*refs, **kw):
        for bi in range(refs[0].shape[0]):
            tile_kernel(*[r if i in shared else r.at[bi] for i, r in enumerate(refs)], **kw)
    return kernel


def _proj0(ctx, x, scale, shift, norm_g, cos_t, sin_t, w, nct):
    bsz = x.shape[0]
    ttot = ctx.shape[1] + x.shape[1]
    nt = ttot // TM
    nb = MLSTM_HEADS * MLSTM_DH
    sb = math.gcd(bsz, TILE_BATCH)
    tile, src_ctx, src_lat, prev, nxt, mod = _tile_specs(nct, nt, sb)
    tab = pl.BlockSpec((TM, LANES), lambda b, t: (t, 0))
    wnames = ["wall", "wgt", "qn", "wuq", "kvn", "wk", "wv", "cw", "cb", "gb"]
    ws = [w[n] for n in wnames]
    tok = lambda width, dt: jax.ShapeDtypeStruct((bsz, ttot, width), dt)
    out_shape = (tok(MLA_HEADS * LANES, BF16), tok(MLA_HEADS * LANES, BF16), tok(MLA_HEADS * LANES, BF16),
                 tok(nb, BF16), tok(nb, BF16), tok(nb, BF16),
                 jax.ShapeDtypeStruct((bsz, ttot // MLSTM_CHUNK, 2 * MLSTM_HEADS, MLSTM_CHUNK), F32), tok(LANES, F32),
                 tok(nb, BF16), tok(D_MODEL, BF16))
    out_specs = (tile(MLA_HEADS * LANES), tile(MLA_HEADS * LANES), tile(MLA_HEADS * LANES),
                 tile(nb), tile(nb), tile(nb),
                 pl.BlockSpec((sb, TM // MLSTM_CHUNK, 2 * MLSTM_HEADS, MLSTM_CHUNK), lambda b, t: (b, t, 0, 0)),
                 tile(LANES),
                 tile(nb), tile(D_MODEL))
    n_blocked, n_out = 6, len(out_shape)
    n_in = n_blocked + 3 + len(ws)
    shared = set(range(n_blocked, n_in)) | {n_in + n_out}
    return pl.pallas_call(
        functools.partial(_per_element(_proj0_kernel, shared), nct=nct, nt=nt),
        out_shape=out_shape,
        grid=(bsz // sb, nt),
        in_specs=[src_ctx, src_lat, prev, nxt, mod, mod, _full((1, D_MODEL)), tab, tab] + [_full(a.shape) for a in ws],
        out_specs=out_specs,
        scratch_shapes=[pltpu.VMEM((TM + 2 * HALO, 2 * nb), F32)],
        compiler_params=_params("parallel", "parallel"),
        name="proj0",
    )(ctx, x, x, x, scale, shift, norm_g[None], cos_t, sin_t, *ws)


def _mla_kernel(qa_ref, qb_ref, k_ref, v_ref, o_ref, *, nkeys_ctx):
    step = pl.program_id(2)

    def attend(queries, nkeys, rows):
        qk = lambda j: _dot_nt(queries(j), k_ref[0:nkeys, j * LANES:(j + 1) * LANES])
        outs = []
        s_next = qk(0)
        for j in range(MLA_HPS):
            s = s_next
            if j + 1 < MLA_HPS:
                s_next = qk(j + 1)
            p = jnp.exp2(s - jnp.max(s, axis=-1, keepdims=True))
            o = _dot(p.astype(BF16), v_ref[0:nkeys, j * LANES:(j + 1) * LANES])
            outs.append(o / pltpu.roll(o, MLA_V, axis=1))
        lane = lax.broadcasted_iota(jnp.int32, outs[0].shape, 1)
        for j in range(MLA_HPS // 2):
            o_ref[rows, j * LANES:(j + 1) * LANES] = jnp.where(lane < MLA_V, outs[2 * j], outs[2 * j + 1]).astype(BF16)

    @pl.when(step == 0)
    def _():
        attend(lambda j: qa_ref[:, j * LANES:(j + 1) * LANES], nkeys_ctx, slice(0, TM))
        o_ref[TM:, :] = jnp.zeros((o_ref.shape[0] - TM, o_ref.shape[1]), BF16)

    @pl.when(step > 0)
    def _():
        stacked = lambda j: jnp.concatenate([qa_ref[:, j * LANES:(j + 1) * LANES],
                                             qb_ref[:, j * LANES:(j + 1) * LANES]], axis=0)
        attend(stacked, k_ref.shape[0], slice(None))


def _mla_attention(q, k, v, nct):
    bsz, ttot, _ = q.shape
    nt = ttot // TM
    assert nct == 1 and MLA_QTILES == 2 and (nt - nct) % MLA_QTILES == 0
    hw = MLA_HPS * LANES
    nsteps = 1 + (nt - nct) // MLA_QTILES
    return pl.pallas_call(
        functools.partial(_mla_kernel, nkeys_ctx=nct * TM),
        out_shape=jax.ShapeDtypeStruct((bsz, nsteps * MLA_QTILES * TM, MLA_HEADS * MLA_V), BF16),
        grid=(bsz, MLA_HEADS // MLA_HPS, nsteps),
        in_specs=[pl.BlockSpec((None, TM, hw), lambda b, p, s: (b, jnp.maximum(2 * s - 1, 0), p)),
                  pl.BlockSpec((None, TM, hw), lambda b, p, s: (b, jnp.maximum(2 * s, 1), p)),
                  pl.BlockSpec((None, ttot, hw), lambda b, p, s: (b, 0, p)),
                  pl.BlockSpec((None, ttot, hw), lambda b, p, s: (b, 0, p))],
        out_specs=pl.BlockSpec((None, MLA_QTILES * TM, hw // 2), lambda b, p, s: (b, s, p)),
        compiler_params=_params("parallel", "parallel", "parallel"),
        name="mla_attention",
    )(q, q, k, v)


def _mla_tile(t, nct):
    return t + (t >= nct).astype(jnp.int32)


def _scan_block(j, nct, nt):
    return jnp.where(j < nct, nct - 1 - j, nt - 1 - (j - nct))


def _mlstm_kernel(qf_ref, kf_ref, vf_ref, grf_ref, gcf_ref, qr_ref, kr_ref, vr_ref, grr_ref, gcr_ref,
                  hf_ref, hr_ref, c_scr, m_scr):
    @pl.when(pl.program_id(1) == 0)
    def _():
        c_scr[...] = jnp.zeros_like(c_scr)
        m_scr[...] = jnp.zeros_like(m_scr)

    L = MLSTM_CHUNK
    nchunk = TM // L
    nh = MLSTM_HEADS
    nchain = 2 * nh
    row = lax.broadcasted_iota(jnp.int32, (L, L), 0)
    col = lax.broadcasted_iota(jnp.int32, (L, L), 1)
    ones = jnp.ones((L, MLSTM_DH), BF16)
    wide = lambda a: jnp.broadcast_to(a, (L, MLSTM_DH))
    twice = lambda a: jnp.concatenate([a, a], axis=1)
    refs = ((qf_ref, kf_ref, vf_ref, grf_ref, gcf_ref, hf_ref), (qr_ref, kr_ref, vr_ref, grr_ref, gcr_ref, hr_ref))
    nbatch = qf_ref.shape[0]

    def load(ci):
        cs = []
        for bi, d in [(bi, d) for bi in range(nbatch) for d in range(2)]:
            q_ref, k_ref, v_ref, gr_ref, gc_ref, o_ref = (r.at[bi] for r in refs[d])
            cd = ci if d == 0 else nchunk - 1 - ci
            rs = slice(cd * L, (cd + 1) * L)
            g_rows = gr_ref[cd]
            g_cols = gc_ref[rs, :]
            last = L - 1 if d == 0 else 0
            for hd in range(nh):
                ch = d * nh + hd
                hs = slice(hd * MLSTM_DH, (hd + 1) * MLSTM_DH)
                cs.append(dict(
                    rs=rs, hs=hs, o_ref=o_ref, causal=(col <= row) if d == 0 else (col >= row),
                    c_row=g_rows[ch:ch + 1], c_col=g_cols[:, ch:ch + 1],
                    pm_col=g_cols[:, nchain + ch:nchain + ch + 1], b_col=g_cols[:, 2 * nchain + ch:2 * nchain + ch + 1],
                    pm_last=g_cols[last:last + 1, nchain + ch:nchain + ch + 1],
                    b_last=g_cols[last:last + 1, 2 * nchain + ch:2 * nchain + ch + 1],
                    q=q_ref[rs, hs], k=k_ref[rs, hs],
                    v_ext=jnp.concatenate([v_ref[rs, hs], ones], axis=1)))
        return cs

    def prep_scores(cs):
        for c in cs:
            c["s"] = _dot_nt(c["q"], c["k"])

    def prep_weights(cs):
        for c in cs:
            c["pm_w"] = wide(c["pm_col"])
            c["b_w"] = wide(c["b_col"])
            c["w"] = jnp.exp2(jnp.where(c["causal"], c["c_row"] - c["pm_w"][:, :L], -jnp.inf))
            c["kwf"] = jnp.exp2(wide(c["c_col"] - c["pm_last"]))

    def prep_operands(cs):
        for c in cs:
            c["p"] = (c["s"] * c["w"]).astype(BF16)
            c["kw"] = (c["kwf"] * c["k"].astype(F32)).astype(BF16)

    def prep_products(cs):
        for c in cs:
            c["o1"] = _dot(c["p"], c["v_ext"])
            c["dc"] = _dot_tn(c["kw"], c["v_ext"])

    def scan_read(cs):
        for ch, c in enumerate(cs):
            c["qc"] = _dot(c["q"], state[ch].astype(BF16))

    def scan_factors(cs):
        for ch, c in enumerate(cs):
            g_w = jnp.maximum(m[ch], c["pm_w"])
            c["e_intra"] = jnp.exp2(c["pm_w"] - g_w)
            c["e_inter"] = jnp.exp2(m[ch] - g_w)
            c["floor"] = jnp.exp2(-(c["b_w"] + g_w))
            g_last = jnp.maximum(m[ch], c["pm_last"])
            c["keep"] = jnp.exp2(m[ch] - g_last)
            c["gain"] = jnp.exp2(c["pm_last"] - g_last)
            c["m_new"] = c["b_last"] + g_last

    def scan_update(cs):
        for ch, c in enumerate(cs):
            o = twice(c["e_intra"]) * c["o1"] + twice(c["e_inter"]) * c["qc"]
            h = o[:, :MLSTM_DH] / jnp.maximum(jnp.abs(o[:, MLSTM_DH:]), c["floor"])
            c["o_ref"][c["rs"], c["hs"]] = h.astype(BF16)
            state[ch] = c["keep"] * state[ch] + c["gain"] * c["dc"]
            m[ch] = c["m_new"]

    nstate = nbatch * nchain
    state = [c_scr[ch] for ch in range(nstate)]
    m = [m_scr[ch, 0:1, 0:1] for ch in range(nstate)]
    chunks = [load(ci) for ci in range(nchunk)]
    everything = [c for cs in chunks for c in cs]
    for stage in (prep_scores, prep_weights, prep_operands):
        stage(everything)
    prep_products(chunks[0])
    for i, cs in enumerate(chunks):
        scan_read(cs)
        if i + 1 < nchunk:
            prep_products(chunks[i + 1])
        scan_factors(cs)
        scan_update(cs)
    for ch in range(nstate):
        c_scr[ch] = state[ch]
        m_scr[ch] = jnp.broadcast_to(m[ch], m_scr.shape[1:])


def _mlstm(mq, mk, mv, g_rows, g_cols, nct):
    bsz, ttot, nb = mq.shape
    nt = ttot // TM
    per = TM // MLSTM_CHUNK
    sb = math.gcd(bsz, MLSTM_BATCH)
    fwd = lambda w: pl.BlockSpec((sb, TM, w), lambda b, j: (b, j, 0))
    rev = lambda w: pl.BlockSpec((sb, TM, w), lambda b, j: (b, _scan_block(j, nct, nt), 0))
    gshape = (sb, per, 2 * MLSTM_HEADS, MLSTM_CHUNK)
    gfwd = pl.BlockSpec(gshape, lambda b, j: (b, j, 0, 0))
    grev = pl.BlockSpec(gshape, lambda b, j: (b, _scan_block(j, nct, nt), 0, 0))
    nstate = sb * 2 * MLSTM_HEADS
    return pl.pallas_call(
        _mlstm_kernel,
        out_shape=(jax.ShapeDtypeStruct((bsz, ttot, nb), BF16),) * 2,
        grid=(bsz // sb, nt),
        in_specs=[fwd(nb), fwd(nb), fwd(nb), gfwd, fwd(LANES), rev(nb), rev(nb), rev(nb), grev, rev(LANES)],
        out_specs=(fwd(nb), rev(nb)),
        scratch_shapes=[pltpu.VMEM((nstate, MLSTM_DH, 2 * MLSTM_DH), F32),
                        pltpu.VMEM((nstate, SUBLANES, LANES), F32)],
        compiler_params=_params("parallel", "arbitrary"),
        name="mlstm_scan",
    )(mq, mk, mv, g_rows, g_cols, mq, mk, mv, g_rows, g_cols)


def _head_rms(x, n_heads):
    w = x.shape[1] // n_heads
    return jnp.concatenate([_rms(x[:, i * w:(i + 1) * w]) for i in range(n_heads)], axis=1)


def _mid_kernel(ctx_ref, lat_ref, gate_ref, a_ref, hf_ref, hr_ref, mo_ref, z0_ref, hn_ref, wout_ref,
                sc_ref, sh_ref, g_ref, wall_ref, wgate_ref, bgate_ref,
                h_out, gq_out, gk_out, gv_out, bc_out, nq_out, nk_out, nv_out, z_out, *, nct):
    nbt = ctx_ref.shape[0]
    rows = lambda f: jnp.concatenate([f(bi) for bi in range(nbt)], axis=0)
    f32 = lambda r, bi: r[bi].astype(F32)

    def put(ref, val):
        for bi in range(nbt):
            ref[bi] = val[bi * TM:(bi + 1) * TM]

    hm = rows(lambda bi: _head_rms(f32(mo_ref, bi) * (f32(hf_ref, bi) + f32(hr_ref, bi)), MLSTM_HEADS) * hn_ref[...])
    cat = (jnp.concatenate([rows(lambda bi: f32(a_ref, bi)), hm], axis=1) * rows(lambda bi: f32(z0_ref, bi))).astype(BF16)
    y = _dot(cat, wout_ref[...])
    is_ctx = pl.program_id(1) < nct
    hs = [jnp.where(is_ctx, ctx_ref[bi], lat_ref[bi]) + gate_ref[bi] * y[bi * TM:(bi + 1) * TM] for bi in range(nbt)]
    for bi in range(nbt):
        h_out[bi] = hs[bi]

    ub = rows(lambda bi: _modnorm(hs[bi], g_ref[...], sc_ref[bi], sh_ref[bi])).astype(BF16)
    wseg = lambda name: wall_ref[:, PROJ1_COLS[name][0]:PROJ1_COLS[name][1]]
    ga = _dot(ub, wseg("ga")).astype(BF16)
    gate_pre = _dot(ga, wgate_ref[...])
    put(gq_out, (_dot(ub, wseg("q")) * GLA_DK ** -0.5).astype(BF16))
    put(gk_out, _dot(ub, wseg("k")).astype(BF16))
    put(gv_out, _dot(ub, wseg("v")).astype(BF16))

    lg = _log_sigmoid(gate_pre + bgate_ref[...]) * (1.0 / GLA_TAU)
    hi, mid, lo = _split3(lg)
    put(nq_out, (_dot(ub, wseg("nq")) * (NA_DH ** -0.5 * LOG2E)).astype(BF16))
    put(nk_out, _dot(ub, wseg("nk")).astype(BF16))
    put(nv_out, _dot(ub, wseg("nv")).astype(BF16))
    row = lax.broadcasted_iota(jnp.int32, (TM, TM), 0)
    col = lax.broadcasted_iota(jnp.int32, (TM, TM), 1)
    same = (row // CHUNK) == (col // CHUNK)
    nk = GLA_HEADS * GLA_DK
    for d in range(2):
        tri = jnp.where(jnp.logical_and(same, (col <= row) if d == 0 else (col >= row)), 1.0, 0.0).astype(BF16)
        cols = slice(d * nk, (d + 1) * nk)
        for bi in range(nbt):
            rs = slice(bi * TM, (bi + 1) * TM)
            bc_out[bi, :, cols] = (_dot(tri, hi[rs, cols]) + _dot(tri, mid[rs, cols]) + _dot(tri, lo[rs, cols])) * LOG2E
    put(z_out, _silu(_dot(ub, wseg("z"))).astype(BF16))


def _proj1_weights(w_in, w_gate, b_gate):
    nk = GLA_HEADS * GLA_DK
    nc = GLA_HEADS * GLA_DV
    nd = NA_HEADS * NA_DH
    o = np.cumsum([0, nk, nk, nc, 2 * GLA_GATE_RANK, nd, nd, nd, nc + nd])
    r = GLA_GATE_RANK
    c = PROJ1_COLS
    wall = _relayout(w_in, [(0, o[4], 0), (o[4], o[8] - o[4], c["nq"][0])], c["z"][1])
    wgate = jnp.zeros((LANES, 2 * nk), F32).at[:r, :nk].set(w_gate[0]).at[r:2 * r, nk:].set(w_gate[1])
    return dict(wall=wall, wgate=wgate.astype(BF16), bgate=b_gate.reshape(1, 2 * nk))


def _mid(ctx, x, gate0, a, hf, hr, mo, z0, h_norm, w_out, scale, shift, norm_g, w, nct):
    bsz = x.shape[0]
    ttot = ctx.shape[1] + x.shape[1]
    nt = ttot // TM
    sb = math.gcd(bsz, TILE_BATCH)
    tile, src_ctx, src_lat, _, _, mod = _tile_specs(nct, nt, sb)
    nb = MLSTM_HEADS * MLSTM_DH
    nk = GLA_HEADS * GLA_DK
    nc = GLA_HEADS * GLA_DV
    nd = NA_HEADS * NA_DH
    ws = [w[n] for n in ("wall", "wgate", "bgate")]
    tok = lambda width, dt: jax.ShapeDtypeStruct((bsz, ttot, width), dt)
    return pl.pallas_call(
        functools.partial(_mid_kernel, nct=nct),
        out_shape=(tok(D_MODEL, F32), tok(nk, BF16), tok(nk, BF16), tok(nc, BF16), tok(2 * nk, F32),
                   tok(nd, BF16), tok(nd, BF16), tok(nd, BF16), tok(nc + nd, BF16)),
        grid=(bsz // sb, nt),
        in_specs=[src_ctx, src_lat, mod,
                  pl.BlockSpec((sb, TM, MLA_HEADS * MLA_V), lambda b, t: (b, _mla_tile(t, nct), 0)),
                  tile(nb), tile(nb), tile(nb), tile(D_MODEL),
                  _full((1, nb)), _full(w_out.shape), mod, mod, _full((1, D_MODEL))] + [_full(a.shape) for a in ws],
        out_specs=(tile(D_MODEL), tile(nk), tile(nk), tile(nc), tile(2 * nk), tile(nd), tile(nd), tile(nd),
                   tile(nc + nd)),
        compiler_params=_params("parallel", "parallel"),
        name="out0_proj1",
    )(ctx, x, gate0, a, hf, hr, mo, z0, h_norm[None], w_out.astype(BF16), scale, shift, norm_g[None], *ws)


def _gla_kernel(qf_ref, kf_ref, vf_ref, bf_ref, qr_ref, kr_ref, vr_ref, br_ref, of_ref, or_ref, s_scr):
    @pl.when(pl.program_id(1) == 0)
    def _():
        s_scr[...] = jnp.zeros_like(s_scr)

    L = CHUNK
    nchunk = TM // L
    npair = GLA_HEADS // 2
    lane = lax.broadcasted_iota(jnp.int32, (L, LANES), 1)
    rowv = lax.broadcasted_iota(jnp.int32, (L, LANES), 0)
    srow = lax.broadcasted_iota(jnp.int32, (LANES, 2 * GLA_DV), 0)
    scol = lax.broadcasted_iota(jnp.int32, (LANES, 2 * GLA_DV), 1)
    own_block = (srow < GLA_DK) == (scol < GLA_DV)
    vcol = lax.broadcasted_iota(jnp.int32, (L, 2 * GLA_DV), 1)
    e_row = lax.broadcasted_iota(jnp.int32, (LANES, LANES), 0)
    e_col = lax.broadcasted_iota(jnp.int32, (LANES, LANES), 1)
    head_ones = jnp.where((e_row < GLA_DK) == (e_col < GLA_DK), 1.0, 0.0).astype(BF16)
    leaf_idx = lax.broadcasted_iota(jnp.int32, (GLA_LEAF * GLA_LEAF, LANES), 0)
    leaf_lane = lax.broadcasted_iota(jnp.int32, (GLA_LEAF * GLA_LEAF, LANES), 1) % L
    leaf_s, leaf_t = leaf_idx // GLA_LEAF, leaf_idx % GLA_LEAF
    rep = lambda a: jnp.concatenate(
        [jnp.broadcast_to(a[s:s + 1], (GLA_LEAF, a.shape[1])) for s in range(GLA_LEAF)], axis=0)
    til = lambda a: jnp.concatenate([a] * GLA_LEAF, axis=0)

    def chunk_step(ci, carry):
        cs = []
        for bi, d in [(bi, d) for bi in range(qf_ref.shape[0]) for d in range(2)]:
            q_ref, k_ref, v_ref, b_ref, o_ref = (
                r.at[bi] for r in ((qf_ref, kf_ref, vf_ref, bf_ref, of_ref), (qr_ref, kr_ref, vr_ref, br_ref, or_ref))[d])
            cd = ci if d == 0 else nchunk - 1 - ci
            base = pl.multiple_of(cd * L, L)
            for p in range(npair):
                ls = slice(p * LANES, (p + 1) * LANES)
                bc = b_ref[pl.ds(base, L), ls]
                cs.append(dict(
                    d=d, p=p, bi=bi, base=base, o_ref=o_ref, bc=bc,
                    tau_v=rowv if d == 0 else L - 1 - rowv,
                    tau_s=(lane % L) if d == 0 else L - 1 - (lane % L),
                    q=q_ref[pl.ds(base, L), ls].astype(F32),
                    k=k_ref[pl.ds(base, L), ls].astype(F32),
                    v=v_ref[pl.ds(base, L), p * 2 * GLA_DV:(p + 1) * 2 * GLA_DV],
                    b_last=bc[L - 1:L] if d == 0 else bc[0:1],
                    state=s_scr[bi, d, p]))

        for c in cs:
            c["o_pair"] = _dot((c["q"] * jnp.exp2(c["bc"])).astype(BF16), c["state"].astype(BF16))

        for c in cs:
            c["att"] = jnp.zeros((L, LANES), F32)
        bs = L // 2
        while bs >= GLA_LEAF:
            for c in cs:
                d, bc, tau_v = c["d"], c["bc"], c["tau_v"]
                later = (tau_v // bs) % 2 == 1
                ref_b = None
                for blk in range(L // (2 * bs)):
                    tau_ref = blk * 2 * bs + bs - 1
                    idx = tau_ref if d == 0 else L - 1 - tau_ref
                    r = jnp.broadcast_to(bc[idx:idx + 1], (L, LANES))
                    ref_b = r if ref_b is None else jnp.where(tau_v // (2 * bs) == blk, r, ref_b)
                c["qs"] = jnp.where(later, c["q"] * jnp.exp2(jnp.where(later, bc - ref_b, 0.0)), 0.0).astype(BF16)
                ks = jnp.where(later, 0.0, c["k"] * jnp.exp2(jnp.where(later, 0.0, ref_b - bc)))
                c["ks2"] = jnp.concatenate([jnp.where(lane < GLA_DK, ks, 0.0), jnp.where(lane < GLA_DK, 0.0, ks)],
                                           axis=0).astype(BF16)
            for c in cs:
                same_parent = (c["tau_v"] // (2 * bs)) == (c["tau_s"] // (2 * bs))
                c["att"] = c["att"] + jnp.where(same_parent, _dot_nt(c["qs"], c["ks2"]), 0.0)
            bs //= 2

        for c in cs:
            c["leaf"] = []
        for blk in range(L // GLA_LEAF):
            rs = slice(blk * GLA_LEAF, (blk + 1) * GLA_LEAF)
            for c in cs:
                qb, bb, kb = c["q"][rs], c["bc"][rs], c["k"][rs]
                ok = (leaf_t >= leaf_s) if c["d"] == 0 else (leaf_t <= leaf_s)
                w = jnp.where(ok, til(qb) * jnp.exp2(til(bb) - rep(bb)) * rep(kb), 0.0)
                c["a"] = _dot(w.astype(BF16), head_ones)
            for c in cs:
                a = jnp.where(leaf_lane == leaf_s + blk * GLA_LEAF, c["a"], 0.0)
                acc = a[0:GLA_LEAF]
                for s in range(1, GLA_LEAF):
                    acc = acc + a[s * GLA_LEAF:(s + 1) * GLA_LEAF]
                c["leaf"].append(acc)

        for c in cs:
            v = c["v"]
            c["att"] = (c["att"] + jnp.concatenate(c["leaf"], axis=0)).astype(BF16)
            c["v_blk"] = jnp.concatenate([jnp.where(vcol < GLA_DV, v, jnp.zeros_like(v)),
                                          jnp.where(vcol < GLA_DV, jnp.zeros_like(v), v)], axis=0)
            c["ke"] = (c["k"] * jnp.exp2(c["b_last"] - c["bc"])).astype(BF16)
            c["decay_col"] = jnp.sum(
                jnp.where(e_row == e_col, jnp.broadcast_to(jnp.exp2(c["b_last"]), (LANES, LANES)), 0.0),
                axis=1, keepdims=True)
        for c in cs:
            c["o"] = c["o_pair"] + _dot(c["att"], c["v_blk"])
            c["ds"] = _dot_tn(c["ke"], c["v"])
        for c in cs:
            p = c["p"]
            c["o_ref"][pl.ds(c["base"], L), p * 2 * GLA_DV:(p + 1) * 2 * GLA_DV] = c["o"].astype(BF16)
            s_scr[c["bi"], c["d"], p] = c["decay_col"] * c["state"] + jnp.where(own_block, c["ds"], 0.0)
        return carry

    lax.fori_loop(0, nchunk, chunk_step, 0)


def _gla(gq, gk, gv, bc, nct):
    bsz, ttot, nk = gq.shape
    nc = gv.shape[2]
    nt = ttot // TM
    sb = math.gcd(bsz, GLA_BATCH)
    fwd = lambda w: pl.BlockSpec((sb, TM, w), lambda b, j: (b, j, 0))
    rev = lambda w: pl.BlockSpec((sb, TM, w), lambda b, j: (b, _scan_block(j, nct, nt), 0))
    bfwd = pl.BlockSpec((sb, TM, nk), lambda b, j: (b, j, 0))
    brev = pl.BlockSpec((sb, TM, nk), lambda b, j: (b, _scan_block(j, nct, nt), 1))
    return pl.pallas_call(
        _gla_kernel,
        out_shape=(jax.ShapeDtypeStruct((bsz, ttot, nc), BF16),) * 2,
        grid=(bsz // sb, nt),
        in_specs=[fwd(nk), fwd(nk), fwd(nc), bfwd, rev(nk), rev(nk), rev(nc), brev],
        out_specs=(fwd(nc), rev(nc)),
        scratch_shapes=[pltpu.VMEM((sb, 2, GLA_HEADS // 2, LANES, 2 * GLA_DV), F32)],
        compiler_params=_params("parallel", "arbitrary"),
        name="gla_scan",
    )(gq, gk, gv, bc, gq, gk, gv, bc)


def _na_bias_tables(rpb, rows):
    kh = NA_KH
    c = np.arange(GRID_W)
    cs = np.clip(c - NA_KW // 2, 0, GRID_W - NA_KW)
    col_ok = (c[None, :] >= cs[:, None]) & (c[None, :] < cs[:, None] + NA_KW)
    pad = GRID_W - NA_KW
    rpb_pad = jnp.pad(rpb.astype(F32) * LOG2E, ((0, 0), (0, 0), (pad, pad)))
    period = 2 * GRID_W
    flat = jnp.tile(jnp.pad(rpb_pad, ((0, 0), (0, 0), (0, 1))), (1, 1, GRID_W))[:, :, :GRID_W * (period - 1)]
    t1 = flat.reshape(NA_HEADS, 2 * NA_KH - 1, GRID_W, period - 1)[..., GRID_W - 1:]
    t1 = jnp.where(col_ok, t1, NEG_BIG)
    masked = jnp.full((NA_HEADS, GRID_W, GRID_W), NEG_BIG, F32)
    per_row = []
    for r0 in (0, NA_QROWS, rows - NA_QROWS):
        kb = int(np.clip(r0 - kh // 2, 0, rows - NA_KROWS))
        for qrow in range(r0, r0 + NA_QROWS):
            ws = int(np.clip(qrow - kh // 2, 0, rows - kh))
            per_row.append(jnp.concatenate(
                [t1[:, krow - qrow + NA_KH - 1] if ws <= krow < ws + kh else masked
                 for krow in range(kb, kb + NA_KROWS)], axis=-1))
    return jnp.concatenate(per_row, axis=1)


def _na_heads(q_ref, kw_ref, vw_ref, kc_ref, vc_ref, bias_ref):
    nq = NA_QROWS * GRID_W
    lane = lax.broadcasted_iota(jnp.int32, (nq, LANES), 1)
    def scores(j):
        ls = slice((j // 2) * LANES, (j // 2 + 1) * LANES)
        q = q_ref[:, ls]
        qj = jnp.where((lane < NA_DH) == (j % 2 == 0), q, jnp.zeros_like(q))
        return _dot_nt(qj, kw_ref[:, ls]) + bias_ref[j], _dot_nt(qj, kc_ref[:, ls])

    outs = []
    nxt = scores(0)
    for j in range(NA_HEADS):
        s_loc, s_ctx = nxt
        if j + 1 < NA_HEADS:
            nxt = scores(j + 1)
        ls = slice((j // 2) * LANES, (j // 2 + 1) * LANES)
        m = jnp.maximum(jnp.max(s_loc, axis=-1, keepdims=True), jnp.max(s_ctx, axis=-1, keepdims=True))
        p_loc = jnp.exp2(s_loc - m)
        p_ctx = jnp.exp2(s_ctx - m)
        den = jnp.sum(p_loc, axis=-1, keepdims=True) + jnp.sum(p_ctx, axis=-1, keepdims=True)
        outs.append((_dot(p_loc.astype(BF16), vw_ref[:, ls]) + _dot(p_ctx.astype(BF16), vc_ref[:, ls])) / den)
    return jnp.concatenate([jnp.where(lane < NA_DH, outs[2 * i], outs[2 * i + 1]) for i in range(NA_HEADS // 2)], axis=1)


def _na_out1_kernel(*refs, nb):
    kv = refs[:4 * nb]
    q_ref, bias_ref, h_ref, gate_ref, of_ref, or_ref, z_ref, gn_ref, w_ref, fn_ref, o_ref = refs[4 * nb:]
    rows = lambda f: jnp.concatenate([f(bi) for bi in range(nb)], axis=0)
    f32 = lambda r, bi: r[bi].astype(F32)
    na = rows(lambda bi: _na_heads(q_ref.at[bi], *kv[4 * bi:4 * bi + 4], bias_ref))
    g = rows(lambda bi: _head_rms(f32(of_ref, bi) + f32(or_ref, bi), GLA_HEADS) * gn_ref[...])
    cat = (jnp.concatenate([g, na], axis=1) * rows(lambda bi: f32(z_ref, bi))).astype(BF16)
    y = _dot(cat, w_ref[...])
    for bi in range(nb):
        h = h_ref[bi] + gate_ref[bi] * y[bi * TM:(bi + 1) * TM]
        o_ref[bi] = _rms(h) * fn_ref[...]


def _na_out1(nq, nk, nv, bias, h, gate, of, orv, z, gla_norm, w_out, final_norm, n_ctx, rows):
    bsz, ttot, nd = nq.shape
    nrb = rows // NA_QROWS
    qrows = NA_QROWS * GRID_W
    krows = NA_KROWS * GRID_W
    assert qrows == TM
    nct = n_ctx // TM
    nc = GLA_HEADS * GLA_DV
    sb = math.gcd(bsz, TILE_BATCH)

    def variant(rb):
        return jnp.where(rb == 0, 0, jnp.where(rb == nrb - 1, 2, 1))

    def key_start(rb):
        first_row = jnp.clip(rb * NA_QROWS - NA_KH // 2, 0, rows - NA_KROWS)
        return pl.multiple_of(n_ctx + first_row * GRID_W, GRID_W)

    window = lambda i: pl.BlockSpec((None, pl.Element(krows), pl.Element(nd)), lambda b, rb: (b * sb + i, key_start(rb), 0))
    context = lambda i: pl.BlockSpec((None, pl.Element(n_ctx), pl.Element(nd)), lambda b, rb: (b * sb + i, 0, 0))
    kv_specs = [spec(i) for i in range(sb) for spec in (window, window, context, context)]
    lat = lambda w: pl.BlockSpec((sb, TM, w), lambda b, rb: (b, rb + nct, 0))
    return pl.pallas_call(
        functools.partial(_na_out1_kernel, nb=sb),
        out_shape=jax.ShapeDtypeStruct((bsz, rows * GRID_W, D_MODEL), F32),
        grid=(bsz // sb, nrb),
        in_specs=kv_specs + [
            lat(nd), pl.BlockSpec((NA_HEADS, qrows, krows), lambda b, rb: (0, variant(rb), 0)),
            lat(D_MODEL), pl.BlockSpec((sb, None, 1, D_MODEL), lambda b, rb: (b, 1, 0, 0)),
            lat(nc), lat(nc), lat(nc + nd), _full((1, nc)), _full(w_out.shape), _full((1, D_MODEL))],
        out_specs=pl.BlockSpec((sb, TM, D_MODEL), lambda b, rb: (b, rb, 0)),
        compiler_params=_params("parallel", "parallel"),
        name="na_out1",
    )(*([nk, nv, nk, nv] * sb), nq, bias, h, gate, of, orv, z, gla_norm[None], w_out.astype(BF16), final_norm[None])


def kernel(x, c, ctx, c_ctx, l0_norm, l0_w_mod, l0_b_mod, l0_w_in, l0_mla_q_norm, l0_mla_w_uq, l0_mla_kv_norm, l0_mla_w_ukv, l0_mlstm_conv_w, l0_mlstm_conv_b, l0_mlstm_b_i, l0_mlstm_b_f, l0_mlstm_norm, l0_w_out, l1_norm, l1_w_mod, l1_b_mod, l1_w_in, l1_gla_w_gate, l1_gla_b_gate, l1_gla_norm, l1_na_rpb, l1_w_out, final_norm):
    bsz, seq, d = x.shape
    n_ctx = ctx.shape[1]
    rows = seq // GRID_W
    assert d == D_MODEL and seq % TM == 0 and n_ctx == TM
    assert rows % NA_QROWS == 0 and rows >= NA_KROWS and rows // NA_QROWS >= 3
    nct = n_ctx // TM

    (shift0, scale0, gate0), (shift, scale, gate) = _mod_vectors(
        c, c_ctx, ((l0_w_mod, l0_b_mod), (l1_w_mod, l1_b_mod)))

    cos_t, sin_t = _rope_tables(n_ctx, seq)
    w0 = _proj0_weights(l0_w_in, l0_mla_q_norm, l0_mla_w_uq, l0_mla_kv_norm, l0_mla_w_ukv,
                        l0_mlstm_conv_w, l0_mlstm_conv_b, l0_mlstm_b_i, l0_mlstm_b_f)
    q, k, v, mq, mk, mv, g_rows, g_cols, mo, z0 = _proj0(ctx, x, scale0, shift0, l0_norm, cos_t, sin_t, w0, nct)
    a = _mla_attention(q, k, v, nct)
    hf, hr = _mlstm(mq, mk, mv, g_rows, g_cols, nct)

    w1 = _proj1_weights(l1_w_in, l1_gla_w_gate, l1_gla_b_gate)
    h, gq, gk, gv, bc, nq, nk, nv, z = _mid(ctx, x, gate0, a, hf, hr, mo, z0, l0_mlstm_norm, l0_w_out,
                                            scale, shift, l1_norm, w1, nct)
    of, orv = _gla(gq, gk, gv, bc, nct)
    return _na_out1(nq, nk, nv, _na_bias_tables(l1_na_rpb, rows), h, gate, of, orv, z,
                    l1_gla_norm, l1_w_out, final_norm, n_ctx, rows)
```

```python
import functools
import math

import jax
import jax.numpy as jnp
import numpy as np
from jax import lax
from jax.experimental import pallas as pl
from jax.experimental.pallas import tpu as pltpu

F32 = jnp.float32
BF16 = jnp.bfloat16

D_MODEL = 1024
GRID_W = 64
EPS = 1e-6
ROPE_BASE = 10000.0

MLA_HEADS = 8
MLA_Q_RANK = 384
MLA_KV_RANK = 256
MLA_NOPE = 64
MLA_ROPE = 32
MLA_V = 64
ROPE_PAIRS = MLA_ROPE // 4
MLSTM_HEADS = 4
MLSTM_DH = 128
MLSTM_CONV = 3
GLA_HEADS = 4
GLA_DK = 64
GLA_DV = 128
GLA_GATE_RANK = 16
GLA_TAU = 16.0
NA_HEADS = 8
NA_DH = 64
NA_KH = 8
NA_KW = 16

CHUNK = 64
MLSTM_CHUNK = 128
TM = 256
LANES = 128
SUBLANES = 8
HALO = SUBLANES
NA_QROWS = 4
NA_KROWS = NA_QROWS + NA_KH - 1
GLA_LEAF = 8
MLA_HPS = 4
MLA_QTILES = 2
TILE_BATCH = 2
MLSTM_BATCH = 2
GLA_BATCH = 8
VMEM_LIMIT = 56 * 1024 * 1024
NEG_BIG = -1e30
LOG2E = 1.4426950408889634


def _dot(a, b):
    return jnp.dot(a, b, preferred_element_type=F32)


def _dot_nt(a, b):
    return lax.dot_general(a, b, (((1,), (1,)), ((), ())), preferred_element_type=F32)


def _dot_tn(a, b):
    return lax.dot_general(a, b, (((0,), (0,)), ((), ())), preferred_element_type=F32)


def _rms(x):
    return x * lax.rsqrt(jnp.mean(x * x, axis=-1, keepdims=True) + EPS)


def _sigmoid(x):
    return 1.0 / (1.0 + jnp.exp(-x))


def _silu(x):
    return x * _sigmoid(x)


def _log_sigmoid(x):
    return jnp.minimum(x, 0.0) - jnp.log(1.0 + jnp.exp(-jnp.abs(x)))


def _params(*sem):
    return pltpu.CompilerParams(dimension_semantics=sem, vmem_limit_bytes=VMEM_LIMIT)


def _full(shape):
    nd = len(shape)
    return pl.BlockSpec(shape, lambda *_: (0,) * nd)


def _mod_kernel(c_ref, w0_ref, b0_ref, w1_ref, b1_ref, o0_ref, o1_ref):
    sc = _silu(c_ref[...]).astype(BF16)
    o0_ref[...] = _dot(sc, w0_ref[...].astype(BF16)) + b0_ref[...]
    o1_ref[...] = _dot(sc, w1_ref[...].astype(BF16)) + b1_ref[...]


def _mod_vectors(c, c_ctx, mods):
    bsz, d = c.shape
    rows = -(-(bsz + 1) // SUBLANES) * SUBLANES
    cc = jnp.zeros((rows, d), F32).at[:bsz].set(c).at[bsz].set(c_ctx)
    (w0, b0), (w1, b1) = mods
    n = w0.shape[1]
    wspec = pl.BlockSpec((d, d), lambda j: (0, j))
    vspec = pl.BlockSpec((1, d), lambda j: (0, j))
    ospec = pl.BlockSpec((rows, d), lambda j: (0, j))
    per_layer = pl.pallas_call(
        _mod_kernel,
        out_shape=(jax.ShapeDtypeStruct((rows, n), F32),) * 2,
        grid=(n // d,),
        in_specs=[_full((rows, d)), wspec, vspec, wspec, vspec],
        out_specs=(ospec, ospec),
        compiler_params=_params("parallel"),
        name="modulation",
    )(cc, w0, b0[None], w1, b1[None])
    out = []
    for mod in per_layer:
        vecs = []
        for part in jnp.split(mod, 3, axis=-1):
            ctx_v = jnp.broadcast_to(part[bsz][None], (bsz, d))
            vecs.append(jnp.stack([ctx_v, part[:bsz]], axis=1)[:, :, None, :])
        out.append(vecs)
    return out


def _modnorm(x, g, scale, shift):
    return _rms(x) * g * (1.0 + scale) + shift


def _seg_scan(x, reverse, op, fill):
    width = x.shape[1]
    pos = lax.broadcasted_iota(jnp.int32, x.shape, 1) % MLSTM_CHUNK
    k = 1
    while k < MLSTM_CHUNK:
        if reverse:
            x = op(x, jnp.where(pos < MLSTM_CHUNK - k, pltpu.roll(x, width - k, axis=1), fill))
        else:
            x = op(x, jnp.where(pos >= k, pltpu.roll(x, k, axis=1), fill))
        k *= 2
    return x


def _split3(x):
    hi = x.astype(BF16)
    r1 = x - hi.astype(F32)
    mid = r1.astype(BF16)
    return hi, mid, (r1 - mid.astype(F32)).astype(BF16)


def _proj0_kernel(ctx_ref, lat_ref, hp_ref, hn_ref, sc_ref, sh_ref, g_ref, cos_ref, sin_ref,
                  wall_ref, wgt_ref, qn_ref, wuq_ref, kvn_ref, wk_ref, wv_ref, cw_ref, cb_ref, gb_ref,
                  q_out, k_out, v_out, mq_out, mk_out, mv_out, gr_out, gc_out, mo_out, z_out,
                  pbuf, *, nct, nt):
    t = pl.program_id(1)
    h = jnp.where(t < nct, ctx_ref[...], lat_ref[...])
    x = jnp.concatenate([hp_ref[...], h, hn_ref[...]], axis=0)
    ub_ext = _modnorm(x, g_ref[...], sc_ref[...], sh_ref[...]).astype(BF16)
    ub = ub_ext[HALO:HALO + TM]

    gt = _dot_nt(wgt_ref[...], ub) + gb_ref[...]
    half = MLSTM_HEADS
    per_dir = []
    for d in range(2):
        gd = gt[d * SUBLANES:(d + 1) * SUBLANES]
        bcum = _seg_scan(_log_sigmoid(gd), d == 1, jnp.add, 0.0)
        b_top = pltpu.roll(bcum, half, axis=0)
        c8 = gd - b_top
        pm8 = _seg_scan(c8, d == 1, jnp.maximum, -jnp.inf)
        per_dir.append((c8, pm8, b_top))

    cos = cos_ref[...]
    sin = sin_ref[...]
    cos_all = jnp.concatenate([cos] * MLA_HEADS, axis=1)
    sin_all = jnp.concatenate([sin] * MLA_HEADS, axis=1)

    def swap_halves(a):
        lane = lax.broadcasted_iota(jnp.int32, a.shape, 1)
        first = lane % (2 * ROPE_PAIRS) < ROPE_PAIRS
        return jnp.where(first, pltpu.roll(a, a.shape[1] - ROPE_PAIRS, axis=1), pltpu.roll(a, ROPE_PAIRS, axis=1))

    wseg = lambda name: wall_ref[:, PROJ0_COLS[name][0]:PROJ0_COLS[name][1]]
    cqkr = _dot(ub, wseg("cqkr"))
    cq = cqkr[:, :MLA_Q_RANK]
    kr = cqkr[:, MLA_Q_RANK:]
    ckv = _dot(ub, wseg("ckv"))
    pqk = _dot(ub_ext, wseg("mqk"))

    nq = (_rms(cq) * qn_ref[...]).astype(BF16)
    nkv = (_rms(ckv) * kvn_ref[...]).astype(BF16)

    q = _dot(nq, wuq_ref[...])
    kn = _dot(nkv, wk_ref[...])
    vv = _dot(nkv, wv_ref[...])

    prev_ok = t > nct
    next_ok = jnp.logical_and(t >= nct, t != nt - 1)
    row = lax.broadcasted_iota(jnp.int32, pqk.shape, 0)
    keep = jnp.logical_and(jnp.logical_or(row >= HALO, prev_ok), jnp.logical_or(row < HALO + TM, next_ok))
    pbuf[...] = jnp.where(keep, pqk, 0.0)
    cw = cw_ref[...]
    y = (pbuf[HALO - 1:HALO - 1 + TM, :] * cw[0:1] + pbuf[HALO:HALO + TM, :] * cw[1:2]
         + pbuf[HALO + 1:HALO + 1 + TM, :] * cw[2:3] + cb_ref[...])
    y = _silu(y)
    nb = MLSTM_HEADS * MLSTM_DH
    mq_out[...] = (y[:, :nb] * MLSTM_DH ** -0.5).astype(BF16)
    mk_out[...] = y[:, nb:].astype(BF16)

    mv = _dot(ub, wseg("mv"))
    mo = _dot(ub, wseg("mo"))

    q = q * cos_all + swap_halves(q) * sin_all
    q_out[...] = (q * ((MLA_NOPE + MLA_ROPE) ** -0.5 * LOG2E)).astype(BF16)
    rot = kr * cos + swap_halves(kr) * sin
    k_out[...] = (kn + jnp.concatenate([rot] * MLA_HEADS, axis=1)).astype(BF16)
    vlane = lax.broadcasted_iota(jnp.int32, (TM, MLA_HEADS * LANES), 1)
    is_value = ((vlane % LANES) < MLA_V) == ((vlane // LANES) % 2 == 0)
    v_out[...] = jnp.where(is_value, vv, 1.0).astype(BF16)

    zz = _dot(ub, wseg("z"))
    mv_out[...] = mv.astype(BF16)
    mo_out[...] = _sigmoid(mo).astype(BF16)
    z_out[...] = _silu(zz).astype(BF16)

    r8 =lax.broadcasted_iota(jnp.int32, (SUBLANES, TM), 0)
    both = lambda i: jnp.where(r8 < half, per_dir[0][i], pltpu.roll(per_dir[1][i], half, axis=0)) * LOG2E
    c_rows, pm_rows, b_rows = both(0), both(1), both(2)
    for ci in range(TM // MLSTM_CHUNK):
        gr_out[ci] = c_rows[:, ci * MLSTM_CHUNK:(ci + 1) * MLSTM_CHUNK]
    stack = jnp.concatenate([c_rows, pm_rows, b_rows, jnp.zeros((LANES - 3 * SUBLANES, TM), F32)], axis=0)
    er = lax.broadcasted_iota(jnp.int32, (TM, TM), 0)
    ec = lax.broadcasted_iota(jnp.int32, (TM, TM), 1)
    eye = jnp.where(er == ec, 1.0, 0.0).astype(BF16)
    gc_out[...] = sum(_dot_nt(eye, part) for part in _split3(stack))


def _rope_tables(n_ctx, seq):
    t = jnp.arange(seq)
    inv = 1.0 / (ROPE_BASE ** (jnp.arange(ROPE_PAIRS, dtype=F32) / ROPE_PAIRS))
    ang = jnp.concatenate([(t // GRID_W)[:, None] * inv, (t % GRID_W)[:, None] * inv], axis=-1)
    cos, sin = jnp.cos(ang), jnp.sin(ang)
    j = np.arange(MLA_ROPE)
    src = (j // (2 * ROPE_PAIRS)) * ROPE_PAIRS + (j % ROPE_PAIRS)
    sign = np.where((j % (2 * ROPE_PAIRS)) < ROPE_PAIRS, -1.0, 1.0).astype(np.float32)
    cos_full = cos[:, src]
    sin_full = sin[:, src] * sign
    cos_t = jnp.ones((n_ctx + seq, LANES), F32).at[n_ctx:, MLA_NOPE:MLA_NOPE + MLA_ROPE].set(cos_full)
    sin_t = jnp.zeros((n_ctx + seq, LANES), F32).at[n_ctx:, MLA_NOPE:MLA_NOPE + MLA_ROPE].set(sin_full)
    return cos_t, sin_t


def _relayout_kernel(wt_ref, o_ref, *, moves):
    lane = lax.broadcasted_iota(jnp.int32, (o_ref.shape[0], LANES), 1)
    end = 0
    for src, n, dst in moves:
        group = dst - dst % LANES
        if group > end:
            o_ref[:, end:group] = jnp.zeros((o_ref.shape[0], group - end), o_ref.dtype)
        if n % LANES == 0 and dst == group:
            o_ref[:, dst:dst + n] = wt_ref[src:src + n, :].T.astype(o_ref.dtype)
            end = dst + n
        else:
            assert n < LANES and dst + n <= group + LANES and src + LANES <= wt_ref.shape[0]
            t = pltpu.roll(wt_ref[src:src + LANES, :].T, dst - group, axis=1)
            keep = (lane >= dst - group) & (lane < dst - group + n)
            o_ref[:, group:group + LANES] = jnp.where(keep, t, 0.0).astype(o_ref.dtype)
            end = group + LANES
    if end < o_ref.shape[1]:
        o_ref[:, end:] = jnp.zeros((o_ref.shape[0], o_ref.shape[1] - end), o_ref.dtype)


def _relayout(wt, moves, width):
    rows = wt.shape[1]
    rb = math.gcd(rows, TM)
    return pl.pallas_call(
        functools.partial(_relayout_kernel, moves=tuple(moves)),
        out_shape=jax.ShapeDtypeStruct((rows, width), BF16),
        grid=(rows // rb,),
        in_specs=[pl.BlockSpec((wt.shape[0], rb), lambda i: (0, i))],
        out_specs=pl.BlockSpec((rb, width), lambda i: (i, 0)),
        compiler_params=_params("parallel"),
        name="weight_relayout",
    )(wt)


def _segments(widths):
    out, start = {}, 0
    for name, width in widths:
        out[name] = (start, start + width)
        start += width
    return out


PROJ0_COLS = _segments([("cqkr", MLA_Q_RANK + LANES), ("ckv", MLA_KV_RANK), ("mqk", 2 * MLSTM_HEADS * MLSTM_DH),
                        ("mv", MLSTM_HEADS * MLSTM_DH), ("mo", MLSTM_HEADS * MLSTM_DH), ("z", D_MODEL)])
PROJ1_COLS = _segments([("q", GLA_HEADS * GLA_DK), ("k", GLA_HEADS * GLA_DK), ("v", GLA_HEADS * GLA_DV),
                        ("ga", LANES), ("nq", NA_HEADS * NA_DH), ("nk", NA_HEADS * NA_DH), ("nv", NA_HEADS * NA_DH),
                        ("z", GLA_HEADS * GLA_DV + NA_HEADS * NA_DH)])


def _proj0_weights(w_in, q_norm, w_uq, kv_norm, w_ukv, conv_w, conv_b, b_i, b_f):
    o = np.cumsum([0, MLA_Q_RANK, MLA_KV_RANK, MLA_ROPE] + [MLSTM_HEADS * MLSTM_DH] * 4 + [4 * MLSTM_HEADS, D_MODEL])
    dq = MLA_NOPE + MLA_ROPE
    wuq = jnp.pad(w_uq.reshape(MLA_Q_RANK, MLA_HEADS, dq), ((0, 0), (0, 0), (0, LANES - dq)))
    wkv = w_ukv.reshape(MLA_KV_RANK, MLA_HEADS, MLA_NOPE + MLA_V)
    wk = jnp.pad(wkv[:, :, :MLA_NOPE], ((0, 0), (0, 0), (0, LANES - MLA_NOPE)))
    zv = jnp.zeros((MLA_KV_RANK, MLA_HEADS // 2, MLA_V), F32)
    wv = jnp.concatenate([wkv[:, 0::2, MLA_NOPE:], zv, zv, wkv[:, 1::2, MLA_NOPE:]], axis=-1)
    wv = wv.reshape(MLA_KV_RANK, MLA_HEADS * LANES)
    gbias = jnp.stack([b_i, b_f], axis=1).reshape(4 * MLSTM_HEADS, 1)
    bf = lambda a: a.astype(BF16)
    c = PROJ0_COLS
    wall = _relayout(w_in.T, [(o[0], MLA_Q_RANK, c["cqkr"][0]), (o[2], MLA_ROPE, c["cqkr"][0] + MLA_Q_RANK + MLA_NOPE),
                            (o[1], MLA_KV_RANK, c["ckv"][0]), (o[3], o[7] - o[3], c["mqk"][0]),
                            (o[8], D_MODEL, c["z"][0])], c["z"][1])
    return dict(
        wall=wall, wgt=bf(w_in[:, o[7]:o[8]].T),
        qn=q_norm[None], wuq=bf(wuq.reshape(MLA_Q_RANK, -1)),
        kvn=kv_norm[None], wk=bf(wk.reshape(MLA_KV_RANK, -1)), wv=bf(wv),
        cw=jnp.zeros((SUBLANES, conv_w.shape[1]), F32).at[:MLSTM_CONV].set(conv_w), cb=conv_b[None], gb=gbias)


def _tile_specs(nct, nt, nb):
    tile = lambda w: pl.BlockSpec((nb, TM, w), lambda b, t: (b, t, 0))
    per = TM // HALO
    nlat = nt - nct
    src_ctx = pl.BlockSpec((nb, TM, D_MODEL), lambda b, t: (b, jnp.minimum(t, nct - 1), 0))
    src_lat = pl.BlockSpec((nb, TM, D_MODEL), lambda b, t: (b, jnp.maximum(t - nct, 0), 0))
    prev = pl.BlockSpec((nb, HALO, D_MODEL), lambda b, t: (b, jnp.maximum((t - nct) * per - 1, 0), 0))
    nxt = pl.BlockSpec((nb, HALO, D_MODEL),
                       lambda b, t: (b, jnp.clip((t - nct + 1) * per, 0, nlat * per - 1), 0))
    mod = pl.BlockSpec((nb, None, 1, D_MODEL), lambda b, t: (b, (t >= nct).astype(jnp.int32), 0, 0))
    return tile, src_ctx, src_lat, prev, nxt, mod


def _per_element(tile_kernel, shared):
    def kernel(*refs, **kw):
        for bi in range(refs[0].shape[0]):
            tile_kernel(*[r if i in shared else r.at[bi] for i, r in enumerate(refs)], **kw)
    return kernel


def _proj0(ctx, x, scale, shift, norm_g, cos_t, sin_t, w, nct):
    bsz = x.shape[0]
    ttot = ctx.shape[1] + x.shape[1]
    nt = ttot // TM
    nb = MLSTM_HEADS * MLSTM_DH
    sb = math.gcd(bsz, TILE_BATCH)
    tile, src_ctx, src_lat, prev, nxt, mod = _tile_specs(nct, nt, sb)
    tab = pl.BlockSpec((TM, LANES), lambda b, t: (t, 0))
    wnames = ["wall", "wgt", "qn", "wuq", "kvn", "wk", "wv", "cw", "cb", "gb"]
    ws = [w[n] for n in wnames]
    tok = lambda width, dt: jax.ShapeDtypeStruct((bsz, ttot, width), dt)
    out_shape = (tok(MLA_HEADS * LANES, BF16), tok(MLA_HEADS * LANES, BF16), tok(MLA_HEADS * LANES, BF16),
                 tok(nb, BF16), tok(nb, BF16), tok(nb, BF16),
                 jax.ShapeDtypeStruct((bsz, ttot // MLSTM_CHUNK, 2 * MLSTM_HEADS, MLSTM_CHUNK), F32), tok(LANES, F32),
                 tok(nb, BF16), tok(D_MODEL, BF16))
    out_specs = (tile(MLA_HEADS * LANES), tile(MLA_HEADS * LANES), tile(MLA_HEADS * LANES),
                 tile(nb), tile(nb), tile(nb),
                 pl.BlockSpec((sb, TM // MLSTM_CHUNK, 2 * MLSTM_HEADS, MLSTM_CHUNK), lambda b, t: (b, t, 0, 0)),
                 tile(LANES),
                 tile(nb), tile(D_MODEL))
    n_blocked, n_out = 6, len(out_shape)
    n_in = n_blocked + 3 + len(ws)
    shared = set(range(n_blocked, n_in)) | {n_in + n_out}
    return pl.pallas_call(
        functools.partial(_per_element(_proj0_kernel, shared), nct=nct, nt=nt),
        out_shape=out_shape,
        grid=(bsz // sb, nt),
        in_specs=[src_ctx, src_lat, prev, nxt, mod, mod, _full((1, D_MODEL)), tab, tab] + [_full(a.shape) for a in ws],
        out_specs=out_specs,
        scratch_shapes=[pltpu.VMEM((TM + 2 * HALO, 2 * nb), F32)],
        compiler_params=_params("parallel", "parallel"),
        name="proj0",
    )(ctx, x, x, x, scale, shift, norm_g[None], cos_t, sin_t, *ws)


def _mla_kernel(qa_ref, qb_ref, k_ref, v_ref, o_ref, *, nkeys_ctx):
    step = pl.program_id(2)

    def attend(queries, nkeys, rows):
        qk = lambda j: _dot_nt(queries(j), k_ref[0:nkeys, j * LANES:(j + 1) * LANES])
        outs = []
        s_next = qk(0)
        for j in range(MLA_HPS):
            s = s_next
            if j + 1 < MLA_HPS:
                s_next = qk(j + 1)
            p = jnp.exp2(s - jnp.max(s, axis=-1, keepdims=True))
            o = _dot(p.astype(BF16), v_ref[0:nkeys, j * LANES:(j + 1) * LANES])
            outs.append(o / pltpu.roll(o, MLA_V, axis=1))
        lane = lax.broadcasted_iota(jnp.int32, outs[0].shape, 1)
        for j in range(MLA_HPS // 2):
            o_ref[rows, j * LANES:(j + 1) * LANES] = jnp.where(lane < MLA_V, outs[2 * j], outs[2 * j + 1]).astype(BF16)

    @pl.when(step == 0)
    def _():
        attend(lambda j: qa_ref[:, j * LANES:(j + 1) * LANES], nkeys_ctx, slice(0, TM))
        o_ref[TM:, :] = jnp.zeros((o_ref.shape[0] - TM, o_ref.shape[1]), BF16)

    @pl.when(step > 0)
    def _():
        stacked = lambda j: jnp.concatenate([qa_ref[:, j * LANES:(j + 1) * LANES],
                                             qb_ref[:, j * LANES:(j + 1) * LANES]], axis=0)
        attend(stacked, k_ref.shape[0], slice(None))


def _mla_attention(q, k, v, nct):
    bsz, ttot, _ = q.shape
    nt = ttot // TM
    assert nct == 1 and MLA_QTILES == 2 and (nt - nct) % MLA_QTILES == 0
    hw = MLA_HPS * LANES
    nsteps = 1 + (nt - nct) // MLA_QTILES
    return pl.pallas_call(
        functools.partial(_mla_kernel, nkeys_ctx=nct * TM),
        out_shape=jax.ShapeDtypeStruct((bsz, nsteps * MLA_QTILES * TM, MLA_HEADS * MLA_V), BF16),
        grid=(bsz, MLA_HEADS // MLA_HPS, nsteps),
        in_specs=[pl.BlockSpec((None, TM, hw), lambda b, p, s: (b, jnp.maximum(2 * s - 1, 0), p)),
                  pl.BlockSpec((None, TM, hw), lambda b, p, s: (b, jnp.maximum(2 * s, 1), p)),
                  pl.BlockSpec((None, ttot, hw), lambda b, p, s: (b, 0, p)),
                  pl.BlockSpec((None, ttot, hw), lambda b, p, s: (b, 0, p))],
        out_specs=pl.BlockSpec((None, MLA_QTILES * TM, hw // 2), lambda b, p, s: (b, s, p)),
        compiler_params=_params("parallel", "parallel", "parallel"),
        name="mla_attention",
    )(q, q, k, v)


def _mla_tile(t, nct):
    return t + (t >= nct).astype(jnp.int32)


def _scan_block(j, nct, nt):
    return jnp.where(j < nct, nct - 1 - j, nt - 1 - (j - nct))


def _mlstm_kernel(qf_ref, kf_ref, vf_ref, grf_ref, gcf_ref, qr_ref, kr_ref, vr_ref, grr_ref, gcr_ref,
                  hf_ref, hr_ref, c_scr, m_scr):
    @pl.when(pl.program_id(1) == 0)
    def _():
        c_scr[...] = jnp.zeros_like(c_scr)
        m_scr[...] = jnp.zeros_like(m_scr)

    L = MLSTM_CHUNK
    nchunk = TM // L
    nh = MLSTM_HEADS
    nchain = 2 * nh
    row = lax.broadcasted_iota(jnp.int32, (L, L), 0)
    col = lax.broadcasted_iota(jnp.int32, (L, L), 1)
    ones = jnp.ones((L, MLSTM_DH), BF16)
    wide = lambda a: jnp.broadcast_to(a, (L, MLSTM_DH))
    twice = lambda a: jnp.concatenate([a, a], axis=1)
    refs = ((qf_ref, kf_ref, vf_ref, grf_ref, gcf_ref, hf_ref), (qr_ref, kr_ref, vr_ref, grr_ref, gcr_ref, hr_ref))
    nbatch = qf_ref.shape[0]

    def load(ci):
        cs = []
        for bi, d in [(bi, d) for bi in range(nbatch) for d in range(2)]:
            q_ref, k_ref, v_ref, gr_ref, gc_ref, o_ref = (r.at[bi] for r in refs[d])
            cd = ci if d == 0 else nchunk - 1 - ci
            rs = slice(cd * L, (cd + 1) * L)
            g_rows = gr_ref[cd]
            g_cols = gc_ref[rs, :]
            last = L - 1 if d == 0 else 0
            for hd in range(nh):
                ch = d * nh + hd
                hs = slice(hd * MLSTM_DH, (hd + 1) * MLSTM_DH)
                cs.append(dict(
                    rs=rs, hs=hs, o_ref=o_ref, causal=(col <= row) if d == 0 else (col >= row),
                    c_row=g_rows[ch:ch + 1], c_col=g_cols[:, ch:ch + 1],
                    pm_col=g_cols[:, nchain + ch:nchain + ch + 1], b_col=g_cols[:, 2 * nchain + ch:2 * nchain + ch + 1],
                    pm_last=g_cols[last:last + 1, nchain + ch:nchain + ch + 1],
                    b_last=g_cols[last:last + 1, 2 * nchain + ch:2 * nchain + ch + 1],
                    q=q_ref[rs, hs], k=k_ref[rs, hs],
                    v_ext=jnp.concatenate([v_ref[rs, hs], ones], axis=1)))
        return cs

    def prep_scores(cs):
        for c in cs:
            c["s"] = _dot_nt(c["q"], c["k"])

    def prep_weights(cs):
        for c in cs:
            c["pm_w"] = wide(c["pm_col"])
            c["b_w"] = wide(c["b_col"])
            c["w"] = jnp.exp2(jnp.where(c["causal"], c["c_row"] - c["pm_w"][:, :L], -jnp.inf))
            c["kwf"] = jnp.exp2(wide(c["c_col"] - c["pm_last"]))

    def prep_operands(cs):
        for c in cs:
            c["p"] = (c["s"] * c["w"]).astype(BF16)
            c["kw"] = (c["kwf"] * c["k"].astype(F32)).astype(BF16)

    def prep_products(cs):
        for c in cs:
            c["o1"] = _dot(c["p"], c["v_ext"])
            c["dc"] = _dot_tn(c["kw"], c["v_ext"])

    def scan_read(cs):
        for ch, c in enumerate(cs):
            c["qc"] = _dot(c["q"], state[ch].astype(BF16))

    def scan_factors(cs):
        for ch, c in enumerate(cs):
            g_w = jnp.maximum(m[ch], c["pm_w"])
            c["e_intra"] = jnp.exp2(c["pm_w"] - g_w)
            c["e_inter"] = jnp.exp2(m[ch] - g_w)
            c["floor"] = jnp.exp2(-(c["b_w"] + g_w))
            g_last = jnp.maximum(m[ch], c["pm_last"])
            c["keep"] = jnp.exp2(m[ch] - g_last)
            c["gain"] = jnp.exp2(c["pm_last"] - g_last)
            c["m_new"] = c["b_last"] + g_last

    def scan_update(cs):
        for ch, c in enumerate(cs):
            o = twice(c["e_intra"]) * c["o1"] + twice(c["e_inter"]) * c["qc"]
            h = o[:, :MLSTM_DH] / jnp.maximum(jnp.abs(o[:, MLSTM_DH:]), c["floor"])
            c["o_ref"][c["rs"], c["hs"]] = h.astype(BF16)
            state[ch] = c["keep"] * state[ch] + c["gain"] * c["dc"]
            m[ch] = c["m_new"]

    nstate = nbatch * nchain
    state = [c_scr[ch] for ch in range(nstate)]
    m = [m_scr[ch, 0:1, 0:1] for ch in range(nstate)]
    chunks = [load(ci) for ci in range(nchunk)]
    everything = [c for cs in chunks for c in cs]
    for stage in (prep_scores, prep_weights, prep_operands):
        stage(everything)
    prep_products(chunks[0])
    for i, cs in enumerate(chunks):
        scan_read(cs)
        if i + 1 < nchunk:
            prep_products(chunks[i + 1])
        scan_factors(cs)
        scan_update(cs)
    for ch in range(nstate):
        c_scr[ch] = state[ch]
        m_scr[ch] = jnp.broadcast_to(m[ch], m_scr.shape[1:])


def _mlstm(mq, mk, mv, g_rows, g_cols, nct):
    bsz, ttot, nb = mq.shape
    nt = ttot // TM
    per = TM // MLSTM_CHUNK
    sb = math.gcd(bsz, MLSTM_BATCH)
    fwd = lambda w: pl.BlockSpec((sb, TM, w), lambda b, j: (b, j, 0))
    rev = lambda w: pl.BlockSpec((sb, TM, w), lambda b, j: (b, _scan_block(j, nct, nt), 0))
    gshape = (sb, per, 2 * MLSTM_HEADS, MLSTM_CHUNK)
    gfwd = pl.BlockSpec(gshape, lambda b, j: (b, j, 0, 0))
    grev = pl.BlockSpec(gshape, lambda b, j: (b, _scan_block(j, nct, nt), 0, 0))
    nstate = sb * 2 * MLSTM_HEADS
    return pl.pallas_call(
        _mlstm_kernel,
        out_shape=(jax.ShapeDtypeStruct((bsz, ttot, nb), BF16),) * 2,
        grid=(bsz // sb, nt),
        in_specs=[fwd(nb), fwd(nb), fwd(nb), gfwd, fwd(LANES), rev(nb), rev(nb), rev(nb), grev, rev(LANES)],
        out_specs=(fwd(nb), rev(nb)),
        scratch_shapes=[pltpu.VMEM((nstate, MLSTM_DH, 2 * MLSTM_DH), F32),
                        pltpu.VMEM((nstate, SUBLANES, LANES), F32)],
        compiler_params=_params("parallel", "arbitrary"),
        name="mlstm_scan",
    )(mq, mk, mv, g_rows, g_cols, mq, mk, mv, g_rows, g_cols)


def _head_rms(x, n_heads):
    w = x.shape[1] // n_heads
    return jnp.concatenate([_rms(x[:, i * w:(i + 1) * w]) for i in range(n_heads)], axis=1)


def _mid_kernel(ctx_ref, lat_ref, gate_ref, a_ref, hf_ref, hr_ref, mo_ref, z0_ref, hn_ref, wout_ref,
                sc_ref, sh_ref, g_ref, wall_ref, wgate_ref, bgate_ref,
                h_out, gq_out, gk_out, gv_out, bc_out, nq_out, nk_out, nv_out, z_out, *, nct):
    nbt = ctx_ref.shape[0]
    rows = lambda f: jnp.concatenate([f(bi) for bi in range(nbt)], axis=0)
    f32 = lambda r, bi: r[bi].astype(F32)

    def put(ref, val):
        for bi in range(nbt):
            ref[bi] = val[bi * TM:(bi + 1) * TM]

    hm = rows(lambda bi: _head_rms(f32(mo_ref, bi) * (f32(hf_ref, bi) + f32(hr_ref, bi)), MLSTM_HEADS) * hn_ref[...])
    cat = (jnp.concatenate([rows(lambda bi: f32(a_ref, bi)), hm], axis=1) * rows(lambda bi: f32(z0_ref, bi))).astype(BF16)
    y = _dot(cat, wout_ref[...])
    is_ctx = pl.program_id(1) < nct
    hs = [jnp.where(is_ctx, ctx_ref[bi], lat_ref[bi]) + gate_ref[bi] * y[bi * TM:(bi + 1) * TM] for bi in range(nbt)]
    for bi in range(nbt):
        h_out[bi] = hs[bi]

    ub = rows(lambda bi: _modnorm(hs[bi], g_ref[...], sc_ref[bi], sh_ref[bi])).astype(BF16)
    wseg = lambda name: wall_ref[:, PROJ1_COLS[name][0]:PROJ1_COLS[name][1]]
    ga = _dot(ub, wseg("ga")).astype(BF16)
    gate_pre = _dot(ga, wgate_ref[...])
    put(gq_out, (_dot(ub, wseg("q")) * GLA_DK ** -0.5).astype(BF16))
    put(gk_out, _dot(ub, wseg("k")).astype(BF16))
    put(gv_out, _dot(ub, wseg("v")).astype(BF16))

    lg = _log_sigmoid(gate_pre + bgate_ref[...]) * (1.0 / GLA_TAU)
    hi, mid, lo = _split3(lg)
    put(nq_out, (_dot(ub, wseg("nq")) * (NA_DH ** -0.5 * LOG2E)).astype(BF16))
    put(nk_out, _dot(ub, wseg("nk")).astype(BF16))
    put(nv_out, _dot(ub, wseg("nv")).astype(BF16))
    row = lax.broadcasted_iota(jnp.int32, (TM, TM), 0)
    col = lax.broadcasted_iota(jnp.int32, (TM, TM), 1)
    same = (row // CHUNK) == (col // CHUNK)
    nk = GLA_HEADS * GLA_DK
    for d in range(2):
        tri = jnp.where(jnp.logical_and(same, (col <= row) if d == 0 else (col >= row)), 1.0, 0.0).astype(BF16)
        cols = slice(d * nk, (d + 1) * nk)
        for bi in range(nbt):
            rs = slice(bi * TM, (bi + 1) * TM)
            bc_out[bi, :, cols] = (_dot(tri, hi[rs, cols]) + _dot(tri, mid[rs, cols]) + _dot(tri, lo[rs, cols])) * LOG2E
    put(z_out, _silu(_dot(ub, wseg("z"))).astype(BF16))


def _proj1_weights(w_in, w_gate, b_gate):
    nk = GLA_HEADS * GLA_DK
    nc = GLA_HEADS * GLA_DV
    nd = NA_HEADS * NA_DH
    o = np.cumsum([0, nk, nk, nc, 2 * GLA_GATE_RANK, nd, nd, nd, nc + nd])
    r = GLA_GATE_RANK
    c = PROJ1_COLS
    wall = _relayout(w_in.T, [(0, o[3], 0), (o[3], 2 * r, c["ga"][0]), (o[4], o[8] - o[4], c["nq"][0])], c["z"][1])
    wgate = jnp.zeros((LANES, 2 * nk), F32).at[:r, :nk].set(w_gate[0]).at[r:2 * r, nk:].set(w_gate[1])
    return dict(wall=wall, wgate=wgate.astype(BF16), bgate=b_gate.reshape(1, 2 * nk))


def _mid(ctx, x, gate0, a, hf, hr, mo, z0, h_norm, w_out, scale, shift, norm_g, w, nct):
    bsz = x.shape[0]
    ttot = ctx.shape[1] + x.shape[1]
    nt = ttot // TM
    sb = math.gcd(bsz, TILE_BATCH)
    tile, src_ctx, src_lat, _, _, mod = _tile_specs(nct, nt, sb)
    nb = MLSTM_HEADS * MLSTM_DH
    nk = GLA_HEADS * GLA_DK
    nc = GLA_HEADS * GLA_DV
    nd = NA_HEADS * NA_DH
    ws = [w[n] for n in ("wall", "wgate", "bgate")]
    tok = lambda width, dt: jax.ShapeDtypeStruct((bsz, ttot, width), dt)
    return pl.pallas_call(
        functools.partial(_mid_kernel, nct=nct),
        out_shape=(tok(D_MODEL, F32), tok(nk, BF16), tok(nk, BF16), tok(nc, BF16), tok(2 * nk, F32),
                   tok(nd, BF16), tok(nd, BF16), tok(nd, BF16), tok(nc + nd, BF16)),
        grid=(bsz // sb, nt),
        in_specs=[src_ctx, src_lat, mod,
                  pl.BlockSpec((sb, TM, MLA_HEADS * MLA_V), lambda b, t: (b, _mla_tile(t, nct), 0)),
                  tile(nb), tile(nb), tile(nb), tile(D_MODEL),
                  _full((1, nb)), _full(w_out.shape), mod, mod, _full((1, D_MODEL))] + [_full(a.shape) for a in ws],
        out_specs=(tile(D_MODEL), tile(nk), tile(nk), tile(nc), tile(2 * nk), tile(nd), tile(nd), tile(nd),
                   tile(nc + nd)),
        compiler_params=_params("parallel", "parallel"),
        name="out0_proj1",
    )(ctx, x, gate0, a, hf, hr, mo, z0, h_norm[None], w_out.astype(BF16), scale, shift, norm_g[None], *ws)


def _gla_kernel(qf_ref, kf_ref, vf_ref, bf_ref, qr_ref, kr_ref, vr_ref, br_ref, of_ref, or_ref, s_scr):
    @pl.when(pl.program_id(1) == 0)
    def _():
        s_scr[...] = jnp.zeros_like(s_scr)

    L = CHUNK
    nchunk = TM // L
    npair = GLA_HEADS // 2
    lane = lax.broadcasted_iota(jnp.int32, (L, LANES), 1)
    rowv = lax.broadcasted_iota(jnp.int32, (L, LANES), 0)
    srow = lax.broadcasted_iota(jnp.int32, (LANES, 2 * GLA_DV), 0)
    scol = lax.broadcasted_iota(jnp.int32, (LANES, 2 * GLA_DV), 1)
    own_block = (srow < GLA_DK) == (scol < GLA_DV)
    vcol = lax.broadcasted_iota(jnp.int32, (L, 2 * GLA_DV), 1)
    e_row = lax.broadcasted_iota(jnp.int32, (LANES, LANES), 0)
    e_col = lax.broadcasted_iota(jnp.int32, (LANES, LANES), 1)
    head_ones = jnp.where((e_row < GLA_DK) == (e_col < GLA_DK), 1.0, 0.0).astype(BF16)
    leaf_idx = lax.broadcasted_iota(jnp.int32, (GLA_LEAF * GLA_LEAF, LANES), 0)
    leaf_lane = lax.broadcasted_iota(jnp.int32, (GLA_LEAF * GLA_LEAF, LANES), 1) % L
    leaf_s, leaf_t = leaf_idx // GLA_LEAF, leaf_idx % GLA_LEAF
    rep = lambda a: jnp.concatenate(
        [jnp.broadcast_to(a[s:s + 1], (GLA_LEAF, a.shape[1])) for s in range(GLA_LEAF)], axis=0)
    til = lambda a: jnp.concatenate([a] * GLA_LEAF, axis=0)

    def chunk_step(ci, carry):
        cs = []
        for bi, d in [(bi, d) for bi in range(qf_ref.shape[0]) for d in range(2)]:
            q_ref, k_ref, v_ref, b_ref, o_ref = (
                r.at[bi] for r in ((qf_ref, kf_ref, vf_ref, bf_ref, of_ref), (qr_ref, kr_ref, vr_ref, br_ref, or_ref))[d])
            cd = ci if d == 0 else nchunk - 1 - ci
            base = pl.multiple_of(cd * L, L)
            for p in range(npair):
                ls = slice(p * LANES, (p + 1) * LANES)
                bc = b_ref[pl.ds(base, L), ls]
                cs.append(dict(
                    d=d, p=p, bi=bi, base=base, o_ref=o_ref, bc=bc,
                    tau_v=rowv if d == 0 else L - 1 - rowv,
                    tau_s=(lane % L) if d == 0 else L - 1 - (lane % L),
                    q=q_ref[pl.ds(base, L), ls].astype(F32),
                    k=k_ref[pl.ds(base, L), ls].astype(F32),
                    v=v_ref[pl.ds(base, L), p * 2 * GLA_DV:(p + 1) * 2 * GLA_DV],
                    b_last=bc[L - 1:L] if d == 0 else bc[0:1],
                    state=s_scr[bi, d, p]))

        for c in cs:
            c["o_pair"] = _dot((c["q"] * jnp.exp2(c["bc"])).astype(BF16), c["state"].astype(BF16))

        for c in cs:
            c["att"] = jnp.zeros((L, LANES), F32)
        bs = L // 2
        while bs >= GLA_LEAF:
            for c in cs:
                d, bc, tau_v = c["d"], c["bc"], c["tau_v"]
                later = (tau_v // bs) % 2 == 1
                ref_b = None
                for blk in range(L // (2 * bs)):
                    tau_ref = blk * 2 * bs + bs - 1
                    idx = tau_ref if d == 0 else L - 1 - tau_ref
                    r = jnp.broadcast_to(bc[idx:idx + 1], (L, LANES))
                    ref_b = r if ref_b is None else jnp.where(tau_v // (2 * bs) == blk, r, ref_b)
                c["qs"] = jnp.where(later, c["q"] * jnp.exp2(jnp.where(later, bc - ref_b, 0.0)), 0.0).astype(BF16)
                ks = jnp.where(later, 0.0, c["k"] * jnp.exp2(jnp.where(later, 0.0, ref_b - bc)))
                c["ks2"] = jnp.concatenate([jnp.where(lane < GLA_DK, ks, 0.0), jnp.where(lane < GLA_DK, 0.0, ks)],
                                           axis=0).astype(BF16)
            for c in cs:
                same_parent = (c["tau_v"] // (2 * bs)) == (c["tau_s"] // (2 * bs))
                c["att"] = c["att"] + jnp.where(same_parent, _dot_nt(c["qs"], c["ks2"]), 0.0)
            bs //= 2

        for c in cs:
            c["leaf"] = []
        for blk in range(L // GLA_LEAF):
            rs = slice(blk * GLA_LEAF, (blk + 1) * GLA_LEAF)
            for c in cs:
                qb, bb, kb = c["q"][rs], c["bc"][rs], c["k"][rs]
                ok = (leaf_t >= leaf_s) if c["d"] == 0 else (leaf_t <= leaf_s)
                w = jnp.where(ok, til(qb) * jnp.exp2(til(bb) - rep(bb)) * rep(kb), 0.0)
                c["a"] = _dot(w.astype(BF16), head_ones)
            for c in cs:
                a = jnp.where(leaf_lane == leaf_s + blk * GLA_LEAF, c["a"], 0.0)
                acc = a[0:GLA_LEAF]
                for s in range(1, GLA_LEAF):
                    acc = acc + a[s * GLA_LEAF:(s + 1) * GLA_LEAF]
                c["leaf"].append(acc)

        for c in cs:
            v = c["v"]
            c["att"] = (c["att"] + jnp.concatenate(c["leaf"], axis=0)).astype(BF16)
            c["v_blk"] = jnp.concatenate([jnp.where(vcol < GLA_DV, v, jnp.zeros_like(v)),
                                          jnp.where(vcol < GLA_DV, jnp.zeros_like(v), v)], axis=0)
            c["ke"] = (c["k"] * jnp.exp2(c["b_last"] - c["bc"])).astype(BF16)
            c["decay_col"] = jnp.sum(
                jnp.where(e_row == e_col, jnp.broadcast_to(jnp.exp2(c["b_last"]), (LANES, LANES)), 0.0),
                axis=1, keepdims=True)
        for c in cs:
            c["o"] = c["o_pair"] + _dot(c["att"], c["v_blk"])
            c["ds"] = _dot_tn(c["ke"], c["v"])
        for c in cs:
            p = c["p"]
            c["o_ref"][pl.ds(c["base"], L), p * 2 * GLA_DV:(p + 1) * 2 * GLA_DV] = c["o"].astype(BF16)
            s_scr[c["bi"], c["d"], p] = c["decay_col"] * c["state"] + jnp.where(own_block, c["ds"], 0.0)
        return carry

    lax.fori_loop(0, nchunk, chunk_step, 0)


def _gla(gq, gk, gv, bc, nct):
    bsz, ttot, nk = gq.shape
    nc = gv.shape[2]
    nt = ttot // TM
    sb = math.gcd(bsz, GLA_BATCH)
    fwd = lambda w: pl.BlockSpec((sb, TM, w), lambda b, j: (b, j, 0))
    rev = lambda w: pl.BlockSpec((sb, TM, w), lambda b, j: (b, _scan_block(j, nct, nt), 0))
    bfwd = pl.BlockSpec((sb, TM, nk), lambda b, j: (b, j, 0))
    brev = pl.BlockSpec((sb, TM, nk), lambda b, j: (b, _scan_block(j, nct, nt), 1))
    return pl.pallas_call(
        _gla_kernel,
        out_shape=(jax.ShapeDtypeStruct((bsz, ttot, nc), BF16),) * 2,
        grid=(bsz // sb, nt),
        in_specs=[fwd(nk), fwd(nk), fwd(nc), bfwd, rev(nk), rev(nk), rev(nc), brev],
        out_specs=(fwd(nc), rev(nc)),
        scratch_shapes=[pltpu.VMEM((sb, 2, GLA_HEADS // 2, LANES, 2 * GLA_DV), F32)],
        compiler_params=_params("parallel", "arbitrary"),
        name="gla_scan",
    )(gq, gk, gv, bc, gq, gk, gv, bc)


def _na_bias_tables(rpb, rows):
    kh = NA_KH
    c = np.arange(GRID_W)
    cs = np.clip(c - NA_KW // 2, 0, GRID_W - NA_KW)
    col_ok = (c[None, :] >= cs[:, None]) & (c[None, :] < cs[:, None] + NA_KW)
    pad = GRID_W - NA_KW
    rpb_pad = jnp.pad(rpb.astype(F32) * LOG2E, ((0, 0), (0, 0), (pad, pad)))
    period = 2 * GRID_W
    flat = jnp.tile(jnp.pad(rpb_pad, ((0, 0), (0, 0), (0, 1))), (1, 1, GRID_W))[:, :, :GRID_W * (period - 1)]
    t1 = flat.reshape(NA_HEADS, 2 * NA_KH - 1, GRID_W, period - 1)[..., GRID_W - 1:]
    t1 = jnp.where(col_ok, t1, NEG_BIG)
    masked = jnp.full((NA_HEADS, GRID_W, GRID_W), NEG_BIG, F32)
    per_row = []
    for r0 in (0, NA_QROWS, rows - NA_QROWS):
        kb = int(np.clip(r0 - kh // 2, 0, rows - NA_KROWS))
        for qrow in range(r0, r0 + NA_QROWS):
            ws = int(np.clip(qrow - kh // 2, 0, rows - kh))
            per_row.append(jnp.concatenate(
                [t1[:, krow - qrow + NA_KH - 1] if ws <= krow < ws + kh else masked
                 for krow in range(kb, kb + NA_KROWS)], axis=-1))
    return jnp.concatenate(per_row, axis=1)


def _na_heads(q_ref, kw_ref, vw_ref, kc_ref, vc_ref, bias_ref):
    nq = NA_QROWS * GRID_W
    lane = lax.broadcasted_iota(jnp.int32, (nq, LANES), 1)
    def scores(j):
        ls = slice((j // 2) * LANES, (j // 2 + 1) * LANES)
        q = q_ref[:, ls]
        qj = jnp.where((lane < NA_DH) == (j % 2 == 0), q, jnp.zeros_like(q))
        return _dot_nt(qj, kw_ref[:, ls]) + bias_ref[j], _dot_nt(qj, kc_ref[:, ls])

    outs = []
    nxt = scores(0)
    for j in range(NA_HEADS):
        s_loc, s_ctx = nxt
        if j + 1 < NA_HEADS:
            nxt = scores(j + 1)
        ls = slice((j // 2) * LANES, (j // 2 + 1) * LANES)
        m = jnp.maximum(jnp.max(s_loc, axis=-1, keepdims=True), jnp.max(s_ctx, axis=-1, keepdims=True))
        p_loc = jnp.exp2(s_loc - m)
        p_ctx = jnp.exp2(s_ctx - m)
        den = jnp.sum(p_loc, axis=-1, keepdims=True) + jnp.sum(p_ctx, axis=-1, keepdims=True)
        outs.append((_dot(p_loc.astype(BF16), vw_ref[:, ls]) + _dot(p_ctx.astype(BF16), vc_ref[:, ls])) / den)
    return jnp.concatenate([jnp.where(lane < NA_DH, outs[2 * i], outs[2 * i + 1]) for i in range(NA_HEADS // 2)], axis=1)


def _na_out1_kernel(*refs, nb):
    kv = refs[:4 * nb]
    q_ref, bias_ref, h_ref, gate_ref, of_ref, or_ref, z_ref, gn_ref, w_ref, fn_ref, o_ref = refs[4 * nb:]
    rows = lambda f: jnp.concatenate([f(bi) for bi in range(nb)], axis=0)
    f32 = lambda r, bi: r[bi].astype(F32)
    na = rows(lambda bi: _na_heads(q_ref.at[bi], *kv[4 * bi:4 * bi + 4], bias_ref))
    g = rows(lambda bi: _head_rms(f32(of_ref, bi) + f32(or_ref, bi), GLA_HEADS) * gn_ref[...])
    cat = (jnp.concatenate([g, na], axis=1) * rows(lambda bi: f32(z_ref, bi))).astype(BF16)
    y = _dot(cat, w_ref[...])
    for bi in range(nb):
        h = h_ref[bi] + gate_ref[bi] * y[bi * TM:(bi + 1) * TM]
        o_ref[bi] = _rms(h) * fn_ref[...]


def _na_out1(nq, nk, nv, bias, h, gate, of, orv, z, gla_norm, w_out, final_norm, n_ctx, rows):
    bsz, ttot, nd = nq.shape
    nrb = rows // NA_QROWS
    qrows = NA_QROWS * GRID_W
    krows = NA_KROWS * GRID_W
    assert qrows == TM
    nct = n_ctx // TM
    nc = GLA_HEADS * GLA_DV
    sb = math.gcd(bsz, TILE_BATCH)

    def variant(rb):
        return jnp.where(rb == 0, 0, jnp.where(rb == nrb - 1, 2, 1))

    def key_start(rb):
        first_row = jnp.clip(rb * NA_QROWS - NA_KH // 2, 0, rows - NA_KROWS)
        return pl.multiple_of(n_ctx + first_row * GRID_W, GRID_W)

    window = lambda i: pl.BlockSpec((None, pl.Element(krows), pl.Element(nd)), lambda b, rb: (b * sb + i, key_start(rb), 0))
    context = lambda i: pl.BlockSpec((None, pl.Element(n_ctx), pl.Element(nd)), lambda b, rb: (b * sb + i, 0, 0))
    kv_specs = [spec(i) for i in range(sb) for spec in (window, window, context, context)]
    lat = lambda w: pl.BlockSpec((sb, TM, w), lambda b, rb: (b, rb + nct, 0))
    return pl.pallas_call(
        functools.partial(_na_out1_kernel, nb=sb),
        out_shape=jax.ShapeDtypeStruct((bsz, rows * GRID_W, D_MODEL), F32),
        grid=(bsz // sb, nrb),
        in_specs=kv_specs + [
            lat(nd), pl.BlockSpec((NA_HEADS, qrows, krows), lambda b, rb: (0, variant(rb), 0)),
            lat(D_MODEL), pl.BlockSpec((sb, None, 1, D_MODEL), lambda b, rb: (b, 1, 0, 0)),
            lat(nc), lat(nc), lat(nc + nd), _full((1, nc)), _full(w_out.shape), _full((1, D_MODEL))],
        out_specs=pl.BlockSpec((sb, TM, D_MODEL), lambda b, rb: (b, rb, 0)),
        compiler_params=_params("parallel", "parallel"),
        name="na_out1",
    )(*([nk, nv, nk, nv] * sb), nq, bias, h, gate, of, orv, z, gla_norm[None], w_out.astype(BF16), final_norm[None])


def kernel(x, c, ctx, c_ctx, l0_norm, l0_w_mod, l0_b_mod, l0_w_in, l0_mla_q_norm, l0_mla_w_uq, l0_mla_kv_norm, l0_mla_w_ukv, l0_mlstm_conv_w, l0_mlstm_conv_b, l0_mlstm_b_i, l0_mlstm_b_f, l0_mlstm_norm, l0_w_out, l1_norm, l1_w_mod, l1_b_mod, l1_w_in, l1_gla_w_gate, l1_gla_b_gate, l1_gla_norm, l1_na_rpb, l1_w_out, final_norm):
    bsz, seq, d = x.shape
    n_ctx = ctx.shape[1]
    rows = seq // GRID_W
    assert d == D_MODEL and seq % TM == 0 and n_ctx == TM
    assert rows % NA_QROWS == 0 and rows >= NA_KROWS and rows // NA_QROWS >= 3
    nct = n_ctx // TM

    (shift0, scale0, gate0), (shift, scale, gate) = _mod_vectors(
        c, c_ctx, ((l0_w_mod, l0_b_mod), (l1_w_mod, l1_b_mod)))

    cos_t, sin_t = _rope_tables(n_ctx, seq)
    w0 = _proj0_weights(l0_w_in, l0_mla_q_norm, l0_mla_w_uq, l0_mla_kv_norm, l0_mla_w_ukv,
                        l0_mlstm_conv_w, l0_mlstm_conv_b, l0_mlstm_b_i, l0_mlstm_b_f)
    q, k, v, mq, mk, mv, g_rows, g_cols, mo, z0 = _proj0(ctx, x, scale0, shift0, l0_norm, cos_t, sin_t, w0, nct)
    a = _mla_attention(q, k, v, nct)
    hf, hr = _mlstm(mq, mk, mv, g_rows, g_cols, nct)

    w1 = _proj1_weights(l1_w_in, l1_gla_w_gate, l1_gla_b_gate)
    h, gq, gk, gv, bc, nq, nk, nv, z = _mid(ctx, x, gate0, a, hf, hr, mo, z0, l0_mlstm_norm, l0_w_out,
                                            scale, shift, l1_norm, w1, nct)
    of, orv = _gla(gq, gk, gv, bc, nct)
    return _na_out1(nq, nk, nv, _na_bias_tables(l1_na_rpb, rows), h, gate, of, orv, z,
                    l1_gla_norm, l1_w_out, final_norm, n_ctx, rows)
```

```python
import functools
import math

import jax
import jax.numpy as jnp
import numpy as np
from jax import lax
from jax.experimental import pallas as pl
from jax.experimental.pallas import tpu as pltpu

F32 = jnp.float32
BF16 = jnp.bfloat16

D_MODEL = 1024
GRID_W = 64
EPS = 1e-6
ROPE_BASE = 10000.0

MLA_HEADS = 8
MLA_Q_RANK = 384
MLA_KV_RANK = 256
MLA_NOPE = 64
MLA_ROPE = 32
MLA_V = 64
ROPE_PAIRS = MLA_ROPE // 4
MLSTM_HEADS = 4
MLSTM_DH = 128
MLSTM_CONV = 3
GLA_HEADS = 4
GLA_DK = 64
GLA_DV = 128
GLA_GATE_RANK = 16
GLA_TAU = 16.0
NA_HEADS = 8
NA_DH = 64
NA_KH = 8
NA_KW = 16

CHUNK = 64
MLSTM_CHUNK = 128
TM = 256
LANES = 128
SUBLANES = 8
HALO = SUBLANES
NA_QROWS = 4
NA_KROWS = NA_QROWS + NA_KH - 1
GLA_LEAF = 8
MLA_HPS = 4
MLA_QTILES = 2
TILE_BATCH = 2
MLSTM_BATCH = 2
GLA_BATCH = 8
VMEM_LIMIT = 56 * 1024 * 1024
NEG_BIG = -1e30
LOG2E = 1.4426950408889634


def _dot(a, b):
    return jnp.dot(a, b, preferred_element_type=F32)


def _dot_nt(a, b):
    return lax.dot_general(a, b, (((1,), (1,)), ((), ())), preferred_element_type=F32)


def _dot_tn(a, b):
    return lax.dot_general(a, b, (((0,), (0,)), ((), ())), preferred_element_type=F32)


def _rms(x):
    return x * lax.rsqrt(jnp.mean(x * x, axis=-1, keepdims=True) + EPS)


def _sigmoid(x):
    return 1.0 / (1.0 + jnp.exp(-x))


def _silu(x):
    return x * _sigmoid(x)


def _log_sigmoid(x):
    return jnp.minimum(x, 0.0) - jnp.log(1.0 + jnp.exp(-jnp.abs(x)))


def _params(*sem):
    return pltpu.CompilerParams(dimension_semantics=sem, vmem_limit_bytes=VMEM_LIMIT)


def _full(shape):
    nd = len(shape)
    return pl.BlockSpec(shape, lambda *_: (0,) * nd)


def _mod_kernel(c_ref, w0_ref, b0_ref, w1_ref, b1_ref, o0_ref, o1_ref):
    sc = _silu(c_ref[...]).astype(BF16)
    o0_ref[...] = _dot(sc, w0_ref[...].astype(BF16)) + b0_ref[...]
    o1_ref[...] = _dot(sc, w1_ref[...].astype(BF16)) + b1_ref[...]


def _mod_vectors(c, c_ctx, mods):
    bsz, d = c.shape
    rows = -(-(bsz + 1) // SUBLANES) * SUBLANES
    cc = jnp.zeros((rows, d), F32).at[:bsz].set(c).at[bsz].set(c_ctx)
    (w0, b0), (w1, b1) = mods
    n = w0.shape[1]
    wspec = pl.BlockSpec((d, d), lambda j: (0, j))
    vspec = pl.BlockSpec((1, d), lambda j: (0, j))
    ospec = pl.BlockSpec((rows, d), lambda j: (0, j))
    per_layer = pl.pallas_call(
        _mod_kernel,
        out_shape=(jax.ShapeDtypeStruct((rows, n), F32),) * 2,
        grid=(n // d,),
        in_specs=[_full((rows, d)), wspec, vspec, wspec, vspec],
        out_specs=(ospec, ospec),
        compiler_params=_params("parallel"),
        name="modulation",
    )(cc, w0, b0[None], w1, b1[None])
    out = []
    for mod in per_layer:
        vecs = []
        for part in jnp.split(mod, 3, axis=-1):
            ctx_v = jnp.broadcast_to(part[bsz][None], (bsz, d))
            vecs.append(jnp.stack([ctx_v, part[:bsz]], axis=1)[:, :, None, :])
        out.append(vecs)
    return out


def _modnorm(x, g, scale, shift):
    return _rms(x) * g * (1.0 + scale) + shift


def _seg_scan(x, reverse, op, fill):
    width = x.shape[1]
    pos = lax.broadcasted_iota(jnp.int32, x.shape, 1) % MLSTM_CHUNK
    k = 1
    while k < MLSTM_CHUNK:
        if reverse:
            x = op(x, jnp.where(pos < MLSTM_CHUNK - k, pltpu.roll(x, width - k, axis=1), fill))
        else:
            x = op(x, jnp.where(pos >= k, pltpu.roll(x, k, axis=1), fill))
        k *= 2
    return x


def _split3(x):
    hi = x.astype(BF16)
    r1 = x - hi.astype(F32)
    mid = r1.astype(BF16)
    return hi, mid, (r1 - mid.astype(F32)).astype(BF16)


def _proj0_kernel(ctx_ref, lat_ref, hp_ref, hn_ref, sc_ref, sh_ref, g_ref, cos_ref, sin_ref,
                  wall_ref, wgt_ref, qn_ref, wuq_ref, kvn_ref, wk_ref, wv_ref, cw_ref, cb_ref, gb_ref,
                  q_out, k_out, v_out, mq_out, mk_out, mv_out, gr_out, gc_out, mo_out, z_out,
                  pbuf, *, nct, nt):
    t = pl.program_id(1)
    h = jnp.where(t < nct, ctx_ref[...], lat_ref[...])
    x = jnp.concatenate([hp_ref[...], h, hn_ref[...]], axis=0)
    ub_ext = _modnorm(x, g_ref[...], sc_ref[...], sh_ref[...]).astype(BF16)
    ub = ub_ext[HALO:HALO + TM]

    gt = _dot_nt(wgt_ref[...], ub) + gb_ref[...]
    half = MLSTM_HEADS
    per_dir = []
    for d in range(2):
        gd = gt[d * SUBLANES:(d + 1) * SUBLANES]
        bcum = _seg_scan(_log_sigmoid(gd), d == 1, jnp.add, 0.0)
        b_top = pltpu.roll(bcum, half, axis=0)
        c8 = gd - b_top
        pm8 = _seg_scan(c8, d == 1, jnp.maximum, -jnp.inf)
        per_dir.append((c8, pm8, b_top))

    cos = cos_ref[...]
    sin = sin_ref[...]
    cos_all = jnp.concatenate([cos] * MLA_HEADS, axis=1)
    sin_all = jnp.concatenate([sin] * MLA_HEADS, axis=1)

    def swap_halves(a):
        lane = lax.broadcasted_iota(jnp.int32, a.shape, 1)
        first = lane % (2 * ROPE_PAIRS) < ROPE_PAIRS
        return jnp.where(first, pltpu.roll(a, a.shape[1] - ROPE_PAIRS, axis=1), pltpu.roll(a, ROPE_PAIRS, axis=1))

    wseg = lambda name: wall_ref[:, PROJ0_COLS[name][0]:PROJ0_COLS[name][1]]
    cqkr = _dot(ub, wseg("cqkr"))
    cq = cqkr[:, :MLA_Q_RANK]
    kr = cqkr[:, MLA_Q_RANK:]
    ckv = _dot(ub, wseg("ckv"))
    pqk = _dot(ub_ext, wseg("mqk"))

    nq = (_rms(cq) * qn_ref[...]).astype(BF16)
    nkv = (_rms(ckv) * kvn_ref[...]).astype(BF16)

    q = _dot(nq, wuq_ref[...])
    kn = _dot(nkv, wk_ref[...])
    vv = _dot(nkv, wv_ref[...])

    prev_ok = t > nct
    next_ok = jnp.logical_and(t >= nct, t != nt - 1)
    row = lax.broadcasted_iota(jnp.int32, pqk.shape, 0)
    keep = jnp.logical_and(jnp.logical_or(row >= HALO, prev_ok), jnp.logical_or(row < HALO + TM, next_ok))
    pbuf[...] = jnp.where(keep, pqk, 0.0)
    cw = cw_ref[...]
    y = (pbuf[HALO - 1:HALO - 1 + TM, :] * cw[0:1] + pbuf[HALO:HALO + TM, :] * cw[1:2]
         + pbuf[HALO + 1:HALO + 1 + TM, :] * cw[2:3] + cb_ref[...])
    y = _silu(y)
    nb = MLSTM_HEADS * MLSTM_DH
    mq_out[...] = (y[:, :nb] * MLSTM_DH ** -0.5).astype(BF16)
    mk_out[...] = y[:, nb:].astype(BF16)

    mv = _dot(ub, wseg("mv"))
    mo = _dot(ub, wseg("mo"))

    q = q * cos_all + swap_halves(q) * sin_all
    q_out[...] = (q * ((MLA_NOPE + MLA_ROPE) ** -0.5 * LOG2E)).astype(BF16)
    rot = kr * cos + swap_halves(kr) * sin
    k_out[...] = (kn + jnp.concatenate([rot] * MLA_HEADS, axis=1)).astype(BF16)
    vlane = lax.broadcasted_iota(jnp.int32, (TM, MLA_HEADS * LANES), 1)
    is_value = ((vlane % LANES) < MLA_V) == ((vlane // LANES) % 2 == 0)
    v_out[...] = jnp.where(is_value, vv, 1.0).astype(BF16)

    zz = _dot(ub, wseg("z"))
    mv_out[...] = mv.astype(BF16)
    mo_out[...] = _sigmoid(mo).astype(BF16)
    z_out[...] = _silu(zz).astype(BF16)

    r8 =lax.broadcasted_iota(jnp.int32, (SUBLANES, TM), 0)
    both = lambda i: jnp.where(r8 < half, per_dir[0][i], pltpu.roll(per_dir[1][i], half, axis=0)) * LOG2E
    c_rows, pm_rows, b_rows = both(0), both(1), both(2)
    for ci in range(TM // MLSTM_CHUNK):
        gr_out[ci] = c_rows[:, ci * MLSTM_CHUNK:(ci + 1) * MLSTM_CHUNK]
    stack = jnp.concatenate([c_rows, pm_rows, b_rows, jnp.zeros((LANES - 3 * SUBLANES, TM), F32)], axis=0)
    er = lax.broadcasted_iota(jnp.int32, (TM, TM), 0)
    ec = lax.broadcasted_iota(jnp.int32, (TM, TM), 1)
    eye = jnp.where(er == ec, 1.0, 0.0).astype(BF16)
    gc_out[...] = sum(_dot_nt(eye, part) for part in _split3(stack))


def _rope_tables(n_ctx, seq):
    t = jnp.arange(seq)
    inv = 1.0 / (ROPE_BASE ** (jnp.arange(ROPE_PAIRS, dtype=F32) / ROPE_PAIRS))
    ang = jnp.concatenate([(t // GRID_W)[:, None] * inv, (t % GRID_W)[:, None] * inv], axis=-1)
    cos, sin = jnp.cos(ang), jnp.sin(ang)
    j = np.arange(MLA_ROPE)
    src = (j // (2 * ROPE_PAIRS)) * ROPE_PAIRS + (j % ROPE_PAIRS)
    sign = np.where((j % (2 * ROPE_PAIRS)) < ROPE_PAIRS, -1.0, 1.0).astype(np.float32)
    cos_full = cos[:, src]
    sin_full = sin[:, src] * sign
    cos_t = jnp.ones((n_ctx + seq, LANES), F32).at[n_ctx:, MLA_NOPE:MLA_NOPE + MLA_ROPE].set(cos_full)
    sin_t = jnp.zeros((n_ctx + seq, LANES), F32).at[n_ctx:, MLA_NOPE:MLA_NOPE + MLA_ROPE].set(sin_full)
    return cos_t, sin_t


def _relayout_kernel(wt_ref, o_ref, *, moves):
    lane = lax.broadcasted_iota(jnp.int32, (o_ref.shape[0], LANES), 1)
    end = 0
    for src, n, dst in moves:
        group = dst - dst % LANES
        if group > end:
            o_ref[:, end:group] = jnp.zeros((o_ref.shape[0], group - end), o_ref.dtype)
        if n % LANES == 0 and dst == group:
            o_ref[:, dst:dst + n] = wt_ref[src:src + n, :].T.astype(o_ref.dtype)
            end = dst + n
        else:
            assert n < LANES and dst + n <= group + LANES and src + LANES <= wt_ref.shape[0]
            t = pltpu.roll(wt_ref[src:src + LANES, :].T, dst - group, axis=1)
            keep = (lane >= dst - group) & (lane < dst - group + n)
            o_ref[:, group:group + LANES] = jnp.where(keep, t, 0.0).astype(o_ref.dtype)
            end = group + LANES
    if end < o_ref.shape[1]:
        o_ref[:, end:] = jnp.zeros((o_ref.shape[0], o_ref.shape[1] - end), o_ref.dtype)


def _relayout(wt, moves, width):
    rows = wt.shape[1]
    rb = math.gcd(rows, TM)
    return pl.pallas_call(
        functools.partial(_relayout_kernel, moves=tuple(moves)),
        out_shape=jax.ShapeDtypeStruct((rows, width), BF16),
        grid=(rows // rb,),
        in_specs=[pl.BlockSpec((wt.shape[0], rb), lambda i: (0, i))],
        out_specs=pl.BlockSpec((rb, width), lambda i: (i, 0)),
        compiler_params=_params("parallel"),
        name="weight_relayout",
    )(wt)


def _segments(widths):
    out, start = {}, 0
    for name, width in widths:
        out[name] = (start, start + width)
        start += width
    return out


PROJ0_COLS = _segments([("cqkr", MLA_Q_RANK + LANES), ("ckv", MLA_KV_RANK), ("mqk", 2 * MLSTM_HEADS * MLSTM_DH),
                        ("mv", MLSTM_HEADS * MLSTM_DH), ("mo", MLSTM_HEADS * MLSTM_DH), ("z", D_MODEL)])
PROJ1_COLS = _segments([("q", GLA_HEADS * GLA_DK), ("k", GLA_HEADS * GLA_DK), ("v", GLA_HEADS * GLA_DV),
                        ("ga", LANES), ("nq", NA_HEADS * NA_DH), ("nk", NA_HEADS * NA_DH), ("nv", NA_HEADS * NA_DH),
                        ("z", GLA_HEADS * GLA_DV + NA_HEADS * NA_DH)])


def _proj0_weights(w_in, q_norm, w_uq, kv_norm, w_ukv, conv_w, conv_b, b_i, b_f):
    o = np.cumsum([0, MLA_Q_RANK, MLA_KV_RANK, MLA_ROPE] + [MLSTM_HEADS * MLSTM_DH] * 4 + [4 * MLSTM_HEADS, D_MODEL])
    dq = MLA_NOPE + MLA_ROPE
    wuq = jnp.pad(w_uq.reshape(MLA_Q_RANK, MLA_HEADS, dq), ((0, 0), (0, 0), (0, LANES - dq)))
    wkv = w_ukv.reshape(MLA_KV_RANK, MLA_HEADS, MLA_NOPE + MLA_V)
    wk = jnp.pad(wkv[:, :, :MLA_NOPE], ((0, 0), (0, 0), (0, LANES - MLA_NOPE)))
    zv = jnp.zeros((MLA_KV_RANK, MLA_HEADS // 2, MLA_V), F32)
    wv = jnp.concatenate([wkv[:, 0::2, MLA_NOPE:], zv, zv, wkv[:, 1::2, MLA_NOPE:]], axis=-1)
    wv = wv.reshape(MLA_KV_RANK, MLA_HEADS * LANES)
    gbias = jnp.stack([b_i, b_f], axis=1).reshape(4 * MLSTM_HEADS, 1)
    bf = lambda a: a.astype(BF16)
    c = PROJ0_COLS
    wall = _relayout(w_in.T, [(o[0], MLA_Q_RANK, c["cqkr"][0]), (o[2], MLA_ROPE, c["cqkr"][0] + MLA_Q_RANK + MLA_NOPE),
                            (o[1], MLA_KV_RANK, c["ckv"][0]), (o[3], o[7] - o[3], c["mqk"][0]),
                            (o[8], D_MODEL, c["z"][0])], c["z"][1])
    return dict(
        wall=wall, wgt=bf(w_in[:, o[7]:o[8]].T),
        qn=q_norm[None], wuq=bf(wuq.reshape(MLA_Q_RANK, -1)),
        kvn=kv_norm[None], wk=bf(wk.reshape(MLA_KV_RANK, -1)), wv=bf(wv),
        cw=jnp.zeros((SUBLANES, conv_w.shape[1]), F32).at[:MLSTM_CONV].set(conv_w), cb=conv_b[None], gb=gbias)


def _tile_specs(nct, nt, nb):
    tile = lambda w: pl.BlockSpec((nb, TM, w), lambda b, t: (b, t, 0))
    per = TM // HALO
    nlat = nt - nct
    src_ctx = pl.BlockSpec((nb, TM, D_MODEL), lambda b, t: (b, jnp.minimum(t, nct - 1), 0))
    src_lat = pl.BlockSpec((nb, TM, D_MODEL), lambda b, t: (b, jnp.maximum(t - nct, 0), 0))
    prev = pl.BlockSpec((nb, HALO, D_MODEL), lambda b, t: (b, jnp.maximum((t - nct) * per - 1, 0), 0))
    nxt = pl.BlockSpec((nb, HALO, D_MODEL),
                       lambda b, t: (b, jnp.clip((t - nct + 1) * per, 0, nlat * per - 1), 0))
    mod = pl.BlockSpec((nb, None, 1, D_MODEL), lambda b, t: (b, (t >= nct).astype(jnp.int32), 0, 0))
    return tile, src_ctx, src_lat, prev, nxt, mod


def _per_element(tile_kernel, shared):
    def kernel(*refs, **kw):
        for bi in range(refs[0].shape[0]):
            tile_kernel(*[r if i in shared else r.at[bi] for i, r in enumerate(refs)], **kw)
    return kernel


def _proj0(ctx, x, scale, shift, norm_g, cos_t, sin_t, w, nct):
    bsz = x.shape[0]
    ttot = ctx.shape[1] + x.shape[1]
    nt = ttot // TM
    nb = MLSTM_HEADS * MLSTM_DH
    sb = math.gcd(bsz, TILE_BATCH)
    tile, src_ctx, src_lat, prev, nxt, mod = _tile_specs(nct, nt, sb)
    tab = pl.BlockSpec((TM, LANES), lambda b, t: (t, 0))
    wnames = ["wall", "wgt", "qn", "wuq", "kvn", "wk", "wv", "cw", "cb", "gb"]
    ws = [w[n] for n in wnames]
    tok = lambda width, dt: jax.ShapeDtypeStruct((bsz, ttot, width), dt)
    out_shape = (tok(MLA_HEADS * LANES, BF16), tok(MLA_HEADS * LANES, BF16), tok(MLA_HEADS * LANES, BF16),
                 tok(nb, BF16), tok(nb, BF16), tok(nb, BF16),
                 jax.ShapeDtypeStruct((bsz, ttot // MLSTM_CHUNK, 2 * MLSTM_HEADS, MLSTM_CHUNK), F32), tok(LANES, F32),
                 tok(nb, BF16), tok(D_MODEL, BF16))
    out_specs = (tile(MLA_HEADS * LANES), tile(MLA_HEADS * LANES), tile(MLA_HEADS * LANES),
                 tile(nb), tile(nb), tile(nb),
                 pl.BlockSpec((sb, TM // MLSTM_CHUNK, 2 * MLSTM_HEADS, MLSTM_CHUNK), lambda b, t: (b, t, 0, 0)),
                 tile(LANES),
                 tile(nb), tile(D_MODEL))
    n_blocked, n_out = 6, len(out_shape)
    n_in = n_blocked + 3 + len(ws)
    shared = set(range(n_blocked, n_in)) | {n_in + n_out}
    return pl.pallas_call(
        functools.partial(_per_element(_proj0_kernel, shared), nct=nct, nt=nt),
        out_shape=out_shape,
        grid=(bsz // sb, nt),
        in_specs=[src_ctx, src_lat, prev, nxt, mod, mod, _full((1, D_MODEL)), tab, tab] + [_full(a.shape) for a in ws],
        out_specs=out_specs,
        scratch_shapes=[pltpu.VMEM((TM + 2 * HALO, 2 * nb), F32)],
        compiler_params=_params("parallel", "parallel"),
        name="proj0",
    )(ctx, x, x, x, scale, shift, norm_g[None], cos_t, sin_t, *ws)


def _mla_kernel(qa_ref, qb_ref, k_ref, v_ref, o_ref, *, nkeys_ctx):
    step = pl.program_id(2)

    def attend(queries, nkeys, rows):
        qk = lambda j: _dot_nt(queries(j), k_ref[0:nkeys, j * LANES:(j + 1) * LANES])
        outs = []
        s_next = qk(0)
        for j in range(MLA_HPS):
            s = s_next
            if j + 1 < MLA_HPS:
                s_next = qk(j + 1)
            p = jnp.exp2(s - jnp.max(s, axis=-1, keepdims=True))
            o = _dot(p.astype(BF16), v_ref[0:nkeys, j * LANES:(j + 1) * LANES])
            outs.append(o / pltpu.roll(o, MLA_V, axis=1))
        lane = lax.broadcasted_iota(jnp.int32, outs[0].shape, 1)
        for j in range(MLA_HPS // 2):
            o_ref[rows, j * LANES:(j + 1) * LANES] = jnp.where(lane < MLA_V, outs[2 * j], outs[2 * j + 1]).astype(BF16)

    @pl.when(step == 0)
    def _():
        attend(lambda j: qa_ref[:, j * LANES:(j + 1) * LANES], nkeys_ctx, slice(0, TM))
        o_ref[TM:, :] = jnp.zeros((o_ref.shape[0] - TM, o_ref.shape[1]), BF16)

    @pl.when(step > 0)
    def _():
        stacked = lambda j: jnp.concatenate([qa_ref[:, j * LANES:(j + 1) * LANES],
                                             qb_ref[:, j * LANES:(j + 1) * LANES]], axis=0)
        attend(stacked, k_ref.shape[0], slice(None))


def _mla_attention(q, k, v, nct):
    bsz, ttot, _ = q.shape
    nt = ttot // TM
    assert nct == 1 and MLA_QTILES == 2 and (nt - nct) % MLA_QTILES == 0
    hw = MLA_HPS * LANES
    nsteps = 1 + (nt - nct) // MLA_QTILES
    return pl.pallas_call(
        functools.partial(_mla_kernel, nkeys_ctx=nct * TM),
        out_shape=jax.ShapeDtypeStruct((bsz, nsteps * MLA_QTILES * TM, MLA_HEADS * MLA_V), BF16),
        grid=(bsz, MLA_HEADS // MLA_HPS, nsteps),
        in_specs=[pl.BlockSpec((None, TM, hw), lambda b, p, s: (b, jnp.maximum(2 * s - 1, 0), p)),
                  pl.BlockSpec((None, TM, hw), lambda b, p, s: (b, jnp.maximum(2 * s, 1), p)),
                  pl.BlockSpec((None, ttot, hw), lambda b, p, s: (b, 0, p)),
                  pl.BlockSpec((None, ttot, hw), lambda b, p, s: (b, 0, p))],
        out_specs=pl.BlockSpec((None, MLA_QTILES * TM, hw // 2), lambda b, p, s: (b, s, p)),
        compiler_params=_params("parallel", "parallel", "parallel"),
        name="mla_attention",
    )(q, q, k, v)


def _mla_tile(t, nct):
    return t + (t >= nct).astype(jnp.int32)


def _scan_block(j, nct, nt):
    return jnp.where(j < nct, nct - 1 - j, nt - 1 - (j - nct))


def _mlstm_kernel(qf_ref, kf_ref, vf_ref, grf_ref, gcf_ref, qr_ref, kr_ref, vr_ref, grr_ref, gcr_ref,
                  hf_ref, hr_ref, c_scr, m_scr):
    @pl.when(pl.program_id(1) == 0)
    def _():
        c_scr[...] = jnp.zeros_like(c_scr)
        m_scr[...] = jnp.zeros_like(m_scr)

    L = MLSTM_CHUNK
    nchunk = TM // L
    nh = MLSTM_HEADS
    nchain = 2 * nh
    row = lax.broadcasted_iota(jnp.int32, (L, L), 0)
    col = lax.broadcasted_iota(jnp.int32, (L, L), 1)
    ones = jnp.ones((L, MLSTM_DH), BF16)
    wide = lambda a: jnp.broadcast_to(a, (L, MLSTM_DH))
    twice = lambda a: jnp.concatenate([a, a], axis=1)
    refs = ((qf_ref, kf_ref, vf_ref, grf_ref, gcf_ref, hf_ref), (qr_ref, kr_ref, vr_ref, grr_ref, gcr_ref, hr_ref))
    nbatch = qf_ref.shape[0]

    def load(ci):
        cs = []
        for bi, d in [(bi, d) for bi in range(nbatch) for d in range(2)]:
            q_ref, k_ref, v_ref, gr_ref, gc_ref, o_ref = (r.at[bi] for r in refs[d])
            cd = ci if d == 0 else nchunk - 1 - ci
            rs = slice(cd * L, (cd + 1) * L)
            g_rows = gr_ref[cd]
            g_cols = gc_ref[rs, :]
            last = L - 1 if d == 0 else 0
            for hd in range(nh):
                ch = d * nh + hd
                hs = slice(hd * MLSTM_DH, (hd + 1) * MLSTM_DH)
                cs.append(dict(
                    rs=rs, hs=hs, o_ref=o_ref, causal=(col <= row) if d == 0 else (col >= row),
                    c_row=g_rows[ch:ch + 1], c_col=g_cols[:, ch:ch + 1],
                    pm_col=g_cols[:, nchain + ch:nchain + ch + 1], b_col=g_cols[:, 2 * nchain + ch:2 * nchain + ch + 1],
                    pm_last=g_cols[last:last + 1, nchain + ch:nchain + ch + 1],
                    b_last=g_cols[last:last + 1, 2 * nchain + ch:2 * nchain + ch + 1],
                    q=q_ref[rs, hs], k=k_ref[rs, hs],
                    v_ext=jnp.concatenate([v_ref[rs, hs], ones], axis=1)))
        return cs

    def prep_scores(cs):
        for c in cs:
            c["s"] = _dot_nt(c["q"], c["k"])

    def prep_weights(cs):
        for c in cs:
            c["pm_w"] = wide(c["pm_col"])
            c["b_w"] = wide(c["b_col"])
            c["w"] = jnp.exp2(jnp.where(c["causal"], c["c_row"] - c["pm_w"][:, :L], -jnp.inf))
            c["kwf"] = jnp.exp2(wide(c["c_col"] - c["pm_last"]))

    def prep_operands(cs):
        for c in cs:
            c["p"] = (c["s"] * c["w"]).astype(BF16)
            c["kw"] = (c["kwf"] * c["k"].astype(F32)).astype(BF16)

    def prep_products(cs):
        for c in cs:
            c["o1"] = _dot(c["p"], c["v_ext"])
            c["dc"] = _dot_tn(c["kw"], c["v_ext"])

    def scan_read(cs):
        for ch, c in enumerate(cs):
            c["qc"] = _dot(c["q"], state[ch].astype(BF16))

    def scan_factors(cs):
        for ch, c in enumerate(cs):
            g_w = jnp.maximum(m[ch], c["pm_w"])
            c["e_intra"] = jnp.exp2(c["pm_w"] - g_w)
            c["e_inter"] = jnp.exp2(m[ch] - g_w)
            c["floor"] = jnp.exp2(-(c["b_w"] + g_w))
            g_last = jnp.maximum(m[ch], c["pm_last"])
            c["keep"] = jnp.exp2(m[ch] - g_last)
            c["gain"] = jnp.exp2(c["pm_last"] - g_last)
            c["m_new"] = c["b_last"] + g_last

    def scan_update(cs):
        for ch, c in enumerate(cs):
            o = twice(c["e_intra"]) * c["o1"] + twice(c["e_inter"]) * c["qc"]
            h = o[:, :MLSTM_DH] / jnp.maximum(jnp.abs(o[:, MLSTM_DH:]), c["floor"])
            c["o_ref"][c["rs"], c["hs"]] = h.astype(BF16)
            state[ch] = c["keep"] * state[ch] + c["gain"] * c["dc"]
            m[ch] = c["m_new"]

    nstate = nbatch * nchain
    state = [c_scr[ch] for ch in range(nstate)]
    m = [m_scr[ch, 0:1, 0:1] for ch in range(nstate)]
    chunks = [load(ci) for ci in range(nchunk)]
    everything = [c for cs in chunks for c in cs]
    for stage in (prep_scores, prep_weights, prep_operands):
        stage(everything)
    prep_products(chunks[0])
    for i, cs in enumerate(chunks):
        scan_read(cs)
        if i + 1 < nchunk:
            prep_products(chunks[i + 1])
        scan_factors(cs)
        scan_update(cs)
    for ch in range(nstate):
        c_scr[ch] = state[ch]
        m_scr[ch] = jnp.broadcast_to(m[ch], m_scr.shape[1:])


def _mlstm(mq, mk, mv, g_rows, g_cols, nct):
    bsz, ttot, nb = mq.shape
    nt = ttot // TM
    per = TM // MLSTM_CHUNK
    sb = math.gcd(bsz, MLSTM_BATCH)
    fwd = lambda w: pl.BlockSpec((sb, TM, w), lambda b, j: (b, j, 0))
    rev = lambda w: pl.BlockSpec((sb, TM, w), lambda b, j: (b, _scan_block(j, nct, nt), 0))
    gshape = (sb, per, 2 * MLSTM_HEADS, MLSTM_CHUNK)
    gfwd = pl.BlockSpec(gshape, lambda b, j: (b, j, 0, 0))
    grev = pl.BlockSpec(gshape, lambda b, j: (b, _scan_block(j, nct, nt), 0, 0))
    nstate = sb * 2 * MLSTM_HEADS
    return pl.pallas_call(
        _mlstm_kernel,
        out_shape=(jax.ShapeDtypeStruct((bsz, ttot, nb), BF16),) * 2,
        grid=(bsz // sb, nt),
        in_specs=[fwd(nb), fwd(nb), fwd(nb), gfwd, fwd(LANES), rev(nb), rev(nb), rev(nb), grev, rev(LANES)],
        out_specs=(fwd(nb), rev(nb)),
        scratch_shapes=[pltpu.VMEM((nstate, MLSTM_DH, 2 * MLSTM_DH), F32),
                        pltpu.VMEM((nstate, SUBLANES, LANES), F32)],
        compiler_params=_params("parallel", "arbitrary"),
        name="mlstm_scan",
    )(mq, mk, mv, g_rows, g_cols, mq, mk, mv, g_rows, g_cols)


def _head_rms(x, n_heads):
    w = x.shape[1] // n_heads
    return jnp.concatenate([_rms(x[:, i * w:(i + 1) * w]) for i in range(n_heads)], axis=1)


def _mid_kernel(ctx_ref, lat_ref, gate_ref, a_ref, hf_ref, hr_ref, mo_ref, z0_ref, hn_ref, wout_ref,
                sc_ref, sh_ref, g_ref, wall_ref, wgate_ref, bgate_ref,
                h_out, gq_out, gk_out, gv_out, bc_out, nq_out, nk_out, nv_out, z_out, *, nct):
    nbt = ctx_ref.shape[0]
    rows = lambda f: jnp.concatenate([f(bi) for bi in range(nbt)], axis=0)
    f32 = lambda r, bi: r[bi].astype(F32)

    def put(ref, val):
        for bi in range(nbt):
            ref[bi] = val[bi * TM:(bi + 1) * TM]

    hm = rows(lambda bi: _head_rms(f32(mo_ref, bi) * (f32(hf_ref, bi) + f32(hr_ref, bi)), MLSTM_HEADS) * hn_ref[...])
    cat = (jnp.concatenate([rows(lambda bi: f32(a_ref, bi)), hm], axis=1) * rows(lambda bi: f32(z0_ref, bi))).astype(BF16)
    y = _dot(cat, wout_ref[...])
    is_ctx = pl.program_id(1) < nct
    hs = [jnp.where(is_ctx, ctx_ref[bi], lat_ref[bi]) + gate_ref[bi] * y[bi * TM:(bi + 1) * TM] for bi in range(nbt)]
    for bi in range(nbt):
        h_out[bi] = hs[bi]

    ub = rows(lambda bi: _modnorm(hs[bi], g_ref[...], sc_ref[bi], sh_ref[bi])).astype(BF16)
    wseg = lambda name: wall_ref[:, PROJ1_COLS[name][0]:PROJ1_COLS[name][1]]
    ga = _dot(ub, wseg("ga")).astype(BF16)
    gate_pre = _dot(ga, wgate_ref[...])
    put(gq_out, (_dot(ub, wseg("q")) * GLA_DK ** -0.5).astype(BF16))
    put(gk_out, _dot(ub, wseg("k")).astype(BF16))
    put(gv_out, _dot(ub, wseg("v")).astype(BF16))

    lg = _log_sigmoid(gate_pre + bgate_ref[...]) * (1.0 / GLA_TAU)
    hi, mid, lo = _split3(lg)
    put(nq_out, (_dot(ub, wseg("nq")) * (NA_DH ** -0.5 * LOG2E)).astype(BF16))
    put(nk_out, _dot(ub, wseg("nk")).astype(BF16))
    put(nv_out, _dot(ub, wseg("nv")).astype(BF16))
    row = lax.broadcasted_iota(jnp.int32, (TM, TM), 0)
    col = lax.broadcasted_iota(jnp.int32, (TM, TM), 1)
    same = (row // CHUNK) == (col // CHUNK)
    nk = GLA_HEADS * GLA_DK
    for d in range(2):
        tri = jnp.where(jnp.logical_and(same, (col <= row) if d == 0 else (col >= row)), 1.0, 0.0).astype(BF16)
        cols = slice(d * nk, (d + 1) * nk)
        for bi in range(nbt):
            rs = slice(bi * TM, (bi + 1) * TM)
            bc_out[bi, :, cols] = (_dot(tri, hi[rs, cols]) + _dot(tri, mid[rs, cols]) + _dot(tri, lo[rs, cols])) * LOG2E
    put(z_out, _silu(_dot(ub, wseg("z"))).astype(BF16))


def _proj1_weights(w_in, w_gate, b_gate):
    nk = GLA_HEADS * GLA_DK
    nc = GLA_HEADS * GLA_DV
    nd = NA_HEADS * NA_DH
    o = np.cumsum([0, nk, nk, nc, 2 * GLA_GATE_RANK, nd, nd, nd, nc + nd])
    r = GLA_GATE_RANK
    c = PROJ1_COLS
    wall = _relayout(w_in.T, [(0, o[3], 0), (o[3], 2 * r, c["ga"][0]), (o[4], o[8] - o[4], c["nq"][0])], c["z"][1])
    wgate = jnp.zeros((LANES, 2 * nk), F32).at[:r, :nk].set(w_gate[0]).at[r:2 * r, nk:].set(w_gate[1])
    return dict(wall=wall, wgate=wgate.astype(BF16), bgate=b_gate.reshape(1, 2 * nk))


def _mid(ctx, x, gate0, a, hf, hr, mo, z0, h_norm, w_out, scale, shift, norm_g, w, nct):
    bsz = x.shape[0]
    ttot = ctx.shape[1] + x.shape[1]
    nt = ttot // TM
    sb = math.gcd(bsz, TILE_BATCH)
    tile, src_ctx, src_lat, _, _, mod = _tile_specs(nct, nt, sb)
    nb = MLSTM_HEADS * MLSTM_DH
    nk = GLA_HEADS * GLA_DK
    nc = GLA_HEADS * GLA_DV
    nd = NA_HEADS * NA_DH
    ws = [w[n] for n in ("wall", "wgate", "bgate")]
    tok = lambda width, dt: jax.ShapeDtypeStruct((bsz, ttot, width), dt)
    return pl.pallas_call(
        functools.partial(_mid_kernel, nct=nct),
        out_shape=(tok(D_MODEL, F32), tok(nk, BF16), tok(nk, BF16), tok(nc, BF16), tok(2 * nk, F32),
                   tok(nd, BF16), tok(nd, BF16), tok(nd, BF16), tok(nc + nd, BF16)),
        grid=(bsz // sb, nt),
        in_specs=[src_ctx, src_lat, mod,
                  pl.BlockSpec((sb, TM, MLA_HEADS * MLA_V), lambda b, t: (b, _mla_tile(t, nct), 0)),
                  tile(nb), tile(nb), tile(nb), tile(D_MODEL),
                  _full((1, nb)), _full(w_out.shape), mod, mod, _full((1, D_MODEL))] + [_full(a.shape) for a in ws],
        out_specs=(tile(D_MODEL), tile(nk), tile(nk), tile(nc), tile(2 * nk), tile(nd), tile(nd), tile(nd),
                   tile(nc + nd)),
        compiler_params=_params("parallel", "parallel"),
        name="out0_proj1",
    )(ctx, x, gate0, a, hf, hr, mo, z0, h_norm[None], w_out.astype(BF16), scale, shift, norm_g[None], *ws)


def _gla_kernel(qf_ref, kf_ref, vf_ref, bf_ref, qr_ref, kr_ref, vr_ref, br_ref, of_ref, or_ref, s_scr):
    @pl.when(pl.program_id(1) == 0)
    def _():
        s_scr[...] = jnp.zeros_like(s_scr)

    L = CHUNK
    nchunk = TM // L
    npair = GLA_HEADS // 2
    lane = lax.broadcasted_iota(jnp.int32, (L, LANES), 1)
    rowv = lax.broadcasted_iota(jnp.int32, (L, LANES), 0)
    srow = lax.broadcasted_iota(jnp.int32, (LANES, 2 * GLA_DV), 0)
    scol = lax.broadcasted_iota(jnp.int32, (LANES, 2 * GLA_DV), 1)
    own_block = (srow < GLA_DK) == (scol < GLA_DV)
    vcol = lax.broadcasted_iota(jnp.int32, (L, 2 * GLA_DV), 1)
    e_row = lax.broadcasted_iota(jnp.int32, (LANES, LANES), 0)
    e_col = lax.broadcasted_iota(jnp.int32, (LANES, LANES), 1)
    head_ones = jnp.where((e_row < GLA_DK) == (e_col < GLA_DK), 1.0, 0.0).astype(BF16)
    leaf_idx = lax.broadcasted_iota(jnp.int32, (GLA_LEAF * GLA_LEAF, LANES), 0)
    leaf_lane = lax.broadcasted_iota(jnp.int32, (GLA_LEAF * GLA_LEAF, LANES), 1) % L
    leaf_s, leaf_t = leaf_idx // GLA_LEAF, leaf_idx % GLA_LEAF
    rep = lambda a: jnp.concatenate(
        [jnp.broadcast_to(a[s:s + 1], (GLA_LEAF, a.shape[1])) for s in range(GLA_LEAF)], axis=0)
    til = lambda a: jnp.concatenate([a] * GLA_LEAF, axis=0)

    def chunk_step(ci, carry):
        cs = []
        for bi, d in [(bi, d) for bi in range(qf_ref.shape[0]) for d in range(2)]:
            q_ref, k_ref, v_ref, b_ref, o_ref = (
                r.at[bi] for r in ((qf_ref, kf_ref, vf_ref, bf_ref, of_ref), (qr_ref, kr_ref, vr_ref, br_ref, or_ref))[d])
            cd = ci if d == 0 else nchunk - 1 - ci
            base = pl.multiple_of(cd * L, L)
            for p in range(npair):
                ls = slice(p * LANES, (p + 1) * LANES)
                bc = b_ref[pl.ds(base, L), ls]
                cs.append(dict(
                    d=d, p=p, bi=bi, base=base, o_ref=o_ref, bc=bc,
                    tau_v=rowv if d == 0 else L - 1 - rowv,
                    tau_s=(lane % L) if d == 0 else L - 1 - (lane % L),
                    q=q_ref[pl.ds(base, L), ls].astype(F32),
                    k=k_ref[pl.ds(base, L), ls].astype(F32),
                    v=v_ref[pl.ds(base, L), p * 2 * GLA_DV:(p + 1) * 2 * GLA_DV],
                    b_last=bc[L - 1:L] if d == 0 else bc[0:1],
                    state=s_scr[bi, d, p]))

        for c in cs:
            c["o_pair"] = _dot((c["q"] * jnp.exp2(c["bc"])).astype(BF16), c["state"].astype(BF16))

        for c in cs:
            c["att"] = jnp.zeros((L, LANES), F32)
        bs = L // 2
        while bs >= GLA_LEAF:
            for c in cs:
                d, bc, tau_v = c["d"], c["bc"], c["tau_v"]
                later = (tau_v // bs) % 2 == 1
                ref_b = None
                for blk in range(L // (2 * bs)):
                    tau_ref = blk * 2 * bs + bs - 1
                    idx = tau_ref if d == 0 else L - 1 - tau_ref
                    r = jnp.broadcast_to(bc[idx:idx + 1], (L, LANES))
                    ref_b = r if ref_b is None else jnp.where(tau_v // (2 * bs) == blk, r, ref_b)
                c["qs"] = jnp.where(later, c["q"] * jnp.exp2(jnp.where(later, bc - ref_b, 0.0)), 0.0).astype(BF16)
                ks = jnp.where(later, 0.0, c["k"] * jnp.exp2(jnp.where(later, 0.0, ref_b - bc)))
                c["ks2"] = jnp.concatenate([jnp.where(lane < GLA_DK, ks, 0.0), jnp.where(lane < GLA_DK, 0.0, ks)],
                                           axis=0).astype(BF16)
            for c in cs:
                same_parent = (c["tau_v"] // (2 * bs)) == (c["tau_s"] // (2 * bs))
                c["att"] = c["att"] + jnp.where(same_parent, _dot_nt(c["qs"], c["ks2"]), 0.0)
            bs //= 2

        for c in cs:
            c["leaf"] = []
        for blk in range(L // GLA_LEAF):
            rs = slice(blk * GLA_LEAF, (blk + 1) * GLA_LEAF)
            for c in cs:
                qb, bb, kb = c["q"][rs], c["bc"][rs], c["k"][rs]
                ok = (leaf_t >= leaf_s) if c["d"] == 0 else (leaf_t <= leaf_s)
                w = jnp.where(ok, til(qb) * jnp.exp2(til(bb) - rep(bb)) * rep(kb), 0.0)
                c["a"] = _dot(w.astype(BF16), head_ones)
            for c in cs:
                a = jnp.where(leaf_lane == leaf_s + blk * GLA_LEAF, c["a"], 0.0)
                acc = a[0:GLA_LEAF]
                for s in range(1, GLA_LEAF):
                    acc = acc + a[s * GLA_LEAF:(s + 1) * GLA_LEAF]
                c["leaf"].append(acc)

        for c in cs:
            v = c["v"]
            c["att"] = (c["att"] + jnp.concatenate(c["leaf"], axis=0)).astype(BF16)
            c["v_blk"] = jnp.concatenate([jnp.where(vcol < GLA_DV, v, jnp.zeros_like(v)),
                                          jnp.where(vcol < GLA_DV, jnp.zeros_like(v), v)], axis=0)
            c["ke"] = (c["k"] * jnp.exp2(c["b_last"] - c["bc"])).astype(BF16)
            c["decay_col"] = jnp.sum(
                jnp.where(e_row == e_col, jnp.broadcast_to(jnp.exp2(c["b_last"]), (LANES, LANES)), 0.0),
                axis=1, keepdims=True)
        for c in cs:
            c["o"] = c["o_pair"] + _dot(c["att"], c["v_blk"])
            c["ds"] = _dot_tn(c["ke"], c["v"])
        for c in cs:
            p = c["p"]
            c["o_ref"][pl.ds(c["base"], L), p * 2 * GLA_DV:(p + 1) * 2 * GLA_DV] = c["o"].astype(BF16)
            s_scr[c["bi"], c["d"], p] = c["decay_col"] * c["state"] + jnp.where(own_block, c["ds"], 0.0)
        return carry

    lax.fori_loop(0, nchunk, chunk_step, 0)


def _gla(gq, gk, gv, bc, nct):
    bsz, ttot, nk = gq.shape
    nc = gv.shape[2]
    nt = ttot // TM
    sb = math.gcd(bsz, GLA_BATCH)
    fwd = lambda w: pl.BlockSpec((sb, TM, w), lambda b, j: (b, j, 0))
    rev = lambda w: pl.BlockSpec((sb, TM, w), lambda b, j: (b, _scan_block(j, nct, nt), 0))
    bfwd = pl.BlockSpec((sb, TM, nk), lambda b, j: (b, j, 0))
    brev = pl.BlockSpec((sb, TM, nk), lambda b, j: (b, _scan_block(j, nct, nt), 1))
    return pl.pallas_call(
        _gla_kernel,
        out_shape=(jax.ShapeDtypeStruct((bsz, ttot, nc), BF16),) * 2,
        grid=(bsz // sb, nt),
        in_specs=[fwd(nk), fwd(nk), fwd(nc), bfwd, rev(nk), rev(nk), rev(nc), brev],
        out_specs=(fwd(nc), rev(nc)),
        scratch_shapes=[pltpu.VMEM((sb, 2, GLA_HEADS // 2, LANES, 2 * GLA_DV), F32)],
        compiler_params=_params("parallel", "arbitrary"),
        name="gla_scan",
    )(gq, gk, gv, bc, gq, gk, gv, bc)


def _na_bias_tables(rpb, rows):
    kh = NA_KH
    c = np.arange(GRID_W)
    cs = np.clip(c - NA_KW // 2, 0, GRID_W - NA_KW)
    col_ok = (c[None, :] >= cs[:, None]) & (c[None, :] < cs[:, None] + NA_KW)
    pad = GRID_W - NA_KW
    rpb_pad = jnp.pad(rpb.astype(F32) * LOG2E, ((0, 0), (0, 0), (pad, pad)))
    period = 2 * GRID_W
    flat = jnp.tile(jnp.pad(rpb_pad, ((0, 0), (0, 0), (0, 1))), (1, 1, GRID_W))[:, :, :GRID_W * (period - 1)]
    t1 = flat.reshape(NA_HEADS, 2 * NA_KH - 1, GRID_W, period - 1)[..., GRID_W - 1:]
    t1 = jnp.where(col_ok, t1, NEG_BIG)
    plan = []
    for r0 in (0, NA_QROWS, rows - NA_QROWS):
        kb = int(np.clip(r0 - kh // 2, 0, rows - NA_KROWS))
        for qrow in range(r0, r0 + NA_QROWS):
            ws = int(np.clip(qrow - kh // 2, 0, rows - kh))
            plan.append(tuple(krow - qrow + NA_KH - 1 if ws <= krow < ws + kh else None
                              for krow in range(kb, kb + NA_KROWS)))
    t1 = jnp.pad(t1, ((0, 0), (0, 0), (0, 0), (0, LANES - GRID_W)))
    return pl.pallas_call(
        functools.partial(_na_table_kernel, plan=tuple(plan)),
        out_shape=jax.ShapeDtypeStruct((NA_HEADS, len(plan) * GRID_W, NA_KROWS * GRID_W), F32),
        grid=(NA_HEADS,),
        in_specs=[pl.BlockSpec((None,) + t1.shape[1:], lambda h: (h, 0, 0, 0))],
        out_specs=pl.BlockSpec((None, len(plan) * GRID_W, NA_KROWS * GRID_W), lambda h: (h, 0, 0)),
        compiler_params=_params("parallel"),
        name="na_bias_table",
    )(t1)


def _na_table_kernel(t1_ref, o_ref, *, plan):
    assert 2 * GRID_W == LANES
    lane = lax.broadcasted_iota(jnp.int32, (GRID_W, LANES), 1)
    masked = jnp.full((GRID_W, LANES), NEG_BIG, F32)
    block = lambda idx: masked if idx is None else t1_ref[idx]
    for i, entries in enumerate(plan):
        rows = slice(i * GRID_W, (i + 1) * GRID_W)
        for k in range(0, len(entries) - 1, 2):
            pair = jnp.where(lane < GRID_W, block(entries[k]), pltpu.roll(block(entries[k + 1]), GRID_W, axis=1))
            o_ref[rows, k * GRID_W:(k + 2) * GRID_W] = pair
        if len(entries) % 2:
            k = len(entries) - 1
            o_ref[rows, k * GRID_W:(k + 1) * GRID_W] = block(entries[k])[:, :GRID_W]


def _na_heads(q_ref, kw_ref, vw_ref, kc_ref, vc_ref, bias_ref):
    nq = NA_QROWS * GRID_W
    lane = lax.broadcasted_iota(jnp.int32, (nq, LANES), 1)
    def scores(j):
        ls = slice((j // 2) * LANES, (j // 2 + 1) * LANES)
        q = q_ref[:, ls]
        qj = jnp.where((lane < NA_DH) == (j % 2 == 0), q, jnp.zeros_like(q))
        return _dot_nt(qj, kw_ref[:, ls]) + bias_ref[j], _dot_nt(qj, kc_ref[:, ls])

    outs = []
    nxt = scores(0)
    for j in range(NA_HEADS):
        s_loc, s_ctx = nxt
        if j + 1 < NA_HEADS:
            nxt = scores(j + 1)
        ls = slice((j // 2) * LANES, (j // 2 + 1) * LANES)
        m = jnp.maximum(jnp.max(s_loc, axis=-1, keepdims=True), jnp.max(s_ctx, axis=-1, keepdims=True))
        p_loc = jnp.exp2(s_loc - m)
        p_ctx = jnp.exp2(s_ctx - m)
        den = jnp.sum(p_loc, axis=-1, keepdims=True) + jnp.sum(p_ctx, axis=-1, keepdims=True)
        outs.append((_dot(p_loc.astype(BF16), vw_ref[:, ls]) + _dot(p_ctx.astype(BF16), vc_ref[:, ls])) / den)
    return jnp.concatenate([jnp.where(lane < NA_DH, outs[2 * i], outs[2 * i + 1]) for i in range(NA_HEADS // 2)], axis=1)


def _na_out1_kernel(*refs, nb):
    kv = refs[:4 * nb]
    q_ref, bias_ref, h_ref, gate_ref, of_ref, or_ref, z_ref, gn_ref, w_ref, fn_ref, o_ref = refs[4 * nb:]
    rows = lambda f: jnp.concatenate([f(bi) for bi in range(nb)], axis=0)
    f32 = lambda r, bi: r[bi].astype(F32)
    na = rows(lambda bi: _na_heads(q_ref.at[bi], *kv[4 * bi:4 * bi + 4], bias_ref))
    g = rows(lambda bi: _head_rms(f32(of_ref, bi) + f32(or_ref, bi), GLA_HEADS) * gn_ref[...])
    cat = (jnp.concatenate([g, na], axis=1) * rows(lambda bi: f32(z_ref, bi))).astype(BF16)
    y = _dot(cat, w_ref[...])
    for bi in range(nb):
        h = h_ref[bi] + gate_ref[bi] * y[bi * TM:(bi + 1) * TM]
        o_ref[bi] = _rms(h) * fn_ref[...]


def _na_out1(nq, nk, nv, bias, h, gate, of, orv, z, gla_norm, w_out, final_norm, n_ctx, rows):
    bsz, ttot, nd = nq.shape
    nrb = rows // NA_QROWS
    qrows = NA_QROWS * GRID_W
    krows = NA_KROWS * GRID_W
    assert qrows == TM
    nct = n_ctx // TM
    nc = GLA_HEADS * GLA_DV
    sb = math.gcd(bsz, TILE_BATCH)

    def variant(rb):
        return jnp.where(rb == 0, 0, jnp.where(rb == nrb - 1, 2, 1))

    def key_start(rb):
        first_row = jnp.clip(rb * NA_QROWS - NA_KH // 2, 0, rows - NA_KROWS)
        return pl.multiple_of(n_ctx + first_row * GRID_W, GRID_W)

    window = lambda i: pl.BlockSpec((None, pl.Element(krows), pl.Element(nd)), lambda b, rb: (b * sb + i, key_start(rb), 0))
    context = lambda i: pl.BlockSpec((None, pl.Element(n_ctx), pl.Element(nd)), lambda b, rb: (b * sb + i, 0, 0))
    kv_specs = [spec(i) for i in range(sb) for spec in (window, window, context, context)]
    lat = lambda w: pl.BlockSpec((sb, TM, w), lambda b, rb: (b, rb + nct, 0))
    return pl.pallas_call(
        functools.partial(_na_out1_kernel, nb=sb),
        out_shape=jax.ShapeDtypeStruct((bsz, rows * GRID_W, D_MODEL), F32),
        grid=(bsz // sb, nrb),
        in_specs=kv_specs + [
            lat(nd), pl.BlockSpec((NA_HEADS, qrows, krows), lambda b, rb: (0, variant(rb), 0)),
            lat(D_MODEL), pl.BlockSpec((sb, None, 1, D_MODEL), lambda b, rb: (b, 1, 0, 0)),
            lat(nc), lat(nc), lat(nc + nd), _full((1, nc)), _full(w_out.shape), _full((1, D_MODEL))],
        out_specs=pl.BlockSpec((sb, TM, D_MODEL), lambda b, rb: (b, rb, 0)),
        compiler_params=_params("parallel", "parallel"),
        name="na_out1",
    )(*([nk, nv, nk, nv] * sb), nq, bias, h, gate, of, orv, z, gla_norm[None], w_out.astype(BF16), final_norm[None])


def kernel(x, c, ctx, c_ctx, l0_norm, l0_w_mod, l0_b_mod, l0_w_in, l0_mla_q_norm, l0_mla_w_uq, l0_mla_kv_norm, l0_mla_w_ukv, l0_mlstm_conv_w, l0_mlstm_conv_b, l0_mlstm_b_i, l0_mlstm_b_f, l0_mlstm_norm, l0_w_out, l1_norm, l1_w_mod, l1_b_mod, l1_w_in, l1_gla_w_gate, l1_gla_b_gate, l1_gla_norm, l1_na_rpb, l1_w_out, final_norm):
    bsz, seq, d = x.shape
    n_ctx = ctx.shape[1]
    rows = seq // GRID_W
    assert d == D_MODEL and seq % TM == 0 and n_ctx == TM
    assert rows % NA_QROWS == 0 and rows >= NA_KROWS and rows // NA_QROWS >= 3
    nct = n_ctx // TM

    (shift0, scale0, gate0), (shift, scale, gate) = _mod_vectors(
        c, c_ctx, ((l0_w_mod, l0_b_mod), (l1_w_mod, l1_b_mod)))

    cos_t, sin_t = _rope_tables(n_ctx, seq)
    w0 = _proj0_weights(l0_w_in, l0_mla_q_norm, l0_mla_w_uq, l0_mla_kv_norm, l0_mla_w_ukv,
                        l0_mlstm_conv_w, l0_mlstm_conv_b, l0_mlstm_b_i, l0_mlstm_b_f)
    q, k, v, mq, mk, mv, g_rows, g_cols, mo, z0 = _proj0(ctx, x, scale0, shift0, l0_norm, cos_t, sin_t, w0, nct)
    a = _mla_attention(q, k, v, nct)
    hf, hr = _mlstm(mq, mk, mv, g_rows, g_cols, nct)

    w1 = _proj1_weights(l1_w_in, l1_gla_w_gate, l1_gla_b_gate)
    h, gq, gk, gv, bc, nq, nk, nv, z = _mid(ctx, x, gate0, a, hf, hr, mo, z0, l0_mlstm_norm, l0_w_out,
                                            scale, shift, l1_norm, w1, nct)
    of, orv = _gla(gq, gk, gv, bc, nct)
    return _na_out1(nq, nk, nv, _na_bias_tables(l1_na_rpb, rows), h, gate, of, orv, z,
                    l1_gla_norm, l1_w_out, final_norm, n_ctx, rows)
```

```python
import functools
import math

import jax
import jax.numpy as jnp
import numpy as np
from jax import lax
from jax.experimental import pallas as pl
from jax.experimental.pallas import tpu as pltpu

F32 = jnp.float32
BF16 = jnp.bfloat16

D_MODEL = 1024
GRID_W = 64
EPS = 1e-6
ROPE_BASE = 10000.0

MLA_HEADS = 8
MLA_Q_RANK = 384
MLA_KV_RANK = 256
MLA_NOPE = 64
MLA_ROPE = 32
MLA_V = 64
ROPE_PAIRS = MLA_ROPE // 4
MLSTM_HEADS = 4
MLSTM_DH = 128
MLSTM_CONV = 3
GLA_HEADS = 4
GLA_DK = 64
GLA_DV = 128
GLA_GATE_RANK = 16
GLA_TAU = 16.0
NA_HEADS = 8
NA_DH = 64
NA_KH = 8
NA_KW = 16

CHUNK = 64
MLSTM_CHUNK = 128
TM = 256
LANES = 128
SUBLANES = 8
HALO = SUBLANES
NA_QROWS = 4
NA_KROWS = NA_QROWS + NA_KH - 1
GLA_LEAF = 8
MLA_HPS = 4
MLA_QTILES = 2
TILE_BATCH = 2
MLSTM_BATCH = 2
GLA_BATCH = 8
VMEM_LIMIT = 56 * 1024 * 1024
NEG_BIG = -1e30
LOG2E = 1.4426950408889634


def _dot(a, b):
    return jnp.dot(a, b, preferred_element_type=F32)


def _dot_nt(a, b):
    return lax.dot_general(a, b, (((1,), (1,)), ((), ())), preferred_element_type=F32)


def _dot_tn(a, b):
    return lax.dot_general(a, b, (((0,), (0,)), ((), ())), preferred_element_type=F32)


def _rms(x):
    return x * lax.rsqrt(jnp.mean(x * x, axis=-1, keepdims=True) + EPS)


def _sigmoid(x):
    return 1.0 / (1.0 + jnp.exp(-x))


def _silu(x):
    return x * _sigmoid(x)


def _log_sigmoid(x):
    return jnp.minimum(x, 0.0) - jnp.log(1.0 + jnp.exp(-jnp.abs(x)))


def _params(*sem):
    return pltpu.CompilerParams(dimension_semantics=sem, vmem_limit_bytes=VMEM_LIMIT)


def _full(shape):
    nd = len(shape)
    return pl.BlockSpec(shape, lambda *_: (0,) * nd)


def _mod_kernel(c_ref, w0_ref, b0_ref, w1_ref, b1_ref, o0_ref, o1_ref):
    sc = _silu(c_ref[...]).astype(BF16)
    o0_ref[...] = _dot(sc, w0_ref[...].astype(BF16)) + b0_ref[...]
    o1_ref[...] = _dot(sc, w1_ref[...].astype(BF16)) + b1_ref[...]


def _mod_vectors(c, c_ctx, mods):
    bsz, d = c.shape
    rows = -(-(bsz + 1) // SUBLANES) * SUBLANES
    cc = jnp.zeros((rows, d), F32).at[:bsz].set(c).at[bsz].set(c_ctx)
    (w0, b0), (w1, b1) = mods
    n = w0.shape[1]
    wspec = pl.BlockSpec((d, d), lambda j: (0, j))
    vspec = pl.BlockSpec((1, d), lambda j: (0, j))
    ospec = pl.BlockSpec((rows, d), lambda j: (0, j))
    per_layer = pl.pallas_call(
        _mod_kernel,
        out_shape=(jax.ShapeDtypeStruct((rows, n), F32),) * 2,
        grid=(n // d,),
        in_specs=[_full((rows, d)), wspec, vspec, wspec, vspec],
        out_specs=(ospec, ospec),
        compiler_params=_params("parallel"),
        name="modulation",
    )(cc, w0, b0[None], w1, b1[None])
    out = []
    for mod in per_layer:
        vecs = []
        for part in jnp.split(mod, 3, axis=-1):
            ctx_v = jnp.broadcast_to(part[bsz][None], (bsz, d))
            vecs.append(jnp.stack([ctx_v, part[:bsz]], axis=1)[:, :, None, :])
        out.append(vecs)
    return out


def _modnorm(x, g, scale, shift):
    return _rms(x) * g * (1.0 + scale) + shift


def _seg_scan(x, reverse, op, fill):
    width = x.shape[1]
    pos = lax.broadcasted_iota(jnp.int32, x.shape, 1) % MLSTM_CHUNK
    k = 1
    while k < MLSTM_CHUNK:
        if reverse:
            x = op(x, jnp.where(pos < MLSTM_CHUNK - k, pltpu.roll(x, width - k, axis=1), fill))
        else:
            x = op(x, jnp.where(pos >= k, pltpu.roll(x, k, axis=1), fill))
        k *= 2
    return x


def _split3(x):
    hi = x.astype(BF16)
    r1 = x - hi.astype(F32)
    mid = r1.astype(BF16)
    return hi, mid, (r1 - mid.astype(F32)).astype(BF16)


def _proj0_kernel(ctx_ref, lat_ref, hp_ref, hn_ref, sc_ref, sh_ref, g_ref, cos_ref, sin_ref,
                  wall_ref, wgt_ref, qn_ref, wuq_ref, kvn_ref, wk_ref, wv_ref, cw_ref, cb_ref, gb_ref,
                  q_out, k_out, v_out, mq_out, mk_out, mv_out, gr_out, gc_out, mo_out, z_out,
                  pbuf, *, nct, nt):
    t = pl.program_id(1)
    h = jnp.where(t < nct, ctx_ref[...], lat_ref[...])
    x = jnp.concatenate([hp_ref[...], h, hn_ref[...]], axis=0)
    ub_ext = _modnorm(x, g_ref[...], sc_ref[...], sh_ref[...]).astype(BF16)
    ub = ub_ext[HALO:HALO + TM]

    gt = _dot_nt(wgt_ref[...], ub) + gb_ref[...]
    half = MLSTM_HEADS
    per_dir = []
    for d in range(2):
        gd = gt[d * SUBLANES:(d + 1) * SUBLANES]
        bcum = _seg_scan(_log_sigmoid(gd), d == 1, jnp.add, 0.0)
        b_top = pltpu.roll(bcum, half, axis=0)
        c8 = gd - b_top
        pm8 = _seg_scan(c8, d == 1, jnp.maximum, -jnp.inf)
        per_dir.append((c8, pm8, b_top))

    cos = cos_ref[...]
    sin = sin_ref[...]
    cos_all = jnp.concatenate([cos] * MLA_HEADS, axis=1)
    sin_all = jnp.concatenate([sin] * MLA_HEADS, axis=1)

    def swap_halves(a):
        lane = lax.broadcasted_iota(jnp.int32, a.shape, 1)
        first = lane % (2 * ROPE_PAIRS) < ROPE_PAIRS
        return jnp.where(first, pltpu.roll(a, a.shape[1] - ROPE_PAIRS, axis=1), pltpu.roll(a, ROPE_PAIRS, axis=1))

    wseg = lambda name: wall_ref[:, PROJ0_COLS[name][0]:PROJ0_COLS[name][1]]
    cqkr = _dot(ub, wseg("cqkr"))
    cq = cqkr[:, :MLA_Q_RANK]
    kr = cqkr[:, MLA_Q_RANK:]
    ckv = _dot(ub, wseg("ckv"))
    pqk = _dot(ub_ext, wseg("mqk"))

    nq = (_rms(cq) * qn_ref[...]).astype(BF16)
    nkv = (_rms(ckv) * kvn_ref[...]).astype(BF16)

    q = _dot(nq, wuq_ref[...])
    kn = _dot(nkv, wk_ref[...])
    vv = _dot(nkv, wv_ref[...])

    prev_ok = t > nct
    next_ok = jnp.logical_and(t >= nct, t != nt - 1)
    row = lax.broadcasted_iota(jnp.int32, pqk.shape, 0)
    keep = jnp.logical_and(jnp.logical_or(row >= HALO, prev_ok), jnp.logical_or(row < HALO + TM, next_ok))
    pbuf[...] = jnp.where(keep, pqk, 0.0)
    cw = cw_ref[...]
    y = (pbuf[HALO - 1:HALO - 1 + TM, :] * cw[0:1] + pbuf[HALO:HALO + TM, :] * cw[1:2]
         + pbuf[HALO + 1:HALO + 1 + TM, :] * cw[2:3] + cb_ref[...])
    y = _silu(y)
    nb = MLSTM_HEADS * MLSTM_DH
    mq_out[...] = (y[:, :nb] * MLSTM_DH ** -0.5).astype(BF16)
    mk_out[...] = y[:, nb:].astype(BF16)

    mv = _dot(ub, wseg("mv"))
    mo = _dot(ub, wseg("mo"))

    q = q * cos_all + swap_halves(q) * sin_all
    q_out[...] = (q * ((MLA_NOPE + MLA_ROPE) ** -0.5 * LOG2E)).astype(BF16)
    rot = kr * cos + swap_halves(kr) * sin
    k_out[...] = (kn + jnp.concatenate([rot] * MLA_HEADS, axis=1)).astype(BF16)
    vlane = lax.broadcasted_iota(jnp.int32, (TM, MLA_HEADS * LANES), 1)
    is_value = ((vlane % LANES) < MLA_V) == ((vlane // LANES) % 2 == 0)
    v_out[...] = jnp.where(is_value, vv, 1.0).astype(BF16)

    zz = _dot(ub, wseg("z"))
    mv_out[...] = mv.astype(BF16)
    mo_out[...] = _sigmoid(mo).astype(BF16)
    z_out[...] = _silu(zz).astype(BF16)

    r8 =lax.broadcasted_iota(jnp.int32, (SUBLANES, TM), 0)
    both = lambda i: jnp.where(r8 < half, per_dir[0][i], pltpu.roll(per_dir[1][i], half, axis=0)) * LOG2E
    c_rows, pm_rows, b_rows = both(0), both(1), both(2)
    for ci in range(TM // MLSTM_CHUNK):
        gr_out[ci] = c_rows[:, ci * MLSTM_CHUNK:(ci + 1) * MLSTM_CHUNK]
    stack = jnp.concatenate([c_rows, pm_rows, b_rows, jnp.zeros((LANES - 3 * SUBLANES, TM), F32)], axis=0)
    er = lax.broadcasted_iota(jnp.int32, (TM, TM), 0)
    ec = lax.broadcasted_iota(jnp.int32, (TM, TM), 1)
    eye = jnp.where(er == ec, 1.0, 0.0).astype(BF16)
    gc_out[...] = sum(_dot_nt(eye, part) for part in _split3(stack))


def _rope_tables(n_ctx, seq):
    t = jnp.arange(seq)
    inv = 1.0 / (ROPE_BASE ** (jnp.arange(ROPE_PAIRS, dtype=F32) / ROPE_PAIRS))
    ang = jnp.concatenate([(t // GRID_W)[:, None] * inv, (t % GRID_W)[:, None] * inv], axis=-1)
    cos, sin = jnp.cos(ang), jnp.sin(ang)
    j = np.arange(MLA_ROPE)
    src = (j // (2 * ROPE_PAIRS)) * ROPE_PAIRS + (j % ROPE_PAIRS)
    sign = np.where((j % (2 * ROPE_PAIRS)) < ROPE_PAIRS, -1.0, 1.0).astype(np.float32)
    cos_full = cos[:, src]
    sin_full = sin[:, src] * sign
    cos_t = jnp.ones((n_ctx + seq, LANES), F32).at[n_ctx:, MLA_NOPE:MLA_NOPE + MLA_ROPE].set(cos_full)
    sin_t = jnp.zeros((n_ctx + seq, LANES), F32).at[n_ctx:, MLA_NOPE:MLA_NOPE + MLA_ROPE].set(sin_full)
    return cos_t, sin_t


def _relayout_kernel(wt_ref, o_ref, *, moves):
    lane = lax.broadcasted_iota(jnp.int32, (o_ref.shape[0], LANES), 1)
    end = 0
    for src, n, dst in moves:
        group = dst - dst % LANES
        if group > end:
            o_ref[:, end:group] = jnp.zeros((o_ref.shape[0], group - end), o_ref.dtype)
        if n % LANES == 0 and dst == group:
            o_ref[:, dst:dst + n] = wt_ref[src:src + n, :].T.astype(o_ref.dtype)
            end = dst + n
        else:
            assert n < LANES and dst + n <= group + LANES and src + LANES <= wt_ref.shape[0]
            t = pltpu.roll(wt_ref[src:src + LANES, :].T, dst - group, axis=1)
            keep = (lane >= dst - group) & (lane < dst - group + n)
            o_ref[:, group:group + LANES] = jnp.where(keep, t, 0.0).astype(o_ref.dtype)
            end = group + LANES
    if end < o_ref.shape[1]:
        o_ref[:, end:] = jnp.zeros((o_ref.shape[0], o_ref.shape[1] - end), o_ref.dtype)


def _relayout(wt, moves, width):
    rows = wt.shape[1]
    rb = math.gcd(rows, TM)
    return pl.pallas_call(
        functools.partial(_relayout_kernel, moves=tuple(moves)),
        out_shape=jax.ShapeDtypeStruct((rows, width), BF16),
        grid=(rows // rb,),
        in_specs=[pl.BlockSpec((wt.shape[0], rb), lambda i: (0, i))],
        out_specs=pl.BlockSpec((rb, width), lambda i: (i, 0)),
        compiler_params=_params("parallel"),
        name="weight_relayout",
    )(wt)


def _segments(widths):
    out, start = {}, 0
    for name, width in widths:
        out[name] = (start, start + width)
        start += width
    return out


PROJ0_COLS = _segments([("cqkr", MLA_Q_RANK + LANES), ("ckv", MLA_KV_RANK), ("mqk", 2 * MLSTM_HEADS * MLSTM_DH),
                        ("mv", MLSTM_HEADS * MLSTM_DH), ("mo", MLSTM_HEADS * MLSTM_DH), ("z", D_MODEL)])
PROJ1_COLS = _segments([("q", GLA_HEADS * GLA_DK), ("k", GLA_HEADS * GLA_DK), ("v", GLA_HEADS * GLA_DV),
                        ("ga", LANES), ("nq", NA_HEADS * NA_DH), ("nk", NA_HEADS * NA_DH), ("nv", NA_HEADS * NA_DH),
                        ("z", GLA_HEADS * GLA_DV + NA_HEADS * NA_DH)])


def _proj0_weights(w_in, q_norm, w_uq, kv_norm, w_ukv, conv_w, conv_b, b_i, b_f):
    o = np.cumsum([0, MLA_Q_RANK, MLA_KV_RANK, MLA_ROPE] + [MLSTM_HEADS * MLSTM_DH] * 4 + [4 * MLSTM_HEADS, D_MODEL])
    dq = MLA_NOPE + MLA_ROPE
    wuq = jnp.pad(w_uq.reshape(MLA_Q_RANK, MLA_HEADS, dq), ((0, 0), (0, 0), (0, LANES - dq)))
    wkv = w_ukv.reshape(MLA_KV_RANK, MLA_HEADS, MLA_NOPE + MLA_V)
    wk = jnp.pad(wkv[:, :, :MLA_NOPE], ((0, 0), (0, 0), (0, LANES - MLA_NOPE)))
    zv = jnp.zeros((MLA_KV_RANK, MLA_HEADS // 2, MLA_V), F32)
    wv = jnp.concatenate([wkv[:, 0::2, MLA_NOPE:], zv, zv, wkv[:, 1::2, MLA_NOPE:]], axis=-1)
    wv = wv.reshape(MLA_KV_RANK, MLA_HEADS * LANES)
    gbias = jnp.stack([b_i, b_f], axis=1).reshape(4 * MLSTM_HEADS, 1)
    bf = lambda a: a.astype(BF16)
    c = PROJ0_COLS
    wall = _relayout(w_in.T, [(o[0], MLA_Q_RANK, c["cqkr"][0]), (o[2], MLA_ROPE, c["cqkr"][0] + MLA_Q_RANK + MLA_NOPE),
                            (o[1], MLA_KV_RANK, c["ckv"][0]), (o[3], o[7] - o[3], c["mqk"][0]),
                            (o[8], D_MODEL, c["z"][0])], c["z"][1])
    return dict(
        wall=wall, wgt=bf(w_in[:, o[7]:o[8]].T),
        qn=q_norm[None], wuq=bf(wuq.reshape(MLA_Q_RANK, -1)),
        kvn=kv_norm[None], wk=bf(wk.reshape(MLA_KV_RANK, -1)), wv=bf(wv),
        cw=jnp.zeros((SUBLANES, conv_w.shape[1]), F32).at[:MLSTM_CONV].set(conv_w), cb=conv_b[None], gb=gbias)


def _tile_specs(nct, nt, nb):
    tile = lambda w: pl.BlockSpec((nb, TM, w), lambda b, t: (b, t, 0))
    per = TM // HALO
    nlat = nt - nct
    src_ctx = pl.BlockSpec((nb, TM, D_MODEL), lambda b, t: (b, jnp.minimum(t, nct - 1), 0))
    src_lat = pl.BlockSpec((nb, TM, D_MODEL), lambda b, t: (b, jnp.maximum(t - nct, 0), 0))
    prev = pl.BlockSpec((nb, HALO, D_MODEL), lambda b, t: (b, jnp.maximum((t - nct) * per - 1, 0), 0))
    nxt = pl.BlockSpec((nb, HALO, D_MODEL),
                       lambda b, t: (b, jnp.clip((t - nct + 1) * per, 0, nlat * per - 1), 0))
    mod = pl.BlockSpec((nb, None, 1, D_MODEL), lambda b, t: (b, (t >= nct).astype(jnp.int32), 0, 0))
    return tile, src_ctx, src_lat, prev, nxt, mod


def _per_element(tile_kernel, shared):
    def kernel(*refs, **kw):
        for bi in range(refs[0].shape[0]):
            tile_kernel(*[r if i in shared else r.at[bi] for i, r in enumerate(refs)], **kw)
    return kernel


def _proj0(ctx, x, scale, shift, norm_g, cos_t, sin_t, w, nct):
    bsz = x.shape[0]
    ttot = ctx.shape[1] + x.shape[1]
    nt = ttot // TM
    nb = MLSTM_HEADS * MLSTM_DH
    sb = math.gcd(bsz, TILE_BATCH)
    tile, src_ctx, src_lat, prev, nxt, mod = _tile_specs(nct, nt, sb)
    tab = pl.BlockSpec((TM, LANES), lambda b, t: (t, 0))
    wnames = ["wall", "wgt", "qn", "wuq", "kvn", "wk", "wv", "cw", "cb", "gb"]
    ws = [w[n] for n in wnames]
    tok = lambda width, dt: jax.ShapeDtypeStruct((bsz, ttot, width), dt)
    out_shape = (tok(MLA_HEADS * LANES, BF16), tok(MLA_HEADS * LANES, BF16), tok(MLA_HEADS * LANES, BF16),
                 tok(nb, BF16), tok(nb, BF16), tok(nb, BF16),
                 jax.ShapeDtypeStruct((bsz, ttot // MLSTM_CHUNK, 2 * MLSTM_HEADS, MLSTM_CHUNK), F32), tok(LANES, F32),
                 tok(nb, BF16), tok(D_MODEL, BF16))
    out_specs = (tile(MLA_HEADS * LANES), tile(MLA_HEADS * LANES), tile(MLA_HEADS * LANES),
                 tile(nb), tile(nb), tile(nb),
                 pl.BlockSpec((sb, TM // MLSTM_CHUNK, 2 * MLSTM_HEADS, MLSTM_CHUNK), lambda b, t: (b, t, 0, 0)),
                 tile(LANES),
                 tile(nb), tile(D_MODEL))
    n_blocked, n_out = 6, len(out_shape)
    n_in = n_blocked + 3 + len(ws)
    shared = set(range(n_blocked, n_in)) | {n_in + n_out}
    return pl.pallas_call(
        functools.partial(_per_element(_proj0_kernel, shared), nct=nct, nt=nt),
        out_shape=out_shape,
        grid=(bsz // sb, nt),
        in_specs=[src_ctx, src_lat, prev, nxt, mod, mod, _full((1, D_MODEL)), tab, tab] + [_full(a.shape) for a in ws],
        out_specs=out_specs,
        scratch_shapes=[pltpu.VMEM((TM + 2 * HALO, 2 * nb), F32)],
        compiler_params=_params("parallel", "parallel"),
        name="proj0",
    )(ctx, x, x, x, scale, shift, norm_g[None], cos_t, sin_t, *ws)


def _mla_kernel(qa_ref, qb_ref, k_ref, v_ref, o_ref, *, nkeys_ctx):
    step = pl.program_id(2)

    def attend(queries, nkeys, rows):
        qk = lambda j: _dot_nt(queries(j), k_ref[0:nkeys, j * LANES:(j + 1) * LANES])
        outs = []
        s_next = qk(0)
        for j in range(MLA_HPS):
            s = s_next
            if j + 1 < MLA_HPS:
                s_next = qk(j + 1)
            p = jnp.exp2(s - jnp.max(s, axis=-1, keepdims=True))
            o = _dot(p.astype(BF16), v_ref[0:nkeys, j * LANES:(j + 1) * LANES])
            outs.append(o / pltpu.roll(o, MLA_V, axis=1))
        lane = lax.broadcasted_iota(jnp.int32, outs[0].shape, 1)
        for j in range(MLA_HPS // 2):
            o_ref[rows, j * LANES:(j + 1) * LANES] = jnp.where(lane < MLA_V, outs[2 * j], outs[2 * j + 1]).astype(BF16)

    @pl.when(step == 0)
    def _():
        attend(lambda j: qa_ref[:, j * LANES:(j + 1) * LANES], nkeys_ctx, slice(0, TM))
        o_ref[TM:, :] = jnp.zeros((o_ref.shape[0] - TM, o_ref.shape[1]), BF16)

    @pl.when(step > 0)
    def _():
        stacked = lambda j: jnp.concatenate([qa_ref[:, j * LANES:(j + 1) * LANES],
                                             qb_ref[:, j * LANES:(j + 1) * LANES]], axis=0)
        attend(stacked, k_ref.shape[0], slice(None))


def _mla_attention(q, k, v, nct):
    bsz, ttot, _ = q.shape
    nt = ttot // TM
    assert nct == 1 and MLA_QTILES == 2 and (nt - nct) % MLA_QTILES == 0
    hw = MLA_HPS * LANES
    nsteps = 1 + (nt - nct) // MLA_QTILES
    return pl.pallas_call(
        functools.partial(_mla_kernel, nkeys_ctx=nct * TM),
        out_shape=jax.ShapeDtypeStruct((bsz, nsteps * MLA_QTILES * TM, MLA_HEADS * MLA_V), BF16),
        grid=(bsz, MLA_HEADS // MLA_HPS, nsteps),
        in_specs=[pl.BlockSpec((None, TM, hw), lambda b, p, s: (b, jnp.maximum(2 * s - 1, 0), p)),
                  pl.BlockSpec((None, TM, hw), lambda b, p, s: (b, jnp.maximum(2 * s, 1), p)),
                  pl.BlockSpec((None, ttot, hw), lambda b, p, s: (b, 0, p)),
                  pl.BlockSpec((None, ttot, hw), lambda b, p, s: (b, 0, p))],
        out_specs=pl.BlockSpec((None, MLA_QTILES * TM, hw // 2), lambda b, p, s: (b, s, p)),
        compiler_params=_params("parallel", "parallel", "parallel"),
        name="mla_attention",
    )(q, q, k, v)


def _mla_tile(t, nct):
    return t + (t >= nct).astype(jnp.int32)


def _scan_block(j, nct, nt):
    return jnp.where(j < nct, nct - 1 - j, nt - 1 - (j - nct))


def _mlstm_kernel(qf_ref, kf_ref, vf_ref, grf_ref, gcf_ref, qr_ref, kr_ref, vr_ref, grr_ref, gcr_ref,
                  hf_ref, hr_ref, c_scr, m_scr):
    @pl.when(pl.program_id(1) == 0)
    def _():
        c_scr[...] = jnp.zeros_like(c_scr)
        m_scr[...] = jnp.zeros_like(m_scr)

    L = MLSTM_CHUNK
    nchunk = TM // L
    nh = MLSTM_HEADS
    nchain = 2 * nh
    row = lax.broadcasted_iota(jnp.int32, (L, L), 0)
    col = lax.broadcasted_iota(jnp.int32, (L, L), 1)
    ones = jnp.ones((L, MLSTM_DH), BF16)
    wide = lambda a: jnp.broadcast_to(a, (L, MLSTM_DH))
    twice = lambda a: jnp.concatenate([a, a], axis=1)
    refs = ((qf_ref, kf_ref, vf_ref, grf_ref, gcf_ref, hf_ref), (qr_ref, kr_ref, vr_ref, grr_ref, gcr_ref, hr_ref))
    nbatch = qf_ref.shape[0]

    def load(ci):
        cs = []
        for bi, d in [(bi, d) for bi in range(nbatch) for d in range(2)]:
            q_ref, k_ref, v_ref, gr_ref, gc_ref, o_ref = (r.at[bi] for r in refs[d])
            cd = ci if d == 0 else nchunk - 1 - ci
            rs = slice(cd * L, (cd + 1) * L)
            g_rows = gr_ref[cd]
            g_cols = gc_ref[rs, :]
            last = L - 1 if d == 0 else 0
            for hd in range(nh):
                ch = d * nh + hd
                hs = slice(hd * MLSTM_DH, (hd + 1) * MLSTM_DH)
                cs.append(dict(
                    rs=rs, hs=hs, o_ref=o_ref, causal=(col <= row) if d == 0 else (col >= row),
                    c_row=g_rows[ch:ch + 1], c_col=g_cols[:, ch:ch + 1],
                    pm_col=g_cols[:, nchain + ch:nchain + ch + 1], b_col=g_cols[:, 2 * nchain + ch:2 * nchain + ch + 1],
                    pm_last=g_cols[last:last + 1, nchain + ch:nchain + ch + 1],
                    b_last=g_cols[last:last + 1, 2 * nchain + ch:2 * nchain + ch + 1],
                    q=q_ref[rs, hs], k=k_ref[rs, hs],
                    v_ext=jnp.concatenate([v_ref[rs, hs], ones], axis=1)))
        return cs

    def prep_scores(cs):
        for c in cs:
            c["s"] = _dot_nt(c["q"], c["k"])

    def prep_weights(cs):
        for c in cs:
            c["pm_w"] = wide(c["pm_col"])
            c["b_w"] = wide(c["b_col"])
            c["w"] = jnp.exp2(jnp.where(c["causal"], c["c_row"] - c["pm_w"][:, :L], -jnp.inf))
            c["kwf"] = jnp.exp2(wide(c["c_col"] - c["pm_last"]))

    def prep_operands(cs):
        for c in cs:
            c["p"] = (c["s"] * c["w"]).astype(BF16)
            c["kw"] = (c["kwf"] * c["k"].astype(F32)).astype(BF16)

    def prep_products(cs):
        for c in cs:
            c["o1"] = _dot(c["p"], c["v_ext"])
            c["dc"] = _dot_tn(c["kw"], c["v_ext"])

    def scan_read(cs):
        for ch, c in enumerate(cs):
            c["qc"] = _dot(c["q"], state[ch].astype(BF16))

    def scan_factors(cs):
        for ch, c in enumerate(cs):
            g_w = jnp.maximum(m[ch], c["pm_w"])
            c["e_intra"] = jnp.exp2(c["pm_w"] - g_w)
            c["e_inter"] = jnp.exp2(m[ch] - g_w)
            c["floor"] = jnp.exp2(-(c["b_w"] + g_w))
            g_last = jnp.maximum(m[ch], c["pm_last"])
            c["keep"] = jnp.exp2(m[ch] - g_last)
            c["gain"] = jnp.exp2(c["pm_last"] - g_last)
            c["m_new"] = c["b_last"] + g_last

    def scan_update(cs):
        for ch, c in enumerate(cs):
            o = twice(c["e_intra"]) * c["o1"] + twice(c["e_inter"]) * c["qc"]
            h = o[:, :MLSTM_DH] / jnp.maximum(jnp.abs(o[:, MLSTM_DH:]), c["floor"])
            c["o_ref"][c["rs"], c["hs"]] = h.astype(BF16)
            state[ch] = c["keep"] * state[ch] + c["gain"] * c["dc"]
            m[ch] = c["m_new"]

    nstate = nbatch * nchain
    state = [c_scr[ch] for ch in range(nstate)]
    m = [m_scr[ch, 0:1, 0:1] for ch in range(nstate)]
    chunks = [load(ci) for ci in range(nchunk)]
    everything = [c for cs in chunks for c in cs]
    for stage in (prep_scores, prep_weights, prep_operands):
        stage(everything)
    prep_products(chunks[0])
    for i, cs in enumerate(chunks):
        scan_read(cs)
        if i + 1 < nchunk:
            prep_products(chunks[i + 1])
        scan_factors(cs)
        scan_update(cs)
    for ch in range(nstate):
        c_scr[ch] = state[ch]
        m_scr[ch] = jnp.broadcast_to(m[ch], m_scr.shape[1:])


def _mlstm(mq, mk, mv, g_rows, g_cols, nct):
    bsz, ttot, nb = mq.shape
    nt = ttot // TM
    per = TM // MLSTM_CHUNK
    sb = math.gcd(bsz, MLSTM_BATCH)
    fwd = lambda w: pl.BlockSpec((sb, TM, w), lambda b, j: (b, j, 0))
    rev = lambda w: pl.BlockSpec((sb, TM, w), lambda b, j: (b, _scan_block(j, nct, nt), 0))
    gshape = (sb, per, 2 * MLSTM_HEADS, MLSTM_CHUNK)
    gfwd = pl.BlockSpec(gshape, lambda b, j: (b, j, 0, 0))
    grev = pl.BlockSpec(gshape, lambda b, j: (b, _scan_block(j, nct, nt), 0, 0))
    nstate = sb * 2 * MLSTM_HEADS
    return pl.pallas_call(
        _mlstm_kernel,
        out_shape=(jax.ShapeDtypeStruct((bsz, ttot, nb), BF16),) * 2,
        grid=(bsz // sb, nt),
        in_specs=[fwd(nb), fwd(nb), fwd(nb), gfwd, fwd(LANES), rev(nb), rev(nb), rev(nb), grev, rev(LANES)],
        out_specs=(fwd(nb), rev(nb)),
        scratch_shapes=[pltpu.VMEM((nstate, MLSTM_DH, 2 * MLSTM_DH), F32),
                        pltpu.VMEM((nstate, SUBLANES, LANES), F32)],
        compiler_params=_params("parallel", "arbitrary"),
        name="mlstm_scan",
    )(mq, mk, mv, g_rows, g_cols, mq, mk, mv, g_rows, g_cols)


def _head_rms(x, n_heads):
    w = x.shape[1] // n_heads
    return jnp.concatenate([_rms(x[:, i * w:(i + 1) * w]) for i in range(n_heads)], axis=1)


def _mid_kernel(ctx_ref, lat_ref, gate_ref, a_ref, hf_ref, hr_ref, mo_ref, z0_ref, hn_ref, wout_ref,
                sc_ref, sh_ref, g_ref, wall_ref, wgate_ref, bgate_ref,
                h_out, gq_out, gk_out, gv_out, bc_out, nq_out, nk_out, nv_out, z_out, *, nct):
    nbt = ctx_ref.shape[0]
    rows = lambda f: jnp.concatenate([f(bi) for bi in range(nbt)], axis=0)
    f32 = lambda r, bi: r[bi].astype(F32)

    def put(ref, val):
        for bi in range(nbt):
            ref[bi] = val[bi * TM:(bi + 1) * TM]

    hm = rows(lambda bi: _head_rms(f32(mo_ref, bi) * (f32(hf_ref, bi) + f32(hr_ref, bi)), MLSTM_HEADS) * hn_ref[...])
    cat = (jnp.concatenate([rows(lambda bi: f32(a_ref, bi)), hm], axis=1) * rows(lambda bi: f32(z0_ref, bi))).astype(BF16)
    y = _dot(cat, wout_ref[...])
    is_ctx = pl.program_id(1) < nct
    hs = [jnp.where(is_ctx, ctx_ref[bi], lat_ref[bi]) + gate_ref[bi] * y[bi * TM:(bi + 1) * TM] for bi in range(nbt)]
    for bi in range(nbt):
        h_out[bi] = hs[bi]

    ub = rows(lambda bi: _modnorm(hs[bi], g_ref[...], sc_ref[bi], sh_ref[bi])).astype(BF16)
    wseg = lambda name: wall_ref[:, PROJ1_COLS[name][0]:PROJ1_COLS[name][1]]
    ga = _dot(ub, wseg("ga")).astype(BF16)
    gate_pre = _dot(ga, wgate_ref[...])
    put(gq_out, (_dot(ub, wseg("q")) * GLA_DK ** -0.5).astype(BF16))
    put(gk_out, _dot(ub, wseg("k")).astype(BF16))
    put(gv_out, _dot(ub, wseg("v")).astype(BF16))

    lg = _log_sigmoid(gate_pre + bgate_ref[...]) * (1.0 / GLA_TAU)
    hi, mid, lo = _split3(lg)
    put(nq_out, (_dot(ub, wseg("nq")) * (NA_DH ** -0.5 * LOG2E)).astype(BF16))
    put(nk_out, _dot(ub, wseg("nk")).astype(BF16))
    put(nv_out, _dot(ub, wseg("nv")).astype(BF16))
    row = lax.broadcasted_iota(jnp.int32, (TM, TM), 0)
    col = lax.broadcasted_iota(jnp.int32, (TM, TM), 1)
    same = (row // CHUNK) == (col // CHUNK)
    nk = GLA_HEADS * GLA_DK
    for d in range(2):
        tri = jnp.where(jnp.logical_and(same, (col <= row) if d == 0 else (col >= row)), 1.0, 0.0).astype(BF16)
        cols = slice(d * nk, (d + 1) * nk)
        for bi in range(nbt):
            rs = slice(bi * TM, (bi + 1) * TM)
            bc_out[bi, :, cols] = (_dot(tri, hi[rs, cols]) + _dot(tri, mid[rs, cols]) + _dot(tri, lo[rs, cols])) * LOG2E
    put(z_out, _silu(_dot(ub, wseg("z"))).astype(BF16))


def _proj1_weights(w_in, w_gate, b_gate):
    nk = GLA_HEADS * GLA_DK
    nc = GLA_HEADS * GLA_DV
    nd = NA_HEADS * NA_DH
    o = np.cumsum([0, nk, nk, nc, 2 * GLA_GATE_RANK, nd, nd, nd, nc + nd])
    r = GLA_GATE_RANK
    c = PROJ1_COLS
    wall = _relayout(w_in.T, [(0, o[3], 0), (o[3], 2 * r, c["ga"][0]), (o[4], o[8] - o[4], c["nq"][0])], c["z"][1])
    wgate = jnp.zeros((LANES, 2 * nk), F32).at[:r, :nk].set(w_gate[0]).at[r:2 * r, nk:].set(w_gate[1])
    return dict(wall=wall, wgate=wgate.astype(BF16), bgate=b_gate.reshape(1, 2 * nk))


def _mid(ctx, x, gate0, a, hf, hr, mo, z0, h_norm, w_out, scale, shift, norm_g, w, nct):
    bsz = x.shape[0]
    ttot = ctx.shape[1] + x.shape[1]
    nt = ttot // TM
    sb = math.gcd(bsz, TILE_BATCH)
    tile, src_ctx, src_lat, _, _, mod = _tile_specs(nct, nt, sb)
    nb = MLSTM_HEADS * MLSTM_DH
    nk = GLA_HEADS * GLA_DK
    nc = GLA_HEADS * GLA_DV
    nd = NA_HEADS * NA_DH
    ws = [w[n] for n in ("wall", "wgate", "bgate")]
    tok = lambda width, dt: jax.ShapeDtypeStruct((bsz, ttot, width), dt)
    return pl.pallas_call(
        functools.partial(_mid_kernel, nct=nct),
        out_shape=(tok(D_MODEL, F32), tok(nk, BF16), tok(nk, BF16), tok(nc, BF16), tok(2 * nk, F32),
                   tok(nd, BF16), tok(nd, BF16), tok(nd, BF16), tok(nc + nd, BF16)),
        grid=(bsz // sb, nt),
        in_specs=[src_ctx, src_lat, mod,
                  pl.BlockSpec((sb, TM, MLA_HEADS * MLA_V), lambda b, t: (b, _mla_tile(t, nct), 0)),
                  tile(nb), tile(nb), tile(nb), tile(D_MODEL),
                  _full((1, nb)), _full(w_out.shape), mod, mod, _full((1, D_MODEL))] + [_full(a.shape) for a in ws],
        out_specs=(tile(D_MODEL), tile(nk), tile(nk), tile(nc), tile(2 * nk), tile(nd), tile(nd), tile(nd),
                   tile(nc + nd)),
        compiler_params=_params("parallel", "parallel"),
        name="out0_proj1",
    )(ctx, x, gate0, a, hf, hr, mo, z0, h_norm[None], w_out.astype(BF16), scale, shift, norm_g[None], *ws)


def _gla_kernel(qf_ref, kf_ref, vf_ref, bf_ref, qr_ref, kr_ref, vr_ref, br_ref, of_ref, or_ref, s_scr):
    @pl.when(pl.program_id(1) == 0)
    def _():
        s_scr[...] = jnp.zeros_like(s_scr)

    L = CHUNK
    nchunk = TM // L
    npair = GLA_HEADS // 2
    lane = lax.broadcasted_iota(jnp.int32, (L, LANES), 1)
    rowv = lax.broadcasted_iota(jnp.int32, (L, LANES), 0)
    srow = lax.broadcasted_iota(jnp.int32, (LANES, 2 * GLA_DV), 0)
    scol = lax.broadcasted_iota(jnp.int32, (LANES, 2 * GLA_DV), 1)
    own_block = (srow < GLA_DK) == (scol < GLA_DV)
    vcol = lax.broadcasted_iota(jnp.int32, (L, 2 * GLA_DV), 1)
    e_row = lax.broadcasted_iota(jnp.int32, (LANES, LANES), 0)
    e_col = lax.broadcasted_iota(jnp.int32, (LANES, LANES), 1)
    head_ones = jnp.where((e_row < GLA_DK) == (e_col < GLA_DK), 1.0, 0.0).astype(BF16)
    leaf_idx = lax.broadcasted_iota(jnp.int32, (GLA_LEAF * GLA_LEAF, LANES), 0)
    leaf_lane = lax.broadcasted_iota(jnp.int32, (GLA_LEAF * GLA_LEAF, LANES), 1) % L
    leaf_s, leaf_t = leaf_idx // GLA_LEAF, leaf_idx % GLA_LEAF
    rep = lambda a: jnp.concatenate(
        [jnp.broadcast_to(a[s:s + 1], (GLA_LEAF, a.shape[1])) for s in range(GLA_LEAF)], axis=0)
    til = lambda a: jnp.concatenate([a] * GLA_LEAF, axis=0)

    def chunk_step(ci, carry):
        cs = []
        for bi, d in [(bi, d) for bi in range(qf_ref.shape[0]) for d in range(2)]:
            q_ref, k_ref, v_ref, b_ref, o_ref = (
                r.at[bi] for r in ((qf_ref, kf_ref, vf_ref, bf_ref, of_ref), (qr_ref, kr_ref, vr_ref, br_ref, or_ref))[d])
            cd = ci if d == 0 else nchunk - 1 - ci
            base = pl.multiple_of(cd * L, L)
            for p in range(npair):
                ls = slice(p * LANES, (p + 1) * LANES)
                bc = b_ref[pl.ds(base, L), ls]
                cs.append(dict(
                    d=d, p=p, bi=bi, base=base, o_ref=o_ref, bc=bc,
                    tau_v=rowv if d == 0 else L - 1 - rowv,
                    tau_s=(lane % L) if d == 0 else L - 1 - (lane % L),
                    q=q_ref[pl.ds(base, L), ls].astype(F32),
                    k=k_ref[pl.ds(base, L), ls].astype(F32),
                    v=v_ref[pl.ds(base, L), p * 2 * GLA_DV:(p + 1) * 2 * GLA_DV],
                    b_last=bc[L - 1:L] if d == 0 else bc[0:1],
                    state=s_scr[bi, d, p]))

        for c in cs:
            c["o_pair"] = _dot((c["q"] * jnp.exp2(c["bc"])).astype(BF16), c["state"].astype(BF16))

        for c in cs:
            c["att"] = jnp.zeros((L, LANES), F32)
        bs = L // 2
        while bs >= GLA_LEAF:
            for c in cs:
                d, bc, tau_v = c["d"], c["bc"], c["tau_v"]
                later = (tau_v // bs) % 2 == 1
                ref_b = None
                for blk in range(L // (2 * bs)):
                    tau_ref = blk * 2 * bs + bs - 1
                    idx = tau_ref if d == 0 else L - 1 - tau_ref
                    r = jnp.broadcast_to(bc[idx:idx + 1], (L, LANES))
                    ref_b = r if ref_b is None else jnp.where(tau_v // (2 * bs) == blk, r, ref_b)
                c["qs"] = jnp.where(later, c["q"] * jnp.exp2(jnp.where(later, bc - ref_b, 0.0)), 0.0).astype(BF16)
                ks = jnp.where(later, 0.0, c["k"] * jnp.exp2(jnp.where(later, 0.0, ref_b - bc)))
                c["ks2"] = jnp.concatenate([jnp.where(lane < GLA_DK, ks, 0.0), jnp.where(lane < GLA_DK, 0.0, ks)],
                                           axis=0).astype(BF16)
            for c in cs:
                same_parent = (c["tau_v"] // (2 * bs)) == (c["tau_s"] // (2 * bs))
                c["att"] = c["att"] + jnp.where(same_parent, _dot_nt(c["qs"], c["ks2"]), 0.0)
            bs //= 2

        for c in cs:
            c["leaf"] = []
        for blk in range(L // GLA_LEAF):
            rs = slice(blk * GLA_LEAF, (blk + 1) * GLA_LEAF)
            for c in cs:
                qb, bb, kb = c["q"][rs], c["bc"][rs], c["k"][rs]
                ok = (leaf_t >= leaf_s) if c["d"] == 0 else (leaf_t <= leaf_s)
                w = jnp.where(ok, til(qb) * jnp.exp2(til(bb) - rep(bb)) * rep(kb), 0.0)
                c["a"] = _dot(w.astype(BF16), head_ones)
            for c in cs:
                a = jnp.where(leaf_lane == leaf_s + blk * GLA_LEAF, c["a"], 0.0)
                acc = a[0:GLA_LEAF]
                for s in range(1, GLA_LEAF):
                    acc = acc + a[s * GLA_LEAF:(s + 1) * GLA_LEAF]
                c["leaf"].append(acc)

        for c in cs:
            v = c["v"]
            c["att"] = (c["att"] + jnp.concatenate(c["leaf"], axis=0)).astype(BF16)
            c["v_blk"] = jnp.concatenate([jnp.where(vcol < GLA_DV, v, jnp.zeros_like(v)),
                                          jnp.where(vcol < GLA_DV, jnp.zeros_like(v), v)], axis=0)
            c["ke"] = (c["k"] * jnp.exp2(c["b_last"] - c["bc"])).astype(BF16)
            c["decay_col"] = jnp.sum(
                jnp.where(e_row == e_col, jnp.broadcast_to(jnp.exp2(c["b_last"]), (LANES, LANES)), 0.0),
                axis=1, keepdims=True)
        for c in cs:
            c["o"] = c["o_pair"] + _dot(c["att"], c["v_blk"])
            c["ds"] = _dot_tn(c["ke"], c["v"])
        for c in cs:
            p = c["p"]
            c["o_ref"][pl.ds(c["base"], L), p * 2 * GLA_DV:(p + 1) * 2 * GLA_DV] = c["o"].astype(BF16)
            s_scr[c["bi"], c["d"], p] = c["decay_col"] * c["state"] + jnp.where(own_block, c["ds"], 0.0)
        return carry

    lax.fori_loop(0, nchunk, chunk_step, 0)


def _gla(gq, gk, gv, bc, nct):
    bsz, ttot, nk = gq.shape
    nc = gv.shape[2]
    nt = ttot // TM
    sb = math.gcd(bsz, GLA_BATCH)
    fwd = lambda w: pl.BlockSpec((sb, TM, w), lambda b, j: (b, j, 0))
    rev = lambda w: pl.BlockSpec((sb, TM, w), lambda b, j: (b, _scan_block(j, nct, nt), 0))
    bfwd = pl.BlockSpec((sb, TM, nk), lambda b, j: (b, j, 0))
    brev = pl.BlockSpec((sb, TM, nk), lambda b, j: (b, _scan_block(j, nct, nt), 1))
    return pl.pallas_call(
        _gla_kernel,
        out_shape=(jax.ShapeDtypeStruct((bsz, ttot, nc), BF16),) * 2,
        grid=(bsz // sb, nt),
        in_specs=[fwd(nk), fwd(nk), fwd(nc), bfwd, rev(nk), rev(nk), rev(nc), brev],
        out_specs=(fwd(nc), rev(nc)),
        scratch_shapes=[pltpu.VMEM((sb, 2, GLA_HEADS // 2, LANES, 2 * GLA_DV), F32)],
        compiler_params=_params("parallel", "arbitrary"),
        name="gla_scan",
    )(gq, gk, gv, bc, gq, gk, gv, bc)


def _na_bias_tables(rpb, rows):
    kh = NA_KH
    pad = GRID_W - NA_KW
    rpb_pad = jnp.pad(rpb.astype(F32) * LOG2E, ((0, 0), (0, 0), (pad, pad + 1)))
    assert rpb_pad.shape[2] == LANES
    plan = []
    for r0 in (0, NA_QROWS, rows - NA_QROWS):
        kb = int(np.clip(r0 - kh // 2, 0, rows - NA_KROWS))
        for qrow in range(r0, r0 + NA_QROWS):
            ws = int(np.clip(qrow - kh // 2, 0, rows - kh))
            plan.append(tuple(krow - qrow + NA_KH - 1 if ws <= krow < ws + kh else None
                              for krow in range(kb, kb + NA_KROWS)))
    return pl.pallas_call(
        functools.partial(_na_table_kernel, plan=tuple(plan)),
        out_shape=jax.ShapeDtypeStruct((NA_HEADS, len(plan) * GRID_W, NA_KROWS * GRID_W), F32),
        grid=(NA_HEADS,),
        in_specs=[pl.BlockSpec((None,) + rpb_pad.shape[1:], lambda h: (h, 0, 0))],
        out_specs=pl.BlockSpec((None, len(plan) * GRID_W, NA_KROWS * GRID_W), lambda h: (h, 0, 0)),
        compiler_params=_params("parallel"),
        name="na_bias_table",
    )(rpb_pad)


def _na_table_kernel(rpb_ref, o_ref, *, plan):
    assert 2 * GRID_W == LANES
    lane = lax.broadcasted_iota(jnp.int32, (GRID_W, LANES), 1)
    masked = jnp.full((GRID_W, LANES), NEG_BIG, F32)
    col = lax.broadcasted_iota(jnp.int32, (GRID_W, LANES), 0)
    first = jnp.clip(col - NA_KW // 2, 0, GRID_W - NA_KW)
    col_ok = (lane >= first) & (lane < first + NA_KW)
    blocks = {}

    def block(idx):
        if idx is None:
            return masked
        if idx not in blocks:
            row = jnp.broadcast_to(rpb_ref[idx:idx + 1, :], (GRID_W, LANES))
            shifted = pltpu.roll(row, LANES - (GRID_W - 1), axis=1, stride=1, stride_axis=0)
            blocks[idx] = jnp.where(col_ok, shifted, NEG_BIG)
        return blocks[idx]

    for i, entries in enumerate(plan):
        rows = slice(i * GRID_W, (i + 1) * GRID_W)
        for k in range(0, len(entries) - 1, 2):
            pair = jnp.where(lane < GRID_W, block(entries[k]), pltpu.roll(block(entries[k + 1]), GRID_W, axis=1))
            o_ref[rows, k * GRID_W:(k + 2) * GRID_W] = pair
        if len(entries) % 2:
            k = len(entries) - 1
            o_ref[rows, k * GRID_W:(k + 1) * GRID_W] = block(entries[k])[:, :GRID_W]


def _na_heads(q_ref, kw_ref, vw_ref, kc_ref, vc_ref, bias_ref):
    nq = NA_QROWS * GRID_W
    lane = lax.broadcasted_iota(jnp.int32, (nq, LANES), 1)
    def scores(j):
        ls = slice((j // 2) * LANES, (j // 2 + 1) * LANES)
        q = q_ref[:, ls]
        qj = jnp.where((lane < NA_DH) == (j % 2 == 0), q, jnp.zeros_like(q))
        return _dot_nt(qj, kw_ref[:, ls]) + bias_ref[j], _dot_nt(qj, kc_ref[:, ls])

    outs = []
    nxt = scores(0)
    for j in range(NA_HEADS):
        s_loc, s_ctx = nxt
        if j + 1 < NA_HEADS:
            nxt = scores(j + 1)
        ls = slice((j // 2) * LANES, (j // 2 + 1) * LANES)
        m = jnp.maximum(jnp.max(s_loc, axis=-1, keepdims=True), jnp.max(s_ctx, axis=-1, keepdims=True))
        p_loc = jnp.exp2(s_loc - m)
        p_ctx = jnp.exp2(s_ctx - m)
        den = jnp.sum(p_loc, axis=-1, keepdims=True) + jnp.sum(p_ctx, axis=-1, keepdims=True)
        outs.append((_dot(p_loc.astype(BF16), vw_ref[:, ls]) + _dot(p_ctx.astype(BF16), vc_ref[:, ls])) / den)
    return jnp.concatenate([jnp.where(lane < NA_DH, outs[2 * i], outs[2 * i + 1]) for i in range(NA_HEADS // 2)], axis=1)


def _na_out1_kernel(*refs, nb):
    kv = refs[:4 * nb]
    q_ref, bias_ref, h_ref, gate_ref, of_ref, or_ref, z_ref, gn_ref, w_ref, fn_ref, o_ref = refs[4 * nb:]
    rows = lambda f: jnp.concatenate([f(bi) for bi in range(nb)], axis=0)
    f32 = lambda r, bi: r[bi].astype(F32)
    na = rows(lambda bi: _na_heads(q_ref.at[bi], *kv[4 * bi:4 * bi + 4], bias_ref))
    g = rows(lambda bi: _head_rms(f32(of_ref, bi) + f32(or_ref, bi), GLA_HEADS) * gn_ref[...])
    cat = (jnp.concatenate([g, na], axis=1) * rows(lambda bi: f32(z_ref, bi))).astype(BF16)
    y = _dot(cat, w_ref[...])
    for bi in range(nb):
        h = h_ref[bi] + gate_ref[bi] * y[bi * TM:(bi + 1) * TM]
        o_ref[bi] = _rms(h) * fn_ref[...]


def _na_out1(nq, nk, nv, bias, h, gate, of, orv, z, gla_norm, w_out, final_norm, n_ctx, rows):
    bsz, ttot, nd = nq.shape
    nrb = rows // NA_QROWS
    qrows = NA_QROWS * GRID_W
    krows = NA_KROWS * GRID_W
    assert qrows == TM
    nct = n_ctx // TM
    nc = GLA_HEADS * GLA_DV
    sb = math.gcd(bsz, TILE_BATCH)

    def variant(rb):
        return jnp.where(rb == 0, 0, jnp.where(rb == nrb - 1, 2, 1))

    def key_start(rb):
        first_row = jnp.clip(rb * NA_QROWS - NA_KH // 2, 0, rows - NA_KROWS)
        return pl.multiple_of(n_ctx + first_row * GRID_W, GRID_W)

    window = lambda i: pl.BlockSpec((None, pl.Element(krows), pl.Element(nd)), lambda b, rb: (b * sb + i, key_start(rb), 0))
    context = lambda i: pl.BlockSpec((None, pl.Element(n_ctx), pl.Element(nd)), lambda b, rb: (b * sb + i, 0, 0))
    kv_specs = [spec(i) for i in range(sb) for spec in (window, window, context, context)]
    lat = lambda w: pl.BlockSpec((sb, TM, w), lambda b, rb: (b, rb + nct, 0))
    return pl.pallas_call(
        functools.partial(_na_out1_kernel, nb=sb),
        out_shape=jax.ShapeDtypeStruct((bsz, rows * GRID_W, D_MODEL), F32),
        grid=(bsz // sb, nrb),
        in_specs=kv_specs + [
            lat(nd), pl.BlockSpec((NA_HEADS, qrows, krows), lambda b, rb: (0, variant(rb), 0)),
            lat(D_MODEL), pl.BlockSpec((sb, None, 1, D_MODEL), lambda b, rb: (b, 1, 0, 0)),
            lat(nc), lat(nc), lat(nc + nd), _full((1, nc)), _full(w_out.shape), _full((1, D_MODEL))],
        out_specs=pl.BlockSpec((sb, TM, D_MODEL), lambda b, rb: (b, rb, 0)),
        compiler_params=_params("parallel", "parallel"),
        name="na_out1",
    )(*([nk, nv, nk, nv] * sb), nq, bias, h, gate, of, orv, z, gla_norm[None], w_out.astype(BF16), final_norm[None])


def kernel(x, c, ctx, c_ctx, l0_norm, l0_w_mod, l0_b_mod, l0_w_in, l0_mla_q_norm, l0_mla_w_uq, l0_mla_kv_norm, l0_mla_w_ukv, l0_mlstm_conv_w, l0_mlstm_conv_b, l0_mlstm_b_i, l0_mlstm_b_f, l0_mlstm_norm, l0_w_out, l1_norm, l1_w_mod, l1_b_mod, l1_w_in, l1_gla_w_gate, l1_gla_b_gate, l1_gla_norm, l1_na_rpb, l1_w_out, final_norm):
    bsz, seq, d = x.shape
    n_ctx = ctx.shape[1]
    rows = seq // GRID_W
    assert d == D_MODEL and seq % TM == 0 and n_ctx == TM
    assert rows % NA_QROWS == 0 and rows >= NA_KROWS and rows // NA_QROWS >= 3
    nct = n_ctx // TM

    (shift0, scale0, gate0), (shift, scale, gate) = _mod_vectors(
        c, c_ctx, ((l0_w_mod, l0_b_mod), (l1_w_mod, l1_b_mod)))

    cos_t, sin_t = _rope_tables(n_ctx, seq)
    w0 = _proj0_weights(l0_w_in, l0_mla_q_norm, l0_mla_w_uq, l0_mla_kv_norm, l0_mla_w_ukv,
                        l0_mlstm_conv_w, l0_mlstm_conv_b, l0_mlstm_b_i, l0_mlstm_b_f)
    q, k, v, mq, mk, mv, g_rows, g_cols, mo, z0 = _proj0(ctx, x, scale0, shift0, l0_norm, cos_t, sin_t, w0, nct)
    a = _mla_attention(q, k, v, nct)
    hf, hr = _mlstm(mq, mk, mv, g_rows, g_cols, nct)

    w1 = _proj1_weights(l1_w_in, l1_gla_w_gate, l1_gla_b_gate)
    h, gq, gk, gv, bc, nq, nk, nv, z = _mid(ctx, x, gate0, a, hf, hr, mo, z0, l0_mlstm_norm, l0_w_out,
                                            scale, shift, l1_norm, w1, nct)
    of, orv = _gla(gq, gk, gv, bc, nct)
    return _na_out1(nq, nk, nv, _na_bias_tables(l1_na_rpb, rows), h, gate, of, orv, z,
                    l1_gla_norm, l1_w_out, final_norm, n_ctx, rows)
```

```python
import functools
import math

import jax
import jax.numpy as jnp
import numpy as np
from jax import lax
from jax.experimental import pallas as pl
from jax.experimental.pallas import tpu as pltpu

F32 = jnp.float32
BF16 = jnp.bfloat16

D_MODEL = 1024
GRID_W = 64
EPS = 1e-6
ROPE_BASE = 10000.0

MLA_HEADS = 8
MLA_Q_RANK = 384
MLA_KV_RANK = 256
MLA_NOPE = 64
MLA_ROPE = 32
MLA_V = 64
ROPE_PAIRS = MLA_ROPE // 4
MLSTM_HEADS = 4
MLSTM_DH = 128
MLSTM_CONV = 3
GLA_HEADS = 4
GLA_DK = 64
GLA_DV = 128
GLA_GATE_RANK = 16
GLA_TAU = 16.0
NA_HEADS = 8
NA_DH = 64
NA_KH = 8
NA_KW = 16

CHUNK = 64
MLSTM_CHUNK = 128
TM = 256
LANES = 128
SUBLANES = 8
HALO = SUBLANES
NA_QROWS = 4
NA_KROWS = NA_QROWS + NA_KH - 1
GLA_LEAF = 8
MLA_HPS = 4
MLA_QTILES = 2
TILE_BATCH = 2
MLSTM_BATCH = 2
GLA_BATCH = 8
VMEM_LIMIT = 56 * 1024 * 1024
NEG_BIG = -1e30
LOG2E = 1.4426950408889634


def _dot(a, b):
    return jnp.dot(a, b, preferred_element_type=F32)


def _dot_nt(a, b):
    return lax.dot_general(a, b, (((1,), (1,)), ((), ())), preferred_element_type=F32)


def _dot_tn(a, b):
    return lax.dot_general(a, b, (((0,), (0,)), ((), ())), preferred_element_type=F32)


def _rms(x):
    return x * lax.rsqrt(jnp.mean(x * x, axis=-1, keepdims=True) + EPS)


def _sigmoid(x):
    return 1.0 / (1.0 + jnp.exp(-x))


def _silu(x):
    return x * _sigmoid(x)


def _log_sigmoid(x):
    return jnp.minimum(x, 0.0) - jnp.log(1.0 + jnp.exp(-jnp.abs(x)))


def _params(*sem):
    return pltpu.CompilerParams(dimension_semantics=sem, vmem_limit_bytes=VMEM_LIMIT)


def _full(shape):
    nd = len(shape)
    return pl.BlockSpec(shape, lambda *_: (0,) * nd)


def _mod_kernel(c_ref, w0_ref, b0_ref, w1_ref, b1_ref, o0_ref, o1_ref):
    sc = _silu(c_ref[...]).astype(BF16)
    o0_ref[...] = _dot(sc, w0_ref[...].astype(BF16)) + b0_ref[...]
    o1_ref[...] = _dot(sc, w1_ref[...].astype(BF16)) + b1_ref[...]


def _mod_vectors(c, c_ctx, mods):
    bsz, d = c.shape
    rows = -(-(bsz + 1) // SUBLANES) * SUBLANES
    cc = jnp.zeros((rows, d), F32).at[:bsz].set(c).at[bsz].set(c_ctx)
    (w0, b0), (w1, b1) = mods
    n = w0.shape[1]
    wspec = pl.BlockSpec((d, d), lambda j: (0, j))
    vspec = pl.BlockSpec((1, d), lambda j: (0, j))
    ospec = pl.BlockSpec((rows, d), lambda j: (0, j))
    per_layer = pl.pallas_call(
        _mod_kernel,
        out_shape=(jax.ShapeDtypeStruct((rows, n), F32),) * 2,
        grid=(n // d,),
        in_specs=[_full((rows, d)), wspec, vspec, wspec, vspec],
        out_specs=(ospec, ospec),
        compiler_params=_params("parallel"),
        name="modulation",
    )(cc, w0, b0[None], w1, b1[None])
    out = []
    for mod in per_layer:
        vecs = []
        for part in jnp.split(mod, 3, axis=-1):
            ctx_v = jnp.broadcast_to(part[bsz][None], (bsz, d))
            vecs.append(jnp.stack([ctx_v, part[:bsz]], axis=1)[:, :, None, :])
        out.append(vecs)
    return out


def _modnorm(x, g, scale, shift):
    return _rms(x) * g * (1.0 + scale) + shift


def _seg_scan(x, reverse, op, fill):
    width = x.shape[1]
    pos = lax.broadcasted_iota(jnp.int32, x.shape, 1) % MLSTM_CHUNK
    k = 1
    while k < MLSTM_CHUNK:
        if reverse:
            x = op(x, jnp.where(pos < MLSTM_CHUNK - k, pltpu.roll(x, width - k, axis=1), fill))
        else:
            x = op(x, jnp.where(pos >= k, pltpu.roll(x, k, axis=1), fill))
        k *= 2
    return x


def _split3(x):
    hi = x.astype(BF16)
    r1 = x - hi.astype(F32)
    mid = r1.astype(BF16)
    return hi, mid, (r1 - mid.astype(F32)).astype(BF16)


def _proj0_kernel(ctx_ref, lat_ref, hp_ref, hn_ref, sc_ref, sh_ref, g_ref, cos_ref, sin_ref,
                  wall_ref, wgt_ref, qn_ref, wuq_ref, kvn_ref, wk_ref, wv_ref, cw_ref, cb_ref, gb_ref,
                  q_out, k_out, v_out, mq_out, mk_out, mv_out, gr_out, gc_out, mo_out, z_out,
                  pbuf, *, nct, nt):
    t = pl.program_id(1)
    h = jnp.where(t < nct, ctx_ref[...], lat_ref[...])
    x = jnp.concatenate([hp_ref[...], h, hn_ref[...]], axis=0)
    ub_ext = _modnorm(x, g_ref[...], sc_ref[...], sh_ref[...]).astype(BF16)
    ub = ub_ext[HALO:HALO + TM]

    gt = _dot_nt(wgt_ref[...], ub) + gb_ref[...]
    half = MLSTM_HEADS
    per_dir = []
    for d in range(2):
        gd = gt[d * SUBLANES:(d + 1) * SUBLANES]
        bcum = _seg_scan(_log_sigmoid(gd), d == 1, jnp.add, 0.0)
        b_top = pltpu.roll(bcum, half, axis=0)
        c8 = gd - b_top
        pm8 = _seg_scan(c8, d == 1, jnp.maximum, -jnp.inf)
        per_dir.append((c8, pm8, b_top))

    cos = cos_ref[...]
    sin = sin_ref[...]
    cos_all = jnp.concatenate([cos] * MLA_HEADS, axis=1)
    sin_all = jnp.concatenate([sin] * MLA_HEADS, axis=1)

    def swap_halves(a):
        lane = lax.broadcasted_iota(jnp.int32, a.shape, 1)
        first = lane % (2 * ROPE_PAIRS) < ROPE_PAIRS
        return jnp.where(first, pltpu.roll(a, a.shape[1] - ROPE_PAIRS, axis=1), pltpu.roll(a, ROPE_PAIRS, axis=1))

    wseg = lambda name: wall_ref[:, PROJ0_COLS[name][0]:PROJ0_COLS[name][1]]
    cqkr = _dot(ub, wseg("cqkr"))
    cq = cqkr[:, :MLA_Q_RANK]
    kr = cqkr[:, MLA_Q_RANK:]
    ckv = _dot(ub, wseg("ckv"))
    pqk = _dot(ub_ext, wseg("mqk"))

    nq = (_rms(cq) * qn_ref[...]).astype(BF16)
    nkv = (_rms(ckv) * kvn_ref[...]).astype(BF16)

    q = _dot(nq, wuq_ref[...])
    kn = _dot(nkv, wk_ref[...])
    vv = _dot(nkv, wv_ref[...])

    prev_ok = t > nct
    next_ok = jnp.logical_and(t >= nct, t != nt - 1)
    row = lax.broadcasted_iota(jnp.int32, pqk.shape, 0)
    keep = jnp.logical_and(jnp.logical_or(row >= HALO, prev_ok), jnp.logical_or(row < HALO + TM, next_ok))
    pbuf[...] = jnp.where(keep, pqk, 0.0)
    cw = cw_ref[...]
    y = (pbuf[HALO - 1:HALO - 1 + TM, :] * cw[0:1] + pbuf[HALO:HALO + TM, :] * cw[1:2]
         + pbuf[HALO + 1:HALO + 1 + TM, :] * cw[2:3] + cb_ref[...])
    y = _silu(y)
    nb = MLSTM_HEADS * MLSTM_DH
    mq_out[...] = (y[:, :nb] * MLSTM_DH ** -0.5).astype(BF16)
    mk_out[...] = y[:, nb:].astype(BF16)

    mv = _dot(ub, wseg("mv"))
    mo = _dot(ub, wseg("mo"))

    q = q * cos_all + swap_halves(q) * sin_all
    q_out[...] = (q * ((MLA_NOPE + MLA_ROPE) ** -0.5 * LOG2E)).astype(BF16)
    rot = kr * cos + swap_halves(kr) * sin
    k_out[...] = (kn + jnp.concatenate([rot] * MLA_HEADS, axis=1)).astype(BF16)
    vlane = lax.broadcasted_iota(jnp.int32, (TM, MLA_HEADS * LANES), 1)
    is_value = ((vlane % LANES) < MLA_V) == ((vlane // LANES) % 2 == 0)
    v_out[...] = jnp.where(is_value, vv, 1.0).astype(BF16)

    zz = _dot(ub, wseg("z"))
    mv_out[...] = mv.astype(BF16)
    mo_out[...] = _sigmoid(mo).astype(BF16)
    z_out[...] = _silu(zz).astype(BF16)

    r8 =lax.broadcasted_iota(jnp.int32, (SUBLANES, TM), 0)
    both = lambda i: jnp.where(r8 < half, per_dir[0][i], pltpu.roll(per_dir[1][i], half, axis=0)) * LOG2E
    c_rows, pm_rows, b_rows = both(0), both(1), both(2)
    for ci in range(TM // MLSTM_CHUNK):
        gr_out[ci] = c_rows[:, ci * MLSTM_CHUNK:(ci + 1) * MLSTM_CHUNK]
    stack = jnp.concatenate([c_rows, pm_rows, b_rows, jnp.zeros((LANES - 3 * SUBLANES, TM), F32)], axis=0)
    er = lax.broadcasted_iota(jnp.int32, (TM, TM), 0)
    ec = lax.broadcasted_iota(jnp.int32, (TM, TM), 1)
    eye = jnp.where(er == ec, 1.0, 0.0).astype(BF16)
    gc_out[...] = sum(_dot_nt(eye, part) for part in _split3(stack))


def _rope_tables(n_ctx, seq):
    t = jnp.arange(seq)
    inv = 1.0 / (ROPE_BASE ** (jnp.arange(ROPE_PAIRS, dtype=F32) / ROPE_PAIRS))
    ang = jnp.concatenate([(t // GRID_W)[:, None] * inv, (t % GRID_W)[:, None] * inv], axis=-1)
    cos, sin = jnp.cos(ang), jnp.sin(ang)
    j = np.arange(MLA_ROPE)
    src = (j // (2 * ROPE_PAIRS)) * ROPE_PAIRS + (j % ROPE_PAIRS)
    sign = np.where((j % (2 * ROPE_PAIRS)) < ROPE_PAIRS, -1.0, 1.0).astype(np.float32)
    cos_full = cos[:, src]
    sin_full = sin[:, src] * sign
    cos_t = jnp.ones((n_ctx + seq, LANES), F32).at[n_ctx:, MLA_NOPE:MLA_NOPE + MLA_ROPE].set(cos_full)
    sin_t = jnp.zeros((n_ctx + seq, LANES), F32).at[n_ctx:, MLA_NOPE:MLA_NOPE + MLA_ROPE].set(sin_full)
    return cos_t, sin_t


def _relayout_kernel(wt_ref, o_ref, *, moves):
    lane = lax.broadcasted_iota(jnp.int32, (o_ref.shape[0], LANES), 1)
    end = 0
    for src, n, dst in moves:
        group = dst - dst % LANES
        if group > end:
            o_ref[:, end:group] = jnp.zeros((o_ref.shape[0], group - end), o_ref.dtype)
        if n % LANES == 0 and dst == group:
            o_ref[:, dst:dst + n] = wt_ref[src:src + n, :].T.astype(o_ref.dtype)
            end = dst + n
        else:
            assert n < LANES and dst + n <= group + LANES and src + LANES <= wt_ref.shape[0]
            t = pltpu.roll(wt_ref[src:src + LANES, :].T, dst - group, axis=1)
            keep = (lane >= dst - group) & (lane < dst - group + n)
            o_ref[:, group:group + LANES] = jnp.where(keep, t, 0.0).astype(o_ref.dtype)
            end = group + LANES
    if end < o_ref.shape[1]:
        o_ref[:, end:] = jnp.zeros((o_ref.shape[0], o_ref.shape[1] - end), o_ref.dtype)


def _relayout(wt, moves, width):
    rows = wt.shape[1]
    rb = math.gcd(rows, TM)
    return pl.pallas_call(
        functools.partial(_relayout_kernel, moves=tuple(moves)),
        out_shape=jax.ShapeDtypeStruct((rows, width), BF16),
        grid=(rows // rb,),
        in_specs=[pl.BlockSpec((wt.shape[0], rb), lambda i: (0, i))],
        out_specs=pl.BlockSpec((rb, width), lambda i: (i, 0)),
        compiler_params=_params("parallel"),
        name="weight_relayout",
    )(wt)


def _segments(widths):
    out, start = {}, 0
    for name, width in widths:
        out[name] = (start, start + width)
        start += width
    return out


PROJ0_COLS = _segments([("cqkr", MLA_Q_RANK + LANES), ("ckv", MLA_KV_RANK), ("mqk", 2 * MLSTM_HEADS * MLSTM_DH),
                        ("mv", MLSTM_HEADS * MLSTM_DH), ("mo", MLSTM_HEADS * MLSTM_DH), ("z", D_MODEL)])
PROJ1_COLS = _segments([("q", GLA_HEADS * GLA_DK), ("k", GLA_HEADS * GLA_DK), ("v", GLA_HEADS * GLA_DV),
                        ("ga", LANES), ("nq", NA_HEADS * NA_DH), ("nk", NA_HEADS * NA_DH), ("nv", NA_HEADS * NA_DH),
                        ("z", GLA_HEADS * GLA_DV + NA_HEADS * NA_DH)])


def _proj0_weights(w_in, q_norm, w_uq, kv_norm, w_ukv, conv_w, conv_b, b_i, b_f):
    o = np.cumsum([0, MLA_Q_RANK, MLA_KV_RANK, MLA_ROPE] + [MLSTM_HEADS * MLSTM_DH] * 4 + [4 * MLSTM_HEADS, D_MODEL])
    dq = MLA_NOPE + MLA_ROPE
    wuq = jnp.pad(w_uq.reshape(MLA_Q_RANK, MLA_HEADS, dq), ((0, 0), (0, 0), (0, LANES - dq)))
    wkv = w_ukv.reshape(MLA_KV_RANK, MLA_HEADS, MLA_NOPE + MLA_V)
    wk = jnp.pad(wkv[:, :, :MLA_NOPE], ((0, 0), (0, 0), (0, LANES - MLA_NOPE)))
    zv = jnp.zeros((MLA_KV_RANK, MLA_HEADS // 2, MLA_V), F32)
    wv = jnp.concatenate([wkv[:, 0::2, MLA_NOPE:], zv, zv, wkv[:, 1::2, MLA_NOPE:]], axis=-1)
    wv = wv.reshape(MLA_KV_RANK, MLA_HEADS * LANES)
    gbias = jnp.stack([b_i, b_f], axis=1).reshape(4 * MLSTM_HEADS, 1)
    bf = lambda a: a.astype(BF16)
    c = PROJ0_COLS
    wall = _relayout(w_in.T, [(o[0], MLA_Q_RANK, c["cqkr"][0]), (o[2], MLA_ROPE, c["cqkr"][0] + MLA_Q_RANK + MLA_NOPE),
                            (o[1], MLA_KV_RANK, c["ckv"][0]), (o[3], o[7] - o[3], c["mqk"][0]),
                            (o[8], D_MODEL, c["z"][0])], c["z"][1])
    return dict(
        wall=wall, wgt=bf(w_in[:, o[7]:o[8]].T),
        qn=q_norm[None], wuq=bf(wuq.reshape(MLA_Q_RANK, -1)),
        kvn=kv_norm[None], wk=bf(wk.reshape(MLA_KV_RANK, -1)), wv=bf(wv),
        cw=jnp.zeros((SUBLANES, conv_w.shape[1]), F32).at[:MLSTM_CONV].set(conv_w), cb=conv_b[None], gb=gbias)


def _tile_specs(nct, nt, nb):
    tile = lambda w: pl.BlockSpec((nb, TM, w), lambda b, t: (b, t, 0))
    per = TM // HALO
    nlat = nt - nct
    src_ctx = pl.BlockSpec((nb, TM, D_MODEL), lambda b, t: (b, jnp.minimum(t, nct - 1), 0))
    src_lat = pl.BlockSpec((nb, TM, D_MODEL), lambda b, t: (b, jnp.maximum(t - nct, 0), 0))
    prev = pl.BlockSpec((nb, HALO, D_MODEL), lambda b, t: (b, jnp.maximum((t - nct) * per - 1, 0), 0))
    nxt = pl.BlockSpec((nb, HALO, D_MODEL),
                       lambda b, t: (b, jnp.clip((t - nct + 1) * per, 0, nlat * per - 1), 0))
    mod = pl.BlockSpec((nb, None, 1, D_MODEL), lambda b, t: (b, (t >= nct).astype(jnp.int32), 0, 0))
    return tile, src_ctx, src_lat, prev, nxt, mod


def _per_element(tile_kernel, shared):
    def kernel(*refs, **kw):
        for bi in range(refs[0].shape[0]):
            tile_kernel(*[r if i in shared else r.at[bi] for i, r in enumerate(refs)], **kw)
    return kernel


def _proj0(ctx, x, scale, shift, norm_g, cos_t, sin_t, w, nct):
    bsz = x.shape[0]
    ttot = ctx.shape[1] + x.shape[1]
    nt = ttot // TM
    nb = MLSTM_HEADS * MLSTM_DH
    sb = math.gcd(bsz, TILE_BATCH)
    tile, src_ctx, src_lat, prev, nxt, mod = _tile_specs(nct, nt, sb)
    tab = pl.BlockSpec((TM, LANES), lambda b, t: (t, 0))
    wnames = ["wall", "wgt", "qn", "wuq", "kvn", "wk", "wv", "cw", "cb", "gb"]
    ws = [w[n] for n in wnames]
    tok = lambda width, dt: jax.ShapeDtypeStruct((bsz, ttot, width), dt)
    out_shape = (tok(MLA_HEADS * LANES, BF16), tok(MLA_HEADS * LANES, BF16), tok(MLA_HEADS * LANES, BF16),
                 tok(nb, BF16), tok(nb, BF16), tok(nb, BF16),
                 jax.ShapeDtypeStruct((bsz, ttot // MLSTM_CHUNK, 2 * MLSTM_HEADS, MLSTM_CHUNK), F32), tok(LANES, F32),
                 tok(nb, BF16), tok(D_MODEL, BF16))
    out_specs = (tile(MLA_HEADS * LANES), tile(MLA_HEADS * LANES), tile(MLA_HEADS * LANES),
                 tile(nb), tile(nb), tile(nb),
                 pl.BlockSpec((sb, TM // MLSTM_CHUNK, 2 * MLSTM_HEADS, MLSTM_CHUNK), lambda b, t: (b, t, 0, 0)),
                 tile(LANES),
                 tile(nb), tile(D_MODEL))
    n_blocked, n_out = 6, len(out_shape)
    n_in = n_blocked + 3 + len(ws)
    shared = set(range(n_blocked, n_in)) | {n_in + n_out}
    return pl.pallas_call(
        functools.partial(_per_element(_proj0_kernel, shared), nct=nct, nt=nt),
        out_shape=out_shape,
        grid=(bsz // sb, nt),
        in_specs=[src_ctx, src_lat, prev, nxt, mod, mod, _full((1, D_MODEL)), tab, tab] + [_full(a.shape) for a in ws],
        out_specs=out_specs,
        scratch_shapes=[pltpu.VMEM((TM + 2 * HALO, 2 * nb), F32)],
        compiler_params=_params("parallel", "parallel"),
        name="proj0",
    )(ctx, x, x, x, scale, shift, norm_g[None], cos_t, sin_t, *ws)


def _mla_kernel(qa_ref, qb_ref, k_hbm, v_hbm, o_ref, k_buf, v_buf, sem, *, nkeys_ctx):
    step = pl.program_id(2)
    b, p = pl.program_id(0), pl.program_id(1)
    ngroup = pl.num_programs(1)
    hw = k_buf.shape[2]

    def copies(bb, pp):
        slot = pp % 2
        cols = pl.ds(pl.multiple_of(pp * hw, LANES), hw)
        return (pltpu.make_async_copy(k_hbm.at[bb, :, cols], k_buf.at[slot], sem.at[0, slot]),
                pltpu.make_async_copy(v_hbm.at[bb, :, cols], v_buf.at[slot], sem.at[1, slot]))

    @pl.when((step == 0) & (b == 0) & (p == 0))
    def _():
        for c in copies(b, p):
            c.start()

    @pl.when(step == 0)
    def _():
        for c in copies(b, p):
            c.wait()
        last = (b == pl.num_programs(0) - 1) & (p == ngroup - 1)

        @pl.when(jnp.logical_not(last))
        def _():
            wrap = p == ngroup - 1
            for c in copies(jnp.where(wrap, b + 1, b), jnp.where(wrap, 0, p + 1)):
                c.start()

    k_ref = k_buf.at[p % 2]
    v_ref = v_buf.at[p % 2]

    def attend(queries, nkeys, rows):
        qk = lambda j: _dot_nt(queries(j), k_ref[0:nkeys, j * LANES:(j + 1) * LANES])
        outs = []
        s_next = qk(0)
        for j in range(MLA_HPS):
            s = s_next
            if j + 1 < MLA_HPS:
                s_next = qk(j + 1)
            p = jnp.exp2(s - jnp.max(s, axis=-1, keepdims=True))
            o = _dot(p.astype(BF16), v_ref[0:nkeys, j * LANES:(j + 1) * LANES])
            outs.append(o / pltpu.roll(o, MLA_V, axis=1))
        lane = lax.broadcasted_iota(jnp.int32, outs[0].shape, 1)
        for j in range(MLA_HPS // 2):
            o_ref[rows, j * LANES:(j + 1) * LANES] = jnp.where(lane < MLA_V, outs[2 * j], outs[2 * j + 1]).astype(BF16)

    @pl.when(step == 0)
    def _():
        attend(lambda j: qa_ref[:, j * LANES:(j + 1) * LANES], nkeys_ctx, slice(0, TM))
        o_ref[TM:, :] = jnp.zeros((o_ref.shape[0] - TM, o_ref.shape[1]), BF16)

    @pl.when(step > 0)
    def _():
        stacked = lambda j: jnp.concatenate([qa_ref[:, j * LANES:(j + 1) * LANES],
                                             qb_ref[:, j * LANES:(j + 1) * LANES]], axis=0)
        attend(stacked, k_ref.shape[0], slice(None))


def _mla_attention(q, k, v, nct):
    bsz, ttot, _ = q.shape
    nt = ttot // TM
    assert nct == 1 and MLA_QTILES == 2 and (nt - nct) % MLA_QTILES == 0 and (MLA_HEADS // MLA_HPS) % 2 == 0
    hw = MLA_HPS * LANES
    nsteps = 1 + (nt - nct) // MLA_QTILES
    return pl.pallas_call(
        functools.partial(_mla_kernel, nkeys_ctx=nct * TM),
        out_shape=jax.ShapeDtypeStruct((bsz, nsteps * MLA_QTILES * TM, MLA_HEADS * MLA_V), BF16),
        grid=(bsz, MLA_HEADS // MLA_HPS, nsteps),
        in_specs=[pl.BlockSpec((None, TM, hw), lambda b, p, s: (b, jnp.maximum(2 * s - 1, 0), p)),
                  pl.BlockSpec((None, TM, hw), lambda b, p, s: (b, jnp.maximum(2 * s, 1), p)),
                  pl.BlockSpec(memory_space=pl.ANY), pl.BlockSpec(memory_space=pl.ANY)],
        out_specs=pl.BlockSpec((None, MLA_QTILES * TM, hw // 2), lambda b, p, s: (b, s, p)),
        scratch_shapes=[pltpu.VMEM((2, ttot, hw), BF16), pltpu.VMEM((2, ttot, hw), BF16),
                        pltpu.SemaphoreType.DMA((2, 2))],
        compiler_params=_params("arbitrary", "arbitrary", "arbitrary"),
        name="mla_attention",
    )(q, q, k, v)


def _mla_tile(t, nct):
    return t + (t >= nct).astype(jnp.int32)


def _scan_block(j, nct, nt):
    return jnp.where(j < nct, nct - 1 - j, nt - 1 - (j - nct))


def _mlstm_kernel(qf_ref, kf_ref, vf_ref, grf_ref, gcf_ref, qr_ref, kr_ref, vr_ref, grr_ref, gcr_ref,
                  hf_ref, hr_ref, c_scr, m_scr):
    @pl.when(pl.program_id(1) == 0)
    def _():
        c_scr[...] = jnp.zeros_like(c_scr)
        m_scr[...] = jnp.zeros_like(m_scr)

    L = MLSTM_CHUNK
    nchunk = TM // L
    nh = MLSTM_HEADS
    nchain = 2 * nh
    row = lax.broadcasted_iota(jnp.int32, (L, L), 0)
    col = lax.broadcasted_iota(jnp.int32, (L, L), 1)
    ones = jnp.ones((L, MLSTM_DH), BF16)
    wide = lambda a: jnp.broadcast_to(a, (L, MLSTM_DH))
    twice = lambda a: jnp.concatenate([a, a], axis=1)
    refs = ((qf_ref, kf_ref, vf_ref, grf_ref, gcf_ref, hf_ref), (qr_ref, kr_ref, vr_ref, grr_ref, gcr_ref, hr_ref))
    nbatch = qf_ref.shape[0]

    def load(ci):
        cs = []
        for bi, d in [(bi, d) for bi in range(nbatch) for d in range(2)]:
            q_ref, k_ref, v_ref, gr_ref, gc_ref, o_ref = (r.at[bi] for r in refs[d])
            cd = ci if d == 0 else nchunk - 1 - ci
            rs = slice(cd * L, (cd + 1) * L)
            g_rows = gr_ref[cd]
            g_cols = gc_ref[rs, :]
            last = L - 1 if d == 0 else 0
            for hd in range(nh):
                ch = d * nh + hd
                hs = slice(hd * MLSTM_DH, (hd + 1) * MLSTM_DH)
                cs.append(dict(
                    rs=rs, hs=hs, o_ref=o_ref, causal=(col <= row) if d == 0 else (col >= row),
                    c_row=g_rows[ch:ch + 1], c_col=g_cols[:, ch:ch + 1],
                    pm_col=g_cols[:, nchain + ch:nchain + ch + 1], b_col=g_cols[:, 2 * nchain + ch:2 * nchain + ch + 1],
                    pm_last=g_cols[last:last + 1, nchain + ch:nchain + ch + 1],
                    b_last=g_cols[last:last + 1, 2 * nchain + ch:2 * nchain + ch + 1],
                    q=q_ref[rs, hs], k=k_ref[rs, hs],
                    v_ext=jnp.concatenate([v_ref[rs, hs], ones], axis=1)))
        return cs

    def prep_scores(cs):
        for c in cs:
            c["s"] = _dot_nt(c["q"], c["k"])

    def prep_weights(cs):
        for c in cs:
            c["pm_w"] = wide(c["pm_col"])
            c["b_w"] = wide(c["b_col"])
            c["w"] = jnp.exp2(jnp.where(c["causal"], c["c_row"] - c["pm_w"][:, :L], -jnp.inf))
            c["kwf"] = jnp.exp2(wide(c["c_col"] - c["pm_last"]))

    def prep_operands(cs):
        for c in cs:
            c["p"] = (c["s"] * c["w"]).astype(BF16)
            c["kw"] = (c["kwf"] * c["k"].astype(F32)).astype(BF16)

    def prep_products(cs):
        for c in cs:
            c["o1"] = _dot(c["p"], c["v_ext"])
            c["dc"] = _dot_tn(c["kw"], c["v_ext"])

    def scan_read(cs):
        for ch, c in enumerate(cs):
            c["qc"] = _dot(c["q"], state[ch].astype(BF16))

    def scan_factors(cs):
        for ch, c in enumerate(cs):
            g_w = jnp.maximum(m[ch], c["pm_w"])
            c["e_intra"] = jnp.exp2(c["pm_w"] - g_w)
            c["e_inter"] = jnp.exp2(m[ch] - g_w)
            c["floor"] = jnp.exp2(-(c["b_w"] + g_w))
            g_last = jnp.maximum(m[ch], c["pm_last"])
            c["keep"] = jnp.exp2(m[ch] - g_last)
            c["gain"] = jnp.exp2(c["pm_last"] - g_last)
            c["m_new"] = c["b_last"] + g_last

    def scan_update(cs):
        for ch, c in enumerate(cs):
            o = twice(c["e_intra"]) * c["o1"] + twice(c["e_inter"]) * c["qc"]
            h = o[:, :MLSTM_DH] / jnp.maximum(jnp.abs(o[:, MLSTM_DH:]), c["floor"])
            c["o_ref"][c["rs"], c["hs"]] = h.astype(BF16)
            state[ch] = c["keep"] * state[ch] + c["gain"] * c["dc"]
            m[ch] = c["m_new"]

    nstate = nbatch * nchain
    state = [c_scr[ch] for ch in range(nstate)]
    m = [m_scr[ch, 0:1, 0:1] for ch in range(nstate)]
    chunks = [load(ci) for ci in range(nchunk)]
    everything = [c for cs in chunks for c in cs]
    for stage in (prep_scores, prep_weights, prep_operands):
        stage(everything)
    prep_products(chunks[0])
    for i, cs in enumerate(chunks):
        scan_read(cs)
        if i + 1 < nchunk:
            prep_products(chunks[i + 1])
        scan_factors(cs)
        scan_update(cs)
    for ch in range(nstate):
        c_scr[ch] = state[ch]
        m_scr[ch] = jnp.broadcast_to(m[ch], m_scr.shape[1:])


def _mlstm(mq, mk, mv, g_rows, g_cols, nct):
    bsz, ttot, nb = mq.shape
    nt = ttot // TM
    per = TM // MLSTM_CHUNK
    sb = math.gcd(bsz, MLSTM_BATCH)
    fwd = lambda w: pl.BlockSpec((sb, TM, w), lambda b, j: (b, j, 0))
    rev = lambda w: pl.BlockSpec((sb, TM, w), lambda b, j: (b, _scan_block(j, nct, nt), 0))
    gshape = (sb, per, 2 * MLSTM_HEADS, MLSTM_CHUNK)
    gfwd = pl.BlockSpec(gshape, lambda b, j: (b, j, 0, 0))
    grev = pl.BlockSpec(gshape, lambda b, j: (b, _scan_block(j, nct, nt), 0, 0))
    nstate = sb * 2 * MLSTM_HEADS
    return pl.pallas_call(
        _mlstm_kernel,
        out_shape=(jax.ShapeDtypeStruct((bsz, ttot, nb), BF16),) * 2,
        grid=(bsz // sb, nt),
        in_specs=[fwd(nb), fwd(nb), fwd(nb), gfwd, fwd(LANES), rev(nb), rev(nb), rev(nb), grev, rev(LANES)],
        out_specs=(fwd(nb), rev(nb)),
        scratch_shapes=[pltpu.VMEM((nstate, MLSTM_DH, 2 * MLSTM_DH), F32),
                        pltpu.VMEM((nstate, SUBLANES, LANES), F32)],
        compiler_params=_params("parallel", "arbitrary"),
        name="mlstm_scan",
    )(mq, mk, mv, g_rows, g_cols, mq, mk, mv, g_rows, g_cols)


def _head_rms(x, n_heads):
    w = x.shape[1] // n_heads
    return jnp.concatenate([_rms(x[:, i * w:(i + 1) * w]) for i in range(n_heads)], axis=1)


def _mid_kernel(ctx_ref, lat_ref, gate_ref, a_ref, hf_ref, hr_ref, mo_ref, z0_ref, hn_ref, wout_ref,
                sc_ref, sh_ref, g_ref, wall_ref, wgate_ref, bgate_ref,
                h_out, gq_out, gk_out, gv_out, bc_out, nq_out, nk_out, nv_out, z_out, *, nct):
    nbt = ctx_ref.shape[0]
    rows = lambda f: jnp.concatenate([f(bi) for bi in range(nbt)], axis=0)
    f32 = lambda r, bi: r[bi].astype(F32)

    def put(ref, val):
        for bi in range(nbt):
            ref[bi] = val[bi * TM:(bi + 1) * TM]

    hm = rows(lambda bi: _head_rms(f32(mo_ref, bi) * (f32(hf_ref, bi) + f32(hr_ref, bi)), MLSTM_HEADS) * hn_ref[...])
    cat = (jnp.concatenate([rows(lambda bi: f32(a_ref, bi)), hm], axis=1) * rows(lambda bi: f32(z0_ref, bi))).astype(BF16)
    y = _dot(cat, wout_ref[...])
    is_ctx = pl.program_id(1) < nct
    hs = [jnp.where(is_ctx, ctx_ref[bi], lat_ref[bi]) + gate_ref[bi] * y[bi * TM:(bi + 1) * TM] for bi in range(nbt)]
    for bi in range(nbt):
        h_out[bi] = hs[bi]

    ub = rows(lambda bi: _modnorm(hs[bi], g_ref[...], sc_ref[bi], sh_ref[bi])).astype(BF16)
    wseg = lambda name: wall_ref[:, PROJ1_COLS[name][0]:PROJ1_COLS[name][1]]
    ga = _dot(ub, wseg("ga")).astype(BF16)
    gate_pre = _dot(ga, wgate_ref[...])
    put(gq_out, (_dot(ub, wseg("q")) * GLA_DK ** -0.5).astype(BF16))
    put(gk_out, _dot(ub, wseg("k")).astype(BF16))
    put(gv_out, _dot(ub, wseg("v")).astype(BF16))

    lg = _log_sigmoid(gate_pre + bgate_ref[...]) * (1.0 / GLA_TAU)
    hi, mid, lo = _split3(lg)
    put(nq_out, (_dot(ub, wseg("nq")) * (NA_DH ** -0.5 * LOG2E)).astype(BF16))
    put(nk_out, _dot(ub, wseg("nk")).astype(BF16))
    put(nv_out, _dot(ub, wseg("nv")).astype(BF16))
    row = lax.broadcasted_iota(jnp.int32, (TM, TM), 0)
    col = lax.broadcasted_iota(jnp.int32, (TM, TM), 1)
    same = (row // CHUNK) == (col // CHUNK)
    nk = GLA_HEADS * GLA_DK
    for d in range(2):
        tri = jnp.where(jnp.logical_and(same, (col <= row) if d == 0 else (col >= row)), 1.0, 0.0).astype(BF16)
        cols = slice(d * nk, (d + 1) * nk)
        for bi in range(nbt):
            rs = slice(bi * TM, (bi + 1) * TM)
            bc_out[bi, :, cols] = (_dot(tri, hi[rs, cols]) + _dot(tri, mid[rs, cols]) + _dot(tri, lo[rs, cols])) * LOG2E
    put(z_out, _silu(_dot(ub, wseg("z"))).astype(BF16))


def _proj1_weights(w_in, w_gate, b_gate):
    nk = GLA_HEADS * GLA_DK
    nc = GLA_HEADS * GLA_DV
    nd = NA_HEADS * NA_DH
    o = np.cumsum([0, nk, nk, nc, 2 * GLA_GATE_RANK, nd, nd, nd, nc + nd])
    r = GLA_GATE_RANK
    c = PROJ1_COLS
    wall = _relayout(w_in.T, [(0, o[3], 0), (o[3], 2 * r, c["ga"][0]), (o[4], o[8] - o[4], c["nq"][0])], c["z"][1])
    wgate = jnp.zeros((LANES, 2 * nk), F32).at[:r, :nk].set(w_gate[0]).at[r:2 * r, nk:].set(w_gate[1])
    return dict(wall=wall, wgate=wgate.astype(BF16), bgate=b_gate.reshape(1, 2 * nk))


def _mid(ctx, x, gate0, a, hf, hr, mo, z0, h_norm, w_out, scale, shift, norm_g, w, nct):
    bsz = x.shape[0]
    ttot = ctx.shape[1] + x.shape[1]
    nt = ttot // TM
    sb = math.gcd(bsz, TILE_BATCH)
    tile, src_ctx, src_lat, _, _, mod = _tile_specs(nct, nt, sb)
    nb = MLSTM_HEADS * MLSTM_DH
    nk = GLA_HEADS * GLA_DK
    nc = GLA_HEADS * GLA_DV
    nd = NA_HEADS * NA_DH
    ws = [w[n] for n in ("wall", "wgate", "bgate")]
    tok = lambda width, dt: jax.ShapeDtypeStruct((bsz, ttot, width), dt)
    return pl.pallas_call(
        functools.partial(_mid_kernel, nct=nct),
        out_shape=(tok(D_MODEL, F32), tok(nk, BF16), tok(nk, BF16), tok(nc, BF16), tok(2 * nk, F32),
                   tok(nd, BF16), tok(nd, BF16), tok(nd, BF16), tok(nc + nd, BF16)),
        grid=(bsz // sb, nt),
        in_specs=[src_ctx, src_lat, mod,
                  pl.BlockSpec((sb, TM, MLA_HEADS * MLA_V), lambda b, t: (b, _mla_tile(t, nct), 0)),
                  tile(nb), tile(nb), tile(nb), tile(D_MODEL),
                  _full((1, nb)), _full(w_out.shape), mod, mod, _full((1, D_MODEL))] + [_full(a.shape) for a in ws],
        out_specs=(tile(D_MODEL), tile(nk), tile(nk), tile(nc), tile(2 * nk), tile(nd), tile(nd), tile(nd),
                   tile(nc + nd)),
        compiler_params=_params("parallel", "parallel"),
        name="out0_proj1",
    )(ctx, x, gate0, a, hf, hr, mo, z0, h_norm[None], w_out.astype(BF16), scale, shift, norm_g[None], *ws)


def _gla_kernel(qf_ref, kf_ref, vf_ref, bf_ref, qr_ref, kr_ref, vr_ref, br_ref, of_ref, or_ref, s_scr):
    @pl.when(pl.program_id(1) == 0)
    def _():
        s_scr[...] = jnp.zeros_like(s_scr)

    L = CHUNK
    nchunk = TM // L
    npair = GLA_HEADS // 2
    lane = lax.broadcasted_iota(jnp.int32, (L, LANES), 1)
    rowv = lax.broadcasted_iota(jnp.int32, (L, LANES), 0)
    srow = lax.broadcasted_iota(jnp.int32, (LANES, 2 * GLA_DV), 0)
    scol = lax.broadcasted_iota(jnp.int32, (LANES, 2 * GLA_DV), 1)
    own_block = (srow < GLA_DK) == (scol < GLA_DV)
    vcol = lax.broadcasted_iota(jnp.int32, (L, 2 * GLA_DV), 1)
    e_row = lax.broadcasted_iota(jnp.int32, (LANES, LANES), 0)
    e_col = lax.broadcasted_iota(jnp.int32, (LANES, LANES), 1)
    head_ones = jnp.where((e_row < GLA_DK) == (e_col < GLA_DK), 1.0, 0.0).astype(BF16)
    leaf_idx = lax.broadcasted_iota(jnp.int32, (GLA_LEAF * GLA_LEAF, LANES), 0)
    leaf_lane = lax.broadcasted_iota(jnp.int32, (GLA_LEAF * GLA_LEAF, LANES), 1) % L
    leaf_s, leaf_t = leaf_idx // GLA_LEAF, leaf_idx % GLA_LEAF
    rep = lambda a: jnp.concatenate(
        [jnp.broadcast_to(a[s:s + 1], (GLA_LEAF, a.shape[1])) for s in range(GLA_LEAF)], axis=0)
    til = lambda a: jnp.concatenate([a] * GLA_LEAF, axis=0)

    def chunk_step(ci, carry):
        cs = []
        for bi, d in [(bi, d) for bi in range(qf_ref.shape[0]) for d in range(2)]:
            q_ref, k_ref, v_ref, b_ref, o_ref = (
                r.at[bi] for r in ((qf_ref, kf_ref, vf_ref, bf_ref, of_ref), (qr_ref, kr_ref, vr_ref, br_ref, or_ref))[d])
            cd = ci if d == 0 else nchunk - 1 - ci
            base = pl.multiple_of(cd * L, L)
            for p in range(npair):
                ls = slice(p * LANES, (p + 1) * LANES)
                bc = b_ref[pl.ds(base, L), ls]
                cs.append(dict(
                    d=d, p=p, bi=bi, base=base, o_ref=o_ref, bc=bc,
                    tau_v=rowv if d == 0 else L - 1 - rowv,
                    tau_s=(lane % L) if d == 0 else L - 1 - (lane % L),
                    q=q_ref[pl.ds(base, L), ls].astype(F32),
                    k=k_ref[pl.ds(base, L), ls].astype(F32),
                    v=v_ref[pl.ds(base, L), p * 2 * GLA_DV:(p + 1) * 2 * GLA_DV],
                    b_last=bc[L - 1:L] if d == 0 else bc[0:1],
                    state=s_scr[bi, d, p]))

        for c in cs:
            c["o_pair"] = _dot((c["q"] * jnp.exp2(c["bc"])).astype(BF16), c["state"].astype(BF16))

        for c in cs:
            c["att"] = jnp.zeros((L, LANES), F32)
        bs = L // 2
        while bs >= GLA_LEAF:
            for c in cs:
                d, bc, tau_v = c["d"], c["bc"], c["tau_v"]
                later = (tau_v // bs) % 2 == 1
                ref_b = None
                for blk in range(L // (2 * bs)):
                    tau_ref = blk * 2 * bs + bs - 1
                    idx = tau_ref if d == 0 else L - 1 - tau_ref
                    r = jnp.broadcast_to(bc[idx:idx + 1], (L, LANES))
                    ref_b = r if ref_b is None else jnp.where(tau_v // (2 * bs) == blk, r, ref_b)
                c["qs"] = jnp.where(later, c["q"] * jnp.exp2(jnp.where(later, bc - ref_b, 0.0)), 0.0).astype(BF16)
                ks = jnp.where(later, 0.0, c["k"] * jnp.exp2(jnp.where(later, 0.0, ref_b - bc)))
                c["ks2"] = jnp.concatenate([jnp.where(lane < GLA_DK, ks, 0.0), jnp.where(lane < GLA_DK, 0.0, ks)],
                                           axis=0).astype(BF16)
            for c in cs:
                same_parent = (c["tau_v"] // (2 * bs)) == (c["tau_s"] // (2 * bs))
                c["att"] = c["att"] + jnp.where(same_parent, _dot_nt(c["qs"], c["ks2"]), 0.0)
            bs //= 2

        for c in cs:
            c["leaf"] = []
        for blk in range(L // GLA_LEAF):
            rs = slice(blk * GLA_LEAF, (blk + 1) * GLA_LEAF)
            for c in cs:
                qb, bb, kb = c["q"][rs], c["bc"][rs], c["k"][rs]
                ok = (leaf_t >= leaf_s) if c["d"] == 0 else (leaf_t <= leaf_s)
                w = jnp.where(ok, til(qb) * jnp.exp2(til(bb) - rep(bb)) * rep(kb), 0.0)
                c["a"] = _dot(w.astype(BF16), head_ones)
            for c in cs:
                a = jnp.where(leaf_lane == leaf_s + blk * GLA_LEAF, c["a"], 0.0)
                acc = a[0:GLA_LEAF]
                for s in range(1, GLA_LEAF):
                    acc = acc + a[s * GLA_LEAF:(s + 1) * GLA_LEAF]
                c["leaf"].append(acc)

        for c in cs:
            v = c["v"]
            c["att"] = (c["att"] + jnp.concatenate(c["leaf"], axis=0)).astype(BF16)
            c["v_blk"] = jnp.concatenate([jnp.where(vcol < GLA_DV, v, jnp.zeros_like(v)),
                                          jnp.where(vcol < GLA_DV, jnp.zeros_like(v), v)], axis=0)
            c["ke"] = (c["k"] * jnp.exp2(c["b_last"] - c["bc"])).astype(BF16)
            c["decay_col"] = jnp.sum(
                jnp.where(e_row == e_col, jnp.broadcast_to(jnp.exp2(c["b_last"]), (LANES, LANES)), 0.0),
                axis=1, keepdims=True)
        for c in cs:
            c["o"] = c["o_pair"] + _dot(c["att"], c["v_blk"])
            c["ds"] = _dot_tn(c["ke"], c["v"])
        for c in cs:
            p = c["p"]
            c["o_ref"][pl.ds(c["base"], L), p * 2 * GLA_DV:(p + 1) * 2 * GLA_DV] = c["o"].astype(BF16)
            s_scr[c["bi"], c["d"], p] = c["decay_col"] * c["state"] + jnp.where(own_block, c["ds"], 0.0)
        return carry

    lax.fori_loop(0, nchunk, chunk_step, 0)


def _gla(gq, gk, gv, bc, nct):
    bsz, ttot, nk = gq.shape
    nc = gv.shape[2]
    nt = ttot // TM
    sb = math.gcd(bsz, GLA_BATCH)
    fwd = lambda w: pl.BlockSpec((sb, TM, w), lambda b, j: (b, j, 0))
    rev = lambda w: pl.BlockSpec((sb, TM, w), lambda b, j: (b, _scan_block(j, nct, nt), 0))
    bfwd = pl.BlockSpec((sb, TM, nk), lambda b, j: (b, j, 0))
    brev = pl.BlockSpec((sb, TM, nk), lambda b, j: (b, _scan_block(j, nct, nt), 1))
    return pl.pallas_call(
        _gla_kernel,
        out_shape=(jax.ShapeDtypeStruct((bsz, ttot, nc), BF16),) * 2,
        grid=(bsz // sb, nt),
        in_specs=[fwd(nk), fwd(nk), fwd(nc), bfwd, rev(nk), rev(nk), rev(nc), brev],
        out_specs=(fwd(nc), rev(nc)),
        scratch_shapes=[pltpu.VMEM((sb, 2, GLA_HEADS // 2, LANES, 2 * GLA_DV), F32)],
        compiler_params=_params("parallel", "arbitrary"),
        name="gla_scan",
    )(gq, gk, gv, bc, gq, gk, gv, bc)


def _na_bias_tables(rpb, rows):
    kh = NA_KH
    pad = GRID_W - NA_KW
    rpb_pad = jnp.pad(rpb.astype(F32) * LOG2E, ((0, 0), (0, 0), (pad, pad + 1)))
    assert rpb_pad.shape[2] == LANES
    plan = []
    for r0 in (0, NA_QROWS, rows - NA_QROWS):
        kb = int(np.clip(r0 - kh // 2, 0, rows - NA_KROWS))
        for qrow in range(r0, r0 + NA_QROWS):
            ws = int(np.clip(qrow - kh // 2, 0, rows - kh))
            plan.append(tuple(krow - qrow + NA_KH - 1 if ws <= krow < ws + kh else None
                              for krow in range(kb, kb + NA_KROWS)))
    return pl.pallas_call(
        functools.partial(_na_table_kernel, plan=tuple(plan)),
        out_shape=jax.ShapeDtypeStruct((NA_HEADS, len(plan) * GRID_W, NA_KROWS * GRID_W), F32),
        grid=(NA_HEADS,),
        in_specs=[pl.BlockSpec((None,) + rpb_pad.shape[1:], lambda h: (h, 0, 0))],
        out_specs=pl.BlockSpec((None, len(plan) * GRID_W, NA_KROWS * GRID_W), lambda h: (h, 0, 0)),
        compiler_params=_params("parallel"),
        name="na_bias_table",
    )(rpb_pad)


def _na_table_kernel(rpb_ref, o_ref, *, plan):
    assert 2 * GRID_W == LANES
    lane = lax.broadcasted_iota(jnp.int32, (GRID_W, LANES), 1)
    masked = jnp.full((GRID_W, LANES), NEG_BIG, F32)
    col = lax.broadcasted_iota(jnp.int32, (GRID_W, LANES), 0)
    first = jnp.clip(col - NA_KW // 2, 0, GRID_W - NA_KW)
    col_ok = (lane >= first) & (lane < first + NA_KW)
    blocks = {}

    def block(idx):
        if idx is None:
            return masked
        if idx not in blocks:
            row = jnp.broadcast_to(rpb_ref[idx:idx + 1, :], (GRID_W, LANES))
            shifted = pltpu.roll(row, LANES - (GRID_W - 1), axis=1, stride=1, stride_axis=0)
            blocks[idx] = jnp.where(col_ok, shifted, NEG_BIG)
        return blocks[idx]

    for i, entries in enumerate(plan):
        rows = slice(i * GRID_W, (i + 1) * GRID_W)
        for k in range(0, len(entries) - 1, 2):
            pair = jnp.where(lane < GRID_W, block(entries[k]), pltpu.roll(block(entries[k + 1]), GRID_W, axis=1))
            o_ref[rows, k * GRID_W:(k + 2) * GRID_W] = pair
        if len(entries) % 2:
            k = len(entries) - 1
            o_ref[rows, k * GRID_W:(k + 1) * GRID_W] = block(entries[k])[:, :GRID_W]


def _na_heads(q_ref, kw_ref, vw_ref, kc_ref, vc_ref, bias_ref):
    nq = NA_QROWS * GRID_W
    lane = lax.broadcasted_iota(jnp.int32, (nq, LANES), 1)
    def scores(j):
        ls = slice((j // 2) * LANES, (j // 2 + 1) * LANES)
        q = q_ref[:, ls]
        qj = jnp.where((lane < NA_DH) == (j % 2 == 0), q, jnp.zeros_like(q))
        return _dot_nt(qj, kw_ref[:, ls]) + bias_ref[j], _dot_nt(qj, kc_ref[:, ls])

    outs = []
    nxt = scores(0)
    for j in range(NA_HEADS):
        s_loc, s_ctx = nxt
        if j + 1 < NA_HEADS:
            nxt = scores(j + 1)
        ls = slice((j // 2) * LANES, (j // 2 + 1) * LANES)
        m = jnp.maximum(jnp.max(s_loc, axis=-1, keepdims=True), jnp.max(s_ctx, axis=-1, keepdims=True))
        p_loc = jnp.exp2(s_loc - m)
        p_ctx = jnp.exp2(s_ctx - m)
        den = jnp.sum(p_loc, axis=-1, keepdims=True) + jnp.sum(p_ctx, axis=-1, keepdims=True)
        outs.append((_dot(p_loc.astype(BF16), vw_ref[:, ls]) + _dot(p_ctx.astype(BF16), vc_ref[:, ls])) / den)
    return jnp.concatenate([jnp.where(lane < NA_DH, outs[2 * i], outs[2 * i + 1]) for i in range(NA_HEADS // 2)], axis=1)


def _na_out1_kernel(*refs, nb):
    kv = refs[:4 * nb]
    q_ref, bias_ref, h_ref, gate_ref, of_ref, or_ref, z_ref, gn_ref, w_ref, fn_ref, o_ref = refs[4 * nb:]
    rows = lambda f: jnp.concatenate([f(bi) for bi in range(nb)], axis=0)
    f32 = lambda r, bi: r[bi].astype(F32)
    na = rows(lambda bi: _na_heads(q_ref.at[bi], *kv[4 * bi:4 * bi + 4], bias_ref))
    g = rows(lambda bi: _head_rms(f32(of_ref, bi) + f32(or_ref, bi), GLA_HEADS) * gn_ref[...])
    cat = (jnp.concatenate([g, na], axis=1) * rows(lambda bi: f32(z_ref, bi))).astype(BF16)
    y = _dot(cat, w_ref[...])
    for bi in range(nb):
        h = h_ref[bi] + gate_ref[bi] * y[bi * TM:(bi + 1) * TM]
        o_ref[bi] = _rms(h) * fn_ref[...]


def _na_out1(nq, nk, nv, bias, h, gate, of, orv, z, gla_norm, w_out, final_norm, n_ctx, rows):
    bsz, ttot, nd = nq.shape
    nrb = rows // NA_QROWS
    qrows = NA_QROWS * GRID_W
    krows = NA_KROWS * GRID_W
    assert qrows == TM
    nct = n_ctx // TM
    nc = GLA_HEADS * GLA_DV
    sb = math.gcd(bsz, TILE_BATCH)

    def variant(rb):
        return jnp.where(rb == 0, 0, jnp.where(rb == nrb - 1, 2, 1))

    def key_start(rb):
        first_row = jnp.clip(rb * NA_QROWS - NA_KH // 2, 0, rows - NA_KROWS)
        return pl.multiple_of(n_ctx + first_row * GRID_W, GRID_W)

    window = lambda i: pl.BlockSpec((None, pl.Element(krows), pl.Element(nd)), lambda b, rb: (b * sb + i, key_start(rb), 0))
    context = lambda i: pl.BlockSpec((None, pl.Element(n_ctx), pl.Element(nd)), lambda b, rb: (b * sb + i, 0, 0))
    kv_specs = [spec(i) for i in range(sb) for spec in (window, window, context, context)]
    lat = lambda w: pl.BlockSpec((sb, TM, w), lambda b, rb: (b, rb + nct, 0))
    return pl.pallas_call(
        functools.partial(_na_out1_kernel, nb=sb),
        out_shape=jax.ShapeDtypeStruct((bsz, rows * GRID_W, D_MODEL), F32),
        grid=(bsz // sb, nrb),
        in_specs=kv_specs + [
            lat(nd), pl.BlockSpec((NA_HEADS, qrows, krows), lambda b, rb: (0, variant(rb), 0)),
            lat(D_MODEL), pl.BlockSpec((sb, None, 1, D_MODEL), lambda b, rb: (b, 1, 0, 0)),
            lat(nc), lat(nc), lat(nc + nd), _full((1, nc)), _full(w_out.shape), _full((1, D_MODEL))],
        out_specs=pl.BlockSpec((sb, TM, D_MODEL), lambda b, rb: (b, rb, 0)),
        compiler_params=_params("parallel", "parallel"),
        name="na_out1",
    )(*([nk, nv, nk, nv] * sb), nq, bias, h, gate, of, orv, z, gla_norm[None], w_out.astype(BF16), final_norm[None])


def kernel(x, c, ctx, c_ctx, l0_norm, l0_w_mod, l0_b_mod, l0_w_in, l0_mla_q_norm, l0_mla_w_uq, l0_mla_kv_norm, l0_mla_w_ukv, l0_mlstm_conv_w, l0_mlstm_conv_b, l0_mlstm_b_i, l0_mlstm_b_f, l0_mlstm_norm, l0_w_out, l1_norm, l1_w_mod, l1_b_mod, l1_w_in, l1_gla_w_gate, l1_gla_b_gate, l1_gla_norm, l1_na_rpb, l1_w_out, final_norm):
    bsz, seq, d = x.shape
    n_ctx = ctx.shape[1]
    rows = seq // GRID_W
    assert d == D_MODEL and seq % TM == 0 and n_ctx == TM
    assert rows % NA_QROWS == 0 and rows >= NA_KROWS and rows // NA_QROWS >= 3
    nct = n_ctx // TM

    (shift0, scale0, gate0), (shift, scale, gate) = _mod_vectors(
        c, c_ctx, ((l0_w_mod, l0_b_mod), (l1_w_mod, l1_b_mod)))

    cos_t, sin_t = _rope_tables(n_ctx, seq)
    w0 = _proj0_weights(l0_w_in, l0_mla_q_norm, l0_mla_w_uq, l0_mla_kv_norm, l0_mla_w_ukv,
                        l0_mlstm_conv_w, l0_mlstm_conv_b, l0_mlstm_b_i, l0_mlstm_b_f)
    q, k, v, mq, mk, mv, g_rows, g_cols, mo, z0 = _proj0(ctx, x, scale0, shift0, l0_norm, cos_t, sin_t, w0, nct)
    a = _mla_attention(q, k, v, nct)
    hf, hr = _mlstm(mq, mk, mv, g_rows, g_cols, nct)

    w1 = _proj1_weights(l1_w_in, l1_gla_w_gate, l1_gla_b_gate)
    h, gq, gk, gv, bc, nq, nk, nv, z = _mid(ctx, x, gate0, a, hf, hr, mo, z0, l0_mlstm_norm, l0_w_out,
                                            scale, shift, l1_norm, w1, nct)
    of, orv = _gla(gq, gk, gv, bc, nct)
    return _na_out1(nq, nk, nv, _na_bias_tables(l1_na_rpb, rows), h, gate, of, orv, z,
                    l1_gla_norm, l1_w_out, final_norm, n_ctx, rows)
```
